```python
import jax, jax.numpy as jnp
from jax import lax
import numpy as np

D_MODEL = 1024
BATCH = 8
SEQ = 4096
DEPTH = 1

N_META = 16
ATTN_BLOCK = 128
META_PAD = ATTN_BLOCK - N_META
PREFIX = ATTN_BLOCK
FOX_HEADS = 8
FOX_HEAD_DIM = 64
FOX_WIDTH = FOX_HEADS * FOX_HEAD_DIM
DN_HEADS = 4
DN_HEAD_DIM = 128
DN_WIDTH = DN_HEADS * DN_HEAD_DIM
DN_CHUNK = 64
CONV_WIDTH = 4
D_FF = -(-8 * D_MODEL // (3 * 256)) * 256
EPS = 1e-6
NEG_INF = -1e30
IN_SPLITS = (FOX_WIDTH, FOX_WIDTH, FOX_WIDTH, FOX_HEADS,
             DN_WIDTH, DN_WIDTH, DN_WIDTH, DN_HEADS, DN_HEADS,
             DN_WIDTH,
             D_MODEL, D_MODEL)
D_IN = sum(IN_SPLITS)

kernel_name = "fox_gdn_gated_hybrid_block"


def rmsnorm(x, w):
    xf = x.astype(jnp.float32)
    y = xf * lax.rsqrt(jnp.mean(xf * xf, axis=-1, keepdims=True) + EPS)
    return (y * w.astype(jnp.float32)).astype(x.dtype)


def l2norm(x):
    return x * lax.rsqrt(jnp.sum(x * x, axis=-1, keepdims=True) + EPS)


def split_columns(t, sizes):
    out, start = [], 0
    for s in sizes:
        out.append(t[..., start:start + s])
        start += s
    return out


def causal_short_conv(u, w):
    K = w.shape[0]
    L = u.shape[1]
    up = jnp.pad(u, ((0, 0), (K - 1, 0), (0, 0)))
    return sum(up[:, i:i + L, :] * w[i] for i in range(K))


def fox_attention(q, k, v, log_f, valid):
    B, L, H, dh = q.shape
    nb = L // ATTN_BLOCK
    qf = q.astype(jnp.float32) * (dh ** -0.5)
    kf = k.astype(jnp.float32)
    vf = v.astype(jnp.float32)
    c = jnp.cumsum(log_f, axis=1)
    c_keys = c.transpose(0, 2, 1)
    kpos = jnp.arange(L)
    q_blocks = qf.reshape(B, nb, ATTN_BLOCK, H, dh).transpose(1, 0, 2, 3, 4)
    c_blocks = c.reshape(B, nb, ATTN_BLOCK, H).transpose(1, 0, 3, 2)

    def one_block(args):
        qb, cb, bi = args
        qpos = bi * ATTN_BLOCK + jnp.arange(ATTN_BLOCK)
        s = jnp.einsum('bqhd,bkhd->bhqk', qb, kf)
        s = s + cb[..., :, None] - c_keys[:, :, None, :]
        mask = (kpos[None, :] <= qpos[:, None]) & valid[None, :]
        s = jnp.where(mask, s, NEG_INF)
        p = jax.nn.softmax(s, axis=-1)
        return jnp.einsum('bhqk,bkhd->bqhd', p, vf)

    o = lax.map(one_block, (q_blocks, c_blocks, jnp.arange(nb)))
    return o.transpose(1, 0, 2, 3, 4).reshape(B, L, H * dh)


def gated_delta_rule(q, k, v, g, beta):
    B, H, L, dk = q.shape
    dv = v.shape[-1]
    C = DN_CHUNK
    N = L // C
    q = q * (dk ** -0.5)
    q, k, v = (t.reshape(B, H, N, C, t.shape[-1]) for t in (q, k, v))
    g, beta = (t.reshape(B, H, N, C) for t in (g, beta))
    gc = jnp.cumsum(g, axis=-1)
    tril = jnp.tril(jnp.ones((C, C), dtype=bool))
    strict = jnp.tril(jnp.ones((C, C), dtype=bool), -1)
    decay = jnp.exp(jnp.where(tril, gc[..., :, None] - gc[..., None, :], -jnp.inf))
    kb = k * beta[..., None]
    low = jnp.where(strict, jnp.einsum('bhncd,bhnsd->bhncs', kb, k) * decay, 0.0)
    a = low + jnp.eye(C, dtype=jnp.float32)
    rhs = jnp.concatenate([v * beta[..., None], kb * jnp.exp(gc)[..., None]], axis=-1)
    sol = lax.linalg.triangular_solve(a, rhs, left_side=True, lower=True, unit_diagonal=True)
    u, w = sol[..., :dv], sol[..., dv:]
    attn = jnp.einsum('bhncd,bhnsd->bhncs', q, k) * decay
    q_dec = q * jnp.exp(gc)[..., None]
    k_dec = k * jnp.exp(gc[..., -1:] - gc)[..., None]
    chunk_decay = jnp.exp(gc[..., -1])
    xs = tuple(jnp.moveaxis(t, 2, 0) for t in (q_dec, k_dec, u, w, attn, chunk_decay))

    def step(S, inp):
        qd, kd, uc, wc, ac, cd = inp
        v_new = uc - jnp.einsum('bhcd,bhde->bhce', wc, S)
        o = jnp.einsum('bhcd,bhde->bhce', qd, S) + jnp.einsum('bhcs,bhse->bhce', ac, v_new)
        S = S * cd[..., None, None] + jnp.einsum('bhcd,bhce->bhde', kd, v_new)
        return S, o

    S0 = jnp.zeros((B, H, dk, dv), jnp.float32)
    _, o = lax.scan(step, S0, xs)
    return jnp.moveaxis(o, 0, 2).reshape(B, H, L, dv)


def hybrid_mixer(h, valid, w_in, fox_forget_bias, dn_conv_w, dn_a_log, dn_dt_bias,
                 dn_out_norm_w, w_branch_fox, w_branch_dn, w_out):
    B, L, _ = h.shape
    proj = h @ w_in
    (fq, fk, fv, f_logit, dq, dk_, dv_, b_logit, a_logit, dz, ga, gb) = split_columns(proj, IN_SPLITS)
    vm = valid[None, :, None].astype(h.dtype)

    log_f = jax.nn.log_sigmoid(f_logit.astype(jnp.float32) + fox_forget_bias.astype(jnp.float32))
    rs_f = lambda t: t.reshape(B, L, FOX_HEADS, FOX_HEAD_DIM)
    o_fox = fox_attention(rs_f(fq), rs_f(fk), rs_f(fv), log_f, valid).astype(h.dtype)

    qkv = jax.nn.silu(causal_short_conv(jnp.concatenate([dq, dk_, dv_], axis=-1) * vm, dn_conv_w))
    qkv = qkv.astype(jnp.float32)
    rs_d = lambda t: t.reshape(B, L, DN_HEADS, DN_HEAD_DIM).transpose(0, 2, 1, 3)
    q_d = l2norm(rs_d(qkv[..., :DN_WIDTH]))
    k_d = l2norm(rs_d(qkv[..., DN_WIDTH:2 * DN_WIDTH]))
    v_d = rs_d(qkv[..., 2 * DN_WIDTH:])
    vmf = valid[None, None, :].astype(jnp.float32)
    beta = jax.nn.sigmoid(b_logit.astype(jnp.float32)).transpose(0, 2, 1) * vmf
    g = (-jnp.exp(dn_a_log.astype(jnp.float32))[None, :, None]
         * jax.nn.softplus(a_logit.astype(jnp.float32) + dn_dt_bias.astype(jnp.float32)).transpose(0, 2, 1)) * vmf
    o_dn = gated_delta_rule(q_d, k_d, v_d, g, beta).transpose(0, 2, 1, 3)
    z = dz.reshape(B, L, DN_HEADS, DN_HEAD_DIM)
    o_dn = (rmsnorm(o_dn, dn_out_norm_w) * jax.nn.silu(z.astype(jnp.float32))).reshape(B, L, DN_WIDTH)
    o_dn = o_dn.astype(h.dtype)

    y = jax.nn.sigmoid(ga) * (o_fox @ w_branch_fox) + jax.nn.sigmoid(gb) * (o_dn @ w_branch_dn)
    return y @ w_out


def swiglu(h, w_gate, w_up, w_down):
    return (jax.nn.silu(h @ w_gate) * (h @ w_up)) @ w_down


def _fwd_setup_inputs(seed: int = 0) -> dict:
    key = jax.random.key(seed)
    ks = jax.random.split(key, 20)
    f32 = jnp.float32
    nrm = lambda k, shape, s: jax.random.normal(k, shape, f32) * s
    dt = jnp.exp(jax.random.uniform(ks[7], (DEPTH, DN_HEADS), f32)
                 * (np.log(0.1) - np.log(0.001)) + np.log(0.001))
    return {
        "x": nrm(ks[0], (BATCH, SEQ, D_MODEL), 1.0),
        "meta_tokens": nrm(ks[1], (N_META, D_MODEL), 1.0),
        "mix_norm_w": 1.0 + nrm(ks[2], (DEPTH, D_MODEL), 0.02),
        "w_in": nrm(ks[3], (DEPTH, D_MODEL, D_IN), D_MODEL ** -0.5),
        "fox_forget_bias": 3.0 + nrm(ks[4], (DEPTH, FOX_HEADS), 0.1),
        "dn_conv_w": nrm(ks[5], (DEPTH, CONV_WIDTH, 3 * DN_WIDTH), CONV_WIDTH ** -0.5),
        "dn_a_log": jnp.log(jax.random.uniform(ks[6], (DEPTH, DN_HEADS), f32, 1.0, 16.0)),
        "dn_dt_bias": dt + jnp.log(-jnp.expm1(-dt)),
        "dn_out_norm_w": 1.0 + nrm(ks[8], (DEPTH, DN_HEAD_DIM), 0.02),
        "w_branch_fox": nrm(ks[9], (DEPTH, FOX_WIDTH, D_MODEL), FOX_WIDTH ** -0.5),
        "w_branch_dn": nrm(ks[10], (DEPTH, DN_WIDTH, D_MODEL), DN_WIDTH ** -0.5),
        "w_out": nrm(ks[11], (DEPTH, D_MODEL, D_MODEL), D_MODEL ** -0.5),
        "ffn_norm_w": 1.0 + nrm(ks[12], (DEPTH, D_MODEL), 0.02),
        "w_ffn_gate": nrm(ks[13], (DEPTH, D_MODEL, D_FF), D_MODEL ** -0.5),
        "w_ffn_up": nrm(ks[14], (DEPTH, D_MODEL, D_FF), D_MODEL ** -0.5),
        "w_ffn_down": nrm(ks[15], (DEPTH, D_FF, D_MODEL), D_FF ** -0.5),
        "final_norm_w": 1.0 + nrm(ks[16], (D_MODEL,), 0.02),
    }


def _fwd_reference(x, meta_tokens, mix_norm_w, w_in, fox_forget_bias, dn_conv_w, dn_a_log,
              dn_dt_bias, dn_out_norm_w, w_branch_fox, w_branch_dn, w_out, ffn_norm_w,
              w_ffn_gate, w_ffn_up, w_ffn_down, final_norm_w):
    B = x.shape[0]
    pad = jnp.zeros((B, META_PAD, D_MODEL), x.dtype)
    meta = jnp.broadcast_to(meta_tokens.astype(x.dtype)[None], (B, N_META, D_MODEL))
    h = jnp.concatenate([pad, meta, x], axis=1)
    L = h.shape[1]
    valid = jnp.arange(L) >= META_PAD
    for l in range(DEPTH):
        h = h + hybrid_mixer(rmsnorm(h, mix_norm_w[l]), valid, w_in[l], fox_forget_bias[l],
                             dn_conv_w[l], dn_a_log[l], dn_dt_bias[l], dn_out_norm_w[l],
                             w_branch_fox[l], w_branch_dn[l], w_out[l])
        h = h + swiglu(rmsnorm(h, ffn_norm_w[l]), w_ffn_gate[l], w_ffn_up[l], w_ffn_down[l])
    h = rmsnorm(h, final_norm_w)
    return h[:, PREFIX:, :]


import jax as _jax
import jax.numpy as _jnp

TWIN_FORMAT = 'train_step'
FWD_PARAMS = ['x', 'meta_tokens', 'mix_norm_w', 'w_in', 'fox_forget_bias', 'dn_conv_w', 'dn_a_log', 'dn_dt_bias', 'dn_out_norm_w', 'w_branch_fox', 'w_branch_dn', 'w_out', 'ffn_norm_w', 'w_ffn_gate', 'w_ffn_up', 'w_ffn_down', 'final_norm_w']
TWIN_WEIGHTS = ['meta_tokens', 'mix_norm_w', 'w_in', 'fox_forget_bias', 'dn_conv_w', 'dn_a_log', 'dn_dt_bias', 'dn_out_norm_w', 'w_branch_fox', 'w_branch_dn', 'w_out', 'ffn_norm_w', 'w_ffn_gate', 'w_ffn_up', 'w_ffn_down', 'final_norm_w']
TWIN_DIFF_INPUT = 'x'
TWIN_INPUTS = ['x', 'meta_tokens', 'mix_norm_w', 'w_in', 'fox_forget_bias', 'dn_conv_w', 'dn_a_log', 'dn_dt_bias', 'dn_out_norm_w', 'w_branch_fox', 'w_branch_dn', 'w_out', 'ffn_norm_w', 'w_ffn_gate', 'w_ffn_up', 'w_ffn_down', 'final_norm_w', 'loss_target', 'm_meta_tokens', 'm_mix_norm_w', 'm_w_in', 'm_fox_forget_bias', 'm_dn_conv_w', 'm_dn_a_log', 'm_dn_dt_bias', 'm_dn_out_norm_w', 'm_w_branch_fox', 'm_w_branch_dn', 'm_w_out', 'm_ffn_norm_w', 'm_w_ffn_gate', 'm_w_ffn_up', 'm_w_ffn_down', 'm_final_norm_w', 'v_meta_tokens', 'v_mix_norm_w', 'v_w_in', 'v_fox_forget_bias', 'v_dn_conv_w', 'v_dn_a_log', 'v_dn_dt_bias', 'v_dn_out_norm_w', 'v_w_branch_fox', 'v_w_branch_dn', 'v_w_out', 'v_ffn_norm_w', 'v_w_ffn_gate', 'v_w_ffn_up', 'v_w_ffn_down', 'v_final_norm_w']
TWIN_OUTPUTS = ['loss', 'grad_x', 'grad_meta_tokens', 'grad_mix_norm_w', 'grad_w_in', 'grad_fox_forget_bias', 'grad_dn_conv_w', 'grad_dn_a_log', 'grad_dn_dt_bias', 'grad_dn_out_norm_w', 'grad_w_branch_fox', 'grad_w_branch_dn', 'grad_w_out', 'grad_ffn_norm_w', 'grad_w_ffn_gate', 'grad_w_ffn_up', 'grad_w_ffn_down', 'grad_final_norm_w', 'delta_meta_tokens', 'delta_mix_norm_w', 'delta_w_in', 'delta_fox_forget_bias', 'delta_dn_conv_w', 'delta_dn_a_log', 'delta_dn_dt_bias', 'delta_dn_out_norm_w', 'delta_w_branch_fox', 'delta_w_branch_dn', 'delta_w_out', 'delta_ffn_norm_w', 'delta_w_ffn_gate', 'delta_w_ffn_up', 'delta_w_ffn_down', 'delta_final_norm_w', 'new_m_meta_tokens', 'new_m_mix_norm_w', 'new_m_w_in', 'new_m_fox_forget_bias', 'new_m_dn_conv_w', 'new_m_dn_a_log', 'new_m_dn_dt_bias', 'new_m_dn_out_norm_w', 'new_m_w_branch_fox', 'new_m_w_branch_dn', 'new_m_w_out', 'new_m_ffn_norm_w', 'new_m_w_ffn_gate', 'new_m_w_ffn_up', 'new_m_w_ffn_down', 'new_m_final_norm_w', 'new_v_meta_tokens', 'new_v_mix_norm_w', 'new_v_w_in', 'new_v_fox_forget_bias', 'new_v_dn_conv_w', 'new_v_dn_a_log', 'new_v_dn_dt_bias', 'new_v_dn_out_norm_w', 'new_v_w_branch_fox', 'new_v_w_branch_dn', 'new_v_w_out', 'new_v_ffn_norm_w', 'new_v_w_ffn_gate', 'new_v_w_ffn_up', 'new_v_w_ffn_down', 'new_v_final_norm_w']
TWIN_LEAF_KINDS = {'loss': 'loss', 'grad_x': 'grad_x', 'grad_meta_tokens': 'grad_w', 'grad_mix_norm_w': 'grad_w', 'grad_w_in': 'grad_w', 'grad_fox_forget_bias': 'grad_w', 'grad_dn_conv_w': 'grad_w', 'grad_dn_a_log': 'grad_w', 'grad_dn_dt_bias': 'grad_w', 'grad_dn_out_norm_w': 'grad_w', 'grad_w_branch_fox': 'grad_w', 'grad_w_branch_dn': 'grad_w', 'grad_w_out': 'grad_w', 'grad_ffn_norm_w': 'grad_w', 'grad_w_ffn_gate': 'grad_w', 'grad_w_ffn_up': 'grad_w', 'grad_w_ffn_down': 'grad_w', 'grad_final_norm_w': 'grad_w', 'delta_meta_tokens': 'delta_w', 'delta_mix_norm_w': 'delta_w', 'delta_w_in': 'delta_w', 'delta_fox_forget_bias': 'delta_w', 'delta_dn_conv_w': 'delta_w', 'delta_dn_a_log': 'delta_w', 'delta_dn_dt_bias': 'delta_w', 'delta_dn_out_norm_w': 'delta_w', 'delta_w_branch_fox': 'delta_w', 'delta_w_branch_dn': 'delta_w', 'delta_w_out': 'delta_w', 'delta_ffn_norm_w': 'delta_w', 'delta_w_ffn_gate': 'delta_w', 'delta_w_ffn_up': 'delta_w', 'delta_w_ffn_down': 'delta_w', 'delta_final_norm_w': 'delta_w', 'new_m_meta_tokens': 'new_m', 'new_m_mix_norm_w': 'new_m', 'new_m_w_in': 'new_m', 'new_m_fox_forget_bias': 'new_m', 'new_m_dn_conv_w': 'new_m', 'new_m_dn_a_log': 'new_m', 'new_m_dn_dt_bias': 'new_m', 'new_m_dn_out_norm_w': 'new_m', 'new_m_w_branch_fox': 'new_m', 'new_m_w_branch_dn': 'new_m', 'new_m_w_out': 'new_m', 'new_m_ffn_norm_w': 'new_m', 'new_m_w_ffn_gate': 'new_m', 'new_m_w_ffn_up': 'new_m', 'new_m_w_ffn_down': 'new_m', 'new_m_final_norm_w': 'new_m', 'new_v_meta_tokens': 'new_v', 'new_v_mix_norm_w': 'new_v', 'new_v_w_in': 'new_v', 'new_v_fox_forget_bias': 'new_v', 'new_v_dn_conv_w': 'new_v', 'new_v_dn_a_log': 'new_v', 'new_v_dn_dt_bias': 'new_v', 'new_v_dn_out_norm_w': 'new_v', 'new_v_w_branch_fox': 'new_v', 'new_v_w_branch_dn': 'new_v', 'new_v_w_out': 'new_v', 'new_v_ffn_norm_w': 'new_v', 'new_v_w_ffn_gate': 'new_v', 'new_v_w_ffn_up': 'new_v', 'new_v_w_ffn_down': 'new_v', 'new_v_final_norm_w': 'new_v'}


def _forward(args):
    return _fwd_reference(*[args[k] for k in FWD_PARAMS])


def _output_shape():
    def fwd():
        inp = _fwd_setup_inputs(0)
        return _fwd_reference(*[inp[k] for k in FWD_PARAMS])
    out = _jax.eval_shape(fwd)
    return out.shape, out.dtype

N_MICROBATCH = 1
ADAM_LR = 0.001
ADAM_B1 = 0.9
ADAM_B2 = 0.999
ADAM_EPS = 1e-08
ADAM_WD = 0.01
ADAM_STEP = 10
PER_EXAMPLE_BATCH_AXIS = {'x': 0, 'loss_target': 0}
SHARED_INPUTS = []
_WEIGHT_DTYPES = {'meta_tokens': _jnp.float32, 'mix_norm_w': _jnp.float32, 'w_in': _jnp.float32, 'fox_forget_bias': _jnp.float32, 'dn_conv_w': _jnp.float32, 'dn_a_log': _jnp.float32, 'dn_dt_bias': _jnp.float32, 'dn_out_norm_w': _jnp.float32, 'w_branch_fox': _jnp.float32, 'w_branch_dn': _jnp.float32, 'w_out': _jnp.float32, 'ffn_norm_w': _jnp.float32, 'w_ffn_gate': _jnp.float32, 'w_ffn_up': _jnp.float32, 'w_ffn_down': _jnp.float32, 'final_norm_w': _jnp.float32}
MOMENT_SCALE = {'meta_tokens': 3.823616e-03, 'mix_norm_w': 1.161169e-01, 'w_in': 4.843670e-02, 'fox_forget_bias': 1.751228e-01, 'dn_conv_w': 6.770370e-02, 'dn_a_log': 2.782090e-01, 'dn_dt_bias': 2.754158e-01, 'dn_out_norm_w': 2.230801e-01, 'w_branch_fox': 2.946985e-02, 'w_branch_dn': 6.012317e-02, 'w_out': 6.656763e-02, 'ffn_norm_w': 1.271308e-01, 'w_ffn_gate': 5.423513e-02, 'w_ffn_up': 5.259385e-02, 'w_ffn_down': 8.724729e-02, 'final_norm_w': 3.202544e+01}


def _to_microbatches(a, axis):
    t = _jnp.moveaxis(a, axis, 0)
    t = t.reshape((N_MICROBATCH, t.shape[0] // N_MICROBATCH) + t.shape[1:])
    return _jnp.moveaxis(t, 1, axis + 1)


def setup_inputs(seed: int = 0) -> dict:
    inp = _fwd_setup_inputs(seed)
    key = _jax.random.fold_in(_jax.random.key(seed), 7919)
    shape, _ = _output_shape()
    out = dict(inp)
    out["loss_target"] = _jax.random.normal(_jax.random.fold_in(key, 0), shape, _jnp.float32)
    for i, name in enumerate(TWIN_WEIGHTS):
        w = inp[name].astype(_jnp.float32)
        if MOMENT_SCALE is None:
            s = _jnp.sqrt(_jnp.mean(_jnp.square(w)) + 1e-30)
        else:
            s = MOMENT_SCALE[name]
        km, kv = _jax.random.split(_jax.random.fold_in(key, i + 1))
        out[name] = w
        out["m_" + name] = s * _jax.random.normal(km, w.shape, _jnp.float32)
        out["v_" + name] = (s * s) * _jax.random.uniform(kv, w.shape, _jnp.float32, 0.5, 1.5)
    if N_MICROBATCH > 1:
        for name, axis in PER_EXAMPLE_BATCH_AXIS.items():
            out[name] = _to_microbatches(out[name], axis)
    return {'x': out['x'], 'meta_tokens': out['meta_tokens'], 'mix_norm_w': out['mix_norm_w'], 'w_in': out['w_in'], 'fox_forget_bias': out['fox_forget_bias'], 'dn_conv_w': out['dn_conv_w'], 'dn_a_log': out['dn_a_log'], 'dn_dt_bias': out['dn_dt_bias'], 'dn_out_norm_w': out['dn_out_norm_w'], 'w_branch_fox': out['w_branch_fox'], 'w_branch_dn': out['w_branch_dn'], 'w_out': out['w_out'], 'ffn_norm_w': out['ffn_norm_w'], 'w_ffn_gate': out['w_ffn_gate'], 'w_ffn_up': out['w_ffn_up'], 'w_ffn_down': out['w_ffn_down'], 'final_norm_w': out['final_norm_w'], 'loss_target': out['loss_target'], 'm_meta_tokens': out['m_meta_tokens'], 'm_mix_norm_w': out['m_mix_norm_w'], 'm_w_in': out['m_w_in'], 'm_fox_forget_bias': out['m_fox_forget_bias'], 'm_dn_conv_w': out['m_dn_conv_w'], 'm_dn_a_log': out['m_dn_a_log'], 'm_dn_dt_bias': out['m_dn_dt_bias'], 'm_dn_out_norm_w': out['m_dn_out_norm_w'], 'm_w_branch_fox': out['m_w_branch_fox'], 'm_w_branch_dn': out['m_w_branch_dn'], 'm_w_out': out['m_w_out'], 'm_ffn_norm_w': out['m_ffn_norm_w'], 'm_w_ffn_gate': out['m_w_ffn_gate'], 'm_w_ffn_up': out['m_w_ffn_up'], 'm_w_ffn_down': out['m_w_ffn_down'], 'm_final_norm_w': out['m_final_norm_w'], 'v_meta_tokens': out['v_meta_tokens'], 'v_mix_norm_w': out['v_mix_norm_w'], 'v_w_in': out['v_w_in'], 'v_fox_forget_bias': out['v_fox_forget_bias'], 'v_dn_conv_w': out['v_dn_conv_w'], 'v_dn_a_log': out['v_dn_a_log'], 'v_dn_dt_bias': out['v_dn_dt_bias'], 'v_dn_out_norm_w': out['v_dn_out_norm_w'], 'v_w_branch_fox': out['v_w_branch_fox'], 'v_w_branch_dn': out['v_w_branch_dn'], 'v_w_out': out['v_w_out'], 'v_ffn_norm_w': out['v_ffn_norm_w'], 'v_w_ffn_gate': out['v_w_ffn_gate'], 'v_w_ffn_up': out['v_w_ffn_up'], 'v_w_ffn_down': out['v_w_ffn_down'], 'v_final_norm_w': out['v_final_norm_w']}


def _loss(weights, diff, rest, loss_target):
    with _jax.named_scope("forward"):
        args = {**rest, TWIN_DIFF_INPUT: diff, **{k: w.astype(_WEIGHT_DTYPES[k]) for k, w in weights.items()}}
        y = _forward(args)
    with _jax.named_scope("loss_head"):
        err = _jnp.square(y.astype(_jnp.float32) - loss_target)
        return 0.5 * _jnp.sum(_jnp.mean(err, axis=-1)) if err.ndim else 0.5 * err


def _adamw(w, g, m, v):
    m = ADAM_B1 * m + (1.0 - ADAM_B1) * g
    v = ADAM_B2 * v + (1.0 - ADAM_B2) * _jnp.square(g)
    m_hat = m / (1.0 - ADAM_B1 ** ADAM_STEP)
    v_hat = v / (1.0 - ADAM_B2 ** ADAM_STEP)
    delta = -ADAM_LR * (m_hat / (_jnp.sqrt(v_hat) + ADAM_EPS) + ADAM_WD * w)
    return delta, m, v


def reference(x, meta_tokens, mix_norm_w, w_in, fox_forget_bias, dn_conv_w, dn_a_log, dn_dt_bias, dn_out_norm_w, w_branch_fox, w_branch_dn, w_out, ffn_norm_w, w_ffn_gate, w_ffn_up, w_ffn_down, final_norm_w, loss_target, m_meta_tokens, m_mix_norm_w, m_w_in, m_fox_forget_bias, m_dn_conv_w, m_dn_a_log, m_dn_dt_bias, m_dn_out_norm_w, m_w_branch_fox, m_w_branch_dn, m_w_out, m_ffn_norm_w, m_w_ffn_gate, m_w_ffn_up, m_w_ffn_down, m_final_norm_w, v_meta_tokens, v_mix_norm_w, v_w_in, v_fox_forget_bias, v_dn_conv_w, v_dn_a_log, v_dn_dt_bias, v_dn_out_norm_w, v_w_branch_fox, v_w_branch_dn, v_w_out, v_ffn_norm_w, v_w_ffn_gate, v_w_ffn_up, v_w_ffn_down, v_final_norm_w):
    given = dict(x=x, meta_tokens=meta_tokens, mix_norm_w=mix_norm_w, w_in=w_in, fox_forget_bias=fox_forget_bias, dn_conv_w=dn_conv_w, dn_a_log=dn_a_log, dn_dt_bias=dn_dt_bias, dn_out_norm_w=dn_out_norm_w, w_branch_fox=w_branch_fox, w_branch_dn=w_branch_dn, w_out=w_out, ffn_norm_w=ffn_norm_w, w_ffn_gate=w_ffn_gate, w_ffn_up=w_ffn_up, w_ffn_down=w_ffn_down, final_norm_w=final_norm_w, loss_target=loss_target, m_meta_tokens=m_meta_tokens, m_mix_norm_w=m_mix_norm_w, m_w_in=m_w_in, m_fox_forget_bias=m_fox_forget_bias, m_dn_conv_w=m_dn_conv_w, m_dn_a_log=m_dn_a_log, m_dn_dt_bias=m_dn_dt_bias, m_dn_out_norm_w=m_dn_out_norm_w, m_w_branch_fox=m_w_branch_fox, m_w_branch_dn=m_w_branch_dn, m_w_out=m_w_out, m_ffn_norm_w=m_ffn_norm_w, m_w_ffn_gate=m_w_ffn_gate, m_w_ffn_up=m_w_ffn_up, m_w_ffn_down=m_w_ffn_down, m_final_norm_w=m_final_norm_w, v_meta_tokens=v_meta_tokens, v_mix_norm_w=v_mix_norm_w, v_w_in=v_w_in, v_fox_forget_bias=v_fox_forget_bias, v_dn_conv_w=v_dn_conv_w, v_dn_a_log=v_dn_a_log, v_dn_dt_bias=v_dn_dt_bias, v_dn_out_norm_w=v_dn_out_norm_w, v_w_branch_fox=v_w_branch_fox, v_w_branch_dn=v_w_branch_dn, v_w_out=v_w_out, v_ffn_norm_w=v_ffn_norm_w, v_w_ffn_gate=v_w_ffn_gate, v_w_ffn_up=v_w_ffn_up, v_w_ffn_down=v_w_ffn_down, v_final_norm_w=v_final_norm_w)
    weights = {n: given[n] for n in TWIN_WEIGHTS}
    shared = {n: given[n] for n in SHARED_INPUTS}
    per_example = {n: given[n] for n in ['x']}
    grad_fn = _jax.value_and_grad(_loss, argnums=(0, 1))

    def one_microbatch(ex, loss_target):
        ex = dict(ex)
        diff = ex.pop(TWIN_DIFF_INPUT)
        return grad_fn(weights, diff, {**shared, **ex}, loss_target)

    if N_MICROBATCH == 1:
        loss, (grad_w, grad_x) = one_microbatch(per_example, given["loss_target"])
    else:
        def body(carry, xs):
            loss_sum, grad_sum = carry
            l_k, (gw_k, gx_k) = one_microbatch(xs[0], xs[1])
            with _jax.named_scope("update"):
                return (loss_sum + l_k, _jax.tree.map(_jnp.add, grad_sum, gw_k)), gx_k

        init = (_jnp.zeros((), _jnp.float32), _jax.tree.map(_jnp.zeros_like, weights))
        (loss, grad_w), grad_x = _jax.lax.scan(body, init, (per_example, given["loss_target"]))
    with _jax.named_scope("update"):
        delta_w, new_m, new_v = {}, {}, {}
        for n in TWIN_WEIGHTS:
            delta_w[n], new_m[n], new_v[n] = _adamw(weights[n], grad_w[n], given["m_" + n], given["v_" + n])
    return (loss, grad_x, *[grad_w[n] for n in TWIN_WEIGHTS], *[delta_w[n] for n in TWIN_WEIGHTS],
            *[new_m[n] for n in TWIN_WEIGHTS], *[new_v[n] for n in TWIN_WEIGHTS])
```

```python
import functools

import jax
import jax.numpy as jnp
from jax import lax
from jax.experimental import pallas as pl
from jax.experimental.pallas import tpu as pltpu

f32, bf16 = jnp.float32, jnp.bfloat16
HI = lax.Precision.HIGHEST
MESH = pl.DeviceIdType.MESH
SDS = jax.ShapeDtypeStruct

N_DEV = 8
N_META = 16
PREFIX = 128
N_PAD = PREFIX - N_META
FOX_H, FOX_D = 8, 64
DN_H, DN_D = 4, 128
DN_C = 64
CONV_K = 4
HP = 128
EPS = 1e-6
NEG = -1e30
C_Q0, C_K0 = 64, 67
LSE_COL = 64

ADAM_LR, ADAM_B1, ADAM_B2, ADAM_EPS, ADAM_WD, ADAM_STEP = 0.001, 0.9, 0.999, 1e-08, 0.01, 10

VMEM_LIMIT_V7X = 56 * 1024 * 1024
ROW_TILES = (384, 128)
ATTN_TILES = (384, 128)
FFN_TILES = (192, 64)
COL_TILES = (512, 256, 128)


def _pick(n, cands):
    for c in cands:
        if n % c == 0:
            return c
    raise ValueError(f"no tile of {cands} divides {n}")


def _cp(n_axes=1):
    return pltpu.CompilerParams(dimension_semantics=("arbitrary",) * n_axes, vmem_limit_bytes=VMEM_LIMIT_V7X)


def _b(x):
    return x.astype(bf16)


def _dot(a, b):
    return jnp.dot(a, b, preferred_element_type=f32)


def _dot_nt(a, b):
    return lax.dot_general(a, b, (((1,), (1,)), ((), ())), preferred_element_type=f32)


def _dot_tn(a, b):
    return lax.dot_general(a, b, (((0,), (0,)), ((), ())), preferred_element_type=f32)


def _dot_hi(a, b):
    return jnp.dot(a, b, preferred_element_type=f32, precision=HI)


def _iota(shape, dim):
    return lax.broadcasted_iota(jnp.int32, shape, dim)


def _rms(x, w):
    return x * lax.rsqrt(jnp.mean(x * x, axis=-1, keepdims=True) + EPS) * w


def _sigmoid(x):
    return jax.nn.sigmoid(x)


def _load_once(pairs, sems):
    @pl.when(pl.program_id(0) == 0)
    def _():
        cps = [pltpu.make_async_copy(src, dst, sems.at[k]) for k, (src, dst) in enumerate(pairs)]
        for cp in cps:
            cp.start()
        for cp in cps:
            cp.wait()


def _seg_layout(d_model):
    return (("fq", FOX_H * HP, bf16), ("fk", FOX_H * HP, bf16), ("fv", FOX_H * HP, bf16),
            ("dn", 3 * DN_H * DN_D, f32), ("dz", DN_H * DN_D, f32), ("ga", d_model, f32), ("gb", d_model, f32),
            ("sf", HP, f32), ("sd", HP, f32))


def _in_proj(h0, w1, wp):
    L, D = h0.shape
    NP = wp.shape[1]
    TM = _pick(L, ROW_TILES)
    segs = _seg_layout(D)
    offs, o = [], 0
    for _, wd, _ in segs:
        offs.append(o)
        o += wd
    assert o == NP

    def body(h_ref, w1_ref, wp_hbm, xn_ref, *rest):
        outs, (wp_v, sems) = rest[:len(segs)], rest[len(segs):]
        _load_once([(wp_hbm, wp_v)], sems)
        xn = _b(_rms(h_ref[...], w1_ref[...]))
        xn_ref[...] = xn
        for o_ref, off, (_, wd, _) in zip(outs, offs, segs):
            o_ref[...] = _dot(xn, wp_v[:, off:off + wd]).astype(o_ref.dtype)

    row = lambda wd: pl.BlockSpec((TM, wd), lambda i: (i, 0))
    return pl.pallas_call(
        body, name="in_proj", grid=(L // TM,),
        in_specs=[row(D), pl.BlockSpec((1, D), lambda i: (0, 0)), pl.BlockSpec(memory_space=pl.ANY)],
        out_specs=[row(D)] + [row(wd) for _, wd, _ in segs],
        out_shape=[SDS((L, D), bf16)] + [SDS((L, wd), dt) for _, wd, dt in segs],
        scratch_shapes=[pltpu.VMEM((D, NP), bf16), pltpu.SemaphoreType.DMA((1,))],
        compiler_params=_cp())(h0, w1, wp)


def _in_proj_bwd(dproj, wp, h0, w1, dh1):
    L, D = h0.shape
    NP = wp.shape[1]
    TM = _pick(L, ROW_TILES)

    def body(dp_ref, wp_hbm, h_ref, w1_ref, dh1_ref, dh0_ref, acc_ref, wp_v, sems):
        _load_once([(wp_hbm, wp_v)], sems)

        @pl.when(pl.program_id(0) == 0)
        def _():
            acc_ref[...] = jnp.zeros_like(acc_ref)

        dxn = _dot_nt(dp_ref[...], wp_v[...])
        _, vjp = jax.vjp(_rms, h_ref[...], w1_ref[...])
        dh0n, dw1 = vjp(dxn)
        dh0_ref[...] = dh1_ref[...] + dh0n
        acc_ref[0:1, :] += dw1

    row = lambda wd: pl.BlockSpec((TM, wd), lambda i: (i, 0))
    return pl.pallas_call(
        body, name="in_proj_bwd", grid=(L // TM,),
        in_specs=[row(NP), pl.BlockSpec(memory_space=pl.ANY), row(D), pl.BlockSpec((1, D), lambda i: (0, 0)), row(D)],
        out_specs=[row(D), pl.BlockSpec((8, D), lambda i: (0, 0))],
        out_shape=[SDS((L, D), f32), SDS((8, D), f32)],
        scratch_shapes=[pltpu.VMEM((D, NP), bf16), pltpu.SemaphoreType.DMA((1,))],
        compiler_params=_cp())(dproj, wp, h0, w1, dh1)


def _matmul_tn(a, b, name):
    L, M = a.shape
    N = b.shape[1]
    tk = _pick(L, ROW_TILES)
    bn = N if N <= COL_TILES[0] else _pick(N, COL_TILES)

    def body(a_ref, b_ref, o_ref):
        @pl.when(pl.program_id(1) == 0)
        def _():
            o_ref[...] = jnp.zeros_like(o_ref)

        o_ref[...] += _dot_tn(a_ref[...], b_ref[...])

    return pl.pallas_call(
        body, name=name, grid=(N // bn, L // tk),
        in_specs=[pl.BlockSpec((tk, M), lambda n, k: (k, 0)), pl.BlockSpec((tk, bn), lambda n, k: (k, n))],
        out_specs=pl.BlockSpec((M, bn), lambda n, k: (0, n)),
        out_shape=SDS((M, N), f32), compiler_params=_cp(2))(a, b)


def _fox_prep(fq, fk, sf, bias_p):
    L = fq.shape[0]
    T = HP
    NT = L // T

    def body(fq_ref, fk_ref, sf_ref, b_ref, qa_ref, ka_ref, carry):
        @pl.when(pl.program_id(0) == 0)
        def _():
            carry[...] = jnp.zeros_like(carry)

        lane, row = _iota((T, HP), 1), _iota((T, HP), 0)
        logf = jnp.where(lane < FOX_H, jax.nn.log_sigmoid(sf_ref[...] + b_ref[...]), 0.0)
        c = _dot_hi((row >= lane).astype(f32), logf) + carry[...]
        carry[...] = jnp.sum(jnp.where(row == T - 1, c, 0.0), axis=0, keepdims=True)
        ones_q = jnp.where((lane >= C_K0) & (lane < C_K0 + 3), 1.0, 0.0)
        ones_k = jnp.where((lane >= C_Q0) & (lane < C_Q0 + 3), 1.0, 0.0)
        for h in range(FOX_H):
            ch = jnp.broadcast_to(jnp.sum(jnp.where(lane == h, c, 0.0), axis=1, keepdims=True), (T, HP))
            c1 = _b(ch).astype(f32)
            c2 = _b(ch - c1).astype(f32)
            c3 = _b(ch - c1 - c2).astype(f32)
            cq = jnp.where(lane == C_Q0, c1, 0.0) + jnp.where(lane == C_Q0 + 1, c2, 0.0) + jnp.where(lane == C_Q0 + 2, c3, 0.0)
            ck = jnp.where(lane == C_K0, c1, 0.0) + jnp.where(lane == C_K0 + 1, c2, 0.0) + jnp.where(lane == C_K0 + 2, c3, 0.0)
            q = fq_ref[:, h * HP:(h + 1) * HP].astype(f32) * (FOX_D ** -0.5)
            k = fk_ref[:, h * HP:(h + 1) * HP].astype(f32)
            qa_ref[h] = _b(q + cq + ones_q)
            ka_ref[h] = _b(k + ones_k - ck)

    return pl.pallas_call(
        body, name="fox_prep", grid=(NT,),
        in_specs=[pl.BlockSpec((T, FOX_H * HP), lambda i: (i, 0)), pl.BlockSpec((T, FOX_H * HP), lambda i: (i, 0)),
                  pl.BlockSpec((T, HP), lambda i: (i, 0)), pl.BlockSpec((1, HP), lambda i: (0, 0))],
        out_specs=[pl.BlockSpec((FOX_H, T, HP), lambda i: (0, i, 0))] * 2,
        out_shape=[SDS((FOX_H, L, HP), bf16)] * 2,
        scratch_shapes=[pltpu.VMEM((1, HP), f32)], compiler_params=_cp())(fq, fk, sf, bias_p)


def _fox_prep_bwd(dqa, dka, sf, bias_p):
    L = sf.shape[0]
    T = HP
    NT = L // T
    rev = lambda i: (NT - 1 - i, 0)

    def body(dq_ref, dk_ref, sf_ref, b_ref, dfq_ref, dfk_ref, dsf_ref, db_ref, carry):
        @pl.when(pl.program_id(0) == 0)
        def _():
            carry[...] = jnp.zeros_like(carry)
            db_ref[...] = jnp.zeros_like(db_ref)

        dq, dk = dq_ref[...], dk_ref[...]
        dfq_ref[...] = _b(dq * (FOX_D ** -0.5))
        dfk_ref[...] = _b(dk)
        lane, row = _iota((T, HP), 1), _iota((T, HP), 0)
        dc = jnp.zeros((T, HP), f32)
        for h in range(FOX_H):
            col = jnp.sum(jnp.where(lane == C_Q0, dq[:, h * HP:(h + 1) * HP], 0.0)
                          - jnp.where(lane == C_K0, dk[:, h * HP:(h + 1) * HP], 0.0), axis=1, keepdims=True)
            dc = dc + jnp.where(lane == h, col, 0.0)
        dl = _dot_hi((row <= lane).astype(f32), dc) + carry[...]
        carry[...] = jnp.sum(jnp.where(row == 0, dl, 0.0), axis=0, keepdims=True)
        dx = jnp.where(lane < FOX_H, dl * _sigmoid(-(sf_ref[...] + b_ref[...])), 0.0)
        dsf_ref[...] = _b(dx)
        db_ref[0:1, :] += jnp.sum(dx, axis=0, keepdims=True)

    return pl.pallas_call(
        body, name="fox_prep_bwd", grid=(NT,),
        in_specs=[pl.BlockSpec((T, FOX_H * HP), rev), pl.BlockSpec((T, FOX_H * HP), rev),
                  pl.BlockSpec((T, HP), rev), pl.BlockSpec((1, HP), lambda i: (0, 0))],
        out_specs=[pl.BlockSpec((T, FOX_H * HP), rev), pl.BlockSpec((T, FOX_H * HP), rev),
                   pl.BlockSpec((T, HP), rev), pl.BlockSpec((8, HP), lambda i: (0, 0))],
        out_shape=[SDS((L, FOX_H * HP), bf16), SDS((L, FOX_H * HP), bf16), SDS((L, HP), bf16), SDS((8, HP), f32)],
        scratch_shapes=[pltpu.VMEM((1, HP), f32)], compiler_params=_cp())(dqa, dka, sf, bias_p)


def _fox_fwd(qa, ka, fv):
    L = qa.shape[1]
    TQ = TK = _pick(L, ATTN_TILES)

    def body(q_ref, k_ref, v_ref, o_ref):
        i = pl.program_id(1)
        q = q_ref[0]
        rowg = i * TQ + _iota((TQ, TK), 0)
        colb = _iota((TQ, TK), 1)

        def step(j, carry):
            m, l, acc = carry
            k0 = pl.multiple_of(j * TK, TK)
            kt = k_ref[0, pl.ds(k0, TK), :]
            vt = v_ref[pl.ds(k0, TK), :]
            colg = colb + j * TK
            s = jnp.where((colg <= rowg) & (colg >= N_PAD), _dot_nt(q, kt), NEG)
            m_new = jnp.maximum(m, jnp.max(s, axis=1, keepdims=True))
            p = jnp.exp(s - m_new)
            alpha = jnp.exp(m - m_new)
            l = alpha * l + jnp.sum(p, axis=1, keepdims=True)
            acc = alpha * acc + _dot(_b(p), vt)
            return m_new, l, acc

        m, l, acc = lax.fori_loop(0, i + 1, step, (jnp.full((TQ, 1), NEG, f32), jnp.zeros((TQ, 1), f32), jnp.zeros((TQ, HP), f32)))
        o_ref[...] = jnp.where(_iota((TQ, HP), 1) == LSE_COL, m + jnp.log(l), acc / l)

    return pl.pallas_call(
        body, name="fox_fwd", grid=(FOX_H, L // TQ),
        in_specs=[pl.BlockSpec((1, TQ, HP), lambda h, i: (h, i, 0)), pl.BlockSpec((1, L, HP), lambda h, i: (h, 0, 0)),
                  pl.BlockSpec((L, HP), lambda h, i: (0, h))],
        out_specs=pl.BlockSpec((TQ, HP), lambda h, i: (i, h)),
        out_shape=SDS((L, FOX_H * HP), f32), compiler_params=_cp(2))(qa, ka, fv)


def _fox_bwd(qa, ka, fv, op, dop):
    L = qa.shape[1]
    TQ = TK = _pick(L, ATTN_TILES)
    NQ = L // TQ

    def body(q_ref, k_ref, v_ref, o_ref, do_ref, dq_ref, dk_ref, dv_ref):
        j = pl.program_id(1)

        @pl.when(j == 0)
        def _():
            dq_ref[...] = jnp.zeros_like(dq_ref)

        kt, vt = k_ref[0], v_ref[...]
        colg = j * TK + _iota((TQ, TK), 1)
        rowb = _iota((TQ, TK), 0)
        lane = _iota((TQ, HP), 1)

        def step(i, carry):
            dk, dv = carry
            r0 = pl.multiple_of(i * TQ, TQ)
            q = q_ref[0, pl.ds(r0, TQ), :]
            o = o_ref[pl.ds(r0, TQ), :]
            do = do_ref[pl.ds(r0, TQ), :]
            lse = jnp.sum(jnp.where(lane == LSE_COL, o, 0.0), axis=1, keepdims=True)
            delta = jnp.sum(jnp.where(lane < FOX_D, o * do, 0.0), axis=1, keepdims=True)
            rowg = rowb + i * TQ
            p = jnp.where((colg <= rowg) & (colg >= N_PAD), jnp.exp(_dot_nt(q, kt) - lse), 0.0)
            dob = _b(do)
            dv = dv + _dot_tn(_b(p), dob)
            ds = _b(p * (_dot_nt(dob, vt) - delta))
            dq_ref[pl.ds(r0, TQ), :] += _dot(ds, kt)
            dk = dk + _dot_tn(ds, q)
            return dk, dv

        dk, dv = lax.fori_loop(j, NQ, step, (jnp.zeros((TK, HP), f32), jnp.zeros((TK, HP), f32)))
        dk_ref[...] = dk
        dv_ref[...] = _b(dv)

    head = pl.BlockSpec((L, HP), lambda h, j: (0, h))
    tile = pl.BlockSpec((TK, HP), lambda h, j: (j, h))
    return pl.pallas_call(
        body, name="fox_bwd", grid=(FOX_H, L // TK),
        in_specs=[pl.BlockSpec((1, L, HP), lambda h, j: (h, 0, 0)), pl.BlockSpec((1, TK, HP), lambda h, j: (h, j, 0)), tile, head, head],
        out_specs=[head, tile, tile],
        out_shape=[SDS((L, FOX_H * HP), f32), SDS((L, FOX_H * HP), f32), SDS((L, FOX_H * HP), bf16)],
        compiler_params=_cp(2))(qa, ka, fv, op, dop)


def _dn_post(y, sd, alog_p, dt_p, valid):
    a = y * _sigmoid(y)
    W = DN_H * DN_D
    heads = []
    for part, scale in ((0, DN_D ** -0.5), (1, 1.0)):
        for h in range(DN_H):
            xh = a[:, part * W + h * DN_D:part * W + (h + 1) * DN_D]
            heads.append(xh * lax.rsqrt(jnp.sum(xh * xh, axis=-1, keepdims=True) + EPS) * scale)
    q = jnp.concatenate(heads[:DN_H], axis=1)
    k = jnp.concatenate(heads[DN_H:], axis=1)
    v = a[:, 2 * W:3 * W]
    lane = _iota(sd.shape, 1)
    beta = _sigmoid(sd) * valid
    g = -jnp.exp(alog_p) * jax.nn.softplus(sd + dt_p) * valid
    bg = jnp.where(lane < DN_H, beta, jnp.where(lane < 2 * DN_H, g, 0.0))
    return q, k, v, bg


def _conv_fwd(ext_ref, cw_ref, TM):
    y = cw_ref[0:1, :] * ext_ref[8 - (CONV_K - 1):8 - (CONV_K - 1) + TM, :]
    for i in range(1, CONV_K):
        o = 8 - (CONV_K - 1) + i
        y = y + cw_ref[i:i + 1, :] * ext_ref[o:o + TM, :]
    return y


def _dn_prep(dn, sd, cw, alog_p, dt_p):
    L, W3 = dn.shape
    TM = _pick(L, ROW_TILES)
    W = DN_H * DN_D

    def body(dn_ref, halo_ref, sd_ref, cw_ref, al_ref, dt_ref, q_ref, k_ref, v_ref, bg_ref, ext):
        i = pl.program_id(0)
        ext[0:8, :] = jnp.where(i == 0, 0.0, halo_ref[...])
        ext[8:, :] = dn_ref[...]
        y = _conv_fwd(ext, cw_ref, TM)
        valid = ((i * TM + _iota((TM, 1), 0)) >= N_PAD).astype(f32)
        q, k, v, bg = _dn_post(y, sd_ref[...], al_ref[...], dt_ref[...], valid)
        q_ref[...], k_ref[...], v_ref[...], bg_ref[...] = q, k, v, bg

    row = lambda wd: pl.BlockSpec((TM, wd), lambda i: (i, 0))
    vec = pl.BlockSpec((1, HP), lambda i: (0, 0))
    return pl.pallas_call(
        body, name="dn_prep", grid=(L // TM,),
        in_specs=[row(W3), pl.BlockSpec((8, W3), lambda i: (jnp.maximum(i * (TM // 8) - 1, 0), 0)), row(HP),
                  pl.BlockSpec((CONV_K, W3), lambda i: (0, 0)), vec, vec],
        out_specs=[row(W), row(W), row(W), row(HP)],
        out_shape=[SDS((L, W), f32)] * 3 + [SDS((L, HP), f32)],
        scratch_shapes=[pltpu.VMEM((TM + 8, W3), f32)], compiler_params=_cp())(dn, dn, sd, cw, alog_p, dt_p)


def _dn_prep_bwd(dn, sd, cw, alog_p, dt_p, dq, dk, dv, dbg):
    L, W3 = dn.shape
    TM = _pick(L, ROW_TILES)
    NT = L // TM
    W = DN_H * DN_D

    def body(dn_ref, halo_ref, sd_ref, cw_ref, al_ref, dt_ref, dq_ref, dk_ref, dv_ref, dbg_ref,
             ddn_ref, dsd_ref, dcw_ref, dp_ref, ext, dyp, carry):
        i = pl.program_id(0)
        t = NT - 1 - i

        @pl.when(i == 0)
        def _():
            carry[...] = jnp.zeros_like(carry)
            dcw_ref[...] = jnp.zeros_like(dcw_ref)
            dp_ref[...] = jnp.zeros_like(dp_ref)
            dyp[...] = jnp.zeros_like(dyp)

        ext[0:8, :] = jnp.where(t == 0, 0.0, halo_ref[...])
        ext[8:, :] = dn_ref[...]
        y = _conv_fwd(ext, cw_ref, TM)
        valid = ((t * TM + _iota((TM, 1), 0)) >= N_PAD).astype(f32)
        _, vjp = jax.vjp(functools.partial(_dn_post, valid=valid), y, sd_ref[...], al_ref[...], dt_ref[...])
        dy, dsd, dal, ddt = vjp((dq_ref[...], dk_ref[...], dv_ref[...], dbg_ref[...]))
        dsd_ref[...] = _b(dsd)
        dp_ref[0:1, :] += dal
        dp_ref[1:2, :] += ddt
        dyp[8:8 + TM, :] = dy
        o0 = CONV_K - 1
        dext = cw_ref[0:1, :] * dyp[o0:o0 + TM + 8, :]
        for k in range(1, CONV_K):
            dext = dext + cw_ref[k:k + 1, :] * dyp[o0 - k:o0 - k + TM + 8, :]
        for k in range(CONV_K):
            o = 8 - (CONV_K - 1) + k
            dcw_ref[k:k + 1, :] += jnp.sum(dy * ext[o:o + TM, :], axis=0, keepdims=True)
        ddn_ref[...] = _b(jnp.concatenate([dext[8:TM, :], dext[TM:TM + 8, :] + carry[...]], axis=0))
        carry[...] = dext[0:8, :]

    row = lambda wd: pl.BlockSpec((TM, wd), lambda i: (NT - 1 - i, 0))
    vec = pl.BlockSpec((1, HP), lambda i: (0, 0))
    return pl.pallas_call(
        body, name="dn_prep_bwd", grid=(NT,),
        in_specs=[row(W3), pl.BlockSpec((8, W3), lambda i: (jnp.maximum((NT - 1 - i) * (TM // 8) - 1, 0), 0)), row(HP),
                  pl.BlockSpec((CONV_K, W3), lambda i: (0, 0)), vec, vec, row(W), row(W), row(W), row(HP)],
        out_specs=[row(W3), row(HP), pl.BlockSpec((8, W3), lambda i: (0, 0)), pl.BlockSpec((8, HP), lambda i: (0, 0))],
        out_shape=[SDS((L, W3), bf16), SDS((L, HP), bf16), SDS((8, W3), f32), SDS((8, HP), f32)],
        scratch_shapes=[pltpu.VMEM((TM + 8, W3), f32), pltpu.VMEM((TM + 16, W3), f32), pltpu.VMEM((8, W3), f32)],
        compiler_params=_cp())(dn, dn, sd, cw, alog_p, dt_p, dq, dk, dv, dbg)


def _dn_chunk(q, k, v, bg, S):
    C = DN_C
    row, col = _iota((C, C), 0), _iota((C, C), 1)
    tri = row >= col
    eye = (row == col).astype(f32)
    G = _dot_hi(tri.astype(f32), bg)
    GT = lax.dot_general(G, eye, (((0,), (0,)), ((), ())), preferred_element_type=f32, precision=HI)
    lane = _iota((C, HP), 1)
    rowt = _iota((HP, C), 0)
    last = _iota((C, 1), 0) == C - 1
    outs, states = [], []
    for h in range(DN_H):
        beta = jnp.sum(jnp.where(lane == h, bg, 0.0), axis=1, keepdims=True)
        gcol = jnp.sum(jnp.where(lane == DN_H + h, G, 0.0), axis=1, keepdims=True)
        grow = jnp.sum(jnp.where(rowt == DN_H + h, GT, 0.0), axis=0, keepdims=True)
        glast = jnp.sum(jnp.where(last, gcol, 0.0), axis=0, keepdims=True)
        decay = jnp.exp(jnp.where(tri, gcol - grow, NEG))
        qh, kh, vh = (t[:, h * DN_D:(h + 1) * DN_D] for t in (q, k, v))
        kb = kh * beta
        low = jnp.where(row > col, _dot_nt(_b(kb), _b(kh)) * decay, 0.0)
        X, P = eye - low, low
        for _ in range(5):
            P = _dot_hi(P, P)
            X = X + _dot_hi(X, P)
        eg = jnp.exp(gcol)
        u = _dot_hi(X, vh * beta)
        w = _dot_hi(X, kb * eg)
        attn = _dot_nt(_b(qh), _b(kh)) * decay
        Sh = S[h]
        vnew = u - _dot(_b(w), _b(Sh))
        outs.append(_dot(_b(qh * eg), _b(Sh)) + _dot(_b(attn), _b(vnew)))
        states.append(Sh * jnp.exp(glast) + _dot_tn(_b(kh * jnp.exp(glast - gcol)), _b(vnew)))
    return jnp.concatenate(outs, axis=1), tuple(states)


def _dn_fwd(q, k, v, bg):
    L, W = q.shape
    NC = L // DN_C

    def body(q_ref, k_ref, v_ref, bg_ref, o_ref, s_ref, S):
        @pl.when(pl.program_id(0) == 0)
        def _():
            S[...] = jnp.zeros_like(S)

        s_in = tuple(S[h] for h in range(DN_H))
        for h in range(DN_H):
            s_ref[0, h] = s_in[h]
        o, s_out = _dn_chunk(q_ref[...], k_ref[...], v_ref[...], bg_ref[...], s_in)
        o_ref[...] = o
        for h in range(DN_H):
            S[h] = s_out[h]

    row = lambda wd: pl.BlockSpec((DN_C, wd), lambda n: (n, 0))
    return pl.pallas_call(
        body, name="dn_fwd", grid=(NC,),
        in_specs=[row(W), row(W), row(W), row(HP)],
        out_specs=[row(W), pl.BlockSpec((1, DN_H, DN_D, DN_D), lambda n: (n, 0, 0, 0))],
        out_shape=[SDS((L, W), f32), SDS((NC, DN_H, DN_D, DN_D), f32)],
        scratch_shapes=[pltpu.VMEM((DN_H, DN_D, DN_D), f32)], compiler_params=_cp())(q, k, v, bg)


def _dn_bwd(q, k, v, bg, states, do):
    L, W = q.shape
    NC = L // DN_C

    def body(q_ref, k_ref, v_ref, bg_ref, s_ref, do_ref, dq_ref, dk_ref, dv_ref, dbg_ref, dS):
        @pl.when(pl.program_id(0) == 0)
        def _():
            dS[...] = jnp.zeros_like(dS)

        s_in = tuple(s_ref[0, h] for h in range(DN_H))
        _, vjp = jax.vjp(_dn_chunk, q_ref[...], k_ref[...], v_ref[...], bg_ref[...], s_in)
        dq, dk, dv, dbg, ds = vjp((do_ref[...], tuple(dS[h] for h in range(DN_H))))
        dq_ref[...], dk_ref[...], dv_ref[...], dbg_ref[...] = dq, dk, dv, dbg
        for h in range(DN_H):
            dS[h] = ds[h]

    row = lambda wd: pl.BlockSpec((DN_C, wd), lambda n: (NC - 1 - n, 0))
    return pl.pallas_call(
        body, name="dn_bwd", grid=(NC,),
        in_specs=[row(W), row(W), row(W), row(HP), pl.BlockSpec((1, DN_H, DN_D, DN_D), lambda n: (NC - 1 - n, 0, 0, 0)), row(W)],
        out_specs=[row(W), row(W), row(W), row(HP)],
        out_shape=[SDS((L, W), f32)] * 3 + [SDS((L, HP), f32)],
        scratch_shapes=[pltpu.VMEM((DN_H, DN_D, DN_D), f32)], compiler_params=_cp())(q, k, v, bg, states, do)


def _dn_normgate(oraw, dz, wn):
    outs = []
    for h in range(DN_H):
        sl = slice(h * DN_D, (h + 1) * DN_D)
        z = dz[:, sl]
        outs.append(_rms(oraw[:, sl], wn) * (z * _sigmoid(z)))
    return jnp.concatenate(outs, axis=1)


def _mix_fwd(op, oraw, dz, ga, gb, h0, wn, wbf, wbd, wo):
    L, D = h0.shape
    TM = _pick(L, ROW_TILES)

    def body(op_ref, or_ref, dz_ref, ga_ref, gb_ref, h0_ref, wn_ref, wbf_ref, wbd_ref, wo_ref, h1_ref):
        pf = _dot(_b(op_ref[...]), wbf_ref[...])
        pd = _dot(_b(_dn_normgate(or_ref[...], dz_ref[...], wn_ref[...])), wbd_ref[...])
        y = _sigmoid(ga_ref[...]) * pf + _sigmoid(gb_ref[...]) * pd
        h1_ref[...] = h0_ref[...] + _dot(_b(y), wo_ref[...])

    row = lambda wd: pl.BlockSpec((TM, wd), lambda i: (i, 0))
    full = lambda a: pl.BlockSpec(a.shape, lambda i: (0, 0))
    return pl.pallas_call(
        body, name="mix_fwd", grid=(L // TM,),
        in_specs=[row(op.shape[1]), row(oraw.shape[1]), row(dz.shape[1]), row(D), row(D), row(D), full(wn), full(wbf), full(wbd), full(wo)],
        out_specs=row(D), out_shape=SDS((L, D), f32), compiler_params=_cp())(op, oraw, dz, ga, gb, h0, wn, wbf, wbd, wo)


def _mix_bwd(dh1, op, oraw, dz, ga, gb, wn, wbf, wbd, wo):
    L, D = dh1.shape
    TM = _pick(L, ROW_TILES)
    WF, WD = op.shape[1], oraw.shape[1]

    def body(dh1_ref, op_ref, or_ref, dz_ref, ga_ref, gb_ref, wn_ref, wbf_ref, wbd_ref, wo_ref,
             dop_ref, dor_ref, ddz_ref, dga_ref, dgb_ref, af_ref, ad_ref, dpf_ref, dpd_ref, y_ref, dmix_ref, acc_ref):
        @pl.when(pl.program_id(0) == 0)
        def _():
            acc_ref[...] = jnp.zeros_like(acc_ref)

        af = _b(op_ref[...])
        ad, vjp = jax.vjp(_dn_normgate, or_ref[...], dz_ref[...], wn_ref[...])
        adb = _b(ad)
        pf, pd = _dot(af, wbf_ref[...]), _dot(adb, wbd_ref[...])
        sa, sb = _sigmoid(ga_ref[...]), _sigmoid(gb_ref[...])
        dmix = _b(dh1_ref[...])
        dy = _dot_nt(dmix, wo_ref[...])
        dpf, dpd = _b(dy * sa), _b(dy * sb)
        dor, ddz, dwn = vjp(_dot_nt(dpd, wbd_ref[...]))
        dop_ref[...] = _dot_nt(dpf, wbf_ref[...])
        dor_ref[...] = dor
        ddz_ref[...] = _b(ddz)
        dga_ref[...] = _b(dy * pf * sa * (1.0 - sa))
        dgb_ref[...] = _b(dy * pd * sb * (1.0 - sb))
        af_ref[...], ad_ref[...], dpf_ref[...], dpd_ref[...] = af, adb, dpf, dpd
        y_ref[...] = _b(sa * pf + sb * pd)
        dmix_ref[...] = dmix
        acc_ref[0:1, :] += dwn

    row = lambda wd: pl.BlockSpec((TM, wd), lambda i: (i, 0))
    full = lambda a: pl.BlockSpec(a.shape, lambda i: (0, 0))
    return pl.pallas_call(
        body, name="mix_bwd", grid=(L // TM,),
        in_specs=[row(D), row(WF), row(WD), row(WD), row(D), row(D), full(wn), full(wbf), full(wbd), full(wo)],
        out_specs=[row(WF), row(WD), row(WD), row(D), row(D), row(WF), row(WD), row(D), row(D), row(D), row(D),
                   pl.BlockSpec((8, HP), lambda i: (0, 0))],
        out_shape=[SDS((L, WF), f32), SDS((L, WD), f32), SDS((L, WD), bf16), SDS((L, D), bf16), SDS((L, D), bf16),
                   SDS((L, WF), bf16), SDS((L, WD), bf16), SDS((L, D), bf16), SDS((L, D), bf16), SDS((L, D), bf16), SDS((L, D), bf16),
                   SDS((8, HP), f32)],
        compiler_params=_cp())(dh1, op, oraw, dz, ga, gb, wn, wbf, wbd, wo)


def _ffn_fwd_bwd(h1, tgt, w2, wf, wg, wu, wd):
    L, D = h1.shape
    F = wg.shape[1]
    TM = _pick(L, FFN_TILES)

    def body(h_ref, t_ref, w2_ref, wf_ref, wg_hbm, wu_hbm, wd_hbm,
             dh1_ref, xn_ref, dg_ref, du_ref, act_ref, dh2_ref, acc_ref, wg_v, wu_v, wd_v, sems):
        i = pl.program_id(0)
        _load_once([(wg_hbm, wg_v), (wu_hbm, wu_v), (wd_hbm, wd_v)], sems)

        @pl.when(i == 0)
        def _():
            acc_ref[...] = jnp.zeros_like(acc_ref)

        h1v = h_ref[...]
        xn2, vjp2 = jax.vjp(_rms, h1v, w2_ref[...])
        xb = _b(xn2)
        g, u = _dot(xb, wg_v[...]), _dot(xb, wu_v[...])
        sg = _sigmoid(g)
        ab = _b(g * sg * u)
        h2 = h1v + _dot(ab, wd_v[...])
        out, vjpf = jax.vjp(_rms, h2, wf_ref[...])
        valid = (i * TM + _iota((TM, 1), 0)) >= PREFIX
        diff = jnp.where(valid, out - t_ref[...], 0.0)
        loss = 0.5 * jnp.sum(jnp.sum(diff * diff, axis=1, keepdims=True), axis=0, keepdims=True) / D
        dh2, dwf = vjpf(diff * (1.0 / D))
        dh2b = _b(dh2)
        dact = _dot_nt(dh2b, wd_v[...])
        dgb = _b(dact * u * (sg * (1.0 + g * (1.0 - sg))))
        dub = _b(dact * (g * sg))
        dh1n, dw2 = vjp2(_dot_nt(dgb, wg_v[...]) + _dot_nt(dub, wu_v[...]))
        dh1_ref[...] = dh2 + dh1n
        xn_ref[...], dg_ref[...], du_ref[...], act_ref[...], dh2_ref[...] = xb, dgb, dub, ab, dh2b
        acc_ref[0:1, :] += dw2
        acc_ref[1:2, :] += dwf
        acc_ref[2:3, :] += jnp.broadcast_to(loss, (1, D))

    row = lambda wd_: pl.BlockSpec((TM, wd_), lambda i: (i, 0))
    vec = pl.BlockSpec((1, D), lambda i: (0, 0))
    anyspec = pl.BlockSpec(memory_space=pl.ANY)
    return pl.pallas_call(
        body, name="ffn_fwd_bwd", grid=(L // TM,),
        in_specs=[row(D), row(D), vec, vec, anyspec, anyspec, anyspec],
        out_specs=[row(D), row(D), row(F), row(F), row(F), row(D), pl.BlockSpec((8, D), lambda i: (0, 0))],
        out_shape=[SDS((L, D), f32), SDS((L, D), bf16), SDS((L, F), bf16), SDS((L, F), bf16), SDS((L, F), bf16), SDS((L, D), bf16),
                   SDS((8, D), f32)],
        scratch_shapes=[pltpu.VMEM((D, F), bf16), pltpu.VMEM((D, F), bf16), pltpu.VMEM((F, D), bf16), pltpu.SemaphoreType.DMA((3,))],
        compiler_params=_cp())(h1, tgt, w2, wf, wg, wu, wd)


def _pad_lanes(v, n=HP):
    return jnp.pad(v.astype(f32), ((0, 0), (0, n - v.shape[1])))


def _pack_w_in(w_full):
    D = w_full.shape[0]
    FW, DW = FOX_H * FOX_D, DN_H * DN_D
    o = 0
    parts = {}
    for name, wd in (("fq", FW), ("fk", FW), ("fv", FW), ("fl", FOX_H), ("dn", 3 * DW), ("ba", 2 * DN_H), ("dz", DW), ("ga", D), ("gb", D)):
        parts[name] = w_full[:, o:o + wd]
        o += wd
    assert o == w_full.shape[1]
    heads = lambda w: jnp.pad(w.reshape(D, FOX_H, FOX_D), ((0, 0), (0, 0), (0, HP - FOX_D))).reshape(D, FOX_H * HP)
    small = lambda w: jnp.pad(w, ((0, 0), (0, HP - w.shape[1])))
    return jnp.concatenate([heads(parts["fq"]), heads(parts["fk"]), heads(parts["fv"]), parts["dn"], parts["dz"], parts["ga"], parts["gb"],
                            small(parts["fl"]), small(parts["ba"])], axis=1)


def _unpack_w_in(gp, d_model):
    D = gp.shape[0]
    FW, DW = FOX_H * FOX_D, DN_H * DN_D
    segs, o = {}, 0
    for name, wd, _ in _seg_layout(d_model):
        segs[name] = gp[:, o:o + wd]
        o += wd
    heads = lambda g: g.reshape(D, FOX_H, HP)[:, :, :FOX_D].reshape(D, FW)
    return jnp.concatenate([heads(segs["fq"]), heads(segs["fk"]), heads(segs["fv"]), segs["sf"][:, :FOX_H], segs["dn"],
                            segs["sd"][:, :2 * DN_H], segs["dz"], segs["ga"], segs["gb"]], axis=1)


def _local_step(x, tgt, meta, w1, w_in, fbias, cw, alog, dtb, wn, wbf, wbd, wo, w2, wg, wu, wd, wf):
    T, D = x.shape
    h0 = jnp.concatenate([jnp.zeros((N_PAD, D), f32), meta, x], axis=0)
    tgt_p = jnp.concatenate([jnp.zeros((PREFIX, D), f32), tgt], axis=0)
    wp = _pack_w_in(w_in)
    wbf_p = jnp.pad(wbf.reshape(FOX_H, FOX_D, D), ((0, 0), (0, HP - FOX_D), (0, 0))).reshape(FOX_H * HP, D)
    bias_p, alog_p, dt_p = _pad_lanes(fbias), _pad_lanes(jnp.pad(alog, ((0, 0), (DN_H, 0)))), _pad_lanes(jnp.pad(dtb, ((0, 0), (DN_H, 0))))

    xn, fq, fk, fv, dn, dz, ga, gb, sf, sd = _in_proj(h0, w1, wp)
    qa, ka = _fox_prep(fq, fk, sf, bias_p)
    op = _fox_fwd(qa, ka, fv)
    qn, kn, vn, bg = _dn_prep(dn, sd, cw, alog_p, dt_p)
    oraw, states = _dn_fwd(qn, kn, vn, bg)
    h1 = _mix_fwd(op, oraw, dz, ga, gb, h0, wn, wbf_p, wbd, wo)

    dh1, xn2, dgate, dup, act, dh2, acc_f = _ffn_fwd_bwd(h1, tgt_p, w2, wf, wg, wu, wd)
    g_wg, g_wu, g_wd = _matmul_tn(xn2, dgate, "dw_ffn_gate"), _matmul_tn(xn2, dup, "dw_ffn_up"), _matmul_tn(act, dh2, "dw_ffn_down")

    dop, dor, ddz, dga, dgb, af, ad, dpf, dpd, yb, dmix, acc_m = _mix_bwd(dh1, op, oraw, dz, ga, gb, wn, wbf_p, wbd, wo)
    g_wbf = _matmul_tn(af, dpf, "dw_branch_fox").reshape(FOX_H, HP, D)[:, :FOX_D].reshape(FOX_H * FOX_D, D)
    g_wbd, g_wo = _matmul_tn(ad, dpd, "dw_branch_dn"), _matmul_tn(yb, dmix, "dw_out")

    dqn, dkn, dvn, dbg = _dn_bwd(qn, kn, vn, bg, states, dor)
    ddn, dsd, acc_cw, acc_p = _dn_prep_bwd(dn, sd, cw, alog_p, dt_p, dqn, dkn, dvn, dbg)
    dqa, dka, dfv = _fox_bwd(qa, ka, fv, op, dop)
    dfq, dfk, dsf, acc_b = _fox_prep_bwd(dqa, dka, sf, bias_p)

    dproj = jnp.concatenate([dfq, dfk, dfv, ddn, ddz, dga, dgb, dsf, dsd], axis=1)
    g_wp = _matmul_tn(xn, dproj, "dw_in")
    dh0, acc_1 = _in_proj_bwd(dproj, wp, h0, w1, dh1)

    small = dict(loss=acc_f[2, 0:1], mix_norm_w=acc_1[0], fox_forget_bias=acc_b[0, :FOX_H], dn_a_log=acc_p[0, DN_H:2 * DN_H],
                 dn_dt_bias=acc_p[1, DN_H:2 * DN_H], dn_out_norm_w=acc_m[0], ffn_norm_w=acc_f[0], final_norm_w=acc_f[1],
                 meta_tokens=dh0[N_PAD:PREFIX].reshape(-1), dn_conv_w=acc_cw[:CONV_K].reshape(-1))
    big = dict(w_in=_unpack_w_in(g_wp, D), w_branch_fox=g_wbf, w_branch_dn=g_wbd, w_out=g_wo, w_ffn_gate=g_wg, w_ffn_up=g_wu, w_ffn_down=g_wd)
    return dh0[PREFIX:], small, big


def _mesh_pos():
    x, y, c = lax.axis_index("x"), lax.axis_index("y"), lax.axis_index("c")
    return x, y, c, 4 * x + 2 * y + c


def _peer(x, y, c, m):
    flip = lambda v, on: 1 - v if on else v
    px, py, pc = flip(x, m & 4), flip(y, m & 2), flip(c, m & 1)
    return (px, py, pc), 4 * px + 2 * py + pc


def _exchange(arrays, name, gather):
    n = len(arrays)
    shapes = [a.shape if gather else a.shape[1:] for a in arrays]

    def body(*refs):
        ins, outs, (send_sems, recv_sems, loc_sems) = refs[:n], refs[n:2 * n], refs[2 * n:]
        x, y, c, me = _mesh_pos()
        src = lambda a, pid: ins[a] if gather else ins[a].at[pid]
        local = [pltpu.make_async_copy(src(a, me), outs[a].at[me], loc_sems.at[a]) for a in range(n)]
        for cp in local:
            cp.start()
        sends = []
        for m in range(1, N_DEV):
            peer, pid = _peer(x, y, c, m)
            for a in range(n):
                cp = pltpu.make_async_remote_copy(src_ref=src(a, pid), dst_ref=outs[a].at[me], send_sem=send_sems.at[a, m - 1],
                                                  recv_sem=recv_sems.at[a, m - 1], device_id=peer, device_id_type=MESH)
                cp.start()
                sends.append(cp)
        for m in range(1, N_DEV):
            peer, pid = _peer(x, y, c, m)
            for a in range(n):
                pltpu.make_async_remote_copy(src_ref=src(a, pid), dst_ref=outs[a].at[pid], send_sem=send_sems.at[a, m - 1],
                                             recv_sem=recv_sems.at[a, m - 1], device_id=peer, device_id_type=MESH).wait_recv()
        for cp in sends:
            cp.wait_send()
        for cp in local:
            cp.wait()

    anyspec = pl.BlockSpec(memory_space=pl.ANY)
    return pl.pallas_call(
        body, name=name, in_specs=[anyspec] * n, out_specs=[anyspec] * n,
        out_shape=[SDS((N_DEV,) + tuple(s), a.dtype) for s, a in zip(shapes, arrays)],
        scratch_shapes=[pltpu.SemaphoreType.DMA((n, N_DEV - 1)), pltpu.SemaphoreType.DMA((n, N_DEV - 1)), pltpu.SemaphoreType.DMA((n,))],
        )(*arrays)


def _all_reduce_small(v):
    R = v.shape[0]

    def body(v_ref, o_ref, gath, send_sems, recv_sems):
        x, y, c, me = _mesh_pos()
        gath[me] = v_ref[...]
        sends = []
        for m in range(1, N_DEV):
            peer, _ = _peer(x, y, c, m)
            cp = pltpu.make_async_remote_copy(src_ref=v_ref, dst_ref=gath.at[me], send_sem=send_sems.at[m - 1],
                                              recv_sem=recv_sems.at[m - 1], device_id=peer, device_id_type=MESH)
            cp.start()
            sends.append(cp)
        for m in range(1, N_DEV):
            peer, pid = _peer(x, y, c, m)
            pltpu.make_async_remote_copy(src_ref=v_ref, dst_ref=gath.at[pid], send_sem=send_sems.at[m - 1],
                                         recv_sem=recv_sems.at[m - 1], device_id=peer, device_id_type=MESH).wait_recv()
        for cp in sends:
            cp.wait_send()
        tot = gath[0]
        for d in range(1, N_DEV):
            tot = tot + gath[d]
        o_ref[...] = tot

    vm = pl.BlockSpec(memory_space=pltpu.VMEM)
    return pl.pallas_call(
        body, name="all_reduce_small", in_specs=[vm], out_specs=vm, out_shape=SDS((R, HP), f32),
        scratch_shapes=[pltpu.VMEM((N_DEV, R, HP), f32), pltpu.SemaphoreType.DMA((N_DEV - 1,)), pltpu.SemaphoreType.DMA((N_DEV - 1,))],
        )(v)


def _adamw_math(w, g, m, v):
    m = ADAM_B1 * m + (1.0 - ADAM_B1) * g
    v = ADAM_B2 * v + (1.0 - ADAM_B2) * (g * g)
    m_hat = m / (1.0 - ADAM_B1 ** ADAM_STEP)
    v_hat = v / (1.0 - ADAM_B2 ** ADAM_STEP)
    return -ADAM_LR * (m_hat / (jnp.sqrt(v_hat) + ADAM_EPS) + ADAM_WD * w), m, v


def _adamw(g, w, m, v, name):
    R, Cc = w.shape
    TR = R if R <= 512 else _pick(R, (256, 128))
    slabs = g.ndim == 3

    def body(g_ref, w_ref, m_ref, v_ref, go_ref, d_ref, mo_ref, vo_ref):
        if slabs:
            gs = g_ref[0].astype(f32)
            for k in range(1, N_DEV):
                gs = gs + g_ref[k].astype(f32)
        else:
            gs = g_ref[...]
        d, mn, vn = _adamw_math(w_ref[...], gs, m_ref[...], v_ref[...])
        go_ref[...], d_ref[...], mo_ref[...], vo_ref[...] = gs, d, mn, vn

    blk = pl.BlockSpec((TR, Cc), lambda i: (i, 0))
    gblk = pl.BlockSpec((N_DEV, TR, Cc), lambda i: (0, i, 0)) if slabs else blk
    return pl.pallas_call(
        body, name=name, grid=(R // TR,), in_specs=[gblk, blk, blk, blk], out_specs=[blk] * 4,
        out_shape=[SDS((R, Cc), f32)] * 4, compiler_params=_cp())(g, w, m, v)


WEIGHTS = ("meta_tokens", "mix_norm_w", "w_in", "fox_forget_bias", "dn_conv_w", "dn_a_log", "dn_dt_bias", "dn_out_norm_w",
           "w_branch_fox", "w_branch_dn", "w_out", "ffn_norm_w", "w_ffn_gate", "w_ffn_up", "w_ffn_down", "final_norm_w")
COL_SHARDED = ("w_in", "w_branch_fox", "w_branch_dn", "w_ffn_gate", "w_ffn_up")
ROW_SHARDED = ("w_out", "w_ffn_down")
BIG = COL_SHARDED + ROW_SHARDED
SMALL = tuple(n for n in WEIGHTS if n not in BIG)


def _to_slabs(name, g):
    r, c = g.shape
    if name in COL_SHARDED:
        return _b(g.reshape(r, N_DEV, c // N_DEV).transpose(1, 0, 2))
    return _b(g.reshape(N_DEV, r // N_DEV, c))


def _from_slabs(name, s):
    n, r, c = s.shape
    if name in COL_SHARDED:
        return s.transpose(1, 0, 2).reshape(r, n * c)
    return s.reshape(n * r, c)


def kernel(x, meta_tokens, mix_norm_w, w_in, fox_forget_bias, dn_conv_w, dn_a_log, dn_dt_bias, dn_out_norm_w, w_branch_fox, w_branch_dn, w_out, ffn_norm_w, w_ffn_gate, w_ffn_up, w_ffn_down, final_norm_w, loss_target, m_meta_tokens, m_mix_norm_w, m_w_in, m_fox_forget_bias, m_dn_conv_w, m_dn_a_log, m_dn_dt_bias, m_dn_out_norm_w, m_w_branch_fox, m_w_branch_dn, m_w_out, m_ffn_norm_w, m_w_ffn_gate, m_w_ffn_up, m_w_ffn_down, m_final_norm_w, v_meta_tokens, v_mix_norm_w, v_w_in, v_fox_forget_bias, v_dn_conv_w, v_dn_a_log, v_dn_dt_bias, v_dn_out_norm_w, v_w_branch_fox, v_w_branch_dn, v_w_out, v_ffn_norm_w, v_w_ffn_gate, v_w_ffn_up, v_w_ffn_down, v_final_norm_w):
    w = dict(meta_tokens=meta_tokens, mix_norm_w=mix_norm_w, w_in=w_in, fox_forget_bias=fox_forget_bias, dn_conv_w=dn_conv_w, dn_a_log=dn_a_log, dn_dt_bias=dn_dt_bias, dn_out_norm_w=dn_out_norm_w, w_branch_fox=w_branch_fox, w_branch_dn=w_branch_dn, w_out=w_out, ffn_norm_w=ffn_norm_w, w_ffn_gate=w_ffn_gate, w_ffn_up=w_ffn_up, w_ffn_down=w_ffn_down, final_norm_w=final_norm_w)
    mom = dict(meta_tokens=m_meta_tokens, mix_norm_w=m_mix_norm_w, w_in=m_w_in, fox_forget_bias=m_fox_forget_bias, dn_conv_w=m_dn_conv_w, dn_a_log=m_dn_a_log, dn_dt_bias=m_dn_dt_bias, dn_out_norm_w=m_dn_out_norm_w, w_branch_fox=m_w_branch_fox, w_branch_dn=m_w_branch_dn, w_out=m_w_out, ffn_norm_w=m_ffn_norm_w, w_ffn_gate=m_w_ffn_gate, w_ffn_up=m_w_ffn_up, w_ffn_down=m_w_ffn_down, final_norm_w=m_final_norm_w)
    var = dict(meta_tokens=v_meta_tokens, mix_norm_w=v_mix_norm_w, w_in=v_w_in, fox_forget_bias=v_fox_forget_bias, dn_conv_w=v_dn_conv_w, dn_a_log=v_dn_a_log, dn_dt_bias=v_dn_dt_bias, dn_out_norm_w=v_dn_out_norm_w, w_branch_fox=v_w_branch_fox, w_branch_dn=v_w_branch_dn, w_out=v_w_out, ffn_norm_w=v_ffn_norm_w, w_ffn_gate=v_w_ffn_gate, w_ffn_up=v_w_ffn_up, w_ffn_down=v_w_ffn_down, final_norm_w=v_final_norm_w)
    two_d = lambda a: a.reshape(a.shape[-2:]) if a.ndim >= 2 else a.reshape(1, -1)
    me = 4 * lax.axis_index("x") + 2 * lax.axis_index("y") + lax.axis_index("c")

    gathered = _exchange([_b(two_d(w[n])) for n in BIG] + [two_d(w["meta_tokens"]), two_d(w["dn_conv_w"])], "all_gather_weights", gather=True)
    full = {n: _from_slabs(n, s) for n, s in zip(BIG, gathered)}
    meta = gathered[len(BIG)].transpose(1, 0, 2).reshape(N_META, -1)
    cw = gathered[len(BIG) + 1].transpose(1, 0, 2).reshape(CONV_K, -1)

    gx, g_small, g_big = _local_step(
        x[0], loss_target[0], meta, two_d(w["mix_norm_w"]), full["w_in"], two_d(w["fox_forget_bias"]), cw, two_d(w["dn_a_log"]),
        two_d(w["dn_dt_bias"]), two_d(w["dn_out_norm_w"]), full["w_branch_fox"], full["w_branch_dn"], full["w_out"], two_d(w["ffn_norm_w"]),
        full["w_ffn_gate"], full["w_ffn_up"], full["w_ffn_down"], two_d(w["final_norm_w"]))

    recv = dict(zip(BIG, _exchange([_to_slabs(n, g_big[n]) for n in BIG], "exchange_weight_grads", gather=False)))
    order = ("loss",) + SMALL
    flat = jnp.concatenate([g_small[n].reshape(-1) for n in order])
    rows = -(-flat.shape[0] // (8 * HP)) * 8
    tot = _all_reduce_small(jnp.pad(flat, (0, rows * HP - flat.shape[0])).reshape(rows, HP)).reshape(-1)
    summed, o = {}, 0
    for n in order:
        k = g_small[n].shape[0]
        summed[n] = tot[o:o + k]
        o += k
    loss = summed["loss"][0]
    d_model = x.shape[-1]
    mcols, ccols = d_model // N_DEV, dn_conv_w.shape[-1]
    summed["meta_tokens"] = lax.dynamic_slice(summed["meta_tokens"].reshape(N_META, d_model), (0, me * mcols), (N_META, mcols)).reshape(-1)
    summed["dn_conv_w"] = lax.dynamic_slice(summed["dn_conv_w"].reshape(CONV_K, ccols * N_DEV), (0, me * ccols), (CONV_K, ccols)).reshape(-1)

    res = {}
    for n in BIG:
        res[n] = [r.reshape(w[n].shape) for r in _adamw(recv[n], two_d(w[n]), two_d(mom[n]), two_d(var[n]), "adamw_" + n)]
    sizes = [summed[n].shape[0] for n in SMALL]
    srows = -(-sum(sizes) // (8 * HP)) * 8
    pack = lambda d: jnp.pad(jnp.concatenate([d[n].reshape(-1) for n in SMALL]), (0, srows * HP - sum(sizes))).reshape(srows, HP)
    sres = _adamw(pack(summed), pack(w), pack(mom), pack(var), "adamw_small")
    o = 0
    for n, k in zip(SMALL, sizes):
        res[n] = [r.reshape(-1)[o:o + k].reshape(w[n].shape) for r in sres]
        o += k
    return (loss, gx[None], *[res[n][0] for n in WEIGHTS], *[res[n][1] for n in WEIGHTS], *[res[n][2] for n in WEIGHTS], *[res[n][3] for n in WEIGHTS])
```

```python
import functools

import jax
import jax.numpy as jnp
from jax import lax
from jax.experimental import pallas as pl
from jax.experimental.pallas import tpu as pltpu

f32, bf16 = jnp.float32, jnp.bfloat16
HI = lax.Precision.HIGHEST
MESH = pl.DeviceIdType.MESH
SDS = jax.ShapeDtypeStruct

N_DEV = 8
N_META = 16
PREFIX = 128
N_PAD = PREFIX - N_META
FOX_H, FOX_D = 8, 64
DN_H, DN_D = 4, 128
DN_C = 64
CONV_K = 4
HP = 128
EPS = 1e-6
NEG = -1e30
C_Q0, C_K0 = 64, 67
LSE_COL = 64

ADAM_LR, ADAM_B1, ADAM_B2, ADAM_EPS, ADAM_WD, ADAM_STEP = 0.001, 0.9, 0.999, 1e-08, 0.01, 10

VMEM_LIMIT_V7X = 56 * 1024 * 1024
ROW_TILES = (384, 128)
ATTN_TILES = (384, 128)
FFN_TILES = (192, 64)
COL_TILES = (512, 256, 128)
ROW_BLOCKS = (704, 512, 256, 128)
DN_INTRA_GROUP = (3, 2, 1)
DN_SCAN_GROUP = (6, 3, 2, 1)


def _pick(n, cands):
    for c in cands:
        if n % c == 0:
            return c
    raise ValueError(f"no tile of {cands} divides {n}")


def _cp(n_axes=1):
    return pltpu.CompilerParams(dimension_semantics=("arbitrary",) * n_axes, vmem_limit_bytes=VMEM_LIMIT_V7X)


def _b(x):
    return x.astype(bf16)


def _dot(a, b):
    return jnp.dot(a, b, preferred_element_type=f32)


def _dot_nt(a, b):
    return lax.dot_general(a, b, (((1,), (1,)), ((), ())), preferred_element_type=f32)


def _dot_tn(a, b):
    return lax.dot_general(a, b, (((0,), (0,)), ((), ())), preferred_element_type=f32)


def _dot_hi(a, b):
    return jnp.dot(a, b, preferred_element_type=f32, precision=HI)


def _iota(shape, dim):
    return lax.broadcasted_iota(jnp.int32, shape, dim)


def _rms(x, w):
    return x * lax.rsqrt(jnp.mean(x * x, axis=-1, keepdims=True) + EPS) * w


def _sigmoid(x):
    return jax.nn.sigmoid(x)


def _load_once(pairs, sems):
    @pl.when(pl.program_id(0) == 0)
    def _():
        cps = [pltpu.make_async_copy(src, dst, sems.at[k]) for k, (src, dst) in enumerate(pairs)]
        for cp in cps:
            cp.start()
        for cp in cps:
            cp.wait()


def _seg_layout(d_model):
    return (("fq", FOX_H * HP, bf16), ("fk", FOX_H * HP, bf16), ("fv", FOX_H * HP, bf16),
            ("dn", 3 * DN_H * DN_D, f32), ("dz", DN_H * DN_D, f32), ("ga", d_model, f32), ("gb", d_model, f32),
            ("sf", HP, f32), ("sd", HP, f32))


def _in_proj(h0, w1, wp):
    L, D = h0.shape
    NP = wp.shape[1]
    TM = _pick(L, ROW_TILES)
    segs = _seg_layout(D)
    offs, o = [], 0
    for _, wd, _ in segs:
        offs.append(o)
        o += wd
    assert o == NP

    def body(h_ref, w1_ref, wp_hbm, xn_ref, *rest):
        outs, (wp_v, sems) = rest[:len(segs)], rest[len(segs):]
        _load_once([(wp_hbm, wp_v)], sems)
        xn = _b(_rms(h_ref[...], w1_ref[...]))
        xn_ref[...] = xn
        for o_ref, off, (_, wd, _) in zip(outs, offs, segs):
            o_ref[...] = _dot(xn, wp_v[:, off:off + wd]).astype(o_ref.dtype)

    row = lambda wd: pl.BlockSpec((TM, wd), lambda i: (i, 0))
    return pl.pallas_call(
        body, name="in_proj", grid=(L // TM,),
        in_specs=[row(D), pl.BlockSpec((1, D), lambda i: (0, 0)), pl.BlockSpec(memory_space=pl.ANY)],
        out_specs=[row(D)] + [row(wd) for _, wd, _ in segs],
        out_shape=[SDS((L, D), bf16)] + [SDS((L, wd), dt) for _, wd, dt in segs],
        scratch_shapes=[pltpu.VMEM((D, NP), bf16), pltpu.SemaphoreType.DMA((1,))],
        compiler_params=_cp())(h0, w1, wp)


def _in_proj_bwd(dproj, wp, h0, w1, dh1):
    L, D = h0.shape
    NP = wp.shape[1]
    TM = _pick(L, ROW_TILES)

    def body(dp_ref, wp_hbm, h_ref, w1_ref, dh1_ref, dh0_ref, acc_ref, wp_v, sems):
        _load_once([(wp_hbm, wp_v)], sems)

        @pl.when(pl.program_id(0) == 0)
        def _():
            acc_ref[...] = jnp.zeros_like(acc_ref)

        dxn = _dot_nt(dp_ref[...], wp_v[...])
        _, vjp = jax.vjp(_rms, h_ref[...], w1_ref[...])
        dh0n, dw1 = vjp(dxn)
        dh0_ref[...] = dh1_ref[...] + dh0n
        acc_ref[0:1, :] += dw1

    row = lambda wd: pl.BlockSpec((TM, wd), lambda i: (i, 0))
    return pl.pallas_call(
        body, name="in_proj_bwd", grid=(L // TM,),
        in_specs=[row(NP), pl.BlockSpec(memory_space=pl.ANY), row(D), pl.BlockSpec((1, D), lambda i: (0, 0)), row(D)],
        out_specs=[row(D), pl.BlockSpec((8, D), lambda i: (0, 0))],
        out_shape=[SDS((L, D), f32), SDS((8, D), f32)],
        scratch_shapes=[pltpu.VMEM((D, NP), bf16), pltpu.SemaphoreType.DMA((1,))],
        compiler_params=_cp())(dproj, wp, h0, w1, dh1)


def _matmul_tn(a, b, name):
    L, M = a.shape
    N = b.shape[1]
    at = a.T

    def body(a_ref, b_ref, o_ref):
        o_ref[...] = _dot(a_ref[...], b_ref[...])

    if M <= N:
        bn = N if N <= COL_TILES[0] else _pick(N, COL_TILES)
        grid, a_spec = (N // bn,), pl.BlockSpec((M, L), lambda n: (0, 0))
        b_spec, o_spec = pl.BlockSpec((L, bn), lambda n: (0, n)), pl.BlockSpec((M, bn), lambda n: (0, n))
    else:
        bm = _pick(M, ROW_BLOCKS)
        grid, a_spec = (M // bm,), pl.BlockSpec((bm, L), lambda m: (m, 0))
        b_spec, o_spec = pl.BlockSpec((L, N), lambda m: (0, 0)), pl.BlockSpec((bm, N), lambda m: (m, 0))
    return pl.pallas_call(body, name=name, grid=grid, in_specs=[a_spec, b_spec], out_specs=o_spec,
                          out_shape=SDS((M, N), f32), compiler_params=_cp())(at, b)


def _fox_prep(fq, fk, sf, bias_p):
    L = fq.shape[0]
    T = HP
    NT = L // T

    def body(fq_ref, fk_ref, sf_ref, b_ref, qa_ref, ka_ref, carry):
        @pl.when(pl.program_id(0) == 0)
        def _():
            carry[...] = jnp.zeros_like(carry)

        lane, row = _iota((T, HP), 1), _iota((T, HP), 0)
        logf = jnp.where(lane < FOX_H, jax.nn.log_sigmoid(sf_ref[...] + b_ref[...]), 0.0)
        c = _dot_hi((row >= lane).astype(f32), logf) + carry[...]
        carry[...] = jnp.sum(jnp.where(row == T - 1, c, 0.0), axis=0, keepdims=True)
        ones_q = jnp.where((lane >= C_K0) & (lane < C_K0 + 3), 1.0, 0.0)
        ones_k = jnp.where((lane >= C_Q0) & (lane < C_Q0 + 3), 1.0, 0.0)
        for h in range(FOX_H):
            ch = jnp.broadcast_to(jnp.sum(jnp.where(lane == h, c, 0.0), axis=1, keepdims=True), (T, HP))
            c1 = _b(ch).astype(f32)
            c2 = _b(ch - c1).astype(f32)
            c3 = _b(ch - c1 - c2).astype(f32)
            cq = jnp.where(lane == C_Q0, c1, 0.0) + jnp.where(lane == C_Q0 + 1, c2, 0.0) + jnp.where(lane == C_Q0 + 2, c3, 0.0)
            ck = jnp.where(lane == C_K0, c1, 0.0) + jnp.where(lane == C_K0 + 1, c2, 0.0) + jnp.where(lane == C_K0 + 2, c3, 0.0)
            q = fq_ref[:, h * HP:(h + 1) * HP].astype(f32) * (FOX_D ** -0.5)
            k = fk_ref[:, h * HP:(h + 1) * HP].astype(f32)
            qa_ref[h] = _b(q + cq + ones_q)
            ka_ref[h] = _b(k + ones_k - ck)

    return pl.pallas_call(
        body, name="fox_prep", grid=(NT,),
        in_specs=[pl.BlockSpec((T, FOX_H * HP), lambda i: (i, 0)), pl.BlockSpec((T, FOX_H * HP), lambda i: (i, 0)),
                  pl.BlockSpec((T, HP), lambda i: (i, 0)), pl.BlockSpec((1, HP), lambda i: (0, 0))],
        out_specs=[pl.BlockSpec((FOX_H, T, HP), lambda i: (0, i, 0))] * 2,
        out_shape=[SDS((FOX_H, L, HP), bf16)] * 2,
        scratch_shapes=[pltpu.VMEM((1, HP), f32)], compiler_params=_cp())(fq, fk, sf, bias_p)


def _fox_prep_bwd(dqa, dka, sf, bias_p):
    L = sf.shape[0]
    T = HP
    NT = L // T
    rev = lambda i: (NT - 1 - i, 0)

    def body(dq_ref, dk_ref, sf_ref, b_ref, dfq_ref, dfk_ref, dsf_ref, db_ref, carry):
        @pl.when(pl.program_id(0) == 0)
        def _():
            carry[...] = jnp.zeros_like(carry)
            db_ref[...] = jnp.zeros_like(db_ref)

        dq, dk = dq_ref[...], dk_ref[...]
        dfq_ref[...] = _b(dq * (FOX_D ** -0.5))
        dfk_ref[...] = _b(dk)
        lane, row = _iota((T, HP), 1), _iota((T, HP), 0)
        dc = jnp.zeros((T, HP), f32)
        for h in range(FOX_H):
            col = jnp.sum(jnp.where(lane == C_Q0, dq[:, h * HP:(h + 1) * HP], 0.0)
                          - jnp.where(lane == C_K0, dk[:, h * HP:(h + 1) * HP], 0.0), axis=1, keepdims=True)
            dc = dc + jnp.where(lane == h, col, 0.0)
        dl = _dot_hi((row <= lane).astype(f32), dc) + carry[...]
        carry[...] = jnp.sum(jnp.where(row == 0, dl, 0.0), axis=0, keepdims=True)
        dx = jnp.where(lane < FOX_H, dl * _sigmoid(-(sf_ref[...] + b_ref[...])), 0.0)
        dsf_ref[...] = _b(dx)
        db_ref[0:1, :] += jnp.sum(dx, axis=0, keepdims=True)

    return pl.pallas_call(
        body, name="fox_prep_bwd", grid=(NT,),
        in_specs=[pl.BlockSpec((T, FOX_H * HP), rev), pl.BlockSpec((T, FOX_H * HP), rev),
                  pl.BlockSpec((T, HP), rev), pl.BlockSpec((1, HP), lambda i: (0, 0))],
        out_specs=[pl.BlockSpec((T, FOX_H * HP), rev), pl.BlockSpec((T, FOX_H * HP), rev),
                   pl.BlockSpec((T, HP), rev), pl.BlockSpec((8, HP), lambda i: (0, 0))],
        out_shape=[SDS((L, FOX_H * HP), bf16), SDS((L, FOX_H * HP), bf16), SDS((L, HP), bf16), SDS((8, HP), f32)],
        scratch_shapes=[pltpu.VMEM((1, HP), f32)], compiler_params=_cp())(dqa, dka, sf, bias_p)


def _fox_fwd(qa, ka, fv):
    L = qa.shape[1]
    TQ = TK = _pick(L, ATTN_TILES)

    def body(q_ref, k_ref, v_ref, o_ref):
        i = pl.program_id(1)
        q = q_ref[0]
        rowg = i * TQ + _iota((TQ, TK), 0)
        colb = _iota((TQ, TK), 1)

        def step(j, carry):
            m, l, acc = carry
            k0 = pl.multiple_of(j * TK, TK)
            kt = k_ref[0, pl.ds(k0, TK), :]
            vt = v_ref[pl.ds(k0, TK), :]
            colg = colb + j * TK
            s = jnp.where((colg <= rowg) & (colg >= N_PAD), _dot_nt(q, kt), NEG)
            m_new = jnp.maximum(m, jnp.max(s, axis=1, keepdims=True))
            p = jnp.exp(s - m_new)
            alpha = jnp.exp(m - m_new)
            l = alpha * l + jnp.sum(p, axis=1, keepdims=True)
            acc = alpha * acc + _dot(_b(p), vt)
            return m_new, l, acc

        m, l, acc = lax.fori_loop(0, i + 1, step, (jnp.full((TQ, 1), NEG, f32), jnp.zeros((TQ, 1), f32), jnp.zeros((TQ, HP), f32)))
        o_ref[...] = jnp.where(_iota((TQ, HP), 1) == LSE_COL, m + jnp.log(l), acc / l)

    return pl.pallas_call(
        body, name="fox_fwd", grid=(FOX_H, L // TQ),
        in_specs=[pl.BlockSpec((1, TQ, HP), lambda h, i: (h, i, 0)), pl.BlockSpec((1, L, HP), lambda h, i: (h, 0, 0)),
                  pl.BlockSpec((L, HP), lambda h, i: (0, h))],
        out_specs=pl.BlockSpec((TQ, HP), lambda h, i: (i, h)),
        out_shape=SDS((L, FOX_H * HP), f32), compiler_params=_cp(2))(qa, ka, fv)


def _fox_bwd(qa, ka, fv, op, dop):
    L = qa.shape[1]
    TQ = TK = _pick(L, ATTN_TILES)
    NQ = L // TQ

    def body(q_ref, k_ref, v_ref, o_ref, do_ref, dq_ref, dk_ref, dv_ref):
        j = pl.program_id(1)

        @pl.when(j == 0)
        def _():
            dq_ref[...] = jnp.zeros_like(dq_ref)

        kt, vt = k_ref[0], v_ref[...]
        colg = j * TK + _iota((TQ, TK), 1)
        rowb = _iota((TQ, TK), 0)
        lane = _iota((TQ, HP), 1)

        def step(i, carry):
            dk, dv = carry
            r0 = pl.multiple_of(i * TQ, TQ)
            q = q_ref[0, pl.ds(r0, TQ), :]
            o = o_ref[pl.ds(r0, TQ), :]
            do = do_ref[pl.ds(r0, TQ), :]
            lse = jnp.sum(jnp.where(lane == LSE_COL, o, 0.0), axis=1, keepdims=True)
            delta = jnp.sum(jnp.where(lane < FOX_D, o * do, 0.0), axis=1, keepdims=True)
            rowg = rowb + i * TQ
            p = jnp.where((colg <= rowg) & (colg >= N_PAD), jnp.exp(_dot_nt(q, kt) - lse), 0.0)
            dob = _b(do)
            dv = dv + _dot_tn(_b(p), dob)
            ds = _b(p * (_dot_nt(dob, vt) - delta))
            dq_ref[pl.ds(r0, TQ), :] += _dot(ds, kt)
            dk = dk + _dot_tn(ds, q)
            return dk, dv

        dk, dv = lax.fori_loop(j, NQ, step, (jnp.zeros((TK, HP), f32), jnp.zeros((TK, HP), f32)))
        dk_ref[...] = dk
        dv_ref[...] = _b(dv)

    head = pl.BlockSpec((L, HP), lambda h, j: (0, h))
    tile = pl.BlockSpec((TK, HP), lambda h, j: (j, h))
    return pl.pallas_call(
        body, name="fox_bwd", grid=(FOX_H, L // TK),
        in_specs=[pl.BlockSpec((1, L, HP), lambda h, j: (h, 0, 0)), pl.BlockSpec((1, TK, HP), lambda h, j: (h, j, 0)), tile, head, head],
        out_specs=[head, tile, tile],
        out_shape=[SDS((L, FOX_H * HP), f32), SDS((L, FOX_H * HP), f32), SDS((L, FOX_H * HP), bf16)],
        compiler_params=_cp(2))(qa, ka, fv, op, dop)


def _dn_post(y, sd, alog_p, dt_p, valid):
    a = y * _sigmoid(y)
    W = DN_H * DN_D
    heads = []
    for part, scale in ((0, DN_D ** -0.5), (1, 1.0)):
        for h in range(DN_H):
            xh = a[:, part * W + h * DN_D:part * W + (h + 1) * DN_D]
            heads.append(xh * lax.rsqrt(jnp.sum(xh * xh, axis=-1, keepdims=True) + EPS) * scale)
    q = jnp.concatenate(heads[:DN_H], axis=1)
    k = jnp.concatenate(heads[DN_H:], axis=1)
    v = a[:, 2 * W:3 * W]
    lane = _iota(sd.shape, 1)
    beta = _sigmoid(sd) * valid
    g = -jnp.exp(alog_p) * jax.nn.softplus(sd + dt_p) * valid
    bg = jnp.where(lane < DN_H, beta, jnp.where(lane < 2 * DN_H, g, 0.0))
    return q, k, v, bg


def _conv_fwd(ext_ref, cw_ref, TM):
    y = cw_ref[0:1, :] * ext_ref[8 - (CONV_K - 1):8 - (CONV_K - 1) + TM, :]
    for i in range(1, CONV_K):
        o = 8 - (CONV_K - 1) + i
        y = y + cw_ref[i:i + 1, :] * ext_ref[o:o + TM, :]
    return y


def _dn_prep(dn, sd, cw, alog_p, dt_p):
    L, W3 = dn.shape
    TM = _pick(L, ROW_TILES)
    W = DN_H * DN_D

    def body(dn_ref, halo_ref, sd_ref, cw_ref, al_ref, dt_ref, q_ref, k_ref, v_ref, bg_ref, ext):
        i = pl.program_id(0)
        ext[0:8, :] = jnp.where(i == 0, 0.0, halo_ref[...])
        ext[8:, :] = dn_ref[...]
        y = _conv_fwd(ext, cw_ref, TM)
        valid = ((i * TM + _iota((TM, 1), 0)) >= N_PAD).astype(f32)
        q, k, v, bg = _dn_post(y, sd_ref[...], al_ref[...], dt_ref[...], valid)
        q_ref[...], k_ref[...], v_ref[...], bg_ref[...] = q, k, v, bg

    row = lambda wd: pl.BlockSpec((TM, wd), lambda i: (i, 0))
    vec = pl.BlockSpec((1, HP), lambda i: (0, 0))
    return pl.pallas_call(
        body, name="dn_prep", grid=(L // TM,),
        in_specs=[row(W3), pl.BlockSpec((8, W3), lambda i: (jnp.maximum(i * (TM // 8) - 1, 0), 0)), row(HP),
                  pl.BlockSpec((CONV_K, W3), lambda i: (0, 0)), vec, vec],
        out_specs=[row(W), row(W), row(W), row(HP)],
        out_shape=[SDS((L, W), f32)] * 3 + [SDS((L, HP), f32)],
        scratch_shapes=[pltpu.VMEM((TM + 8, W3), f32)], compiler_params=_cp())(dn, dn, sd, cw, alog_p, dt_p)


def _dn_prep_bwd(dn, sd, cw, alog_p, dt_p, dq, dk, dv, dbg):
    L, W3 = dn.shape
    TM = _pick(L, ROW_TILES)
    NT = L // TM
    W = DN_H * DN_D

    def body(dn_ref, halo_ref, sd_ref, cw_ref, al_ref, dt_ref, dq_ref, dk_ref, dv_ref, dbg_ref,
             ddn_ref, dsd_ref, dcw_ref, dp_ref, ext, dyp, carry):
        i = pl.program_id(0)
        t = NT - 1 - i

        @pl.when(i == 0)
        def _():
            carry[...] = jnp.zeros_like(carry)
            dcw_ref[...] = jnp.zeros_like(dcw_ref)
            dp_ref[...] = jnp.zeros_like(dp_ref)
            dyp[...] = jnp.zeros_like(dyp)

        ext[0:8, :] = jnp.where(t == 0, 0.0, halo_ref[...])
        ext[8:, :] = dn_ref[...]
        y = _conv_fwd(ext, cw_ref, TM)
        valid = ((t * TM + _iota((TM, 1), 0)) >= N_PAD).astype(f32)
        _, vjp = jax.vjp(functools.partial(_dn_post, valid=valid), y, sd_ref[...], al_ref[...], dt_ref[...])
        dy, dsd, dal, ddt = vjp((dq_ref[...], dk_ref[...], dv_ref[...], dbg_ref[...]))
        dsd_ref[...] = _b(dsd)
        dp_ref[0:1, :] += dal
        dp_ref[1:2, :] += ddt
        dyp[8:8 + TM, :] = dy
        o0 = CONV_K - 1
        dext = cw_ref[0:1, :] * dyp[o0:o0 + TM + 8, :]
        for k in range(1, CONV_K):
            dext = dext + cw_ref[k:k + 1, :] * dyp[o0 - k:o0 - k + TM + 8, :]
        for k in range(CONV_K):
            o = 8 - (CONV_K - 1) + k
            dcw_ref[k:k + 1, :] += jnp.sum(dy * ext[o:o + TM, :], axis=0, keepdims=True)
        ddn_ref[...] = _b(jnp.concatenate([dext[8:TM, :], dext[TM:TM + 8, :] + carry[...]], axis=0))
        carry[...] = dext[0:8, :]

    row = lambda wd: pl.BlockSpec((TM, wd), lambda i: (NT - 1 - i, 0))
    vec = pl.BlockSpec((1, HP), lambda i: (0, 0))
    return pl.pallas_call(
        body, name="dn_prep_bwd", grid=(NT,),
        in_specs=[row(W3), pl.BlockSpec((8, W3), lambda i: (jnp.maximum((NT - 1 - i) * (TM // 8) - 1, 0), 0)), row(HP),
                  pl.BlockSpec((CONV_K, W3), lambda i: (0, 0)), vec, vec, row(W), row(W), row(W), row(HP)],
        out_specs=[row(W3), row(HP), pl.BlockSpec((8, W3), lambda i: (0, 0)), pl.BlockSpec((8, HP), lambda i: (0, 0))],
        out_shape=[SDS((L, W3), bf16), SDS((L, HP), bf16), SDS((8, W3), f32), SDS((8, HP), f32)],
        scratch_shapes=[pltpu.VMEM((TM + 8, W3), f32), pltpu.VMEM((TM + 16, W3), f32), pltpu.VMEM((8, W3), f32)],
        compiler_params=_cp())(dn, dn, sd, cw, alog_p, dt_p, dq, dk, dv, dbg)


def _dn_intra_chunk(q, k, v, bg):
    C = DN_C
    row, col = _iota((C, C), 0), _iota((C, C), 1)
    tri = row >= col
    eye = (row == col).astype(f32)
    G = _dot_hi(tri.astype(f32), bg)
    GT = lax.dot_general(G, eye, (((0,), (0,)), ((), ())), preferred_element_type=f32, precision=HI)
    lane = _iota((C, HP), 1)
    rowt = _iota((HP, C), 0)
    last = _iota((C, 1), 0) == C - 1
    lane1 = _iota((1, HP), 1)
    us, ws, qds, kds, attns = [], [], [], [], []
    glrow = jnp.zeros((1, HP), f32)
    for h in range(DN_H):
        beta = jnp.sum(jnp.where(lane == h, bg, 0.0), axis=1, keepdims=True)
        gcol = jnp.sum(jnp.where(lane == DN_H + h, G, 0.0), axis=1, keepdims=True)
        grow = jnp.sum(jnp.where(rowt == DN_H + h, GT, 0.0), axis=0, keepdims=True)
        glast = jnp.sum(jnp.where(last, gcol, 0.0), axis=0, keepdims=True)
        decay = jnp.exp(jnp.where(tri, gcol - grow, NEG))
        qh, kh, vh = (t[:, h * DN_D:(h + 1) * DN_D] for t in (q, k, v))
        kb = kh * beta
        low = jnp.where(row > col, _dot_nt(_b(kb), _b(kh)) * decay, 0.0)
        X, P = eye - low, low
        for _ in range(5):
            P = _dot_hi(P, P)
            X = X + _dot_hi(X, P)
        eg = jnp.exp(gcol)
        us.append(_dot_hi(X, vh * beta))
        ws.append(_dot_hi(X, kb * eg))
        attns.append(_dot_nt(_b(qh), _b(kh)) * decay)
        qds.append(qh * eg)
        kds.append(kh * jnp.exp(glast - gcol))
        glrow = glrow + jnp.where(lane1 == h, glast, 0.0)
    cat = lambda xs: jnp.concatenate(xs, axis=1)
    return cat(us), cat(ws), cat(qds), cat(kds), cat(attns), glrow


def _lane_pick(rowvec, h):
    return jnp.sum(jnp.where(_iota(rowvec.shape, 1) == h, rowvec, 0.0), axis=1, keepdims=True)


def _dn_intra(q, k, v, bg):
    L, W = q.shape
    NC = L // DN_C
    G = _pick(NC, DN_INTRA_GROUP)
    R = G * DN_C
    WA = DN_H * DN_C

    def body(q_ref, k_ref, v_ref, bg_ref, u_ref, w_ref, qd_ref, kd_ref, at_ref, gl_ref):
        for j in range(G):
            r = slice(j * DN_C, (j + 1) * DN_C)
            u, w, qd, kd, at, gl = _dn_intra_chunk(q_ref[r, :], k_ref[r, :], v_ref[r, :], bg_ref[r, :])
            u_ref[r, :] = u
            w_ref[r, :], qd_ref[r, :], kd_ref[r, :], at_ref[r, :] = _b(w), _b(qd), _b(kd), _b(at)
            gl_ref[j] = gl

    row = lambda wd: pl.BlockSpec((R, wd), lambda n: (n, 0))
    return pl.pallas_call(
        body, name="dn_intra", grid=(NC // G,),
        in_specs=[row(W), row(W), row(W), row(HP)],
        out_specs=[row(W), row(W), row(W), row(W), row(WA), pl.BlockSpec((G, 1, HP), lambda n: (n, 0, 0))],
        out_shape=[SDS((L, W), f32), SDS((L, W), bf16), SDS((L, W), bf16), SDS((L, W), bf16), SDS((L, WA), bf16), SDS((NC, 1, HP), f32)],
        compiler_params=_cp())(q, k, v, bg)


def _dn_scan(u, w, qd, kd, at, gl):
    L, W = u.shape
    NC = L // DN_C
    G = _pick(NC, DN_SCAN_GROUP)
    R = G * DN_C

    def body(u_ref, w_ref, qd_ref, kd_ref, at_ref, gl_ref, o_ref, vn_ref, s_ref, S):
        @pl.when(pl.program_id(0) == 0)
        def _():
            S[...] = jnp.zeros_like(S)

        for j in range(G):
            r = slice(j * DN_C, (j + 1) * DN_C)
            glrow = gl_ref[j]
            for h in range(DN_H):
                c = slice(h * DN_D, (h + 1) * DN_D)
                Sh = S[h]
                s_ref[j, h] = Sh
                Sb = _b(Sh)
                vb = _b(u_ref[r, c] - _dot(w_ref[r, c], Sb))
                vn_ref[r, c] = vb
                o_ref[r, c] = _dot(qd_ref[r, c], Sb) + _dot(at_ref[r, h * DN_C:(h + 1) * DN_C], vb)
                S[h] = Sh * jnp.exp(_lane_pick(glrow, h)) + _dot_tn(kd_ref[r, c], vb)

    row = lambda wd: pl.BlockSpec((R, wd), lambda n: (n, 0))
    return pl.pallas_call(
        body, name="dn_scan", grid=(NC // G,),
        in_specs=[row(W), row(W), row(W), row(W), row(DN_H * DN_C), pl.BlockSpec((G, 1, HP), lambda n: (n, 0, 0))],
        out_specs=[row(W), row(W), pl.BlockSpec((G, DN_H, DN_D, DN_D), lambda n: (n, 0, 0, 0))],
        out_shape=[SDS((L, W), f32), SDS((L, W), bf16), SDS((NC, DN_H, DN_D, DN_D), f32)],
        scratch_shapes=[pltpu.VMEM((DN_H, DN_D, DN_D), f32)], compiler_params=_cp())(u, w, qd, kd, at, gl)


def _dn_scan_bwd(do, w, qd, kd, at, gl):
    L, W = do.shape
    NC = L // DN_C
    G = _pick(NC, DN_SCAN_GROUP)
    R = G * DN_C
    NS = NC // G

    def body(do_ref, w_ref, qd_ref, kd_ref, at_ref, gl_ref, dvn_ref, ds_ref, dS):
        @pl.when(pl.program_id(0) == 0)
        def _():
            dS[...] = jnp.zeros_like(dS)

        for j in reversed(range(G)):
            r = slice(j * DN_C, (j + 1) * DN_C)
            glrow = gl_ref[j]
            for h in range(DN_H):
                c = slice(h * DN_D, (h + 1) * DN_D)
                dSo = dS[h]
                ds_ref[j, h] = dSo
                dob = _b(do_ref[r, c])
                dvn = _dot_tn(at_ref[r, h * DN_C:(h + 1) * DN_C], dob) + _dot(kd_ref[r, c], _b(dSo))
                dvn_ref[r, c] = dvn
                dS[h] = _dot_tn(qd_ref[r, c], dob) + dSo * jnp.exp(_lane_pick(glrow, h)) - _dot_tn(w_ref[r, c], _b(dvn))

    row = lambda wd: pl.BlockSpec((R, wd), lambda n: (NS - 1 - n, 0))
    return pl.pallas_call(
        body, name="dn_scan_bwd", grid=(NS,),
        in_specs=[row(W), row(W), row(W), row(W), row(DN_H * DN_C), pl.BlockSpec((G, 1, HP), lambda n: (NS - 1 - n, 0, 0))],
        out_specs=[row(W), pl.BlockSpec((G, DN_H, DN_D, DN_D), lambda n: (NS - 1 - n, 0, 0, 0))],
        out_shape=[SDS((L, W), f32), SDS((NC, DN_H, DN_D, DN_D), f32)],
        scratch_shapes=[pltpu.VMEM((DN_H, DN_D, DN_D), f32)], compiler_params=_cp())(do, w, qd, kd, at, gl)


def _dn_intra_bwd(q, k, v, bg, do, vn, dvn, states, dstates):
    L, W = q.shape
    NC = L // DN_C
    G = _pick(NC, DN_INTRA_GROUP)
    R = G * DN_C

    def body(q_ref, k_ref, v_ref, bg_ref, do_ref, vn_ref, dvn_ref, s_ref, ds_ref, dq_ref, dk_ref, dv_ref, dbg_ref):
        lane1 = _iota((1, HP), 1)
        for j in range(G):
            r = slice(j * DN_C, (j + 1) * DN_C)
            outs, vjp = jax.vjp(_dn_intra_chunk, q_ref[r, :], k_ref[r, :], v_ref[r, :], bg_ref[r, :])
            glrow = outs[5]
            dws, dqds, dkds, dats = [], [], [], []
            dgl = jnp.zeros((1, HP), f32)
            for h in range(DN_H):
                c = slice(h * DN_D, (h + 1) * DN_D)
                Sh, dSo = s_ref[j, h], ds_ref[j, h]
                Sb, dob, vb = _b(Sh), _b(do_ref[r, c]), vn_ref[r, c]
                dws.append(-_dot_nt(_b(dvn_ref[r, c]), Sb))
                dqds.append(_dot_nt(dob, Sb))
                dats.append(_dot_nt(dob, vb))
                dkds.append(_dot_nt(vb, _b(dSo)))
                dcd = jnp.sum(jnp.sum(Sh * dSo, axis=1, keepdims=True), axis=0, keepdims=True)
                dgl = dgl + jnp.where(lane1 == h, dcd * jnp.exp(_lane_pick(glrow, h)), 0.0)
            cat = lambda xs: jnp.concatenate(xs, axis=1)
            dq, dk, dv, dbg = vjp((dvn_ref[r, :], cat(dws), cat(dqds), cat(dkds), cat(dats), dgl))
            dq_ref[r, :], dk_ref[r, :], dv_ref[r, :], dbg_ref[r, :] = dq, dk, dv, dbg

    row = lambda wd: pl.BlockSpec((R, wd), lambda n: (n, 0))
    st = pl.BlockSpec((G, DN_H, DN_D, DN_D), lambda n: (n, 0, 0, 0))
    return pl.pallas_call(
        body, name="dn_intra_bwd", grid=(NC // G,),
        in_specs=[row(W), row(W), row(W), row(HP), row(W), row(W), row(W), st, st],
        out_specs=[row(W), row(W), row(W), row(HP)],
        out_shape=[SDS((L, W), f32)] * 3 + [SDS((L, HP), f32)],
        compiler_params=_cp())(q, k, v, bg, do, vn, dvn, states, dstates)


def _dn_normgate(oraw, dz, wn):
    outs = []
    for h in range(DN_H):
        sl = slice(h * DN_D, (h + 1) * DN_D)
        z = dz[:, sl]
        outs.append(_rms(oraw[:, sl], wn) * (z * _sigmoid(z)))
    return jnp.concatenate(outs, axis=1)


def _mix_fwd(op, oraw, dz, ga, gb, h0, wn, wbf, wbd, wo):
    L, D = h0.shape
    TM = _pick(L, ROW_TILES)

    def body(op_ref, or_ref, dz_ref, ga_ref, gb_ref, h0_ref, wn_ref, wbf_ref, wbd_ref, wo_ref, h1_ref):
        pf = _dot(_b(op_ref[...]), wbf_ref[...])
        pd = _dot(_b(_dn_normgate(or_ref[...], dz_ref[...], wn_ref[...])), wbd_ref[...])
        y = _sigmoid(ga_ref[...]) * pf + _sigmoid(gb_ref[...]) * pd
        h1_ref[...] = h0_ref[...] + _dot(_b(y), wo_ref[...])

    row = lambda wd: pl.BlockSpec((TM, wd), lambda i: (i, 0))
    full = lambda a: pl.BlockSpec(a.shape, lambda i: (0, 0))
    return pl.pallas_call(
        body, name="mix_fwd", grid=(L // TM,),
        in_specs=[row(op.shape[1]), row(oraw.shape[1]), row(dz.shape[1]), row(D), row(D), row(D), full(wn), full(wbf), full(wbd), full(wo)],
        out_specs=row(D), out_shape=SDS((L, D), f32), compiler_params=_cp())(op, oraw, dz, ga, gb, h0, wn, wbf, wbd, wo)


def _mix_bwd(dh1, op, oraw, dz, ga, gb, wn, wbf, wbd, wo):
    L, D = dh1.shape
    TM = _pick(L, ROW_TILES)
    WF, WD = op.shape[1], oraw.shape[1]

    def body(dh1_ref, op_ref, or_ref, dz_ref, ga_ref, gb_ref, wn_ref, wbf_ref, wbd_ref, wo_ref,
             dop_ref, dor_ref, ddz_ref, dga_ref, dgb_ref, af_ref, ad_ref, dpf_ref, dpd_ref, y_ref, dmix_ref, acc_ref):
        @pl.when(pl.program_id(0) == 0)
        def _():
            acc_ref[...] = jnp.zeros_like(acc_ref)

        af = _b(op_ref[...])
        ad, vjp = jax.vjp(_dn_normgate, or_ref[...], dz_ref[...], wn_ref[...])
        adb = _b(ad)
        pf, pd = _dot(af, wbf_ref[...]), _dot(adb, wbd_ref[...])
        sa, sb = _sigmoid(ga_ref[...]), _sigmoid(gb_ref[...])
        dmix = _b(dh1_ref[...])
        dy = _dot_nt(dmix, wo_ref[...])
        dpf, dpd = _b(dy * sa), _b(dy * sb)
        dor, ddz, dwn = vjp(_dot_nt(dpd, wbd_ref[...]))
        dop_ref[...] = _dot_nt(dpf, wbf_ref[...])
        dor_ref[...] = dor
        ddz_ref[...] = _b(ddz)
        dga_ref[...] = _b(dy * pf * sa * (1.0 - sa))
        dgb_ref[...] = _b(dy * pd * sb * (1.0 - sb))
        af_ref[...], ad_ref[...], dpf_ref[...], dpd_ref[...] = af, adb, dpf, dpd
        y_ref[...] = _b(sa * pf + sb * pd)
        dmix_ref[...] = dmix
        acc_ref[0:1, :] += dwn

    row = lambda wd: pl.BlockSpec((TM, wd), lambda i: (i, 0))
    full = lambda a: pl.BlockSpec(a.shape, lambda i: (0, 0))
    return pl.pallas_call(
        body, name="mix_bwd", grid=(L // TM,),
        in_specs=[row(D), row(WF), row(WD), row(WD), row(D), row(D), full(wn), full(wbf), full(wbd), full(wo)],
        out_specs=[row(WF), row(WD), row(WD), row(D), row(D), row(WF), row(WD), row(D), row(D), row(D), row(D),
                   pl.BlockSpec((8, HP), lambda i: (0, 0))],
        out_shape=[SDS((L, WF), f32), SDS((L, WD), f32), SDS((L, WD), bf16), SDS((L, D), bf16), SDS((L, D), bf16),
                   SDS((L, WF), bf16), SDS((L, WD), bf16), SDS((L, D), bf16), SDS((L, D), bf16), SDS((L, D), bf16), SDS((L, D), bf16),
                   SDS((8, HP), f32)],
        compiler_params=_cp())(dh1, op, oraw, dz, ga, gb, wn, wbf, wbd, wo)


def _ffn_fwd_bwd(h1, tgt, w2, wf, wg, wu, wd):
    L, D = h1.shape
    F = wg.shape[1]
    TM = _pick(L, FFN_TILES)

    def body(h_ref, t_ref, w2_ref, wf_ref, wg_hbm, wu_hbm, wd_hbm,
             dh1_ref, xn_ref, dg_ref, du_ref, act_ref, dh2_ref, acc_ref, wg_v, wu_v, wd_v, sems):
        i = pl.program_id(0)
        _load_once([(wg_hbm, wg_v), (wu_hbm, wu_v), (wd_hbm, wd_v)], sems)

        @pl.when(i == 0)
        def _():
            acc_ref[...] = jnp.zeros_like(acc_ref)

        h1v = h_ref[...]
        xn2, vjp2 = jax.vjp(_rms, h1v, w2_ref[...])
        xb = _b(xn2)
        g, u = _dot(xb, wg_v[...]), _dot(xb, wu_v[...])
        sg = _sigmoid(g)
        ab = _b(g * sg * u)
        h2 = h1v + _dot(ab, wd_v[...])
        out, vjpf = jax.vjp(_rms, h2, wf_ref[...])
        valid = (i * TM + _iota((TM, 1), 0)) >= PREFIX
        diff = jnp.where(valid, out - t_ref[...], 0.0)
        loss = 0.5 * jnp.sum(jnp.sum(diff * diff, axis=1, keepdims=True), axis=0, keepdims=True) / D
        dh2, dwf = vjpf(diff * (1.0 / D))
        dh2b = _b(dh2)
        dact = _dot_nt(dh2b, wd_v[...])
        dgb = _b(dact * u * (sg * (1.0 + g * (1.0 - sg))))
        dub = _b(dact * (g * sg))
        dh1n, dw2 = vjp2(_dot_nt(dgb, wg_v[...]) + _dot_nt(dub, wu_v[...]))
        dh1_ref[...] = dh2 + dh1n
        xn_ref[...], dg_ref[...], du_ref[...], act_ref[...], dh2_ref[...] = xb, dgb, dub, ab, dh2b
        acc_ref[0:1, :] += dw2
        acc_ref[1:2, :] += dwf
        acc_ref[2:3, :] += jnp.broadcast_to(loss, (1, D))

    row = lambda wd_: pl.BlockSpec((TM, wd_), lambda i: (i, 0))
    vec = pl.BlockSpec((1, D), lambda i: (0, 0))
    anyspec = pl.BlockSpec(memory_space=pl.ANY)
    return pl.pallas_call(
        body, name="ffn_fwd_bwd", grid=(L // TM,),
        in_specs=[row(D), row(D), vec, vec, anyspec, anyspec, anyspec],
        out_specs=[row(D), row(D), row(F), row(F), row(F), row(D), pl.BlockSpec((8, D), lambda i: (0, 0))],
        out_shape=[SDS((L, D), f32), SDS((L, D), bf16), SDS((L, F), bf16), SDS((L, F), bf16), SDS((L, F), bf16), SDS((L, D), bf16),
                   SDS((8, D), f32)],
        scratch_shapes=[pltpu.VMEM((D, F), bf16), pltpu.VMEM((D, F), bf16), pltpu.VMEM((F, D), bf16), pltpu.SemaphoreType.DMA((3,))],
        compiler_params=_cp())(h1, tgt, w2, wf, wg, wu, wd)


def _pad_lanes(v, n=HP):
    return jnp.pad(v.astype(f32), ((0, 0), (0, n - v.shape[1])))


def _pack_w_in(w_full):
    D = w_full.shape[0]
    FW, DW = FOX_H * FOX_D, DN_H * DN_D
    o = 0
    parts = {}
    for name, wd in (("fq", FW), ("fk", FW), ("fv", FW), ("fl", FOX_H), ("dn", 3 * DW), ("ba", 2 * DN_H), ("dz", DW), ("ga", D), ("gb", D)):
        parts[name] = w_full[:, o:o + wd]
        o += wd
    assert o == w_full.shape[1]
    heads = lambda w: jnp.pad(w.reshape(D, FOX_H, FOX_D), ((0, 0), (0, 0), (0, HP - FOX_D))).reshape(D, FOX_H * HP)
    small = lambda w: jnp.pad(w, ((0, 0), (0, HP - w.shape[1])))
    return jnp.concatenate([heads(parts["fq"]), heads(parts["fk"]), heads(parts["fv"]), parts["dn"], parts["dz"], parts["ga"], parts["gb"],
                            small(parts["fl"]), small(parts["ba"])], axis=1)


def _unpack_w_in(gp, d_model):
    D = gp.shape[0]
    FW, DW = FOX_H * FOX_D, DN_H * DN_D
    segs, o = {}, 0
    for name, wd, _ in _seg_layout(d_model):
        segs[name] = gp[:, o:o + wd]
        o += wd
    heads = lambda g: g.reshape(D, FOX_H, HP)[:, :, :FOX_D].reshape(D, FW)
    return jnp.concatenate([heads(segs["fq"]), heads(segs["fk"]), heads(segs["fv"]), segs["sf"][:, :FOX_H], segs["dn"],
                            segs["sd"][:, :2 * DN_H], segs["dz"], segs["ga"], segs["gb"]], axis=1)


def _local_step(x, tgt, meta, w1, w_in, fbias, cw, alog, dtb, wn, wbf, wbd, wo, w2, wg, wu, wd, wf):
    T, D = x.shape
    h0 = jnp.concatenate([jnp.zeros((N_PAD, D), f32), meta, x], axis=0)
    tgt_p = jnp.concatenate([jnp.zeros((PREFIX, D), f32), tgt], axis=0)
    wp = _pack_w_in(w_in)
    wbf_p = jnp.pad(wbf.reshape(FOX_H, FOX_D, D), ((0, 0), (0, HP - FOX_D), (0, 0))).reshape(FOX_H * HP, D)
    bias_p, alog_p, dt_p = _pad_lanes(fbias), _pad_lanes(jnp.pad(alog, ((0, 0), (DN_H, 0)))), _pad_lanes(jnp.pad(dtb, ((0, 0), (DN_H, 0))))

    xn, fq, fk, fv, dn, dz, ga, gb, sf, sd = _in_proj(h0, w1, wp)
    qa, ka = _fox_prep(fq, fk, sf, bias_p)
    op = _fox_fwd(qa, ka, fv)
    qn, kn, vn, bg = _dn_prep(dn, sd, cw, alog_p, dt_p)
    u_dn, w_dn, qd_dn, kd_dn, at_dn, gl_dn = _dn_intra(qn, kn, vn, bg)
    oraw, vnew, states = _dn_scan(u_dn, w_dn, qd_dn, kd_dn, at_dn, gl_dn)
    h1 = _mix_fwd(op, oraw, dz, ga, gb, h0, wn, wbf_p, wbd, wo)

    dh1, xn2, dgate, dup, act, dh2, acc_f = _ffn_fwd_bwd(h1, tgt_p, w2, wf, wg, wu, wd)
    g_wg, g_wu, g_wd = _matmul_tn(xn2, dgate, "dw_ffn_gate"), _matmul_tn(xn2, dup, "dw_ffn_up"), _matmul_tn(act, dh2, "dw_ffn_down")

    dop, dor, ddz, dga, dgb, af, ad, dpf, dpd, yb, dmix, acc_m = _mix_bwd(dh1, op, oraw, dz, ga, gb, wn, wbf_p, wbd, wo)
    g_wbf = _matmul_tn(af, dpf, "dw_branch_fox").reshape(FOX_H, HP, D)[:, :FOX_D].reshape(FOX_H * FOX_D, D)
    g_wbd, g_wo = _matmul_tn(ad, dpd, "dw_branch_dn"), _matmul_tn(yb, dmix, "dw_out")

    dvnew, dstates = _dn_scan_bwd(dor, w_dn, qd_dn, kd_dn, at_dn, gl_dn)
    dqn, dkn, dvn, dbg = _dn_intra_bwd(qn, kn, vn, bg, dor, vnew, dvnew, states, dstates)
    ddn, dsd, acc_cw, acc_p = _dn_prep_bwd(dn, sd, cw, alog_p, dt_p, dqn, dkn, dvn, dbg)
    dqa, dka, dfv = _fox_bwd(qa, ka, fv, op, dop)
    dfq, dfk, dsf, acc_b = _fox_prep_bwd(dqa, dka, sf, bias_p)

    dproj = jnp.concatenate([dfq, dfk, dfv, ddn, ddz, dga, dgb, dsf, dsd], axis=1)
    g_wp = _matmul_tn(xn, dproj, "dw_in")
    dh0, acc_1 = _in_proj_bwd(dproj, wp, h0, w1, dh1)

    small = dict(loss=acc_f[2, 0:1], mix_norm_w=acc_1[0], fox_forget_bias=acc_b[0, :FOX_H], dn_a_log=acc_p[0, DN_H:2 * DN_H],
                 dn_dt_bias=acc_p[1, DN_H:2 * DN_H], dn_out_norm_w=acc_m[0], ffn_norm_w=acc_f[0], final_norm_w=acc_f[1],
                 meta_tokens=dh0[N_PAD:PREFIX].reshape(-1), dn_conv_w=acc_cw[:CONV_K].reshape(-1))
    big = dict(w_in=_unpack_w_in(g_wp, D), w_branch_fox=g_wbf, w_branch_dn=g_wbd, w_out=g_wo, w_ffn_gate=g_wg, w_ffn_up=g_wu, w_ffn_down=g_wd)
    return dh0[PREFIX:], small, big


def _mesh_pos():
    x, y, c = lax.axis_index("x"), lax.axis_index("y"), lax.axis_index("c")
    return x, y, c, 4 * x + 2 * y + c


def _peer(x, y, c, m):
    flip = lambda v, on: 1 - v if on else v
    px, py, pc = flip(x, m & 4), flip(y, m & 2), flip(c, m & 1)
    return (px, py, pc), 4 * px + 2 * py + pc


def _exchange(arrays, name, gather):
    n = len(arrays)
    shapes = [a.shape if gather else a.shape[1:] for a in arrays]

    def body(*refs):
        ins, outs, (send_sems, recv_sems, loc_sems) = refs[:n], refs[n:2 * n], refs[2 * n:]
        x, y, c, me = _mesh_pos()
        src = lambda a, pid: ins[a] if gather else ins[a].at[pid]
        local = [pltpu.make_async_copy(src(a, me), outs[a].at[me], loc_sems.at[a]) for a in range(n)]
        for cp in local:
            cp.start()
        sends = []
        for m in range(1, N_DEV):
            peer, pid = _peer(x, y, c, m)
            for a in range(n):
                cp = pltpu.make_async_remote_copy(src_ref=src(a, pid), dst_ref=outs[a].at[me], send_sem=send_sems.at[a, m - 1],
                                                  recv_sem=recv_sems.at[a, m - 1], device_id=peer, device_id_type=MESH)
                cp.start()
                sends.append(cp)
        for m in range(1, N_DEV):
            peer, pid = _peer(x, y, c, m)
            for a in range(n):
                pltpu.make_async_remote_copy(src_ref=src(a, pid), dst_ref=outs[a].at[pid], send_sem=send_sems.at[a, m - 1],
                                             recv_sem=recv_sems.at[a, m - 1], device_id=peer, device_id_type=MESH).wait_recv()
        for cp in sends:
            cp.wait_send()
        for cp in local:
            cp.wait()

    anyspec = pl.BlockSpec(memory_space=pl.ANY)
    return pl.pallas_call(
        body, name=name, in_specs=[anyspec] * n, out_specs=[anyspec] * n,
        out_shape=[SDS((N_DEV,) + tuple(s), a.dtype) for s, a in zip(shapes, arrays)],
        scratch_shapes=[pltpu.SemaphoreType.DMA((n, N_DEV - 1)), pltpu.SemaphoreType.DMA((n, N_DEV - 1)), pltpu.SemaphoreType.DMA((n,))],
        )(*arrays)


def _all_reduce_small(v):
    R = v.shape[0]

    def body(v_ref, o_ref, gath, send_sems, recv_sems):
        x, y, c, me = _mesh_pos()
        gath[me] = v_ref[...]
        sends = []
        for m in range(1, N_DEV):
            peer, _ = _peer(x, y, c, m)
            cp = pltpu.make_async_remote_copy(src_ref=v_ref, dst_ref=gath.at[me], send_sem=send_sems.at[m - 1],
                                              recv_sem=recv_sems.at[m - 1], device_id=peer, device_id_type=MESH)
            cp.start()
            sends.append(cp)
        for m in range(1, N_DEV):
            peer, pid = _peer(x, y, c, m)
            pltpu.make_async_remote_copy(src_ref=v_ref, dst_ref=gath.at[pid], send_sem=send_sems.at[m - 1],
                                         recv_sem=recv_sems.at[m - 1], device_id=peer, device_id_type=MESH).wait_recv()
        for cp in sends:
            cp.wait_send()
        tot = gath[0]
        for d in range(1, N_DEV):
            tot = tot + gath[d]
        o_ref[...] = tot

    vm = pl.BlockSpec(memory_space=pltpu.VMEM)
    return pl.pallas_call(
        body, name="all_reduce_small", in_specs=[vm], out_specs=vm, out_shape=SDS((R, HP), f32),
        scratch_shapes=[pltpu.VMEM((N_DEV, R, HP), f32), pltpu.SemaphoreType.DMA((N_DEV - 1,)), pltpu.SemaphoreType.DMA((N_DEV - 1,))],
        )(v)


def _adamw_math(w, g, m, v):
    m = ADAM_B1 * m + (1.0 - ADAM_B1) * g
    v = ADAM_B2 * v + (1.0 - ADAM_B2) * (g * g)
    m_hat = m / (1.0 - ADAM_B1 ** ADAM_STEP)
    v_hat = v / (1.0 - ADAM_B2 ** ADAM_STEP)
    return -ADAM_LR * (m_hat / (jnp.sqrt(v_hat) + ADAM_EPS) + ADAM_WD * w), m, v


def _adamw(g, w, m, v, name):
    R, Cc = w.shape
    TR = R if R <= 512 else _pick(R, (256, 128))
    slabs = g.ndim == 3

    def body(g_ref, w_ref, m_ref, v_ref, go_ref, d_ref, mo_ref, vo_ref):
        if slabs:
            gs = g_ref[0].astype(f32)
            for k in range(1, N_DEV):
                gs = gs + g_ref[k].astype(f32)
        else:
            gs = g_ref[...]
        d, mn, vn = _adamw_math(w_ref[...], gs, m_ref[...], v_ref[...])
        go_ref[...], d_ref[...], mo_ref[...], vo_ref[...] = gs, d, mn, vn

    blk = pl.BlockSpec((TR, Cc), lambda i: (i, 0))
    gblk = pl.BlockSpec((N_DEV, TR, Cc), lambda i: (0, i, 0)) if slabs else blk
    return pl.pallas_call(
        body, name=name, grid=(R // TR,), in_specs=[gblk, blk, blk, blk], out_specs=[blk] * 4,
        out_shape=[SDS((R, Cc), f32)] * 4, compiler_params=_cp())(g, w, m, v)


WEIGHTS = ("meta_tokens", "mix_norm_w", "w_in", "fox_forget_bias", "dn_conv_w", "dn_a_log", "dn_dt_bias", "dn_out_norm_w",
           "w_branch_fox", "w_branch_dn", "w_out", "ffn_norm_w", "w_ffn_gate", "w_ffn_up", "w_ffn_down", "final_norm_w")
COL_SHARDED = ("w_in", "w_branch_fox", "w_branch_dn", "w_ffn_gate", "w_ffn_up")
ROW_SHARDED = ("w_out", "w_ffn_down")
BIG = COL_SHARDED + ROW_SHARDED
SMALL = tuple(n for n in WEIGHTS if n not in BIG)


def _to_slabs(name, g):
    r, c = g.shape
    if name in COL_SHARDED:
        return _b(g.reshape(r, N_DEV, c // N_DEV).transpose(1, 0, 2))
    return _b(g.reshape(N_DEV, r // N_DEV, c))


def _from_slabs(name, s):
    n, r, c = s.shape
    if name in COL_SHARDED:
        return s.transpose(1, 0, 2).reshape(r, n * c)
    return s.reshape(n * r, c)


def kernel(x, meta_tokens, mix_norm_w, w_in, fox_forget_bias, dn_conv_w, dn_a_log, dn_dt_bias, dn_out_norm_w, w_branch_fox, w_branch_dn, w_out, ffn_norm_w, w_ffn_gate, w_ffn_up, w_ffn_down, final_norm_w, loss_target, m_meta_tokens, m_mix_norm_w, m_w_in, m_fox_forget_bias, m_dn_conv_w, m_dn_a_log, m_dn_dt_bias, m_dn_out_norm_w, m_w_branch_fox, m_w_branch_dn, m_w_out, m_ffn_norm_w, m_w_ffn_gate, m_w_ffn_up, m_w_ffn_down, m_final_norm_w, v_meta_tokens, v_mix_norm_w, v_w_in, v_fox_forget_bias, v_dn_conv_w, v_dn_a_log, v_dn_dt_bias, v_dn_out_norm_w, v_w_branch_fox, v_w_branch_dn, v_w_out, v_ffn_norm_w, v_w_ffn_gate, v_w_ffn_up, v_w_ffn_down, v_final_norm_w):
    w = dict(meta_tokens=meta_tokens, mix_norm_w=mix_norm_w, w_in=w_in, fox_forget_bias=fox_forget_bias, dn_conv_w=dn_conv_w, dn_a_log=dn_a_log, dn_dt_bias=dn_dt_bias, dn_out_norm_w=dn_out_norm_w, w_branch_fox=w_branch_fox, w_branch_dn=w_branch_dn, w_out=w_out, ffn_norm_w=ffn_norm_w, w_ffn_gate=w_ffn_gate, w_ffn_up=w_ffn_up, w_ffn_down=w_ffn_down, final_norm_w=final_norm_w)
    mom = dict(meta_tokens=m_meta_tokens, mix_norm_w=m_mix_norm_w, w_in=m_w_in, fox_forget_bias=m_fox_forget_bias, dn_conv_w=m_dn_conv_w, dn_a_log=m_dn_a_log, dn_dt_bias=m_dn_dt_bias, dn_out_norm_w=m_dn_out_norm_w, w_branch_fox=m_w_branch_fox, w_branch_dn=m_w_branch_dn, w_out=m_w_out, ffn_norm_w=m_ffn_norm_w, w_ffn_gate=m_w_ffn_gate, w_ffn_up=m_w_ffn_up, w_ffn_down=m_w_ffn_down, final_norm_w=m_final_norm_w)
    var = dict(meta_tokens=v_meta_tokens, mix_norm_w=v_mix_norm_w, w_in=v_w_in, fox_forget_bias=v_fox_forget_bias, dn_conv_w=v_dn_conv_w, dn_a_log=v_dn_a_log, dn_dt_bias=v_dn_dt_bias, dn_out_norm_w=v_dn_out_norm_w, w_branch_fox=v_w_branch_fox, w_branch_dn=v_w_branch_dn, w_out=v_w_out, ffn_norm_w=v_ffn_norm_w, w_ffn_gate=v_w_ffn_gate, w_ffn_up=v_w_ffn_up, w_ffn_down=v_w_ffn_down, final_norm_w=v_final_norm_w)
    two_d = lambda a: a.reshape(a.shape[-2:]) if a.ndim >= 2 else a.reshape(1, -1)
    me = 4 * lax.axis_index("x") + 2 * lax.axis_index("y") + lax.axis_index("c")

    gathered = _exchange([_b(two_d(w[n])) for n in BIG] + [two_d(w["meta_tokens"]), two_d(w["dn_conv_w"])], "all_gather_weights", gather=True)
    full = {n: _from_slabs(n, s) for n, s in zip(BIG, gathered)}
    meta = gathered[len(BIG)].transpose(1, 0, 2).reshape(N_META, -1)
    cw = gathered[len(BIG) + 1].transpose(1, 0, 2).reshape(CONV_K, -1)

    gx, g_small, g_big = _local_step(
        x[0], loss_target[0], meta, two_d(w["mix_norm_w"]), full["w_in"], two_d(w["fox_forget_bias"]), cw, two_d(w["dn_a_log"]),
        two_d(w["dn_dt_bias"]), two_d(w["dn_out_norm_w"]), full["w_branch_fox"], full["w_branch_dn"], full["w_out"], two_d(w["ffn_norm_w"]),
        full["w_ffn_gate"], full["w_ffn_up"], full["w_ffn_down"], two_d(w["final_norm_w"]))

    recv = dict(zip(BIG, _exchange([_to_slabs(n, g_big[n]) for n in BIG], "exchange_weight_grads", gather=False)))
    order = ("loss",) + SMALL
    flat = jnp.concatenate([g_small[n].reshape(-1) for n in order])
    rows = -(-flat.shape[0] // (8 * HP)) * 8
    tot = _all_reduce_small(jnp.pad(flat, (0, rows * HP - flat.shape[0])).reshape(rows, HP)).reshape(-1)
    summed, o = {}, 0
    for n in order:
        k = g_small[n].shape[0]
        summed[n] = tot[o:o + k]
        o += k
    loss = summed["loss"][0]
    d_model = x.shape[-1]
    mcols, ccols = d_model // N_DEV, dn_conv_w.shape[-1]
    summed["meta_tokens"] = lax.dynamic_slice(summed["meta_tokens"].reshape(N_META, d_model), (0, me * mcols), (N_META, mcols)).reshape(-1)
    summed["dn_conv_w"] = lax.dynamic_slice(summed["dn_conv_w"].reshape(CONV_K, ccols * N_DEV), (0, me * ccols), (CONV_K, ccols)).reshape(-1)

    res = {}
    for n in BIG:
        res[n] = [r.reshape(w[n].shape) for r in _adamw(recv[n], two_d(w[n]), two_d(mom[n]), two_d(var[n]), "adamw_" + n)]
    sizes = [summed[n].shape[0] for n in SMALL]
    srows = -(-sum(sizes) // (8 * HP)) * 8
    pack = lambda d: jnp.pad(jnp.concatenate([d[n].reshape(-1) for n in SMALL]), (0, srows * HP - sum(sizes))).reshape(srows, HP)
    sres = _adamw(pack(summed), pack(w), pack(mom), pack(var), "adamw_small")
    o = 0
    for n, k in zip(SMALL, sizes):
        res[n] = [r.reshape(-1)[o:o + k].reshape(w[n].shape) for r in sres]
        o += k
    return (loss, gx[None], *[res[n][0] for n in WEIGHTS], *[res[n][1] for n in WEIGHTS], *[res[n][2] for n in WEIGHTS], *[res[n][3] for n in WEIGHTS])
```

```python
import functools

import jax
import jax.numpy as jnp
from jax import lax
from jax.experimental import pallas as pl
from jax.experimental.pallas import tpu as pltpu

f32, bf16 = jnp.float32, jnp.bfloat16
HI = lax.Precision.HIGHEST
MESH = pl.DeviceIdType.MESH
SDS = jax.ShapeDtypeStruct

N_DEV = 8
N_META = 16
PREFIX = 128
N_PAD = PREFIX - N_META
FOX_H, FOX_D = 8, 64
DN_H, DN_D = 4, 128
DN_C = 64
CONV_K = 4
HP = 128
EPS = 1e-6
NEG = -1e30
C_Q0, C_K0 = 64, 67
LSE_COL = 64

ADAM_LR, ADAM_B1, ADAM_B2, ADAM_EPS, ADAM_WD, ADAM_STEP = 0.001, 0.9, 0.999, 1e-08, 0.01, 10

VMEM_LIMIT_V7X = 56 * 1024 * 1024
ROW_TILES = (384, 128)
ATTN_TILES = (384, 128)
FFN_TILES = (192, 64)
COL_TILES = (512, 256, 128)
ROW_BLOCKS = (704, 512, 256, 128)
DN_INTRA_GROUP = (6, 3, 2, 1)
DN_SCAN_GROUP = (6, 3, 2, 1)


def _pick(n, cands):
    for c in cands:
        if n % c == 0:
            return c
    raise ValueError(f"no tile of {cands} divides {n}")


def _cp(n_axes=1):
    return pltpu.CompilerParams(dimension_semantics=("arbitrary",) * n_axes, vmem_limit_bytes=VMEM_LIMIT_V7X)


def _b(x):
    return x.astype(bf16)


def _dot(a, b):
    return jnp.dot(a, b, preferred_element_type=f32)


def _dot_nt(a, b):
    return lax.dot_general(a, b, (((1,), (1,)), ((), ())), preferred_element_type=f32)


def _dot_tn(a, b):
    return lax.dot_general(a, b, (((0,), (0,)), ((), ())), preferred_element_type=f32)


def _dot_hi(a, b):
    return jnp.dot(a, b, preferred_element_type=f32, precision=HI)


def _iota(shape, dim):
    return lax.broadcasted_iota(jnp.int32, shape, dim)


def _rms(x, w):
    return x * lax.rsqrt(jnp.mean(x * x, axis=-1, keepdims=True) + EPS) * w


def _sigmoid(x):
    return jax.nn.sigmoid(x)


def _load_once(pairs, sems):
    @pl.when(pl.program_id(0) == 0)
    def _():
        cps = [pltpu.make_async_copy(src, dst, sems.at[k]) for k, (src, dst) in enumerate(pairs)]
        for cp in cps:
            cp.start()
        for cp in cps:
            cp.wait()


def _seg_layout(d_model):
    return (("fq", FOX_H * HP, bf16), ("fk", FOX_H * HP, bf16), ("fv", FOX_H * HP, bf16),
            ("dn", 3 * DN_H * DN_D, f32), ("dz", DN_H * DN_D, f32), ("ga", d_model, f32), ("gb", d_model, f32),
            ("sf", HP, f32), ("sd", HP, f32))


def _in_proj(h0, w1, wp):
    L, D = h0.shape
    NP = wp.shape[1]
    TM = _pick(L, ROW_TILES)
    segs = _seg_layout(D)
    offs, o = [], 0
    for _, wd, _ in segs:
        offs.append(o)
        o += wd
    assert o == NP

    def body(h_ref, w1_ref, wp_hbm, xn_ref, *rest):
        outs, (wp_v, sems) = rest[:len(segs)], rest[len(segs):]
        _load_once([(wp_hbm, wp_v)], sems)
        xn = _b(_rms(h_ref[...], w1_ref[...]))
        xn_ref[...] = xn
        for o_ref, off, (_, wd, _) in zip(outs, offs, segs):
            o_ref[...] = _dot(xn, wp_v[:, off:off + wd]).astype(o_ref.dtype)

    row = lambda wd: pl.BlockSpec((TM, wd), lambda i: (i, 0))
    return pl.pallas_call(
        body, name="in_proj", grid=(L // TM,),
        in_specs=[row(D), pl.BlockSpec((1, D), lambda i: (0, 0)), pl.BlockSpec(memory_space=pl.ANY)],
        out_specs=[row(D)] + [row(wd) for _, wd, _ in segs],
        out_shape=[SDS((L, D), bf16)] + [SDS((L, wd), dt) for _, wd, dt in segs],
        scratch_shapes=[pltpu.VMEM((D, NP), bf16), pltpu.SemaphoreType.DMA((1,))],
        compiler_params=_cp())(h0, w1, wp)


def _in_proj_bwd(dproj, wp, h0, w1, dh1):
    L, D = h0.shape
    NP = wp.shape[1]
    TM = _pick(L, ROW_TILES)

    def body(dp_ref, wp_hbm, h_ref, w1_ref, dh1_ref, dh0_ref, acc_ref, wp_v, sems):
        _load_once([(wp_hbm, wp_v)], sems)

        @pl.when(pl.program_id(0) == 0)
        def _():
            acc_ref[...] = jnp.zeros_like(acc_ref)

        dxn = _dot_nt(dp_ref[...], wp_v[...])
        _, vjp = jax.vjp(_rms, h_ref[...], w1_ref[...])
        dh0n, dw1 = vjp(dxn)
        dh0_ref[...] = dh1_ref[...] + dh0n
        acc_ref[0:1, :] += dw1

    row = lambda wd: pl.BlockSpec((TM, wd), lambda i: (i, 0))
    return pl.pallas_call(
        body, name="in_proj_bwd", grid=(L // TM,),
        in_specs=[row(NP), pl.BlockSpec(memory_space=pl.ANY), row(D), pl.BlockSpec((1, D), lambda i: (0, 0)), row(D)],
        out_specs=[row(D), pl.BlockSpec((8, D), lambda i: (0, 0))],
        out_shape=[SDS((L, D), f32), SDS((8, D), f32)],
        scratch_shapes=[pltpu.VMEM((D, NP), bf16), pltpu.SemaphoreType.DMA((1,))],
        compiler_params=_cp())(dproj, wp, h0, w1, dh1)


def _matmul_tn(a, b, name):
    L, M = a.shape
    N = b.shape[1]
    at = a.T

    def body(a_ref, b_ref, o_ref):
        o_ref[...] = _dot(a_ref[...], b_ref[...])

    if M <= N:
        bn = N if N <= COL_TILES[0] else _pick(N, COL_TILES)
        grid, a_spec = (N // bn,), pl.BlockSpec((M, L), lambda n: (0, 0))
        b_spec, o_spec = pl.BlockSpec((L, bn), lambda n: (0, n)), pl.BlockSpec((M, bn), lambda n: (0, n))
    else:
        bm = _pick(M, ROW_BLOCKS)
        grid, a_spec = (M // bm,), pl.BlockSpec((bm, L), lambda m: (m, 0))
        b_spec, o_spec = pl.BlockSpec((L, N), lambda m: (0, 0)), pl.BlockSpec((bm, N), lambda m: (m, 0))
    return pl.pallas_call(body, name=name, grid=grid, in_specs=[a_spec, b_spec], out_specs=o_spec,
                          out_shape=SDS((M, N), f32), compiler_params=_cp())(at, b)


def _fox_prep(fq, fk, sf, bias_p):
    L = fq.shape[0]
    T = HP
    NT = L // T

    def body(fq_ref, fk_ref, sf_ref, b_ref, qa_ref, ka_ref, carry):
        @pl.when(pl.program_id(0) == 0)
        def _():
            carry[...] = jnp.zeros_like(carry)

        lane, row = _iota((T, HP), 1), _iota((T, HP), 0)
        logf = jnp.where(lane < FOX_H, jax.nn.log_sigmoid(sf_ref[...] + b_ref[...]), 0.0)
        c = _dot_hi((row >= lane).astype(f32), logf) + carry[...]
        carry[...] = jnp.sum(jnp.where(row == T - 1, c, 0.0), axis=0, keepdims=True)
        ones_q = jnp.where((lane >= C_K0) & (lane < C_K0 + 3), 1.0, 0.0)
        ones_k = jnp.where((lane >= C_Q0) & (lane < C_Q0 + 3), 1.0, 0.0)
        for h in range(FOX_H):
            ch = jnp.broadcast_to(jnp.sum(jnp.where(lane == h, c, 0.0), axis=1, keepdims=True), (T, HP))
            c1 = _b(ch).astype(f32)
            c2 = _b(ch - c1).astype(f32)
            c3 = _b(ch - c1 - c2).astype(f32)
            cq = jnp.where(lane == C_Q0, c1, 0.0) + jnp.where(lane == C_Q0 + 1, c2, 0.0) + jnp.where(lane == C_Q0 + 2, c3, 0.0)
            ck = jnp.where(lane == C_K0, c1, 0.0) + jnp.where(lane == C_K0 + 1, c2, 0.0) + jnp.where(lane == C_K0 + 2, c3, 0.0)
            q = fq_ref[:, h * HP:(h + 1) * HP].astype(f32) * (FOX_D ** -0.5)
            k = fk_ref[:, h * HP:(h + 1) * HP].astype(f32)
            qa_ref[h] = _b(q + cq + ones_q)
            ka_ref[h] = _b(k + ones_k - ck)

    return pl.pallas_call(
        body, name="fox_prep", grid=(NT,),
        in_specs=[pl.BlockSpec((T, FOX_H * HP), lambda i: (i, 0)), pl.BlockSpec((T, FOX_H * HP), lambda i: (i, 0)),
                  pl.BlockSpec((T, HP), lambda i: (i, 0)), pl.BlockSpec((1, HP), lambda i: (0, 0))],
        out_specs=[pl.BlockSpec((FOX_H, T, HP), lambda i: (0, i, 0))] * 2,
        out_shape=[SDS((FOX_H, L, HP), bf16)] * 2,
        scratch_shapes=[pltpu.VMEM((1, HP), f32)], compiler_params=_cp())(fq, fk, sf, bias_p)


def _fox_prep_bwd(dqa, dka, sf, bias_p):
    L = sf.shape[0]
    T = HP
    NT = L // T
    rev = lambda i: (NT - 1 - i, 0)

    def body(dq_ref, dk_ref, sf_ref, b_ref, dfq_ref, dfk_ref, dsf_ref, db_ref, carry):
        @pl.when(pl.program_id(0) == 0)
        def _():
            carry[...] = jnp.zeros_like(carry)
            db_ref[...] = jnp.zeros_like(db_ref)

        dq, dk = dq_ref[...], dk_ref[...]
        dfq_ref[...] = _b(dq * (FOX_D ** -0.5))
        dfk_ref[...] = _b(dk)
        lane, row = _iota((T, HP), 1), _iota((T, HP), 0)
        dc = jnp.zeros((T, HP), f32)
        for h in range(FOX_H):
            col = jnp.sum(jnp.where(lane == C_Q0, dq[:, h * HP:(h + 1) * HP], 0.0)
                          - jnp.where(lane == C_K0, dk[:, h * HP:(h + 1) * HP], 0.0), axis=1, keepdims=True)
            dc = dc + jnp.where(lane == h, col, 0.0)
        dl = _dot_hi((row <= lane).astype(f32), dc) + carry[...]
        carry[...] = jnp.sum(jnp.where(row == 0, dl, 0.0), axis=0, keepdims=True)
        dx = jnp.where(lane < FOX_H, dl * _sigmoid(-(sf_ref[...] + b_ref[...])), 0.0)
        dsf_ref[...] = _b(dx)
        db_ref[0:1, :] += jnp.sum(dx, axis=0, keepdims=True)

    return pl.pallas_call(
        body, name="fox_prep_bwd", grid=(NT,),
        in_specs=[pl.BlockSpec((T, FOX_H * HP), rev), pl.BlockSpec((T, FOX_H * HP), rev),
                  pl.BlockSpec((T, HP), rev), pl.BlockSpec((1, HP), lambda i: (0, 0))],
        out_specs=[pl.BlockSpec((T, FOX_H * HP), rev), pl.BlockSpec((T, FOX_H * HP), rev),
                   pl.BlockSpec((T, HP), rev), pl.BlockSpec((8, HP), lambda i: (0, 0))],
        out_shape=[SDS((L, FOX_H * HP), bf16), SDS((L, FOX_H * HP), bf16), SDS((L, HP), bf16), SDS((8, HP), f32)],
        scratch_shapes=[pltpu.VMEM((1, HP), f32)], compiler_params=_cp())(dqa, dka, sf, bias_p)


def _fox_fwd(qa, ka, fv):
    L = qa.shape[1]
    TQ = TK = _pick(L, ATTN_TILES)

    def body(q_ref, k_ref, v_ref, o_ref):
        i = pl.program_id(1)
        q = q_ref[0]
        rowg = i * TQ + _iota((TQ, TK), 0)
        colb = _iota((TQ, TK), 1)

        def step(j, carry):
            m, l, acc = carry
            k0 = pl.multiple_of(j * TK, TK)
            kt = k_ref[0, pl.ds(k0, TK), :]
            vt = v_ref[pl.ds(k0, TK), :]
            colg = colb + j * TK
            s = jnp.where((colg <= rowg) & (colg >= N_PAD), _dot_nt(q, kt), NEG)
            m_new = jnp.maximum(m, jnp.max(s, axis=1, keepdims=True))
            p = jnp.exp(s - m_new)
            alpha = jnp.exp(m - m_new)
            l = alpha * l + jnp.sum(p, axis=1, keepdims=True)
            acc = alpha * acc + _dot(_b(p), vt)
            return m_new, l, acc

        m, l, acc = lax.fori_loop(0, i + 1, step, (jnp.full((TQ, 1), NEG, f32), jnp.zeros((TQ, 1), f32), jnp.zeros((TQ, HP), f32)))
        o_ref[...] = jnp.where(_iota((TQ, HP), 1) == LSE_COL, m + jnp.log(l), acc / l)

    return pl.pallas_call(
        body, name="fox_fwd", grid=(FOX_H, L // TQ),
        in_specs=[pl.BlockSpec((1, TQ, HP), lambda h, i: (h, i, 0)), pl.BlockSpec((1, L, HP), lambda h, i: (h, 0, 0)),
                  pl.BlockSpec((L, HP), lambda h, i: (0, h))],
        out_specs=pl.BlockSpec((TQ, HP), lambda h, i: (i, h)),
        out_shape=SDS((L, FOX_H * HP), f32), compiler_params=_cp(2))(qa, ka, fv)


def _fox_bwd(qa, ka, fv, op, dop):
    L = qa.shape[1]
    TQ = TK = _pick(L, ATTN_TILES)
    NQ = L // TQ

    def body(q_ref, k_ref, v_ref, o_ref, do_ref, dq_ref, dk_ref, dv_ref):
        j = pl.program_id(1)

        @pl.when(j == 0)
        def _():
            dq_ref[...] = jnp.zeros_like(dq_ref)

        kt, vt = k_ref[0], v_ref[...]
        colg = j * TK + _iota((TQ, TK), 1)
        rowb = _iota((TQ, TK), 0)
        lane = _iota((TQ, HP), 1)

        def step(i, carry):
            dk, dv = carry
            r0 = pl.multiple_of(i * TQ, TQ)
            q = q_ref[0, pl.ds(r0, TQ), :]
            o = o_ref[pl.ds(r0, TQ), :]
            do = do_ref[pl.ds(r0, TQ), :]
            lse = jnp.sum(jnp.where(lane == LSE_COL, o, 0.0), axis=1, keepdims=True)
            delta = jnp.sum(jnp.where(lane < FOX_D, o * do, 0.0), axis=1, keepdims=True)
            rowg = rowb + i * TQ
            p = jnp.where((colg <= rowg) & (colg >= N_PAD), jnp.exp(_dot_nt(q, kt) - lse), 0.0)
            dob = _b(do)
            dv = dv + _dot_tn(_b(p), dob)
            ds = _b(p * (_dot_nt(dob, vt) - delta))
            dq_ref[pl.ds(r0, TQ), :] += _dot(ds, kt)
            dk = dk + _dot_tn(ds, q)
            return dk, dv

        dk, dv = lax.fori_loop(j, NQ, step, (jnp.zeros((TK, HP), f32), jnp.zeros((TK, HP), f32)))
        dk_ref[...] = dk
        dv_ref[...] = _b(dv)

    head = pl.BlockSpec((L, HP), lambda h, j: (0, h))
    tile = pl.BlockSpec((TK, HP), lambda h, j: (j, h))
    return pl.pallas_call(
        body, name="fox_bwd", grid=(FOX_H, L // TK),
        in_specs=[pl.BlockSpec((1, L, HP), lambda h, j: (h, 0, 0)), pl.BlockSpec((1, TK, HP), lambda h, j: (h, j, 0)), tile, head, head],
        out_specs=[head, tile, tile],
        out_shape=[SDS((L, FOX_H * HP), f32), SDS((L, FOX_H * HP), f32), SDS((L, FOX_H * HP), bf16)],
        compiler_params=_cp(2))(qa, ka, fv, op, dop)


def _dn_post(y, sd, alog_p, dt_p, valid):
    a = y * _sigmoid(y)
    W = DN_H * DN_D
    heads = []
    for part, scale in ((0, DN_D ** -0.5), (1, 1.0)):
        for h in range(DN_H):
            xh = a[:, part * W + h * DN_D:part * W + (h + 1) * DN_D]
            heads.append(xh * lax.rsqrt(jnp.sum(xh * xh, axis=-1, keepdims=True) + EPS) * scale)
    q = jnp.concatenate(heads[:DN_H], axis=1)
    k = jnp.concatenate(heads[DN_H:], axis=1)
    v = a[:, 2 * W:3 * W]
    lane = _iota(sd.shape, 1)
    beta = _sigmoid(sd) * valid
    g = -jnp.exp(alog_p) * jax.nn.softplus(sd + dt_p) * valid
    bg = jnp.where(lane < DN_H, beta, jnp.where(lane < 2 * DN_H, g, 0.0))
    return q, k, v, bg


def _conv_fwd(ext_ref, cw_ref, TM):
    y = cw_ref[0:1, :] * ext_ref[8 - (CONV_K - 1):8 - (CONV_K - 1) + TM, :]
    for i in range(1, CONV_K):
        o = 8 - (CONV_K - 1) + i
        y = y + cw_ref[i:i + 1, :] * ext_ref[o:o + TM, :]
    return y


def _dn_prep(dn, sd, cw, alog_p, dt_p):
    L, W3 = dn.shape
    TM = _pick(L, ROW_TILES)
    W = DN_H * DN_D

    def body(dn_ref, halo_ref, sd_ref, cw_ref, al_ref, dt_ref, q_ref, k_ref, v_ref, bg_ref, ext):
        i = pl.program_id(0)
        ext[0:8, :] = jnp.where(i == 0, 0.0, halo_ref[...])
        ext[8:, :] = dn_ref[...]
        y = _conv_fwd(ext, cw_ref, TM)
        valid = ((i * TM + _iota((TM, 1), 0)) >= N_PAD).astype(f32)
        q, k, v, bg = _dn_post(y, sd_ref[...], al_ref[...], dt_ref[...], valid)
        q_ref[...], k_ref[...], v_ref[...], bg_ref[...] = q, k, v, bg

    row = lambda wd: pl.BlockSpec((TM, wd), lambda i: (i, 0))
    vec = pl.BlockSpec((1, HP), lambda i: (0, 0))
    return pl.pallas_call(
        body, name="dn_prep", grid=(L // TM,),
        in_specs=[row(W3), pl.BlockSpec((8, W3), lambda i: (jnp.maximum(i * (TM // 8) - 1, 0), 0)), row(HP),
                  pl.BlockSpec((CONV_K, W3), lambda i: (0, 0)), vec, vec],
        out_specs=[row(W), row(W), row(W), row(HP)],
        out_shape=[SDS((L, W), f32)] * 3 + [SDS((L, HP), f32)],
        scratch_shapes=[pltpu.VMEM((TM + 8, W3), f32)], compiler_params=_cp())(dn, dn, sd, cw, alog_p, dt_p)


def _dn_prep_bwd(dn, sd, cw, alog_p, dt_p, dq, dk, dv, dbg):
    L, W3 = dn.shape
    TM = _pick(L, ROW_TILES)
    NT = L // TM
    W = DN_H * DN_D

    def body(dn_ref, halo_ref, sd_ref, cw_ref, al_ref, dt_ref, dq_ref, dk_ref, dv_ref, dbg_ref,
             ddn_ref, dsd_ref, dcw_ref, dp_ref, ext, dyp, carry):
        i = pl.program_id(0)
        t = NT - 1 - i

        @pl.when(i == 0)
        def _():
            carry[...] = jnp.zeros_like(carry)
            dcw_ref[...] = jnp.zeros_like(dcw_ref)
            dp_ref[...] = jnp.zeros_like(dp_ref)
            dyp[...] = jnp.zeros_like(dyp)

        ext[0:8, :] = jnp.where(t == 0, 0.0, halo_ref[...])
        ext[8:, :] = dn_ref[...]
        y = _conv_fwd(ext, cw_ref, TM)
        valid = ((t * TM + _iota((TM, 1), 0)) >= N_PAD).astype(f32)
        _, vjp = jax.vjp(functools.partial(_dn_post, valid=valid), y, sd_ref[...], al_ref[...], dt_ref[...])
        dy, dsd, dal, ddt = vjp((dq_ref[...], dk_ref[...], dv_ref[...], dbg_ref[...]))
        dsd_ref[...] = _b(dsd)
        dp_ref[0:1, :] += dal
        dp_ref[1:2, :] += ddt
        dyp[8:8 + TM, :] = dy
        o0 = CONV_K - 1
        dext = cw_ref[0:1, :] * dyp[o0:o0 + TM + 8, :]
        for k in range(1, CONV_K):
            dext = dext + cw_ref[k:k + 1, :] * dyp[o0 - k:o0 - k + TM + 8, :]
        for k in range(CONV_K):
            o = 8 - (CONV_K - 1) + k
            dcw_ref[k:k + 1, :] += jnp.sum(dy * ext[o:o + TM, :], axis=0, keepdims=True)
        ddn_ref[...] = _b(jnp.concatenate([dext[8:TM, :], dext[TM:TM + 8, :] + carry[...]], axis=0))
        carry[...] = dext[0:8, :]

    row = lambda wd: pl.BlockSpec((TM, wd), lambda i: (NT - 1 - i, 0))
    vec = pl.BlockSpec((1, HP), lambda i: (0, 0))
    return pl.pallas_call(
        body, name="dn_prep_bwd", grid=(NT,),
        in_specs=[row(W3), pl.BlockSpec((8, W3), lambda i: (jnp.maximum((NT - 1 - i) * (TM // 8) - 1, 0), 0)), row(HP),
                  pl.BlockSpec((CONV_K, W3), lambda i: (0, 0)), vec, vec, row(W), row(W), row(W), row(HP)],
        out_specs=[row(W3), row(HP), pl.BlockSpec((8, W3), lambda i: (0, 0)), pl.BlockSpec((8, HP), lambda i: (0, 0))],
        out_shape=[SDS((L, W3), bf16), SDS((L, HP), bf16), SDS((8, W3), f32), SDS((8, HP), f32)],
        scratch_shapes=[pltpu.VMEM((TM + 8, W3), f32), pltpu.VMEM((TM + 16, W3), f32), pltpu.VMEM((8, W3), f32)],
        compiler_params=_cp())(dn, dn, sd, cw, alog_p, dt_p, dq, dk, dv, dbg)


def _unit_lower_inverses(lows):
    C = lows[0].shape[0]
    P = jnp.stack(lows)
    X = (_iota((C, C), 0) == _iota((C, C), 1)).astype(f32)[None] - P
    bdot = lambda a, b: jnp.einsum("bij,bjk->bik", a, b, preferred_element_type=f32, precision=HI)
    for _ in range(5):
        P = bdot(P, P)
        X = X + bdot(X, P)
    return [X[i] for i in range(len(lows))]


@jax.custom_vjp
def _inverse_given(low, X):
    return X


def _inverse_given_bwd(X, g):
    t = lax.dot_general(X, g, (((0,), (0,)), ((), ())), preferred_element_type=f32, precision=HI)
    return -lax.dot_general(t, X, (((1,), (1,)), ((), ())), preferred_element_type=f32, precision=HI), jnp.zeros_like(X)


_inverse_given.defvjp(lambda low, X: (X, X), _inverse_given_bwd)


def _dn_intra_pre(q, k, v, bg):
    C = DN_C
    row, col = _iota((C, C), 0), _iota((C, C), 1)
    tri = row >= col
    eye = (row == col).astype(f32)
    G = _dot_hi(tri.astype(f32), bg)
    GT = lax.dot_general(G, eye, (((0,), (0,)), ((), ())), preferred_element_type=f32, precision=HI)
    lane = _iota((C, HP), 1)
    rowt = _iota((HP, C), 0)
    last = _iota((C, 1), 0) == C - 1
    heads = []
    for h in range(DN_H):
        beta = jnp.sum(jnp.where(lane == h, bg, 0.0), axis=1, keepdims=True)
        gcol = jnp.sum(jnp.where(lane == DN_H + h, G, 0.0), axis=1, keepdims=True)
        grow = jnp.sum(jnp.where(rowt == DN_H + h, GT, 0.0), axis=0, keepdims=True)
        glast = jnp.sum(jnp.where(last, gcol, 0.0), axis=0, keepdims=True)
        decay = jnp.exp(jnp.where(tri, gcol - grow, NEG))
        qh, kh, vh = (t[:, h * DN_D:(h + 1) * DN_D] for t in (q, k, v))
        kb = kh * beta
        low = jnp.where(row > col, _dot_nt(_b(kb), _b(kh)) * decay, 0.0)
        heads.append((beta, gcol, glast, decay, qh, kh, vh, kb, low))
    return heads


def _dn_intra_post(heads, xs):
    lane1 = _iota((1, HP), 1)
    us, ws, qds, kds, attns = [], [], [], [], []
    glrow = jnp.zeros((1, HP), f32)
    for h, ((beta, gcol, glast, decay, qh, kh, vh, kb, _), X) in enumerate(zip(heads, xs)):
        eg = jnp.exp(gcol)
        us.append(_dot_hi(X, vh * beta))
        ws.append(_dot_hi(X, kb * eg))
        attns.append(_dot_nt(_b(qh), _b(kh)) * decay)
        qds.append(qh * eg)
        kds.append(kh * jnp.exp(glast - gcol))
        glrow = glrow + jnp.where(lane1 == h, glast, 0.0)
    cat = lambda xs_: jnp.concatenate(xs_, axis=1)
    return cat(us), cat(ws), cat(qds), cat(kds), cat(attns), glrow, cat(list(xs))


def _dn_intra_group(q, k, v, bg, xs):
    G = q.shape[0] // DN_C
    rows = [slice(j * DN_C, (j + 1) * DN_C) for j in range(G)]
    pre = [_dn_intra_pre(q[r, :], k[r, :], v[r, :], bg[r, :]) for r in rows]
    inv = [[_inverse_given(hd[-1], x) for hd, x in zip(heads, xj)] for heads, xj in zip(pre, xs)]
    post = [_dn_intra_post(heads, xj) for heads, xj in zip(pre, inv)]
    return tuple(jnp.concatenate([p[i] for p in post], axis=0) for i in range(5)) + (tuple(p[5] for p in post),)


def _lane_pick(rowvec, h):
    return jnp.sum(jnp.where(_iota(rowvec.shape, 1) == h, rowvec, 0.0), axis=1, keepdims=True)


def _dn_intra(q, k, v, bg):
    L, W = q.shape
    NC = L // DN_C
    G = _pick(NC, DN_INTRA_GROUP)
    R = G * DN_C
    WA = DN_H * DN_C

    def body(q_ref, k_ref, v_ref, bg_ref, u_ref, w_ref, qd_ref, kd_ref, at_ref, gl_ref, x_ref):
        rows = [slice(j * DN_C, (j + 1) * DN_C) for j in range(G)]
        pre = [_dn_intra_pre(q_ref[r, :], k_ref[r, :], v_ref[r, :], bg_ref[r, :]) for r in rows]
        inv = _unit_lower_inverses([hd[-1] for heads in pre for hd in heads])
        for j, r in enumerate(rows):
            u, w, qd, kd, at, gl, xs = _dn_intra_post(pre[j], inv[j * DN_H:(j + 1) * DN_H])
            u_ref[r, :], x_ref[r, :] = u, xs
            w_ref[r, :], qd_ref[r, :], kd_ref[r, :], at_ref[r, :] = _b(w), _b(qd), _b(kd), _b(at)
            gl_ref[j] = gl

    row = lambda wd: pl.BlockSpec((R, wd), lambda n: (n, 0))
    return pl.pallas_call(
        body, name="dn_intra", grid=(NC // G,),
        in_specs=[row(W), row(W), row(W), row(HP)],
        out_specs=[row(W), row(W), row(W), row(W), row(WA), pl.BlockSpec((G, 1, HP), lambda n: (n, 0, 0)), row(WA)],
        out_shape=[SDS((L, W), f32), SDS((L, W), bf16), SDS((L, W), bf16), SDS((L, W), bf16), SDS((L, WA), bf16), SDS((NC, 1, HP), f32),
                   SDS((L, WA), f32)],
        compiler_params=_cp())(q, k, v, bg)


def _dn_scan(u, w, qd, kd, at, gl):
    L, W = u.shape
    NC = L // DN_C
    G = _pick(NC, DN_SCAN_GROUP)
    R = G * DN_C

    def body(u_ref, w_ref, qd_ref, kd_ref, at_ref, gl_ref, o_ref, vn_ref, s_ref, S):
        @pl.when(pl.program_id(0) == 0)
        def _():
            S[...] = jnp.zeros_like(S)

        for j in range(G):
            r = slice(j * DN_C, (j + 1) * DN_C)
            glrow = gl_ref[j]
            for h in range(DN_H):
                c = slice(h * DN_D, (h + 1) * DN_D)
                Sh = S[h]
                s_ref[j, h] = Sh
                Sb = _b(Sh)
                vb = _b(u_ref[r, c] - _dot(w_ref[r, c], Sb))
                vn_ref[r, c] = vb
                o_ref[r, c] = _dot(qd_ref[r, c], Sb) + _dot(at_ref[r, h * DN_C:(h + 1) * DN_C], vb)
                S[h] = Sh * jnp.exp(_lane_pick(glrow, h)) + _dot_tn(kd_ref[r, c], vb)

    row = lambda wd: pl.BlockSpec((R, wd), lambda n: (n, 0))
    return pl.pallas_call(
        body, name="dn_scan", grid=(NC // G,),
        in_specs=[row(W), row(W), row(W), row(W), row(DN_H * DN_C), pl.BlockSpec((G, 1, HP), lambda n: (n, 0, 0))],
        out_specs=[row(W), row(W), pl.BlockSpec((G, DN_H, DN_D, DN_D), lambda n: (n, 0, 0, 0))],
        out_shape=[SDS((L, W), f32), SDS((L, W), bf16), SDS((NC, DN_H, DN_D, DN_D), f32)],
        scratch_shapes=[pltpu.VMEM((DN_H, DN_D, DN_D), f32)], compiler_params=_cp())(u, w, qd, kd, at, gl)


def _dn_scan_bwd(do, w, qd, kd, at, gl):
    L, W = do.shape
    NC = L // DN_C
    G = _pick(NC, DN_SCAN_GROUP)
    R = G * DN_C
    NS = NC // G

    def body(do_ref, w_ref, qd_ref, kd_ref, at_ref, gl_ref, dvn_ref, ds_ref, dS):
        @pl.when(pl.program_id(0) == 0)
        def _():
            dS[...] = jnp.zeros_like(dS)

        for j in reversed(range(G)):
            r = slice(j * DN_C, (j + 1) * DN_C)
            glrow = gl_ref[j]
            for h in range(DN_H):
                c = slice(h * DN_D, (h + 1) * DN_D)
                dSo = dS[h]
                ds_ref[j, h] = dSo
                dob = _b(do_ref[r, c])
                dvn = _dot_tn(at_ref[r, h * DN_C:(h + 1) * DN_C], dob) + _dot(kd_ref[r, c], _b(dSo))
                dvn_ref[r, c] = dvn
                dS[h] = _dot_tn(qd_ref[r, c], dob) + dSo * jnp.exp(_lane_pick(glrow, h)) - _dot_tn(w_ref[r, c], _b(dvn))

    row = lambda wd: pl.BlockSpec((R, wd), lambda n: (NS - 1 - n, 0))
    return pl.pallas_call(
        body, name="dn_scan_bwd", grid=(NS,),
        in_specs=[row(W), row(W), row(W), row(W), row(DN_H * DN_C), pl.BlockSpec((G, 1, HP), lambda n: (NS - 1 - n, 0, 0))],
        out_specs=[row(W), pl.BlockSpec((G, DN_H, DN_D, DN_D), lambda n: (NS - 1 - n, 0, 0, 0))],
        out_shape=[SDS((L, W), f32), SDS((NC, DN_H, DN_D, DN_D), f32)],
        scratch_shapes=[pltpu.VMEM((DN_H, DN_D, DN_D), f32)], compiler_params=_cp())(do, w, qd, kd, at, gl)


def _dn_intra_bwd(q, k, v, bg, xinv, do, vn, dvn, states, dstates):
    L, W = q.shape
    NC = L // DN_C
    G = _pick(NC, DN_INTRA_GROUP)
    R = G * DN_C

    def body(q_ref, k_ref, v_ref, bg_ref, x_ref, do_ref, vn_ref, dvn_ref, s_ref, ds_ref, dq_ref, dk_ref, dv_ref, dbg_ref):
        lane1 = _iota((1, HP), 1)
        rows = [slice(j * DN_C, (j + 1) * DN_C) for j in range(G)]
        xs = [[x_ref[r, h * DN_C:(h + 1) * DN_C] for h in range(DN_H)] for r in rows]
        outs, vjp = jax.vjp(functools.partial(_dn_intra_group, xs=xs), q_ref[...], k_ref[...], v_ref[...], bg_ref[...])
        dws, dqds, dkds, dats, dgls = [], [], [], [], []
        for j, r in enumerate(rows):
            dw, dqd, dkd, dat = [], [], [], []
            dgl = jnp.zeros((1, HP), f32)
            for h in range(DN_H):
                c = slice(h * DN_D, (h + 1) * DN_D)
                Sh, dSo = s_ref[j, h], ds_ref[j, h]
                Sb, dob, vb = _b(Sh), _b(do_ref[r, c]), vn_ref[r, c]
                dw.append(-_dot_nt(_b(dvn_ref[r, c]), Sb))
                dqd.append(_dot_nt(dob, Sb))
                dat.append(_dot_nt(dob, vb))
                dkd.append(_dot_nt(vb, _b(dSo)))
                dcd = jnp.sum(jnp.sum(Sh * dSo, axis=1, keepdims=True), axis=0, keepdims=True)
                dgl = dgl + jnp.where(lane1 == h, dcd * jnp.exp(_lane_pick(outs[5][j], h)), 0.0)
            cat = lambda xs_: jnp.concatenate(xs_, axis=1)
            dws.append(cat(dw)), dqds.append(cat(dqd)), dkds.append(cat(dkd)), dats.append(cat(dat)), dgls.append(dgl)
        cat0 = lambda xs_: jnp.concatenate(xs_, axis=0)
        dq, dk, dv, dbg = vjp((dvn_ref[...], cat0(dws), cat0(dqds), cat0(dkds), cat0(dats), tuple(dgls)))
        dq_ref[...], dk_ref[...], dv_ref[...], dbg_ref[...] = dq, dk, dv, dbg

    row = lambda wd: pl.BlockSpec((R, wd), lambda n: (n, 0))
    st = pl.BlockSpec((G, DN_H, DN_D, DN_D), lambda n: (n, 0, 0, 0))
    return pl.pallas_call(
        body, name="dn_intra_bwd", grid=(NC // G,),
        in_specs=[row(W), row(W), row(W), row(HP), row(DN_H * DN_C), row(W), row(W), row(W), st, st],
        out_specs=[row(W), row(W), row(W), row(HP)],
        out_shape=[SDS((L, W), f32)] * 3 + [SDS((L, HP), f32)],
        compiler_params=_cp())(q, k, v, bg, xinv, do, vn, dvn, states, dstates)


def _dn_normgate(oraw, dz, wn):
    outs = []
    for h in range(DN_H):
        sl = slice(h * DN_D, (h + 1) * DN_D)
        z = dz[:, sl]
        outs.append(_rms(oraw[:, sl], wn) * (z * _sigmoid(z)))
    return jnp.concatenate(outs, axis=1)


def _mix_fwd(op, oraw, dz, ga, gb, h0, wn, wbf, wbd, wo):
    L, D = h0.shape
    TM = _pick(L, ROW_TILES)

    def body(op_ref, or_ref, dz_ref, ga_ref, gb_ref, h0_ref, wn_ref, wbf_ref, wbd_ref, wo_ref, h1_ref):
        pf = _dot(_b(op_ref[...]), wbf_ref[...])
        pd = _dot(_b(_dn_normgate(or_ref[...], dz_ref[...], wn_ref[...])), wbd_ref[...])
        y = _sigmoid(ga_ref[...]) * pf + _sigmoid(gb_ref[...]) * pd
        h1_ref[...] = h0_ref[...] + _dot(_b(y), wo_ref[...])

    row = lambda wd: pl.BlockSpec((TM, wd), lambda i: (i, 0))
    full = lambda a: pl.BlockSpec(a.shape, lambda i: (0, 0))
    return pl.pallas_call(
        body, name="mix_fwd", grid=(L // TM,),
        in_specs=[row(op.shape[1]), row(oraw.shape[1]), row(dz.shape[1]), row(D), row(D), row(D), full(wn), full(wbf), full(wbd), full(wo)],
        out_specs=row(D), out_shape=SDS((L, D), f32), compiler_params=_cp())(op, oraw, dz, ga, gb, h0, wn, wbf, wbd, wo)


def _mix_bwd(dh1, op, oraw, dz, ga, gb, wn, wbf, wbd, wo):
    L, D = dh1.shape
    TM = _pick(L, ROW_TILES)
    WF, WD = op.shape[1], oraw.shape[1]

    def body(dh1_ref, op_ref, or_ref, dz_ref, ga_ref, gb_ref, wn_ref, wbf_ref, wbd_ref, wo_ref,
             dop_ref, dor_ref, ddz_ref, dga_ref, dgb_ref, af_ref, ad_ref, dpf_ref, dpd_ref, y_ref, dmix_ref, acc_ref):
        @pl.when(pl.program_id(0) == 0)
        def _():
            acc_ref[...] = jnp.zeros_like(acc_ref)

        af = _b(op_ref[...])
        ad, vjp = jax.vjp(_dn_normgate, or_ref[...], dz_ref[...], wn_ref[...])
        adb = _b(ad)
        pf, pd = _dot(af, wbf_ref[...]), _dot(adb, wbd_ref[...])
        sa, sb = _sigmoid(ga_ref[...]), _sigmoid(gb_ref[...])
        dmix = _b(dh1_ref[...])
        dy = _dot_nt(dmix, wo_ref[...])
        dpf, dpd = _b(dy * sa), _b(dy * sb)
        dor, ddz, dwn = vjp(_dot_nt(dpd, wbd_ref[...]))
        dop_ref[...] = _dot_nt(dpf, wbf_ref[...])
        dor_ref[...] = dor
        ddz_ref[...] = _b(ddz)
        dga_ref[...] = _b(dy * pf * sa * (1.0 - sa))
        dgb_ref[...] = _b(dy * pd * sb * (1.0 - sb))
        af_ref[...], ad_ref[...], dpf_ref[...], dpd_ref[...] = af, adb, dpf, dpd
        y_ref[...] = _b(sa * pf + sb * pd)
        dmix_ref[...] = dmix
        acc_ref[0:1, :] += dwn

    row = lambda wd: pl.BlockSpec((TM, wd), lambda i: (i, 0))
    full = lambda a: pl.BlockSpec(a.shape, lambda i: (0, 0))
    return pl.pallas_call(
        body, name="mix_bwd", grid=(L // TM,),
        in_specs=[row(D), row(WF), row(WD), row(WD), row(D), row(D), full(wn), full(wbf), full(wbd), full(wo)],
        out_specs=[row(WF), row(WD), row(WD), row(D), row(D), row(WF), row(WD), row(D), row(D), row(D), row(D),
                   pl.BlockSpec((8, HP), lambda i: (0, 0))],
        out_shape=[SDS((L, WF), f32), SDS((L, WD), f32), SDS((L, WD), bf16), SDS((L, D), bf16), SDS((L, D), bf16),
                   SDS((L, WF), bf16), SDS((L, WD), bf16), SDS((L, D), bf16), SDS((L, D), bf16), SDS((L, D), bf16), SDS((L, D), bf16),
                   SDS((8, HP), f32)],
        compiler_params=_cp())(dh1, op, oraw, dz, ga, gb, wn, wbf, wbd, wo)


def _ffn_fwd_bwd(h1, tgt, w2, wf, wg, wu, wd):
    L, D = h1.shape
    F = wg.shape[1]
    TM = _pick(L, FFN_TILES)

    def body(h_ref, t_ref, w2_ref, wf_ref, wg_hbm, wu_hbm, wd_hbm,
             dh1_ref, xn_ref, dg_ref, du_ref, act_ref, dh2_ref, acc_ref, wg_v, wu_v, wd_v, sems):
        i = pl.program_id(0)
        _load_once([(wg_hbm, wg_v), (wu_hbm, wu_v), (wd_hbm, wd_v)], sems)

        @pl.when(i == 0)
        def _():
            acc_ref[...] = jnp.zeros_like(acc_ref)

        h1v = h_ref[...]
        xn2, vjp2 = jax.vjp(_rms, h1v, w2_ref[...])
        xb = _b(xn2)
        g, u = _dot(xb, wg_v[...]), _dot(xb, wu_v[...])
        sg = _sigmoid(g)
        ab = _b(g * sg * u)
        h2 = h1v + _dot(ab, wd_v[...])
        out, vjpf = jax.vjp(_rms, h2, wf_ref[...])
        valid = (i * TM + _iota((TM, 1), 0)) >= PREFIX
        diff = jnp.where(valid, out - t_ref[...], 0.0)
        loss = 0.5 * jnp.sum(jnp.sum(diff * diff, axis=1, keepdims=True), axis=0, keepdims=True) / D
        dh2, dwf = vjpf(diff * (1.0 / D))
        dh2b = _b(dh2)
        dact = _dot_nt(dh2b, wd_v[...])
        dgb = _b(dact * u * (sg * (1.0 + g * (1.0 - sg))))
        dub = _b(dact * (g * sg))
        dh1n, dw2 = vjp2(_dot_nt(dgb, wg_v[...]) + _dot_nt(dub, wu_v[...]))
        dh1_ref[...] = dh2 + dh1n
        xn_ref[...], dg_ref[...], du_ref[...], act_ref[...], dh2_ref[...] = xb, dgb, dub, ab, dh2b
        acc_ref[0:1, :] += dw2
        acc_ref[1:2, :] += dwf
        acc_ref[2:3, :] += jnp.broadcast_to(loss, (1, D))

    row = lambda wd_: pl.BlockSpec((TM, wd_), lambda i: (i, 0))
    vec = pl.BlockSpec((1, D), lambda i: (0, 0))
    anyspec = pl.BlockSpec(memory_space=pl.ANY)
    return pl.pallas_call(
        body, name="ffn_fwd_bwd", grid=(L // TM,),
        in_specs=[row(D), row(D), vec, vec, anyspec, anyspec, anyspec],
        out_specs=[row(D), row(D), row(F), row(F), row(F), row(D), pl.BlockSpec((8, D), lambda i: (0, 0))],
        out_shape=[SDS((L, D), f32), SDS((L, D), bf16), SDS((L, F), bf16), SDS((L, F), bf16), SDS((L, F), bf16), SDS((L, D), bf16),
                   SDS((8, D), f32)],
        scratch_shapes=[pltpu.VMEM((D, F), bf16), pltpu.VMEM((D, F), bf16), pltpu.VMEM((F, D), bf16), pltpu.SemaphoreType.DMA((3,))],
        compiler_params=_cp())(h1, tgt, w2, wf, wg, wu, wd)


def _pad_lanes(v, n=HP):
    return jnp.pad(v.astype(f32), ((0, 0), (0, n - v.shape[1])))


def _pack_w_in(w_full):
    D = w_full.shape[0]
    FW, DW = FOX_H * FOX_D, DN_H * DN_D
    o = 0
    parts = {}
    for name, wd in (("fq", FW), ("fk", FW), ("fv", FW), ("fl", FOX_H), ("dn", 3 * DW), ("ba", 2 * DN_H), ("dz", DW), ("ga", D), ("gb", D)):
        parts[name] = w_full[:, o:o + wd]
        o += wd
    assert o == w_full.shape[1]
    heads = lambda w: jnp.pad(w.reshape(D, FOX_H, FOX_D), ((0, 0), (0, 0), (0, HP - FOX_D))).reshape(D, FOX_H * HP)
    small = lambda w: jnp.pad(w, ((0, 0), (0, HP - w.shape[1])))
    return jnp.concatenate([heads(parts["fq"]), heads(parts["fk"]), heads(parts["fv"]), parts["dn"], parts["dz"], parts["ga"], parts["gb"],
                            small(parts["fl"]), small(parts["ba"])], axis=1)


def _unpack_w_in(gp, d_model):
    D = gp.shape[0]
    FW, DW = FOX_H * FOX_D, DN_H * DN_D
    segs, o = {}, 0
    for name, wd, _ in _seg_layout(d_model):
        segs[name] = gp[:, o:o + wd]
        o += wd
    heads = lambda g: g.reshape(D, FOX_H, HP)[:, :, :FOX_D].reshape(D, FW)
    return jnp.concatenate([heads(segs["fq"]), heads(segs["fk"]), heads(segs["fv"]), segs["sf"][:, :FOX_H], segs["dn"],
                            segs["sd"][:, :2 * DN_H], segs["dz"], segs["ga"], segs["gb"]], axis=1)


def _local_step(x, tgt, meta, w1, w_in, fbias, cw, alog, dtb, wn, wbf, wbd, wo, w2, wg, wu, wd, wf):
    T, D = x.shape
    h0 = jnp.concatenate([jnp.zeros((N_PAD, D), f32), meta, x], axis=0)
    tgt_p = jnp.concatenate([jnp.zeros((PREFIX, D), f32), tgt], axis=0)
    wp = _pack_w_in(w_in)
    wbf_p = jnp.pad(wbf.reshape(FOX_H, FOX_D, D), ((0, 0), (0, HP - FOX_D), (0, 0))).reshape(FOX_H * HP, D)
    bias_p, alog_p, dt_p = _pad_lanes(fbias), _pad_lanes(jnp.pad(alog, ((0, 0), (DN_H, 0)))), _pad_lanes(jnp.pad(dtb, ((0, 0), (DN_H, 0))))

    xn, fq, fk, fv, dn, dz, ga, gb, sf, sd = _in_proj(h0, w1, wp)
    qa, ka = _fox_prep(fq, fk, sf, bias_p)
    op = _fox_fwd(qa, ka, fv)
    qn, kn, vn, bg = _dn_prep(dn, sd, cw, alog_p, dt_p)
    u_dn, w_dn, qd_dn, kd_dn, at_dn, gl_dn, x_dn = _dn_intra(qn, kn, vn, bg)
    oraw, vnew, states = _dn_scan(u_dn, w_dn, qd_dn, kd_dn, at_dn, gl_dn)
    h1 = _mix_fwd(op, oraw, dz, ga, gb, h0, wn, wbf_p, wbd, wo)

    dh1, xn2, dgate, dup, act, dh2, acc_f = _ffn_fwd_bwd(h1, tgt_p, w2, wf, wg, wu, wd)
    g_wg, g_wu, g_wd = _matmul_tn(xn2, dgate, "dw_ffn_gate"), _matmul_tn(xn2, dup, "dw_ffn_up"), _matmul_tn(act, dh2, "dw_ffn_down")

    dop, dor, ddz, dga, dgb, af, ad, dpf, dpd, yb, dmix, acc_m = _mix_bwd(dh1, op, oraw, dz, ga, gb, wn, wbf_p, wbd, wo)
    g_wbf = _matmul_tn(af, dpf, "dw_branch_fox").reshape(FOX_H, HP, D)[:, :FOX_D].reshape(FOX_H * FOX_D, D)
    g_wbd, g_wo = _matmul_tn(ad, dpd, "dw_branch_dn"), _matmul_tn(yb, dmix, "dw_out")

    dvnew, dstates = _dn_scan_bwd(dor, w_dn, qd_dn, kd_dn, at_dn, gl_dn)
    dqn, dkn, dvn, dbg = _dn_intra_bwd(qn, kn, vn, bg, x_dn, dor, vnew, dvnew, states, dstates)
    ddn, dsd, acc_cw, acc_p = _dn_prep_bwd(dn, sd, cw, alog_p, dt_p, dqn, dkn, dvn, dbg)
    dqa, dka, dfv = _fox_bwd(qa, ka, fv, op, dop)
    dfq, dfk, dsf, acc_b = _fox_prep_bwd(dqa, dka, sf, bias_p)

    dproj = jnp.concatenate([dfq, dfk, dfv, ddn, ddz, dga, dgb, dsf, dsd], axis=1)
    g_wp = _matmul_tn(xn, dproj, "dw_in")
    dh0, acc_1 = _in_proj_bwd(dproj, wp, h0, w1, dh1)

    small = dict(loss=acc_f[2, 0:1], mix_norm_w=acc_1[0], fox_forget_bias=acc_b[0, :FOX_H], dn_a_log=acc_p[0, DN_H:2 * DN_H],
                 dn_dt_bias=acc_p[1, DN_H:2 * DN_H], dn_out_norm_w=acc_m[0], ffn_norm_w=acc_f[0], final_norm_w=acc_f[1],
                 meta_tokens=dh0[N_PAD:PREFIX].reshape(-1), dn_conv_w=acc_cw[:CONV_K].reshape(-1))
    big = dict(w_in=_unpack_w_in(g_wp, D), w_branch_fox=g_wbf, w_branch_dn=g_wbd, w_out=g_wo, w_ffn_gate=g_wg, w_ffn_up=g_wu, w_ffn_down=g_wd)
    return dh0[PREFIX:], small, big


def _mesh_pos():
    x, y, c = lax.axis_index("x"), lax.axis_index("y"), lax.axis_index("c")
    return x, y, c, 4 * x + 2 * y + c


def _peer(x, y, c, m):
    flip = lambda v, on: 1 - v if on else v
    px, py, pc = flip(x, m & 4), flip(y, m & 2), flip(c, m & 1)
    return (px, py, pc), 4 * px + 2 * py + pc


def _exchange(arrays, name, gather):
    n = len(arrays)
    shapes = [a.shape if gather else a.shape[1:] for a in arrays]

    def body(*refs):
        ins, outs, (send_sems, recv_sems, loc_sems) = refs[:n], refs[n:2 * n], refs[2 * n:]
        x, y, c, me = _mesh_pos()
        src = lambda a, pid: ins[a] if gather else ins[a].at[pid]
        local = [pltpu.make_async_copy(src(a, me), outs[a].at[me], loc_sems.at[a]) for a in range(n)]
        for cp in local:
            cp.start()
        sends = []
        for m in range(1, N_DEV):
            peer, pid = _peer(x, y, c, m)
            for a in range(n):
                cp = pltpu.make_async_remote_copy(src_ref=src(a, pid), dst_ref=outs[a].at[me], send_sem=send_sems.at[a, m - 1],
                                                  recv_sem=recv_sems.at[a, m - 1], device_id=peer, device_id_type=MESH)
                cp.start()
                sends.append(cp)
        for m in range(1, N_DEV):
            peer, pid = _peer(x, y, c, m)
            for a in range(n):
                pltpu.make_async_remote_copy(src_ref=src(a, pid), dst_ref=outs[a].at[pid], send_sem=send_sems.at[a, m - 1],
                                             recv_sem=recv_sems.at[a, m - 1], device_id=peer, device_id_type=MESH).wait_recv()
        for cp in sends:
            cp.wait_send()
        for cp in local:
            cp.wait()

    anyspec = pl.BlockSpec(memory_space=pl.ANY)
    return pl.pallas_call(
        body, name=name, in_specs=[anyspec] * n, out_specs=[anyspec] * n,
        out_shape=[SDS((N_DEV,) + tuple(s), a.dtype) for s, a in zip(shapes, arrays)],
        scratch_shapes=[pltpu.SemaphoreType.DMA((n, N_DEV - 1)), pltpu.SemaphoreType.DMA((n, N_DEV - 1)), pltpu.SemaphoreType.DMA((n,))],
        )(*arrays)


def _all_reduce_small(v):
    R = v.shape[0]

    def body(v_ref, o_ref, gath, send_sems, recv_sems):
        x, y, c, me = _mesh_pos()
        gath[me] = v_ref[...]
        sends = []
        for m in range(1, N_DEV):
            peer, _ = _peer(x, y, c, m)
            cp = pltpu.make_async_remote_copy(src_ref=v_ref, dst_ref=gath.at[me], send_sem=send_sems.at[m - 1],
                                              recv_sem=recv_sems.at[m - 1], device_id=peer, device_id_type=MESH)
            cp.start()
            sends.append(cp)
        for m in range(1, N_DEV):
            peer, pid = _peer(x, y, c, m)
            pltpu.make_async_remote_copy(src_ref=v_ref, dst_ref=gath.at[pid], send_sem=send_sems.at[m - 1],
                                         recv_sem=recv_sems.at[m - 1], device_id=peer, device_id_type=MESH).wait_recv()
        for cp in sends:
            cp.wait_send()
        tot = gath[0]
        for d in range(1, N_DEV):
            tot = tot + gath[d]
        o_ref[...] = tot

    vm = pl.BlockSpec(memory_space=pltpu.VMEM)
    return pl.pallas_call(
        body, name="all_reduce_small", in_specs=[vm], out_specs=vm, out_shape=SDS((R, HP), f32),
        scratch_shapes=[pltpu.VMEM((N_DEV, R, HP), f32), pltpu.SemaphoreType.DMA((N_DEV - 1,)), pltpu.SemaphoreType.DMA((N_DEV - 1,))],
        )(v)


def _adamw_math(w, g, m, v):
    m = ADAM_B1 * m + (1.0 - ADAM_B1) * g
    v = ADAM_B2 * v + (1.0 - ADAM_B2) * (g * g)
    m_hat = m / (1.0 - ADAM_B1 ** ADAM_STEP)
    v_hat = v / (1.0 - ADAM_B2 ** ADAM_STEP)
    return -ADAM_LR * (m_hat / (jnp.sqrt(v_hat) + ADAM_EPS) + ADAM_WD * w), m, v


def _adamw(g, w, m, v, name):
    R, Cc = w.shape
    TR = R if R <= 512 else _pick(R, (256, 128))
    slabs = g.ndim == 3

    def body(g_ref, w_ref, m_ref, v_ref, go_ref, d_ref, mo_ref, vo_ref):
        if slabs:
            gs = g_ref[0].astype(f32)
            for k in range(1, N_DEV):
                gs = gs + g_ref[k].astype(f32)
        else:
            gs = g_ref[...]
        d, mn, vn = _adamw_math(w_ref[...], gs, m_ref[...], v_ref[...])
        go_ref[...], d_ref[...], mo_ref[...], vo_ref[...] = gs, d, mn, vn

    blk = pl.BlockSpec((TR, Cc), lambda i: (i, 0))
    gblk = pl.BlockSpec((N_DEV, TR, Cc), lambda i: (0, i, 0)) if slabs else blk
    return pl.pallas_call(
        body, name=name, grid=(R // TR,), in_specs=[gblk, blk, blk, blk], out_specs=[blk] * 4,
        out_shape=[SDS((R, Cc), f32)] * 4, compiler_params=_cp())(g, w, m, v)


WEIGHTS = ("meta_tokens", "mix_norm_w", "w_in", "fox_forget_bias", "dn_conv_w", "dn_a_log", "dn_dt_bias", "dn_out_norm_w",
           "w_branch_fox", "w_branch_dn", "w_out", "ffn_norm_w", "w_ffn_gate", "w_ffn_up", "w_ffn_down", "final_norm_w")
COL_SHARDED = ("w_in", "w_branch_fox", "w_branch_dn", "w_ffn_gate", "w_ffn_up")
ROW_SHARDED = ("w_out", "w_ffn_down")
BIG = COL_SHARDED + ROW_SHARDED
SMALL = tuple(n for n in WEIGHTS if n not in BIG)


def _to_slabs(name, g):
    r, c = g.shape
    if name in COL_SHARDED:
        return _b(g.reshape(r, N_DEV, c // N_DEV).transpose(1, 0, 2))
    return _b(g.reshape(N_DEV, r // N_DEV, c))


def _from_slabs(name, s):
    n, r, c = s.shape
    if name in COL_SHARDED:
        return s.transpose(1, 0, 2).reshape(r, n * c)
    return s.reshape(n * r, c)


def kernel(x, meta_tokens, mix_norm_w, w_in, fox_forget_bias, dn_conv_w, dn_a_log, dn_dt_bias, dn_out_norm_w, w_branch_fox, w_branch_dn, w_out, ffn_norm_w, w_ffn_gate, w_ffn_up, w_ffn_down, final_norm_w, loss_target, m_meta_tokens, m_mix_norm_w, m_w_in, m_fox_forget_bias, m_dn_conv_w, m_dn_a_log, m_dn_dt_bias, m_dn_out_norm_w, m_w_branch_fox, m_w_branch_dn, m_w_out, m_ffn_norm_w, m_w_ffn_gate, m_w_ffn_up, m_w_ffn_down, m_final_norm_w, v_meta_tokens, v_mix_norm_w, v_w_in, v_fox_forget_bias, v_dn_conv_w, v_dn_a_log, v_dn_dt_bias, v_dn_out_norm_w, v_w_branch_fox, v_w_branch_dn, v_w_out, v_ffn_norm_w, v_w_ffn_gate, v_w_ffn_up, v_w_ffn_down, v_final_norm_w):
    w = dict(meta_tokens=meta_tokens, mix_norm_w=mix_norm_w, w_in=w_in, fox_forget_bias=fox_forget_bias, dn_conv_w=dn_conv_w, dn_a_log=dn_a_log, dn_dt_bias=dn_dt_bias, dn_out_norm_w=dn_out_norm_w, w_branch_fox=w_branch_fox, w_branch_dn=w_branch_dn, w_out=w_out, ffn_norm_w=ffn_norm_w, w_ffn_gate=w_ffn_gate, w_ffn_up=w_ffn_up, w_ffn_down=w_ffn_down, final_norm_w=final_norm_w)
    mom = dict(meta_tokens=m_meta_tokens, mix_norm_w=m_mix_norm_w, w_in=m_w_in, fox_forget_bias=m_fox_forget_bias, dn_conv_w=m_dn_conv_w, dn_a_log=m_dn_a_log, dn_dt_bias=m_dn_dt_bias, dn_out_norm_w=m_dn_out_norm_w, w_branch_fox=m_w_branch_fox, w_branch_dn=m_w_branch_dn, w_out=m_w_out, ffn_norm_w=m_ffn_norm_w, w_ffn_gate=m_w_ffn_gate, w_ffn_up=m_w_ffn_up, w_ffn_down=m_w_ffn_down, final_norm_w=m_final_norm_w)
    var = dict(meta_tokens=v_meta_tokens, mix_norm_w=v_mix_norm_w, w_in=v_w_in, fox_forget_bias=v_fox_forget_bias, dn_conv_w=v_dn_conv_w, dn_a_log=v_dn_a_log, dn_dt_bias=v_dn_dt_bias, dn_out_norm_w=v_dn_out_norm_w, w_branch_fox=v_w_branch_fox, w_branch_dn=v_w_branch_dn, w_out=v_w_out, ffn_norm_w=v_ffn_norm_w, w_ffn_gate=v_w_ffn_gate, w_ffn_up=v_w_ffn_up, w_ffn_down=v_w_ffn_down, final_norm_w=v_final_norm_w)
    two_d = lambda a: a.reshape(a.shape[-2:]) if a.ndim >= 2 else a.reshape(1, -1)
    me = 4 * lax.axis_index("x") + 2 * lax.axis_index("y") + lax.axis_index("c")

    gathered = _exchange([_b(two_d(w[n])) for n in BIG] + [two_d(w["meta_tokens"]), two_d(w["dn_conv_w"])], "all_gather_weights", gather=True)
    full = {n: _from_slabs(n, s) for n, s in zip(BIG, gathered)}
    meta = gathered[len(BIG)].transpose(1, 0, 2).reshape(N_META, -1)
    cw = gathered[len(BIG) + 1].transpose(1, 0, 2).reshape(CONV_K, -1)

    gx, g_small, g_big = _local_step(
        x[0], loss_target[0], meta, two_d(w["mix_norm_w"]), full["w_in"], two_d(w["fox_forget_bias"]), cw, two_d(w["dn_a_log"]),
        two_d(w["dn_dt_bias"]), two_d(w["dn_out_norm_w"]), full["w_branch_fox"], full["w_branch_dn"], full["w_out"], two_d(w["ffn_norm_w"]),
        full["w_ffn_gate"], full["w_ffn_up"], full["w_ffn_down"], two_d(w["final_norm_w"]))

    recv = dict(zip(BIG, _exchange([_to_slabs(n, g_big[n]) for n in BIG], "exchange_weight_grads", gather=False)))
    order = ("loss",) + SMALL
    flat = jnp.concatenate([g_small[n].reshape(-1) for n in order])
    rows = -(-flat.shape[0] // (8 * HP)) * 8
    tot = _all_reduce_small(jnp.pad(flat, (0, rows * HP - flat.shape[0])).reshape(rows, HP)).reshape(-1)
    summed, o = {}, 0
    for n in order:
        k = g_small[n].shape[0]
        summed[n] = tot[o:o + k]
        o += k
    loss = summed["loss"][0]
    d_model = x.shape[-1]
    mcols, ccols = d_model // N_DEV, dn_conv_w.shape[-1]
    summed["meta_tokens"] = lax.dynamic_slice(summed["meta_tokens"].reshape(N_META, d_model), (0, me * mcols), (N_META, mcols)).reshape(-1)
    summed["dn_conv_w"] = lax.dynamic_slice(summed["dn_conv_w"].reshape(CONV_K, ccols * N_DEV), (0, me * ccols), (CONV_K, ccols)).reshape(-1)

    res = {}
    for n in BIG:
        res[n] = [r.reshape(w[n].shape) for r in _adamw(recv[n], two_d(w[n]), two_d(mom[n]), two_d(var[n]), "adamw_" + n)]
    sizes = [summed[n].shape[0] for n in SMALL]
    srows = -(-sum(sizes) // (8 * HP)) * 8
    pack = lambda d: jnp.pad(jnp.concatenate([d[n].reshape(-1) for n in SMALL]), (0, srows * HP - sum(sizes))).reshape(srows, HP)
    sres = _adamw(pack(summed), pack(w), pack(mom), pack(var), "adamw_small")
    o = 0
    for n, k in zip(SMALL, sizes):
        res[n] = [r.reshape(-1)[o:o + k].reshape(w[n].shape) for r in sres]
        o += k
    return (loss, gx[None], *[res[n][0] for n in WEIGHTS], *[res[n][1] for n in WEIGHTS], *[res[n][2] for n in WEIGHTS], *[res[n][3] for n in WEIGHTS])
```

```python
import functools

import jax
import jax.numpy as jnp
from jax import lax
from jax.experimental import pallas as pl
from jax.experimental.pallas import tpu as pltpu

f32, bf16 = jnp.float32, jnp.bfloat16
HI = lax.Precision.HIGHEST
MESH = pl.DeviceIdType.MESH
SDS = jax.ShapeDtypeStruct

N_DEV = 8
N_META = 16
PREFIX = 128
N_PAD = PREFIX - N_META
FOX_H, FOX_D = 8, 64
DN_H, DN_D = 4, 128
DN_C = 64
CONV_K = 4
HP = 128
EPS = 1e-6
NEG = -1e30
C_Q0, C_K0 = 64, 67
LSE_COL = 64

ADAM_LR, ADAM_B1, ADAM_B2, ADAM_EPS, ADAM_WD, ADAM_STEP = 0.001, 0.9, 0.999, 1e-08, 0.01, 10

VMEM_LIMIT_V7X = 56 * 1024 * 1024
ROW_TILES = (384, 128)
ATTN_TILES = (384, 128)
FFN_TILES = (192, 64)
COL_TILES = (512, 256, 128)
ROW_BLOCKS = (704, 512, 256, 128)
DN_INTRA_GROUP = (6, 3, 2, 1)
DN_SCAN_GROUP = (6, 3, 2, 1)


def _pick(n, cands):
    for c in cands:
        if n % c == 0:
            return c
    raise ValueError(f"no tile of {cands} divides {n}")


def _cp(n_axes=1):
    return pltpu.CompilerParams(dimension_semantics=("arbitrary",) * n_axes, vmem_limit_bytes=VMEM_LIMIT_V7X)


def _b(x):
    return x.astype(bf16)


def _dot(a, b):
    return jnp.dot(a, b, preferred_element_type=f32)


def _dot_nt(a, b):
    return lax.dot_general(a, b, (((1,), (1,)), ((), ())), preferred_element_type=f32)


def _dot_tn(a, b):
    return lax.dot_general(a, b, (((0,), (0,)), ((), ())), preferred_element_type=f32)


def _dot_hi(a, b):
    return jnp.dot(a, b, preferred_element_type=f32, precision=HI)


def _iota(shape, dim):
    return lax.broadcasted_iota(jnp.int32, shape, dim)


def _rms(x, w):
    return x * lax.rsqrt(jnp.mean(x * x, axis=-1, keepdims=True) + EPS) * w


def _sigmoid(x):
    return jax.nn.sigmoid(x)


def _load_once(pairs, sems):
    @pl.when(pl.program_id(0) == 0)
    def _():
        cps = [pltpu.make_async_copy(src, dst, sems.at[k]) for k, (src, dst) in enumerate(pairs)]
        for cp in cps:
            cp.start()
        for cp in cps:
            cp.wait()


def _seg_layout(d_model):
    return (("fq", FOX_H * HP, bf16), ("fk", FOX_H * HP, bf16), ("fv", FOX_H * HP, bf16),
            ("dn", 3 * DN_H * DN_D, f32), ("dz", DN_H * DN_D, f32), ("ga", d_model, f32), ("gb", d_model, f32),
            ("sf", HP, f32), ("sd", HP, f32))


def _in_proj(h0, w1, wp):
    L, D = h0.shape
    NP = wp.shape[1]
    TM = _pick(L, ROW_TILES)
    segs = _seg_layout(D)
    offs, o = [], 0
    for _, wd, _ in segs:
        offs.append(o)
        o += wd
    assert o == NP

    def body(h_ref, w1_ref, wp_hbm, xn_ref, *rest):
        outs, (wp_v, sems) = rest[:len(segs)], rest[len(segs):]
        _load_once([(wp_hbm, wp_v)], sems)
        xn = _b(_rms(h_ref[...], w1_ref[...]))
        xn_ref[...] = xn
        for o_ref, off, (_, wd, _) in zip(outs, offs, segs):
            o_ref[...] = _dot(xn, wp_v[:, off:off + wd]).astype(o_ref.dtype)

    row = lambda wd: pl.BlockSpec((TM, wd), lambda i: (i, 0))
    return pl.pallas_call(
        body, name="in_proj", grid=(L // TM,),
        in_specs=[row(D), pl.BlockSpec((1, D), lambda i: (0, 0)), pl.BlockSpec(memory_space=pl.ANY)],
        out_specs=[row(D)] + [row(wd) for _, wd, _ in segs],
        out_shape=[SDS((L, D), bf16)] + [SDS((L, wd), dt) for _, wd, dt in segs],
        scratch_shapes=[pltpu.VMEM((D, NP), bf16), pltpu.SemaphoreType.DMA((1,))],
        compiler_params=_cp())(h0, w1, wp)


def _in_proj_bwd(dproj, wp, h0, w1, dh1):
    L, D = h0.shape
    NP = wp.shape[1]
    TM = _pick(L, ROW_TILES)

    def body(dp_ref, wp_hbm, h_ref, w1_ref, dh1_ref, dh0_ref, acc_ref, wp_v, sems):
        _load_once([(wp_hbm, wp_v)], sems)

        @pl.when(pl.program_id(0) == 0)
        def _():
            acc_ref[...] = jnp.zeros_like(acc_ref)

        dxn = _dot_nt(dp_ref[...], wp_v[...])
        _, vjp = jax.vjp(_rms, h_ref[...], w1_ref[...])
        dh0n, dw1 = vjp(dxn)
        dh0_ref[...] = dh1_ref[...] + dh0n
        acc_ref[0:1, :] += dw1

    row = lambda wd: pl.BlockSpec((TM, wd), lambda i: (i, 0))
    return pl.pallas_call(
        body, name="in_proj_bwd", grid=(L // TM,),
        in_specs=[row(NP), pl.BlockSpec(memory_space=pl.ANY), row(D), pl.BlockSpec((1, D), lambda i: (0, 0)), row(D)],
        out_specs=[row(D), pl.BlockSpec((8, D), lambda i: (0, 0))],
        out_shape=[SDS((L, D), f32), SDS((8, D), f32)],
        scratch_shapes=[pltpu.VMEM((D, NP), bf16), pltpu.SemaphoreType.DMA((1,))],
        compiler_params=_cp())(dproj, wp, h0, w1, dh1)


def _matmul_tn(a, b, name):
    L, M = a.shape
    N = b.shape[1]
    at = a.T

    def body(a_ref, b_ref, o_ref):
        o_ref[...] = _dot(a_ref[...], b_ref[...])

    if M <= N:
        bn = N if N <= COL_TILES[0] else _pick(N, COL_TILES)
        grid, a_spec = (N // bn,), pl.BlockSpec((M, L), lambda n: (0, 0))
        b_spec, o_spec = pl.BlockSpec((L, bn), lambda n: (0, n)), pl.BlockSpec((M, bn), lambda n: (0, n))
    else:
        bm = _pick(M, ROW_BLOCKS)
        grid, a_spec = (M // bm,), pl.BlockSpec((bm, L), lambda m: (m, 0))
        b_spec, o_spec = pl.BlockSpec((L, N), lambda m: (0, 0)), pl.BlockSpec((bm, N), lambda m: (m, 0))
    return pl.pallas_call(body, name=name, grid=grid, in_specs=[a_spec, b_spec], out_specs=o_spec,
                          out_shape=SDS((M, N), f32), compiler_params=_cp())(at, b)


def _fox_prep(fq, fk, sf, bias_p):
    L = fq.shape[0]
    T = HP
    NT = L // T

    def body(fq_ref, fk_ref, sf_ref, b_ref, qa_ref, ka_ref, carry):
        @pl.when(pl.program_id(0) == 0)
        def _():
            carry[...] = jnp.zeros_like(carry)

        lane, row = _iota((T, HP), 1), _iota((T, HP), 0)
        logf = jnp.where(lane < FOX_H, jax.nn.log_sigmoid(sf_ref[...] + b_ref[...]), 0.0)
        c = _dot_hi((row >= lane).astype(f32), logf) + carry[...]
        carry[...] = jnp.sum(jnp.where(row == T - 1, c, 0.0), axis=0, keepdims=True)
        ones_q = jnp.where((lane >= C_K0) & (lane < C_K0 + 3), 1.0, 0.0)
        ones_k = jnp.where((lane >= C_Q0) & (lane < C_Q0 + 3), 1.0, 0.0)
        for h in range(FOX_H):
            ch = jnp.broadcast_to(jnp.sum(jnp.where(lane == h, c, 0.0), axis=1, keepdims=True), (T, HP))
            c1 = _b(ch).astype(f32)
            c2 = _b(ch - c1).astype(f32)
            c3 = _b(ch - c1 - c2).astype(f32)
            cq = jnp.where(lane == C_Q0, c1, 0.0) + jnp.where(lane == C_Q0 + 1, c2, 0.0) + jnp.where(lane == C_Q0 + 2, c3, 0.0)
            ck = jnp.where(lane == C_K0, c1, 0.0) + jnp.where(lane == C_K0 + 1, c2, 0.0) + jnp.where(lane == C_K0 + 2, c3, 0.0)
            q = fq_ref[:, h * HP:(h + 1) * HP].astype(f32) * (FOX_D ** -0.5)
            k = fk_ref[:, h * HP:(h + 1) * HP].astype(f32)
            qa_ref[h] = _b(q + cq + ones_q)
            ka_ref[h] = _b(k + ones_k - ck)

    return pl.pallas_call(
        body, name="fox_prep", grid=(NT,),
        in_specs=[pl.BlockSpec((T, FOX_H * HP), lambda i: (i, 0)), pl.BlockSpec((T, FOX_H * HP), lambda i: (i, 0)),
                  pl.BlockSpec((T, HP), lambda i: (i, 0)), pl.BlockSpec((1, HP), lambda i: (0, 0))],
        out_specs=[pl.BlockSpec((FOX_H, T, HP), lambda i: (0, i, 0))] * 2,
        out_shape=[SDS((FOX_H, L, HP), bf16)] * 2,
        scratch_shapes=[pltpu.VMEM((1, HP), f32)], compiler_params=_cp())(fq, fk, sf, bias_p)


def _fox_prep_bwd(dqa, dka, sf, bias_p):
    L = sf.shape[0]
    T = HP
    NT = L // T
    rev = lambda i: (NT - 1 - i, 0)

    def body(dq_ref, dk_ref, sf_ref, b_ref, dfq_ref, dfk_ref, dsf_ref, db_ref, carry):
        @pl.when(pl.program_id(0) == 0)
        def _():
            carry[...] = jnp.zeros_like(carry)
            db_ref[...] = jnp.zeros_like(db_ref)

        dq, dk = dq_ref[...], dk_ref[...]
        dfq_ref[...] = _b(dq * (FOX_D ** -0.5))
        dfk_ref[...] = _b(dk)
        lane, row = _iota((T, HP), 1), _iota((T, HP), 0)
        dc = jnp.zeros((T, HP), f32)
        for h in range(FOX_H):
            col = jnp.sum(jnp.where(lane == C_Q0, dq[:, h * HP:(h + 1) * HP], 0.0)
                          - jnp.where(lane == C_K0, dk[:, h * HP:(h + 1) * HP], 0.0), axis=1, keepdims=True)
            dc = dc + jnp.where(lane == h, col, 0.0)
        dl = _dot_hi((row <= lane).astype(f32), dc) + carry[...]
        carry[...] = jnp.sum(jnp.where(row == 0, dl, 0.0), axis=0, keepdims=True)
        dx = jnp.where(lane < FOX_H, dl * _sigmoid(-(sf_ref[...] + b_ref[...])), 0.0)
        dsf_ref[...] = _b(dx)
        db_ref[0:1, :] += jnp.sum(dx, axis=0, keepdims=True)

    return pl.pallas_call(
        body, name="fox_prep_bwd", grid=(NT,),
        in_specs=[pl.BlockSpec((T, FOX_H * HP), rev), pl.BlockSpec((T, FOX_H * HP), rev),
                  pl.BlockSpec((T, HP), rev), pl.BlockSpec((1, HP), lambda i: (0, 0))],
        out_specs=[pl.BlockSpec((T, FOX_H * HP), rev), pl.BlockSpec((T, FOX_H * HP), rev),
                   pl.BlockSpec((T, HP), rev), pl.BlockSpec((8, HP), lambda i: (0, 0))],
        out_shape=[SDS((L, FOX_H * HP), bf16), SDS((L, FOX_H * HP), bf16), SDS((L, HP), bf16), SDS((8, HP), f32)],
        scratch_shapes=[pltpu.VMEM((1, HP), f32)], compiler_params=_cp())(dqa, dka, sf, bias_p)


def _tile_start(j, T):
    return j * T if isinstance(j, int) else pl.multiple_of(j * T, T)


def _fox_fwd(qa, ka, fv, shards):
    L = qa.shape[1]
    TQ = TK = _pick(L, ATTN_TILES)
    NQ = L // TQ
    n = len(shards)

    def body(q_ref, k_ref, v_ref, *rest):
        ins, o_ref, outs, sems = rest[:n], rest[n], rest[n + 1:2 * n + 1], rest[2 * n + 1:]
        h, i = pl.program_id(0), pl.program_id(1)

        @pl.when((h == 0) & (i == 0))
        def _():
            _exchange_start(ins, outs, *sems, gather=True)

        q = q_ref[0]
        rowg = i * TQ + _iota((TQ, TK), 0)
        colb = _iota((TQ, TK), 1)

        def step(j, carry, masked):
            m, l, acc = carry
            k0 = _tile_start(j, TK)
            kt = k_ref[0, pl.ds(k0, TK), :]
            vt = v_ref[pl.ds(k0, TK), :]
            s = _dot_nt(q, kt)
            if masked:
                colg = colb + j * TK
                s = jnp.where((colg <= rowg) & (colg >= N_PAD), s, NEG)
            m_new = jnp.maximum(m, jnp.max(s, axis=1, keepdims=True))
            p = jnp.exp(s - m_new)
            alpha = jnp.exp(m - m_new)
            l = alpha * l + jnp.sum(p, axis=1, keepdims=True)
            acc = alpha * acc + _dot(_b(p), vt)
            return m_new, l, acc

        carry = step(0, (jnp.full((TQ, 1), NEG, f32), jnp.zeros((TQ, 1), f32), jnp.zeros((TQ, HP), f32)), True)
        carry = lax.fori_loop(1, i, functools.partial(step, masked=False), carry)
        m, l, acc = lax.fori_loop(jnp.maximum(i, 1), i + 1, functools.partial(step, masked=True), carry)
        o_ref[...] = jnp.where(_iota((TQ, HP), 1) == LSE_COL, m + jnp.log(l), acc / l)

        @pl.when((h == FOX_H - 1) & (i == NQ - 1))
        def _():
            _exchange_wait(ins, outs, *sems, gather=True)

    anyspec = pl.BlockSpec(memory_space=pl.ANY)
    res = pl.pallas_call(
        body, name="fox_fwd", grid=(FOX_H, NQ),
        in_specs=[pl.BlockSpec((1, TQ, HP), lambda h, i: (h, i, 0)), pl.BlockSpec((1, L, HP), lambda h, i: (h, 0, 0)),
                  pl.BlockSpec((L, HP), lambda h, i: (0, h))] + [anyspec] * n,
        out_specs=[pl.BlockSpec((TQ, HP), lambda h, i: (i, h))] + [anyspec] * n,
        out_shape=[SDS((L, FOX_H * HP), f32)] + [SDS((N_DEV,) + a.shape, a.dtype) for a in shards],
        scratch_shapes=_exchange_sems(n), compiler_params=_cp(2))(qa, ka, fv, *shards)
    return res[0], res[1:]


def _fox_bwd(qa, ka, fv, op, dop, slabs):
    L = qa.shape[1]
    TQ = TK = _pick(L, ATTN_TILES)
    NQ = L // TQ
    n = len(slabs)

    def body(q_ref, k_ref, v_ref, o_ref, do_ref, *rest):
        ins, (dq_ref, dk_ref, dv_ref), outs = rest[:n], rest[n:n + 3], rest[n + 3:2 * n + 3]
        lse_s, delta_s = rest[2 * n + 3:2 * n + 5]
        sems = rest[2 * n + 5:]
        h, j = pl.program_id(0), pl.program_id(1)

        @pl.when((h == 0) & (j == 0))
        def _():
            _exchange_start(ins, outs, *sems, gather=False)

        lane = _iota((TQ, HP), 1)

        @pl.when(j == 0)
        def _():
            dq_ref[...] = jnp.zeros_like(dq_ref)
            for t in range(NQ):
                o, do = o_ref[t * TQ:(t + 1) * TQ, :], do_ref[t * TQ:(t + 1) * TQ, :]
                lse_s[t * TQ:(t + 1) * TQ, :] = jnp.sum(jnp.where(lane == LSE_COL, o, 0.0), axis=1, keepdims=True)
                delta_s[t * TQ:(t + 1) * TQ, :] = jnp.sum(jnp.where(lane < FOX_D, o * do, 0.0), axis=1, keepdims=True)

        kt, vt = k_ref[0], v_ref[...]
        colg = j * TK + _iota((TQ, TK), 1)
        rowb = _iota((TQ, TK), 0)

        def step(i, carry, masked):
            dk, dv = carry
            r0 = _tile_start(i, TQ)
            q = q_ref[0, pl.ds(r0, TQ), :]
            p = jnp.exp(_dot_nt(q, kt) - lse_s[pl.ds(r0, TQ), :])
            if masked:
                p = jnp.where((colg <= rowb + i * TQ) & (colg >= N_PAD), p, 0.0)
            dob = _b(do_ref[pl.ds(r0, TQ), :])
            dv = dv + _dot_tn(_b(p), dob)
            ds = _b(p * (_dot_nt(dob, vt) - delta_s[pl.ds(r0, TQ), :]))
            dq_ref[pl.ds(r0, TQ), :] += _dot(ds, kt)
            dk = dk + _dot_tn(ds, q)
            return dk, dv

        carry = step(j, (jnp.zeros((TK, HP), f32), jnp.zeros((TK, HP), f32)), True)
        split = jnp.where(j == 0, NQ, j + 1)
        carry = lax.fori_loop(j + 1, split, functools.partial(step, masked=True), carry)
        dk, dv = lax.fori_loop(split, NQ, functools.partial(step, masked=False), carry)
        dk_ref[...] = dk
        dv_ref[...] = _b(dv)

        @pl.when((h == FOX_H - 1) & (j == NQ - 1))
        def _():
            _exchange_wait(ins, outs, *sems, gather=False)

    head = pl.BlockSpec((L, HP), lambda h, j: (0, h))
    tile = pl.BlockSpec((TK, HP), lambda h, j: (j, h))
    anyspec = pl.BlockSpec(memory_space=pl.ANY)
    res = pl.pallas_call(
        body, name="fox_bwd", grid=(FOX_H, L // TK),
        in_specs=[pl.BlockSpec((1, L, HP), lambda h, j: (h, 0, 0)), pl.BlockSpec((1, TK, HP), lambda h, j: (h, j, 0)), tile, head, head]
        + [anyspec] * n,
        out_specs=[head, tile, tile] + [anyspec] * n,
        out_shape=[SDS((L, FOX_H * HP), f32), SDS((L, FOX_H * HP), f32), SDS((L, FOX_H * HP), bf16)] + [SDS(a.shape, a.dtype) for a in slabs],
        scratch_shapes=[pltpu.VMEM((L, 1), f32), pltpu.VMEM((L, 1), f32)] + _exchange_sems(n),
        compiler_params=_cp(2))(qa, ka, fv, op, dop, *slabs)
    return res[:3], res[3:]


def _dn_post(y, sd, alog_p, dt_p, valid):
    a = y * _sigmoid(y)
    W = DN_H * DN_D
    heads = []
    for part, scale in ((0, DN_D ** -0.5), (1, 1.0)):
        for h in range(DN_H):
            xh = a[:, part * W + h * DN_D:part * W + (h + 1) * DN_D]
            heads.append(xh * lax.rsqrt(jnp.sum(xh * xh, axis=-1, keepdims=True) + EPS) * scale)
    q = jnp.concatenate(heads[:DN_H], axis=1)
    k = jnp.concatenate(heads[DN_H:], axis=1)
    v = a[:, 2 * W:3 * W]
    lane = _iota(sd.shape, 1)
    beta = _sigmoid(sd) * valid
    g = -jnp.exp(alog_p) * jax.nn.softplus(sd + dt_p) * valid
    bg = jnp.where(lane < DN_H, beta, jnp.where(lane < 2 * DN_H, g, 0.0))
    return q, k, v, bg


def _conv_fwd(ext_ref, cw_ref, TM):
    y = cw_ref[0:1, :] * ext_ref[8 - (CONV_K - 1):8 - (CONV_K - 1) + TM, :]
    for i in range(1, CONV_K):
        o = 8 - (CONV_K - 1) + i
        y = y + cw_ref[i:i + 1, :] * ext_ref[o:o + TM, :]
    return y


def _dn_prep(dn, sd, cw, alog_p, dt_p):
    L, W3 = dn.shape
    TM = _pick(L, ROW_TILES)
    W = DN_H * DN_D

    def body(dn_ref, halo_ref, sd_ref, cw_ref, al_ref, dt_ref, q_ref, k_ref, v_ref, bg_ref, ext):
        i = pl.program_id(0)
        ext[0:8, :] = jnp.where(i == 0, 0.0, halo_ref[...])
        ext[8:, :] = dn_ref[...]
        y = _conv_fwd(ext, cw_ref, TM)
        valid = ((i * TM + _iota((TM, 1), 0)) >= N_PAD).astype(f32)
        q, k, v, bg = _dn_post(y, sd_ref[...], al_ref[...], dt_ref[...], valid)
        q_ref[...], k_ref[...], v_ref[...], bg_ref[...] = q, k, v, bg

    row = lambda wd: pl.BlockSpec((TM, wd), lambda i: (i, 0))
    vec = pl.BlockSpec((1, HP), lambda i: (0, 0))
    return pl.pallas_call(
        body, name="dn_prep", grid=(L // TM,),
        in_specs=[row(W3), pl.BlockSpec((8, W3), lambda i: (jnp.maximum(i * (TM // 8) - 1, 0), 0)), row(HP),
                  pl.BlockSpec((CONV_K, W3), lambda i: (0, 0)), vec, vec],
        out_specs=[row(W), row(W), row(W), row(HP)],
        out_shape=[SDS((L, W), f32)] * 3 + [SDS((L, HP), f32)],
        scratch_shapes=[pltpu.VMEM((TM + 8, W3), f32)], compiler_params=_cp())(dn, dn, sd, cw, alog_p, dt_p)


def _dn_prep_bwd(dn, sd, cw, alog_p, dt_p, dq, dk, dv, dbg):
    L, W3 = dn.shape
    TM = _pick(L, ROW_TILES)
    NT = L // TM
    W = DN_H * DN_D

    def body(dn_ref, halo_ref, sd_ref, cw_ref, al_ref, dt_ref, dq_ref, dk_ref, dv_ref, dbg_ref,
             ddn_ref, dsd_ref, dcw_ref, dp_ref, ext, dyp, carry):
        i = pl.program_id(0)
        t = NT - 1 - i

        @pl.when(i == 0)
        def _():
            carry[...] = jnp.zeros_like(carry)
            dcw_ref[...] = jnp.zeros_like(dcw_ref)
            dp_ref[...] = jnp.zeros_like(dp_ref)
            dyp[...] = jnp.zeros_like(dyp)

        ext[0:8, :] = jnp.where(t == 0, 0.0, halo_ref[...])
        ext[8:, :] = dn_ref[...]
        y = _conv_fwd(ext, cw_ref, TM)
        valid = ((t * TM + _iota((TM, 1), 0)) >= N_PAD).astype(f32)
        _, vjp = jax.vjp(functools.partial(_dn_post, valid=valid), y, sd_ref[...], al_ref[...], dt_ref[...])
        dy, dsd, dal, ddt = vjp((dq_ref[...], dk_ref[...], dv_ref[...], dbg_ref[...]))
        dsd_ref[...] = _b(dsd)
        dp_ref[0:1, :] += dal
        dp_ref[1:2, :] += ddt
        dyp[8:8 + TM, :] = dy
        o0 = CONV_K - 1
        dext = cw_ref[0:1, :] * dyp[o0:o0 + TM + 8, :]
        for k in range(1, CONV_K):
            dext = dext + cw_ref[k:k + 1, :] * dyp[o0 - k:o0 - k + TM + 8, :]
        for k in range(CONV_K):
            o = 8 - (CONV_K - 1) + k
            dcw_ref[k:k + 1, :] += jnp.sum(dy * ext[o:o + TM, :], axis=0, keepdims=True)
        ddn_ref[...] = _b(jnp.concatenate([dext[8:TM, :], dext[TM:TM + 8, :] + carry[...]], axis=0))
        carry[...] = dext[0:8, :]

    row = lambda wd: pl.BlockSpec((TM, wd), lambda i: (NT - 1 - i, 0))
    vec = pl.BlockSpec((1, HP), lambda i: (0, 0))
    return pl.pallas_call(
        body, name="dn_prep_bwd", grid=(NT,),
        in_specs=[row(W3), pl.BlockSpec((8, W3), lambda i: (jnp.maximum((NT - 1 - i) * (TM // 8) - 1, 0), 0)), row(HP),
                  pl.BlockSpec((CONV_K, W3), lambda i: (0, 0)), vec, vec, row(W), row(W), row(W), row(HP)],
        out_specs=[row(W3), row(HP), pl.BlockSpec((8, W3), lambda i: (0, 0)), pl.BlockSpec((8, HP), lambda i: (0, 0))],
        out_shape=[SDS((L, W3), bf16), SDS((L, HP), bf16), SDS((8, W3), f32), SDS((8, HP), f32)],
        scratch_shapes=[pltpu.VMEM((TM + 8, W3), f32), pltpu.VMEM((TM + 16, W3), f32), pltpu.VMEM((8, W3), f32)],
        compiler_params=_cp())(dn, dn, sd, cw, alog_p, dt_p, dq, dk, dv, dbg)


def _unit_lower_inverses(lows):
    C = lows[0].shape[0]
    P = jnp.stack(lows)
    X = (_iota((C, C), 0) == _iota((C, C), 1)).astype(f32)[None] - P
    bdot = lambda a, b: jnp.einsum("bij,bjk->bik", a, b, preferred_element_type=f32, precision=HI)
    for _ in range(5):
        P = bdot(P, P)
        X = X + bdot(X, P)
    return [X[i] for i in range(len(lows))]


@jax.custom_vjp
def _inverse_given(low, X):
    return X


def _inverse_given_bwd(X, g):
    t = lax.dot_general(X, g, (((0,), (0,)), ((), ())), preferred_element_type=f32, precision=HI)
    return -lax.dot_general(t, X, (((1,), (1,)), ((), ())), preferred_element_type=f32, precision=HI), jnp.zeros_like(X)


_inverse_given.defvjp(lambda low, X: (X, X), _inverse_given_bwd)


def _dn_intra_pre(q, k, v, bg):
    C = DN_C
    row, col = _iota((C, C), 0), _iota((C, C), 1)
    tri = row >= col
    eye = (row == col).astype(f32)
    G = _dot_hi(tri.astype(f32), bg)
    GT = lax.dot_general(G, eye, (((0,), (0,)), ((), ())), preferred_element_type=f32, precision=HI)
    lane = _iota((C, HP), 1)
    rowt = _iota((HP, C), 0)
    last = _iota((C, 1), 0) == C - 1
    heads = []
    for h in range(DN_H):
        beta = jnp.sum(jnp.where(lane == h, bg, 0.0), axis=1, keepdims=True)
        gcol = jnp.sum(jnp.where(lane == DN_H + h, G, 0.0), axis=1, keepdims=True)
        grow = jnp.sum(jnp.where(rowt == DN_H + h, GT, 0.0), axis=0, keepdims=True)
        glast = jnp.sum(jnp.where(last, gcol, 0.0), axis=0, keepdims=True)
        decay = jnp.exp(jnp.where(tri, gcol - grow, NEG))
        qh, kh, vh = (t[:, h * DN_D:(h + 1) * DN_D] for t in (q, k, v))
        kb = kh * beta
        low = jnp.where(row > col, _dot_nt(_b(kb), _b(kh)) * decay, 0.0)
        heads.append((beta, gcol, glast, decay, qh, kh, vh, kb, low))
    return heads


def _dn_intra_post(heads, xs):
    lane1 = _iota((1, HP), 1)
    us, ws, qds, kds, attns = [], [], [], [], []
    glrow = jnp.zeros((1, HP), f32)
    for h, ((beta, gcol, glast, decay, qh, kh, vh, kb, _), X) in enumerate(zip(heads, xs)):
        eg = jnp.exp(gcol)
        us.append(_dot_hi(X, vh * beta))
        ws.append(_dot_hi(X, kb * eg))
        attns.append(_dot_nt(_b(qh), _b(kh)) * decay)
        qds.append(qh * eg)
        kds.append(kh * jnp.exp(glast - gcol))
        glrow = glrow + jnp.where(lane1 == h, glast, 0.0)
    cat = lambda xs_: jnp.concatenate(xs_, axis=1)
    return cat(us), cat(ws), cat(qds), cat(kds), cat(attns), glrow, cat(list(xs))


def _dn_intra_group(q, k, v, bg, xs):
    G = q.shape[0] // DN_C
    rows = [slice(j * DN_C, (j + 1) * DN_C) for j in range(G)]
    pre = [_dn_intra_pre(q[r, :], k[r, :], v[r, :], bg[r, :]) for r in rows]
    inv = [[_inverse_given(hd[-1], x) for hd, x in zip(heads, xj)] for heads, xj in zip(pre, xs)]
    post = [_dn_intra_post(heads, xj) for heads, xj in zip(pre, inv)]
    return tuple(jnp.concatenate([p[i] for p in post], axis=0) for i in range(5)) + (tuple(p[5] for p in post),)


def _lane_pick(rowvec, h):
    return jnp.sum(jnp.where(_iota(rowvec.shape, 1) == h, rowvec, 0.0), axis=1, keepdims=True)


def _dn_intra(q, k, v, bg):
    L, W = q.shape
    NC = L // DN_C
    G = _pick(NC, DN_INTRA_GROUP)
    R = G * DN_C
    WA = DN_H * DN_C

    def body(q_ref, k_ref, v_ref, bg_ref, u_ref, w_ref, qd_ref, kd_ref, at_ref, gl_ref, x_ref):
        rows = [slice(j * DN_C, (j + 1) * DN_C) for j in range(G)]
        pre = [_dn_intra_pre(q_ref[r, :], k_ref[r, :], v_ref[r, :], bg_ref[r, :]) for r in rows]
        inv = _unit_lower_inverses([hd[-1] for heads in pre for hd in heads])
        for j, r in enumerate(rows):
            u, w, qd, kd, at, gl, xs = _dn_intra_post(pre[j], inv[j * DN_H:(j + 1) * DN_H])
            u_ref[r, :], x_ref[r, :] = u, xs
            w_ref[r, :], qd_ref[r, :], kd_ref[r, :], at_ref[r, :] = _b(w), _b(qd), _b(kd), _b(at)
            gl_ref[j] = gl

    row = lambda wd: pl.BlockSpec((R, wd), lambda n: (n, 0))
    return pl.pallas_call(
        body, name="dn_intra", grid=(NC // G,),
        in_specs=[row(W), row(W), row(W), row(HP)],
        out_specs=[row(W), row(W), row(W), row(W), row(WA), pl.BlockSpec((G, 1, HP), lambda n: (n, 0, 0)), row(WA)],
        out_shape=[SDS((L, W), f32), SDS((L, W), bf16), SDS((L, W), bf16), SDS((L, W), bf16), SDS((L, WA), bf16), SDS((NC, 1, HP), f32),
                   SDS((L, WA), f32)],
        compiler_params=_cp())(q, k, v, bg)


def _dn_scan(u, w, qd, kd, at, gl):
    L, W = u.shape
    NC = L // DN_C
    G = _pick(NC, DN_SCAN_GROUP)
    R = G * DN_C

    def body(u_ref, w_ref, qd_ref, kd_ref, at_ref, gl_ref, o_ref, vn_ref, s_ref, S):
        @pl.when(pl.program_id(0) == 0)
        def _():
            S[...] = jnp.zeros_like(S)

        for j in range(G):
            r = slice(j * DN_C, (j + 1) * DN_C)
            glrow = gl_ref[j]
            for h in range(DN_H):
                c = slice(h * DN_D, (h + 1) * DN_D)
                Sh = S[h]
                s_ref[j, h] = Sh
                Sb = _b(Sh)
                vb = _b(u_ref[r, c] - _dot(w_ref[r, c], Sb))
                vn_ref[r, c] = vb
                o_ref[r, c] = _dot(qd_ref[r, c], Sb) + _dot(at_ref[r, h * DN_C:(h + 1) * DN_C], vb)
                S[h] = Sh * jnp.exp(_lane_pick(glrow, h)) + _dot_tn(kd_ref[r, c], vb)

    row = lambda wd: pl.BlockSpec((R, wd), lambda n: (n, 0))
    return pl.pallas_call(
        body, name="dn_scan", grid=(NC // G,),
        in_specs=[row(W), row(W), row(W), row(W), row(DN_H * DN_C), pl.BlockSpec((G, 1, HP), lambda n: (n, 0, 0))],
        out_specs=[row(W), row(W), pl.BlockSpec((G, DN_H, DN_D, DN_D), lambda n: (n, 0, 0, 0))],
        out_shape=[SDS((L, W), f32), SDS((L, W), bf16), SDS((NC, DN_H, DN_D, DN_D), f32)],
        scratch_shapes=[pltpu.VMEM((DN_H, DN_D, DN_D), f32)], compiler_params=_cp())(u, w, qd, kd, at, gl)


def _dn_scan_bwd(do, w, qd, kd, at, gl):
    L, W = do.shape
    NC = L // DN_C
    G = _pick(NC, DN_SCAN_GROUP)
    R = G * DN_C
    NS = NC // G

    def body(do_ref, w_ref, qd_ref, kd_ref, at_ref, gl_ref, dvn_ref, ds_ref, dS):
        @pl.when(pl.program_id(0) == 0)
        def _():
            dS[...] = jnp.zeros_like(dS)

        for j in reversed(range(G)):
            r = slice(j * DN_C, (j + 1) * DN_C)
            glrow = gl_ref[j]
            for h in range(DN_H):
                c = slice(h * DN_D, (h + 1) * DN_D)
                dSo = dS[h]
                ds_ref[j, h] = dSo
                dob = _b(do_ref[r, c])
                dvn = _dot_tn(at_ref[r, h * DN_C:(h + 1) * DN_C], dob) + _dot(kd_ref[r, c], _b(dSo))
                dvn_ref[r, c] = dvn
                dS[h] = _dot_tn(qd_ref[r, c], dob) + dSo * jnp.exp(_lane_pick(glrow, h)) - _dot_tn(w_ref[r, c], _b(dvn))

    row = lambda wd: pl.BlockSpec((R, wd), lambda n: (NS - 1 - n, 0))
    return pl.pallas_call(
        body, name="dn_scan_bwd", grid=(NS,),
        in_specs=[row(W), row(W), row(W), row(W), row(DN_H * DN_C), pl.BlockSpec((G, 1, HP), lambda n: (NS - 1 - n, 0, 0))],
        out_specs=[row(W), pl.BlockSpec((G, DN_H, DN_D, DN_D), lambda n: (NS - 1 - n, 0, 0, 0))],
        out_shape=[SDS((L, W), f32), SDS((NC, DN_H, DN_D, DN_D), f32)],
        scratch_shapes=[pltpu.VMEM((DN_H, DN_D, DN_D), f32)], compiler_params=_cp())(do, w, qd, kd, at, gl)


def _dn_intra_bwd(q, k, v, bg, xinv, do, vn, dvn, states, dstates):
    L, W = q.shape
    NC = L // DN_C
    G = _pick(NC, DN_INTRA_GROUP)
    R = G * DN_C

    def body(q_ref, k_ref, v_ref, bg_ref, x_ref, do_ref, vn_ref, dvn_ref, s_ref, ds_ref, dq_ref, dk_ref, dv_ref, dbg_ref):
        lane1 = _iota((1, HP), 1)
        rows = [slice(j * DN_C, (j + 1) * DN_C) for j in range(G)]
        xs = [[x_ref[r, h * DN_C:(h + 1) * DN_C] for h in range(DN_H)] for r in rows]
        outs, vjp = jax.vjp(functools.partial(_dn_intra_group, xs=xs), q_ref[...], k_ref[...], v_ref[...], bg_ref[...])
        dws, dqds, dkds, dats, dgls = [], [], [], [], []
        for j, r in enumerate(rows):
            dw, dqd, dkd, dat = [], [], [], []
            dgl = jnp.zeros((1, HP), f32)
            for h in range(DN_H):
                c = slice(h * DN_D, (h + 1) * DN_D)
                Sh, dSo = s_ref[j, h], ds_ref[j, h]
                Sb, dob, vb = _b(Sh), _b(do_ref[r, c]), vn_ref[r, c]
                dw.append(-_dot_nt(_b(dvn_ref[r, c]), Sb))
                dqd.append(_dot_nt(dob, Sb))
                dat.append(_dot_nt(dob, vb))
                dkd.append(_dot_nt(vb, _b(dSo)))
                dcd = jnp.sum(jnp.sum(Sh * dSo, axis=1, keepdims=True), axis=0, keepdims=True)
                dgl = dgl + jnp.where(lane1 == h, dcd * jnp.exp(_lane_pick(outs[5][j], h)), 0.0)
            cat = lambda xs_: jnp.concatenate(xs_, axis=1)
            dws.append(cat(dw)), dqds.append(cat(dqd)), dkds.append(cat(dkd)), dats.append(cat(dat)), dgls.append(dgl)
        cat0 = lambda xs_: jnp.concatenate(xs_, axis=0)
        dq, dk, dv, dbg = vjp((dvn_ref[...], cat0(dws), cat0(dqds), cat0(dkds), cat0(dats), tuple(dgls)))
        dq_ref[...], dk_ref[...], dv_ref[...], dbg_ref[...] = dq, dk, dv, dbg

    row = lambda wd: pl.BlockSpec((R, wd), lambda n: (n, 0))
    st = pl.BlockSpec((G, DN_H, DN_D, DN_D), lambda n: (n, 0, 0, 0))
    return pl.pallas_call(
        body, name="dn_intra_bwd", grid=(NC // G,),
        in_specs=[row(W), row(W), row(W), row(HP), row(DN_H * DN_C), row(W), row(W), row(W), st, st],
        out_specs=[row(W), row(W), row(W), row(HP)],
        out_shape=[SDS((L, W), f32)] * 3 + [SDS((L, HP), f32)],
        compiler_params=_cp())(q, k, v, bg, xinv, do, vn, dvn, states, dstates)


def _dn_normgate(oraw, dz, wn):
    outs = []
    for h in range(DN_H):
        sl = slice(h * DN_D, (h + 1) * DN_D)
        z = dz[:, sl]
        outs.append(_rms(oraw[:, sl], wn) * (z * _sigmoid(z)))
    return jnp.concatenate(outs, axis=1)


def _mix_fwd(op, oraw, dz, ga, gb, h0, wn, wbf, wbd, wo):
    L, D = h0.shape
    TM = _pick(L, ROW_TILES)

    def body(op_ref, or_ref, dz_ref, ga_ref, gb_ref, h0_ref, wn_ref, wbf_ref, wbd_ref, wo_ref, h1_ref):
        pf = _dot(_b(op_ref[...]), wbf_ref[...])
        pd = _dot(_b(_dn_normgate(or_ref[...], dz_ref[...], wn_ref[...])), wbd_ref[...])
        y = _sigmoid(ga_ref[...]) * pf + _sigmoid(gb_ref[...]) * pd
        h1_ref[...] = h0_ref[...] + _dot(_b(y), wo_ref[...])

    row = lambda wd: pl.BlockSpec((TM, wd), lambda i: (i, 0))
    full = lambda a: pl.BlockSpec(a.shape, lambda i: (0, 0))
    return pl.pallas_call(
        body, name="mix_fwd", grid=(L // TM,),
        in_specs=[row(op.shape[1]), row(oraw.shape[1]), row(dz.shape[1]), row(D), row(D), row(D), full(wn), full(wbf), full(wbd), full(wo)],
        out_specs=row(D), out_shape=SDS((L, D), f32), compiler_params=_cp())(op, oraw, dz, ga, gb, h0, wn, wbf, wbd, wo)


def _mix_bwd(dh1, op, oraw, dz, ga, gb, wn, wbf, wbd, wo):
    L, D = dh1.shape
    TM = _pick(L, ROW_TILES)
    WF, WD = op.shape[1], oraw.shape[1]

    def body(dh1_ref, op_ref, or_ref, dz_ref, ga_ref, gb_ref, wn_ref, wbf_ref, wbd_ref, wo_ref,
             dop_ref, dor_ref, ddz_ref, dga_ref, dgb_ref, af_ref, ad_ref, dpf_ref, dpd_ref, y_ref, dmix_ref, acc_ref):
        @pl.when(pl.program_id(0) == 0)
        def _():
            acc_ref[...] = jnp.zeros_like(acc_ref)

        af = _b(op_ref[...])
        ad, vjp = jax.vjp(_dn_normgate, or_ref[...], dz_ref[...], wn_ref[...])
        adb = _b(ad)
        pf, pd = _dot(af, wbf_ref[...]), _dot(adb, wbd_ref[...])
        sa, sb = _sigmoid(ga_ref[...]), _sigmoid(gb_ref[...])
        dmix = _b(dh1_ref[...])
        dy = _dot_nt(dmix, wo_ref[...])
        dpf, dpd = _b(dy * sa), _b(dy * sb)
        dor, ddz, dwn = vjp(_dot_nt(dpd, wbd_ref[...]))
        dop_ref[...] = _dot_nt(dpf, wbf_ref[...])
        dor_ref[...] = dor
        ddz_ref[...] = _b(ddz)
        dga_ref[...] = _b(dy * pf * sa * (1.0 - sa))
        dgb_ref[...] = _b(dy * pd * sb * (1.0 - sb))
        af_ref[...], ad_ref[...], dpf_ref[...], dpd_ref[...] = af, adb, dpf, dpd
        y_ref[...] = _b(sa * pf + sb * pd)
        dmix_ref[...] = dmix
        acc_ref[0:1, :] += dwn

    row = lambda wd: pl.BlockSpec((TM, wd), lambda i: (i, 0))
    full = lambda a: pl.BlockSpec(a.shape, lambda i: (0, 0))
    return pl.pallas_call(
        body, name="mix_bwd", grid=(L // TM,),
        in_specs=[row(D), row(WF), row(WD), row(WD), row(D), row(D), full(wn), full(wbf), full(wbd), full(wo)],
        out_specs=[row(WF), row(WD), row(WD), row(D), row(D), row(WF), row(WD), row(D), row(D), row(D), row(D),
                   pl.BlockSpec((8, HP), lambda i: (0, 0))],
        out_shape=[SDS((L, WF), f32), SDS((L, WD), f32), SDS((L, WD), bf16), SDS((L, D), bf16), SDS((L, D), bf16),
                   SDS((L, WF), bf16), SDS((L, WD), bf16), SDS((L, D), bf16), SDS((L, D), bf16), SDS((L, D), bf16), SDS((L, D), bf16),
                   SDS((8, HP), f32)],
        compiler_params=_cp())(dh1, op, oraw, dz, ga, gb, wn, wbf, wbd, wo)


def _ffn_fwd_bwd(h1, tgt, w2, wf, wg, wu, wd):
    L, D = h1.shape
    F = wg.shape[1]
    TM = _pick(L, FFN_TILES)

    def body(h_ref, t_ref, w2_ref, wf_ref, wg_hbm, wu_hbm, wd_hbm,
             dh1_ref, xn_ref, dg_ref, du_ref, act_ref, dh2_ref, acc_ref, wg_v, wu_v, wd_v, sems):
        i = pl.program_id(0)
        _load_once([(wg_hbm, wg_v), (wu_hbm, wu_v), (wd_hbm, wd_v)], sems)

        @pl.when(i == 0)
        def _():
            acc_ref[...] = jnp.zeros_like(acc_ref)

        h1v = h_ref[...]
        xn2, vjp2 = jax.vjp(_rms, h1v, w2_ref[...])
        xb = _b(xn2)
        g, u = _dot(xb, wg_v[...]), _dot(xb, wu_v[...])
        sg = _sigmoid(g)
        ab = _b(g * sg * u)
        h2 = h1v + _dot(ab, wd_v[...])
        out, vjpf = jax.vjp(_rms, h2, wf_ref[...])
        valid = (i * TM + _iota((TM, 1), 0)) >= PREFIX
        diff = jnp.where(valid, out - t_ref[...], 0.0)
        loss = 0.5 * jnp.sum(jnp.sum(diff * diff, axis=1, keepdims=True), axis=0, keepdims=True) / D
        dh2, dwf = vjpf(diff * (1.0 / D))
        dh2b = _b(dh2)
        dact = _dot_nt(dh2b, wd_v[...])
        dgb = _b(dact * u * (sg * (1.0 + g * (1.0 - sg))))
        dub = _b(dact * (g * sg))
        dh1n, dw2 = vjp2(_dot_nt(dgb, wg_v[...]) + _dot_nt(dub, wu_v[...]))
        dh1_ref[...] = dh2 + dh1n
        xn_ref[...], dg_ref[...], du_ref[...], act_ref[...], dh2_ref[...] = xb, dgb, dub, ab, dh2b
        acc_ref[0:1, :] += dw2
        acc_ref[1:2, :] += dwf
        acc_ref[2:3, :] += jnp.broadcast_to(loss, (1, D))

    row = lambda wd_: pl.BlockSpec((TM, wd_), lambda i: (i, 0))
    vec = pl.BlockSpec((1, D), lambda i: (0, 0))
    anyspec = pl.BlockSpec(memory_space=pl.ANY)
    return pl.pallas_call(
        body, name="ffn_fwd_bwd", grid=(L // TM,),
        in_specs=[row(D), row(D), vec, vec, anyspec, anyspec, anyspec],
        out_specs=[row(D), row(D), row(F), row(F), row(F), row(D), pl.BlockSpec((8, D), lambda i: (0, 0))],
        out_shape=[SDS((L, D), f32), SDS((L, D), bf16), SDS((L, F), bf16), SDS((L, F), bf16), SDS((L, F), bf16), SDS((L, D), bf16),
                   SDS((8, D), f32)],
        scratch_shapes=[pltpu.VMEM((D, F), bf16), pltpu.VMEM((D, F), bf16), pltpu.VMEM((F, D), bf16), pltpu.SemaphoreType.DMA((3,))],
        compiler_params=_cp())(h1, tgt, w2, wf, wg, wu, wd)


def _pad_lanes(v, n=HP):
    return jnp.pad(v.astype(f32), ((0, 0), (0, n - v.shape[1])))


def _pack_w_in(w_full):
    D = w_full.shape[0]
    FW, DW = FOX_H * FOX_D, DN_H * DN_D
    o = 0
    parts = {}
    for name, wd in (("fq", FW), ("fk", FW), ("fv", FW), ("fl", FOX_H), ("dn", 3 * DW), ("ba", 2 * DN_H), ("dz", DW), ("ga", D), ("gb", D)):
        parts[name] = w_full[:, o:o + wd]
        o += wd
    assert o == w_full.shape[1]
    heads = lambda w: jnp.pad(w.reshape(D, FOX_H, FOX_D), ((0, 0), (0, 0), (0, HP - FOX_D))).reshape(D, FOX_H * HP)
    small = lambda w: jnp.pad(w, ((0, 0), (0, HP - w.shape[1])))
    return jnp.concatenate([heads(parts["fq"]), heads(parts["fk"]), heads(parts["fv"]), parts["dn"], parts["dz"], parts["ga"], parts["gb"],
                            small(parts["fl"]), small(parts["ba"])], axis=1)


def _unpack_w_in(gp, d_model):
    D = gp.shape[0]
    FW, DW = FOX_H * FOX_D, DN_H * DN_D
    segs, o = {}, 0
    for name, wd, _ in _seg_layout(d_model):
        segs[name] = gp[:, o:o + wd]
        o += wd
    heads = lambda g: g.reshape(D, FOX_H, HP)[:, :, :FOX_D].reshape(D, FW)
    return jnp.concatenate([heads(segs["fq"]), heads(segs["fk"]), heads(segs["fv"]), segs["sf"][:, :FOX_H], segs["dn"],
                            segs["sd"][:, :2 * DN_H], segs["dz"], segs["ga"], segs["gb"]], axis=1)


def _local_step(x, tgt, meta, w1, w_in, fbias, cw, alog, dtb, wn, w2, wf, late_shards):
    T, D = x.shape
    h0 = jnp.concatenate([jnp.zeros((N_PAD, D), f32), meta, x], axis=0)
    tgt_p = jnp.concatenate([jnp.zeros((PREFIX, D), f32), tgt], axis=0)
    wp = _pack_w_in(w_in)
    bias_p, alog_p, dt_p = _pad_lanes(fbias), _pad_lanes(jnp.pad(alog, ((0, 0), (DN_H, 0)))), _pad_lanes(jnp.pad(dtb, ((0, 0), (DN_H, 0))))

    xn, fq, fk, fv, dn, dz, ga, gb, sf, sd = _in_proj(h0, w1, wp)
    qa, ka = _fox_prep(fq, fk, sf, bias_p)
    op, gathered = _fox_fwd(qa, ka, fv, [late_shards[n] for n in LATE])
    full = {n: _from_slabs(n, s) for n, s in zip(LATE, gathered)}
    wbf, wbd, wo, wg, wu, wd = (full[n] for n in ("w_branch_fox", "w_branch_dn", "w_out", "w_ffn_gate", "w_ffn_up", "w_ffn_down"))
    wbf_p = jnp.pad(wbf.reshape(FOX_H, FOX_D, D), ((0, 0), (0, HP - FOX_D), (0, 0))).reshape(FOX_H * HP, D)
    qn, kn, vn, bg = _dn_prep(dn, sd, cw, alog_p, dt_p)
    u_dn, w_dn, qd_dn, kd_dn, at_dn, gl_dn, x_dn = _dn_intra(qn, kn, vn, bg)
    oraw, vnew, states = _dn_scan(u_dn, w_dn, qd_dn, kd_dn, at_dn, gl_dn)
    h1 = _mix_fwd(op, oraw, dz, ga, gb, h0, wn, wbf_p, wbd, wo)

    dh1, xn2, dgate, dup, act, dh2, acc_f = _ffn_fwd_bwd(h1, tgt_p, w2, wf, wg, wu, wd)
    g_wg, g_wu, g_wd = _matmul_tn(xn2, dgate, "dw_ffn_gate"), _matmul_tn(xn2, dup, "dw_ffn_up"), _matmul_tn(act, dh2, "dw_ffn_down")

    dop, dor, ddz, dga, dgb, af, ad, dpf, dpd, yb, dmix, acc_m = _mix_bwd(dh1, op, oraw, dz, ga, gb, wn, wbf_p, wbd, wo)
    g_wbf = _matmul_tn(af, dpf, "dw_branch_fox").reshape(FOX_H, HP, D)[:, :FOX_D].reshape(FOX_H * FOX_D, D)
    g_wbd, g_wo = _matmul_tn(ad, dpd, "dw_branch_dn"), _matmul_tn(yb, dmix, "dw_out")

    dvnew, dstates = _dn_scan_bwd(dor, w_dn, qd_dn, kd_dn, at_dn, gl_dn)
    dqn, dkn, dvn, dbg = _dn_intra_bwd(qn, kn, vn, bg, x_dn, dor, vnew, dvnew, states, dstates)
    ddn, dsd, acc_cw, acc_p = _dn_prep_bwd(dn, sd, cw, alog_p, dt_p, dqn, dkn, dvn, dbg)
    g_late = dict(w_branch_fox=g_wbf, w_branch_dn=g_wbd, w_out=g_wo, w_ffn_gate=g_wg, w_ffn_up=g_wu, w_ffn_down=g_wd)
    (dqa, dka, dfv), recv = _fox_bwd(qa, ka, fv, op, dop, [_to_slabs(n, g_late[n]) for n in LATE])
    dfq, dfk, dsf, acc_b = _fox_prep_bwd(dqa, dka, sf, bias_p)

    dproj = jnp.concatenate([dfq, dfk, dfv, ddn, ddz, dga, dgb, dsf, dsd], axis=1)
    g_wp = _matmul_tn(xn, dproj, "dw_in")
    dh0, acc_1 = _in_proj_bwd(dproj, wp, h0, w1, dh1)

    small = dict(loss=acc_f[2, 0:1], mix_norm_w=acc_1[0], fox_forget_bias=acc_b[0, :FOX_H], dn_a_log=acc_p[0, DN_H:2 * DN_H],
                 dn_dt_bias=acc_p[1, DN_H:2 * DN_H], dn_out_norm_w=acc_m[0], ffn_norm_w=acc_f[0], final_norm_w=acc_f[1],
                 meta_tokens=dh0[N_PAD:PREFIX].reshape(-1), dn_conv_w=acc_cw[:CONV_K].reshape(-1))
    return dh0[PREFIX:], small, _unpack_w_in(g_wp, D), dict(zip(LATE, recv))


def _mesh_pos():
    x, y, c = lax.axis_index("x"), lax.axis_index("y"), lax.axis_index("c")
    return x, y, c, 4 * x + 2 * y + c


def _peer(x, y, c, m):
    flip = lambda v, on: 1 - v if on else v
    px, py, pc = flip(x, m & 4), flip(y, m & 2), flip(c, m & 1)
    return (px, py, pc), 4 * px + 2 * py + pc


def _exchange_sems(n):
    return [pltpu.SemaphoreType.DMA((n, N_DEV - 1)), pltpu.SemaphoreType.DMA((n, N_DEV - 1)), pltpu.SemaphoreType.DMA((n,))]


def _exchange_copies(ins, outs, send_sems, recv_sems, loc_sems, gather, with_receives):
    x, y, c, me = _mesh_pos()
    src = lambda a, pid: ins[a] if gather else ins[a].at[pid]
    local = [pltpu.make_async_copy(src(a, me), outs[a].at[me], loc_sems.at[a]) for a in range(len(ins))]
    sends, recvs = [], []
    for m in range(1, N_DEV):
        peer, pid = _peer(x, y, c, m)
        for a in range(len(ins)):
            kw = dict(send_sem=send_sems.at[a, m - 1], recv_sem=recv_sems.at[a, m - 1], device_id=peer, device_id_type=MESH)
            sends.append(pltpu.make_async_remote_copy(src_ref=src(a, pid), dst_ref=outs[a].at[me], **kw))
            if with_receives:
                recvs.append(pltpu.make_async_remote_copy(src_ref=src(a, pid), dst_ref=outs[a].at[pid], **kw))
    return local, sends, recvs


def _exchange_start(ins, outs, send_sems, recv_sems, loc_sems, gather):
    local, sends, _ = _exchange_copies(ins, outs, send_sems, recv_sems, loc_sems, gather, with_receives=False)
    for cp in local + sends:
        cp.start()


def _exchange_wait(ins, outs, send_sems, recv_sems, loc_sems, gather):
    local, sends, recvs = _exchange_copies(ins, outs, send_sems, recv_sems, loc_sems, gather, with_receives=True)
    for cp in recvs:
        cp.wait_recv()
    for cp in sends:
        cp.wait_send()
    for cp in local:
        cp.wait()


def _exchange(arrays, name, gather):
    n = len(arrays)
    shapes = [a.shape if gather else a.shape[1:] for a in arrays]

    def body(*refs):
        ins, outs, sems = refs[:n], refs[n:2 * n], refs[2 * n:]
        _exchange_start(ins, outs, *sems, gather=gather)
        _exchange_wait(ins, outs, *sems, gather=gather)

    anyspec = pl.BlockSpec(memory_space=pl.ANY)
    return pl.pallas_call(
        body, name=name, in_specs=[anyspec] * n, out_specs=[anyspec] * n,
        out_shape=[SDS((N_DEV,) + tuple(s), a.dtype) for s, a in zip(shapes, arrays)],
        scratch_shapes=_exchange_sems(n))(*arrays)


def _all_reduce_small(v):
    R = v.shape[0]

    def body(v_ref, o_ref, gath, send_sems, recv_sems):
        x, y, c, me = _mesh_pos()
        gath[me] = v_ref[...]
        sends = []
        for m in range(1, N_DEV):
            peer, _ = _peer(x, y, c, m)
            cp = pltpu.make_async_remote_copy(src_ref=v_ref, dst_ref=gath.at[me], send_sem=send_sems.at[m - 1],
                                              recv_sem=recv_sems.at[m - 1], device_id=peer, device_id_type=MESH)
            cp.start()
            sends.append(cp)
        for m in range(1, N_DEV):
            peer, pid = _peer(x, y, c, m)
            pltpu.make_async_remote_copy(src_ref=v_ref, dst_ref=gath.at[pid], send_sem=send_sems.at[m - 1],
                                         recv_sem=recv_sems.at[m - 1], device_id=peer, device_id_type=MESH).wait_recv()
        for cp in sends:
            cp.wait_send()
        tot = gath[0]
        for d in range(1, N_DEV):
            tot = tot + gath[d]
        o_ref[...] = tot

    vm = pl.BlockSpec(memory_space=pltpu.VMEM)
    return pl.pallas_call(
        body, name="all_reduce_small", in_specs=[vm], out_specs=vm, out_shape=SDS((R, HP), f32),
        scratch_shapes=[pltpu.VMEM((N_DEV, R, HP), f32), pltpu.SemaphoreType.DMA((N_DEV - 1,)), pltpu.SemaphoreType.DMA((N_DEV - 1,))],
        )(v)


def _adamw_math(w, g, m, v):
    m = ADAM_B1 * m + (1.0 - ADAM_B1) * g
    v = ADAM_B2 * v + (1.0 - ADAM_B2) * (g * g)
    m_hat = m / (1.0 - ADAM_B1 ** ADAM_STEP)
    v_hat = v / (1.0 - ADAM_B2 ** ADAM_STEP)
    return -ADAM_LR * (m_hat / (jnp.sqrt(v_hat) + ADAM_EPS) + ADAM_WD * w), m, v


def _adamw(g, w, m, v, name):
    R, Cc = w.shape
    TR = R if R <= 512 else _pick(R, (256, 128))
    slabs = g.ndim == 3

    def body(g_ref, w_ref, m_ref, v_ref, go_ref, d_ref, mo_ref, vo_ref):
        if slabs:
            gs = g_ref[0].astype(f32)
            for k in range(1, N_DEV):
                gs = gs + g_ref[k].astype(f32)
        else:
            gs = g_ref[...]
        d, mn, vn = _adamw_math(w_ref[...], gs, m_ref[...], v_ref[...])
        go_ref[...], d_ref[...], mo_ref[...], vo_ref[...] = gs, d, mn, vn

    blk = pl.BlockSpec((TR, Cc), lambda i: (i, 0))
    gblk = pl.BlockSpec((N_DEV, TR, Cc), lambda i: (0, i, 0)) if slabs else blk
    return pl.pallas_call(
        body, name=name, grid=(R // TR,), in_specs=[gblk, blk, blk, blk], out_specs=[blk] * 4,
        out_shape=[SDS((R, Cc), f32)] * 4, compiler_params=_cp())(g, w, m, v)


WEIGHTS = ("meta_tokens", "mix_norm_w", "w_in", "fox_forget_bias", "dn_conv_w", "dn_a_log", "dn_dt_bias", "dn_out_norm_w",
           "w_branch_fox", "w_branch_dn", "w_out", "ffn_norm_w", "w_ffn_gate", "w_ffn_up", "w_ffn_down", "final_norm_w")
COL_SHARDED = ("w_in", "w_branch_fox", "w_branch_dn", "w_ffn_gate", "w_ffn_up")
ROW_SHARDED = ("w_out", "w_ffn_down")
BIG = COL_SHARDED + ROW_SHARDED
LATE = tuple(n for n in BIG if n != "w_in")
SMALL = tuple(n for n in WEIGHTS if n not in BIG)


def _to_slabs(name, g):
    r, c = g.shape
    if name in COL_SHARDED:
        return _b(g.reshape(r, N_DEV, c // N_DEV).transpose(1, 0, 2))
    return _b(g.reshape(N_DEV, r // N_DEV, c))


def _from_slabs(name, s):
    n, r, c = s.shape
    if name in COL_SHARDED:
        return s.transpose(1, 0, 2).reshape(r, n * c)
    return s.reshape(n * r, c)


def kernel(x, meta_tokens, mix_norm_w, w_in, fox_forget_bias, dn_conv_w, dn_a_log, dn_dt_bias, dn_out_norm_w, w_branch_fox, w_branch_dn, w_out, ffn_norm_w, w_ffn_gate, w_ffn_up, w_ffn_down, final_norm_w, loss_target, m_meta_tokens, m_mix_norm_w, m_w_in, m_fox_forget_bias, m_dn_conv_w, m_dn_a_log, m_dn_dt_bias, m_dn_out_norm_w, m_w_branch_fox, m_w_branch_dn, m_w_out, m_ffn_norm_w, m_w_ffn_gate, m_w_ffn_up, m_w_ffn_down, m_final_norm_w, v_meta_tokens, v_mix_norm_w, v_w_in, v_fox_forget_bias, v_dn_conv_w, v_dn_a_log, v_dn_dt_bias, v_dn_out_norm_w, v_w_branch_fox, v_w_branch_dn, v_w_out, v_ffn_norm_w, v_w_ffn_gate, v_w_ffn_up, v_w_ffn_down, v_final_norm_w):
    w = dict(meta_tokens=meta_tokens, mix_norm_w=mix_norm_w, w_in=w_in, fox_forget_bias=fox_forget_bias, dn_conv_w=dn_conv_w, dn_a_log=dn_a_log, dn_dt_bias=dn_dt_bias, dn_out_norm_w=dn_out_norm_w, w_branch_fox=w_branch_fox, w_branch_dn=w_branch_dn, w_out=w_out, ffn_norm_w=ffn_norm_w, w_ffn_gate=w_ffn_gate, w_ffn_up=w_ffn_up, w_ffn_down=w_ffn_down, final_norm_w=final_norm_w)
    mom = dict(meta_tokens=m_meta_tokens, mix_norm_w=m_mix_norm_w, w_in=m_w_in, fox_forget_bias=m_fox_forget_bias, dn_conv_w=m_dn_conv_w, dn_a_log=m_dn_a_log, dn_dt_bias=m_dn_dt_bias, dn_out_norm_w=m_dn_out_norm_w, w_branch_fox=m_w_branch_fox, w_branch_dn=m_w_branch_dn, w_out=m_w_out, ffn_norm_w=m_ffn_norm_w, w_ffn_gate=m_w_ffn_gate, w_ffn_up=m_w_ffn_up, w_ffn_down=m_w_ffn_down, final_norm_w=m_final_norm_w)
    var = dict(meta_tokens=v_meta_tokens, mix_norm_w=v_mix_norm_w, w_in=v_w_in, fox_forget_bias=v_fox_forget_bias, dn_conv_w=v_dn_conv_w, dn_a_log=v_dn_a_log, dn_dt_bias=v_dn_dt_bias, dn_out_norm_w=v_dn_out_norm_w, w_branch_fox=v_w_branch_fox, w_branch_dn=v_w_branch_dn, w_out=v_w_out, ffn_norm_w=v_ffn_norm_w, w_ffn_gate=v_w_ffn_gate, w_ffn_up=v_w_ffn_up, w_ffn_down=v_w_ffn_down, final_norm_w=v_final_norm_w)
    two_d = lambda a: a.reshape(a.shape[-2:]) if a.ndim >= 2 else a.reshape(1, -1)
    me = 4 * lax.axis_index("x") + 2 * lax.axis_index("y") + lax.axis_index("c")

    g_in, g_meta, g_cw = _exchange([_b(two_d(w["w_in"])), two_d(w["meta_tokens"]), two_d(w["dn_conv_w"])], "all_gather_early", gather=True)
    meta = g_meta.transpose(1, 0, 2).reshape(N_META, -1)
    cw = g_cw.transpose(1, 0, 2).reshape(CONV_K, -1)

    gx, g_small, g_w_in, recv = _local_step(
        x[0], loss_target[0], meta, two_d(w["mix_norm_w"]), _from_slabs("w_in", g_in), two_d(w["fox_forget_bias"]), cw, two_d(w["dn_a_log"]),
        two_d(w["dn_dt_bias"]), two_d(w["dn_out_norm_w"]), two_d(w["ffn_norm_w"]), two_d(w["final_norm_w"]),
        {n: _b(two_d(w[n])) for n in LATE})

    recv["w_in"] = _exchange([_to_slabs("w_in", g_w_in)], "exchange_w_in_grad", gather=False)[0]
    order = ("loss",) + SMALL
    flat = jnp.concatenate([g_small[n].reshape(-1) for n in order])
    rows = -(-flat.shape[0] // (8 * HP)) * 8
    tot = _all_reduce_small(jnp.pad(flat, (0, rows * HP - flat.shape[0])).reshape(rows, HP)).reshape(-1)
    summed, o = {}, 0
    for n in order:
        k = g_small[n].shape[0]
        summed[n] = tot[o:o + k]
        o += k
    loss = summed["loss"][0]
    d_model = x.shape[-1]
    mcols, ccols = d_model // N_DEV, dn_conv_w.shape[-1]
    summed["meta_tokens"] = lax.dynamic_slice(summed["meta_tokens"].reshape(N_META, d_model), (0, me * mcols), (N_META, mcols)).reshape(-1)
    summed["dn_conv_w"] = lax.dynamic_slice(summed["dn_conv_w"].reshape(CONV_K, ccols * N_DEV), (0, me * ccols), (CONV_K, ccols)).reshape(-1)

    res = {}
    for n in BIG:
        res[n] = [r.reshape(w[n].shape) for r in _adamw(recv[n], two_d(w[n]), two_d(mom[n]), two_d(var[n]), "adamw_" + n)]
    sizes = [summed[n].shape[0] for n in SMALL]
    srows = -(-sum(sizes) // (8 * HP)) * 8
    pack = lambda d: jnp.pad(jnp.concatenate([d[n].reshape(-1) for n in SMALL]), (0, srows * HP - sum(sizes))).reshape(srows, HP)
    sres = _adamw(pack(summed), pack(w), pack(mom), pack(var), "adamw_small")
    o = 0
    for n, k in zip(SMALL, sizes):
        res[n] = [r.reshape(-1)[o:o + k].reshape(w[n].shape) for r in sres]
        o += k
    return (loss, gx[None], *[res[n][0] for n in WEIGHTS], *[res[n][1] for n in WEIGHTS], *[res[n][2] for n in WEIGHTS], *[res[n][3] for n in WEIGHTS])
```

```python
import functools

import jax
import jax.numpy as jnp
from jax import lax
from jax.experimental import pallas as pl
from jax.experimental.pallas import tpu as pltpu

f32, bf16 = jnp.float32, jnp.bfloat16
HI = lax.Precision.HIGHEST
MESH = pl.DeviceIdType.MESH
SDS = jax.ShapeDtypeStruct

N_DEV = 8
N_META = 16
PREFIX = 128
N_PAD = PREFIX - N_META
FOX_H, FOX_D = 8, 64
DN_H, DN_D = 4, 128
DN_C = 64
CONV_K = 4
HP = 128
SMALL_W = 256
EPS = 1e-6
NEG = -1e30
C_Q0, C_K0 = 64, 67
LSE_COL = 64

ADAM_LR, ADAM_B1, ADAM_B2, ADAM_EPS, ADAM_WD, ADAM_STEP = 0.001, 0.9, 0.999, 1e-08, 0.01, 10

VMEM_LIMIT_V7X = 56 * 1024 * 1024
ROW_TILES = (384, 128)
ATTN_TILES = (384, 128)
FFN_TILES = (192, 64)
COL_TILES = (512, 256, 128)
FOX_HEAD_GROUP = 2
ROW_BLOCKS = (704, 512, 256, 128)
DN_INTRA_GROUP = (6, 3, 2, 1)
DN_SCAN_GROUP = (6, 3, 2, 1)


def _pick(n, cands):
    for c in cands:
        if n % c == 0:
            return c
    raise ValueError(f"no tile of {cands} divides {n}")


def _cp(n_axes=1):
    return pltpu.CompilerParams(dimension_semantics=("arbitrary",) * n_axes, vmem_limit_bytes=VMEM_LIMIT_V7X)


def _b(x):
    return x.astype(bf16)


def _dot(a, b):
    return jnp.dot(a, b, preferred_element_type=f32)


def _dot_nt(a, b):
    return lax.dot_general(a, b, (((1,), (1,)), ((), ())), preferred_element_type=f32)


def _dot_tn(a, b):
    return lax.dot_general(a, b, (((0,), (0,)), ((), ())), preferred_element_type=f32)


def _dot_hi(a, b):
    return jnp.dot(a, b, preferred_element_type=f32, precision=HI)


def _iota(shape, dim):
    return lax.broadcasted_iota(jnp.int32, shape, dim)


def _rms(x, w):
    return x * lax.rsqrt(jnp.mean(x * x, axis=-1, keepdims=True) + EPS) * w


def _sigmoid(x):
    return jax.nn.sigmoid(x)


def _load_once(pairs, sems):
    @pl.when(pl.program_id(0) == 0)
    def _():
        cps = [pltpu.make_async_copy(src, dst, sems.at[k]) for k, (src, dst) in enumerate(pairs)]
        for cp in cps:
            cp.start()
        for cp in cps:
            cp.wait()


def _seg_layout(d_model):
    return (("fq", FOX_H * HP, bf16, "fox"), ("fk", FOX_H * HP, bf16, "fox"), ("sf", SMALL_W, f32, "fox"),
            ("fv", FOX_H * HP, bf16, "fv"),
            ("dn", 3 * DN_H * DN_D, f32, "dn"), ("sd", SMALL_W, f32, "dn"),
            ("dz", DN_H * DN_D, f32, "mix"), ("ga", d_model, f32, "mix"), ("gb", d_model, f32, "mix"))


GROUPS = ("fox", "fv", "dn", "mix")


def _group_widths(d_model):
    return [sum(wd for _, wd, _, g in _seg_layout(d_model) if g == grp) for grp in GROUPS]


def _in_proj(h0, w1, wp):
    L, D = h0.shape
    NP = wp.shape[1]
    TM = _pick(L, ROW_TILES)
    segs = _seg_layout(D)
    offs, o = [], 0
    for _, wd, _, _ in segs:
        offs.append(o)
        o += wd
    assert o == NP

    def body(h_ref, w1_ref, wp_hbm, xnt_ref, *rest):
        outs, (wp_v, sems) = rest[:len(segs)], rest[len(segs):]
        _load_once([(wp_hbm, wp_v)], sems)
        xf = _rms(h_ref[...], w1_ref[...])
        xn = _b(xf)
        xnt_ref[...] = _b(xf.T)
        for o_ref, off, (_, wd, _, _) in zip(outs, offs, segs):
            o_ref[...] = _dot(xn, wp_v[:, off:off + wd]).astype(o_ref.dtype)

    row = lambda wd: pl.BlockSpec((TM, wd), lambda i: (i, 0))
    return pl.pallas_call(
        body, name="in_proj", grid=(L // TM,),
        in_specs=[row(D), pl.BlockSpec((1, D), lambda i: (0, 0)), pl.BlockSpec(memory_space=pl.ANY)],
        out_specs=[pl.BlockSpec((D, TM), lambda i: (0, i))] + [row(wd) for _, wd, _, _ in segs],
        out_shape=[SDS((D, L), bf16)] + [SDS((L, wd), dt) for _, wd, dt, _ in segs],
        scratch_shapes=[pltpu.VMEM((D, NP), bf16), pltpu.SemaphoreType.DMA((1,))],
        compiler_params=_cp())(h0, w1, wp)


def _in_proj_bwd(dgroups, wp, h0, w1, dh1):
    L, D = h0.shape
    NP = wp.shape[1]
    TM = _pick(L, ROW_TILES)
    widths = [g.shape[1] for g in dgroups]
    assert sum(widths) == NP
    ng = len(dgroups)

    def body(*refs):
        dg_refs, (wp_hbm, h_ref, w1_ref, dh1_ref, dh0_ref, acc_ref, wp_v, sems) = refs[:ng], refs[ng:]
        _load_once([(wp_hbm, wp_v)], sems)

        @pl.when(pl.program_id(0) == 0)
        def _():
            acc_ref[...] = jnp.zeros_like(acc_ref)

        dxn, off = None, 0
        for g_ref, wd in zip(dg_refs, widths):
            part = _dot_nt(g_ref[...], wp_v[:, off:off + wd])
            dxn = part if dxn is None else dxn + part
            off += wd
        _, vjp = jax.vjp(_rms, h_ref[...], w1_ref[...])
        dh0n, dw1 = vjp(dxn)
        dh0_ref[...] = dh1_ref[...] + dh0n
        acc_ref[0:1, :] += dw1

    row = lambda wd: pl.BlockSpec((TM, wd), lambda i: (i, 0))
    return pl.pallas_call(
        body, name="in_proj_bwd", grid=(L // TM,),
        in_specs=[row(wd) for wd in widths] + [pl.BlockSpec(memory_space=pl.ANY), row(D), pl.BlockSpec((1, D), lambda i: (0, 0)), row(D)],
        out_specs=[row(D), pl.BlockSpec((8, D), lambda i: (0, 0))],
        out_shape=[SDS((L, D), f32), SDS((8, D), f32)],
        scratch_shapes=[pltpu.VMEM((D, NP), bf16), pltpu.SemaphoreType.DMA((1,))],
        compiler_params=_cp())(*dgroups, wp, h0, w1, dh1)


def _matmul_tn(at, b, name):
    M, L = at.shape
    N = b.shape[1]

    def body(a_ref, b_ref, o_ref):
        o_ref[...] = _b(_dot(a_ref[...], b_ref[...]))

    if M <= N:
        bn = N if N <= COL_TILES[0] else _pick(N, COL_TILES)
        grid, a_spec = (N // bn,), pl.BlockSpec((M, L), lambda n: (0, 0))
        b_spec, o_spec = pl.BlockSpec((L, bn), lambda n: (0, n)), pl.BlockSpec((M, bn), lambda n: (0, n))
    else:
        bm = _pick(M, ROW_BLOCKS)
        grid, a_spec = (M // bm,), pl.BlockSpec((bm, L), lambda m: (m, 0))
        b_spec, o_spec = pl.BlockSpec((L, N), lambda m: (0, 0)), pl.BlockSpec((bm, N), lambda m: (m, 0))
    return pl.pallas_call(body, name=name, grid=grid, in_specs=[a_spec, b_spec], out_specs=o_spec,
                          out_shape=SDS((M, N), bf16), compiler_params=_cp())(at, b)


def _fox_prep(fq, fk, sf, bias_p):
    L = fq.shape[0]
    T = HP
    NT = L // T

    def body(fq_ref, fk_ref, sf_ref, b_ref, qa_ref, ka_ref, carry):
        @pl.when(pl.program_id(0) == 0)
        def _():
            carry[...] = jnp.zeros_like(carry)

        lane, row = _iota((T, HP), 1), _iota((T, HP), 0)
        logf = jnp.where(lane < FOX_H, jax.nn.log_sigmoid(sf_ref[...] + b_ref[...]), 0.0)
        c = _dot_hi((row >= lane).astype(f32), logf) + carry[...]
        carry[...] = jnp.sum(jnp.where(row == T - 1, c, 0.0), axis=0, keepdims=True)
        ones_q = jnp.where((lane >= C_K0) & (lane < C_K0 + 3), 1.0, 0.0)
        ones_k = jnp.where((lane >= C_Q0) & (lane < C_Q0 + 3), 1.0, 0.0)
        for h in range(FOX_H):
            ch = jnp.broadcast_to(jnp.sum(jnp.where(lane == h, c, 0.0), axis=1, keepdims=True), (T, HP))
            c1 = _b(ch).astype(f32)
            c2 = _b(ch - c1).astype(f32)
            c3 = _b(ch - c1 - c2).astype(f32)
            cq = jnp.where(lane == C_Q0, c1, 0.0) + jnp.where(lane == C_Q0 + 1, c2, 0.0) + jnp.where(lane == C_Q0 + 2, c3, 0.0)
            ck = jnp.where(lane == C_K0, c1, 0.0) + jnp.where(lane == C_K0 + 1, c2, 0.0) + jnp.where(lane == C_K0 + 2, c3, 0.0)
            q = fq_ref[:, h * HP:(h + 1) * HP].astype(f32) * (FOX_D ** -0.5)
            k = fk_ref[:, h * HP:(h + 1) * HP].astype(f32)
            qa_ref[h] = _b(q + cq + ones_q)
            ka_ref[h] = _b(k + ones_k - ck)

    return pl.pallas_call(
        body, name="fox_prep", grid=(NT,),
        in_specs=[pl.BlockSpec((T, FOX_H * HP), lambda i: (i, 0)), pl.BlockSpec((T, FOX_H * HP), lambda i: (i, 0)),
                  pl.BlockSpec((T, HP), lambda i: (i, 0)), pl.BlockSpec((1, HP), lambda i: (0, 0))],
        out_specs=[pl.BlockSpec((FOX_H, T, HP), lambda i: (0, i, 0))] * 2,
        out_shape=[SDS((FOX_H, L, HP), bf16)] * 2,
        scratch_shapes=[pltpu.VMEM((1, HP), f32)], compiler_params=_cp())(fq, fk, sf, bias_p)


def _fox_prep_bwd(dqa, dka, sf, bias_p):
    L = sf.shape[0]
    T = HP
    NT = L // T
    rev = lambda i: (NT - 1 - i, 0)

    W = FOX_H * HP

    def body(dq_ref, dk_ref, sf_ref, b_ref, dg_ref, db_ref, carry):
        @pl.when(pl.program_id(0) == 0)
        def _():
            carry[...] = jnp.zeros_like(carry)
            db_ref[...] = jnp.zeros_like(db_ref)

        dq, dk = dq_ref[...], dk_ref[...]
        dg_ref[:, 0:W] = _b(dq * (FOX_D ** -0.5))
        dg_ref[:, W:2 * W] = _b(dk)
        lane, row = _iota((T, HP), 1), _iota((T, HP), 0)
        dc = jnp.zeros((T, HP), f32)
        for h in range(FOX_H):
            col = jnp.sum(jnp.where(lane == C_Q0, dq[:, h * HP:(h + 1) * HP], 0.0)
                          - jnp.where(lane == C_K0, dk[:, h * HP:(h + 1) * HP], 0.0), axis=1, keepdims=True)
            dc = dc + jnp.where(lane == h, col, 0.0)
        dl = _dot_hi((row <= lane).astype(f32), dc) + carry[...]
        carry[...] = jnp.sum(jnp.where(row == 0, dl, 0.0), axis=0, keepdims=True)
        dx = jnp.where(lane < FOX_H, dl * _sigmoid(-(sf_ref[...] + b_ref[...])), 0.0)
        dg_ref[:, 2 * W:2 * W + HP] = _b(dx)
        dg_ref[:, 2 * W + HP:] = jnp.zeros((T, SMALL_W - HP), bf16)
        db_ref[0:1, :] += jnp.sum(dx, axis=0, keepdims=True)

    return pl.pallas_call(
        body, name="fox_prep_bwd", grid=(NT,),
        in_specs=[pl.BlockSpec((T, W), rev), pl.BlockSpec((T, W), rev), pl.BlockSpec((T, HP), rev), pl.BlockSpec((1, HP), lambda i: (0, 0))],
        out_specs=[pl.BlockSpec((T, 2 * W + SMALL_W), rev), pl.BlockSpec((8, HP), lambda i: (0, 0))],
        out_shape=[SDS((L, 2 * W + SMALL_W), bf16), SDS((8, HP), f32)],
        scratch_shapes=[pltpu.VMEM((1, HP), f32)], compiler_params=_cp())(dqa, dka, sf, bias_p)


def _tile_start(j, T):
    return j * T if isinstance(j, int) else pl.multiple_of(j * T, T)


def _fox_fwd(qa, ka, fv, shards):
    L = qa.shape[1]
    TQ = TK = _pick(L, ATTN_TILES)
    NQ = L // TQ
    n = len(shards)
    HG = FOX_HEAD_GROUP

    def body(q_ref, k_ref, v_ref, *rest):
        ins, o_ref, outs, sems = rest[:n], rest[n], rest[n + 1:2 * n + 1], rest[2 * n + 1:]
        h, i = pl.program_id(0), pl.program_id(1)

        @pl.when((h == 0) & (i == 0))
        def _():
            _exchange_start(ins, outs, *sems, gather=True)

        qs = [q_ref[a] for a in range(HG)]
        rowg = i * TQ + _iota((TQ, TK), 0)
        colb = _iota((TQ, TK), 1)

        def step(j, carry, masked):
            ms, ls, accs = carry
            k0 = _tile_start(j, TK)
            ss = [_dot_nt(qs[a], k_ref[a, pl.ds(k0, TK), :]) for a in range(HG)]
            if masked:
                colg = colb + j * TK
                keep = (colg <= rowg) & (colg >= N_PAD)
                ss = [jnp.where(keep, s, NEG) for s in ss]
            m_new = [jnp.maximum(m, jnp.max(s, axis=1, keepdims=True)) for m, s in zip(ms, ss)]
            ps = [jnp.exp(s - m) for s, m in zip(ss, m_new)]
            alphas = [jnp.exp(m - mn) for m, mn in zip(ms, m_new)]
            ls = [al * l + jnp.sum(p, axis=1, keepdims=True) for al, l, p in zip(alphas, ls, ps)]
            accs = [al * acc + _dot(_b(p), v_ref[pl.ds(k0, TK), a * HP:(a + 1) * HP]) for a, (al, acc, p) in enumerate(zip(alphas, accs, ps))]
            return m_new, ls, accs

        init = ([jnp.full((TQ, 1), NEG, f32)] * HG, [jnp.zeros((TQ, 1), f32)] * HG, [jnp.zeros((TQ, HP), f32)] * HG)
        carry = step(0, init, True)
        carry = lax.fori_loop(1, i, functools.partial(step, masked=False), carry)
        ms, ls, accs = lax.fori_loop(jnp.maximum(i, 1), i + 1, functools.partial(step, masked=True), carry)
        lane = _iota((TQ, HP), 1)
        for a in range(HG):
            o_ref[:, a * HP:(a + 1) * HP] = jnp.where(lane == LSE_COL, ms[a] + jnp.log(ls[a]), accs[a] / ls[a])

        @pl.when((h == FOX_H // HG - 1) & (i == NQ - 1))
        def _():
            _exchange_wait(ins, outs, *sems, gather=True)

    anyspec = pl.BlockSpec(memory_space=pl.ANY)
    res = pl.pallas_call(
        body, name="fox_fwd", grid=(FOX_H // HG, NQ),
        in_specs=[pl.BlockSpec((HG, TQ, HP), lambda h, i: (h, i, 0)), pl.BlockSpec((HG, L, HP), lambda h, i: (h, 0, 0)),
                  pl.BlockSpec((L, HG * HP), lambda h, i: (0, h))] + [anyspec] * n,
        out_specs=[pl.BlockSpec((TQ, HG * HP), lambda h, i: (i, h))] + [anyspec] * n,
        out_shape=[SDS((L, FOX_H * HP), f32)] + [SDS((N_DEV,) + a.shape, a.dtype) for a in shards],
        scratch_shapes=_exchange_sems(n), compiler_params=_cp(2))(qa, ka, fv, *shards)
    return res[0], res[1:]


def _fox_bwd(qa, ka, fv, op, dop, slabs):
    L = qa.shape[1]
    TQ = TK = _pick(L, ATTN_TILES)
    NQ = L // TQ
    n = len(slabs)
    HG = FOX_HEAD_GROUP

    def body(q_ref, k_ref, v_ref, o_ref, do_ref, *rest):
        ins, (dq_ref, dk_ref, dv_ref), outs = rest[:n], rest[n:n + 3], rest[n + 3:2 * n + 3]
        lse_s, delta_s = rest[2 * n + 3:2 * n + 5]
        sems = rest[2 * n + 5:]
        h, j = pl.program_id(0), pl.program_id(1)
        cols = [slice(a * HP, (a + 1) * HP) for a in range(HG)]

        @pl.when((h == 0) & (j == 0))
        def _():
            _exchange_start(ins, outs, *sems, gather=False)

        lane = _iota((TQ, HP), 1)

        @pl.when(j == 0)
        def _():
            dq_ref[...] = jnp.zeros_like(dq_ref)
            for t in range(NQ):
                r = slice(t * TQ, (t + 1) * TQ)
                for a in range(HG):
                    o, do = o_ref[r, cols[a]], do_ref[r, cols[a]]
                    lse_s[a, r, :] = jnp.sum(jnp.where(lane == LSE_COL, o, 0.0), axis=1, keepdims=True)
                    delta_s[a, r, :] = jnp.sum(jnp.where(lane < FOX_D, o * do, 0.0), axis=1, keepdims=True)

        kts = [k_ref[a] for a in range(HG)]
        vts = [v_ref[:, cols[a]] for a in range(HG)]
        colg = j * TK + _iota((TQ, TK), 1)
        rowb = _iota((TQ, TK), 0)

        def step(i, carry, masked):
            dks, dvs = carry
            r0 = _tile_start(i, TQ)
            rows = pl.ds(r0, TQ)
            qs = [q_ref[a, rows, :] for a in range(HG)]
            ps = [jnp.exp(_dot_nt(q, kt) - lse_s[a, rows, :]) for a, (q, kt) in enumerate(zip(qs, kts))]
            if masked:
                keep = (colg <= rowb + i * TQ) & (colg >= N_PAD)
                ps = [jnp.where(keep, p, 0.0) for p in ps]
            dobs = [_b(do_ref[rows, cols[a]]) for a in range(HG)]
            dvs = [dv + _dot_tn(_b(p), dob) for dv, p, dob in zip(dvs, ps, dobs)]
            dss = [_b(p * (_dot_nt(dob, vt) - delta_s[a, rows, :])) for a, (p, dob, vt) in enumerate(zip(ps, dobs, vts))]
            for a in range(HG):
                dq_ref[rows, cols[a]] += _dot(dss[a], kts[a])
            dks = [dk + _dot_tn(ds, q) for dk, ds, q in zip(dks, dss, qs)]
            return dks, dvs

        zeros = [jnp.zeros((TK, HP), f32)] * HG
        carry = step(j, (zeros, zeros), True)
        split = jnp.where(j == 0, NQ, j + 1)
        carry = lax.fori_loop(j + 1, split, functools.partial(step, masked=True), carry)
        dks, dvs = lax.fori_loop(split, NQ, functools.partial(step, masked=False), carry)
        for a in range(HG):
            dk_ref[:, cols[a]] = dks[a]
            dv_ref[:, cols[a]] = _b(dvs[a])

        @pl.when((h == FOX_H // HG - 1) & (j == NQ - 1))
        def _():
            _exchange_wait(ins, outs, *sems, gather=False)

    head = pl.BlockSpec((L, HG * HP), lambda h, j: (0, h))
    tile = pl.BlockSpec((TK, HG * HP), lambda h, j: (j, h))
    anyspec = pl.BlockSpec(memory_space=pl.ANY)
    res = pl.pallas_call(
        body, name="fox_bwd", grid=(FOX_H // HG, L // TK),
        in_specs=[pl.BlockSpec((HG, L, HP), lambda h, j: (h, 0, 0)), pl.BlockSpec((HG, TK, HP), lambda h, j: (h, j, 0)), tile, head, head]
        + [anyspec] * n,
        out_specs=[head, tile, tile] + [anyspec] * n,
        out_shape=[SDS((L, FOX_H * HP), f32), SDS((L, FOX_H * HP), f32), SDS((L, FOX_H * HP), bf16)] + [SDS(a.shape, a.dtype) for a in slabs],
        scratch_shapes=[pltpu.VMEM((HG, L, 1), f32), pltpu.VMEM((HG, L, 1), f32)] + _exchange_sems(n),
        compiler_params=_cp(2))(qa, ka, fv, op, dop, *slabs)
    return res[:3], res[3:]


def _dn_post(y, sd, alog_p, dt_p, valid):
    a = y * _sigmoid(y)
    W = DN_H * DN_D
    heads = []
    for part, scale in ((0, DN_D ** -0.5), (1, 1.0)):
        for h in range(DN_H):
            xh = a[:, part * W + h * DN_D:part * W + (h + 1) * DN_D]
            heads.append(xh * lax.rsqrt(jnp.sum(xh * xh, axis=-1, keepdims=True) + EPS) * scale)
    q = jnp.concatenate(heads[:DN_H], axis=1)
    k = jnp.concatenate(heads[DN_H:], axis=1)
    v = a[:, 2 * W:3 * W]
    lane = _iota(sd.shape, 1)
    beta = _sigmoid(sd) * valid
    g = -jnp.exp(alog_p) * jax.nn.softplus(sd + dt_p) * valid
    bg = jnp.where(lane < DN_H, beta, jnp.where(lane < 2 * DN_H, g, 0.0))
    return q, k, v, bg


def _conv_fwd(ext_ref, cw_ref, TM):
    y = cw_ref[0:1, :] * ext_ref[8 - (CONV_K - 1):8 - (CONV_K - 1) + TM, :]
    for i in range(1, CONV_K):
        o = 8 - (CONV_K - 1) + i
        y = y + cw_ref[i:i + 1, :] * ext_ref[o:o + TM, :]
    return y


def _dn_prep(dn, sd, cw, alog_p, dt_p):
    L, W3 = dn.shape
    TM = _pick(L, ROW_TILES)
    W = DN_H * DN_D

    def body(dn_ref, halo_ref, sd_ref, cw_ref, al_ref, dt_ref, q_ref, k_ref, v_ref, bg_ref, ext):
        i = pl.program_id(0)
        ext[0:8, :] = jnp.where(i == 0, 0.0, halo_ref[...])
        ext[8:, :] = dn_ref[...]
        y = _conv_fwd(ext, cw_ref, TM)
        valid = ((i * TM + _iota((TM, 1), 0)) >= N_PAD).astype(f32)
        q, k, v, bg = _dn_post(y, sd_ref[...], al_ref[...], dt_ref[...], valid)
        q_ref[...], k_ref[...], v_ref[...], bg_ref[...] = q, k, v, bg

    row = lambda wd: pl.BlockSpec((TM, wd), lambda i: (i, 0))
    vec = pl.BlockSpec((1, HP), lambda i: (0, 0))
    return pl.pallas_call(
        body, name="dn_prep", grid=(L // TM,),
        in_specs=[row(W3), pl.BlockSpec((8, W3), lambda i: (jnp.maximum(i * (TM // 8) - 1, 0), 0)), row(HP),
                  pl.BlockSpec((CONV_K, W3), lambda i: (0, 0)), vec, vec],
        out_specs=[row(W), row(W), row(W), row(HP)],
        out_shape=[SDS((L, W), f32)] * 3 + [SDS((L, HP), f32)],
        scratch_shapes=[pltpu.VMEM((TM + 8, W3), f32)], compiler_params=_cp())(dn, dn, sd, cw, alog_p, dt_p)


def _dn_prep_bwd(dn, sd, cw, alog_p, dt_p, dq, dk, dv, dbg):
    L, W3 = dn.shape
    TM = _pick(L, ROW_TILES)
    NT = L // TM
    W = DN_H * DN_D

    def body(dn_ref, halo_ref, sd_ref, cw_ref, al_ref, dt_ref, dq_ref, dk_ref, dv_ref, dbg_ref,
             dg_ref, dcw_ref, dp_ref, ext, dyp, carry):
        i = pl.program_id(0)
        t = NT - 1 - i

        @pl.when(i == 0)
        def _():
            carry[...] = jnp.zeros_like(carry)
            dcw_ref[...] = jnp.zeros_like(dcw_ref)
            dp_ref[...] = jnp.zeros_like(dp_ref)
            dyp[...] = jnp.zeros_like(dyp)

        ext[0:8, :] = jnp.where(t == 0, 0.0, halo_ref[...])
        ext[8:, :] = dn_ref[...]
        y = _conv_fwd(ext, cw_ref, TM)
        valid = ((t * TM + _iota((TM, 1), 0)) >= N_PAD).astype(f32)
        _, vjp = jax.vjp(functools.partial(_dn_post, valid=valid), y, sd_ref[...], al_ref[...], dt_ref[...])
        dy, dsd, dal, ddt = vjp((dq_ref[...], dk_ref[...], dv_ref[...], dbg_ref[...]))
        dg_ref[:, W3:W3 + HP] = _b(dsd)
        dg_ref[:, W3 + HP:] = jnp.zeros((TM, SMALL_W - HP), bf16)
        dp_ref[0:1, :] += dal
        dp_ref[1:2, :] += ddt
        dyp[8:8 + TM, :] = dy
        o0 = CONV_K - 1
        dext = cw_ref[0:1, :] * dyp[o0:o0 + TM + 8, :]
        for k in range(1, CONV_K):
            dext = dext + cw_ref[k:k + 1, :] * dyp[o0 - k:o0 - k + TM + 8, :]
        for k in range(CONV_K):
            o = 8 - (CONV_K - 1) + k
            dcw_ref[k:k + 1, :] += jnp.sum(dy * ext[o:o + TM, :], axis=0, keepdims=True)
        dg_ref[:, 0:W3] = _b(jnp.concatenate([dext[8:TM, :], dext[TM:TM + 8, :] + carry[...]], axis=0))
        carry[...] = dext[0:8, :]

    row = lambda wd: pl.BlockSpec((TM, wd), lambda i: (NT - 1 - i, 0))
    vec = pl.BlockSpec((1, HP), lambda i: (0, 0))
    return pl.pallas_call(
        body, name="dn_prep_bwd", grid=(NT,),
        in_specs=[row(W3), pl.BlockSpec((8, W3), lambda i: (jnp.maximum((NT - 1 - i) * (TM // 8) - 1, 0), 0)), row(HP),
                  pl.BlockSpec((CONV_K, W3), lambda i: (0, 0)), vec, vec, row(W), row(W), row(W), row(HP)],
        out_specs=[row(W3 + SMALL_W), pl.BlockSpec((8, W3), lambda i: (0, 0)), pl.BlockSpec((8, HP), lambda i: (0, 0))],
        out_shape=[SDS((L, W3 + SMALL_W), bf16), SDS((8, W3), f32), SDS((8, HP), f32)],
        scratch_shapes=[pltpu.VMEM((TM + 8, W3), f32), pltpu.VMEM((TM + 16, W3), f32), pltpu.VMEM((8, W3), f32)],
        compiler_params=_cp())(dn, dn, sd, cw, alog_p, dt_p, dq, dk, dv, dbg)


def _unit_lower_inverses(lows):
    C = lows[0].shape[0]
    P = jnp.stack(lows)
    X = (_iota((C, C), 0) == _iota((C, C), 1)).astype(f32)[None] - P
    bdot = lambda a, b: jnp.einsum("bij,bjk->bik", a, b, preferred_element_type=f32, precision=HI)
    for _ in range(5):
        P = bdot(P, P)
        X = X + bdot(X, P)
    return [X[i] for i in range(len(lows))]


@jax.custom_vjp
def _inverse_given(low, X):
    return X


def _inverse_given_bwd(X, g):
    t = lax.dot_general(X, g, (((0,), (0,)), ((), ())), preferred_element_type=f32, precision=HI)
    return -lax.dot_general(t, X, (((1,), (1,)), ((), ())), preferred_element_type=f32, precision=HI), jnp.zeros_like(X)


_inverse_given.defvjp(lambda low, X: (X, X), _inverse_given_bwd)


def _dn_intra_pre(q, k, v, bg):
    C = DN_C
    row, col = _iota((C, C), 0), _iota((C, C), 1)
    tri = row >= col
    eye = (row == col).astype(f32)
    G = _dot_hi(tri.astype(f32), bg)
    GT = lax.dot_general(G, eye, (((0,), (0,)), ((), ())), preferred_element_type=f32, precision=HI)
    lane = _iota((C, HP), 1)
    rowt = _iota((HP, C), 0)
    last = _iota((C, 1), 0) == C - 1
    heads = []
    for h in range(DN_H):
        beta = jnp.sum(jnp.where(lane == h, bg, 0.0), axis=1, keepdims=True)
        gcol = jnp.sum(jnp.where(lane == DN_H + h, G, 0.0), axis=1, keepdims=True)
        grow = jnp.sum(jnp.where(rowt == DN_H + h, GT, 0.0), axis=0, keepdims=True)
        glast = jnp.sum(jnp.where(last, gcol, 0.0), axis=0, keepdims=True)
        decay = jnp.exp(jnp.where(tri, gcol - grow, NEG))
        qh, kh, vh = (t[:, h * DN_D:(h + 1) * DN_D] for t in (q, k, v))
        kb = kh * beta
        low = jnp.where(row > col, _dot_nt(_b(kb), _b(kh)) * decay, 0.0)
        heads.append((beta, gcol, glast, decay, qh, kh, vh, kb, low))
    return heads


def _dn_intra_post(heads, xs):
    lane1 = _iota((1, HP), 1)
    us, ws, qds, kds, attns = [], [], [], [], []
    glrow = jnp.zeros((1, HP), f32)
    for h, ((beta, gcol, glast, decay, qh, kh, vh, kb, _), X) in enumerate(zip(heads, xs)):
        eg = jnp.exp(gcol)
        us.append(_dot_hi(X, vh * beta))
        ws.append(_dot_hi(X, kb * eg))
        attns.append(_dot_nt(_b(qh), _b(kh)) * decay)
        qds.append(qh * eg)
        kds.append(kh * jnp.exp(glast - gcol))
        glrow = glrow + jnp.where(lane1 == h, glast, 0.0)
    cat = lambda xs_: jnp.concatenate(xs_, axis=1)
    return cat(us), cat(ws), cat(qds), cat(kds), cat(attns), glrow, cat(list(xs))


def _dn_intra_group(q, k, v, bg, xs):
    G = q.shape[0] // DN_C
    rows = [slice(j * DN_C, (j + 1) * DN_C) for j in range(G)]
    pre = [_dn_intra_pre(q[r, :], k[r, :], v[r, :], bg[r, :]) for r in rows]
    inv = [[_inverse_given(hd[-1], x) for hd, x in zip(heads, xj)] for heads, xj in zip(pre, xs)]
    post = [_dn_intra_post(heads, xj) for heads, xj in zip(pre, inv)]
    return tuple(jnp.concatenate([p[i] for p in post], axis=0) for i in range(5)) + (tuple(p[5] for p in post),)


def _lane_pick(rowvec, h):
    return jnp.sum(jnp.where(_iota(rowvec.shape, 1) == h, rowvec, 0.0), axis=1, keepdims=True)


def _dn_intra(q, k, v, bg):
    L, W = q.shape
    NC = L // DN_C
    G = _pick(NC, DN_INTRA_GROUP)
    R = G * DN_C
    WA = DN_H * DN_C

    def body(q_ref, k_ref, v_ref, bg_ref, u_ref, w_ref, qd_ref, kd_ref, at_ref, gl_ref, x_ref):
        rows = [slice(j * DN_C, (j + 1) * DN_C) for j in range(G)]
        pre = [_dn_intra_pre(q_ref[r, :], k_ref[r, :], v_ref[r, :], bg_ref[r, :]) for r in rows]
        inv = _unit_lower_inverses([hd[-1] for heads in pre for hd in heads])
        for j, r in enumerate(rows):
            u, w, qd, kd, at, gl, xs = _dn_intra_post(pre[j], inv[j * DN_H:(j + 1) * DN_H])
            u_ref[r, :], x_ref[r, :] = u, xs
            w_ref[r, :], qd_ref[r, :], kd_ref[r, :], at_ref[r, :] = _b(w), _b(qd), _b(kd), _b(at)
            gl_ref[j] = gl

    row = lambda wd: pl.BlockSpec((R, wd), lambda n: (n, 0))
    return pl.pallas_call(
        body, name="dn_intra", grid=(NC // G,),
        in_specs=[row(W), row(W), row(W), row(HP)],
        out_specs=[row(W), row(W), row(W), row(W), row(WA), pl.BlockSpec((G, 1, HP), lambda n: (n, 0, 0)), row(WA)],
        out_shape=[SDS((L, W), f32), SDS((L, W), bf16), SDS((L, W), bf16), SDS((L, W), bf16), SDS((L, WA), bf16), SDS((NC, 1, HP), f32),
                   SDS((L, WA), f32)],
        compiler_params=_cp())(q, k, v, bg)


def _dn_scan(u, w, qd, kd, at, gl):
    L, W = u.shape
    NC = L // DN_C
    G = _pick(NC, DN_SCAN_GROUP)
    R = G * DN_C

    def body(u_ref, w_ref, qd_ref, kd_ref, at_ref, gl_ref, o_ref, vn_ref, s_ref, S):
        @pl.when(pl.program_id(0) == 0)
        def _():
            S[...] = jnp.zeros_like(S)

        for j in range(G):
            r = slice(j * DN_C, (j + 1) * DN_C)
            glrow = gl_ref[j]
            for h in range(DN_H):
                c = slice(h * DN_D, (h + 1) * DN_D)
                Sh = S[h]
                s_ref[j, h] = Sh
                Sb = _b(Sh)
                vb = _b(u_ref[r, c] - _dot(w_ref[r, c], Sb))
                vn_ref[r, c] = vb
                o_ref[r, c] = _dot(qd_ref[r, c], Sb) + _dot(at_ref[r, h * DN_C:(h + 1) * DN_C], vb)
                S[h] = Sh * jnp.exp(_lane_pick(glrow, h)) + _dot_tn(kd_ref[r, c], vb)

    row = lambda wd: pl.BlockSpec((R, wd), lambda n: (n, 0))
    return pl.pallas_call(
        body, name="dn_scan", grid=(NC // G,),
        in_specs=[row(W), row(W), row(W), row(W), row(DN_H * DN_C), pl.BlockSpec((G, 1, HP), lambda n: (n, 0, 0))],
        out_specs=[row(W), row(W), pl.BlockSpec((G, DN_H, DN_D, DN_D), lambda n: (n, 0, 0, 0))],
        out_shape=[SDS((L, W), f32), SDS((L, W), bf16), SDS((NC, DN_H, DN_D, DN_D), f32)],
        scratch_shapes=[pltpu.VMEM((DN_H, DN_D, DN_D), f32)], compiler_params=_cp())(u, w, qd, kd, at, gl)


def _dn_scan_bwd(do, w, qd, kd, at, gl):
    L, W = do.shape
    NC = L // DN_C
    G = _pick(NC, DN_SCAN_GROUP)
    R = G * DN_C
    NS = NC // G

    def body(do_ref, w_ref, qd_ref, kd_ref, at_ref, gl_ref, dvn_ref, ds_ref, dS):
        @pl.when(pl.program_id(0) == 0)
        def _():
            dS[...] = jnp.zeros_like(dS)

        for j in reversed(range(G)):
            r = slice(j * DN_C, (j + 1) * DN_C)
            glrow = gl_ref[j]
            for h in range(DN_H):
                c = slice(h * DN_D, (h + 1) * DN_D)
                dSo = dS[h]
                ds_ref[j, h] = dSo
                dob = _b(do_ref[r, c])
                dvn = _dot_tn(at_ref[r, h * DN_C:(h + 1) * DN_C], dob) + _dot(kd_ref[r, c], _b(dSo))
                dvn_ref[r, c] = dvn
                dS[h] = _dot_tn(qd_ref[r, c], dob) + dSo * jnp.exp(_lane_pick(glrow, h)) - _dot_tn(w_ref[r, c], _b(dvn))

    row = lambda wd: pl.BlockSpec((R, wd), lambda n: (NS - 1 - n, 0))
    return pl.pallas_call(
        body, name="dn_scan_bwd", grid=(NS,),
        in_specs=[row(W), row(W), row(W), row(W), row(DN_H * DN_C), pl.BlockSpec((G, 1, HP), lambda n: (NS - 1 - n, 0, 0))],
        out_specs=[row(W), pl.BlockSpec((G, DN_H, DN_D, DN_D), lambda n: (NS - 1 - n, 0, 0, 0))],
        out_shape=[SDS((L, W), f32), SDS((NC, DN_H, DN_D, DN_D), f32)],
        scratch_shapes=[pltpu.VMEM((DN_H, DN_D, DN_D), f32)], compiler_params=_cp())(do, w, qd, kd, at, gl)


def _dn_intra_bwd(q, k, v, bg, xinv, do, vn, dvn, states, dstates):
    L, W = q.shape
    NC = L // DN_C
    G = _pick(NC, DN_INTRA_GROUP)
    R = G * DN_C

    def body(q_ref, k_ref, v_ref, bg_ref, x_ref, do_ref, vn_ref, dvn_ref, s_ref, ds_ref, dq_ref, dk_ref, dv_ref, dbg_ref):
        lane1 = _iota((1, HP), 1)
        rows = [slice(j * DN_C, (j + 1) * DN_C) for j in range(G)]
        xs = [[x_ref[r, h * DN_C:(h + 1) * DN_C] for h in range(DN_H)] for r in rows]
        outs, vjp = jax.vjp(functools.partial(_dn_intra_group, xs=xs), q_ref[...], k_ref[...], v_ref[...], bg_ref[...])
        dws, dqds, dkds, dats, dgls = [], [], [], [], []
        for j, r in enumerate(rows):
            dw, dqd, dkd, dat = [], [], [], []
            dgl = jnp.zeros((1, HP), f32)
            for h in range(DN_H):
                c = slice(h * DN_D, (h + 1) * DN_D)
                Sh, dSo = s_ref[j, h], ds_ref[j, h]
                Sb, dob, vb = _b(Sh), _b(do_ref[r, c]), vn_ref[r, c]
                dw.append(-_dot_nt(_b(dvn_ref[r, c]), Sb))
                dqd.append(_dot_nt(dob, Sb))
                dat.append(_dot_nt(dob, vb))
                dkd.append(_dot_nt(vb, _b(dSo)))
                dcd = jnp.sum(jnp.sum(Sh * dSo, axis=1, keepdims=True), axis=0, keepdims=True)
                dgl = dgl + jnp.where(lane1 == h, dcd * jnp.exp(_lane_pick(outs[5][j], h)), 0.0)
            cat = lambda xs_: jnp.concatenate(xs_, axis=1)
            dws.append(cat(dw)), dqds.append(cat(dqd)), dkds.append(cat(dkd)), dats.append(cat(dat)), dgls.append(dgl)
        cat0 = lambda xs_: jnp.concatenate(xs_, axis=0)
        dq, dk, dv, dbg = vjp((dvn_ref[...], cat0(dws), cat0(dqds), cat0(dkds), cat0(dats), tuple(dgls)))
        dq_ref[...], dk_ref[...], dv_ref[...], dbg_ref[...] = dq, dk, dv, dbg

    row = lambda wd: pl.BlockSpec((R, wd), lambda n: (n, 0))
    st = pl.BlockSpec((G, DN_H, DN_D, DN_D), lambda n: (n, 0, 0, 0))
    return pl.pallas_call(
        body, name="dn_intra_bwd", grid=(NC // G,),
        in_specs=[row(W), row(W), row(W), row(HP), row(DN_H * DN_C), row(W), row(W), row(W), st, st],
        out_specs=[row(W), row(W), row(W), row(HP)],
        out_shape=[SDS((L, W), f32)] * 3 + [SDS((L, HP), f32)],
        compiler_params=_cp())(q, k, v, bg, xinv, do, vn, dvn, states, dstates)


def _dn_normgate(oraw, dz, wn):
    outs = []
    for h in range(DN_H):
        sl = slice(h * DN_D, (h + 1) * DN_D)
        z = dz[:, sl]
        outs.append(_rms(oraw[:, sl], wn) * (z * _sigmoid(z)))
    return jnp.concatenate(outs, axis=1)


def _mix_fwd(op, oraw, dz, ga, gb, h0, wn, wbf, wbd, wo):
    L, D = h0.shape
    TM = _pick(L, ROW_TILES)

    def body(op_ref, or_ref, dz_ref, ga_ref, gb_ref, h0_ref, wn_ref, wbf_ref, wbd_ref, wo_ref, h1_ref):
        pf = _dot(_b(op_ref[...]), wbf_ref[...])
        pd = _dot(_b(_dn_normgate(or_ref[...], dz_ref[...], wn_ref[...])), wbd_ref[...])
        y = _sigmoid(ga_ref[...]) * pf + _sigmoid(gb_ref[...]) * pd
        h1_ref[...] = h0_ref[...] + _dot(_b(y), wo_ref[...])

    row = lambda wd: pl.BlockSpec((TM, wd), lambda i: (i, 0))
    full = lambda a: pl.BlockSpec(a.shape, lambda i: (0, 0))
    return pl.pallas_call(
        body, name="mix_fwd", grid=(L // TM,),
        in_specs=[row(op.shape[1]), row(oraw.shape[1]), row(dz.shape[1]), row(D), row(D), row(D), full(wn), full(wbf), full(wbd), full(wo)],
        out_specs=row(D), out_shape=SDS((L, D), f32), compiler_params=_cp())(op, oraw, dz, ga, gb, h0, wn, wbf, wbd, wo)


def _mix_bwd(dh1, op, oraw, dz, ga, gb, wn, wbf, wbd, wo):
    L, D = dh1.shape
    TM = _pick(L, ROW_TILES)
    WF, WD = op.shape[1], oraw.shape[1]

    def body(dh1_ref, op_ref, or_ref, dz_ref, ga_ref, gb_ref, wn_ref, wbf_ref, wbd_ref, wo_ref,
             dop_ref, dor_ref, dg_ref, aft_ref, adt_ref, dpf_ref, dpd_ref, yt_ref, dmix_ref, acc_ref):
        @pl.when(pl.program_id(0) == 0)
        def _():
            acc_ref[...] = jnp.zeros_like(acc_ref)

        opv = op_ref[...]
        af = _b(opv)
        ad, vjp = jax.vjp(_dn_normgate, or_ref[...], dz_ref[...], wn_ref[...])
        adb = _b(ad)
        pf, pd = _dot(af, wbf_ref[...]), _dot(adb, wbd_ref[...])
        sa, sb = _sigmoid(ga_ref[...]), _sigmoid(gb_ref[...])
        dmix = _b(dh1_ref[...])
        dy = _dot_nt(dmix, wo_ref[...])
        dpf, dpd = _b(dy * sa), _b(dy * sb)
        dor, ddz, dwn = vjp(_dot_nt(dpd, wbd_ref[...]))
        dop_ref[...] = _dot_nt(dpf, wbf_ref[...])
        dor_ref[...] = dor
        dg_ref[:, 0:WD] = _b(ddz)
        dg_ref[:, WD:WD + D] = _b(dy * pf * sa * (1.0 - sa))
        dg_ref[:, WD + D:] = _b(dy * pd * sb * (1.0 - sb))
        aft_ref[...], adt_ref[...], yt_ref[...] = _b(opv.T), _b(ad.T), _b((sa * pf + sb * pd).T)
        dpf_ref[...], dpd_ref[...], dmix_ref[...] = dpf, dpd, dmix
        acc_ref[0:1, :] += dwn

    row = lambda wd: pl.BlockSpec((TM, wd), lambda i: (i, 0))
    col = lambda wd: pl.BlockSpec((wd, TM), lambda i: (0, i))
    full = lambda a: pl.BlockSpec(a.shape, lambda i: (0, 0))
    return pl.pallas_call(
        body, name="mix_bwd", grid=(L // TM,),
        in_specs=[row(D), row(WF), row(WD), row(WD), row(D), row(D), full(wn), full(wbf), full(wbd), full(wo)],
        out_specs=[row(WF), row(WD), row(WD + 2 * D), col(WF), col(WD), row(D), row(D), col(D), row(D),
                   pl.BlockSpec((8, HP), lambda i: (0, 0))],
        out_shape=[SDS((L, WF), f32), SDS((L, WD), f32), SDS((L, WD + 2 * D), bf16), SDS((WF, L), bf16), SDS((WD, L), bf16),
                   SDS((L, D), bf16), SDS((L, D), bf16), SDS((D, L), bf16), SDS((L, D), bf16), SDS((8, HP), f32)],
        compiler_params=_cp())(dh1, op, oraw, dz, ga, gb, wn, wbf, wbd, wo)


def _ffn_fwd_bwd(h1, tgt, w2, wf, wg, wu, wd):
    L, D = h1.shape
    F = wg.shape[1]
    TM = _pick(L, FFN_TILES)

    def body(h_ref, t_ref, w2_ref, wf_ref, wg_hbm, wu_hbm, wd_hbm,
             dh1_ref, xn_ref, dg_ref, du_ref, act_ref, dh2_ref, acc_ref, wg_v, wu_v, wd_v, sems):
        i = pl.program_id(0)
        _load_once([(wg_hbm, wg_v), (wu_hbm, wu_v), (wd_hbm, wd_v)], sems)

        @pl.when(i == 0)
        def _():
            acc_ref[...] = jnp.zeros_like(acc_ref)

        h1v = h_ref[...]
        xn2, vjp2 = jax.vjp(_rms, h1v, w2_ref[...])
        xb = _b(xn2)
        g, u = _dot(xb, wg_v[...]), _dot(xb, wu_v[...])
        sg = _sigmoid(g)
        ab = _b(g * sg * u)
        h2 = h1v + _dot(ab, wd_v[...])
        out, vjpf = jax.vjp(_rms, h2, wf_ref[...])
        valid = (i * TM + _iota((TM, 1), 0)) >= PREFIX
        diff = jnp.where(valid, out - t_ref[...], 0.0)
        loss = 0.5 * jnp.sum(jnp.sum(diff * diff, axis=1, keepdims=True), axis=0, keepdims=True) / D
        dh2, dwf = vjpf(diff * (1.0 / D))
        dh2b = _b(dh2)
        dact = _dot_nt(dh2b, wd_v[...])
        dgb = _b(dact * u * (sg * (1.0 + g * (1.0 - sg))))
        dub = _b(dact * (g * sg))
        dh1n, dw2 = vjp2(_dot_nt(dgb, wg_v[...]) + _dot_nt(dub, wu_v[...]))
        dh1_ref[...] = dh2 + dh1n
        xn_ref[...], dg_ref[...], du_ref[...], act_ref[...], dh2_ref[...] = xb, dgb, dub, ab, dh2b
        acc_ref[0:1, :] += dw2
        acc_ref[1:2, :] += dwf
        acc_ref[2:3, :] += jnp.broadcast_to(loss, (1, D))

    row = lambda wd_: pl.BlockSpec((TM, wd_), lambda i: (i, 0))
    vec = pl.BlockSpec((1, D), lambda i: (0, 0))
    anyspec = pl.BlockSpec(memory_space=pl.ANY)
    return pl.pallas_call(
        body, name="ffn_fwd_bwd", grid=(L // TM,),
        in_specs=[row(D), row(D), vec, vec, anyspec, anyspec, anyspec],
        out_specs=[row(D), row(D), row(F), row(F), row(F), row(D), pl.BlockSpec((8, D), lambda i: (0, 0))],
        out_shape=[SDS((L, D), f32), SDS((L, D), bf16), SDS((L, F), bf16), SDS((L, F), bf16), SDS((L, F), bf16), SDS((L, D), bf16),
                   SDS((8, D), f32)],
        scratch_shapes=[pltpu.VMEM((D, F), bf16), pltpu.VMEM((D, F), bf16), pltpu.VMEM((F, D), bf16), pltpu.SemaphoreType.DMA((3,))],
        compiler_params=_cp())(h1, tgt, w2, wf, wg, wu, wd)


def _pad_lanes(v, n=HP):
    return jnp.pad(v.astype(f32), ((0, 0), (0, n - v.shape[1])))


def _pack_w_in(w_full):
    D = w_full.shape[0]
    FW, DW = FOX_H * FOX_D, DN_H * DN_D
    o = 0
    parts = {}
    for name, wd in (("fq", FW), ("fk", FW), ("fv", FW), ("fl", FOX_H), ("dn", 3 * DW), ("ba", 2 * DN_H), ("dz", DW), ("ga", D), ("gb", D)):
        parts[name] = w_full[:, o:o + wd]
        o += wd
    assert o == w_full.shape[1]
    heads = lambda w: jnp.pad(w.reshape(D, FOX_H, FOX_D), ((0, 0), (0, 0), (0, HP - FOX_D))).reshape(D, FOX_H * HP)
    small = lambda w: jnp.pad(w, ((0, 0), (0, SMALL_W - w.shape[1])))
    packed = dict(fq=heads(parts["fq"]), fk=heads(parts["fk"]), fv=heads(parts["fv"]), sf=small(parts["fl"]), sd=small(parts["ba"]),
                  dn=parts["dn"], dz=parts["dz"], ga=parts["ga"], gb=parts["gb"])
    return jnp.concatenate([packed[name] for name, _, _, _ in _seg_layout(D)], axis=1)


def _unpack_w_in(groups, d_model):
    D = groups[0].shape[0]
    FW = FOX_H * FOX_D
    segs = {}
    for grp, g in zip(GROUPS, groups):
        o = 0
        for name, wd, _, sg in _seg_layout(d_model):
            if sg == grp:
                segs[name] = g[:, o:o + wd]
                o += wd
    heads = lambda g: g.reshape(D, FOX_H, HP)[:, :, :FOX_D].reshape(D, FW)
    return jnp.concatenate([heads(segs["fq"]), heads(segs["fk"]), heads(segs["fv"]), segs["sf"][:, :FOX_H], segs["dn"],
                            segs["sd"][:, :2 * DN_H], segs["dz"], segs["ga"], segs["gb"]], axis=1)


def _local_step(x, tgt, meta, w1, w_in, fbias, cw, alog, dtb, wn, w2, wf, late_shards):
    T, D = x.shape
    h0 = jnp.concatenate([jnp.zeros((N_PAD, D), f32), meta, x], axis=0)
    tgt_p = jnp.concatenate([jnp.zeros((PREFIX, D), f32), tgt], axis=0)
    wp = _pack_w_in(w_in)
    bias_p, alog_p, dt_p = _pad_lanes(fbias), _pad_lanes(jnp.pad(alog, ((0, 0), (DN_H, 0)))), _pad_lanes(jnp.pad(dtb, ((0, 0), (DN_H, 0))))

    xnt, fq, fk, sf, fv, dn, sd, dz, ga, gb = _in_proj(h0, w1, wp)
    qa, ka = _fox_prep(fq, fk, sf, bias_p)
    op, gathered = _fox_fwd(qa, ka, fv, [late_shards[n] for n in LATE])
    full = {n: _from_slabs(n, s) for n, s in zip(LATE, gathered)}
    wbf, wbd, wo, wg, wu, wd = (full[n] for n in ("w_branch_fox", "w_branch_dn", "w_out", "w_ffn_gate", "w_ffn_up", "w_ffn_down"))
    wbf_p = jnp.pad(wbf.reshape(FOX_H, FOX_D, D), ((0, 0), (0, HP - FOX_D), (0, 0))).reshape(FOX_H * HP, D)
    qn, kn, vn, bg = _dn_prep(dn, sd, cw, alog_p, dt_p)
    u_dn, w_dn, qd_dn, kd_dn, at_dn, gl_dn, x_dn = _dn_intra(qn, kn, vn, bg)
    oraw, vnew, states = _dn_scan(u_dn, w_dn, qd_dn, kd_dn, at_dn, gl_dn)
    h1 = _mix_fwd(op, oraw, dz, ga, gb, h0, wn, wbf_p, wbd, wo)

    dh1, xn2, dgate, dup, act, dh2, acc_f = _ffn_fwd_bwd(h1, tgt_p, w2, wf, wg, wu, wd)
    xn2t = xn2.T
    g_wg, g_wu, g_wd = _matmul_tn(xn2t, dgate, "dw_ffn_gate"), _matmul_tn(xn2t, dup, "dw_ffn_up"), _matmul_tn(act.T, dh2, "dw_ffn_down")

    dop, dor, d_mix, aft, adt, dpf, dpd, yt, dmix, acc_m = _mix_bwd(dh1, op, oraw, dz, ga, gb, wn, wbf_p, wbd, wo)
    g_wbf = _matmul_tn(aft, dpf, "dw_branch_fox").reshape(FOX_H, HP, D)[:, :FOX_D].reshape(FOX_H * FOX_D, D)
    g_wbd, g_wo = _matmul_tn(adt, dpd, "dw_branch_dn"), _matmul_tn(yt, dmix, "dw_out")

    dvnew, dstates = _dn_scan_bwd(dor, w_dn, qd_dn, kd_dn, at_dn, gl_dn)
    dqn, dkn, dvn, dbg = _dn_intra_bwd(qn, kn, vn, bg, x_dn, dor, vnew, dvnew, states, dstates)
    d_dn, acc_cw, acc_p = _dn_prep_bwd(dn, sd, cw, alog_p, dt_p, dqn, dkn, dvn, dbg)
    g_late = dict(w_branch_fox=g_wbf, w_branch_dn=g_wbd, w_out=g_wo, w_ffn_gate=g_wg, w_ffn_up=g_wu, w_ffn_down=g_wd)
    (dqa, dka, d_fv), recv = _fox_bwd(qa, ka, fv, op, dop, [_to_slabs(n, g_late[n]) for n in LATE])
    d_fox, acc_b = _fox_prep_bwd(dqa, dka, sf, bias_p)

    dgroups = [d_fox, d_fv, d_dn, d_mix]
    g_wp = [_matmul_tn(xnt, dg, "dw_in_" + grp) for grp, dg in zip(GROUPS, dgroups)]
    dh0, acc_1 = _in_proj_bwd(dgroups, wp, h0, w1, dh1)

    small = dict(loss=acc_f[2, 0:1], mix_norm_w=acc_1[0], fox_forget_bias=acc_b[0, :FOX_H], dn_a_log=acc_p[0, DN_H:2 * DN_H],
                 dn_dt_bias=acc_p[1, DN_H:2 * DN_H], dn_out_norm_w=acc_m[0], ffn_norm_w=acc_f[0], final_norm_w=acc_f[1],
                 meta_tokens=dh0[N_PAD:PREFIX].reshape(-1), dn_conv_w=acc_cw[:CONV_K].reshape(-1))
    return dh0[PREFIX:], small, _unpack_w_in(g_wp, D), dict(zip(LATE, recv))


def _mesh_pos():
    x, y, c = lax.axis_index("x"), lax.axis_index("y"), lax.axis_index("c")
    return x, y, c, 4 * x + 2 * y + c


def _peer(x, y, c, m):
    flip = lambda v, on: 1 - v if on else v
    px, py, pc = flip(x, m & 4), flip(y, m & 2), flip(c, m & 1)
    return (px, py, pc), 4 * px + 2 * py + pc


def _exchange_sems(n):
    return [pltpu.SemaphoreType.DMA((n, N_DEV - 1)), pltpu.SemaphoreType.DMA((n, N_DEV - 1)), pltpu.SemaphoreType.DMA((n,))]


def _exchange_copies(ins, outs, send_sems, recv_sems, loc_sems, gather, with_receives):
    x, y, c, me = _mesh_pos()
    src = lambda a, pid: ins[a] if gather else ins[a].at[pid]
    local = [pltpu.make_async_copy(src(a, me), outs[a].at[me], loc_sems.at[a]) for a in range(len(ins))]
    sends, recvs = [], []
    for m in range(1, N_DEV):
        peer, pid = _peer(x, y, c, m)
        for a in range(len(ins)):
            kw = dict(send_sem=send_sems.at[a, m - 1], recv_sem=recv_sems.at[a, m - 1], device_id=peer, device_id_type=MESH)
            sends.append(pltpu.make_async_remote_copy(src_ref=src(a, pid), dst_ref=outs[a].at[me], **kw))
            if with_receives:
                recvs.append(pltpu.make_async_remote_copy(src_ref=src(a, pid), dst_ref=outs[a].at[pid], **kw))
    return local, sends, recvs


def _exchange_start(ins, outs, send_sems, recv_sems, loc_sems, gather):
    local, sends, _ = _exchange_copies(ins, outs, send_sems, recv_sems, loc_sems, gather, with_receives=False)
    for cp in local + sends:
        cp.start()


def _exchange_wait(ins, outs, send_sems, recv_sems, loc_sems, gather):
    local, sends, recvs = _exchange_copies(ins, outs, send_sems, recv_sems, loc_sems, gather, with_receives=True)
    for cp in recvs:
        cp.wait_recv()
    for cp in sends:
        cp.wait_send()
    for cp in local:
        cp.wait()


def _exchange(arrays, name, gather):
    n = len(arrays)
    shapes = [a.shape if gather else a.shape[1:] for a in arrays]

    def body(*refs):
        ins, outs, sems = refs[:n], refs[n:2 * n], refs[2 * n:]
        _exchange_start(ins, outs, *sems, gather=gather)
        _exchange_wait(ins, outs, *sems, gather=gather)

    anyspec = pl.BlockSpec(memory_space=pl.ANY)
    return pl.pallas_call(
        body, name=name, in_specs=[anyspec] * n, out_specs=[anyspec] * n,
        out_shape=[SDS((N_DEV,) + tuple(s), a.dtype) for s, a in zip(shapes, arrays)],
        scratch_shapes=_exchange_sems(n))(*arrays)


def _all_reduce_small(v):
    R = v.shape[0]

    def body(v_ref, o_ref, gath, send_sems, recv_sems):
        x, y, c, me = _mesh_pos()
        gath[me] = v_ref[...]
        sends = []
        for m in range(1, N_DEV):
            peer, _ = _peer(x, y, c, m)
            cp = pltpu.make_async_remote_copy(src_ref=v_ref, dst_ref=gath.at[me], send_sem=send_sems.at[m - 1],
                                              recv_sem=recv_sems.at[m - 1], device_id=peer, device_id_type=MESH)
            cp.start()
            sends.append(cp)
        for m in range(1, N_DEV):
            peer, pid = _peer(x, y, c, m)
            pltpu.make_async_remote_copy(src_ref=v_ref, dst_ref=gath.at[pid], send_sem=send_sems.at[m - 1],
                                         recv_sem=recv_sems.at[m - 1], device_id=peer, device_id_type=MESH).wait_recv()
        for cp in sends:
            cp.wait_send()
        tot = gath[0]
        for d in range(1, N_DEV):
            tot = tot + gath[d]
        o_ref[...] = tot

    vm = pl.BlockSpec(memory_space=pltpu.VMEM)
    return pl.pallas_call(
        body, name="all_reduce_small", in_specs=[vm], out_specs=vm, out_shape=SDS((R, HP), f32),
        scratch_shapes=[pltpu.VMEM((N_DEV, R, HP), f32), pltpu.SemaphoreType.DMA((N_DEV - 1,)), pltpu.SemaphoreType.DMA((N_DEV - 1,))],
        )(v)


def _adamw_math(w, g, m, v):
    m = ADAM_B1 * m + (1.0 - ADAM_B1) * g
    v = ADAM_B2 * v + (1.0 - ADAM_B2) * (g * g)
    m_hat = m / (1.0 - ADAM_B1 ** ADAM_STEP)
    v_hat = v / (1.0 - ADAM_B2 ** ADAM_STEP)
    return -ADAM_LR * (m_hat / (jnp.sqrt(v_hat) + ADAM_EPS) + ADAM_WD * w), m, v


def _adamw(g, w, m, v, name):
    R, Cc = w.shape[-2:]
    TR = R if R <= 512 else _pick(R, (256, 128))
    slabs = g.ndim == 3
    lead = w.ndim - 2

    def body(g_ref, w_ref, m_ref, v_ref, go_ref, d_ref, mo_ref, vo_ref):
        if slabs:
            gs = g_ref[0].astype(f32)
            for k in range(1, N_DEV):
                gs = gs + g_ref[k].astype(f32)
        else:
            gs = g_ref[...]
        at = 0 if lead else Ellipsis
        d, mn, vn = _adamw_math(w_ref[at], gs, m_ref[at], v_ref[at])
        go_ref[at], d_ref[at], mo_ref[at], vo_ref[at] = gs, d, mn, vn

    blk = pl.BlockSpec((1,) * lead + (TR, Cc), lambda i: (0,) * lead + (i, 0))
    gblk = pl.BlockSpec((N_DEV, TR, Cc), lambda i: (0, i, 0)) if slabs else pl.BlockSpec((TR, Cc), lambda i: (i, 0))
    return pl.pallas_call(
        body, name=name, grid=(R // TR,), in_specs=[gblk, blk, blk, blk], out_specs=[blk] * 4,
        out_shape=[SDS(w.shape, f32)] * 4, compiler_params=_cp())(g, w, m, v)


WEIGHTS = ("meta_tokens", "mix_norm_w", "w_in", "fox_forget_bias", "dn_conv_w", "dn_a_log", "dn_dt_bias", "dn_out_norm_w",
           "w_branch_fox", "w_branch_dn", "w_out", "ffn_norm_w", "w_ffn_gate", "w_ffn_up", "w_ffn_down", "final_norm_w")
COL_SHARDED = ("w_in", "w_branch_fox", "w_branch_dn", "w_ffn_gate", "w_ffn_up")
ROW_SHARDED = ("w_out", "w_ffn_down")
BIG = COL_SHARDED + ROW_SHARDED
LATE = tuple(n for n in BIG if n != "w_in")
SMALL = tuple(n for n in WEIGHTS if n not in BIG)


def _to_slabs(name, g):
    r, c = g.shape
    if name in COL_SHARDED:
        return _b(g.reshape(r, N_DEV, c // N_DEV).transpose(1, 0, 2))
    return _b(g.reshape(N_DEV, r // N_DEV, c))


def _from_slabs(name, s):
    n, r, c = s.shape
    if name in COL_SHARDED:
        return s.transpose(1, 0, 2).reshape(r, n * c)
    return s.reshape(n * r, c)


def kernel(x, meta_tokens, mix_norm_w, w_in, fox_forget_bias, dn_conv_w, dn_a_log, dn_dt_bias, dn_out_norm_w, w_branch_fox, w_branch_dn, w_out, ffn_norm_w, w_ffn_gate, w_ffn_up, w_ffn_down, final_norm_w, loss_target, m_meta_tokens, m_mix_norm_w, m_w_in, m_fox_forget_bias, m_dn_conv_w, m_dn_a_log, m_dn_dt_bias, m_dn_out_norm_w, m_w_branch_fox, m_w_branch_dn, m_w_out, m_ffn_norm_w, m_w_ffn_gate, m_w_ffn_up, m_w_ffn_down, m_final_norm_w, v_meta_tokens, v_mix_norm_w, v_w_in, v_fox_forget_bias, v_dn_conv_w, v_dn_a_log, v_dn_dt_bias, v_dn_out_norm_w, v_w_branch_fox, v_w_branch_dn, v_w_out, v_ffn_norm_w, v_w_ffn_gate, v_w_ffn_up, v_w_ffn_down, v_final_norm_w):
    w = dict(meta_tokens=meta_tokens, mix_norm_w=mix_norm_w, w_in=w_in, fox_forget_bias=fox_forget_bias, dn_conv_w=dn_conv_w, dn_a_log=dn_a_log, dn_dt_bias=dn_dt_bias, dn_out_norm_w=dn_out_norm_w, w_branch_fox=w_branch_fox, w_branch_dn=w_branch_dn, w_out=w_out, ffn_norm_w=ffn_norm_w, w_ffn_gate=w_ffn_gate, w_ffn_up=w_ffn_up, w_ffn_down=w_ffn_down, final_norm_w=final_norm_w)
    mom = dict(meta_tokens=m_meta_tokens, mix_norm_w=m_mix_norm_w, w_in=m_w_in, fox_forget_bias=m_fox_forget_bias, dn_conv_w=m_dn_conv_w, dn_a_log=m_dn_a_log, dn_dt_bias=m_dn_dt_bias, dn_out_norm_w=m_dn_out_norm_w, w_branch_fox=m_w_branch_fox, w_branch_dn=m_w_branch_dn, w_out=m_w_out, ffn_norm_w=m_ffn_norm_w, w_ffn_gate=m_w_ffn_gate, w_ffn_up=m_w_ffn_up, w_ffn_down=m_w_ffn_down, final_norm_w=m_final_norm_w)
    var = dict(meta_tokens=v_meta_tokens, mix_norm_w=v_mix_norm_w, w_in=v_w_in, fox_forget_bias=v_fox_forget_bias, dn_conv_w=v_dn_conv_w, dn_a_log=v_dn_a_log, dn_dt_bias=v_dn_dt_bias, dn_out_norm_w=v_dn_out_norm_w, w_branch_fox=v_w_branch_fox, w_branch_dn=v_w_branch_dn, w_out=v_w_out, ffn_norm_w=v_ffn_norm_w, w_ffn_gate=v_w_ffn_gate, w_ffn_up=v_w_ffn_up, w_ffn_down=v_w_ffn_down, final_norm_w=v_final_norm_w)
    two_d = lambda a: a.reshape(a.shape[-2:]) if a.ndim >= 2 else a.reshape(1, -1)
    me = 4 * lax.axis_index("x") + 2 * lax.axis_index("y") + lax.axis_index("c")

    g_in, g_meta, g_cw = _exchange([_b(two_d(w["w_in"])), two_d(w["meta_tokens"]), two_d(w["dn_conv_w"])], "all_gather_early", gather=True)
    meta = g_meta.transpose(1, 0, 2).reshape(N_META, -1)
    cw = g_cw.transpose(1, 0, 2).reshape(CONV_K, -1)

    gx, g_small, g_w_in, recv = _local_step(
        x[0], loss_target[0], meta, two_d(w["mix_norm_w"]), _from_slabs("w_in", g_in), two_d(w["fox_forget_bias"]), cw, two_d(w["dn_a_log"]),
        two_d(w["dn_dt_bias"]), two_d(w["dn_out_norm_w"]), two_d(w["ffn_norm_w"]), two_d(w["final_norm_w"]),
        {n: _b(two_d(w[n])) for n in LATE})

    recv["w_in"] = _exchange([_to_slabs("w_in", g_w_in)], "exchange_w_in_grad", gather=False)[0]
    order = ("loss",) + SMALL
    flat = jnp.concatenate([g_small[n].reshape(-1) for n in order])
    rows = -(-flat.shape[0] // (8 * HP)) * 8
    tot = _all_reduce_small(jnp.pad(flat, (0, rows * HP - flat.shape[0])).reshape(rows, HP)).reshape(-1)
    summed, o = {}, 0
    for n in order:
        k = g_small[n].shape[0]
        summed[n] = tot[o:o + k]
        o += k
    loss = summed["loss"][0]
    d_model = x.shape[-1]
    mcols, ccols = d_model // N_DEV, dn_conv_w.shape[-1]
    summed["meta_tokens"] = lax.dynamic_slice(summed["meta_tokens"].reshape(N_META, d_model), (0, me * mcols), (N_META, mcols)).reshape(-1)
    summed["dn_conv_w"] = lax.dynamic_slice(summed["dn_conv_w"].reshape(CONV_K, ccols * N_DEV), (0, me * ccols), (CONV_K, ccols)).reshape(-1)

    res = {}
    for n in BIG:
        res[n] = _adamw(recv[n], w[n], mom[n], var[n], "adamw_" + n)
    sizes = [summed[n].shape[0] for n in SMALL]
    srows = -(-sum(sizes) // (8 * HP)) * 8
    pack = lambda d: jnp.pad(jnp.concatenate([d[n].reshape(-1) for n in SMALL]), (0, srows * HP - sum(sizes))).reshape(srows, HP)
    sres = _adamw(pack(summed), pack(w), pack(mom), pack(var), "adamw_small")
    o = 0
    for n, k in zip(SMALL, sizes):
        res[n] = [r.reshape(-1)[o:o + k].reshape(w[n].shape) for r in sres]
        o += k
    return (loss, gx[None], *[res[n][0] for n in WEIGHTS], *[res[n][1] for n in WEIGHTS], *[res[n][2] for n in WEIGHTS], *[res[n][3] for n in WEIGHTS])
```

```python
import functools

import jax
import jax.numpy as jnp
from jax import lax
from jax.experimental import pallas as pl
from jax.experimental.pallas import tpu as pltpu

f32, bf16 = jnp.float32, jnp.bfloat16
HI = lax.Precision.HIGHEST
MESH = pl.DeviceIdType.MESH
SDS = jax.ShapeDtypeStruct

N_DEV = 8
N_META = 16
PREFIX = 128
N_PAD = PREFIX - N_META
FOX_H, FOX_D = 8, 64
DN_H, DN_D = 4, 128
DN_C = 64
CONV_K = 4
HP = 128
SMALL_W = 256
EPS = 1e-6
NEG = -1e30
C_Q0, C_K0 = 64, 67
LSE_COL = 64

ADAM_LR, ADAM_B1, ADAM_B2, ADAM_EPS, ADAM_WD, ADAM_STEP = 0.001, 0.9, 0.999, 1e-08, 0.01, 10

VMEM_LIMIT_V7X = 56 * 1024 * 1024
ROW_TILES = (384, 128)
ATTN_TILES = (384, 128)
FFN_TILES = (192, 64)
COL_TILES = (512, 256, 128)
FOX_HEAD_GROUP = 2
ROW_BLOCKS = (704, 512, 256, 128)
DN_INTRA_GROUP = (6, 3, 2, 1)
DN_SCAN_GROUP = (6, 3, 2, 1)


def _pick(n, cands):
    for c in cands:
        if n % c == 0:
            return c
    raise ValueError(f"no tile of {cands} divides {n}")


def _cp(n_axes=1):
    return pltpu.CompilerParams(dimension_semantics=("arbitrary",) * n_axes, vmem_limit_bytes=VMEM_LIMIT_V7X)


def _b(x):
    return x.astype(bf16)


def _dot(a, b):
    return jnp.dot(a, b, preferred_element_type=f32)


def _dot_nt(a, b):
    return lax.dot_general(a, b, (((1,), (1,)), ((), ())), preferred_element_type=f32)


def _dot_tn(a, b):
    return lax.dot_general(a, b, (((0,), (0,)), ((), ())), preferred_element_type=f32)


def _dot_hi(a, b):
    return jnp.dot(a, b, preferred_element_type=f32, precision=HI)


def _iota(shape, dim):
    return lax.broadcasted_iota(jnp.int32, shape, dim)


def _rms(x, w):
    return x * lax.rsqrt(jnp.mean(x * x, axis=-1, keepdims=True) + EPS) * w


def _sigmoid(x):
    return jax.nn.sigmoid(x)


def _load_once(pairs, sems):
    @pl.when(pl.program_id(0) == 0)
    def _():
        cps = [pltpu.make_async_copy(src, dst, sems.at[k]) for k, (src, dst) in enumerate(pairs)]
        for cp in cps:
            cp.start()
        for cp in cps:
            cp.wait()


def _seg_layout(d_model):
    return (("fq", FOX_H * HP, bf16, "fox"), ("fk", FOX_H * HP, bf16, "fox"), ("sf", SMALL_W, f32, "fox"),
            ("fv", FOX_H * HP, bf16, "fv"),
            ("dn", 3 * DN_H * DN_D, f32, "dn"), ("sd", SMALL_W, f32, "dn"),
            ("dz", DN_H * DN_D, f32, "mix"), ("ga", d_model, f32, "mix"), ("gb", d_model, f32, "mix"))


GROUPS = ("fox", "fv", "dn", "mix")


def _group_widths(d_model):
    return [sum(wd for _, wd, _, g in _seg_layout(d_model) if g == grp) for grp in GROUPS]


def _in_proj(h0, w1, wp):
    L, D = h0.shape
    NP = wp.shape[1]
    TM = _pick(L, ROW_TILES)
    segs = _seg_layout(D)
    offs, o = [], 0
    for _, wd, _, _ in segs:
        offs.append(o)
        o += wd
    assert o == NP

    def body(h_ref, w1_ref, wp_hbm, xnt_ref, *rest):
        outs, (wp_v, sems) = rest[:len(segs)], rest[len(segs):]
        _load_once([(wp_hbm, wp_v)], sems)
        xf = _rms(h_ref[...], w1_ref[...])
        xn = _b(xf)
        xnt_ref[...] = _b(xf.T)
        for o_ref, off, (_, wd, _, _) in zip(outs, offs, segs):
            o_ref[...] = _dot(xn, wp_v[:, off:off + wd]).astype(o_ref.dtype)

    row = lambda wd: pl.BlockSpec((TM, wd), lambda i: (i, 0))
    return pl.pallas_call(
        body, name="in_proj", grid=(L // TM,),
        in_specs=[row(D), pl.BlockSpec((1, D), lambda i: (0, 0)), pl.BlockSpec(memory_space=pl.ANY)],
        out_specs=[pl.BlockSpec((D, TM), lambda i: (0, i))] + [row(wd) for _, wd, _, _ in segs],
        out_shape=[SDS((D, L), bf16)] + [SDS((L, wd), dt) for _, wd, dt, _ in segs],
        scratch_shapes=[pltpu.VMEM((D, NP), bf16), pltpu.SemaphoreType.DMA((1,))],
        compiler_params=_cp())(h0, w1, wp)


def _in_proj_bwd(dgroups, wp, h0, w1, dh1, slabs):
    L, D = h0.shape
    NP = wp.shape[1]
    TM = _pick(L, ROW_TILES)
    NT = L // TM
    widths = [g.shape[1] for g in dgroups]
    assert sum(widths) == NP
    ng, n = len(dgroups), len(slabs)

    def body(*refs):
        dg_refs, (wp_hbm, h_ref, w1_ref, dh1_ref) = refs[:ng], refs[ng:ng + 4]
        ins, (dh0_ref, acc_ref), outs = refs[ng + 4:ng + 4 + n], refs[ng + 4 + n:ng + 6 + n], refs[ng + 6 + n:ng + 6 + 2 * n]
        wp_v, sems = refs[ng + 6 + 2 * n:ng + 8 + 2 * n]
        xsems = refs[ng + 8 + 2 * n:]
        _load_once([(wp_hbm, wp_v)], sems)

        @pl.when(pl.program_id(0) == 0)
        def _():
            acc_ref[...] = jnp.zeros_like(acc_ref)
            _exchange_start(ins, outs, *xsems, gather=False)

        dxn, off = None, 0
        for g_ref, wd in zip(dg_refs, widths):
            part = _dot_nt(g_ref[...], wp_v[:, off:off + wd])
            dxn = part if dxn is None else dxn + part
            off += wd
        _, vjp = jax.vjp(_rms, h_ref[...], w1_ref[...])
        dh0n, dw1 = vjp(dxn)
        dh0_ref[...] = dh1_ref[...] + dh0n
        acc_ref[0:1, :] += dw1

        @pl.when(pl.program_id(0) == NT - 1)
        def _():
            _exchange_wait(ins, outs, *xsems, gather=False)

    row = lambda wd: pl.BlockSpec((TM, wd), lambda i: (i, 0))
    anyspec = pl.BlockSpec(memory_space=pl.ANY)
    res = pl.pallas_call(
        body, name="in_proj_bwd", grid=(NT,),
        in_specs=[row(wd) for wd in widths] + [anyspec, row(D), pl.BlockSpec((1, D), lambda i: (0, 0)), row(D)] + [anyspec] * n,
        out_specs=[row(D), pl.BlockSpec((8, D), lambda i: (0, 0))] + [anyspec] * n,
        out_shape=[SDS((L, D), f32), SDS((8, D), f32)] + [SDS(a.shape, a.dtype) for a in slabs],
        scratch_shapes=[pltpu.VMEM((D, NP), bf16), pltpu.SemaphoreType.DMA((1,))] + _exchange_sems(n),
        compiler_params=_cp())(*dgroups, wp, h0, w1, dh1, *slabs)
    return res[0], res[1], res[2:]


def _matmul_tn(at, b, name):
    M, L = at.shape
    N = b.shape[1]

    def body(a_ref, b_ref, o_ref):
        o_ref[...] = _b(_dot(a_ref[...], b_ref[...]))

    if M <= N:
        bn = N if N <= COL_TILES[0] else _pick(N, COL_TILES)
        grid, a_spec = (N // bn,), pl.BlockSpec((M, L), lambda n: (0, 0))
        b_spec, o_spec = pl.BlockSpec((L, bn), lambda n: (0, n)), pl.BlockSpec((M, bn), lambda n: (0, n))
    else:
        bm = _pick(M, ROW_BLOCKS)
        grid, a_spec = (M // bm,), pl.BlockSpec((bm, L), lambda m: (m, 0))
        b_spec, o_spec = pl.BlockSpec((L, N), lambda m: (0, 0)), pl.BlockSpec((bm, N), lambda m: (m, 0))
    return pl.pallas_call(body, name=name, grid=grid, in_specs=[a_spec, b_spec], out_specs=o_spec,
                          out_shape=SDS((M, N), bf16), compiler_params=_cp())(at, b)


def _fox_prep(fq, fk, sf, bias_p):
    L = fq.shape[0]
    T = HP
    NT = L // T

    def body(fq_ref, fk_ref, sf_ref, b_ref, qa_ref, ka_ref, carry):
        @pl.when(pl.program_id(0) == 0)
        def _():
            carry[...] = jnp.zeros_like(carry)

        lane, row = _iota((T, HP), 1), _iota((T, HP), 0)
        logf = jnp.where(lane < FOX_H, jax.nn.log_sigmoid(sf_ref[...] + b_ref[...]), 0.0)
        c = _dot_hi((row >= lane).astype(f32), logf) + carry[...]
        carry[...] = jnp.sum(jnp.where(row == T - 1, c, 0.0), axis=0, keepdims=True)
        ones_q = jnp.where((lane >= C_K0) & (lane < C_K0 + 3), 1.0, 0.0)
        ones_k = jnp.where((lane >= C_Q0) & (lane < C_Q0 + 3), 1.0, 0.0)
        for h in range(FOX_H):
            ch = jnp.broadcast_to(jnp.sum(jnp.where(lane == h, c, 0.0), axis=1, keepdims=True), (T, HP))
            c1 = _b(ch).astype(f32)
            c2 = _b(ch - c1).astype(f32)
            c3 = _b(ch - c1 - c2).astype(f32)
            cq = jnp.where(lane == C_Q0, c1, 0.0) + jnp.where(lane == C_Q0 + 1, c2, 0.0) + jnp.where(lane == C_Q0 + 2, c3, 0.0)
            ck = jnp.where(lane == C_K0, c1, 0.0) + jnp.where(lane == C_K0 + 1, c2, 0.0) + jnp.where(lane == C_K0 + 2, c3, 0.0)
            q = fq_ref[:, h * HP:(h + 1) * HP].astype(f32) * (FOX_D ** -0.5)
            k = fk_ref[:, h * HP:(h + 1) * HP].astype(f32)
            qa_ref[h] = _b(q + cq + ones_q)
            ka_ref[h] = _b(k + ones_k - ck)

    return pl.pallas_call(
        body, name="fox_prep", grid=(NT,),
        in_specs=[pl.BlockSpec((T, FOX_H * HP), lambda i: (i, 0)), pl.BlockSpec((T, FOX_H * HP), lambda i: (i, 0)),
                  pl.BlockSpec((T, HP), lambda i: (i, 0)), pl.BlockSpec((1, HP), lambda i: (0, 0))],
        out_specs=[pl.BlockSpec((FOX_H, T, HP), lambda i: (0, i, 0))] * 2,
        out_shape=[SDS((FOX_H, L, HP), bf16)] * 2,
        scratch_shapes=[pltpu.VMEM((1, HP), f32)], compiler_params=_cp())(fq, fk, sf, bias_p)


def _fox_prep_bwd(dqa, dka, sf, bias_p):
    L = sf.shape[0]
    T = HP
    NT = L // T
    rev = lambda i: (NT - 1 - i, 0)

    W = FOX_H * HP

    def body(dq_ref, dk_ref, sf_ref, b_ref, dg_ref, db_ref, carry):
        @pl.when(pl.program_id(0) == 0)
        def _():
            carry[...] = jnp.zeros_like(carry)
            db_ref[...] = jnp.zeros_like(db_ref)

        dq, dk = dq_ref[...], dk_ref[...]
        dg_ref[:, 0:W] = _b(dq * (FOX_D ** -0.5))
        dg_ref[:, W:2 * W] = _b(dk)
        lane, row = _iota((T, HP), 1), _iota((T, HP), 0)
        dc = jnp.zeros((T, HP), f32)
        for h in range(FOX_H):
            col = jnp.sum(jnp.where(lane == C_Q0, dq[:, h * HP:(h + 1) * HP], 0.0)
                          - jnp.where(lane == C_K0, dk[:, h * HP:(h + 1) * HP], 0.0), axis=1, keepdims=True)
            dc = dc + jnp.where(lane == h, col, 0.0)
        dl = _dot_hi((row <= lane).astype(f32), dc) + carry[...]
        carry[...] = jnp.sum(jnp.where(row == 0, dl, 0.0), axis=0, keepdims=True)
        dx = jnp.where(lane < FOX_H, dl * _sigmoid(-(sf_ref[...] + b_ref[...])), 0.0)
        dg_ref[:, 2 * W:2 * W + HP] = _b(dx)
        dg_ref[:, 2 * W + HP:] = jnp.zeros((T, SMALL_W - HP), bf16)
        db_ref[0:1, :] += jnp.sum(dx, axis=0, keepdims=True)

    return pl.pallas_call(
        body, name="fox_prep_bwd", grid=(NT,),
        in_specs=[pl.BlockSpec((T, W), rev), pl.BlockSpec((T, W), rev), pl.BlockSpec((T, HP), rev), pl.BlockSpec((1, HP), lambda i: (0, 0))],
        out_specs=[pl.BlockSpec((T, 2 * W + SMALL_W), rev), pl.BlockSpec((8, HP), lambda i: (0, 0))],
        out_shape=[SDS((L, 2 * W + SMALL_W), bf16), SDS((8, HP), f32)],
        scratch_shapes=[pltpu.VMEM((1, HP), f32)], compiler_params=_cp())(dqa, dka, sf, bias_p)


def _tile_start(j, T):
    return j * T if isinstance(j, int) else pl.multiple_of(j * T, T)


def _fox_fwd(qa, ka, fv, shards):
    L = qa.shape[1]
    TQ = TK = _pick(L, ATTN_TILES)
    NQ = L // TQ
    n = len(shards)
    HG = FOX_HEAD_GROUP

    def body(q_ref, k_ref, v_ref, *rest):
        ins, o_ref, outs, sems = rest[:n], rest[n], rest[n + 1:2 * n + 1], rest[2 * n + 1:]
        h, i = pl.program_id(0), pl.program_id(1)

        @pl.when((h == 0) & (i == 0))
        def _():
            _exchange_start(ins, outs, *sems, gather=True)

        qs = [q_ref[a] for a in range(HG)]
        rowg = i * TQ + _iota((TQ, TK), 0)
        colb = _iota((TQ, TK), 1)

        def step(j, carry, masked):
            ms, ls, accs = carry
            k0 = _tile_start(j, TK)
            ss = [_dot_nt(qs[a], k_ref[a, pl.ds(k0, TK), :]) for a in range(HG)]
            if masked:
                colg = colb + j * TK
                keep = (colg <= rowg) & (colg >= N_PAD)
                ss = [jnp.where(keep, s, NEG) for s in ss]
            m_new = [jnp.maximum(m, jnp.max(s, axis=1, keepdims=True)) for m, s in zip(ms, ss)]
            ps = [jnp.exp(s - m) for s, m in zip(ss, m_new)]
            alphas = [jnp.exp(m - mn) for m, mn in zip(ms, m_new)]
            ls = [al * l + jnp.sum(p, axis=1, keepdims=True) for al, l, p in zip(alphas, ls, ps)]
            accs = [al * acc + _dot(_b(p), v_ref[pl.ds(k0, TK), a * HP:(a + 1) * HP]) for a, (al, acc, p) in enumerate(zip(alphas, accs, ps))]
            return m_new, ls, accs

        init = ([jnp.full((TQ, 1), NEG, f32)] * HG, [jnp.zeros((TQ, 1), f32)] * HG, [jnp.zeros((TQ, HP), f32)] * HG)
        carry = step(0, init, True)
        carry = lax.fori_loop(1, i, functools.partial(step, masked=False), carry)
        ms, ls, accs = lax.fori_loop(jnp.maximum(i, 1), i + 1, functools.partial(step, masked=True), carry)
        lane = _iota((TQ, HP), 1)
        for a in range(HG):
            o_ref[:, a * HP:(a + 1) * HP] = jnp.where(lane == LSE_COL, ms[a] + jnp.log(ls[a]), accs[a] / ls[a])

        @pl.when((h == FOX_H // HG - 1) & (i == NQ - 1))
        def _():
            _exchange_wait(ins, outs, *sems, gather=True)

    anyspec = pl.BlockSpec(memory_space=pl.ANY)
    res = pl.pallas_call(
        body, name="fox_fwd", grid=(FOX_H // HG, NQ),
        in_specs=[pl.BlockSpec((HG, TQ, HP), lambda h, i: (h, i, 0)), pl.BlockSpec((HG, L, HP), lambda h, i: (h, 0, 0)),
                  pl.BlockSpec((L, HG * HP), lambda h, i: (0, h))] + [anyspec] * n,
        out_specs=[pl.BlockSpec((TQ, HG * HP), lambda h, i: (i, h))] + [anyspec] * n,
        out_shape=[SDS((L, FOX_H * HP), f32)] + [SDS((N_DEV,) + a.shape, a.dtype) for a in shards],
        scratch_shapes=_exchange_sems(n), compiler_params=_cp(2))(qa, ka, fv, *shards)
    return res[0], res[1:]


def _fox_bwd(qa, ka, fv, op, dop, slabs):
    L = qa.shape[1]
    TQ = TK = _pick(L, ATTN_TILES)
    NQ = L // TQ
    n = len(slabs)
    HG = FOX_HEAD_GROUP

    def body(q_ref, k_ref, v_ref, o_ref, do_ref, *rest):
        ins, (dq_ref, dk_ref, dv_ref), outs = rest[:n], rest[n:n + 3], rest[n + 3:2 * n + 3]
        lse_s, delta_s = rest[2 * n + 3:2 * n + 5]
        sems = rest[2 * n + 5:]
        h, j = pl.program_id(0), pl.program_id(1)
        cols = [slice(a * HP, (a + 1) * HP) for a in range(HG)]

        @pl.when((h == 0) & (j == 0))
        def _():
            _exchange_start(ins, outs, *sems, gather=False)

        lane = _iota((TQ, HP), 1)

        @pl.when(j == 0)
        def _():
            dq_ref[...] = jnp.zeros_like(dq_ref)
            for t in range(NQ):
                r = slice(t * TQ, (t + 1) * TQ)
                for a in range(HG):
                    o, do = o_ref[r, cols[a]], do_ref[r, cols[a]]
                    lse_s[a, r, :] = jnp.sum(jnp.where(lane == LSE_COL, o, 0.0), axis=1, keepdims=True)
                    delta_s[a, r, :] = jnp.sum(jnp.where(lane < FOX_D, o * do, 0.0), axis=1, keepdims=True)

        kts = [k_ref[a] for a in range(HG)]
        vts = [v_ref[:, cols[a]] for a in range(HG)]
        colg = j * TK + _iota((TQ, TK), 1)
        rowb = _iota((TQ, TK), 0)

        def step(i, carry, masked):
            dks, dvs = carry
            r0 = _tile_start(i, TQ)
            rows = pl.ds(r0, TQ)
            qs = [q_ref[a, rows, :] for a in range(HG)]
            ps = [jnp.exp(_dot_nt(q, kt) - lse_s[a, rows, :]) for a, (q, kt) in enumerate(zip(qs, kts))]
            if masked:
                keep = (colg <= rowb + i * TQ) & (colg >= N_PAD)
                ps = [jnp.where(keep, p, 0.0) for p in ps]
            dobs = [_b(do_ref[rows, cols[a]]) for a in range(HG)]
            dvs = [dv + _dot_tn(_b(p), dob) for dv, p, dob in zip(dvs, ps, dobs)]
            dss = [_b(p * (_dot_nt(dob, vt) - delta_s[a, rows, :])) for a, (p, dob, vt) in enumerate(zip(ps, dobs, vts))]
            for a in range(HG):
                dq_ref[rows, cols[a]] += _dot(dss[a], kts[a])
            dks = [dk + _dot_tn(ds, q) for dk, ds, q in zip(dks, dss, qs)]
            return dks, dvs

        zeros = [jnp.zeros((TK, HP), f32)] * HG
        carry = step(j, (zeros, zeros), True)
        split = jnp.where(j == 0, NQ, j + 1)
        carry = lax.fori_loop(j + 1, split, functools.partial(step, masked=True), carry)
        dks, dvs = lax.fori_loop(split, NQ, functools.partial(step, masked=False), carry)
        for a in range(HG):
            dk_ref[:, cols[a]] = dks[a]
            dv_ref[:, cols[a]] = _b(dvs[a])

        @pl.when((h == FOX_H // HG - 1) & (j == NQ - 1))
        def _():
            _exchange_wait(ins, outs, *sems, gather=False)

    head = pl.BlockSpec((L, HG * HP), lambda h, j: (0, h))
    tile = pl.BlockSpec((TK, HG * HP), lambda h, j: (j, h))
    anyspec = pl.BlockSpec(memory_space=pl.ANY)
    res = pl.pallas_call(
        body, name="fox_bwd", grid=(FOX_H // HG, L // TK),
        in_specs=[pl.BlockSpec((HG, L, HP), lambda h, j: (h, 0, 0)), pl.BlockSpec((HG, TK, HP), lambda h, j: (h, j, 0)), tile, head, head]
        + [anyspec] * n,
        out_specs=[head, tile, tile] + [anyspec] * n,
        out_shape=[SDS((L, FOX_H * HP), f32), SDS((L, FOX_H * HP), f32), SDS((L, FOX_H * HP), bf16)] + [SDS(a.shape, a.dtype) for a in slabs],
        scratch_shapes=[pltpu.VMEM((HG, L, 1), f32), pltpu.VMEM((HG, L, 1), f32)] + _exchange_sems(n),
        compiler_params=_cp(2))(qa, ka, fv, op, dop, *slabs)
    return res[:3], res[3:]


def _dn_post(y, sd, alog_p, dt_p, valid):
    a = y * _sigmoid(y)
    W = DN_H * DN_D
    heads = []
    for part, scale in ((0, DN_D ** -0.5), (1, 1.0)):
        for h in range(DN_H):
            xh = a[:, part * W + h * DN_D:part * W + (h + 1) * DN_D]
            heads.append(xh * lax.rsqrt(jnp.sum(xh * xh, axis=-1, keepdims=True) + EPS) * scale)
    q = jnp.concatenate(heads[:DN_H], axis=1)
    k = jnp.concatenate(heads[DN_H:], axis=1)
    v = a[:, 2 * W:3 * W]
    lane = _iota(sd.shape, 1)
    beta = _sigmoid(sd) * valid
    g = -jnp.exp(alog_p) * jax.nn.softplus(sd + dt_p) * valid
    bg = jnp.where(lane < DN_H, beta, jnp.where(lane < 2 * DN_H, g, 0.0))
    return q, k, v, bg


def _conv_fwd(ext_ref, cw_ref, TM):
    y = cw_ref[0:1, :] * ext_ref[8 - (CONV_K - 1):8 - (CONV_K - 1) + TM, :]
    for i in range(1, CONV_K):
        o = 8 - (CONV_K - 1) + i
        y = y + cw_ref[i:i + 1, :] * ext_ref[o:o + TM, :]
    return y


def _dn_prep(dn, sd, cw, alog_p, dt_p):
    L, W3 = dn.shape
    TM = _pick(L, ROW_TILES)
    W = DN_H * DN_D

    def body(dn_ref, halo_ref, sd_ref, cw_ref, al_ref, dt_ref, q_ref, k_ref, v_ref, bg_ref, ext):
        i = pl.program_id(0)
        ext[0:8, :] = jnp.where(i == 0, 0.0, halo_ref[...])
        ext[8:, :] = dn_ref[...]
        y = _conv_fwd(ext, cw_ref, TM)
        valid = ((i * TM + _iota((TM, 1), 0)) >= N_PAD).astype(f32)
        q, k, v, bg = _dn_post(y, sd_ref[...], al_ref[...], dt_ref[...], valid)
        q_ref[...], k_ref[...], v_ref[...], bg_ref[...] = q, k, v, bg

    row = lambda wd: pl.BlockSpec((TM, wd), lambda i: (i, 0))
    vec = pl.BlockSpec((1, HP), lambda i: (0, 0))
    return pl.pallas_call(
        body, name="dn_prep", grid=(L // TM,),
        in_specs=[row(W3), pl.BlockSpec((8, W3), lambda i: (jnp.maximum(i * (TM // 8) - 1, 0), 0)), row(HP),
                  pl.BlockSpec((CONV_K, W3), lambda i: (0, 0)), vec, vec],
        out_specs=[row(W), row(W), row(W), row(HP)],
        out_shape=[SDS((L, W), f32)] * 3 + [SDS((L, HP), f32)],
        scratch_shapes=[pltpu.VMEM((TM + 8, W3), f32)], compiler_params=_cp())(dn, dn, sd, cw, alog_p, dt_p)


def _dn_prep_bwd(dn, sd, cw, alog_p, dt_p, dq, dk, dv, dbg):
    L, W3 = dn.shape
    TM = _pick(L, ROW_TILES)
    NT = L // TM
    W = DN_H * DN_D

    def body(dn_ref, halo_ref, sd_ref, cw_ref, al_ref, dt_ref, dq_ref, dk_ref, dv_ref, dbg_ref,
             dg_ref, dcw_ref, dp_ref, ext, dyp, carry):
        i = pl.program_id(0)
        t = NT - 1 - i

        @pl.when(i == 0)
        def _():
            carry[...] = jnp.zeros_like(carry)
            dcw_ref[...] = jnp.zeros_like(dcw_ref)
            dp_ref[...] = jnp.zeros_like(dp_ref)
            dyp[...] = jnp.zeros_like(dyp)

        ext[0:8, :] = jnp.where(t == 0, 0.0, halo_ref[...])
        ext[8:, :] = dn_ref[...]
        y = _conv_fwd(ext, cw_ref, TM)
        valid = ((t * TM + _iota((TM, 1), 0)) >= N_PAD).astype(f32)
        _, vjp = jax.vjp(functools.partial(_dn_post, valid=valid), y, sd_ref[...], al_ref[...], dt_ref[...])
        dy, dsd, dal, ddt = vjp((dq_ref[...], dk_ref[...], dv_ref[...], dbg_ref[...]))
        dg_ref[:, W3:W3 + HP] = _b(dsd)
        dg_ref[:, W3 + HP:] = jnp.zeros((TM, SMALL_W - HP), bf16)
        dp_ref[0:1, :] += dal
        dp_ref[1:2, :] += ddt
        dyp[8:8 + TM, :] = dy
        o0 = CONV_K - 1
        dext = cw_ref[0:1, :] * dyp[o0:o0 + TM + 8, :]
        for k in range(1, CONV_K):
            dext = dext + cw_ref[k:k + 1, :] * dyp[o0 - k:o0 - k + TM + 8, :]
        for k in range(CONV_K):
            o = 8 - (CONV_K - 1) + k
            dcw_ref[k:k + 1, :] += jnp.sum(dy * ext[o:o + TM, :], axis=0, keepdims=True)
        dg_ref[:, 0:W3] = _b(jnp.concatenate([dext[8:TM, :], dext[TM:TM + 8, :] + carry[...]], axis=0))
        carry[...] = dext[0:8, :]

    row = lambda wd: pl.BlockSpec((TM, wd), lambda i: (NT - 1 - i, 0))
    vec = pl.BlockSpec((1, HP), lambda i: (0, 0))
    return pl.pallas_call(
        body, name="dn_prep_bwd", grid=(NT,),
        in_specs=[row(W3), pl.BlockSpec((8, W3), lambda i: (jnp.maximum((NT - 1 - i) * (TM // 8) - 1, 0), 0)), row(HP),
                  pl.BlockSpec((CONV_K, W3), lambda i: (0, 0)), vec, vec, row(W), row(W), row(W), row(HP)],
        out_specs=[row(W3 + SMALL_W), pl.BlockSpec((8, W3), lambda i: (0, 0)), pl.BlockSpec((8, HP), lambda i: (0, 0))],
        out_shape=[SDS((L, W3 + SMALL_W), bf16), SDS((8, W3), f32), SDS((8, HP), f32)],
        scratch_shapes=[pltpu.VMEM((TM + 8, W3), f32), pltpu.VMEM((TM + 16, W3), f32), pltpu.VMEM((8, W3), f32)],
        compiler_params=_cp())(dn, dn, sd, cw, alog_p, dt_p, dq, dk, dv, dbg)


def _split2(x):
    hi = _b(x)
    return hi, _b(x - hi.astype(f32))


def _split3(x):
    hi = _b(x)
    r = x - hi.astype(f32)
    mid = _b(r)
    return hi, mid, _b(r - mid.astype(f32))


def _x3(a, b, dot):
    (a1, a2), (b1, b2) = _split2(a), _split2(b)
    return dot(a1, b1) + (dot(a1, b2) + dot(a2, b1))


@jax.custom_vjp
def _dot_x3(a, b):
    return _x3(a, b, _dot)


_dot_x3.defvjp(lambda a, b: (_x3(a, b, _dot), (a, b)), lambda res, g: (_x3(g, res[1], _dot_nt), _x3(res[0], g, _dot_tn)))


def _exact3(m, x, dot):
    x1, x2, x3 = _split3(x)
    return dot(m, x1) + (dot(m, x2) + dot(m, x3))


def _tri_ones(C, lower):
    row, col = _iota((C, C), 0), _iota((C, C), 1)
    return _b(((row >= col) if lower else (row <= col)).astype(f32))


@jax.custom_vjp
def _chunk_cumsum(x):
    return _exact3(_tri_ones(x.shape[0], True), x, _dot)


_chunk_cumsum.defvjp(lambda x: (_exact3(_tri_ones(x.shape[0], True), x, _dot), None),
                     lambda _, g: (_exact3(_tri_ones(g.shape[0], False), g, _dot),))


def _mxu_transpose(x):
    C = x.shape[0]
    eye = _b((_iota((C, C), 0) == _iota((C, C), 1)).astype(f32))
    return _exact3(eye, x, lambda m, part: _dot_tn(part, m))


@jax.custom_vjp
def _transpose_exact(x):
    return _mxu_transpose(x)


_transpose_exact.defvjp(lambda x: (_mxu_transpose(x), None), lambda _, g: (_mxu_transpose(g),))


def _unit_lower_inverses(lows):
    C = lows[0].shape[0]
    P = jnp.stack(lows)
    X = (_iota((C, C), 0) == _iota((C, C), 1)).astype(f32)[None] - P
    bdot = functools.partial(_x3, dot=lambda a, b: jnp.einsum("bij,bjk->bik", a, b, preferred_element_type=f32))
    for _ in range(5):
        P = bdot(P, P)
        X = X + bdot(X, P)
    return [X[i] for i in range(len(lows))]


@jax.custom_vjp
def _inverse_given(low, X):
    return X


def _inverse_given_bwd(X, g):
    return -_x3(_x3(X, g, _dot_tn), X, _dot_nt), jnp.zeros_like(X)


_inverse_given.defvjp(lambda low, X: (X, X), _inverse_given_bwd)


def _dn_intra_pre(q, k, v, bg):
    C = DN_C
    row, col = _iota((C, C), 0), _iota((C, C), 1)
    tri = row >= col
    G = _chunk_cumsum(bg)
    GT = _transpose_exact(G)
    lane = _iota((C, HP), 1)
    rowt = _iota((HP, C), 0)
    last = _iota((C, 1), 0) == C - 1
    heads = []
    for h in range(DN_H):
        beta = jnp.sum(jnp.where(lane == h, bg, 0.0), axis=1, keepdims=True)
        gcol = jnp.sum(jnp.where(lane == DN_H + h, G, 0.0), axis=1, keepdims=True)
        grow = jnp.sum(jnp.where(rowt == DN_H + h, GT, 0.0), axis=0, keepdims=True)
        glast = jnp.sum(jnp.where(last, gcol, 0.0), axis=0, keepdims=True)
        decay = jnp.exp(jnp.where(tri, gcol - grow, NEG))
        qh, kh, vh = (t[:, h * DN_D:(h + 1) * DN_D] for t in (q, k, v))
        kb = kh * beta
        low = jnp.where(row > col, _dot_nt(_b(kb), _b(kh)) * decay, 0.0)
        heads.append((beta, gcol, glast, decay, qh, kh, vh, kb, low))
    return heads


def _dn_intra_post(heads, xs):
    lane1 = _iota((1, HP), 1)
    us, ws, qds, kds, attns = [], [], [], [], []
    glrow = jnp.zeros((1, HP), f32)
    for h, ((beta, gcol, glast, decay, qh, kh, vh, kb, _), X) in enumerate(zip(heads, xs)):
        eg = jnp.exp(gcol)
        us.append(_dot_x3(X, vh * beta))
        ws.append(_dot_x3(X, kb * eg))
        attns.append(_dot_nt(_b(qh), _b(kh)) * decay)
        qds.append(qh * eg)
        kds.append(kh * jnp.exp(glast - gcol))
        glrow = glrow + jnp.where(lane1 == h, glast, 0.0)
    cat = lambda xs_: jnp.concatenate(xs_, axis=1)
    return cat(us), cat(ws), cat(qds), cat(kds), cat(attns), glrow, cat(list(xs))


def _dn_intra_group(q, k, v, bg, xs):
    G = q.shape[0] // DN_C
    rows = [slice(j * DN_C, (j + 1) * DN_C) for j in range(G)]
    pre = [_dn_intra_pre(q[r, :], k[r, :], v[r, :], bg[r, :]) for r in rows]
    inv = [[_inverse_given(hd[-1], x) for hd, x in zip(heads, xj)] for heads, xj in zip(pre, xs)]
    post = [_dn_intra_post(heads, xj) for heads, xj in zip(pre, inv)]
    return tuple(jnp.concatenate([p[i] for p in post], axis=0) for i in range(5)) + (tuple(p[5] for p in post),)


def _lane_pick(rowvec, h):
    return jnp.sum(jnp.where(_iota(rowvec.shape, 1) == h, rowvec, 0.0), axis=1, keepdims=True)


def _dn_intra(q, k, v, bg):
    L, W = q.shape
    NC = L // DN_C
    G = _pick(NC, DN_INTRA_GROUP)
    R = G * DN_C
    WA = DN_H * DN_C

    def body(q_ref, k_ref, v_ref, bg_ref, u_ref, w_ref, qd_ref, kd_ref, at_ref, gl_ref, x_ref):
        rows = [slice(j * DN_C, (j + 1) * DN_C) for j in range(G)]
        pre = [_dn_intra_pre(q_ref[r, :], k_ref[r, :], v_ref[r, :], bg_ref[r, :]) for r in rows]
        inv = _unit_lower_inverses([hd[-1] for heads in pre for hd in heads])
        for j, r in enumerate(rows):
            u, w, qd, kd, at, gl, xs = _dn_intra_post(pre[j], inv[j * DN_H:(j + 1) * DN_H])
            u_ref[r, :], x_ref[r, :] = u, xs
            w_ref[r, :], qd_ref[r, :], kd_ref[r, :], at_ref[r, :] = _b(w), _b(qd), _b(kd), _b(at)
            gl_ref[j] = gl

    row = lambda wd: pl.BlockSpec((R, wd), lambda n: (n, 0))
    return pl.pallas_call(
        body, name="dn_intra", grid=(NC // G,),
        in_specs=[row(W), row(W), row(W), row(HP)],
        out_specs=[row(W), row(W), row(W), row(W), row(WA), pl.BlockSpec((G, 1, HP), lambda n: (n, 0, 0)), row(WA)],
        out_shape=[SDS((L, W), f32), SDS((L, W), bf16), SDS((L, W), bf16), SDS((L, W), bf16), SDS((L, WA), bf16), SDS((NC, 1, HP), f32),
                   SDS((L, WA), f32)],
        compiler_params=_cp())(q, k, v, bg)


def _dn_scan(u, w, qd, kd, at, gl):
    L, W = u.shape
    NC = L // DN_C
    G = _pick(NC, DN_SCAN_GROUP)
    R = G * DN_C

    def body(u_ref, w_ref, qd_ref, kd_ref, at_ref, gl_ref, o_ref, vn_ref, s_ref, S):
        @pl.when(pl.program_id(0) == 0)
        def _():
            S[...] = jnp.zeros_like(S)

        for j in range(G):
            r = slice(j * DN_C, (j + 1) * DN_C)
            glrow = gl_ref[j]
            for h in range(DN_H):
                c = slice(h * DN_D, (h + 1) * DN_D)
                Sh = S[h]
                s_ref[j, h] = Sh
                Sb = _b(Sh)
                vb = _b(u_ref[r, c] - _dot(w_ref[r, c], Sb))
                vn_ref[r, c] = vb
                o_ref[r, c] = _dot(qd_ref[r, c], Sb) + _dot(at_ref[r, h * DN_C:(h + 1) * DN_C], vb)
                S[h] = Sh * jnp.exp(_lane_pick(glrow, h)) + _dot_tn(kd_ref[r, c], vb)

    row = lambda wd: pl.BlockSpec((R, wd), lambda n: (n, 0))
    return pl.pallas_call(
        body, name="dn_scan", grid=(NC // G,),
        in_specs=[row(W), row(W), row(W), row(W), row(DN_H * DN_C), pl.BlockSpec((G, 1, HP), lambda n: (n, 0, 0))],
        out_specs=[row(W), row(W), pl.BlockSpec((G, DN_H, DN_D, DN_D), lambda n: (n, 0, 0, 0))],
        out_shape=[SDS((L, W), f32), SDS((L, W), bf16), SDS((NC, DN_H, DN_D, DN_D), f32)],
        scratch_shapes=[pltpu.VMEM((DN_H, DN_D, DN_D), f32)], compiler_params=_cp())(u, w, qd, kd, at, gl)


def _dn_scan_bwd(do, w, qd, kd, at, gl):
    L, W = do.shape
    NC = L // DN_C
    G = _pick(NC, DN_SCAN_GROUP)
    R = G * DN_C
    NS = NC // G

    def body(do_ref, w_ref, qd_ref, kd_ref, at_ref, gl_ref, dvn_ref, ds_ref, dS):
        @pl.when(pl.program_id(0) == 0)
        def _():
            dS[...] = jnp.zeros_like(dS)

        for j in reversed(range(G)):
            r = slice(j * DN_C, (j + 1) * DN_C)
            glrow = gl_ref[j]
            for h in range(DN_H):
                c = slice(h * DN_D, (h + 1) * DN_D)
                dSo = dS[h]
                ds_ref[j, h] = dSo
                dob = _b(do_ref[r, c])
                dvn = _dot_tn(at_ref[r, h * DN_C:(h + 1) * DN_C], dob) + _dot(kd_ref[r, c], _b(dSo))
                dvn_ref[r, c] = dvn
                dS[h] = _dot_tn(qd_ref[r, c], dob) + dSo * jnp.exp(_lane_pick(glrow, h)) - _dot_tn(w_ref[r, c], _b(dvn))

    row = lambda wd: pl.BlockSpec((R, wd), lambda n: (NS - 1 - n, 0))
    return pl.pallas_call(
        body, name="dn_scan_bwd", grid=(NS,),
        in_specs=[row(W), row(W), row(W), row(W), row(DN_H * DN_C), pl.BlockSpec((G, 1, HP), lambda n: (NS - 1 - n, 0, 0))],
        out_specs=[row(W), pl.BlockSpec((G, DN_H, DN_D, DN_D), lambda n: (NS - 1 - n, 0, 0, 0))],
        out_shape=[SDS((L, W), f32), SDS((NC, DN_H, DN_D, DN_D), f32)],
        scratch_shapes=[pltpu.VMEM((DN_H, DN_D, DN_D), f32)], compiler_params=_cp())(do, w, qd, kd, at, gl)


def _dn_intra_bwd(q, k, v, bg, xinv, do, vn, dvn, states, dstates):
    L, W = q.shape
    NC = L // DN_C
    G = _pick(NC, DN_INTRA_GROUP)
    R = G * DN_C

    def body(q_ref, k_ref, v_ref, bg_ref, x_ref, do_ref, vn_ref, dvn_ref, s_ref, ds_ref, dq_ref, dk_ref, dv_ref, dbg_ref):
        lane1 = _iota((1, HP), 1)
        rows = [slice(j * DN_C, (j + 1) * DN_C) for j in range(G)]
        xs = [[x_ref[r, h * DN_C:(h + 1) * DN_C] for h in range(DN_H)] for r in rows]
        outs, vjp = jax.vjp(functools.partial(_dn_intra_group, xs=xs), q_ref[...], k_ref[...], v_ref[...], bg_ref[...])
        dws, dqds, dkds, dats, dgls = [], [], [], [], []
        for j, r in enumerate(rows):
            dw, dqd, dkd, dat = [], [], [], []
            dgl = jnp.zeros((1, HP), f32)
            for h in range(DN_H):
                c = slice(h * DN_D, (h + 1) * DN_D)
                Sh, dSo = s_ref[j, h], ds_ref[j, h]
                Sb, dob, vb = _b(Sh), _b(do_ref[r, c]), vn_ref[r, c]
                dw.append(-_dot_nt(_b(dvn_ref[r, c]), Sb))
                dqd.append(_dot_nt(dob, Sb))
                dat.append(_dot_nt(dob, vb))
                dkd.append(_dot_nt(vb, _b(dSo)))
                dcd = jnp.sum(jnp.sum(Sh * dSo, axis=1, keepdims=True), axis=0, keepdims=True)
                dgl = dgl + jnp.where(lane1 == h, dcd * jnp.exp(_lane_pick(outs[5][j], h)), 0.0)
            cat = lambda xs_: jnp.concatenate(xs_, axis=1)
            dws.append(cat(dw)), dqds.append(cat(dqd)), dkds.append(cat(dkd)), dats.append(cat(dat)), dgls.append(dgl)
        cat0 = lambda xs_: jnp.concatenate(xs_, axis=0)
        dq, dk, dv, dbg = vjp((dvn_ref[...], cat0(dws), cat0(dqds), cat0(dkds), cat0(dats), tuple(dgls)))
        dq_ref[...], dk_ref[...], dv_ref[...], dbg_ref[...] = dq, dk, dv, dbg

    row = lambda wd: pl.BlockSpec((R, wd), lambda n: (n, 0))
    st = pl.BlockSpec((G, DN_H, DN_D, DN_D), lambda n: (n, 0, 0, 0))
    return pl.pallas_call(
        body, name="dn_intra_bwd", grid=(NC // G,),
        in_specs=[row(W), row(W), row(W), row(HP), row(DN_H * DN_C), row(W), row(W), row(W), st, st],
        out_specs=[row(W), row(W), row(W), row(HP)],
        out_shape=[SDS((L, W), f32)] * 3 + [SDS((L, HP), f32)],
        compiler_params=_cp())(q, k, v, bg, xinv, do, vn, dvn, states, dstates)


def _dn_normgate(oraw, dz, wn):
    outs = []
    for h in range(DN_H):
        sl = slice(h * DN_D, (h + 1) * DN_D)
        z = dz[:, sl]
        outs.append(_rms(oraw[:, sl], wn) * (z * _sigmoid(z)))
    return jnp.concatenate(outs, axis=1)


def _mix_fwd(op, oraw, dz, ga, gb, h0, wn, wbf, wbd, wo):
    L, D = h0.shape
    TM = _pick(L, ROW_TILES)

    def body(op_ref, or_ref, dz_ref, ga_ref, gb_ref, h0_ref, wn_ref, wbf_ref, wbd_ref, wo_ref, h1_ref):
        pf = _dot(_b(op_ref[...]), wbf_ref[...])
        pd = _dot(_b(_dn_normgate(or_ref[...], dz_ref[...], wn_ref[...])), wbd_ref[...])
        y = _sigmoid(ga_ref[...]) * pf + _sigmoid(gb_ref[...]) * pd
        h1_ref[...] = h0_ref[...] + _dot(_b(y), wo_ref[...])

    row = lambda wd: pl.BlockSpec((TM, wd), lambda i: (i, 0))
    full = lambda a: pl.BlockSpec(a.shape, lambda i: (0, 0))
    return pl.pallas_call(
        body, name="mix_fwd", grid=(L // TM,),
        in_specs=[row(op.shape[1]), row(oraw.shape[1]), row(dz.shape[1]), row(D), row(D), row(D), full(wn), full(wbf), full(wbd), full(wo)],
        out_specs=row(D), out_shape=SDS((L, D), f32), compiler_params=_cp())(op, oraw, dz, ga, gb, h0, wn, wbf, wbd, wo)


def _mix_bwd(dh1, op, oraw, dz, ga, gb, wn, wbf, wbd, wo):
    L, D = dh1.shape
    TM = _pick(L, ROW_TILES)
    WF, WD = op.shape[1], oraw.shape[1]

    def body(dh1_ref, op_ref, or_ref, dz_ref, ga_ref, gb_ref, wn_ref, wbf_ref, wbd_ref, wo_ref,
             dop_ref, dor_ref, dg_ref, aft_ref, adt_ref, dpf_ref, dpd_ref, yt_ref, dmix_ref, acc_ref):
        @pl.when(pl.program_id(0) == 0)
        def _():
            acc_ref[...] = jnp.zeros_like(acc_ref)

        opv = op_ref[...]
        af = _b(opv)
        ad, vjp = jax.vjp(_dn_normgate, or_ref[...], dz_ref[...], wn_ref[...])
        adb = _b(ad)
        pf, pd = _dot(af, wbf_ref[...]), _dot(adb, wbd_ref[...])
        sa, sb = _sigmoid(ga_ref[...]), _sigmoid(gb_ref[...])
        dmix = _b(dh1_ref[...])
        dy = _dot_nt(dmix, wo_ref[...])
        dpf, dpd = _b(dy * sa), _b(dy * sb)
        dor, ddz, dwn = vjp(_dot_nt(dpd, wbd_ref[...]))
        dop_ref[...] = _dot_nt(dpf, wbf_ref[...])
        dor_ref[...] = dor
        dg_ref[:, 0:WD] = _b(ddz)
        dg_ref[:, WD:WD + D] = _b(dy * pf * sa * (1.0 - sa))
        dg_ref[:, WD + D:] = _b(dy * pd * sb * (1.0 - sb))
        aft_ref[...], adt_ref[...], yt_ref[...] = _b(opv.T), _b(ad.T), _b((sa * pf + sb * pd).T)
        dpf_ref[...], dpd_ref[...], dmix_ref[...] = dpf, dpd, dmix
        acc_ref[0:1, :] += dwn

    row = lambda wd: pl.BlockSpec((TM, wd), lambda i: (i, 0))
    col = lambda wd: pl.BlockSpec((wd, TM), lambda i: (0, i))
    full = lambda a: pl.BlockSpec(a.shape, lambda i: (0, 0))
    return pl.pallas_call(
        body, name="mix_bwd", grid=(L // TM,),
        in_specs=[row(D), row(WF), row(WD), row(WD), row(D), row(D), full(wn), full(wbf), full(wbd), full(wo)],
        out_specs=[row(WF), row(WD), row(WD + 2 * D), col(WF), col(WD), row(D), row(D), col(D), row(D),
                   pl.BlockSpec((8, HP), lambda i: (0, 0))],
        out_shape=[SDS((L, WF), f32), SDS((L, WD), f32), SDS((L, WD + 2 * D), bf16), SDS((WF, L), bf16), SDS((WD, L), bf16),
                   SDS((L, D), bf16), SDS((L, D), bf16), SDS((D, L), bf16), SDS((L, D), bf16), SDS((8, HP), f32)],
        compiler_params=_cp())(dh1, op, oraw, dz, ga, gb, wn, wbf, wbd, wo)


def _ffn_fwd_bwd(h1, tgt, w2, wf, wg, wu, wd):
    L, D = h1.shape
    F = wg.shape[1]
    TM = _pick(L, FFN_TILES)

    def body(h_ref, t_ref, w2_ref, wf_ref, wg_hbm, wu_hbm, wd_hbm,
             dh1_ref, xn_ref, dg_ref, du_ref, act_ref, dh2_ref, acc_ref, wg_v, wu_v, wd_v, sems):
        i = pl.program_id(0)
        _load_once([(wg_hbm, wg_v), (wu_hbm, wu_v), (wd_hbm, wd_v)], sems)

        @pl.when(i == 0)
        def _():
            acc_ref[...] = jnp.zeros_like(acc_ref)

        h1v = h_ref[...]
        xn2, vjp2 = jax.vjp(_rms, h1v, w2_ref[...])
        xb = _b(xn2)
        g, u = _dot(xb, wg_v[...]), _dot(xb, wu_v[...])
        sg = _sigmoid(g)
        ab = _b(g * sg * u)
        h2 = h1v + _dot(ab, wd_v[...])
        out, vjpf = jax.vjp(_rms, h2, wf_ref[...])
        valid = (i * TM + _iota((TM, 1), 0)) >= PREFIX
        diff = jnp.where(valid, out - t_ref[...], 0.0)
        loss = 0.5 * jnp.sum(jnp.sum(diff * diff, axis=1, keepdims=True), axis=0, keepdims=True) / D
        dh2, dwf = vjpf(diff * (1.0 / D))
        dh2b = _b(dh2)
        dact = _dot_nt(dh2b, wd_v[...])
        dgb = _b(dact * u * (sg * (1.0 + g * (1.0 - sg))))
        dub = _b(dact * (g * sg))
        dh1n, dw2 = vjp2(_dot_nt(dgb, wg_v[...]) + _dot_nt(dub, wu_v[...]))
        dh1_ref[...] = dh2 + dh1n
        xn_ref[...], dg_ref[...], du_ref[...], act_ref[...], dh2_ref[...] = xb, dgb, dub, ab, dh2b
        acc_ref[0:1, :] += dw2
        acc_ref[1:2, :] += dwf
        acc_ref[2:3, :] += jnp.broadcast_to(loss, (1, D))

    row = lambda wd_: pl.BlockSpec((TM, wd_), lambda i: (i, 0))
    vec = pl.BlockSpec((1, D), lambda i: (0, 0))
    anyspec = pl.BlockSpec(memory_space=pl.ANY)
    return pl.pallas_call(
        body, name="ffn_fwd_bwd", grid=(L // TM,),
        in_specs=[row(D), row(D), vec, vec, anyspec, anyspec, anyspec],
        out_specs=[row(D), row(D), row(F), row(F), row(F), row(D), pl.BlockSpec((8, D), lambda i: (0, 0))],
        out_shape=[SDS((L, D), f32), SDS((L, D), bf16), SDS((L, F), bf16), SDS((L, F), bf16), SDS((L, F), bf16), SDS((L, D), bf16),
                   SDS((8, D), f32)],
        scratch_shapes=[pltpu.VMEM((D, F), bf16), pltpu.VMEM((D, F), bf16), pltpu.VMEM((F, D), bf16), pltpu.SemaphoreType.DMA((3,))],
        compiler_params=_cp())(h1, tgt, w2, wf, wg, wu, wd)


def _pad_lanes(v, n=HP):
    return jnp.pad(v.astype(f32), ((0, 0), (0, n - v.shape[1])))


def _pack_w_in(w_full):
    D = w_full.shape[0]
    FW, DW = FOX_H * FOX_D, DN_H * DN_D
    o = 0
    parts = {}
    for name, wd in (("fq", FW), ("fk", FW), ("fv", FW), ("fl", FOX_H), ("dn", 3 * DW), ("ba", 2 * DN_H), ("dz", DW), ("ga", D), ("gb", D)):
        parts[name] = w_full[:, o:o + wd]
        o += wd
    assert o == w_full.shape[1]
    heads = lambda w: jnp.pad(w.reshape(D, FOX_H, FOX_D), ((0, 0), (0, 0), (0, HP - FOX_D))).reshape(D, FOX_H * HP)
    small = lambda w: jnp.pad(w, ((0, 0), (0, SMALL_W - w.shape[1])))
    packed = dict(fq=heads(parts["fq"]), fk=heads(parts["fk"]), fv=heads(parts["fv"]), sf=small(parts["fl"]), sd=small(parts["ba"]),
                  dn=parts["dn"], dz=parts["dz"], ga=parts["ga"], gb=parts["gb"])
    return jnp.concatenate([packed[name] for name, _, _, _ in _seg_layout(D)], axis=1)


def _unpack_w_in(groups, d_model):
    D = groups[0].shape[0]
    FW = FOX_H * FOX_D
    segs = {}
    for grp, g in zip(GROUPS, groups):
        o = 0
        for name, wd, _, sg in _seg_layout(d_model):
            if sg == grp:
                segs[name] = g[:, o:o + wd]
                o += wd
    heads = lambda g: g.reshape(D, FOX_H, HP)[:, :, :FOX_D].reshape(D, FW)
    return jnp.concatenate([heads(segs["fq"]), heads(segs["fk"]), heads(segs["fv"]), segs["sf"][:, :FOX_H], segs["dn"],
                            segs["sd"][:, :2 * DN_H], segs["dz"], segs["ga"], segs["gb"]], axis=1)


def _local_step(x, tgt, meta, w1, w_in, fbias, cw, alog, dtb, wn, w2, wf, late_shards):
    T, D = x.shape
    h0 = jnp.concatenate([jnp.zeros((N_PAD, D), f32), meta, x], axis=0)
    tgt_p = jnp.concatenate([jnp.zeros((PREFIX, D), f32), tgt], axis=0)
    wp = _pack_w_in(w_in)
    bias_p, alog_p, dt_p = _pad_lanes(fbias), _pad_lanes(jnp.pad(alog, ((0, 0), (DN_H, 0)))), _pad_lanes(jnp.pad(dtb, ((0, 0), (DN_H, 0))))

    xnt, fq, fk, sf, fv, dn, sd, dz, ga, gb = _in_proj(h0, w1, wp)
    qa, ka = _fox_prep(fq, fk, sf, bias_p)
    op, gathered = _fox_fwd(qa, ka, fv, [late_shards[n] for n in LATE])
    full = {n: _from_slabs(n, s) for n, s in zip(LATE, gathered)}
    wbf, wbd, wo, wg, wu, wd = (full[n] for n in ("w_branch_fox", "w_branch_dn", "w_out", "w_ffn_gate", "w_ffn_up", "w_ffn_down"))
    wbf_p = jnp.pad(wbf.reshape(FOX_H, FOX_D, D), ((0, 0), (0, HP - FOX_D), (0, 0))).reshape(FOX_H * HP, D)
    qn, kn, vn, bg = _dn_prep(dn, sd, cw, alog_p, dt_p)
    u_dn, w_dn, qd_dn, kd_dn, at_dn, gl_dn, x_dn = _dn_intra(qn, kn, vn, bg)
    oraw, vnew, states = _dn_scan(u_dn, w_dn, qd_dn, kd_dn, at_dn, gl_dn)
    h1 = _mix_fwd(op, oraw, dz, ga, gb, h0, wn, wbf_p, wbd, wo)

    dh1, xn2, dgate, dup, act, dh2, acc_f = _ffn_fwd_bwd(h1, tgt_p, w2, wf, wg, wu, wd)
    xn2t = xn2.T
    g_wg, g_wu, g_wd = _matmul_tn(xn2t, dgate, "dw_ffn_gate"), _matmul_tn(xn2t, dup, "dw_ffn_up"), _matmul_tn(act.T, dh2, "dw_ffn_down")

    dop, dor, d_mix, aft, adt, dpf, dpd, yt, dmix, acc_m = _mix_bwd(dh1, op, oraw, dz, ga, gb, wn, wbf_p, wbd, wo)
    g_wbf = _matmul_tn(aft, dpf, "dw_branch_fox").reshape(FOX_H, HP, D)[:, :FOX_D].reshape(FOX_H * FOX_D, D)
    g_wbd, g_wo = _matmul_tn(adt, dpd, "dw_branch_dn"), _matmul_tn(yt, dmix, "dw_out")

    dvnew, dstates = _dn_scan_bwd(dor, w_dn, qd_dn, kd_dn, at_dn, gl_dn)
    dqn, dkn, dvn, dbg = _dn_intra_bwd(qn, kn, vn, bg, x_dn, dor, vnew, dvnew, states, dstates)
    d_dn, acc_cw, acc_p = _dn_prep_bwd(dn, sd, cw, alog_p, dt_p, dqn, dkn, dvn, dbg)
    g_late = dict(w_branch_fox=g_wbf, w_branch_dn=g_wbd, w_out=g_wo, w_ffn_gate=g_wg, w_ffn_up=g_wu, w_ffn_down=g_wd)
    (dqa, dka, d_fv), recv = _fox_bwd(qa, ka, fv, op, dop, [_to_slabs(n, g_late[n]) for n in LATE])
    d_fox, acc_b = _fox_prep_bwd(dqa, dka, sf, bias_p)

    dgroups = [d_fox, d_fv, d_dn, d_mix]
    g_wp = [_matmul_tn(xnt, dg, "dw_in_" + grp) for grp, dg in zip(GROUPS, dgroups)]
    dh0, acc_1, (recv_w_in,) = _in_proj_bwd(dgroups, wp, h0, w1, dh1, [_to_slabs("w_in", _unpack_w_in(g_wp, D))])
    recv = dict(zip(LATE, recv), w_in=recv_w_in)

    small = dict(loss=acc_f[2, 0:1], mix_norm_w=acc_1[0], fox_forget_bias=acc_b[0, :FOX_H], dn_a_log=acc_p[0, DN_H:2 * DN_H],
                 dn_dt_bias=acc_p[1, DN_H:2 * DN_H], dn_out_norm_w=acc_m[0], ffn_norm_w=acc_f[0], final_norm_w=acc_f[1],
                 meta_tokens=dh0[N_PAD:PREFIX].reshape(-1), dn_conv_w=acc_cw[:CONV_K].reshape(-1))
    return dh0[PREFIX:], small, recv


def _mesh_pos():
    x, y, c = lax.axis_index("x"), lax.axis_index("y"), lax.axis_index("c")
    return x, y, c, 4 * x + 2 * y + c


def _peer(x, y, c, m):
    flip = lambda v, on: 1 - v if on else v
    px, py, pc = flip(x, m & 4), flip(y, m & 2), flip(c, m & 1)
    return (px, py, pc), 4 * px + 2 * py + pc


def _exchange_sems(n):
    return [pltpu.SemaphoreType.DMA((n, N_DEV - 1)), pltpu.SemaphoreType.DMA((n, N_DEV - 1)), pltpu.SemaphoreType.DMA((n,))]


def _exchange_copies(ins, outs, send_sems, recv_sems, loc_sems, gather, with_receives):
    x, y, c, me = _mesh_pos()
    src = lambda a, pid: ins[a] if gather else ins[a].at[pid]
    local = [pltpu.make_async_copy(src(a, me), outs[a].at[me], loc_sems.at[a]) for a in range(len(ins))]
    sends, recvs = [], []
    for m in range(1, N_DEV):
        peer, pid = _peer(x, y, c, m)
        for a in range(len(ins)):
            kw = dict(send_sem=send_sems.at[a, m - 1], recv_sem=recv_sems.at[a, m - 1], device_id=peer, device_id_type=MESH)
            sends.append(pltpu.make_async_remote_copy(src_ref=src(a, pid), dst_ref=outs[a].at[me], **kw))
            if with_receives:
                recvs.append(pltpu.make_async_remote_copy(src_ref=src(a, pid), dst_ref=outs[a].at[pid], **kw))
    return local, sends, recvs


def _exchange_start(ins, outs, send_sems, recv_sems, loc_sems, gather):
    local, sends, _ = _exchange_copies(ins, outs, send_sems, recv_sems, loc_sems, gather, with_receives=False)
    for cp in local + sends:
        cp.start()


def _exchange_wait(ins, outs, send_sems, recv_sems, loc_sems, gather):
    local, sends, recvs = _exchange_copies(ins, outs, send_sems, recv_sems, loc_sems, gather, with_receives=True)
    for cp in recvs:
        cp.wait_recv()
    for cp in sends:
        cp.wait_send()
    for cp in local:
        cp.wait()


def _gather_two_level(arrays, name):
    n = len(arrays)

    def body(*refs):
        ins, outs, (send_sems, recv_sems, loc_sems) = refs[:n], refs[n:2 * n], refs[2 * n:]
        x, y, c, me = _mesh_pos()
        sib = (x, y, 1 - c)
        chips = [(1 - x, y), (x, 1 - y), (1 - x, 1 - y)]
        dev_id = lambda px, py, pc: 4 * px + 2 * py + pc

        def copy(a, k, block, to, own=False):
            return pltpu.make_async_remote_copy(src_ref=ins[a] if own else outs[a].at[block], dst_ref=outs[a].at[block],
                                                send_sem=send_sems.at[a, k], recv_sem=recv_sems.at[a, k], device_id=to, device_id_type=MESH)

        local = [pltpu.make_async_copy(ins[a], outs[a].at[me], loc_sems.at[a]) for a in range(n)]
        first = [copy(a, 0, me, sib, own=True) for a in range(n)]
        first += [copy(a, 1 + j, me, (*chip, c), own=True) for j, chip in enumerate(chips) for a in range(n)]
        for cp in local + first:
            cp.start()
        passed = []
        for j, chip in enumerate(chips):
            for a in range(n):
                copy(a, 1 + j, dev_id(*chip, c), sib).wait_recv()
                cp = copy(a, 4 + j, dev_id(*chip, c), sib)
                cp.start()
                passed.append(cp)
        for a in range(n):
            copy(a, 0, dev_id(x, y, 1 - c), sib).wait_recv()
        for j, chip in enumerate(chips):
            for a in range(n):
                copy(a, 4 + j, dev_id(*chip, 1 - c), sib).wait_recv()
        for cp in first + passed:
            cp.wait_send()
        for cp in local:
            cp.wait()

    anyspec = pl.BlockSpec(memory_space=pl.ANY)
    return pl.pallas_call(
        body, name=name, in_specs=[anyspec] * n, out_specs=[anyspec] * n,
        out_shape=[SDS((N_DEV,) + a.shape, a.dtype) for a in arrays],
        scratch_shapes=_exchange_sems(n))(*arrays)


def _all_reduce_small(v):
    R = v.shape[0]

    def body(v_ref, o_ref, gath, send_sems, recv_sems):
        x, y, c, me = _mesh_pos()
        gath[me] = v_ref[...]
        sends = []
        for m in range(1, N_DEV):
            peer, _ = _peer(x, y, c, m)
            cp = pltpu.make_async_remote_copy(src_ref=v_ref, dst_ref=gath.at[me], send_sem=send_sems.at[m - 1],
                                              recv_sem=recv_sems.at[m - 1], device_id=peer, device_id_type=MESH)
            cp.start()
            sends.append(cp)
        for m in range(1, N_DEV):
            peer, pid = _peer(x, y, c, m)
            pltpu.make_async_remote_copy(src_ref=v_ref, dst_ref=gath.at[pid], send_sem=send_sems.at[m - 1],
                                         recv_sem=recv_sems.at[m - 1], device_id=peer, device_id_type=MESH).wait_recv()
        for cp in sends:
            cp.wait_send()
        tot = gath[0]
        for d in range(1, N_DEV):
            tot = tot + gath[d]
        o_ref[...] = tot

    vm = pl.BlockSpec(memory_space=pltpu.VMEM)
    return pl.pallas_call(
        body, name="all_reduce_small", in_specs=[vm], out_specs=vm, out_shape=SDS((R, HP), f32),
        scratch_shapes=[pltpu.VMEM((N_DEV, R, HP), f32), pltpu.SemaphoreType.DMA((N_DEV - 1,)), pltpu.SemaphoreType.DMA((N_DEV - 1,))],
        )(v)


def _adamw_math(w, g, m, v):
    m = ADAM_B1 * m + (1.0 - ADAM_B1) * g
    v = ADAM_B2 * v + (1.0 - ADAM_B2) * (g * g)
    m_hat = m / (1.0 - ADAM_B1 ** ADAM_STEP)
    v_hat = v / (1.0 - ADAM_B2 ** ADAM_STEP)
    return -ADAM_LR * (m_hat / (jnp.sqrt(v_hat) + ADAM_EPS) + ADAM_WD * w), m, v


def _adamw(g, w, m, v, name):
    R, Cc = w.shape[-2:]
    TR = R if R <= 512 else _pick(R, (256, 128))
    slabs = g.ndim == 3
    lead = w.ndim - 2

    def body(g_ref, w_ref, m_ref, v_ref, go_ref, d_ref, mo_ref, vo_ref):
        if slabs:
            gs = g_ref[0].astype(f32)
            for k in range(1, N_DEV):
                gs = gs + g_ref[k].astype(f32)
        else:
            gs = g_ref[...]
        at = 0 if lead else Ellipsis
        d, mn, vn = _adamw_math(w_ref[at], gs, m_ref[at], v_ref[at])
        go_ref[at], d_ref[at], mo_ref[at], vo_ref[at] = gs, d, mn, vn

    blk = pl.BlockSpec((1,) * lead + (TR, Cc), lambda i: (0,) * lead + (i, 0))
    gblk = pl.BlockSpec((N_DEV, TR, Cc), lambda i: (0, i, 0)) if slabs else pl.BlockSpec((TR, Cc), lambda i: (i, 0))
    return pl.pallas_call(
        body, name=name, grid=(R // TR,), in_specs=[gblk, blk, blk, blk], out_specs=[blk] * 4,
        out_shape=[SDS(w.shape, f32)] * 4, compiler_params=_cp())(g, w, m, v)


WEIGHTS = ("meta_tokens", "mix_norm_w", "w_in", "fox_forget_bias", "dn_conv_w", "dn_a_log", "dn_dt_bias", "dn_out_norm_w",
           "w_branch_fox", "w_branch_dn", "w_out", "ffn_norm_w", "w_ffn_gate", "w_ffn_up", "w_ffn_down", "final_norm_w")
COL_SHARDED = ("w_in", "w_branch_fox", "w_branch_dn", "w_ffn_gate", "w_ffn_up")
ROW_SHARDED = ("w_out", "w_ffn_down")
BIG = COL_SHARDED + ROW_SHARDED
LATE = tuple(n for n in BIG if n != "w_in")
SMALL = tuple(n for n in WEIGHTS if n not in BIG)


def _to_slabs(name, g):
    r, c = g.shape
    if name in COL_SHARDED:
        return _b(g.reshape(r, N_DEV, c // N_DEV).transpose(1, 0, 2))
    return _b(g.reshape(N_DEV, r // N_DEV, c))


def _from_slabs(name, s):
    n, r, c = s.shape
    if name in COL_SHARDED:
        return s.transpose(1, 0, 2).reshape(r, n * c)
    return s.reshape(n * r, c)


def kernel(x, meta_tokens, mix_norm_w, w_in, fox_forget_bias, dn_conv_w, dn_a_log, dn_dt_bias, dn_out_norm_w, w_branch_fox, w_branch_dn, w_out, ffn_norm_w, w_ffn_gate, w_ffn_up, w_ffn_down, final_norm_w, loss_target, m_meta_tokens, m_mix_norm_w, m_w_in, m_fox_forget_bias, m_dn_conv_w, m_dn_a_log, m_dn_dt_bias, m_dn_out_norm_w, m_w_branch_fox, m_w_branch_dn, m_w_out, m_ffn_norm_w, m_w_ffn_gate, m_w_ffn_up, m_w_ffn_down, m_final_norm_w, v_meta_tokens, v_mix_norm_w, v_w_in, v_fox_forget_bias, v_dn_conv_w, v_dn_a_log, v_dn_dt_bias, v_dn_out_norm_w, v_w_branch_fox, v_w_branch_dn, v_w_out, v_ffn_norm_w, v_w_ffn_gate, v_w_ffn_up, v_w_ffn_down, v_final_norm_w):
    w = dict(meta_tokens=meta_tokens, mix_norm_w=mix_norm_w, w_in=w_in, fox_forget_bias=fox_forget_bias, dn_conv_w=dn_conv_w, dn_a_log=dn_a_log, dn_dt_bias=dn_dt_bias, dn_out_norm_w=dn_out_norm_w, w_branch_fox=w_branch_fox, w_branch_dn=w_branch_dn, w_out=w_out, ffn_norm_w=ffn_norm_w, w_ffn_gate=w_ffn_gate, w_ffn_up=w_ffn_up, w_ffn_down=w_ffn_down, final_norm_w=final_norm_w)
    mom = dict(meta_tokens=m_meta_tokens, mix_norm_w=m_mix_norm_w, w_in=m_w_in, fox_forget_bias=m_fox_forget_bias, dn_conv_w=m_dn_conv_w, dn_a_log=m_dn_a_log, dn_dt_bias=m_dn_dt_bias, dn_out_norm_w=m_dn_out_norm_w, w_branch_fox=m_w_branch_fox, w_branch_dn=m_w_branch_dn, w_out=m_w_out, ffn_norm_w=m_ffn_norm_w, w_ffn_gate=m_w_ffn_gate, w_ffn_up=m_w_ffn_up, w_ffn_down=m_w_ffn_down, final_norm_w=m_final_norm_w)
    var = dict(meta_tokens=v_meta_tokens, mix_norm_w=v_mix_norm_w, w_in=v_w_in, fox_forget_bias=v_fox_forget_bias, dn_conv_w=v_dn_conv_w, dn_a_log=v_dn_a_log, dn_dt_bias=v_dn_dt_bias, dn_out_norm_w=v_dn_out_norm_w, w_branch_fox=v_w_branch_fox, w_branch_dn=v_w_branch_dn, w_out=v_w_out, ffn_norm_w=v_ffn_norm_w, w_ffn_gate=v_w_ffn_gate, w_ffn_up=v_w_ffn_up, w_ffn_down=v_w_ffn_down, final_norm_w=v_final_norm_w)
    two_d = lambda a: a.reshape(a.shape[-2:]) if a.ndim >= 2 else a.reshape(1, -1)
    me = 4 * lax.axis_index("x") + 2 * lax.axis_index("y") + lax.axis_index("c")

    g_in, g_meta, g_cw = _gather_two_level([_b(two_d(w["w_in"])), two_d(w["meta_tokens"]), two_d(w["dn_conv_w"])], "all_gather_early")
    meta = g_meta.transpose(1, 0, 2).reshape(N_META, -1)
    cw = g_cw.transpose(1, 0, 2).reshape(CONV_K, -1)

    gx, g_small, recv = _local_step(
        x[0], loss_target[0], meta, two_d(w["mix_norm_w"]), _from_slabs("w_in", g_in), two_d(w["fox_forget_bias"]), cw, two_d(w["dn_a_log"]),
        two_d(w["dn_dt_bias"]), two_d(w["dn_out_norm_w"]), two_d(w["ffn_norm_w"]), two_d(w["final_norm_w"]),
        {n: _b(two_d(w[n])) for n in LATE})

    order = ("loss",) + SMALL
    flat = jnp.concatenate([g_small[n].reshape(-1) for n in order])
    rows = -(-flat.shape[0] // (8 * HP)) * 8
    tot = _all_reduce_small(jnp.pad(flat, (0, rows * HP - flat.shape[0])).reshape(rows, HP)).reshape(-1)
    summed, o = {}, 0
    for n in order:
        k = g_small[n].shape[0]
        summed[n] = tot[o:o + k]
        o += k
    loss = summed["loss"][0]
    d_model = x.shape[-1]
    mcols, ccols = d_model // N_DEV, dn_conv_w.shape[-1]
    summed["meta_tokens"] = lax.dynamic_slice(summed["meta_tokens"].reshape(N_META, d_model), (0, me * mcols), (N_META, mcols)).reshape(-1)
    summed["dn_conv_w"] = lax.dynamic_slice(summed["dn_conv_w"].reshape(CONV_K, ccols * N_DEV), (0, me * ccols), (CONV_K, ccols)).reshape(-1)

    res = {}
    for n in BIG:
        res[n] = _adamw(recv[n], w[n], mom[n], var[n], "adamw_" + n)
    sizes = [summed[n].shape[0] for n in SMALL]
    srows = -(-sum(sizes) // (8 * HP)) * 8
    pack = lambda d: jnp.pad(jnp.concatenate([d[n].reshape(-1) for n in SMALL]), (0, srows * HP - sum(sizes))).reshape(srows, HP)
    sres = _adamw(pack(summed), pack(w), pack(mom), pack(var), "adamw_small")
    o = 0
    for n, k in zip(SMALL, sizes):
        res[n] = [r.reshape(-1)[o:o + k].reshape(w[n].shape) for r in sres]
        o += k
    return (loss, gx[None], *[res[n][0] for n in WEIGHTS], *[res[n][1] for n in WEIGHTS], *[res[n][2] for n in WEIGHTS], *[res[n][3] for n in WEIGHTS])
```

```python
import functools

import jax
import jax.numpy as jnp
from jax import lax
from jax.experimental import pallas as pl
from jax.experimental.pallas import tpu as pltpu

f32, bf16 = jnp.float32, jnp.bfloat16
HI = lax.Precision.HIGHEST
MESH = pl.DeviceIdType.MESH
SDS = jax.ShapeDtypeStruct

N_DEV = 8
N_META = 16
PREFIX = 128
N_PAD = PREFIX - N_META
FOX_H, FOX_D = 8, 64
DN_H, DN_D = 4, 128
DN_C = 64
CONV_K = 4
HP = 128
SMALL_W = 256
EPS = 1e-6
NEG = -1e30
C_Q0, C_K0 = 64, 67
LSE_COL = 64

ADAM_LR, ADAM_B1, ADAM_B2, ADAM_EPS, ADAM_WD, ADAM_STEP = 0.001, 0.9, 0.999, 1e-08, 0.01, 10

VMEM_LIMIT_V7X = 56 * 1024 * 1024
ROW_TILES = (384, 128)
ATTN_TILES = (384, 128)
FFN_TILES = (192, 64)
FOX_HEAD_GROUP = 2
MAX_WGRAD_BLOCK = 1408
DN_INTRA_GROUP = (6, 3, 2, 1)
DN_SCAN_GROUP = (6, 3, 2, 1)


def _pick(n, cands):
    for c in cands:
        if n % c == 0:
            return c
    raise ValueError(f"no tile of {cands} divides {n}")


def _cp(n_axes=1):
    return pltpu.CompilerParams(dimension_semantics=("arbitrary",) * n_axes, vmem_limit_bytes=VMEM_LIMIT_V7X)


def _b(x):
    return x.astype(bf16)


def _dot(a, b):
    return jnp.dot(a, b, preferred_element_type=f32)


def _dot_nt(a, b):
    return lax.dot_general(a, b, (((1,), (1,)), ((), ())), preferred_element_type=f32)


def _dot_tn(a, b):
    return lax.dot_general(a, b, (((0,), (0,)), ((), ())), preferred_element_type=f32)


def _dot_hi(a, b):
    return jnp.dot(a, b, preferred_element_type=f32, precision=HI)


def _iota(shape, dim):
    return lax.broadcasted_iota(jnp.int32, shape, dim)


def _rms(x, w):
    return x * lax.rsqrt(jnp.mean(x * x, axis=-1, keepdims=True) + EPS) * w


def _sigmoid(x):
    return jax.nn.sigmoid(x)


def _load_once(pairs, sems):
    @pl.when(pl.program_id(0) == 0)
    def _():
        cps = [pltpu.make_async_copy(src, dst, sems.at[k]) for k, (src, dst) in enumerate(pairs)]
        for cp in cps:
            cp.start()
        for cp in cps:
            cp.wait()


def _seg_layout(d_model):
    return (("fq", FOX_H * HP, bf16, "fox"), ("fk", FOX_H * HP, bf16, "fox"), ("sf", SMALL_W, f32, "fox"),
            ("fv", FOX_H * HP, bf16, "fv"),
            ("dn", 3 * DN_H * DN_D, f32, "dn"), ("sd", SMALL_W, f32, "dn"),
            ("dz", DN_H * DN_D, f32, "mix"), ("ga", d_model, f32, "mix"), ("gb", d_model, f32, "mix"))


GROUPS = ("fox", "fv", "dn", "mix")


def _group_widths(d_model):
    return [sum(wd for _, wd, _, g in _seg_layout(d_model) if g == grp) for grp in GROUPS]


def _in_proj(h0, w1, wpt):
    L, D = h0.shape
    NP = wpt.shape[0]
    TM = _pick(L, ROW_TILES)
    segs = _seg_layout(D)
    offs, o = [], 0
    for _, wd, _, _ in segs:
        offs.append(o)
        o += wd
    assert o == NP

    def body(h_ref, w1_ref, wp_hbm, xn_ref, *rest):
        outs, (wp_v, sems) = rest[:len(segs)], rest[len(segs):]
        _load_once([(wp_hbm, wp_v)], sems)
        xn = _b(_rms(h_ref[...], w1_ref[...]))
        xn_ref[...] = xn
        for o_ref, off, (_, wd, _, _) in zip(outs, offs, segs):
            o_ref[...] = _dot_nt(xn, wp_v[off:off + wd, :]).astype(o_ref.dtype)

    row = lambda wd: pl.BlockSpec((TM, wd), lambda i: (i, 0))
    return pl.pallas_call(
        body, name="in_proj", grid=(L // TM,),
        in_specs=[row(D), pl.BlockSpec((1, D), lambda i: (0, 0)), pl.BlockSpec(memory_space=pl.ANY)],
        out_specs=[row(D)] + [row(wd) for _, wd, _, _ in segs],
        out_shape=[SDS((L, D), bf16)] + [SDS((L, wd), dt) for _, wd, dt, _ in segs],
        scratch_shapes=[pltpu.VMEM((NP, D), bf16), pltpu.SemaphoreType.DMA((1,))],
        compiler_params=_cp())(h0, w1, wpt)


def _in_proj_bwd(dgroups, wpt, h0, w1, dh1, slabs):
    L, D = h0.shape
    NP = wpt.shape[0]
    TM = _pick(L, ROW_TILES)
    NT = L // TM
    widths = [g.shape[1] for g in dgroups]
    assert sum(widths) == NP
    ng, n = len(dgroups), len(slabs)

    def body(*refs):
        dg_refs, (wp_hbm, h_ref, w1_ref, dh1_ref) = refs[:ng], refs[ng:ng + 4]
        ins, (dh0_ref, acc_ref), outs = refs[ng + 4:ng + 4 + n], refs[ng + 4 + n:ng + 6 + n], refs[ng + 6 + n:ng + 6 + 2 * n]
        wp_v, sems = refs[ng + 6 + 2 * n:ng + 8 + 2 * n]
        xsems = refs[ng + 8 + 2 * n:]
        _load_once([(wp_hbm, wp_v)], sems)

        @pl.when(pl.program_id(0) == 0)
        def _():
            acc_ref[...] = jnp.zeros_like(acc_ref)
            _exchange_start(ins, outs, *xsems, gather=False)

        dxn, off = None, 0
        for g_ref, wd in zip(dg_refs, widths):
            part = _dot(g_ref[...], wp_v[off:off + wd, :])
            dxn = part if dxn is None else dxn + part
            off += wd
        _, vjp = jax.vjp(_rms, h_ref[...], w1_ref[...])
        dh0n, dw1 = vjp(dxn)
        dh0_ref[...] = dh1_ref[...] + dh0n
        acc_ref[0:1, :] += dw1

        @pl.when(pl.program_id(0) == NT - 1)
        def _():
            _exchange_wait(ins, outs, *xsems, gather=False)

    row = lambda wd: pl.BlockSpec((TM, wd), lambda i: (i, 0))
    anyspec = pl.BlockSpec(memory_space=pl.ANY)
    res = pl.pallas_call(
        body, name="in_proj_bwd", grid=(NT,),
        in_specs=[row(wd) for wd in widths] + [anyspec, row(D), pl.BlockSpec((1, D), lambda i: (0, 0)), row(D)] + [anyspec] * n,
        out_specs=[row(D), pl.BlockSpec((8, D), lambda i: (0, 0))] + [anyspec] * n,
        out_shape=[SDS((L, D), f32), SDS((8, D), f32)] + [SDS(a.shape, a.dtype) for a in slabs],
        scratch_shapes=[pltpu.VMEM((NP, D), bf16), pltpu.SemaphoreType.DMA((1,))] + _exchange_sems(n),
        compiler_params=_cp())(*dgroups, wpt, h0, w1, dh1, *slabs)
    return res[0], res[1], res[2:]


def _matmul_tn(a, b, name):
    L, R = a.shape
    C = b.shape[1]
    br = max(k for k in range(HP, MAX_WGRAD_BLOCK + 1, HP) if R % k == 0)

    def body(a_ref, b_ref, o_ref):
        o_ref[...] = _b(_dot_tn(a_ref[...], b_ref[...]))

    return pl.pallas_call(
        body, name=name, grid=(R // br,),
        in_specs=[pl.BlockSpec((L, br), lambda r: (0, r)), pl.BlockSpec((L, C), lambda r: (0, 0))],
        out_specs=pl.BlockSpec((br, C), lambda r: (r, 0)), out_shape=SDS((R, C), bf16), compiler_params=_cp())(a, b)


def _fox_prep(fq, fk, sf, bias_p):
    L = fq.shape[0]
    T = HP
    NT = L // T

    def body(fq_ref, fk_ref, sf_ref, b_ref, qa_ref, ka_ref, carry):
        @pl.when(pl.program_id(0) == 0)
        def _():
            carry[...] = jnp.zeros_like(carry)

        lane, row = _iota((T, HP), 1), _iota((T, HP), 0)
        logf = jnp.where(lane < FOX_H, jax.nn.log_sigmoid(sf_ref[...] + b_ref[...]), 0.0)
        c = _dot_hi((row >= lane).astype(f32), logf) + carry[...]
        carry[...] = jnp.sum(jnp.where(row == T - 1, c, 0.0), axis=0, keepdims=True)
        ones_q = jnp.where((lane >= C_K0) & (lane < C_K0 + 3), 1.0, 0.0)
        ones_k = jnp.where((lane >= C_Q0) & (lane < C_Q0 + 3), 1.0, 0.0)
        for h in range(FOX_H):
            ch = jnp.broadcast_to(jnp.sum(jnp.where(lane == h, c, 0.0), axis=1, keepdims=True), (T, HP))
            c1 = _b(ch).astype(f32)
            c2 = _b(ch - c1).astype(f32)
            c3 = _b(ch - c1 - c2).astype(f32)
            cq = jnp.where(lane == C_Q0, c1, 0.0) + jnp.where(lane == C_Q0 + 1, c2, 0.0) + jnp.where(lane == C_Q0 + 2, c3, 0.0)
            ck = jnp.where(lane == C_K0, c1, 0.0) + jnp.where(lane == C_K0 + 1, c2, 0.0) + jnp.where(lane == C_K0 + 2, c3, 0.0)
            q = fq_ref[:, h * HP:(h + 1) * HP].astype(f32) * (FOX_D ** -0.5)
            k = fk_ref[:, h * HP:(h + 1) * HP].astype(f32)
            qa_ref[h] = _b(q + cq + ones_q)
            ka_ref[h] = _b(k + ones_k - ck)

    return pl.pallas_call(
        body, name="fox_prep", grid=(NT,),
        in_specs=[pl.BlockSpec((T, FOX_H * HP), lambda i: (i, 0)), pl.BlockSpec((T, FOX_H * HP), lambda i: (i, 0)),
                  pl.BlockSpec((T, HP), lambda i: (i, 0)), pl.BlockSpec((1, HP), lambda i: (0, 0))],
        out_specs=[pl.BlockSpec((FOX_H, T, HP), lambda i: (0, i, 0))] * 2,
        out_shape=[SDS((FOX_H, L, HP), bf16)] * 2,
        scratch_shapes=[pltpu.VMEM((1, HP), f32)], compiler_params=_cp())(fq, fk, sf, bias_p)


def _fox_prep_bwd(dqa, dka, sf, bias_p):
    L = sf.shape[0]
    T = HP
    NT = L // T
    rev = lambda i: (NT - 1 - i, 0)

    W = FOX_H * HP

    def body(dq_ref, dk_ref, sf_ref, b_ref, dg_ref, db_ref, carry):
        @pl.when(pl.program_id(0) == 0)
        def _():
            carry[...] = jnp.zeros_like(carry)
            db_ref[...] = jnp.zeros_like(db_ref)

        dq, dk = dq_ref[...], dk_ref[...]
        dg_ref[:, 0:W] = _b(dq * (FOX_D ** -0.5))
        dg_ref[:, W:2 * W] = _b(dk)
        lane, row = _iota((T, HP), 1), _iota((T, HP), 0)
        dc = jnp.zeros((T, HP), f32)
        for h in range(FOX_H):
            col = jnp.sum(jnp.where(lane == C_Q0, dq[:, h * HP:(h + 1) * HP], 0.0)
                          - jnp.where(lane == C_K0, dk[:, h * HP:(h + 1) * HP], 0.0), axis=1, keepdims=True)
            dc = dc + jnp.where(lane == h, col, 0.0)
        dl = _dot_hi((row <= lane).astype(f32), dc) + carry[...]
        carry[...] = jnp.sum(jnp.where(row == 0, dl, 0.0), axis=0, keepdims=True)
        dx = jnp.where(lane < FOX_H, dl * _sigmoid(-(sf_ref[...] + b_ref[...])), 0.0)
        dg_ref[:, 2 * W:2 * W + HP] = _b(dx)
        dg_ref[:, 2 * W + HP:] = jnp.zeros((T, SMALL_W - HP), bf16)
        db_ref[0:1, :] += jnp.sum(dx, axis=0, keepdims=True)

    return pl.pallas_call(
        body, name="fox_prep_bwd", grid=(NT,),
        in_specs=[pl.BlockSpec((T, W), rev), pl.BlockSpec((T, W), rev), pl.BlockSpec((T, HP), rev), pl.BlockSpec((1, HP), lambda i: (0, 0))],
        out_specs=[pl.BlockSpec((T, 2 * W + SMALL_W), rev), pl.BlockSpec((8, HP), lambda i: (0, 0))],
        out_shape=[SDS((L, 2 * W + SMALL_W), bf16), SDS((8, HP), f32)],
        scratch_shapes=[pltpu.VMEM((1, HP), f32)], compiler_params=_cp())(dqa, dka, sf, bias_p)


def _tile_start(j, T):
    return j * T if isinstance(j, int) else pl.multiple_of(j * T, T)


def _fox_fwd(qa, ka, fv, shards):
    L = qa.shape[1]
    TQ = TK = _pick(L, ATTN_TILES)
    NQ = L // TQ
    n = len(shards)
    HG = FOX_HEAD_GROUP

    def body(q_ref, k_ref, v_ref, *rest):
        ins, o_ref, outs, sems = rest[:n], rest[n], rest[n + 1:2 * n + 1], rest[2 * n + 1:]
        h, i = pl.program_id(0), pl.program_id(1)

        @pl.when((h == 0) & (i == 0))
        def _():
            _exchange_start(ins, outs, *sems, gather=True)

        qs = [q_ref[a] for a in range(HG)]
        rowg = i * TQ + _iota((TQ, TK), 0)
        colb = _iota((TQ, TK), 1)

        def step(j, carry, masked):
            ms, ls, accs = carry
            k0 = _tile_start(j, TK)
            ss = [_dot_nt(qs[a], k_ref[a, pl.ds(k0, TK), :]) for a in range(HG)]
            if masked:
                colg = colb + j * TK
                keep = (colg <= rowg) & (colg >= N_PAD)
                ss = [jnp.where(keep, s, NEG) for s in ss]
            m_new = [jnp.maximum(m, jnp.max(s, axis=1, keepdims=True)) for m, s in zip(ms, ss)]
            ps = [jnp.exp(s - m) for s, m in zip(ss, m_new)]
            alphas = [jnp.exp(m - mn) for m, mn in zip(ms, m_new)]
            ls = [al * l + jnp.sum(p, axis=1, keepdims=True) for al, l, p in zip(alphas, ls, ps)]
            accs = [al * acc + _dot(_b(p), v_ref[pl.ds(k0, TK), a * HP:(a + 1) * HP]) for a, (al, acc, p) in enumerate(zip(alphas, accs, ps))]
            return m_new, ls, accs

        init = ([jnp.full((TQ, 1), NEG, f32)] * HG, [jnp.zeros((TQ, 1), f32)] * HG, [jnp.zeros((TQ, HP), f32)] * HG)
        carry = step(0, init, True)
        carry = lax.fori_loop(1, i, functools.partial(step, masked=False), carry)
        ms, ls, accs = lax.fori_loop(jnp.maximum(i, 1), i + 1, functools.partial(step, masked=True), carry)
        lane = _iota((TQ, HP), 1)
        for a in range(HG):
            o_ref[:, a * HP:(a + 1) * HP] = jnp.where(lane == LSE_COL, ms[a] + jnp.log(ls[a]), accs[a] / ls[a])

        @pl.when((h == FOX_H // HG - 1) & (i == NQ - 1))
        def _():
            _exchange_wait(ins, outs, *sems, gather=True)

    anyspec = pl.BlockSpec(memory_space=pl.ANY)
    res = pl.pallas_call(
        body, name="fox_fwd", grid=(FOX_H // HG, NQ),
        in_specs=[pl.BlockSpec((HG, TQ, HP), lambda h, i: (h, i, 0)), pl.BlockSpec((HG, L, HP), lambda h, i: (h, 0, 0)),
                  pl.BlockSpec((L, HG * HP), lambda h, i: (0, h))] + [anyspec] * n,
        out_specs=[pl.BlockSpec((TQ, HG * HP), lambda h, i: (i, h))] + [anyspec] * n,
        out_shape=[SDS((L, FOX_H * HP), f32)] + [SDS((N_DEV,) + a.shape, a.dtype) for a in shards],
        scratch_shapes=_exchange_sems(n), compiler_params=_cp(2))(qa, ka, fv, *shards)
    return res[0], res[1:]


def _fox_bwd(qa, ka, fv, op, dop, slabs):
    L = qa.shape[1]
    TQ = TK = _pick(L, ATTN_TILES)
    NQ = L // TQ
    n = len(slabs)
    HG = FOX_HEAD_GROUP

    def body(q_ref, k_ref, v_ref, o_ref, do_ref, *rest):
        ins, (dq_ref, dk_ref, dv_ref), outs = rest[:n], rest[n:n + 3], rest[n + 3:2 * n + 3]
        lse_s, delta_s = rest[2 * n + 3:2 * n + 5]
        sems = rest[2 * n + 5:]
        h, j = pl.program_id(0), pl.program_id(1)
        cols = [slice(a * HP, (a + 1) * HP) for a in range(HG)]

        @pl.when((h == 0) & (j == 0))
        def _():
            _exchange_start(ins, outs, *sems, gather=False)

        lane = _iota((TQ, HP), 1)

        @pl.when(j == 0)
        def _():
            dq_ref[...] = jnp.zeros_like(dq_ref)
            for t in range(NQ):
                r = slice(t * TQ, (t + 1) * TQ)
                for a in range(HG):
                    o, do = o_ref[r, cols[a]], do_ref[r, cols[a]]
                    lse_s[a, r, :] = jnp.sum(jnp.where(lane == LSE_COL, o, 0.0), axis=1, keepdims=True)
                    delta_s[a, r, :] = jnp.sum(jnp.where(lane < FOX_D, o * do, 0.0), axis=1, keepdims=True)

        kts = [k_ref[a] for a in range(HG)]
        vts = [v_ref[:, cols[a]] for a in range(HG)]
        colg = j * TK + _iota((TQ, TK), 1)
        rowb = _iota((TQ, TK), 0)

        def step(i, carry, masked):
            dks, dvs = carry
            r0 = _tile_start(i, TQ)
            rows = pl.ds(r0, TQ)
            qs = [q_ref[a, rows, :] for a in range(HG)]
            ps = [jnp.exp(_dot_nt(q, kt) - lse_s[a, rows, :]) for a, (q, kt) in enumerate(zip(qs, kts))]
            if masked:
                keep = (colg <= rowb + i * TQ) & (colg >= N_PAD)
                ps = [jnp.where(keep, p, 0.0) for p in ps]
            dobs = [_b(do_ref[rows, cols[a]]) for a in range(HG)]
            dvs = [dv + _dot_tn(_b(p), dob) for dv, p, dob in zip(dvs, ps, dobs)]
            dss = [_b(p * (_dot_nt(dob, vt) - delta_s[a, rows, :])) for a, (p, dob, vt) in enumerate(zip(ps, dobs, vts))]
            for a in range(HG):
                dq_ref[rows, cols[a]] += _dot(dss[a], kts[a])
            dks = [dk + _dot_tn(ds, q) for dk, ds, q in zip(dks, dss, qs)]
            return dks, dvs

        zeros = [jnp.zeros((TK, HP), f32)] * HG
        carry = step(j, (zeros, zeros), True)
        split = jnp.where(j == 0, NQ, j + 1)
        carry = lax.fori_loop(j + 1, split, functools.partial(step, masked=True), carry)
        dks, dvs = lax.fori_loop(split, NQ, functools.partial(step, masked=False), carry)
        for a in range(HG):
            dk_ref[:, cols[a]] = dks[a]
            dv_ref[:, cols[a]] = _b(dvs[a])

        @pl.when((h == FOX_H // HG - 1) & (j == NQ - 1))
        def _():
            _exchange_wait(ins, outs, *sems, gather=False)

    head = pl.BlockSpec((L, HG * HP), lambda h, j: (0, h))
    tile = pl.BlockSpec((TK, HG * HP), lambda h, j: (j, h))
    anyspec = pl.BlockSpec(memory_space=pl.ANY)
    res = pl.pallas_call(
        body, name="fox_bwd", grid=(FOX_H // HG, L // TK),
        in_specs=[pl.BlockSpec((HG, L, HP), lambda h, j: (h, 0, 0)), pl.BlockSpec((HG, TK, HP), lambda h, j: (h, j, 0)), tile, head, head]
        + [anyspec] * n,
        out_specs=[head, tile, tile] + [anyspec] * n,
        out_shape=[SDS((L, FOX_H * HP), f32), SDS((L, FOX_H * HP), f32), SDS((L, FOX_H * HP), bf16)] + [SDS(a.shape, a.dtype) for a in slabs],
        scratch_shapes=[pltpu.VMEM((HG, L, 1), f32), pltpu.VMEM((HG, L, 1), f32)] + _exchange_sems(n),
        compiler_params=_cp(2))(qa, ka, fv, op, dop, *slabs)
    return res[:3], res[3:]


def _dn_post(y, sd, alog_p, dt_p, valid):
    a = y * _sigmoid(y)
    W = DN_H * DN_D
    heads = []
    for part, scale in ((0, DN_D ** -0.5), (1, 1.0)):
        for h in range(DN_H):
            xh = a[:, part * W + h * DN_D:part * W + (h + 1) * DN_D]
            heads.append(xh * lax.rsqrt(jnp.sum(xh * xh, axis=-1, keepdims=True) + EPS) * scale)
    q = jnp.concatenate(heads[:DN_H], axis=1)
    k = jnp.concatenate(heads[DN_H:], axis=1)
    v = a[:, 2 * W:3 * W]
    lane = _iota(sd.shape, 1)
    beta = _sigmoid(sd) * valid
    g = -jnp.exp(alog_p) * jax.nn.softplus(sd + dt_p) * valid
    bg = jnp.where(lane < DN_H, beta, jnp.where(lane < 2 * DN_H, g, 0.0))
    return q, k, v, bg


def _conv_fwd(ext_ref, cw_ref, TM):
    y = cw_ref[0:1, :] * ext_ref[8 - (CONV_K - 1):8 - (CONV_K - 1) + TM, :]
    for i in range(1, CONV_K):
        o = 8 - (CONV_K - 1) + i
        y = y + cw_ref[i:i + 1, :] * ext_ref[o:o + TM, :]
    return y


def _dn_prep(dn, sd, cw, alog_p, dt_p):
    L, W3 = dn.shape
    TM = _pick(L, ROW_TILES)
    W = DN_H * DN_D

    def body(dn_ref, halo_ref, sd_ref, cw_ref, al_ref, dt_ref, q_ref, k_ref, v_ref, bg_ref, ext):
        i = pl.program_id(0)
        ext[0:8, :] = jnp.where(i == 0, 0.0, halo_ref[...])
        ext[8:, :] = dn_ref[...]
        y = _conv_fwd(ext, cw_ref, TM)
        valid = ((i * TM + _iota((TM, 1), 0)) >= N_PAD).astype(f32)
        q, k, v, bg = _dn_post(y, sd_ref[...], al_ref[...], dt_ref[...], valid)
        q_ref[...], k_ref[...], v_ref[...], bg_ref[...] = q, k, v, bg

    row = lambda wd: pl.BlockSpec((TM, wd), lambda i: (i, 0))
    vec = pl.BlockSpec((1, HP), lambda i: (0, 0))
    return pl.pallas_call(
        body, name="dn_prep", grid=(L // TM,),
        in_specs=[row(W3), pl.BlockSpec((8, W3), lambda i: (jnp.maximum(i * (TM // 8) - 1, 0), 0)), row(HP),
                  pl.BlockSpec((CONV_K, W3), lambda i: (0, 0)), vec, vec],
        out_specs=[row(W), row(W), row(W), row(HP)],
        out_shape=[SDS((L, W), f32)] * 3 + [SDS((L, HP), f32)],
        scratch_shapes=[pltpu.VMEM((TM + 8, W3), f32)], compiler_params=_cp())(dn, dn, sd, cw, alog_p, dt_p)


def _dn_prep_bwd(dn, sd, cw, alog_p, dt_p, dq, dk, dv, dbg):
    L, W3 = dn.shape
    TM = _pick(L, ROW_TILES)
    NT = L // TM
    W = DN_H * DN_D

    def body(dn_ref, halo_ref, sd_ref, cw_ref, al_ref, dt_ref, dq_ref, dk_ref, dv_ref, dbg_ref,
             dg_ref, dcw_ref, dp_ref, ext, dyp, carry):
        i = pl.program_id(0)
        t = NT - 1 - i

        @pl.when(i == 0)
        def _():
            carry[...] = jnp.zeros_like(carry)
            dcw_ref[...] = jnp.zeros_like(dcw_ref)
            dp_ref[...] = jnp.zeros_like(dp_ref)
            dyp[...] = jnp.zeros_like(dyp)

        ext[0:8, :] = jnp.where(t == 0, 0.0, halo_ref[...])
        ext[8:, :] = dn_ref[...]
        y = _conv_fwd(ext, cw_ref, TM)
        valid = ((t * TM + _iota((TM, 1), 0)) >= N_PAD).astype(f32)
        _, vjp = jax.vjp(functools.partial(_dn_post, valid=valid), y, sd_ref[...], al_ref[...], dt_ref[...])
        dy, dsd, dal, ddt = vjp((dq_ref[...], dk_ref[...], dv_ref[...], dbg_ref[...]))
        dg_ref[:, W3:W3 + HP] = _b(dsd)
        dg_ref[:, W3 + HP:] = jnp.zeros((TM, SMALL_W - HP), bf16)
        dp_ref[0:1, :] += dal
        dp_ref[1:2, :] += ddt
        dyp[8:8 + TM, :] = dy
        o0 = CONV_K - 1
        dext = cw_ref[0:1, :] * dyp[o0:o0 + TM + 8, :]
        for k in range(1, CONV_K):
            dext = dext + cw_ref[k:k + 1, :] * dyp[o0 - k:o0 - k + TM + 8, :]
        for k in range(CONV_K):
            o = 8 - (CONV_K - 1) + k
            dcw_ref[k:k + 1, :] += jnp.sum(dy * ext[o:o + TM, :], axis=0, keepdims=True)
        dg_ref[:, 0:W3] = _b(jnp.concatenate([dext[8:TM, :], dext[TM:TM + 8, :] + carry[...]], axis=0))
        carry[...] = dext[0:8, :]

    row = lambda wd: pl.BlockSpec((TM, wd), lambda i: (NT - 1 - i, 0))
    vec = pl.BlockSpec((1, HP), lambda i: (0, 0))
    return pl.pallas_call(
        body, name="dn_prep_bwd", grid=(NT,),
        in_specs=[row(W3), pl.BlockSpec((8, W3), lambda i: (jnp.maximum((NT - 1 - i) * (TM // 8) - 1, 0), 0)), row(HP),
                  pl.BlockSpec((CONV_K, W3), lambda i: (0, 0)), vec, vec, row(W), row(W), row(W), row(HP)],
        out_specs=[row(W3 + SMALL_W), pl.BlockSpec((8, W3), lambda i: (0, 0)), pl.BlockSpec((8, HP), lambda i: (0, 0))],
        out_shape=[SDS((L, W3 + SMALL_W), bf16), SDS((8, W3), f32), SDS((8, HP), f32)],
        scratch_shapes=[pltpu.VMEM((TM + 8, W3), f32), pltpu.VMEM((TM + 16, W3), f32), pltpu.VMEM((8, W3), f32)],
        compiler_params=_cp())(dn, dn, sd, cw, alog_p, dt_p, dq, dk, dv, dbg)


def _split2(x):
    hi = _b(x)
    return hi, _b(x - hi.astype(f32))


def _split3(x):
    hi = _b(x)
    r = x - hi.astype(f32)
    mid = _b(r)
    return hi, mid, _b(r - mid.astype(f32))


def _x3(a, b, dot):
    (a1, a2), (b1, b2) = _split2(a), _split2(b)
    return dot(a1, b1) + (dot(a1, b2) + dot(a2, b1))


@jax.custom_vjp
def _dot_x3(a, b):
    return _x3(a, b, _dot)


_dot_x3.defvjp(lambda a, b: (_x3(a, b, _dot), (a, b)), lambda res, g: (_x3(g, res[1], _dot_nt), _x3(res[0], g, _dot_tn)))


def _exact3(m, x, dot):
    x1, x2, x3 = _split3(x)
    return dot(m, x1) + (dot(m, x2) + dot(m, x3))


def _tri_ones(C, lower):
    row, col = _iota((C, C), 0), _iota((C, C), 1)
    return _b(((row >= col) if lower else (row <= col)).astype(f32))


@jax.custom_vjp
def _chunk_cumsum(x):
    return _exact3(_tri_ones(x.shape[0], True), x, _dot)


_chunk_cumsum.defvjp(lambda x: (_exact3(_tri_ones(x.shape[0], True), x, _dot), None),
                     lambda _, g: (_exact3(_tri_ones(g.shape[0], False), g, _dot),))


def _mxu_transpose(x):
    C = x.shape[0]
    eye = _b((_iota((C, C), 0) == _iota((C, C), 1)).astype(f32))
    return _exact3(eye, x, lambda m, part: _dot_tn(part, m))


@jax.custom_vjp
def _transpose_exact(x):
    return _mxu_transpose(x)


_transpose_exact.defvjp(lambda x: (_mxu_transpose(x), None), lambda _, g: (_mxu_transpose(g),))


def _unit_lower_inverses(lows):
    C = lows[0].shape[0]
    P = jnp.stack(lows)
    X = (_iota((C, C), 0) == _iota((C, C), 1)).astype(f32)[None] - P
    bdot = functools.partial(_x3, dot=lambda a, b: jnp.einsum("bij,bjk->bik", a, b, preferred_element_type=f32))
    for _ in range(5):
        P = bdot(P, P)
        X = X + bdot(X, P)
    return [X[i] for i in range(len(lows))]


@jax.custom_vjp
def _inverse_given(low, X):
    return X


def _inverse_given_bwd(X, g):
    return -_x3(_x3(X, g, _dot_tn), X, _dot_nt), jnp.zeros_like(X)


_inverse_given.defvjp(lambda low, X: (X, X), _inverse_given_bwd)


def _dn_intra_pre(q, k, v, bg):
    C = DN_C
    row, col = _iota((C, C), 0), _iota((C, C), 1)
    tri = row >= col
    G = _chunk_cumsum(bg)
    GT = _transpose_exact(G)
    lane = _iota((C, HP), 1)
    rowt = _iota((HP, C), 0)
    last = _iota((C, 1), 0) == C - 1
    heads = []
    for h in range(DN_H):
        beta = jnp.sum(jnp.where(lane == h, bg, 0.0), axis=1, keepdims=True)
        gcol = jnp.sum(jnp.where(lane == DN_H + h, G, 0.0), axis=1, keepdims=True)
        grow = jnp.sum(jnp.where(rowt == DN_H + h, GT, 0.0), axis=0, keepdims=True)
        glast = jnp.sum(jnp.where(last, gcol, 0.0), axis=0, keepdims=True)
        decay = jnp.exp(jnp.where(tri, gcol - grow, NEG))
        qh, kh, vh = (t[:, h * DN_D:(h + 1) * DN_D] for t in (q, k, v))
        kb = kh * beta
        low = jnp.where(row > col, _dot_nt(_b(kb), _b(kh)) * decay, 0.0)
        heads.append((beta, gcol, glast, decay, qh, kh, vh, kb, low))
    return heads


def _dn_intra_post(heads, xs):
    lane1 = _iota((1, HP), 1)
    us, ws, qds, kds, attns = [], [], [], [], []
    glrow = jnp.zeros((1, HP), f32)
    for h, ((beta, gcol, glast, decay, qh, kh, vh, kb, _), X) in enumerate(zip(heads, xs)):
        eg = jnp.exp(gcol)
        us.append(_dot_x3(X, vh * beta))
        ws.append(_dot_x3(X, kb * eg))
        attns.append(_dot_nt(_b(qh), _b(kh)) * decay)
        qds.append(qh * eg)
        kds.append(kh * jnp.exp(glast - gcol))
        glrow = glrow + jnp.where(lane1 == h, glast, 0.0)
    cat = lambda xs_: jnp.concatenate(xs_, axis=1)
    return cat(us), cat(ws), cat(qds), cat(kds), cat(attns), glrow, cat(list(xs))


def _dn_intra_group(q, k, v, bg, xs):
    G = q.shape[0] // DN_C
    rows = [slice(j * DN_C, (j + 1) * DN_C) for j in range(G)]
    pre = [_dn_intra_pre(q[r, :], k[r, :], v[r, :], bg[r, :]) for r in rows]
    inv = [[_inverse_given(hd[-1], x) for hd, x in zip(heads, xj)] for heads, xj in zip(pre, xs)]
    post = [_dn_intra_post(heads, xj) for heads, xj in zip(pre, inv)]
    return tuple(jnp.concatenate([p[i] for p in post], axis=0) for i in range(5)) + (tuple(p[5] for p in post),)


def _lane_pick(rowvec, h):
    return jnp.sum(jnp.where(_iota(rowvec.shape, 1) == h, rowvec, 0.0), axis=1, keepdims=True)


def _dn_intra(q, k, v, bg):
    L, W = q.shape
    NC = L // DN_C
    G = _pick(NC, DN_INTRA_GROUP)
    R = G * DN_C
    WA = DN_H * DN_C

    def body(q_ref, k_ref, v_ref, bg_ref, u_ref, w_ref, qd_ref, kd_ref, at_ref, gl_ref, x_ref):
        rows = [slice(j * DN_C, (j + 1) * DN_C) for j in range(G)]
        pre = [_dn_intra_pre(q_ref[r, :], k_ref[r, :], v_ref[r, :], bg_ref[r, :]) for r in rows]
        inv = _unit_lower_inverses([hd[-1] for heads in pre for hd in heads])
        for j, r in enumerate(rows):
            u, w, qd, kd, at, gl, xs = _dn_intra_post(pre[j], inv[j * DN_H:(j + 1) * DN_H])
            u_ref[r, :], x_ref[r, :] = u, xs
            w_ref[r, :], qd_ref[r, :], kd_ref[r, :], at_ref[r, :] = _b(w), _b(qd), _b(kd), _b(at)
            gl_ref[j] = gl

    row = lambda wd: pl.BlockSpec((R, wd), lambda n: (n, 0))
    return pl.pallas_call(
        body, name="dn_intra", grid=(NC // G,),
        in_specs=[row(W), row(W), row(W), row(HP)],
        out_specs=[row(W), row(W), row(W), row(W), row(WA), pl.BlockSpec((G, 1, HP), lambda n: (n, 0, 0)), row(WA)],
        out_shape=[SDS((L, W), f32), SDS((L, W), bf16), SDS((L, W), bf16), SDS((L, W), bf16), SDS((L, WA), bf16), SDS((NC, 1, HP), f32),
                   SDS((L, WA), f32)],
        compiler_params=_cp())(q, k, v, bg)


def _dn_scan(u, w, qd, kd, at, gl):
    L, W = u.shape
    NC = L // DN_C
    G = _pick(NC, DN_SCAN_GROUP)
    R = G * DN_C

    def body(u_ref, w_ref, qd_ref, kd_ref, at_ref, gl_ref, o_ref, vn_ref, s_ref, S):
        @pl.when(pl.program_id(0) == 0)
        def _():
            S[...] = jnp.zeros_like(S)

        for j in range(G):
            r = slice(j * DN_C, (j + 1) * DN_C)
            glrow = gl_ref[j]
            for h in range(DN_H):
                c = slice(h * DN_D, (h + 1) * DN_D)
                Sh = S[h]
                s_ref[j, h] = Sh
                Sb = _b(Sh)
                vb = _b(u_ref[r, c] - _dot(w_ref[r, c], Sb))
                vn_ref[r, c] = vb
                o_ref[r, c] = _dot(qd_ref[r, c], Sb) + _dot(at_ref[r, h * DN_C:(h + 1) * DN_C], vb)
                S[h] = Sh * jnp.exp(_lane_pick(glrow, h)) + _dot_tn(kd_ref[r, c], vb)

    row = lambda wd: pl.BlockSpec((R, wd), lambda n: (n, 0))
    return pl.pallas_call(
        body, name="dn_scan", grid=(NC // G,),
        in_specs=[row(W), row(W), row(W), row(W), row(DN_H * DN_C), pl.BlockSpec((G, 1, HP), lambda n: (n, 0, 0))],
        out_specs=[row(W), row(W), pl.BlockSpec((G, DN_H, DN_D, DN_D), lambda n: (n, 0, 0, 0))],
        out_shape=[SDS((L, W), f32), SDS((L, W), bf16), SDS((NC, DN_H, DN_D, DN_D), f32)],
        scratch_shapes=[pltpu.VMEM((DN_H, DN_D, DN_D), f32)], compiler_params=_cp())(u, w, qd, kd, at, gl)


def _dn_scan_bwd(do, w, qd, kd, at, gl):
    L, W = do.shape
    NC = L // DN_C
    G = _pick(NC, DN_SCAN_GROUP)
    R = G * DN_C
    NS = NC // G

    def body(do_ref, w_ref, qd_ref, kd_ref, at_ref, gl_ref, dvn_ref, ds_ref, dS):
        @pl.when(pl.program_id(0) == 0)
        def _():
            dS[...] = jnp.zeros_like(dS)

        for j in reversed(range(G)):
            r = slice(j * DN_C, (j + 1) * DN_C)
            glrow = gl_ref[j]
            for h in range(DN_H):
                c = slice(h * DN_D, (h + 1) * DN_D)
                dSo = dS[h]
                ds_ref[j, h] = dSo
                dob = _b(do_ref[r, c])
                dvn = _dot_tn(at_ref[r, h * DN_C:(h + 1) * DN_C], dob) + _dot(kd_ref[r, c], _b(dSo))
                dvn_ref[r, c] = dvn
                dS[h] = _dot_tn(qd_ref[r, c], dob) + dSo * jnp.exp(_lane_pick(glrow, h)) - _dot_tn(w_ref[r, c], _b(dvn))

    row = lambda wd: pl.BlockSpec((R, wd), lambda n: (NS - 1 - n, 0))
    return pl.pallas_call(
        body, name="dn_scan_bwd", grid=(NS,),
        in_specs=[row(W), row(W), row(W), row(W), row(DN_H * DN_C), pl.BlockSpec((G, 1, HP), lambda n: (NS - 1 - n, 0, 0))],
        out_specs=[row(W), pl.BlockSpec((G, DN_H, DN_D, DN_D), lambda n: (NS - 1 - n, 0, 0, 0))],
        out_shape=[SDS((L, W), f32), SDS((NC, DN_H, DN_D, DN_D), f32)],
        scratch_shapes=[pltpu.VMEM((DN_H, DN_D, DN_D), f32)], compiler_params=_cp())(do, w, qd, kd, at, gl)


def _dn_intra_bwd(q, k, v, bg, xinv, do, vn, dvn, states, dstates):
    L, W = q.shape
    NC = L // DN_C
    G = _pick(NC, DN_INTRA_GROUP)
    R = G * DN_C

    def body(q_ref, k_ref, v_ref, bg_ref, x_ref, do_ref, vn_ref, dvn_ref, s_ref, ds_ref, dq_ref, dk_ref, dv_ref, dbg_ref):
        lane1 = _iota((1, HP), 1)
        rows = [slice(j * DN_C, (j + 1) * DN_C) for j in range(G)]
        xs = [[x_ref[r, h * DN_C:(h + 1) * DN_C] for h in range(DN_H)] for r in rows]
        outs, vjp = jax.vjp(functools.partial(_dn_intra_group, xs=xs), q_ref[...], k_ref[...], v_ref[...], bg_ref[...])
        dws, dqds, dkds, dats, dgls = [], [], [], [], []
        for j, r in enumerate(rows):
            dw, dqd, dkd, dat = [], [], [], []
            dgl = jnp.zeros((1, HP), f32)
            for h in range(DN_H):
                c = slice(h * DN_D, (h + 1) * DN_D)
                Sh, dSo = s_ref[j, h], ds_ref[j, h]
                Sb, dob, vb = _b(Sh), _b(do_ref[r, c]), vn_ref[r, c]
                dw.append(-_dot_nt(_b(dvn_ref[r, c]), Sb))
                dqd.append(_dot_nt(dob, Sb))
                dat.append(_dot_nt(dob, vb))
                dkd.append(_dot_nt(vb, _b(dSo)))
                dcd = jnp.sum(jnp.sum(Sh * dSo, axis=1, keepdims=True), axis=0, keepdims=True)
                dgl = dgl + jnp.where(lane1 == h, dcd * jnp.exp(_lane_pick(outs[5][j], h)), 0.0)
            cat = lambda xs_: jnp.concatenate(xs_, axis=1)
            dws.append(cat(dw)), dqds.append(cat(dqd)), dkds.append(cat(dkd)), dats.append(cat(dat)), dgls.append(dgl)
        cat0 = lambda xs_: jnp.concatenate(xs_, axis=0)
        dq, dk, dv, dbg = vjp((dvn_ref[...], cat0(dws), cat0(dqds), cat0(dkds), cat0(dats), tuple(dgls)))
        dq_ref[...], dk_ref[...], dv_ref[...], dbg_ref[...] = dq, dk, dv, dbg

    row = lambda wd: pl.BlockSpec((R, wd), lambda n: (n, 0))
    st = pl.BlockSpec((G, DN_H, DN_D, DN_D), lambda n: (n, 0, 0, 0))
    return pl.pallas_call(
        body, name="dn_intra_bwd", grid=(NC // G,),
        in_specs=[row(W), row(W), row(W), row(HP), row(DN_H * DN_C), row(W), row(W), row(W), st, st],
        out_specs=[row(W), row(W), row(W), row(HP)],
        out_shape=[SDS((L, W), f32)] * 3 + [SDS((L, HP), f32)],
        compiler_params=_cp())(q, k, v, bg, xinv, do, vn, dvn, states, dstates)


def _dn_normgate(oraw, dz, wn):
    outs = []
    for h in range(DN_H):
        sl = slice(h * DN_D, (h + 1) * DN_D)
        z = dz[:, sl]
        outs.append(_rms(oraw[:, sl], wn) * (z * _sigmoid(z)))
    return jnp.concatenate(outs, axis=1)


def _mix_fwd(op, oraw, dz, ga, gb, h0, wn, wbf, wbd, wo):
    L, D = h0.shape
    TM = _pick(L, ROW_TILES)

    def body(op_ref, or_ref, dz_ref, ga_ref, gb_ref, h0_ref, wn_ref, wbf_ref, wbd_ref, wo_ref, h1_ref):
        pf = _dot(_b(op_ref[...]), wbf_ref[...])
        pd = _dot(_b(_dn_normgate(or_ref[...], dz_ref[...], wn_ref[...])), wbd_ref[...])
        y = _sigmoid(ga_ref[...]) * pf + _sigmoid(gb_ref[...]) * pd
        h1_ref[...] = h0_ref[...] + _dot(_b(y), wo_ref[...])

    row = lambda wd: pl.BlockSpec((TM, wd), lambda i: (i, 0))
    full = lambda a: pl.BlockSpec(a.shape, lambda i: (0, 0))
    return pl.pallas_call(
        body, name="mix_fwd", grid=(L // TM,),
        in_specs=[row(op.shape[1]), row(oraw.shape[1]), row(dz.shape[1]), row(D), row(D), row(D), full(wn), full(wbf), full(wbd), full(wo)],
        out_specs=row(D), out_shape=SDS((L, D), f32), compiler_params=_cp())(op, oraw, dz, ga, gb, h0, wn, wbf, wbd, wo)


def _mix_bwd(dh1, op, oraw, dz, ga, gb, wn, wbf, wbd, wo):
    L, D = dh1.shape
    TM = _pick(L, ROW_TILES)
    WF, WD = op.shape[1], oraw.shape[1]

    def body(dh1_ref, op_ref, or_ref, dz_ref, ga_ref, gb_ref, wn_ref, wbf_ref, wbd_ref, wo_ref,
             dop_ref, dor_ref, dg_ref, af_ref, ad_ref, dpf_ref, dpd_ref, y_ref, dmix_ref, acc_ref):
        @pl.when(pl.program_id(0) == 0)
        def _():
            acc_ref[...] = jnp.zeros_like(acc_ref)

        af = _b(op_ref[...])
        ad, vjp = jax.vjp(_dn_normgate, or_ref[...], dz_ref[...], wn_ref[...])
        adb = _b(ad)
        pf, pd = _dot(af, wbf_ref[...]), _dot(adb, wbd_ref[...])
        sa, sb = _sigmoid(ga_ref[...]), _sigmoid(gb_ref[...])
        dmix = _b(dh1_ref[...])
        dy = _dot_nt(dmix, wo_ref[...])
        dpf, dpd = _b(dy * sa), _b(dy * sb)
        dor, ddz, dwn = vjp(_dot_nt(dpd, wbd_ref[...]))
        dop_ref[...] = _dot_nt(dpf, wbf_ref[...])
        dor_ref[...] = dor
        dg_ref[:, 0:WD] = _b(ddz)
        dg_ref[:, WD:WD + D] = _b(dy * pf * sa * (1.0 - sa))
        dg_ref[:, WD + D:] = _b(dy * pd * sb * (1.0 - sb))
        af_ref[...], ad_ref[...], y_ref[...] = af, adb, _b(sa * pf + sb * pd)
        dpf_ref[...], dpd_ref[...], dmix_ref[...] = dpf, dpd, dmix
        acc_ref[0:1, :] += dwn

    row = lambda wd: pl.BlockSpec((TM, wd), lambda i: (i, 0))
    full = lambda a: pl.BlockSpec(a.shape, lambda i: (0, 0))
    return pl.pallas_call(
        body, name="mix_bwd", grid=(L // TM,),
        in_specs=[row(D), row(WF), row(WD), row(WD), row(D), row(D), full(wn), full(wbf), full(wbd), full(wo)],
        out_specs=[row(WF), row(WD), row(WD + 2 * D), row(WF), row(WD), row(D), row(D), row(D), row(D),
                   pl.BlockSpec((8, HP), lambda i: (0, 0))],
        out_shape=[SDS((L, WF), f32), SDS((L, WD), f32), SDS((L, WD + 2 * D), bf16), SDS((L, WF), bf16), SDS((L, WD), bf16),
                   SDS((L, D), bf16), SDS((L, D), bf16), SDS((L, D), bf16), SDS((L, D), bf16), SDS((8, HP), f32)],
        compiler_params=_cp())(dh1, op, oraw, dz, ga, gb, wn, wbf, wbd, wo)


def _ffn_fwd_bwd(h1, tgt, w2, wf, wgt, wut, wd):
    L, D = h1.shape
    F = wd.shape[0]
    TM = _pick(L, FFN_TILES)

    def body(h_ref, t_ref, w2_ref, wf_ref, wg_hbm, wu_hbm, wd_hbm,
             dh1_ref, xn_ref, dg_ref, du_ref, act_ref, dh2_ref, acc_ref, wg_v, wu_v, wd_v, sems):
        i = pl.program_id(0)
        _load_once([(wg_hbm, wg_v), (wu_hbm, wu_v), (wd_hbm, wd_v)], sems)

        @pl.when(i == 0)
        def _():
            acc_ref[...] = jnp.zeros_like(acc_ref)

        h1v = h_ref[...]
        xn2, vjp2 = jax.vjp(_rms, h1v, w2_ref[...])
        xb = _b(xn2)
        g, u = _dot_nt(xb, wg_v[...]), _dot_nt(xb, wu_v[...])
        sg = _sigmoid(g)
        ab = _b(g * sg * u)
        h2 = h1v + _dot(ab, wd_v[...])
        out, vjpf = jax.vjp(_rms, h2, wf_ref[...])
        valid = (i * TM + _iota((TM, 1), 0)) >= PREFIX
        diff = jnp.where(valid, out - t_ref[...], 0.0)
        loss = 0.5 * jnp.sum(jnp.sum(diff * diff, axis=1, keepdims=True), axis=0, keepdims=True) / D
        dh2, dwf = vjpf(diff * (1.0 / D))
        dh2b = _b(dh2)
        dact = _dot_nt(dh2b, wd_v[...])
        dgb = _b(dact * u * (sg * (1.0 + g * (1.0 - sg))))
        dub = _b(dact * (g * sg))
        dh1n, dw2 = vjp2(_dot(dgb, wg_v[...]) + _dot(dub, wu_v[...]))
        dh1_ref[...] = dh2 + dh1n
        xn_ref[...], dg_ref[...], du_ref[...], act_ref[...], dh2_ref[...] = xb, dgb, dub, ab, dh2b
        acc_ref[0:1, :] += dw2
        acc_ref[1:2, :] += dwf
        acc_ref[2:3, :] += jnp.broadcast_to(loss, (1, D))

    row = lambda wd_: pl.BlockSpec((TM, wd_), lambda i: (i, 0))
    vec = pl.BlockSpec((1, D), lambda i: (0, 0))
    anyspec = pl.BlockSpec(memory_space=pl.ANY)
    return pl.pallas_call(
        body, name="ffn_fwd_bwd", grid=(L // TM,),
        in_specs=[row(D), row(D), vec, vec, anyspec, anyspec, anyspec],
        out_specs=[row(D), row(D), row(F), row(F), row(F), row(D), pl.BlockSpec((8, D), lambda i: (0, 0))],
        out_shape=[SDS((L, D), f32), SDS((L, D), bf16), SDS((L, F), bf16), SDS((L, F), bf16), SDS((L, F), bf16), SDS((L, D), bf16),
                   SDS((8, D), f32)],
        scratch_shapes=[pltpu.VMEM((F, D), bf16), pltpu.VMEM((F, D), bf16), pltpu.VMEM((F, D), bf16), pltpu.SemaphoreType.DMA((3,))],
        compiler_params=_cp())(h1, tgt, w2, wf, wgt, wut, wd)


def _pad_lanes(v, n=HP):
    return jnp.pad(v.astype(f32), ((0, 0), (0, n - v.shape[1])))


def _pack_w_in(wt_full):
    D = wt_full.shape[1]
    FW, DW = FOX_H * FOX_D, DN_H * DN_D
    o = 0
    parts = {}
    for name, wd in (("fq", FW), ("fk", FW), ("fv", FW), ("fl", FOX_H), ("dn", 3 * DW), ("ba", 2 * DN_H), ("dz", DW), ("ga", D), ("gb", D)):
        parts[name] = wt_full[o:o + wd]
        o += wd
    assert o == wt_full.shape[0]
    heads = lambda w: jnp.pad(w.reshape(FOX_H, FOX_D, D), ((0, 0), (0, HP - FOX_D), (0, 0))).reshape(FOX_H * HP, D)
    small = lambda w: jnp.pad(w, ((0, SMALL_W - w.shape[0]), (0, 0)))
    packed = dict(fq=heads(parts["fq"]), fk=heads(parts["fk"]), fv=heads(parts["fv"]), sf=small(parts["fl"]), sd=small(parts["ba"]),
                  dn=parts["dn"], dz=parts["dz"], ga=parts["ga"], gb=parts["gb"])
    return jnp.concatenate([packed[name] for name, _, _, _ in _seg_layout(D)], axis=0)


def _unpack_w_in(groups, d_model):
    D = groups[0].shape[1]
    FW = FOX_H * FOX_D
    segs = {}
    for grp, g in zip(GROUPS, groups):
        o = 0
        for name, wd, _, sg in _seg_layout(d_model):
            if sg == grp:
                segs[name] = g[o:o + wd]
                o += wd
    heads = lambda g: g.reshape(FOX_H, HP, D)[:, :FOX_D].reshape(FW, D)
    return jnp.concatenate([heads(segs["fq"]), heads(segs["fk"]), heads(segs["fv"]), segs["sf"][:FOX_H], segs["dn"],
                            segs["sd"][:2 * DN_H], segs["dz"], segs["ga"], segs["gb"]], axis=0)


def _local_step(x, tgt, meta, w1, w_in_t, fbias, cw, alog, dtb, wn, w2, wf, late_shards):
    T, D = x.shape
    h0 = jnp.concatenate([jnp.zeros((N_PAD, D), f32), meta, x], axis=0)
    tgt_p = jnp.concatenate([jnp.zeros((PREFIX, D), f32), tgt], axis=0)
    wp = _pack_w_in(w_in_t)
    bias_p, alog_p, dt_p = _pad_lanes(fbias), _pad_lanes(jnp.pad(alog, ((0, 0), (DN_H, 0)))), _pad_lanes(jnp.pad(dtb, ((0, 0), (DN_H, 0))))

    xn, fq, fk, sf, fv, dn, sd, dz, ga, gb = _in_proj(h0, w1, wp)
    qa, ka = _fox_prep(fq, fk, sf, bias_p)
    op, gathered = _fox_fwd(qa, ka, fv, [late_shards[n] for n in LATE])
    full = {n: _from_slabs(n, s) for n, s in zip(LATE, gathered)}
    wbf, wbd, wo, wgt, wut, wd = (full[n] for n in ("w_branch_fox", "w_branch_dn", "w_out", "w_ffn_gate", "w_ffn_up", "w_ffn_down"))
    wbf_p = jnp.pad(wbf.reshape(FOX_H, FOX_D, D), ((0, 0), (0, HP - FOX_D), (0, 0))).reshape(FOX_H * HP, D)
    qn, kn, vn, bg = _dn_prep(dn, sd, cw, alog_p, dt_p)
    u_dn, w_dn, qd_dn, kd_dn, at_dn, gl_dn, x_dn = _dn_intra(qn, kn, vn, bg)
    oraw, vnew, states = _dn_scan(u_dn, w_dn, qd_dn, kd_dn, at_dn, gl_dn)
    h1 = _mix_fwd(op, oraw, dz, ga, gb, h0, wn, wbf_p, wbd, wo)

    dh1, xn2, dgate, dup, act, dh2, acc_f = _ffn_fwd_bwd(h1, tgt_p, w2, wf, wgt, wut, wd)
    g_wg, g_wu, g_wd = _matmul_tn(dgate, xn2, "dw_ffn_gate"), _matmul_tn(dup, xn2, "dw_ffn_up"), _matmul_tn(act, dh2, "dw_ffn_down")

    dop, dor, d_mix, af, ad, dpf, dpd, yb, dmix, acc_m = _mix_bwd(dh1, op, oraw, dz, ga, gb, wn, wbf_p, wbd, wo)
    g_wbf = _matmul_tn(af, dpf, "dw_branch_fox").reshape(FOX_H, HP, D)[:, :FOX_D].reshape(FOX_H * FOX_D, D)
    g_wbd, g_wo = _matmul_tn(ad, dpd, "dw_branch_dn"), _matmul_tn(yb, dmix, "dw_out")

    dvnew, dstates = _dn_scan_bwd(dor, w_dn, qd_dn, kd_dn, at_dn, gl_dn)
    dqn, dkn, dvn, dbg = _dn_intra_bwd(qn, kn, vn, bg, x_dn, dor, vnew, dvnew, states, dstates)
    d_dn, acc_cw, acc_p = _dn_prep_bwd(dn, sd, cw, alog_p, dt_p, dqn, dkn, dvn, dbg)
    g_late = dict(w_branch_fox=g_wbf, w_branch_dn=g_wbd, w_out=g_wo, w_ffn_gate=g_wg, w_ffn_up=g_wu, w_ffn_down=g_wd)
    (dqa, dka, d_fv), recv = _fox_bwd(qa, ka, fv, op, dop, [_to_slabs(n, g_late[n]) for n in LATE])
    d_fox, acc_b = _fox_prep_bwd(dqa, dka, sf, bias_p)

    dgroups = [d_fox, d_fv, d_dn, d_mix]
    g_wp = [_matmul_tn(dg, xn, "dw_in_" + grp) for grp, dg in zip(GROUPS, dgroups)]
    dh0, acc_1, (recv_w_in,) = _in_proj_bwd(dgroups, wp, h0, w1, dh1, [_to_slabs("w_in", _unpack_w_in(g_wp, D))])
    recv = dict(zip(LATE, recv), w_in=recv_w_in)

    small = dict(loss=acc_f[2, 0:1], mix_norm_w=acc_1[0], fox_forget_bias=acc_b[0, :FOX_H], dn_a_log=acc_p[0, DN_H:2 * DN_H],
                 dn_dt_bias=acc_p[1, DN_H:2 * DN_H], dn_out_norm_w=acc_m[0], ffn_norm_w=acc_f[0], final_norm_w=acc_f[1],
                 meta_tokens=dh0[N_PAD:PREFIX].reshape(-1), dn_conv_w=acc_cw[:CONV_K].reshape(-1))
    return dh0[PREFIX:], small, recv


def _mesh_pos():
    x, y, c = lax.axis_index("x"), lax.axis_index("y"), lax.axis_index("c")
    return x, y, c, 4 * x + 2 * y + c


def _peer(x, y, c, m):
    flip = lambda v, on: 1 - v if on else v
    px, py, pc = flip(x, m & 4), flip(y, m & 2), flip(c, m & 1)
    return (px, py, pc), 4 * px + 2 * py + pc


def _exchange_sems(n):
    return [pltpu.SemaphoreType.DMA((n, N_DEV - 1)), pltpu.SemaphoreType.DMA((n, N_DEV - 1)), pltpu.SemaphoreType.DMA((n,))]


def _exchange_copies(ins, outs, send_sems, recv_sems, loc_sems, gather, with_receives):
    x, y, c, me = _mesh_pos()
    src = lambda a, pid: ins[a] if gather else ins[a].at[pid]
    local = [pltpu.make_async_copy(src(a, me), outs[a].at[me], loc_sems.at[a]) for a in range(len(ins))]
    sends, recvs = [], []
    for m in range(1, N_DEV):
        peer, pid = _peer(x, y, c, m)
        for a in range(len(ins)):
            kw = dict(send_sem=send_sems.at[a, m - 1], recv_sem=recv_sems.at[a, m - 1], device_id=peer, device_id_type=MESH)
            sends.append(pltpu.make_async_remote_copy(src_ref=src(a, pid), dst_ref=outs[a].at[me], **kw))
            if with_receives:
                recvs.append(pltpu.make_async_remote_copy(src_ref=src(a, pid), dst_ref=outs[a].at[pid], **kw))
    return local, sends, recvs


def _exchange_start(ins, outs, send_sems, recv_sems, loc_sems, gather):
    local, sends, _ = _exchange_copies(ins, outs, send_sems, recv_sems, loc_sems, gather, with_receives=False)
    for cp in local + sends:
        cp.start()


def _exchange_wait(ins, outs, send_sems, recv_sems, loc_sems, gather):
    local, sends, recvs = _exchange_copies(ins, outs, send_sems, recv_sems, loc_sems, gather, with_receives=True)
    for cp in recvs:
        cp.wait_recv()
    for cp in sends:
        cp.wait_send()
    for cp in local:
        cp.wait()


def _gather_two_level(arrays, name):
    n = len(arrays)

    def body(*refs):
        ins, outs, (send_sems, recv_sems, loc_sems) = refs[:n], refs[n:2 * n], refs[2 * n:]
        x, y, c, me = _mesh_pos()
        sib = (x, y, 1 - c)
        chips = [(1 - x, y), (x, 1 - y), (1 - x, 1 - y)]
        dev_id = lambda px, py, pc: 4 * px + 2 * py + pc

        def copy(a, k, block, to, own=False):
            return pltpu.make_async_remote_copy(src_ref=ins[a] if own else outs[a].at[block], dst_ref=outs[a].at[block],
                                                send_sem=send_sems.at[a, k], recv_sem=recv_sems.at[a, k], device_id=to, device_id_type=MESH)

        local = [pltpu.make_async_copy(ins[a], outs[a].at[me], loc_sems.at[a]) for a in range(n)]
        first = [copy(a, 0, me, sib, own=True) for a in range(n)]
        first += [copy(a, 1 + j, me, (*chip, c), own=True) for j, chip in enumerate(chips) for a in range(n)]
        for cp in local + first:
            cp.start()
        passed = []
        for j, chip in enumerate(chips):
            for a in range(n):
                copy(a, 1 + j, dev_id(*chip, c), sib).wait_recv()
                cp = copy(a, 4 + j, dev_id(*chip, c), sib)
                cp.start()
                passed.append(cp)
        for a in range(n):
            copy(a, 0, dev_id(x, y, 1 - c), sib).wait_recv()
        for j, chip in enumerate(chips):
            for a in range(n):
                copy(a, 4 + j, dev_id(*chip, 1 - c), sib).wait_recv()
        for cp in first + passed:
            cp.wait_send()
        for cp in local:
            cp.wait()

    anyspec = pl.BlockSpec(memory_space=pl.ANY)
    return pl.pallas_call(
        body, name=name, in_specs=[anyspec] * n, out_specs=[anyspec] * n,
        out_shape=[SDS((N_DEV,) + a.shape, a.dtype) for a in arrays],
        scratch_shapes=_exchange_sems(n))(*arrays)


def _all_reduce_small(v):
    R = v.shape[0]

    def body(v_ref, o_ref, gath, send_sems, recv_sems):
        x, y, c, me = _mesh_pos()
        gath[me] = v_ref[...]
        sends = []
        for m in range(1, N_DEV):
            peer, _ = _peer(x, y, c, m)
            cp = pltpu.make_async_remote_copy(src_ref=v_ref, dst_ref=gath.at[me], send_sem=send_sems.at[m - 1],
                                              recv_sem=recv_sems.at[m - 1], device_id=peer, device_id_type=MESH)
            cp.start()
            sends.append(cp)
        for m in range(1, N_DEV):
            peer, pid = _peer(x, y, c, m)
            pltpu.make_async_remote_copy(src_ref=v_ref, dst_ref=gath.at[pid], send_sem=send_sems.at[m - 1],
                                         recv_sem=recv_sems.at[m - 1], device_id=peer, device_id_type=MESH).wait_recv()
        for cp in sends:
            cp.wait_send()
        tot = gath[0]
        for d in range(1, N_DEV):
            tot = tot + gath[d]
        o_ref[...] = tot

    vm = pl.BlockSpec(memory_space=pltpu.VMEM)
    return pl.pallas_call(
        body, name="all_reduce_small", in_specs=[vm], out_specs=vm, out_shape=SDS((R, HP), f32),
        scratch_shapes=[pltpu.VMEM((N_DEV, R, HP), f32), pltpu.SemaphoreType.DMA((N_DEV - 1,)), pltpu.SemaphoreType.DMA((N_DEV - 1,))],
        )(v)


def _adamw_math(w, g, m, v):
    m = ADAM_B1 * m + (1.0 - ADAM_B1) * g
    v = ADAM_B2 * v + (1.0 - ADAM_B2) * (g * g)
    m_hat = m / (1.0 - ADAM_B1 ** ADAM_STEP)
    v_hat = v / (1.0 - ADAM_B2 ** ADAM_STEP)
    return -ADAM_LR * (m_hat / (jnp.sqrt(v_hat) + ADAM_EPS) + ADAM_WD * w), m, v


def _adamw(g, w, m, v, name):
    R, Cc = w.shape[-2:]
    if R <= 512 or R % 128 == 0:
        TR, TC = (R if R <= 512 else _pick(R, (256, 128))), Cc
    else:
        TR, TC = R, _pick(Cc, (256, 128))
    slabs = g.ndim == 3
    lead = w.ndim - 2

    def body(g_ref, w_ref, m_ref, v_ref, go_ref, d_ref, mo_ref, vo_ref):
        if slabs:
            gs = g_ref[0].astype(f32)
            for k in range(1, N_DEV):
                gs = gs + g_ref[k].astype(f32)
        else:
            gs = g_ref[...]
        at = 0 if lead else Ellipsis
        d, mn, vn = _adamw_math(w_ref[at], gs, m_ref[at], v_ref[at])
        go_ref[at], d_ref[at], mo_ref[at], vo_ref[at] = gs, d, mn, vn

    grid = (R // TR, Cc // TC)
    blk = pl.BlockSpec((1,) * lead + (TR, TC), lambda i, j: (0,) * lead + (i, j))
    gblk = pl.BlockSpec((N_DEV, TR, TC), lambda i, j: (0, i, j)) if slabs else pl.BlockSpec((TR, TC), lambda i, j: (i, j))
    return pl.pallas_call(
        body, name=name, grid=grid, in_specs=[gblk, blk, blk, blk], out_specs=[blk] * 4,
        out_shape=[SDS(w.shape, f32)] * 4, compiler_params=_cp(2))(g, w, m, v)


WEIGHTS = ("meta_tokens", "mix_norm_w", "w_in", "fox_forget_bias", "dn_conv_w", "dn_a_log", "dn_dt_bias", "dn_out_norm_w",
           "w_branch_fox", "w_branch_dn", "w_out", "ffn_norm_w", "w_ffn_gate", "w_ffn_up", "w_ffn_down", "final_norm_w")
COL_SHARDED = ("w_in", "w_branch_fox", "w_branch_dn", "w_ffn_gate", "w_ffn_up")
ROW_SHARDED = ("w_out", "w_ffn_down")
BIG = COL_SHARDED + ROW_SHARDED
LATE = tuple(n for n in BIG if n != "w_in")
SMALL = tuple(n for n in WEIGHTS if n not in BIG)
TRANSPOSED = ("w_in", "w_ffn_gate", "w_ffn_up")


def _to_slabs(name, g):
    r, c = g.shape
    if name in COL_SHARDED and name not in TRANSPOSED:
        return _b(g.reshape(r, N_DEV, c // N_DEV).transpose(1, 0, 2))
    return _b(g.reshape(N_DEV, r // N_DEV, c))


def _from_slabs(name, s):
    n, r, c = s.shape
    if name in COL_SHARDED and name not in TRANSPOSED:
        return s.transpose(1, 0, 2).reshape(r, n * c)
    return s.reshape(n * r, c)


def kernel(x, meta_tokens, mix_norm_w, w_in, fox_forget_bias, dn_conv_w, dn_a_log, dn_dt_bias, dn_out_norm_w, w_branch_fox, w_branch_dn, w_out, ffn_norm_w, w_ffn_gate, w_ffn_up, w_ffn_down, final_norm_w, loss_target, m_meta_tokens, m_mix_norm_w, m_w_in, m_fox_forget_bias, m_dn_conv_w, m_dn_a_log, m_dn_dt_bias, m_dn_out_norm_w, m_w_branch_fox, m_w_branch_dn, m_w_out, m_ffn_norm_w, m_w_ffn_gate, m_w_ffn_up, m_w_ffn_down, m_final_norm_w, v_meta_tokens, v_mix_norm_w, v_w_in, v_fox_forget_bias, v_dn_conv_w, v_dn_a_log, v_dn_dt_bias, v_dn_out_norm_w, v_w_branch_fox, v_w_branch_dn, v_w_out, v_ffn_norm_w, v_w_ffn_gate, v_w_ffn_up, v_w_ffn_down, v_final_norm_w):
    w = dict(meta_tokens=meta_tokens, mix_norm_w=mix_norm_w, w_in=w_in, fox_forget_bias=fox_forget_bias, dn_conv_w=dn_conv_w, dn_a_log=dn_a_log, dn_dt_bias=dn_dt_bias, dn_out_norm_w=dn_out_norm_w, w_branch_fox=w_branch_fox, w_branch_dn=w_branch_dn, w_out=w_out, ffn_norm_w=ffn_norm_w, w_ffn_gate=w_ffn_gate, w_ffn_up=w_ffn_up, w_ffn_down=w_ffn_down, final_norm_w=final_norm_w)
    mom = dict(meta_tokens=m_meta_tokens, mix_norm_w=m_mix_norm_w, w_in=m_w_in, fox_forget_bias=m_fox_forget_bias, dn_conv_w=m_dn_conv_w, dn_a_log=m_dn_a_log, dn_dt_bias=m_dn_dt_bias, dn_out_norm_w=m_dn_out_norm_w, w_branch_fox=m_w_branch_fox, w_branch_dn=m_w_branch_dn, w_out=m_w_out, ffn_norm_w=m_ffn_norm_w, w_ffn_gate=m_w_ffn_gate, w_ffn_up=m_w_ffn_up, w_ffn_down=m_w_ffn_down, final_norm_w=m_final_norm_w)
    var = dict(meta_tokens=v_meta_tokens, mix_norm_w=v_mix_norm_w, w_in=v_w_in, fox_forget_bias=v_fox_forget_bias, dn_conv_w=v_dn_conv_w, dn_a_log=v_dn_a_log, dn_dt_bias=v_dn_dt_bias, dn_out_norm_w=v_dn_out_norm_w, w_branch_fox=v_w_branch_fox, w_branch_dn=v_w_branch_dn, w_out=v_w_out, ffn_norm_w=v_ffn_norm_w, w_ffn_gate=v_w_ffn_gate, w_ffn_up=v_w_ffn_up, w_ffn_down=v_w_ffn_down, final_norm_w=v_final_norm_w)
    two_d = lambda a: a.reshape(a.shape[-2:]) if a.ndim >= 2 else a.reshape(1, -1)
    me = 4 * lax.axis_index("x") + 2 * lax.axis_index("y") + lax.axis_index("c")
    for d in (w, mom, var):
        for n in TRANSPOSED:
            d[n] = jnp.swapaxes(d[n], -1, -2)

    g_in, g_meta, g_cw = _gather_two_level([_b(two_d(w["w_in"])), two_d(w["meta_tokens"]), two_d(w["dn_conv_w"])], "all_gather_early")
    meta = g_meta.transpose(1, 0, 2).reshape(N_META, -1)
    cw = g_cw.transpose(1, 0, 2).reshape(CONV_K, -1)

    gx, g_small, recv = _local_step(
        x[0], loss_target[0], meta, two_d(w["mix_norm_w"]), _from_slabs("w_in", g_in), two_d(w["fox_forget_bias"]), cw, two_d(w["dn_a_log"]),
        two_d(w["dn_dt_bias"]), two_d(w["dn_out_norm_w"]), two_d(w["ffn_norm_w"]), two_d(w["final_norm_w"]),
        {n: _b(two_d(w[n])) for n in LATE})

    order = ("loss",) + SMALL
    flat = jnp.concatenate([g_small[n].reshape(-1) for n in order])
    rows = -(-flat.shape[0] // (8 * HP)) * 8
    tot = _all_reduce_small(jnp.pad(flat, (0, rows * HP - flat.shape[0])).reshape(rows, HP)).reshape(-1)
    summed, o = {}, 0
    for n in order:
        k = g_small[n].shape[0]
        summed[n] = tot[o:o + k]
        o += k
    loss = summed["loss"][0]
    d_model = x.shape[-1]
    mcols, ccols = d_model // N_DEV, dn_conv_w.shape[-1]
    summed["meta_tokens"] = lax.dynamic_slice(summed["meta_tokens"].reshape(N_META, d_model), (0, me * mcols), (N_META, mcols)).reshape(-1)
    summed["dn_conv_w"] = lax.dynamic_slice(summed["dn_conv_w"].reshape(CONV_K, ccols * N_DEV), (0, me * ccols), (CONV_K, ccols)).reshape(-1)

    res = {}
    for n in BIG:
        res[n] = _adamw(recv[n], w[n], mom[n], var[n], "adamw_" + n)
        if n in TRANSPOSED:
            res[n] = [jnp.swapaxes(r, -1, -2) for r in res[n]]
    sizes = [summed[n].shape[0] for n in SMALL]
    srows = -(-sum(sizes) // (8 * HP)) * 8
    pack = lambda d: jnp.pad(jnp.concatenate([d[n].reshape(-1) for n in SMALL]), (0, srows * HP - sum(sizes))).reshape(srows, HP)
    sres = _adamw(pack(summed), pack(w), pack(mom), pack(var), "adamw_small")
    o = 0
    for n, k in zip(SMALL, sizes):
        res[n] = [r.reshape(-1)[o:o + k].reshape(w[n].shape) for r in sres]
        o += k
    return (loss, gx[None], *[res[n][0] for n in WEIGHTS], *[res[n][1] for n in WEIGHTS], *[res[n][2] for n in WEIGHTS], *[res[n][3] for n in WEIGHTS])
```

```python
import functools

import jax
import jax.numpy as jnp
from jax import lax
from jax.experimental import pallas as pl
from jax.experimental.pallas import tpu as pltpu

f32, bf16 = jnp.float32, jnp.bfloat16
HI = lax.Precision.HIGHEST
MESH = pl.DeviceIdType.MESH
SDS = jax.ShapeDtypeStruct

N_DEV = 8
N_META = 16
PREFIX = 128
N_PAD = PREFIX - N_META
FOX_H, FOX_D = 8, 64
DN_H, DN_D = 4, 128
DN_C = 64
CONV_K = 4
HP = 128
SMALL_W = 256
EPS = 1e-6
NEG = -1e30
C_Q0, C_K0 = 64, 67
LSE_COL = 64
LOG2E, LN2 = 1.4426950408889634, 0.6931471805599453

ADAM_LR, ADAM_B1, ADAM_B2, ADAM_EPS, ADAM_WD, ADAM_STEP = 0.001, 0.9, 0.999, 1e-08, 0.01, 10

VMEM_LIMIT_V7X = 56 * 1024 * 1024
ROW_TILES = (384, 128)
ATTN_TILES = (384, 128)
FFN_TILES = (192, 64)
FOX_HEAD_GROUP = 2
FOX_HEAD_GROUP_FWD = 4
MAX_WGRAD_BLOCK = 1408
DN_INTRA_GROUP = (6, 3, 2, 1)
DN_SCAN_GROUP = (6, 3, 2, 1)


def _pick(n, cands):
    for c in cands:
        if n % c == 0:
            return c
    raise ValueError(f"no tile of {cands} divides {n}")


def _cp(n_axes=1):
    return pltpu.CompilerParams(dimension_semantics=("arbitrary",) * n_axes, vmem_limit_bytes=VMEM_LIMIT_V7X)


def _b(x):
    return x.astype(bf16)


def _dot(a, b):
    return jnp.dot(a, b, preferred_element_type=f32)


def _dot_nt(a, b):
    return lax.dot_general(a, b, (((1,), (1,)), ((), ())), preferred_element_type=f32)


def _dot_tn(a, b):
    return lax.dot_general(a, b, (((0,), (0,)), ((), ())), preferred_element_type=f32)


def _dot_hi(a, b):
    return jnp.dot(a, b, preferred_element_type=f32, precision=HI)


def _iota(shape, dim):
    return lax.broadcasted_iota(jnp.int32, shape, dim)


def _rms(x, w):
    return x * lax.rsqrt(jnp.mean(x * x, axis=-1, keepdims=True) + EPS) * w


def _sigmoid(x):
    return jax.nn.sigmoid(x)


def _load_once(pairs, sems):
    @pl.when(pl.program_id(0) == 0)
    def _():
        cps = [pltpu.make_async_copy(src, dst, sems.at[k]) for k, (src, dst) in enumerate(pairs)]
        for cp in cps:
            cp.start()
        for cp in cps:
            cp.wait()


def _seg_layout(d_model):
    return (("fq", FOX_H * HP, bf16, "fox"), ("fk", FOX_H * HP, bf16, "fox"), ("sf", SMALL_W, f32, "fox"),
            ("fv", FOX_H * HP, bf16, "fv"),
            ("dn", 3 * DN_H * DN_D, f32, "dn"), ("sd", SMALL_W, f32, "dn"),
            ("dz", DN_H * DN_D, f32, "mix"), ("ga", d_model, f32, "mix"), ("gb", d_model, f32, "mix"))


GROUPS = ("fox", "fv", "dn", "mix")


def _group_widths(d_model):
    return [sum(wd for _, wd, _, g in _seg_layout(d_model) if g == grp) for grp in GROUPS]


def _in_proj(h0, w1, wpt):
    L, D = h0.shape
    NP = wpt.shape[0]
    TM = _pick(L, ROW_TILES)
    segs = _seg_layout(D)
    offs, o = [], 0
    for _, wd, _, _ in segs:
        offs.append(o)
        o += wd
    assert o == NP

    def body(h_ref, w1_ref, wp_hbm, xn_ref, *rest):
        outs, (wp_v, sems) = rest[:len(segs)], rest[len(segs):]
        _load_once([(wp_hbm, wp_v)], sems)
        xn = _b(_rms(h_ref[...], w1_ref[...]))
        xn_ref[...] = xn
        for o_ref, off, (_, wd, _, _) in zip(outs, offs, segs):
            o_ref[...] = _dot_nt(xn, wp_v[off:off + wd, :]).astype(o_ref.dtype)

    row = lambda wd: pl.BlockSpec((TM, wd), lambda i: (i, 0))
    return pl.pallas_call(
        body, name="in_proj", grid=(L // TM,),
        in_specs=[row(D), pl.BlockSpec((1, D), lambda i: (0, 0)), pl.BlockSpec(memory_space=pl.ANY)],
        out_specs=[row(D)] + [row(wd) for _, wd, _, _ in segs],
        out_shape=[SDS((L, D), bf16)] + [SDS((L, wd), dt) for _, wd, dt, _ in segs],
        scratch_shapes=[pltpu.VMEM((NP, D), bf16), pltpu.SemaphoreType.DMA((1,))],
        compiler_params=_cp())(h0, w1, wpt)


def _in_proj_bwd(dgroups, wpt, h0, w1, dh1, slabs):
    L, D = h0.shape
    NP = wpt.shape[0]
    TM = _pick(L, ROW_TILES)
    NT = L // TM
    widths = [g.shape[1] for g in dgroups]
    assert sum(widths) == NP
    ng, n = len(dgroups), len(slabs)

    def body(*refs):
        dg_refs, (wp_hbm, h_ref, w1_ref, dh1_ref) = refs[:ng], refs[ng:ng + 4]
        ins, (dh0_ref, acc_ref), outs = refs[ng + 4:ng + 4 + n], refs[ng + 4 + n:ng + 6 + n], refs[ng + 6 + n:ng + 6 + 2 * n]
        wp_v, sems = refs[ng + 6 + 2 * n:ng + 8 + 2 * n]
        xsems = refs[ng + 8 + 2 * n:]
        _load_once([(wp_hbm, wp_v)], sems)

        @pl.when(pl.program_id(0) == 0)
        def _():
            acc_ref[...] = jnp.zeros_like(acc_ref)
            _exchange_start(ins, outs, *xsems, gather=False)

        dxn, off = None, 0
        for g_ref, wd in zip(dg_refs, widths):
            part = _dot(g_ref[...], wp_v[off:off + wd, :])
            dxn = part if dxn is None else dxn + part
            off += wd
        _, vjp = jax.vjp(_rms, h_ref[...], w1_ref[...])
        dh0n, dw1 = vjp(dxn)
        dh0_ref[...] = dh1_ref[...] + dh0n
        acc_ref[0:1, :] += dw1

        @pl.when(pl.program_id(0) == NT - 1)
        def _():
            _exchange_wait(ins, outs, *xsems, gather=False)

    row = lambda wd: pl.BlockSpec((TM, wd), lambda i: (i, 0))
    anyspec = pl.BlockSpec(memory_space=pl.ANY)
    res = pl.pallas_call(
        body, name="in_proj_bwd", grid=(NT,),
        in_specs=[row(wd) for wd in widths] + [anyspec, row(D), pl.BlockSpec((1, D), lambda i: (0, 0)), row(D)] + [anyspec] * n,
        out_specs=[row(D), pl.BlockSpec((8, D), lambda i: (0, 0))] + [anyspec] * n,
        out_shape=[SDS((L, D), f32), SDS((8, D), f32)] + [SDS(a.shape, a.dtype) for a in slabs],
        scratch_shapes=[pltpu.VMEM((NP, D), bf16), pltpu.SemaphoreType.DMA((1,))] + _exchange_sems(n),
        compiler_params=_cp())(*dgroups, wpt, h0, w1, dh1, *slabs)
    return res[0], res[1], res[2:]


def _matmul_tn(a, b, name):
    L, R = a.shape
    C = b.shape[1]
    br = max(k for k in range(HP, MAX_WGRAD_BLOCK + 1, HP) if R % k == 0)

    def body(a_ref, b_ref, o_ref):
        o_ref[...] = _b(_dot_tn(a_ref[...], b_ref[...]))

    return pl.pallas_call(
        body, name=name, grid=(R // br,),
        in_specs=[pl.BlockSpec((L, br), lambda r: (0, r)), pl.BlockSpec((L, C), lambda r: (0, 0))],
        out_specs=pl.BlockSpec((br, C), lambda r: (r, 0)), out_shape=SDS((R, C), bf16), compiler_params=_cp())(a, b)


def _fox_prep(fq, fk, fv, sf, bias_p):
    L = fq.shape[0]
    T = HP
    NT = L // T
    W = FOX_H * HP

    def body(fq_ref, fk_ref, fv_ref, sf_ref, b_ref, qa_ref, ka_ref, va_ref, carry):
        @pl.when(pl.program_id(0) == 0)
        def _():
            carry[...] = jnp.zeros_like(carry)

        lane, row = _iota((T, HP), 1), _iota((T, HP), 0)
        logf = jnp.where(lane < FOX_H, jax.nn.log_sigmoid(sf_ref[...] + b_ref[...]), 0.0)
        c = _dot_hi((row >= lane).astype(f32), logf) + carry[...]
        carry[...] = jnp.sum(jnp.where(row == T - 1, c, 0.0), axis=0, keepdims=True)
        ones_q = jnp.where((lane >= C_K0) & (lane < C_K0 + 3), 1.0, 0.0)
        ones_k = jnp.where((lane >= C_Q0) & (lane < C_Q0 + 3), 1.0, 0.0)
        ones_v = _b(jnp.where(lane == LSE_COL, 1.0, 0.0))
        for h in range(FOX_H):
            ch = jnp.broadcast_to(jnp.sum(jnp.where(lane == h, c, 0.0), axis=1, keepdims=True), (T, HP)) * LOG2E
            c1 = _b(ch).astype(f32)
            c2 = _b(ch - c1).astype(f32)
            c3 = _b(ch - c1 - c2).astype(f32)
            cq = jnp.where(lane == C_Q0, c1, 0.0) + jnp.where(lane == C_Q0 + 1, c2, 0.0) + jnp.where(lane == C_Q0 + 2, c3, 0.0)
            ck = jnp.where(lane == C_K0, c1, 0.0) + jnp.where(lane == C_K0 + 1, c2, 0.0) + jnp.where(lane == C_K0 + 2, c3, 0.0)
            q = fq_ref[:, h * HP:(h + 1) * HP].astype(f32) * (FOX_D ** -0.5 * LOG2E)
            k = fk_ref[:, h * HP:(h + 1) * HP].astype(f32)
            qa_ref[h] = _b(q + cq + ones_q)
            ka_ref[h] = _b(k + ones_k - ck)
            va_ref[:, h * HP:(h + 1) * HP] = fv_ref[:, h * HP:(h + 1) * HP] + ones_v

    wide = pl.BlockSpec((T, W), lambda i: (i, 0))
    return pl.pallas_call(
        body, name="fox_prep", grid=(NT,),
        in_specs=[wide, wide, wide, pl.BlockSpec((T, HP), lambda i: (i, 0)), pl.BlockSpec((1, HP), lambda i: (0, 0))],
        out_specs=[pl.BlockSpec((FOX_H, T, HP), lambda i: (0, i, 0))] * 2 + [wide],
        out_shape=[SDS((FOX_H, L, HP), bf16)] * 2 + [SDS((L, W), bf16)],
        scratch_shapes=[pltpu.VMEM((1, HP), f32)], compiler_params=_cp())(fq, fk, fv, sf, bias_p)


def _fox_prep_bwd(dqa, dka, sf, bias_p):
    L = sf.shape[0]
    T = HP
    NT = L // T
    rev = lambda i: (NT - 1 - i, 0)

    W = FOX_H * HP

    def body(dq_ref, dk_ref, sf_ref, b_ref, dg_ref, db_ref, carry):
        @pl.when(pl.program_id(0) == 0)
        def _():
            carry[...] = jnp.zeros_like(carry)
            db_ref[...] = jnp.zeros_like(db_ref)

        dq, dk = dq_ref[...], dk_ref[...]
        dg_ref[:, 0:W] = _b(dq * (FOX_D ** -0.5))
        dg_ref[:, W:2 * W] = _b(dk * LN2)
        lane, row = _iota((T, HP), 1), _iota((T, HP), 0)
        dc = jnp.zeros((T, HP), f32)
        for h in range(FOX_H):
            col = jnp.sum(jnp.where(lane == C_Q0, dq[:, h * HP:(h + 1) * HP], 0.0)
                          - jnp.where(lane == C_K0, dk[:, h * HP:(h + 1) * HP], 0.0), axis=1, keepdims=True)
            dc = dc + jnp.where(lane == h, col, 0.0)
        dl = _dot_hi((row <= lane).astype(f32), dc) + carry[...]
        carry[...] = jnp.sum(jnp.where(row == 0, dl, 0.0), axis=0, keepdims=True)
        dx = jnp.where(lane < FOX_H, dl * _sigmoid(-(sf_ref[...] + b_ref[...])), 0.0)
        dg_ref[:, 2 * W:2 * W + HP] = _b(dx)
        dg_ref[:, 2 * W + HP:] = jnp.zeros((T, SMALL_W - HP), bf16)
        db_ref[0:1, :] += jnp.sum(dx, axis=0, keepdims=True)

    return pl.pallas_call(
        body, name="fox_prep_bwd", grid=(NT,),
        in_specs=[pl.BlockSpec((T, W), rev), pl.BlockSpec((T, W), rev), pl.BlockSpec((T, HP), rev), pl.BlockSpec((1, HP), lambda i: (0, 0))],
        out_specs=[pl.BlockSpec((T, 2 * W + SMALL_W), rev), pl.BlockSpec((8, HP), lambda i: (0, 0))],
        out_shape=[SDS((L, 2 * W + SMALL_W), bf16), SDS((8, HP), f32)],
        scratch_shapes=[pltpu.VMEM((1, HP), f32)], compiler_params=_cp())(dqa, dka, sf, bias_p)


def _tile_start(j, T):
    return j * T if isinstance(j, int) else pl.multiple_of(j * T, T)


def _fox_fwd(qa, ka, fv, shards):
    L = qa.shape[1]
    TQ = TK = _pick(L, ATTN_TILES)
    NQ = L // TQ
    n = len(shards)
    HG = FOX_HEAD_GROUP_FWD

    def body(q_ref, k_ref, v_ref, *rest):
        ins, o_ref, outs, sems = rest[:n], rest[n], rest[n + 1:2 * n + 1], rest[2 * n + 1:]
        h, i = pl.program_id(0), pl.program_id(1)

        @pl.when((h == 0) & (i == 0))
        def _():
            _exchange_start(ins, outs, *sems, gather=True)

        qs = [q_ref[a] for a in range(HG)]
        rowg = i * TQ + _iota((TQ, TK), 0)
        colb = _iota((TQ, TK), 1)

        def step(j, carry, masked):
            ms, accs = carry
            k0 = _tile_start(j, TK)
            ss = [_dot_nt(qs[a], k_ref[a, pl.ds(k0, TK), :]) for a in range(HG)]
            if masked:
                colg = colb + j * TK
                keep = (colg <= rowg) & (colg >= N_PAD)
                ss = [jnp.where(keep, s, NEG) for s in ss]
            m_new = [jnp.maximum(m, jnp.max(s, axis=1, keepdims=True)) for m, s in zip(ms, ss)]
            ps = [_b(jnp.exp2(s - m)) for s, m in zip(ss, m_new)]
            alphas = [jnp.exp2(m - mn) for m, mn in zip(ms, m_new)]
            accs = [al * acc + _dot(p, v_ref[pl.ds(k0, TK), a * HP:(a + 1) * HP]) for a, (al, acc, p) in enumerate(zip(alphas, accs, ps))]
            return m_new, accs

        init = ([jnp.full((TQ, 1), NEG, f32)] * HG, [jnp.zeros((TQ, HP), f32)] * HG)
        carry = step(0, init, True)
        carry = lax.fori_loop(1, i, functools.partial(step, masked=False), carry)
        ms, accs = lax.fori_loop(jnp.maximum(i, 1), i + 1, functools.partial(step, masked=True), carry)
        lane = _iota((TQ, HP), 1)
        for a in range(HG):
            l = jnp.sum(jnp.where(lane == LSE_COL, accs[a], 0.0), axis=1, keepdims=True)
            o_ref[:, a * HP:(a + 1) * HP] = jnp.where(lane == LSE_COL, ms[a] + jnp.log2(l), accs[a] / l)

        @pl.when((h == FOX_H // HG - 1) & (i == NQ - 1))
        def _():
            _exchange_wait(ins, outs, *sems, gather=True)

    anyspec = pl.BlockSpec(memory_space=pl.ANY)
    res = pl.pallas_call(
        body, name="fox_fwd", grid=(FOX_H // HG, NQ),
        in_specs=[pl.BlockSpec((HG, TQ, HP), lambda h, i: (h, i, 0)), pl.BlockSpec((HG, L, HP), lambda h, i: (h, 0, 0)),
                  pl.BlockSpec((L, HG * HP), lambda h, i: (0, h))] + [anyspec] * n,
        out_specs=[pl.BlockSpec((TQ, HG * HP), lambda h, i: (i, h))] + [anyspec] * n,
        out_shape=[SDS((L, FOX_H * HP), f32)] + [SDS((N_DEV,) + a.shape, a.dtype) for a in shards],
        scratch_shapes=_exchange_sems(n), compiler_params=_cp(2))(qa, ka, fv, *shards)
    return res[0], res[1:]


def _fox_bwd(qa, ka, fv, op, dop, slabs):
    L = qa.shape[1]
    TQ = TK = _pick(L, ATTN_TILES)
    NQ = L // TQ
    n = len(slabs)
    HG = FOX_HEAD_GROUP

    def body(q_ref, k_ref, v_ref, o_ref, do_ref, *rest):
        ins, (dq_ref, dk_ref, dv_ref), outs = rest[:n], rest[n:n + 3], rest[n + 3:2 * n + 3]
        lse_s, delta_s = rest[2 * n + 3:2 * n + 5]
        sems = rest[2 * n + 5:]
        h, j = pl.program_id(0), pl.program_id(1)
        cols = [slice(a * HP, (a + 1) * HP) for a in range(HG)]

        @pl.when((h == 0) & (j == 0))
        def _():
            _exchange_start(ins, outs, *sems, gather=False)

        lane = _iota((TQ, HP), 1)

        @pl.when(j == 0)
        def _():
            dq_ref[...] = jnp.zeros_like(dq_ref)
            for t in range(NQ):
                r = slice(t * TQ, (t + 1) * TQ)
                for a in range(HG):
                    o, do = o_ref[r, cols[a]], do_ref[r, cols[a]]
                    lse_s[a, r, :] = jnp.sum(jnp.where(lane == LSE_COL, o, 0.0), axis=1, keepdims=True)
                    delta_s[a, r, :] = jnp.sum(jnp.where(lane < FOX_D, o * do, 0.0), axis=1, keepdims=True)

        kts = [k_ref[a] for a in range(HG)]
        vts = [v_ref[:, cols[a]] for a in range(HG)]
        colg = j * TK + _iota((TQ, TK), 1)
        rowb = _iota((TQ, TK), 0)

        def step(i, carry, masked):
            dks, dvs = carry
            r0 = _tile_start(i, TQ)
            rows = pl.ds(r0, TQ)
            qs = [q_ref[a, rows, :] for a in range(HG)]
            ps = [jnp.exp2(_dot_nt(q, kt) - lse_s[a, rows, :]) for a, (q, kt) in enumerate(zip(qs, kts))]
            if masked:
                keep = (colg <= rowb + i * TQ) & (colg >= N_PAD)
                ps = [jnp.where(keep, p, 0.0) for p in ps]
            dobs = [_b(do_ref[rows, cols[a]]) for a in range(HG)]
            dvs = [dv + _dot_tn(_b(p), dob) for dv, p, dob in zip(dvs, ps, dobs)]
            dss = [_b(p * (_dot_nt(dob, vt) - delta_s[a, rows, :])) for a, (p, dob, vt) in enumerate(zip(ps, dobs, vts))]
            for a in range(HG):
                dq_ref[rows, cols[a]] += _dot(dss[a], kts[a])
            dks = [dk + _dot_tn(ds, q) for dk, ds, q in zip(dks, dss, qs)]
            return dks, dvs

        zeros = [jnp.zeros((TK, HP), f32)] * HG
        carry = step(j, (zeros, zeros), True)
        split = jnp.where(j == 0, NQ, j + 1)
        carry = lax.fori_loop(j + 1, split, functools.partial(step, masked=True), carry)
        dks, dvs = lax.fori_loop(split, NQ, functools.partial(step, masked=False), carry)
        for a in range(HG):
            dk_ref[:, cols[a]] = dks[a]
            dv_ref[:, cols[a]] = _b(dvs[a])

        @pl.when((h == FOX_H // HG - 1) & (j == NQ - 1))
        def _():
            _exchange_wait(ins, outs, *sems, gather=False)

    head = pl.BlockSpec((L, HG * HP), lambda h, j: (0, h))
    tile = pl.BlockSpec((TK, HG * HP), lambda h, j: (j, h))
    anyspec = pl.BlockSpec(memory_space=pl.ANY)
    res = pl.pallas_call(
        body, name="fox_bwd", grid=(FOX_H // HG, L // TK),
        in_specs=[pl.BlockSpec((HG, L, HP), lambda h, j: (h, 0, 0)), pl.BlockSpec((HG, TK, HP), lambda h, j: (h, j, 0)), tile, head, head]
        + [anyspec] * n,
        out_specs=[head, tile, tile] + [anyspec] * n,
        out_shape=[SDS((L, FOX_H * HP), f32), SDS((L, FOX_H * HP), f32), SDS((L, FOX_H * HP), bf16)] + [SDS(a.shape, a.dtype) for a in slabs],
        scratch_shapes=[pltpu.VMEM((HG, L, 1), f32), pltpu.VMEM((HG, L, 1), f32)] + _exchange_sems(n),
        compiler_params=_cp(2))(qa, ka, fv, op, dop, *slabs)
    return res[:3], res[3:]


def _dn_post(y, sd, alog_p, dt_p, valid):
    a = y * _sigmoid(y)
    W = DN_H * DN_D
    heads = []
    for part, scale in ((0, DN_D ** -0.5), (1, 1.0)):
        for h in range(DN_H):
            xh = a[:, part * W + h * DN_D:part * W + (h + 1) * DN_D]
            heads.append(xh * lax.rsqrt(jnp.sum(xh * xh, axis=-1, keepdims=True) + EPS) * scale)
    q = jnp.concatenate(heads[:DN_H], axis=1)
    k = jnp.concatenate(heads[DN_H:], axis=1)
    v = a[:, 2 * W:3 * W]
    lane = _iota(sd.shape, 1)
    beta = _sigmoid(sd) * valid
    g = -jnp.exp(alog_p) * jax.nn.softplus(sd + dt_p) * valid
    bg = jnp.where(lane < DN_H, beta, jnp.where(lane < 2 * DN_H, g, 0.0))
    return q, k, v, bg


def _conv_fwd(ext_ref, cw_ref, TM):
    y = cw_ref[0:1, :] * ext_ref[8 - (CONV_K - 1):8 - (CONV_K - 1) + TM, :]
    for i in range(1, CONV_K):
        o = 8 - (CONV_K - 1) + i
        y = y + cw_ref[i:i + 1, :] * ext_ref[o:o + TM, :]
    return y


def _dn_prep(dn, sd, cw, alog_p, dt_p):
    L, W3 = dn.shape
    TM = _pick(L, ROW_TILES)
    W = DN_H * DN_D

    def body(dn_ref, halo_ref, sd_ref, cw_ref, al_ref, dt_ref, q_ref, k_ref, v_ref, bg_ref, ext):
        i = pl.program_id(0)
        ext[0:8, :] = jnp.where(i == 0, 0.0, halo_ref[...])
        ext[8:, :] = dn_ref[...]
        y = _conv_fwd(ext, cw_ref, TM)
        valid = ((i * TM + _iota((TM, 1), 0)) >= N_PAD).astype(f32)
        q, k, v, bg = _dn_post(y, sd_ref[...], al_ref[...], dt_ref[...], valid)
        q_ref[...], k_ref[...], v_ref[...], bg_ref[...] = q, k, v, bg

    row = lambda wd: pl.BlockSpec((TM, wd), lambda i: (i, 0))
    vec = pl.BlockSpec((1, HP), lambda i: (0, 0))
    return pl.pallas_call(
        body, name="dn_prep", grid=(L // TM,),
        in_specs=[row(W3), pl.BlockSpec((8, W3), lambda i: (jnp.maximum(i * (TM // 8) - 1, 0), 0)), row(HP),
                  pl.BlockSpec((CONV_K, W3), lambda i: (0, 0)), vec, vec],
        out_specs=[row(W), row(W), row(W), row(HP)],
        out_shape=[SDS((L, W), f32)] * 3 + [SDS((L, HP), f32)],
        scratch_shapes=[pltpu.VMEM((TM + 8, W3), f32)], compiler_params=_cp())(dn, dn, sd, cw, alog_p, dt_p)


def _dn_prep_bwd(dn, sd, cw, alog_p, dt_p, dq, dk, dv, dbg):
    L, W3 = dn.shape
    TM = _pick(L, ROW_TILES)
    NT = L // TM
    W = DN_H * DN_D

    def body(dn_ref, halo_ref, sd_ref, cw_ref, al_ref, dt_ref, dq_ref, dk_ref, dv_ref, dbg_ref,
             dg_ref, dcw_ref, dp_ref, ext, dyp, carry):
        i = pl.program_id(0)
        t = NT - 1 - i

        @pl.when(i == 0)
        def _():
            carry[...] = jnp.zeros_like(carry)
            dcw_ref[...] = jnp.zeros_like(dcw_ref)
            dp_ref[...] = jnp.zeros_like(dp_ref)
            dyp[...] = jnp.zeros_like(dyp)

        ext[0:8, :] = jnp.where(t == 0, 0.0, halo_ref[...])
        ext[8:, :] = dn_ref[...]
        y = _conv_fwd(ext, cw_ref, TM)
        valid = ((t * TM + _iota((TM, 1), 0)) >= N_PAD).astype(f32)
        _, vjp = jax.vjp(functools.partial(_dn_post, valid=valid), y, sd_ref[...], al_ref[...], dt_ref[...])
        dy, dsd, dal, ddt = vjp((dq_ref[...], dk_ref[...], dv_ref[...], dbg_ref[...]))
        dg_ref[:, W3:W3 + HP] = _b(dsd)
        dg_ref[:, W3 + HP:] = jnp.zeros((TM, SMALL_W - HP), bf16)
        dp_ref[0:1, :] += dal
        dp_ref[1:2, :] += ddt
        dyp[8:8 + TM, :] = dy
        o0 = CONV_K - 1
        dext = cw_ref[0:1, :] * dyp[o0:o0 + TM + 8, :]
        for k in range(1, CONV_K):
            dext = dext + cw_ref[k:k + 1, :] * dyp[o0 - k:o0 - k + TM + 8, :]
        for k in range(CONV_K):
            o = 8 - (CONV_K - 1) + k
            dcw_ref[k:k + 1, :] += jnp.sum(dy * ext[o:o + TM, :], axis=0, keepdims=True)
        dg_ref[:, 0:W3] = _b(jnp.concatenate([dext[8:TM, :], dext[TM:TM + 8, :] + carry[...]], axis=0))
        carry[...] = dext[0:8, :]

    row = lambda wd: pl.BlockSpec((TM, wd), lambda i: (NT - 1 - i, 0))
    vec = pl.BlockSpec((1, HP), lambda i: (0, 0))
    return pl.pallas_call(
        body, name="dn_prep_bwd", grid=(NT,),
        in_specs=[row(W3), pl.BlockSpec((8, W3), lambda i: (jnp.maximum((NT - 1 - i) * (TM // 8) - 1, 0), 0)), row(HP),
                  pl.BlockSpec((CONV_K, W3), lambda i: (0, 0)), vec, vec, row(W), row(W), row(W), row(HP)],
        out_specs=[row(W3 + SMALL_W), pl.BlockSpec((8, W3), lambda i: (0, 0)), pl.BlockSpec((8, HP), lambda i: (0, 0))],
        out_shape=[SDS((L, W3 + SMALL_W), bf16), SDS((8, W3), f32), SDS((8, HP), f32)],
        scratch_shapes=[pltpu.VMEM((TM + 8, W3), f32), pltpu.VMEM((TM + 16, W3), f32), pltpu.VMEM((8, W3), f32)],
        compiler_params=_cp())(dn, dn, sd, cw, alog_p, dt_p, dq, dk, dv, dbg)


def _split2(x):
    hi = _b(x)
    return hi, _b(x - hi.astype(f32))


def _split3(x):
    hi = _b(x)
    r = x - hi.astype(f32)
    mid = _b(r)
    return hi, mid, _b(r - mid.astype(f32))


def _x3(a, b, dot):
    (a1, a2), (b1, b2) = _split2(a), _split2(b)
    return dot(a1, b1) + (dot(a1, b2) + dot(a2, b1))


@jax.custom_vjp
def _dot_x3(a, b):
    return _x3(a, b, _dot)


_dot_x3.defvjp(lambda a, b: (_x3(a, b, _dot), (a, b)), lambda res, g: (_x3(g, res[1], _dot_nt), _x3(res[0], g, _dot_tn)))


def _exact3(m, x, dot):
    x1, x2, x3 = _split3(x)
    return dot(m, x1) + (dot(m, x2) + dot(m, x3))


def _tri_ones(C, lower):
    row, col = _iota((C, C), 0), _iota((C, C), 1)
    return _b(((row >= col) if lower else (row <= col)).astype(f32))


@jax.custom_vjp
def _chunk_cumsum(x):
    return _exact3(_tri_ones(x.shape[0], True), x, _dot)


_chunk_cumsum.defvjp(lambda x: (_exact3(_tri_ones(x.shape[0], True), x, _dot), None),
                     lambda _, g: (_exact3(_tri_ones(g.shape[0], False), g, _dot),))


def _mxu_transpose(x):
    C = x.shape[0]
    eye = _b((_iota((C, C), 0) == _iota((C, C), 1)).astype(f32))
    return _exact3(eye, x, lambda m, part: _dot_tn(part, m))


@jax.custom_vjp
def _transpose_exact(x):
    return _mxu_transpose(x)


_transpose_exact.defvjp(lambda x: (_mxu_transpose(x), None), lambda _, g: (_mxu_transpose(g),))


def _unit_lower_inverses(lows):
    C = lows[0].shape[0]
    P = jnp.stack(lows)
    X = (_iota((C, C), 0) == _iota((C, C), 1)).astype(f32)[None] - P
    bdot = functools.partial(_x3, dot=lambda a, b: jnp.einsum("bij,bjk->bik", a, b, preferred_element_type=f32))
    for _ in range(5):
        P = bdot(P, P)
        X = X + bdot(X, P)
    return [X[i] for i in range(len(lows))]


@jax.custom_vjp
def _inverse_given(low, X):
    return X


def _inverse_given_bwd(X, g):
    return -_x3(_x3(X, g, _dot_tn), X, _dot_nt), jnp.zeros_like(X)


_inverse_given.defvjp(lambda low, X: (X, X), _inverse_given_bwd)


def _dn_intra_pre(q, k, v, bg):
    C = DN_C
    row, col = _iota((C, C), 0), _iota((C, C), 1)
    tri = row >= col
    G = _chunk_cumsum(bg)
    GT = _transpose_exact(G)
    lane = _iota((C, HP), 1)
    rowt = _iota((HP, C), 0)
    last = _iota((C, 1), 0) == C - 1
    heads = []
    for h in range(DN_H):
        beta = jnp.sum(jnp.where(lane == h, bg, 0.0), axis=1, keepdims=True)
        gcol = jnp.sum(jnp.where(lane == DN_H + h, G, 0.0), axis=1, keepdims=True)
        grow = jnp.sum(jnp.where(rowt == DN_H + h, GT, 0.0), axis=0, keepdims=True)
        glast = jnp.sum(jnp.where(last, gcol, 0.0), axis=0, keepdims=True)
        decay = jnp.exp(jnp.where(tri, gcol - grow, NEG))
        qh, kh, vh = (t[:, h * DN_D:(h + 1) * DN_D] for t in (q, k, v))
        kb = kh * beta
        low = jnp.where(row > col, _dot_nt(_b(kb), _b(kh)) * decay, 0.0)
        heads.append((beta, gcol, glast, decay, qh, kh, vh, kb, low))
    return heads


def _dn_intra_post(heads, xs):
    lane1 = _iota((1, HP), 1)
    us, ws, qds, kds, attns = [], [], [], [], []
    glrow = jnp.zeros((1, HP), f32)
    for h, ((beta, gcol, glast, decay, qh, kh, vh, kb, _), X) in enumerate(zip(heads, xs)):
        eg = jnp.exp(gcol)
        us.append(_dot_x3(X, vh * beta))
        ws.append(_dot_x3(X, kb * eg))
        attns.append(_dot_nt(_b(qh), _b(kh)) * decay)
        qds.append(qh * eg)
        kds.append(kh * jnp.exp(glast - gcol))
        glrow = glrow + jnp.where(lane1 == h, glast, 0.0)
    cat = lambda xs_: jnp.concatenate(xs_, axis=1)
    return cat(us), cat(ws), cat(qds), cat(kds), cat(attns), glrow, cat(list(xs))


def _dn_intra_group(q, k, v, bg, xs):
    G = q.shape[0] // DN_C
    rows = [slice(j * DN_C, (j + 1) * DN_C) for j in range(G)]
    pre = [_dn_intra_pre(q[r, :], k[r, :], v[r, :], bg[r, :]) for r in rows]
    inv = [[_inverse_given(hd[-1], x) for hd, x in zip(heads, xj)] for heads, xj in zip(pre, xs)]
    post = [_dn_intra_post(heads, xj) for heads, xj in zip(pre, inv)]
    return tuple(jnp.concatenate([p[i] for p in post], axis=0) for i in range(5)) + (tuple(p[5] for p in post),)


def _lane_pick(rowvec, h):
    return jnp.sum(jnp.where(_iota(rowvec.shape, 1) == h, rowvec, 0.0), axis=1, keepdims=True)


def _dn_intra(q, k, v, bg):
    L, W = q.shape
    NC = L // DN_C
    G = _pick(NC, DN_INTRA_GROUP)
    R = G * DN_C
    WA = DN_H * DN_C

    def body(q_ref, k_ref, v_ref, bg_ref, u_ref, w_ref, qd_ref, kd_ref, at_ref, gl_ref, x_ref):
        rows = [slice(j * DN_C, (j + 1) * DN_C) for j in range(G)]
        pre = [_dn_intra_pre(q_ref[r, :], k_ref[r, :], v_ref[r, :], bg_ref[r, :]) for r in rows]
        inv = _unit_lower_inverses([hd[-1] for heads in pre for hd in heads])
        for j, r in enumerate(rows):
            u, w, qd, kd, at, gl, xs = _dn_intra_post(pre[j], inv[j * DN_H:(j + 1) * DN_H])
            u_ref[r, :], x_ref[r, :] = u, xs
            w_ref[r, :], qd_ref[r, :], kd_ref[r, :], at_ref[r, :] = _b(w), _b(qd), _b(kd), _b(at)
            gl_ref[j] = gl

    row = lambda wd: pl.BlockSpec((R, wd), lambda n: (n, 0))
    return pl.pallas_call(
        body, name="dn_intra", grid=(NC // G,),
        in_specs=[row(W), row(W), row(W), row(HP)],
        out_specs=[row(W), row(W), row(W), row(W), row(WA), pl.BlockSpec((G, 1, HP), lambda n: (n, 0, 0)), row(WA)],
        out_shape=[SDS((L, W), f32), SDS((L, W), bf16), SDS((L, W), bf16), SDS((L, W), bf16), SDS((L, WA), bf16), SDS((NC, 1, HP), f32),
                   SDS((L, WA), f32)],
        compiler_params=_cp())(q, k, v, bg)


def _dn_scan(u, w, qd, kd, at, gl):
    L, W = u.shape
    NC = L // DN_C
    G = _pick(NC, DN_SCAN_GROUP)
    R = G * DN_C

    def body(u_ref, w_ref, qd_ref, kd_ref, at_ref, gl_ref, o_ref, vn_ref, s_ref, S):
        @pl.when(pl.program_id(0) == 0)
        def _():
            S[...] = jnp.zeros_like(S)

        for j in range(G):
            r = slice(j * DN_C, (j + 1) * DN_C)
            glrow = gl_ref[j]
            for h in range(DN_H):
                c = slice(h * DN_D, (h + 1) * DN_D)
                Sh = S[h]
                s_ref[j, h] = Sh
                Sb = _b(Sh)
                vb = _b(u_ref[r, c] - _dot(w_ref[r, c], Sb))
                vn_ref[r, c] = vb
                o_ref[r, c] = _dot(qd_ref[r, c], Sb) + _dot(at_ref[r, h * DN_C:(h + 1) * DN_C], vb)
                S[h] = Sh * jnp.exp(_lane_pick(glrow, h)) + _dot_tn(kd_ref[r, c], vb)

    row = lambda wd: pl.BlockSpec((R, wd), lambda n: (n, 0))
    return pl.pallas_call(
        body, name="dn_scan", grid=(NC // G,),
        in_specs=[row(W), row(W), row(W), row(W), row(DN_H * DN_C), pl.BlockSpec((G, 1, HP), lambda n: (n, 0, 0))],
        out_specs=[row(W), row(W), pl.BlockSpec((G, DN_H, DN_D, DN_D), lambda n: (n, 0, 0, 0))],
        out_shape=[SDS((L, W), f32), SDS((L, W), bf16), SDS((NC, DN_H, DN_D, DN_D), f32)],
        scratch_shapes=[pltpu.VMEM((DN_H, DN_D, DN_D), f32)], compiler_params=_cp())(u, w, qd, kd, at, gl)


def _dn_scan_bwd(do, w, qd, kd, at, gl):
    L, W = do.shape
    NC = L // DN_C
    G = _pick(NC, DN_SCAN_GROUP)
    R = G * DN_C
    NS = NC // G

    def body(do_ref, w_ref, qd_ref, kd_ref, at_ref, gl_ref, dvn_ref, ds_ref, dS):
        @pl.when(pl.program_id(0) == 0)
        def _():
            dS[...] = jnp.zeros_like(dS)

        for j in reversed(range(G)):
            r = slice(j * DN_C, (j + 1) * DN_C)
            glrow = gl_ref[j]
            for h in range(DN_H):
                c = slice(h * DN_D, (h + 1) * DN_D)
                dSo = dS[h]
                ds_ref[j, h] = dSo
                dob = _b(do_ref[r, c])
                dvn = _dot_tn(at_ref[r, h * DN_C:(h + 1) * DN_C], dob) + _dot(kd_ref[r, c], _b(dSo))
                dvn_ref[r, c] = dvn
                dS[h] = _dot_tn(qd_ref[r, c], dob) + dSo * jnp.exp(_lane_pick(glrow, h)) - _dot_tn(w_ref[r, c], _b(dvn))

    row = lambda wd: pl.BlockSpec((R, wd), lambda n: (NS - 1 - n, 0))
    return pl.pallas_call(
        body, name="dn_scan_bwd", grid=(NS,),
        in_specs=[row(W), row(W), row(W), row(W), row(DN_H * DN_C), pl.BlockSpec((G, 1, HP), lambda n: (NS - 1 - n, 0, 0))],
        out_specs=[row(W), pl.BlockSpec((G, DN_H, DN_D, DN_D), lambda n: (NS - 1 - n, 0, 0, 0))],
        out_shape=[SDS((L, W), f32), SDS((NC, DN_H, DN_D, DN_D), f32)],
        scratch_shapes=[pltpu.VMEM((DN_H, DN_D, DN_D), f32)], compiler_params=_cp())(do, w, qd, kd, at, gl)


def _dn_intra_bwd(q, k, v, bg, xinv, do, vn, dvn, states, dstates):
    L, W = q.shape
    NC = L // DN_C
    G = _pick(NC, DN_INTRA_GROUP)
    R = G * DN_C

    def body(q_ref, k_ref, v_ref, bg_ref, x_ref, do_ref, vn_ref, dvn_ref, s_ref, ds_ref, dq_ref, dk_ref, dv_ref, dbg_ref):
        lane1 = _iota((1, HP), 1)
        rows = [slice(j * DN_C, (j + 1) * DN_C) for j in range(G)]
        xs = [[x_ref[r, h * DN_C:(h + 1) * DN_C] for h in range(DN_H)] for r in rows]
        outs, vjp = jax.vjp(functools.partial(_dn_intra_group, xs=xs), q_ref[...], k_ref[...], v_ref[...], bg_ref[...])
        dws, dqds, dkds, dats, dgls = [], [], [], [], []
        for j, r in enumerate(rows):
            dw, dqd, dkd, dat = [], [], [], []
            dgl = jnp.zeros((1, HP), f32)
            for h in range(DN_H):
                c = slice(h * DN_D, (h + 1) * DN_D)
                Sh, dSo = s_ref[j, h], ds_ref[j, h]
                Sb, dob, vb = _b(Sh), _b(do_ref[r, c]), vn_ref[r, c]
                dw.append(-_dot_nt(_b(dvn_ref[r, c]), Sb))
                dqd.append(_dot_nt(dob, Sb))
                dat.append(_dot_nt(dob, vb))
                dkd.append(_dot_nt(vb, _b(dSo)))
                dcd = jnp.sum(jnp.sum(Sh * dSo, axis=1, keepdims=True), axis=0, keepdims=True)
                dgl = dgl + jnp.where(lane1 == h, dcd * jnp.exp(_lane_pick(outs[5][j], h)), 0.0)
            cat = lambda xs_: jnp.concatenate(xs_, axis=1)
            dws.append(cat(dw)), dqds.append(cat(dqd)), dkds.append(cat(dkd)), dats.append(cat(dat)), dgls.append(dgl)
        cat0 = lambda xs_: jnp.concatenate(xs_, axis=0)
        dq, dk, dv, dbg = vjp((dvn_ref[...], cat0(dws), cat0(dqds), cat0(dkds), cat0(dats), tuple(dgls)))
        dq_ref[...], dk_ref[...], dv_ref[...], dbg_ref[...] = dq, dk, dv, dbg

    row = lambda wd: pl.BlockSpec((R, wd), lambda n: (n, 0))
    st = pl.BlockSpec((G, DN_H, DN_D, DN_D), lambda n: (n, 0, 0, 0))
    return pl.pallas_call(
        body, name="dn_intra_bwd", grid=(NC // G,),
        in_specs=[row(W), row(W), row(W), row(HP), row(DN_H * DN_C), row(W), row(W), row(W), st, st],
        out_specs=[row(W), row(W), row(W), row(HP)],
        out_shape=[SDS((L, W), f32)] * 3 + [SDS((L, HP), f32)],
        compiler_params=_cp())(q, k, v, bg, xinv, do, vn, dvn, states, dstates)


def _dn_normgate(oraw, dz, wn):
    outs = []
    for h in range(DN_H):
        sl = slice(h * DN_D, (h + 1) * DN_D)
        z = dz[:, sl]
        outs.append(_rms(oraw[:, sl], wn) * (z * _sigmoid(z)))
    return jnp.concatenate(outs, axis=1)


def _mix_fwd(op, oraw, dz, ga, gb, h0, wn, wbf, wbd, wo):
    L, D = h0.shape
    TM = _pick(L, ROW_TILES)

    def body(op_ref, or_ref, dz_ref, ga_ref, gb_ref, h0_ref, wn_ref, wbf_ref, wbd_ref, wo_ref, h1_ref):
        pf = _dot(_b(op_ref[...]), wbf_ref[...])
        pd = _dot(_b(_dn_normgate(or_ref[...], dz_ref[...], wn_ref[...])), wbd_ref[...])
        y = _sigmoid(ga_ref[...]) * pf + _sigmoid(gb_ref[...]) * pd
        h1_ref[...] = h0_ref[...] + _dot(_b(y), wo_ref[...])

    row = lambda wd: pl.BlockSpec((TM, wd), lambda i: (i, 0))
    full = lambda a: pl.BlockSpec(a.shape, lambda i: (0, 0))
    return pl.pallas_call(
        body, name="mix_fwd", grid=(L // TM,),
        in_specs=[row(op.shape[1]), row(oraw.shape[1]), row(dz.shape[1]), row(D), row(D), row(D), full(wn), full(wbf), full(wbd), full(wo)],
        out_specs=row(D), out_shape=SDS((L, D), f32), compiler_params=_cp())(op, oraw, dz, ga, gb, h0, wn, wbf, wbd, wo)


def _mix_bwd(dh1, op, oraw, dz, ga, gb, wn, wbf, wbd, wo):
    L, D = dh1.shape
    TM = _pick(L, ROW_TILES)
    WF, WD = op.shape[1], oraw.shape[1]

    def body(dh1_ref, op_ref, or_ref, dz_ref, ga_ref, gb_ref, wn_ref, wbf_ref, wbd_ref, wo_ref,
             dop_ref, dor_ref, dg_ref, af_ref, ad_ref, dpf_ref, dpd_ref, y_ref, dmix_ref, acc_ref):
        @pl.when(pl.program_id(0) == 0)
        def _():
            acc_ref[...] = jnp.zeros_like(acc_ref)

        af = _b(op_ref[...])
        ad, vjp = jax.vjp(_dn_normgate, or_ref[...], dz_ref[...], wn_ref[...])
        adb = _b(ad)
        pf, pd = _dot(af, wbf_ref[...]), _dot(adb, wbd_ref[...])
        sa, sb = _sigmoid(ga_ref[...]), _sigmoid(gb_ref[...])
        dmix = _b(dh1_ref[...])
        dy = _dot_nt(dmix, wo_ref[...])
        dpf, dpd = _b(dy * sa), _b(dy * sb)
        dor, ddz, dwn = vjp(_dot_nt(dpd, wbd_ref[...]))
        dop_ref[...] = _dot_nt(dpf, wbf_ref[...])
        dor_ref[...] = dor
        dg_ref[:, 0:WD] = _b(ddz)
        dg_ref[:, WD:WD + D] = _b(dy * pf * sa * (1.0 - sa))
        dg_ref[:, WD + D:] = _b(dy * pd * sb * (1.0 - sb))
        af_ref[...], ad_ref[...], y_ref[...] = af, adb, _b(sa * pf + sb * pd)
        dpf_ref[...], dpd_ref[...], dmix_ref[...] = dpf, dpd, dmix
        acc_ref[0:1, :] += dwn

    row = lambda wd: pl.BlockSpec((TM, wd), lambda i: (i, 0))
    full = lambda a: pl.BlockSpec(a.shape, lambda i: (0, 0))
    return pl.pallas_call(
        body, name="mix_bwd", grid=(L // TM,),
        in_specs=[row(D), row(WF), row(WD), row(WD), row(D), row(D), full(wn), full(wbf), full(wbd), full(wo)],
        out_specs=[row(WF), row(WD), row(WD + 2 * D), row(WF), row(WD), row(D), row(D), row(D), row(D),
                   pl.BlockSpec((8, HP), lambda i: (0, 0))],
        out_shape=[SDS((L, WF), f32), SDS((L, WD), f32), SDS((L, WD + 2 * D), bf16), SDS((L, WF), bf16), SDS((L, WD), bf16),
                   SDS((L, D), bf16), SDS((L, D), bf16), SDS((L, D), bf16), SDS((L, D), bf16), SDS((8, HP), f32)],
        compiler_params=_cp())(dh1, op, oraw, dz, ga, gb, wn, wbf, wbd, wo)


def _ffn_fwd_bwd(h1, tgt, w2, wf, wgt, wut, wd):
    L, D = h1.shape
    F = wd.shape[0]
    TM = _pick(L, FFN_TILES)

    def body(h_ref, t_ref, w2_ref, wf_ref, wg_hbm, wu_hbm, wd_hbm,
             dh1_ref, xn_ref, dg_ref, du_ref, act_ref, dh2_ref, acc_ref, wg_v, wu_v, wd_v, sems):
        i = pl.program_id(0)
        _load_once([(wg_hbm, wg_v), (wu_hbm, wu_v), (wd_hbm, wd_v)], sems)

        @pl.when(i == 0)
        def _():
            acc_ref[...] = jnp.zeros_like(acc_ref)

        h1v = h_ref[...]
        xn2, vjp2 = jax.vjp(_rms, h1v, w2_ref[...])
        xb = _b(xn2)
        g, u = _dot_nt(xb, wg_v[...]), _dot_nt(xb, wu_v[...])
        sg = _sigmoid(g)
        ab = _b(g * sg * u)
        h2 = h1v + _dot(ab, wd_v[...])
        out, vjpf = jax.vjp(_rms, h2, wf_ref[...])
        valid = (i * TM + _iota((TM, 1), 0)) >= PREFIX
        diff = jnp.where(valid, out - t_ref[...], 0.0)
        loss = 0.5 * jnp.sum(jnp.sum(diff * diff, axis=1, keepdims=True), axis=0, keepdims=True) / D
        dh2, dwf = vjpf(diff * (1.0 / D))
        dh2b = _b(dh2)
        dact = _dot_nt(dh2b, wd_v[...])
        dgb = _b(dact * u * (sg * (1.0 + g * (1.0 - sg))))
        dub = _b(dact * (g * sg))
        dh1n, dw2 = vjp2(_dot(dgb, wg_v[...]) + _dot(dub, wu_v[...]))
        dh1_ref[...] = dh2 + dh1n
        xn_ref[...], dg_ref[...], du_ref[...], act_ref[...], dh2_ref[...] = xb, dgb, dub, ab, dh2b
        acc_ref[0:1, :] += dw2
        acc_ref[1:2, :] += dwf
        acc_ref[2:3, :] += jnp.broadcast_to(loss, (1, D))

    row = lambda wd_: pl.BlockSpec((TM, wd_), lambda i: (i, 0))
    vec = pl.BlockSpec((1, D), lambda i: (0, 0))
    anyspec = pl.BlockSpec(memory_space=pl.ANY)
    return pl.pallas_call(
        body, name="ffn_fwd_bwd", grid=(L // TM,),
        in_specs=[row(D), row(D), vec, vec, anyspec, anyspec, anyspec],
        out_specs=[row(D), row(D), row(F), row(F), row(F), row(D), pl.BlockSpec((8, D), lambda i: (0, 0))],
        out_shape=[SDS((L, D), f32), SDS((L, D), bf16), SDS((L, F), bf16), SDS((L, F), bf16), SDS((L, F), bf16), SDS((L, D), bf16),
                   SDS((8, D), f32)],
        scratch_shapes=[pltpu.VMEM((F, D), bf16), pltpu.VMEM((F, D), bf16), pltpu.VMEM((F, D), bf16), pltpu.SemaphoreType.DMA((3,))],
        compiler_params=_cp())(h1, tgt, w2, wf, wgt, wut, wd)


def _pad_lanes(v, n=HP):
    return jnp.pad(v.astype(f32), ((0, 0), (0, n - v.shape[1])))


def _pack_w_in(wt_full):
    D = wt_full.shape[1]
    FW, DW = FOX_H * FOX_D, DN_H * DN_D
    o = 0
    parts = {}
    for name, wd in (("fq", FW), ("fk", FW), ("fv", FW), ("fl", FOX_H), ("dn", 3 * DW), ("ba", 2 * DN_H), ("dz", DW), ("ga", D), ("gb", D)):
        parts[name] = wt_full[o:o + wd]
        o += wd
    assert o == wt_full.shape[0]
    heads = lambda w: jnp.pad(w.reshape(FOX_H, FOX_D, D), ((0, 0), (0, HP - FOX_D), (0, 0))).reshape(FOX_H * HP, D)
    small = lambda w: jnp.pad(w, ((0, SMALL_W - w.shape[0]), (0, 0)))
    packed = dict(fq=heads(parts["fq"]), fk=heads(parts["fk"]), fv=heads(parts["fv"]), sf=small(parts["fl"]), sd=small(parts["ba"]),
                  dn=parts["dn"], dz=parts["dz"], ga=parts["ga"], gb=parts["gb"])
    return jnp.concatenate([packed[name] for name, _, _, _ in _seg_layout(D)], axis=0)


def _unpack_w_in(groups, d_model):
    D = groups[0].shape[1]
    FW = FOX_H * FOX_D
    segs = {}
    for grp, g in zip(GROUPS, groups):
        o = 0
        for name, wd, _, sg in _seg_layout(d_model):
            if sg == grp:
                segs[name] = g[o:o + wd]
                o += wd
    heads = lambda g: g.reshape(FOX_H, HP, D)[:, :FOX_D].reshape(FW, D)
    return jnp.concatenate([heads(segs["fq"]), heads(segs["fk"]), heads(segs["fv"]), segs["sf"][:FOX_H], segs["dn"],
                            segs["sd"][:2 * DN_H], segs["dz"], segs["ga"], segs["gb"]], axis=0)


def _local_step(x, tgt, meta, w1, w_in_t, fbias, cw, alog, dtb, wn, w2, wf, late_shards):
    T, D = x.shape
    h0 = jnp.concatenate([jnp.zeros((N_PAD, D), f32), meta, x], axis=0)
    tgt_p = jnp.concatenate([jnp.zeros((PREFIX, D), f32), tgt], axis=0)
    wp = _pack_w_in(w_in_t)
    bias_p, alog_p, dt_p = _pad_lanes(fbias), _pad_lanes(jnp.pad(alog, ((0, 0), (DN_H, 0)))), _pad_lanes(jnp.pad(dtb, ((0, 0), (DN_H, 0))))

    xn, fq, fk, sf, fv, dn, sd, dz, ga, gb = _in_proj(h0, w1, wp)
    qa, ka, va = _fox_prep(fq, fk, fv, sf, bias_p)
    op, gathered = _fox_fwd(qa, ka, va, [late_shards[n] for n in LATE])
    full = {n: _from_slabs(n, s) for n, s in zip(LATE, gathered)}
    wbf, wbd, wo, wgt, wut, wd = (full[n] for n in ("w_branch_fox", "w_branch_dn", "w_out", "w_ffn_gate", "w_ffn_up", "w_ffn_down"))
    wbf_p = jnp.pad(wbf.reshape(FOX_H, FOX_D, D), ((0, 0), (0, HP - FOX_D), (0, 0))).reshape(FOX_H * HP, D)
    qn, kn, vn, bg = _dn_prep(dn, sd, cw, alog_p, dt_p)
    u_dn, w_dn, qd_dn, kd_dn, at_dn, gl_dn, x_dn = _dn_intra(qn, kn, vn, bg)
    oraw, vnew, states = _dn_scan(u_dn, w_dn, qd_dn, kd_dn, at_dn, gl_dn)
    h1 = _mix_fwd(op, oraw, dz, ga, gb, h0, wn, wbf_p, wbd, wo)

    dh1, xn2, dgate, dup, act, dh2, acc_f = _ffn_fwd_bwd(h1, tgt_p, w2, wf, wgt, wut, wd)
    g_wg, g_wu, g_wd = _matmul_tn(dgate, xn2, "dw_ffn_gate"), _matmul_tn(dup, xn2, "dw_ffn_up"), _matmul_tn(act, dh2, "dw_ffn_down")

    dop, dor, d_mix, af, ad, dpf, dpd, yb, dmix, acc_m = _mix_bwd(dh1, op, oraw, dz, ga, gb, wn, wbf_p, wbd, wo)
    g_wbf = _matmul_tn(af, dpf, "dw_branch_fox").reshape(FOX_H, HP, D)[:, :FOX_D].reshape(FOX_H * FOX_D, D)
    g_wbd, g_wo = _matmul_tn(ad, dpd, "dw_branch_dn"), _matmul_tn(yb, dmix, "dw_out")

    dvnew, dstates = _dn_scan_bwd(dor, w_dn, qd_dn, kd_dn, at_dn, gl_dn)
    dqn, dkn, dvn, dbg = _dn_intra_bwd(qn, kn, vn, bg, x_dn, dor, vnew, dvnew, states, dstates)
    d_dn, acc_cw, acc_p = _dn_prep_bwd(dn, sd, cw, alog_p, dt_p, dqn, dkn, dvn, dbg)
    g_late = dict(w_branch_fox=g_wbf, w_branch_dn=g_wbd, w_out=g_wo, w_ffn_gate=g_wg, w_ffn_up=g_wu, w_ffn_down=g_wd)
    (dqa, dka, d_fv), recv = _fox_bwd(qa, ka, va, op, dop, [_to_slabs(n, g_late[n]) for n in LATE])
    d_fox, acc_b = _fox_prep_bwd(dqa, dka, sf, bias_p)

    dgroups = [d_fox, d_fv, d_dn, d_mix]
    g_wp = [_matmul_tn(dg, xn, "dw_in_" + grp) for grp, dg in zip(GROUPS, dgroups)]
    dh0, acc_1, (recv_w_in,) = _in_proj_bwd(dgroups, wp, h0, w1, dh1, [_to_slabs("w_in", _unpack_w_in(g_wp, D))])
    recv = dict(zip(LATE, recv), w_in=recv_w_in)

    small = dict(loss=acc_f[2, 0:1], mix_norm_w=acc_1[0], fox_forget_bias=acc_b[0, :FOX_H], dn_a_log=acc_p[0, DN_H:2 * DN_H],
                 dn_dt_bias=acc_p[1, DN_H:2 * DN_H], dn_out_norm_w=acc_m[0], ffn_norm_w=acc_f[0], final_norm_w=acc_f[1],
                 meta_tokens=dh0[N_PAD:PREFIX].reshape(-1), dn_conv_w=acc_cw[:CONV_K].reshape(-1))
    return dh0[PREFIX:], small, recv


def _mesh_pos():
    x, y, c = lax.axis_index("x"), lax.axis_index("y"), lax.axis_index("c")
    return x, y, c, 4 * x + 2 * y + c


def _peer(x, y, c, m):
    flip = lambda v, on: 1 - v if on else v
    px, py, pc = flip(x, m & 4), flip(y, m & 2), flip(c, m & 1)
    return (px, py, pc), 4 * px + 2 * py + pc


def _exchange_sems(n):
    return [pltpu.SemaphoreType.DMA((n, N_DEV - 1)), pltpu.SemaphoreType.DMA((n, N_DEV - 1)), pltpu.SemaphoreType.DMA((n,))]


def _exchange_copies(ins, outs, send_sems, recv_sems, loc_sems, gather, with_receives):
    x, y, c, me = _mesh_pos()
    src = lambda a, pid: ins[a] if gather else ins[a].at[pid]
    local = [pltpu.make_async_copy(src(a, me), outs[a].at[me], loc_sems.at[a]) for a in range(len(ins))]
    sends, recvs = [], []
    for m in range(1, N_DEV):
        peer, pid = _peer(x, y, c, m)
        for a in range(len(ins)):
            kw = dict(send_sem=send_sems.at[a, m - 1], recv_sem=recv_sems.at[a, m - 1], device_id=peer, device_id_type=MESH)
            sends.append(pltpu.make_async_remote_copy(src_ref=src(a, pid), dst_ref=outs[a].at[me], **kw))
            if with_receives:
                recvs.append(pltpu.make_async_remote_copy(src_ref=src(a, pid), dst_ref=outs[a].at[pid], **kw))
    return local, sends, recvs


def _exchange_start(ins, outs, send_sems, recv_sems, loc_sems, gather):
    local, sends, _ = _exchange_copies(ins, outs, send_sems, recv_sems, loc_sems, gather, with_receives=False)
    for cp in local + sends:
        cp.start()


def _exchange_wait(ins, outs, send_sems, recv_sems, loc_sems, gather):
    local, sends, recvs = _exchange_copies(ins, outs, send_sems, recv_sems, loc_sems, gather, with_receives=True)
    for cp in recvs:
        cp.wait_recv()
    for cp in sends:
        cp.wait_send()
    for cp in local:
        cp.wait()


def _gather_two_level(arrays, name):
    n = len(arrays)

    def body(*refs):
        ins, outs, (send_sems, recv_sems, loc_sems) = refs[:n], refs[n:2 * n], refs[2 * n:]
        x, y, c, me = _mesh_pos()
        sib = (x, y, 1 - c)
        chips = [(1 - x, y), (x, 1 - y), (1 - x, 1 - y)]
        dev_id = lambda px, py, pc: 4 * px + 2 * py + pc

        def copy(a, k, block, to, own=False):
            return pltpu.make_async_remote_copy(src_ref=ins[a] if own else outs[a].at[block], dst_ref=outs[a].at[block],
                                                send_sem=send_sems.at[a, k], recv_sem=recv_sems.at[a, k], device_id=to, device_id_type=MESH)

        local = [pltpu.make_async_copy(ins[a], outs[a].at[me], loc_sems.at[a]) for a in range(n)]
        first = [copy(a, 0, me, sib, own=True) for a in range(n)]
        first += [copy(a, 1 + j, me, (*chip, c), own=True) for j, chip in enumerate(chips) for a in range(n)]
        for cp in local + first:
            cp.start()
        passed = []
        for j, chip in enumerate(chips):
            for a in range(n):
                copy(a, 1 + j, dev_id(*chip, c), sib).wait_recv()
                cp = copy(a, 4 + j, dev_id(*chip, c), sib)
                cp.start()
                passed.append(cp)
        for a in range(n):
            copy(a, 0, dev_id(x, y, 1 - c), sib).wait_recv()
        for j, chip in enumerate(chips):
            for a in range(n):
                copy(a, 4 + j, dev_id(*chip, 1 - c), sib).wait_recv()
        for cp in first + passed:
            cp.wait_send()
        for cp in local:
            cp.wait()

    anyspec = pl.BlockSpec(memory_space=pl.ANY)
    return pl.pallas_call(
        body, name=name, in_specs=[anyspec] * n, out_specs=[anyspec] * n,
        out_shape=[SDS((N_DEV,) + a.shape, a.dtype) for a in arrays],
        scratch_shapes=_exchange_sems(n))(*arrays)


def _all_reduce_small(v):
    R = v.shape[0]

    def body(v_ref, o_ref, gath, send_sems, recv_sems):
        x, y, c, me = _mesh_pos()
        gath[me] = v_ref[...]
        sends = []
        for m in range(1, N_DEV):
            peer, _ = _peer(x, y, c, m)
            cp = pltpu.make_async_remote_copy(src_ref=v_ref, dst_ref=gath.at[me], send_sem=send_sems.at[m - 1],
                                              recv_sem=recv_sems.at[m - 1], device_id=peer, device_id_type=MESH)
            cp.start()
            sends.append(cp)
        for m in range(1, N_DEV):
            peer, pid = _peer(x, y, c, m)
            pltpu.make_async_remote_copy(src_ref=v_ref, dst_ref=gath.at[pid], send_sem=send_sems.at[m - 1],
                                         recv_sem=recv_sems.at[m - 1], device_id=peer, device_id_type=MESH).wait_recv()
        for cp in sends:
            cp.wait_send()
        tot = gath[0]
        for d in range(1, N_DEV):
            tot = tot + gath[d]
        o_ref[...] = tot

    vm = pl.BlockSpec(memory_space=pltpu.VMEM)
    return pl.pallas_call(
        body, name="all_reduce_small", in_specs=[vm], out_specs=vm, out_shape=SDS((R, HP), f32),
        scratch_shapes=[pltpu.VMEM((N_DEV, R, HP), f32), pltpu.SemaphoreType.DMA((N_DEV - 1,)), pltpu.SemaphoreType.DMA((N_DEV - 1,))],
        )(v)


def _adamw_math(w, g, m, v):
    m = ADAM_B1 * m + (1.0 - ADAM_B1) * g
    v = ADAM_B2 * v + (1.0 - ADAM_B2) * (g * g)
    m_hat = m / (1.0 - ADAM_B1 ** ADAM_STEP)
    v_hat = v / (1.0 - ADAM_B2 ** ADAM_STEP)
    return -ADAM_LR * (m_hat / (jnp.sqrt(v_hat) + ADAM_EPS) + ADAM_WD * w), m, v


def _adamw(g, w, m, v, name):
    R, Cc = w.shape[-2:]
    if R <= 512 or R % 128 == 0:
        TR, TC = (R if R <= 512 else _pick(R, (256, 128))), Cc
    else:
        TR, TC = R, _pick(Cc, (256, 128))
    slabs = g.ndim == 3
    lead = w.ndim - 2

    def body(g_ref, w_ref, m_ref, v_ref, go_ref, d_ref, mo_ref, vo_ref):
        if slabs:
            gs = g_ref[0].astype(f32)
            for k in range(1, N_DEV):
                gs = gs + g_ref[k].astype(f32)
        else:
            gs = g_ref[...]
        at = 0 if lead else Ellipsis
        d, mn, vn = _adamw_math(w_ref[at], gs, m_ref[at], v_ref[at])
        go_ref[at], d_ref[at], mo_ref[at], vo_ref[at] = gs, d, mn, vn

    grid = (R // TR, Cc // TC)
    blk = pl.BlockSpec((1,) * lead + (TR, TC), lambda i, j: (0,) * lead + (i, j))
    gblk = pl.BlockSpec((N_DEV, TR, TC), lambda i, j: (0, i, j)) if slabs else pl.BlockSpec((TR, TC), lambda i, j: (i, j))
    return pl.pallas_call(
        body, name=name, grid=grid, in_specs=[gblk, blk, blk, blk], out_specs=[blk] * 4,
        out_shape=[SDS(w.shape, f32)] * 4, compiler_params=_cp(2))(g, w, m, v)


WEIGHTS = ("meta_tokens", "mix_norm_w", "w_in", "fox_forget_bias", "dn_conv_w", "dn_a_log", "dn_dt_bias", "dn_out_norm_w",
           "w_branch_fox", "w_branch_dn", "w_out", "ffn_norm_w", "w_ffn_gate", "w_ffn_up", "w_ffn_down", "final_norm_w")
COL_SHARDED = ("w_in", "w_branch_fox", "w_branch_dn", "w_ffn_gate", "w_ffn_up")
ROW_SHARDED = ("w_out", "w_ffn_down")
BIG = COL_SHARDED + ROW_SHARDED
LATE = tuple(n for n in BIG if n != "w_in")
SMALL = tuple(n for n in WEIGHTS if n not in BIG)
TRANSPOSED = ("w_in", "w_ffn_gate", "w_ffn_up")


def _to_slabs(name, g):
    r, c = g.shape
    if name in COL_SHARDED and name not in TRANSPOSED:
        return _b(g.reshape(r, N_DEV, c // N_DEV).transpose(1, 0, 2))
    return _b(g.reshape(N_DEV, r // N_DEV, c))


def _from_slabs(name, s):
    n, r, c = s.shape
    if name in COL_SHARDED and name not in TRANSPOSED:
        return s.transpose(1, 0, 2).reshape(r, n * c)
    return s.reshape(n * r, c)


def kernel(x, meta_tokens, mix_norm_w, w_in, fox_forget_bias, dn_conv_w, dn_a_log, dn_dt_bias, dn_out_norm_w, w_branch_fox, w_branch_dn, w_out, ffn_norm_w, w_ffn_gate, w_ffn_up, w_ffn_down, final_norm_w, loss_target, m_meta_tokens, m_mix_norm_w, m_w_in, m_fox_forget_bias, m_dn_conv_w, m_dn_a_log, m_dn_dt_bias, m_dn_out_norm_w, m_w_branch_fox, m_w_branch_dn, m_w_out, m_ffn_norm_w, m_w_ffn_gate, m_w_ffn_up, m_w_ffn_down, m_final_norm_w, v_meta_tokens, v_mix_norm_w, v_w_in, v_fox_forget_bias, v_dn_conv_w, v_dn_a_log, v_dn_dt_bias, v_dn_out_norm_w, v_w_branch_fox, v_w_branch_dn, v_w_out, v_ffn_norm_w, v_w_ffn_gate, v_w_ffn_up, v_w_ffn_down, v_final_norm_w):
    w = dict(meta_tokens=meta_tokens, mix_norm_w=mix_norm_w, w_in=w_in, fox_forget_bias=fox_forget_bias, dn_conv_w=dn_conv_w, dn_a_log=dn_a_log, dn_dt_bias=dn_dt_bias, dn_out_norm_w=dn_out_norm_w, w_branch_fox=w_branch_fox, w_branch_dn=w_branch_dn, w_out=w_out, ffn_norm_w=ffn_norm_w, w_ffn_gate=w_ffn_gate, w_ffn_up=w_ffn_up, w_ffn_down=w_ffn_down, final_norm_w=final_norm_w)
    mom = dict(meta_tokens=m_meta_tokens, mix_norm_w=m_mix_norm_w, w_in=m_w_in, fox_forget_bias=m_fox_forget_bias, dn_conv_w=m_dn_conv_w, dn_a_log=m_dn_a_log, dn_dt_bias=m_dn_dt_bias, dn_out_norm_w=m_dn_out_norm_w, w_branch_fox=m_w_branch_fox, w_branch_dn=m_w_branch_dn, w_out=m_w_out, ffn_norm_w=m_ffn_norm_w, w_ffn_gate=m_w_ffn_gate, w_ffn_up=m_w_ffn_up, w_ffn_down=m_w_ffn_down, final_norm_w=m_final_norm_w)
    var = dict(meta_tokens=v_meta_tokens, mix_norm_w=v_mix_norm_w, w_in=v_w_in, fox_forget_bias=v_fox_forget_bias, dn_conv_w=v_dn_conv_w, dn_a_log=v_dn_a_log, dn_dt_bias=v_dn_dt_bias, dn_out_norm_w=v_dn_out_norm_w, w_branch_fox=v_w_branch_fox, w_branch_dn=v_w_branch_dn, w_out=v_w_out, ffn_norm_w=v_ffn_norm_w, w_ffn_gate=v_w_ffn_gate, w_ffn_up=v_w_ffn_up, w_ffn_down=v_w_ffn_down, final_norm_w=v_final_norm_w)
    two_d = lambda a: a.reshape(a.shape[-2:]) if a.ndim >= 2 else a.reshape(1, -1)
    me = 4 * lax.axis_index("x") + 2 * lax.axis_index("y") + lax.axis_index("c")
    for d in (w, mom, var):
        for n in TRANSPOSED:
            d[n] = jnp.swapaxes(d[n], -1, -2)

    g_in, g_meta, g_cw = _gather_two_level([_b(two_d(w["w_in"])), two_d(w["meta_tokens"]), two_d(w["dn_conv_w"])], "all_gather_early")
    meta = g_meta.transpose(1, 0, 2).reshape(N_META, -1)
    cw = g_cw.transpose(1, 0, 2).reshape(CONV_K, -1)

    gx, g_small, recv = _local_step(
        x[0], loss_target[0], meta, two_d(w["mix_norm_w"]), _from_slabs("w_in", g_in), two_d(w["fox_forget_bias"]), cw, two_d(w["dn_a_log"]),
        two_d(w["dn_dt_bias"]), two_d(w["dn_out_norm_w"]), two_d(w["ffn_norm_w"]), two_d(w["final_norm_w"]),
        {n: _b(two_d(w[n])) for n in LATE})

    order = ("loss",) + SMALL
    flat = jnp.concatenate([g_small[n].reshape(-1) for n in order])
    rows = -(-flat.shape[0] // (8 * HP)) * 8
    tot = _all_reduce_small(jnp.pad(flat, (0, rows * HP - flat.shape[0])).reshape(rows, HP)).reshape(-1)
    summed, o = {}, 0
    for n in order:
        k = g_small[n].shape[0]
        summed[n] = tot[o:o + k]
        o += k
    loss = summed["loss"][0]
    d_model = x.shape[-1]
    mcols, ccols = d_model // N_DEV, dn_conv_w.shape[-1]
    summed["meta_tokens"] = lax.dynamic_slice(summed["meta_tokens"].reshape(N_META, d_model), (0, me * mcols), (N_META, mcols)).reshape(-1)
    summed["dn_conv_w"] = lax.dynamic_slice(summed["dn_conv_w"].reshape(CONV_K, ccols * N_DEV), (0, me * ccols), (CONV_K, ccols)).reshape(-1)

    res = {}
    for n in BIG:
        res[n] = _adamw(recv[n], w[n], mom[n], var[n], "adamw_" + n)
        if n in TRANSPOSED:
            res[n] = [jnp.swapaxes(r, -1, -2) for r in res[n]]
    sizes = [summed[n].shape[0] for n in SMALL]
    srows = -(-sum(sizes) // (8 * HP)) * 8
    pack = lambda d: jnp.pad(jnp.concatenate([d[n].reshape(-1) for n in SMALL]), (0, srows * HP - sum(sizes))).reshape(srows, HP)
    sres = _adamw(pack(summed), pack(w), pack(mom), pack(var), "adamw_small")
    o = 0
    for n, k in zip(SMALL, sizes):
        res[n] = [r.reshape(-1)[o:o + k].reshape(w[n].shape) for r in sres]
        o += k
    return (loss, gx[None], *[res[n][0] for n in WEIGHTS], *[res[n][1] for n in WEIGHTS], *[res[n][2] for n in WEIGHTS], *[res[n][3] for n in WEIGHTS])
```

```python
import functools

import jax
import jax.numpy as jnp
from jax import lax
from jax.experimental import pallas as pl
from jax.experimental.pallas import tpu as pltpu

f32, bf16 = jnp.float32, jnp.bfloat16
HI = lax.Precision.HIGHEST
MESH = pl.DeviceIdType.MESH
SDS = jax.ShapeDtypeStruct

N_DEV = 8
N_META = 16
PREFIX = 128
N_PAD = PREFIX - N_META
FOX_H, FOX_D = 8, 64
DN_H, DN_D = 4, 128
DN_C = 64
CONV_K = 4
HP = 128
SMALL_W = 256
EPS = 1e-6
NEG = -1e30
C_Q0, C_K0 = 64, 67
LSE_COL = 64
LOG2E, LN2 = 1.4426950408889634, 0.6931471805599453

ADAM_LR, ADAM_B1, ADAM_B2, ADAM_EPS, ADAM_WD, ADAM_STEP = 0.001, 0.9, 0.999, 1e-08, 0.01, 10

VMEM_LIMIT_V7X = 56 * 1024 * 1024
ROW_TILES = (384, 128)
ATTN_TILES = (384, 128)
FFN_TILES = (192, 64)
FOX_HEAD_GROUP = 2
FOX_HEAD_GROUP_FWD = 4
MAX_WGRAD_BLOCK = 1408
DN_INTRA_GROUP = (6, 3, 2, 1)
DN_SCAN_GROUP = (6, 3, 2, 1)


def _pick(n, cands):
    for c in cands:
        if n % c == 0:
            return c
    raise ValueError(f"no tile of {cands} divides {n}")


def _cp(n_axes=1):
    return pltpu.CompilerParams(dimension_semantics=("arbitrary",) * n_axes, vmem_limit_bytes=VMEM_LIMIT_V7X)


def _b(x):
    return x.astype(bf16)


def _dot(a, b):
    return jnp.dot(a, b, preferred_element_type=f32)


def _dot_nt(a, b):
    return lax.dot_general(a, b, (((1,), (1,)), ((), ())), preferred_element_type=f32)


def _dot_tn(a, b):
    return lax.dot_general(a, b, (((0,), (0,)), ((), ())), preferred_element_type=f32)


def _dot_hi(a, b):
    return jnp.dot(a, b, preferred_element_type=f32, precision=HI)


def _iota(shape, dim):
    return lax.broadcasted_iota(jnp.int32, shape, dim)


def _rms(x, w):
    return x * lax.rsqrt(jnp.mean(x * x, axis=-1, keepdims=True) + EPS) * w


def _sigmoid(x):
    return jax.nn.sigmoid(x)


def _load_once(pairs, sems):
    @pl.when(pl.program_id(0) == 0)
    def _():
        cps = [pltpu.make_async_copy(src, dst, sems.at[k]) for k, (src, dst) in enumerate(pairs)]
        for cp in cps:
            cp.start()
        for cp in cps:
            cp.wait()


def _seg_layout(d_model):
    return (("fq", FOX_H * HP, bf16, "fox"), ("fk", FOX_H * HP, bf16, "fox"), ("sf", SMALL_W, f32, "fox"),
            ("fv", FOX_H * HP, bf16, "fv"),
            ("dn", 3 * DN_H * DN_D, f32, "dn"), ("sd", SMALL_W, f32, "dn"),
            ("dz", DN_H * DN_D, f32, "mix"), ("ga", d_model, f32, "mix"), ("gb", d_model, f32, "mix"))


GROUPS = ("fox", "fv", "dn", "mix")


def _group_widths(d_model):
    return [sum(wd for _, wd, _, g in _seg_layout(d_model) if g == grp) for grp in GROUPS]


def _in_proj(h0, w1, wpt, shards):
    L, D = h0.shape
    NP = wpt.shape[0]
    TM = _pick(L, ROW_TILES)
    NT = L // TM
    segs = _seg_layout(D)
    ns, n = len(segs), len(shards)
    offs, o = [], 0
    for _, wd, _, _ in segs:
        offs.append(o)
        o += wd
    assert o == NP

    def body(h_ref, w1_ref, wp_hbm, *rest):
        ins, xn_ref, outs, gouts = rest[:n], rest[n], rest[n + 1:n + 1 + ns], rest[n + 1 + ns:2 * n + 1 + ns]
        wp_v, sems = rest[2 * n + 1 + ns:2 * n + 3 + ns]
        xsems = rest[2 * n + 3 + ns:]
        _load_once([(wp_hbm, wp_v)], sems)

        @pl.when(pl.program_id(0) == 0)
        def _():
            _exchange_start(ins, gouts, *xsems, gather=True)

        xn = _b(_rms(h_ref[...], w1_ref[...]))
        xn_ref[...] = xn
        for o_ref, off, (_, wd, _, _) in zip(outs, offs, segs):
            o_ref[...] = _dot_nt(xn, wp_v[off:off + wd, :]).astype(o_ref.dtype)

        @pl.when(pl.program_id(0) == NT - 1)
        def _():
            _exchange_wait(ins, gouts, *xsems, gather=True)

    row = lambda wd: pl.BlockSpec((TM, wd), lambda i: (i, 0))
    anyspec = pl.BlockSpec(memory_space=pl.ANY)
    res = pl.pallas_call(
        body, name="in_proj", grid=(NT,),
        in_specs=[row(D), pl.BlockSpec((1, D), lambda i: (0, 0)), anyspec] + [anyspec] * n,
        out_specs=[row(D)] + [row(wd) for _, wd, _, _ in segs] + [anyspec] * n,
        out_shape=[SDS((L, D), bf16)] + [SDS((L, wd), dt) for _, wd, dt, _ in segs] + [SDS((N_DEV,) + a.shape, a.dtype) for a in shards],
        scratch_shapes=[pltpu.VMEM((NP, D), bf16), pltpu.SemaphoreType.DMA((1,))] + _exchange_sems(n),
        compiler_params=_cp())(h0, w1, wpt, *shards)
    return res[:1 + ns], res[1 + ns:]


def _in_proj_bwd(dgroups, wpt, h0, w1, dh1, slabs):
    L, D = h0.shape
    NP = wpt.shape[0]
    TM = _pick(L, ROW_TILES)
    NT = L // TM
    widths = [g.shape[1] for g in dgroups]
    assert sum(widths) == NP
    ng, n = len(dgroups), len(slabs)

    def body(*refs):
        dg_refs, (wp_hbm, h_ref, w1_ref, dh1_ref) = refs[:ng], refs[ng:ng + 4]
        ins, (dh0_ref, acc_ref), outs = refs[ng + 4:ng + 4 + n], refs[ng + 4 + n:ng + 6 + n], refs[ng + 6 + n:ng + 6 + 2 * n]
        wp_v, sems = refs[ng + 6 + 2 * n:ng + 8 + 2 * n]
        xsems = refs[ng + 8 + 2 * n:]
        _load_once([(wp_hbm, wp_v)], sems)

        @pl.when(pl.program_id(0) == 0)
        def _():
            acc_ref[...] = jnp.zeros_like(acc_ref)
            _exchange_start(ins, outs, *xsems, gather=False)

        dxn, off = None, 0
        for g_ref, wd in zip(dg_refs, widths):
            part = _dot(g_ref[...], wp_v[off:off + wd, :])
            dxn = part if dxn is None else dxn + part
            off += wd
        _, vjp = jax.vjp(_rms, h_ref[...], w1_ref[...])
        dh0n, dw1 = vjp(dxn)
        dh0_ref[...] = dh1_ref[...] + dh0n
        acc_ref[0:1, :] += dw1

        @pl.when(pl.program_id(0) == NT - 1)
        def _():
            _exchange_wait(ins, outs, *xsems, gather=False)

    row = lambda wd: pl.BlockSpec((TM, wd), lambda i: (i, 0))
    anyspec = pl.BlockSpec(memory_space=pl.ANY)
    res = pl.pallas_call(
        body, name="in_proj_bwd", grid=(NT,),
        in_specs=[row(wd) for wd in widths] + [anyspec, row(D), pl.BlockSpec((1, D), lambda i: (0, 0)), row(D)] + [anyspec] * n,
        out_specs=[row(D), pl.BlockSpec((8, D), lambda i: (0, 0))] + [anyspec] * n,
        out_shape=[SDS((L, D), f32), SDS((8, D), f32)] + [SDS(a.shape, a.dtype) for a in slabs],
        scratch_shapes=[pltpu.VMEM((NP, D), bf16), pltpu.SemaphoreType.DMA((1,))] + _exchange_sems(n),
        compiler_params=_cp())(*dgroups, wpt, h0, w1, dh1, *slabs)
    return res[0], res[1], res[2:]


def _matmul_tn(a, b, name):
    L, R = a.shape
    C = b.shape[1]
    br = max(k for k in range(HP, MAX_WGRAD_BLOCK + 1, HP) if R % k == 0)

    def body(a_ref, b_ref, o_ref):
        o_ref[...] = _b(_dot_tn(a_ref[...], b_ref[...]))

    return pl.pallas_call(
        body, name=name, grid=(R // br,),
        in_specs=[pl.BlockSpec((L, br), lambda r: (0, r)), pl.BlockSpec((L, C), lambda r: (0, 0))],
        out_specs=pl.BlockSpec((br, C), lambda r: (r, 0)), out_shape=SDS((R, C), bf16), compiler_params=_cp())(a, b)


def _fox_prep(fq, fk, fv, sf, bias_p):
    L = fq.shape[0]
    T = HP
    NT = L // T
    W = FOX_H * HP

    def body(fq_ref, fk_ref, fv_ref, sf_ref, b_ref, qa_ref, ka_ref, va_ref, carry):
        @pl.when(pl.program_id(0) == 0)
        def _():
            carry[...] = jnp.zeros_like(carry)

        lane, row = _iota((T, HP), 1), _iota((T, HP), 0)
        logf = jnp.where(lane < FOX_H, jax.nn.log_sigmoid(sf_ref[...] + b_ref[...]), 0.0)
        c = _dot_hi((row >= lane).astype(f32), logf) + carry[...]
        carry[...] = jnp.sum(jnp.where(row == T - 1, c, 0.0), axis=0, keepdims=True)
        ones_q = jnp.where((lane >= C_K0) & (lane < C_K0 + 3), 1.0, 0.0)
        ones_k = jnp.where((lane >= C_Q0) & (lane < C_Q0 + 3), 1.0, 0.0)
        ones_v = _b(jnp.where(lane == LSE_COL, 1.0, 0.0))
        for h in range(FOX_H):
            ch = jnp.broadcast_to(jnp.sum(jnp.where(lane == h, c, 0.0), axis=1, keepdims=True), (T, HP)) * LOG2E
            c1 = _b(ch).astype(f32)
            c2 = _b(ch - c1).astype(f32)
            c3 = _b(ch - c1 - c2).astype(f32)
            cq = jnp.where(lane == C_Q0, c1, 0.0) + jnp.where(lane == C_Q0 + 1, c2, 0.0) + jnp.where(lane == C_Q0 + 2, c3, 0.0)
            ck = jnp.where(lane == C_K0, c1, 0.0) + jnp.where(lane == C_K0 + 1, c2, 0.0) + jnp.where(lane == C_K0 + 2, c3, 0.0)
            q = fq_ref[:, h * HP:(h + 1) * HP].astype(f32) * (FOX_D ** -0.5 * LOG2E)
            k = fk_ref[:, h * HP:(h + 1) * HP].astype(f32)
            qa_ref[h] = _b(q + cq + ones_q)
            ka_ref[h] = _b(k + ones_k - ck)
            va_ref[:, h * HP:(h + 1) * HP] = fv_ref[:, h * HP:(h + 1) * HP] + ones_v

    wide = pl.BlockSpec((T, W), lambda i: (i, 0))
    return pl.pallas_call(
        body, name="fox_prep", grid=(NT,),
        in_specs=[wide, wide, wide, pl.BlockSpec((T, HP), lambda i: (i, 0)), pl.BlockSpec((1, HP), lambda i: (0, 0))],
        out_specs=[pl.BlockSpec((FOX_H, T, HP), lambda i: (0, i, 0))] * 2 + [wide],
        out_shape=[SDS((FOX_H, L, HP), bf16)] * 2 + [SDS((L, W), bf16)],
        scratch_shapes=[pltpu.VMEM((1, HP), f32)], compiler_params=_cp())(fq, fk, fv, sf, bias_p)


def _fox_prep_bwd(dqa, dka, sf, bias_p):
    L = sf.shape[0]
    T = HP
    NT = L // T
    rev = lambda i: (NT - 1 - i, 0)

    W = FOX_H * HP

    def body(dq_ref, dk_ref, sf_ref, b_ref, dg_ref, db_ref, carry):
        @pl.when(pl.program_id(0) == 0)
        def _():
            carry[...] = jnp.zeros_like(carry)
            db_ref[...] = jnp.zeros_like(db_ref)

        dq, dk = dq_ref[...], dk_ref[...]
        dg_ref[:, 0:W] = _b(dq * (FOX_D ** -0.5))
        dg_ref[:, W:2 * W] = _b(dk * LN2)
        lane, row = _iota((T, HP), 1), _iota((T, HP), 0)
        dc = jnp.zeros((T, HP), f32)
        for h in range(FOX_H):
            col = jnp.sum(jnp.where(lane == C_Q0, dq[:, h * HP:(h + 1) * HP], 0.0)
                          - jnp.where(lane == C_K0, dk[:, h * HP:(h + 1) * HP], 0.0), axis=1, keepdims=True)
            dc = dc + jnp.where(lane == h, col, 0.0)
        dl = _dot_hi((row <= lane).astype(f32), dc) + carry[...]
        carry[...] = jnp.sum(jnp.where(row == 0, dl, 0.0), axis=0, keepdims=True)
        dx = jnp.where(lane < FOX_H, dl * _sigmoid(-(sf_ref[...] + b_ref[...])), 0.0)
        dg_ref[:, 2 * W:2 * W + HP] = _b(dx)
        dg_ref[:, 2 * W + HP:] = jnp.zeros((T, SMALL_W - HP), bf16)
        db_ref[0:1, :] += jnp.sum(dx, axis=0, keepdims=True)

    return pl.pallas_call(
        body, name="fox_prep_bwd", grid=(NT,),
        in_specs=[pl.BlockSpec((T, W), rev), pl.BlockSpec((T, W), rev), pl.BlockSpec((T, HP), rev), pl.BlockSpec((1, HP), lambda i: (0, 0))],
        out_specs=[pl.BlockSpec((T, 2 * W + SMALL_W), rev), pl.BlockSpec((8, HP), lambda i: (0, 0))],
        out_shape=[SDS((L, 2 * W + SMALL_W), bf16), SDS((8, HP), f32)],
        scratch_shapes=[pltpu.VMEM((1, HP), f32)], compiler_params=_cp())(dqa, dka, sf, bias_p)


def _tile_start(j, T):
    return j * T if isinstance(j, int) else pl.multiple_of(j * T, T)


def _fox_fwd(qa, ka, fv, shards):
    L = qa.shape[1]
    TQ = TK = _pick(L, ATTN_TILES)
    NQ = L // TQ
    n = len(shards)
    HG = FOX_HEAD_GROUP_FWD

    def body(q_ref, k_ref, v_ref, *rest):
        ins, o_ref, outs, sems = rest[:n], rest[n], rest[n + 1:2 * n + 1], rest[2 * n + 1:]
        h, i = pl.program_id(0), pl.program_id(1)

        @pl.when((h == 0) & (i == 0))
        def _():
            _exchange_start(ins, outs, *sems, gather=True)

        qs = [q_ref[a] for a in range(HG)]
        rowg = i * TQ + _iota((TQ, TK), 0)
        colb = _iota((TQ, TK), 1)

        def step(j, carry, masked):
            ms, accs = carry
            k0 = _tile_start(j, TK)
            ss = [_dot_nt(qs[a], k_ref[a, pl.ds(k0, TK), :]) for a in range(HG)]
            if masked:
                colg = colb + j * TK
                keep = (colg <= rowg) & (colg >= N_PAD)
                ss = [jnp.where(keep, s, NEG) for s in ss]
            m_new = [jnp.maximum(m, jnp.max(s, axis=1, keepdims=True)) for m, s in zip(ms, ss)]
            ps = [_b(jnp.exp2(s - m)) for s, m in zip(ss, m_new)]
            alphas = [jnp.exp2(m - mn) for m, mn in zip(ms, m_new)]
            accs = [al * acc + _dot(p, v_ref[pl.ds(k0, TK), a * HP:(a + 1) * HP]) for a, (al, acc, p) in enumerate(zip(alphas, accs, ps))]
            return m_new, accs

        init = ([jnp.full((TQ, 1), NEG, f32)] * HG, [jnp.zeros((TQ, HP), f32)] * HG)
        carry = step(0, init, True)
        carry = lax.fori_loop(1, i, functools.partial(step, masked=False), carry)
        ms, accs = lax.fori_loop(jnp.maximum(i, 1), i + 1, functools.partial(step, masked=True), carry)
        lane = _iota((TQ, HP), 1)
        for a in range(HG):
            l = jnp.sum(jnp.where(lane == LSE_COL, accs[a], 0.0), axis=1, keepdims=True)
            o_ref[:, a * HP:(a + 1) * HP] = jnp.where(lane == LSE_COL, ms[a] + jnp.log2(l), accs[a] / l)

        @pl.when((h == FOX_H // HG - 1) & (i == NQ - 1))
        def _():
            _exchange_wait(ins, outs, *sems, gather=True)

    anyspec = pl.BlockSpec(memory_space=pl.ANY)
    res = pl.pallas_call(
        body, name="fox_fwd", grid=(FOX_H // HG, NQ),
        in_specs=[pl.BlockSpec((HG, TQ, HP), lambda h, i: (h, i, 0)), pl.BlockSpec((HG, L, HP), lambda h, i: (h, 0, 0)),
                  pl.BlockSpec((L, HG * HP), lambda h, i: (0, h))] + [anyspec] * n,
        out_specs=[pl.BlockSpec((TQ, HG * HP), lambda h, i: (i, h))] + [anyspec] * n,
        out_shape=[SDS((L, FOX_H * HP), f32)] + [SDS((N_DEV,) + a.shape, a.dtype) for a in shards],
        scratch_shapes=_exchange_sems(n), compiler_params=_cp(2))(qa, ka, fv, *shards)
    return res[0], res[1:]


def _fox_bwd(qa, ka, fv, op, dop, slabs):
    L = qa.shape[1]
    TQ = TK = _pick(L, ATTN_TILES)
    NQ = L // TQ
    n = len(slabs)
    HG = FOX_HEAD_GROUP

    def body(q_ref, k_ref, v_ref, o_ref, do_ref, *rest):
        ins, (dq_ref, dk_ref, dv_ref), outs = rest[:n], rest[n:n + 3], rest[n + 3:2 * n + 3]
        lse_s, delta_s = rest[2 * n + 3:2 * n + 5]
        sems = rest[2 * n + 5:]
        h, j = pl.program_id(0), pl.program_id(1)
        cols = [slice(a * HP, (a + 1) * HP) for a in range(HG)]

        @pl.when((h == 0) & (j == 0))
        def _():
            _exchange_start(ins, outs, *sems, gather=False)

        lane = _iota((TQ, HP), 1)

        @pl.when(j == 0)
        def _():
            dq_ref[...] = jnp.zeros_like(dq_ref)
            for t in range(NQ):
                r = slice(t * TQ, (t + 1) * TQ)
                for a in range(HG):
                    o, do = o_ref[r, cols[a]], do_ref[r, cols[a]]
                    lse_s[a, r, :] = jnp.sum(jnp.where(lane == LSE_COL, o, 0.0), axis=1, keepdims=True)
                    delta_s[a, r, :] = jnp.sum(jnp.where(lane < FOX_D, o * do, 0.0), axis=1, keepdims=True)

        kts = [k_ref[a] for a in range(HG)]
        vts = [v_ref[:, cols[a]] for a in range(HG)]
        colg = j * TK + _iota((TQ, TK), 1)
        rowb = _iota((TQ, TK), 0)

        def step(i, carry, masked):
            dks, dvs = carry
            r0 = _tile_start(i, TQ)
            rows = pl.ds(r0, TQ)
            qs = [q_ref[a, rows, :] for a in range(HG)]
            ps = [jnp.exp2(_dot_nt(q, kt) - lse_s[a, rows, :]) for a, (q, kt) in enumerate(zip(qs, kts))]
            if masked:
                keep = (colg <= rowb + i * TQ) & (colg >= N_PAD)
                ps = [jnp.where(keep, p, 0.0) for p in ps]
            dobs = [_b(do_ref[rows, cols[a]]) for a in range(HG)]
            dvs = [dv + _dot_tn(_b(p), dob) for dv, p, dob in zip(dvs, ps, dobs)]
            dss = [_b(p * (_dot_nt(dob, vt) - delta_s[a, rows, :])) for a, (p, dob, vt) in enumerate(zip(ps, dobs, vts))]
            for a in range(HG):
                dq_ref[rows, cols[a]] += _dot(dss[a], kts[a])
            dks = [dk + _dot_tn(ds, q) for dk, ds, q in zip(dks, dss, qs)]
            return dks, dvs

        zeros = [jnp.zeros((TK, HP), f32)] * HG
        carry = step(j, (zeros, zeros), True)
        split = jnp.where(j == 0, NQ, j + 1)
        carry = lax.fori_loop(j + 1, split, functools.partial(step, masked=True), carry)
        dks, dvs = lax.fori_loop(split, NQ, functools.partial(step, masked=False), carry)
        for a in range(HG):
            dk_ref[:, cols[a]] = dks[a]
            dv_ref[:, cols[a]] = _b(dvs[a])

        @pl.when((h == FOX_H // HG - 1) & (j == NQ - 1))
        def _():
            _exchange_wait(ins, outs, *sems, gather=False)

    head = pl.BlockSpec((L, HG * HP), lambda h, j: (0, h))
    tile = pl.BlockSpec((TK, HG * HP), lambda h, j: (j, h))
    anyspec = pl.BlockSpec(memory_space=pl.ANY)
    res = pl.pallas_call(
        body, name="fox_bwd", grid=(FOX_H // HG, L // TK),
        in_specs=[pl.BlockSpec((HG, L, HP), lambda h, j: (h, 0, 0)), pl.BlockSpec((HG, TK, HP), lambda h, j: (h, j, 0)), tile, head, head]
        + [anyspec] * n,
        out_specs=[head, tile, tile] + [anyspec] * n,
        out_shape=[SDS((L, FOX_H * HP), f32), SDS((L, FOX_H * HP), f32), SDS((L, FOX_H * HP), bf16)] + [SDS(a.shape, a.dtype) for a in slabs],
        scratch_shapes=[pltpu.VMEM((HG, L, 1), f32), pltpu.VMEM((HG, L, 1), f32)] + _exchange_sems(n),
        compiler_params=_cp(2))(qa, ka, fv, op, dop, *slabs)
    return res[:3], res[3:]


def _dn_post(y, sd, alog_p, dt_p, valid):
    a = y * _sigmoid(y)
    W = DN_H * DN_D
    heads = []
    for part, scale in ((0, DN_D ** -0.5), (1, 1.0)):
        for h in range(DN_H):
            xh = a[:, part * W + h * DN_D:part * W + (h + 1) * DN_D]
            heads.append(xh * lax.rsqrt(jnp.sum(xh * xh, axis=-1, keepdims=True) + EPS) * scale)
    q = jnp.concatenate(heads[:DN_H], axis=1)
    k = jnp.concatenate(heads[DN_H:], axis=1)
    v = a[:, 2 * W:3 * W]
    lane = _iota(sd.shape, 1)
    beta = _sigmoid(sd) * valid
    g = -jnp.exp(alog_p) * jax.nn.softplus(sd + dt_p) * valid
    bg = jnp.where(lane < DN_H, beta, jnp.where(lane < 2 * DN_H, g, 0.0))
    return q, k, v, bg


def _conv_fwd(ext_ref, cw_ref, TM):
    y = cw_ref[0:1, :] * ext_ref[8 - (CONV_K - 1):8 - (CONV_K - 1) + TM, :]
    for i in range(1, CONV_K):
        o = 8 - (CONV_K - 1) + i
        y = y + cw_ref[i:i + 1, :] * ext_ref[o:o + TM, :]
    return y


def _dn_prep(dn, sd, cw, alog_p, dt_p):
    L, W3 = dn.shape
    TM = _pick(L, ROW_TILES)
    W = DN_H * DN_D

    def body(dn_ref, halo_ref, sd_ref, cw_ref, al_ref, dt_ref, q_ref, k_ref, v_ref, bg_ref, ext):
        i = pl.program_id(0)
        ext[0:8, :] = jnp.where(i == 0, 0.0, halo_ref[...])
        ext[8:, :] = dn_ref[...]
        y = _conv_fwd(ext, cw_ref, TM)
        valid = ((i * TM + _iota((TM, 1), 0)) >= N_PAD).astype(f32)
        q, k, v, bg = _dn_post(y, sd_ref[...], al_ref[...], dt_ref[...], valid)
        q_ref[...], k_ref[...], v_ref[...], bg_ref[...] = q, k, v, bg

    row = lambda wd: pl.BlockSpec((TM, wd), lambda i: (i, 0))
    vec = pl.BlockSpec((1, HP), lambda i: (0, 0))
    return pl.pallas_call(
        body, name="dn_prep", grid=(L // TM,),
        in_specs=[row(W3), pl.BlockSpec((8, W3), lambda i: (jnp.maximum(i * (TM // 8) - 1, 0), 0)), row(HP),
                  pl.BlockSpec((CONV_K, W3), lambda i: (0, 0)), vec, vec],
        out_specs=[row(W), row(W), row(W), row(HP)],
        out_shape=[SDS((L, W), f32)] * 3 + [SDS((L, HP), f32)],
        scratch_shapes=[pltpu.VMEM((TM + 8, W3), f32)], compiler_params=_cp())(dn, dn, sd, cw, alog_p, dt_p)


def _dn_prep_bwd(dn, sd, cw, alog_p, dt_p, dq, dk, dv, dbg):
    L, W3 = dn.shape
    TM = _pick(L, ROW_TILES)
    NT = L // TM
    W = DN_H * DN_D

    def body(dn_ref, halo_ref, sd_ref, cw_ref, al_ref, dt_ref, dq_ref, dk_ref, dv_ref, dbg_ref,
             dg_ref, dcw_ref, dp_ref, ext, dyp, carry):
        i = pl.program_id(0)
        t = NT - 1 - i

        @pl.when(i == 0)
        def _():
            carry[...] = jnp.zeros_like(carry)
            dcw_ref[...] = jnp.zeros_like(dcw_ref)
            dp_ref[...] = jnp.zeros_like(dp_ref)
            dyp[...] = jnp.zeros_like(dyp)

        ext[0:8, :] = jnp.where(t == 0, 0.0, halo_ref[...])
        ext[8:, :] = dn_ref[...]
        y = _conv_fwd(ext, cw_ref, TM)
        valid = ((t * TM + _iota((TM, 1), 0)) >= N_PAD).astype(f32)
        _, vjp = jax.vjp(functools.partial(_dn_post, valid=valid), y, sd_ref[...], al_ref[...], dt_ref[...])
        dy, dsd, dal, ddt = vjp((dq_ref[...], dk_ref[...], dv_ref[...], dbg_ref[...]))
        dg_ref[:, W3:W3 + HP] = _b(dsd)
        dg_ref[:, W3 + HP:] = jnp.zeros((TM, SMALL_W - HP), bf16)
        dp_ref[0:1, :] += dal
        dp_ref[1:2, :] += ddt
        dyp[8:8 + TM, :] = dy
        o0 = CONV_K - 1
        dext = cw_ref[0:1, :] * dyp[o0:o0 + TM + 8, :]
        for k in range(1, CONV_K):
            dext = dext + cw_ref[k:k + 1, :] * dyp[o0 - k:o0 - k + TM + 8, :]
        for k in range(CONV_K):
            o = 8 - (CONV_K - 1) + k
            dcw_ref[k:k + 1, :] += jnp.sum(dy * ext[o:o + TM, :], axis=0, keepdims=True)
        dg_ref[:, 0:W3] = _b(jnp.concatenate([dext[8:TM, :], dext[TM:TM + 8, :] + carry[...]], axis=0))
        carry[...] = dext[0:8, :]

    row = lambda wd: pl.BlockSpec((TM, wd), lambda i: (NT - 1 - i, 0))
    vec = pl.BlockSpec((1, HP), lambda i: (0, 0))
    return pl.pallas_call(
        body, name="dn_prep_bwd", grid=(NT,),
        in_specs=[row(W3), pl.BlockSpec((8, W3), lambda i: (jnp.maximum((NT - 1 - i) * (TM // 8) - 1, 0), 0)), row(HP),
                  pl.BlockSpec((CONV_K, W3), lambda i: (0, 0)), vec, vec, row(W), row(W), row(W), row(HP)],
        out_specs=[row(W3 + SMALL_W), pl.BlockSpec((8, W3), lambda i: (0, 0)), pl.BlockSpec((8, HP), lambda i: (0, 0))],
        out_shape=[SDS((L, W3 + SMALL_W), bf16), SDS((8, W3), f32), SDS((8, HP), f32)],
        scratch_shapes=[pltpu.VMEM((TM + 8, W3), f32), pltpu.VMEM((TM + 16, W3), f32), pltpu.VMEM((8, W3), f32)],
        compiler_params=_cp())(dn, dn, sd, cw, alog_p, dt_p, dq, dk, dv, dbg)


def _split2(x):
    hi = _b(x)
    return hi, _b(x - hi.astype(f32))


def _split3(x):
    hi = _b(x)
    r = x - hi.astype(f32)
    mid = _b(r)
    return hi, mid, _b(r - mid.astype(f32))


def _x3(a, b, dot):
    (a1, a2), (b1, b2) = _split2(a), _split2(b)
    return dot(a1, b1) + (dot(a1, b2) + dot(a2, b1))


@jax.custom_vjp
def _dot_x3(a, b):
    return _x3(a, b, _dot)


_dot_x3.defvjp(lambda a, b: (_x3(a, b, _dot), (a, b)), lambda res, g: (_x3(g, res[1], _dot_nt), _x3(res[0], g, _dot_tn)))


def _exact3(m, x, dot):
    x1, x2, x3 = _split3(x)
    return dot(m, x1) + (dot(m, x2) + dot(m, x3))


def _tri_ones(C, lower):
    row, col = _iota((C, C), 0), _iota((C, C), 1)
    return _b(((row >= col) if lower else (row <= col)).astype(f32))


@jax.custom_vjp
def _chunk_cumsum(x):
    return _exact3(_tri_ones(x.shape[0], True), x, _dot)


_chunk_cumsum.defvjp(lambda x: (_exact3(_tri_ones(x.shape[0], True), x, _dot), None),
                     lambda _, g: (_exact3(_tri_ones(g.shape[0], False), g, _dot),))


def _mxu_transpose(x):
    C = x.shape[0]
    eye = _b((_iota((C, C), 0) == _iota((C, C), 1)).astype(f32))
    return _exact3(eye, x, lambda m, part: _dot_tn(part, m))


@jax.custom_vjp
def _transpose_exact(x):
    return _mxu_transpose(x)


_transpose_exact.defvjp(lambda x: (_mxu_transpose(x), None), lambda _, g: (_mxu_transpose(g),))


def _unit_lower_inverses(lows):
    C = lows[0].shape[0]
    P = jnp.stack(lows)
    X = (_iota((C, C), 0) == _iota((C, C), 1)).astype(f32)[None] - P
    bdot = functools.partial(_x3, dot=lambda a, b: jnp.einsum("bij,bjk->bik", a, b, preferred_element_type=f32))
    for _ in range(5):
        P = bdot(P, P)
        X = X + bdot(X, P)
    return [X[i] for i in range(len(lows))]


@jax.custom_vjp
def _inverse_given(low, X):
    return X


def _inverse_given_bwd(X, g):
    return -_x3(_x3(X, g, _dot_tn), X, _dot_nt), jnp.zeros_like(X)


_inverse_given.defvjp(lambda low, X: (X, X), _inverse_given_bwd)


def _dn_intra_pre(q, k, v, bg):
    C = DN_C
    row, col = _iota((C, C), 0), _iota((C, C), 1)
    tri = row >= col
    G = _chunk_cumsum(bg)
    GT = _transpose_exact(G)
    lane = _iota((C, HP), 1)
    rowt = _iota((HP, C), 0)
    last = _iota((C, 1), 0) == C - 1
    heads = []
    for h in range(DN_H):
        beta = jnp.sum(jnp.where(lane == h, bg, 0.0), axis=1, keepdims=True)
        gcol = jnp.sum(jnp.where(lane == DN_H + h, G, 0.0), axis=1, keepdims=True)
        grow = jnp.sum(jnp.where(rowt == DN_H + h, GT, 0.0), axis=0, keepdims=True)
        glast = jnp.sum(jnp.where(last, gcol, 0.0), axis=0, keepdims=True)
        decay = jnp.exp(jnp.where(tri, gcol - grow, NEG))
        qh, kh, vh = (t[:, h * DN_D:(h + 1) * DN_D] for t in (q, k, v))
        kb = kh * beta
        low = jnp.where(row > col, _dot_nt(_b(kb), _b(kh)) * decay, 0.0)
        heads.append((beta, gcol, glast, decay, qh, kh, vh, kb, low))
    return heads


def _dn_intra_post(heads, xs):
    lane1 = _iota((1, HP), 1)
    us, ws, qds, kds, attns = [], [], [], [], []
    glrow = jnp.zeros((1, HP), f32)
    for h, ((beta, gcol, glast, decay, qh, kh, vh, kb, _), X) in enumerate(zip(heads, xs)):
        eg = jnp.exp(gcol)
        us.append(_dot_x3(X, vh * beta))
        ws.append(_dot_x3(X, kb * eg))
        attns.append(_dot_nt(_b(qh), _b(kh)) * decay)
        qds.append(qh * eg)
        kds.append(kh * jnp.exp(glast - gcol))
        glrow = glrow + jnp.where(lane1 == h, glast, 0.0)
    cat = lambda xs_: jnp.concatenate(xs_, axis=1)
    return cat(us), cat(ws), cat(qds), cat(kds), cat(attns), glrow, cat(list(xs))


def _dn_intra_group(q, k, v, bg, xs):
    G = q.shape[0] // DN_C
    rows = [slice(j * DN_C, (j + 1) * DN_C) for j in range(G)]
    pre = [_dn_intra_pre(q[r, :], k[r, :], v[r, :], bg[r, :]) for r in rows]
    inv = [[_inverse_given(hd[-1], x) for hd, x in zip(heads, xj)] for heads, xj in zip(pre, xs)]
    post = [_dn_intra_post(heads, xj) for heads, xj in zip(pre, inv)]
    return tuple(jnp.concatenate([p[i] for p in post], axis=0) for i in range(5)) + (tuple(p[5] for p in post),)


def _lane_pick(rowvec, h):
    return jnp.sum(jnp.where(_iota(rowvec.shape, 1) == h, rowvec, 0.0), axis=1, keepdims=True)


def _dn_intra(q, k, v, bg):
    L, W = q.shape
    NC = L // DN_C
    G = _pick(NC, DN_INTRA_GROUP)
    R = G * DN_C
    WA = DN_H * DN_C

    def body(q_ref, k_ref, v_ref, bg_ref, u_ref, w_ref, qd_ref, kd_ref, at_ref, gl_ref, x_ref):
        rows = [slice(j * DN_C, (j + 1) * DN_C) for j in range(G)]
        pre = [_dn_intra_pre(q_ref[r, :], k_ref[r, :], v_ref[r, :], bg_ref[r, :]) for r in rows]
        inv = _unit_lower_inverses([hd[-1] for heads in pre for hd in heads])
        for j, r in enumerate(rows):
            u, w, qd, kd, at, gl, xs = _dn_intra_post(pre[j], inv[j * DN_H:(j + 1) * DN_H])
            u_ref[r, :], x_ref[r, :] = u, xs
            w_ref[r, :], qd_ref[r, :], kd_ref[r, :], at_ref[r, :] = _b(w), _b(qd), _b(kd), _b(at)
            gl_ref[j] = gl

    row = lambda wd: pl.BlockSpec((R, wd), lambda n: (n, 0))
    return pl.pallas_call(
        body, name="dn_intra", grid=(NC // G,),
        in_specs=[row(W), row(W), row(W), row(HP)],
        out_specs=[row(W), row(W), row(W), row(W), row(WA), pl.BlockSpec((G, 1, HP), lambda n: (n, 0, 0)), row(WA)],
        out_shape=[SDS((L, W), f32), SDS((L, W), bf16), SDS((L, W), bf16), SDS((L, W), bf16), SDS((L, WA), bf16), SDS((NC, 1, HP), f32),
                   SDS((L, WA), f32)],
        compiler_params=_cp())(q, k, v, bg)


def _dn_scan(u, w, qd, kd, at, gl):
    L, W = u.shape
    NC = L // DN_C
    G = _pick(NC, DN_SCAN_GROUP)
    R = G * DN_C

    def body(u_ref, w_ref, qd_ref, kd_ref, at_ref, gl_ref, o_ref, vn_ref, s_ref, S):
        @pl.when(pl.program_id(0) == 0)
        def _():
            S[...] = jnp.zeros_like(S)

        for j in range(G):
            r = slice(j * DN_C, (j + 1) * DN_C)
            glrow = gl_ref[j]
            for h in range(DN_H):
                c = slice(h * DN_D, (h + 1) * DN_D)
                Sh = S[h]
                s_ref[j, h] = Sh
                Sb = _b(Sh)
                vb = _b(u_ref[r, c] - _dot(w_ref[r, c], Sb))
                vn_ref[r, c] = vb
                o_ref[r, c] = _dot(qd_ref[r, c], Sb) + _dot(at_ref[r, h * DN_C:(h + 1) * DN_C], vb)
                S[h] = Sh * jnp.exp(_lane_pick(glrow, h)) + _dot_tn(kd_ref[r, c], vb)

    row = lambda wd: pl.BlockSpec((R, wd), lambda n: (n, 0))
    return pl.pallas_call(
        body, name="dn_scan", grid=(NC // G,),
        in_specs=[row(W), row(W), row(W), row(W), row(DN_H * DN_C), pl.BlockSpec((G, 1, HP), lambda n: (n, 0, 0))],
        out_specs=[row(W), row(W), pl.BlockSpec((G, DN_H, DN_D, DN_D), lambda n: (n, 0, 0, 0))],
        out_shape=[SDS((L, W), f32), SDS((L, W), bf16), SDS((NC, DN_H, DN_D, DN_D), f32)],
        scratch_shapes=[pltpu.VMEM((DN_H, DN_D, DN_D), f32)], compiler_params=_cp())(u, w, qd, kd, at, gl)


def _dn_scan_bwd(do, w, qd, kd, at, gl):
    L, W = do.shape
    NC = L // DN_C
    G = _pick(NC, DN_SCAN_GROUP)
    R = G * DN_C
    NS = NC // G

    def body(do_ref, w_ref, qd_ref, kd_ref, at_ref, gl_ref, dvn_ref, ds_ref, dS):
        @pl.when(pl.program_id(0) == 0)
        def _():
            dS[...] = jnp.zeros_like(dS)

        for j in reversed(range(G)):
            r = slice(j * DN_C, (j + 1) * DN_C)
            glrow = gl_ref[j]
            for h in range(DN_H):
                c = slice(h * DN_D, (h + 1) * DN_D)
                dSo = dS[h]
                ds_ref[j, h] = dSo
                dob = _b(do_ref[r, c])
                dvn = _dot_tn(at_ref[r, h * DN_C:(h + 1) * DN_C], dob) + _dot(kd_ref[r, c], _b(dSo))
                dvn_ref[r, c] = dvn
                dS[h] = _dot_tn(qd_ref[r, c], dob) + dSo * jnp.exp(_lane_pick(glrow, h)) - _dot_tn(w_ref[r, c], _b(dvn))

    row = lambda wd: pl.BlockSpec((R, wd), lambda n: (NS - 1 - n, 0))
    return pl.pallas_call(
        body, name="dn_scan_bwd", grid=(NS,),
        in_specs=[row(W), row(W), row(W), row(W), row(DN_H * DN_C), pl.BlockSpec((G, 1, HP), lambda n: (NS - 1 - n, 0, 0))],
        out_specs=[row(W), pl.BlockSpec((G, DN_H, DN_D, DN_D), lambda n: (NS - 1 - n, 0, 0, 0))],
        out_shape=[SDS((L, W), f32), SDS((NC, DN_H, DN_D, DN_D), f32)],
        scratch_shapes=[pltpu.VMEM((DN_H, DN_D, DN_D), f32)], compiler_params=_cp())(do, w, qd, kd, at, gl)


def _dn_intra_bwd(q, k, v, bg, xinv, do, vn, dvn, states, dstates):
    L, W = q.shape
    NC = L // DN_C
    G = _pick(NC, DN_INTRA_GROUP)
    R = G * DN_C

    def body(q_ref, k_ref, v_ref, bg_ref, x_ref, do_ref, vn_ref, dvn_ref, s_ref, ds_ref, dq_ref, dk_ref, dv_ref, dbg_ref):
        lane1 = _iota((1, HP), 1)
        rows = [slice(j * DN_C, (j + 1) * DN_C) for j in range(G)]
        xs = [[x_ref[r, h * DN_C:(h + 1) * DN_C] for h in range(DN_H)] for r in rows]
        outs, vjp = jax.vjp(functools.partial(_dn_intra_group, xs=xs), q_ref[...], k_ref[...], v_ref[...], bg_ref[...])
        dws, dqds, dkds, dats, dgls = [], [], [], [], []
        for j, r in enumerate(rows):
            dw, dqd, dkd, dat = [], [], [], []
            dgl = jnp.zeros((1, HP), f32)
            for h in range(DN_H):
                c = slice(h * DN_D, (h + 1) * DN_D)
                Sh, dSo = s_ref[j, h], ds_ref[j, h]
                Sb, dob, vb = _b(Sh), _b(do_ref[r, c]), vn_ref[r, c]
                dw.append(-_dot_nt(_b(dvn_ref[r, c]), Sb))
                dqd.append(_dot_nt(dob, Sb))
                dat.append(_dot_nt(dob, vb))
                dkd.append(_dot_nt(vb, _b(dSo)))
                dcd = jnp.sum(jnp.sum(Sh * dSo, axis=1, keepdims=True), axis=0, keepdims=True)
                dgl = dgl + jnp.where(lane1 == h, dcd * jnp.exp(_lane_pick(outs[5][j], h)), 0.0)
            cat = lambda xs_: jnp.concatenate(xs_, axis=1)
            dws.append(cat(dw)), dqds.append(cat(dqd)), dkds.append(cat(dkd)), dats.append(cat(dat)), dgls.append(dgl)
        cat0 = lambda xs_: jnp.concatenate(xs_, axis=0)
        dq, dk, dv, dbg = vjp((dvn_ref[...], cat0(dws), cat0(dqds), cat0(dkds), cat0(dats), tuple(dgls)))
        dq_ref[...], dk_ref[...], dv_ref[...], dbg_ref[...] = dq, dk, dv, dbg

    row = lambda wd: pl.BlockSpec((R, wd), lambda n: (n, 0))
    st = pl.BlockSpec((G, DN_H, DN_D, DN_D), lambda n: (n, 0, 0, 0))
    return pl.pallas_call(
        body, name="dn_intra_bwd", grid=(NC // G,),
        in_specs=[row(W), row(W), row(W), row(HP), row(DN_H * DN_C), row(W), row(W), row(W), st, st],
        out_specs=[row(W), row(W), row(W), row(HP)],
        out_shape=[SDS((L, W), f32)] * 3 + [SDS((L, HP), f32)],
        compiler_params=_cp())(q, k, v, bg, xinv, do, vn, dvn, states, dstates)


def _dn_normgate(oraw, dz, wn):
    outs = []
    for h in range(DN_H):
        sl = slice(h * DN_D, (h + 1) * DN_D)
        z = dz[:, sl]
        outs.append(_rms(oraw[:, sl], wn) * (z * _sigmoid(z)))
    return jnp.concatenate(outs, axis=1)


def _mix_fwd(op, oraw, dz, ga, gb, h0, wn, wbf, wbd, wo):
    L, D = h0.shape
    TM = _pick(L, ROW_TILES)

    def body(op_ref, or_ref, dz_ref, ga_ref, gb_ref, h0_ref, wn_ref, wbf_ref, wbd_ref, wo_ref, h1_ref):
        pf = _dot(_b(op_ref[...]), wbf_ref[...])
        pd = _dot(_b(_dn_normgate(or_ref[...], dz_ref[...], wn_ref[...])), wbd_ref[...])
        y = _sigmoid(ga_ref[...]) * pf + _sigmoid(gb_ref[...]) * pd
        h1_ref[...] = h0_ref[...] + _dot(_b(y), wo_ref[...])

    row = lambda wd: pl.BlockSpec((TM, wd), lambda i: (i, 0))
    full = lambda a: pl.BlockSpec(a.shape, lambda i: (0, 0))
    return pl.pallas_call(
        body, name="mix_fwd", grid=(L // TM,),
        in_specs=[row(op.shape[1]), row(oraw.shape[1]), row(dz.shape[1]), row(D), row(D), row(D), full(wn), full(wbf), full(wbd), full(wo)],
        out_specs=row(D), out_shape=SDS((L, D), f32), compiler_params=_cp())(op, oraw, dz, ga, gb, h0, wn, wbf, wbd, wo)


def _mix_bwd(dh1, op, oraw, dz, ga, gb, wn, wbf, wbd, wo):
    L, D = dh1.shape
    TM = _pick(L, ROW_TILES)
    WF, WD = op.shape[1], oraw.shape[1]

    def body(dh1_ref, op_ref, or_ref, dz_ref, ga_ref, gb_ref, wn_ref, wbf_ref, wbd_ref, wo_ref,
             dop_ref, dor_ref, dg_ref, af_ref, ad_ref, dpf_ref, dpd_ref, y_ref, dmix_ref, acc_ref):
        @pl.when(pl.program_id(0) == 0)
        def _():
            acc_ref[...] = jnp.zeros_like(acc_ref)

        af = _b(op_ref[...])
        ad, vjp = jax.vjp(_dn_normgate, or_ref[...], dz_ref[...], wn_ref[...])
        adb = _b(ad)
        pf, pd = _dot(af, wbf_ref[...]), _dot(adb, wbd_ref[...])
        sa, sb = _sigmoid(ga_ref[...]), _sigmoid(gb_ref[...])
        dmix = _b(dh1_ref[...])
        dy = _dot_nt(dmix, wo_ref[...])
        dpf, dpd = _b(dy * sa), _b(dy * sb)
        dor, ddz, dwn = vjp(_dot_nt(dpd, wbd_ref[...]))
        dop_ref[...] = _dot_nt(dpf, wbf_ref[...])
        dor_ref[...] = dor
        dg_ref[:, 0:WD] = _b(ddz)
        dg_ref[:, WD:WD + D] = _b(dy * pf * sa * (1.0 - sa))
        dg_ref[:, WD + D:] = _b(dy * pd * sb * (1.0 - sb))
        af_ref[...], ad_ref[...], y_ref[...] = af, adb, _b(sa * pf + sb * pd)
        dpf_ref[...], dpd_ref[...], dmix_ref[...] = dpf, dpd, dmix
        acc_ref[0:1, :] += dwn

    row = lambda wd: pl.BlockSpec((TM, wd), lambda i: (i, 0))
    full = lambda a: pl.BlockSpec(a.shape, lambda i: (0, 0))
    return pl.pallas_call(
        body, name="mix_bwd", grid=(L // TM,),
        in_specs=[row(D), row(WF), row(WD), row(WD), row(D), row(D), full(wn), full(wbf), full(wbd), full(wo)],
        out_specs=[row(WF), row(WD), row(WD + 2 * D), row(WF), row(WD), row(D), row(D), row(D), row(D),
                   pl.BlockSpec((8, HP), lambda i: (0, 0))],
        out_shape=[SDS((L, WF), f32), SDS((L, WD), f32), SDS((L, WD + 2 * D), bf16), SDS((L, WF), bf16), SDS((L, WD), bf16),
                   SDS((L, D), bf16), SDS((L, D), bf16), SDS((L, D), bf16), SDS((L, D), bf16), SDS((8, HP), f32)],
        compiler_params=_cp())(dh1, op, oraw, dz, ga, gb, wn, wbf, wbd, wo)


def _ffn_fwd_bwd(h1, tgt, w2, wf, wgt, wut, wd):
    L, D = h1.shape
    F = wd.shape[0]
    TM = _pick(L, FFN_TILES)

    def body(h_ref, t_ref, w2_ref, wf_ref, wg_hbm, wu_hbm, wd_hbm,
             dh1_ref, xn_ref, dg_ref, du_ref, act_ref, dh2_ref, acc_ref, wg_v, wu_v, wd_v, sems):
        i = pl.program_id(0)
        _load_once([(wg_hbm, wg_v), (wu_hbm, wu_v), (wd_hbm, wd_v)], sems)

        @pl.when(i == 0)
        def _():
            acc_ref[...] = jnp.zeros_like(acc_ref)

        h1v = h_ref[...]
        xn2, vjp2 = jax.vjp(_rms, h1v, w2_ref[...])
        xb = _b(xn2)
        g, u = _dot_nt(xb, wg_v[...]), _dot_nt(xb, wu_v[...])
        sg = _sigmoid(g)
        ab = _b(g * sg * u)
        h2 = h1v + _dot(ab, wd_v[...])
        out, vjpf = jax.vjp(_rms, h2, wf_ref[...])
        valid = (i * TM + _iota((TM, 1), 0)) >= PREFIX
        diff = jnp.where(valid, out - t_ref[...], 0.0)
        loss = 0.5 * jnp.sum(jnp.sum(diff * diff, axis=1, keepdims=True), axis=0, keepdims=True) / D
        dh2, dwf = vjpf(diff * (1.0 / D))
        dh2b = _b(dh2)
        dact = _dot_nt(dh2b, wd_v[...])
        dgb = _b(dact * u * (sg * (1.0 + g * (1.0 - sg))))
        dub = _b(dact * (g * sg))
        dh1n, dw2 = vjp2(_dot(dgb, wg_v[...]) + _dot(dub, wu_v[...]))
        dh1_ref[...] = dh2 + dh1n
        xn_ref[...], dg_ref[...], du_ref[...], act_ref[...], dh2_ref[...] = xb, dgb, dub, ab, dh2b
        acc_ref[0:1, :] += dw2
        acc_ref[1:2, :] += dwf
        acc_ref[2:3, :] += jnp.broadcast_to(loss, (1, D))

    row = lambda wd_: pl.BlockSpec((TM, wd_), lambda i: (i, 0))
    vec = pl.BlockSpec((1, D), lambda i: (0, 0))
    anyspec = pl.BlockSpec(memory_space=pl.ANY)
    return pl.pallas_call(
        body, name="ffn_fwd_bwd", grid=(L // TM,),
        in_specs=[row(D), row(D), vec, vec, anyspec, anyspec, anyspec],
        out_specs=[row(D), row(D), row(F), row(F), row(F), row(D), pl.BlockSpec((8, D), lambda i: (0, 0))],
        out_shape=[SDS((L, D), f32), SDS((L, D), bf16), SDS((L, F), bf16), SDS((L, F), bf16), SDS((L, F), bf16), SDS((L, D), bf16),
                   SDS((8, D), f32)],
        scratch_shapes=[pltpu.VMEM((F, D), bf16), pltpu.VMEM((F, D), bf16), pltpu.VMEM((F, D), bf16), pltpu.SemaphoreType.DMA((3,))],
        compiler_params=_cp())(h1, tgt, w2, wf, wgt, wut, wd)


def _pad_lanes(v, n=HP):
    return jnp.pad(v.astype(f32), ((0, 0), (0, n - v.shape[1])))


def _pack_w_in(wt_full):
    D = wt_full.shape[1]
    FW, DW = FOX_H * FOX_D, DN_H * DN_D
    o = 0
    parts = {}
    for name, wd in (("fq", FW), ("fk", FW), ("fv", FW), ("fl", FOX_H), ("dn", 3 * DW), ("ba", 2 * DN_H), ("dz", DW), ("ga", D), ("gb", D)):
        parts[name] = wt_full[o:o + wd]
        o += wd
    assert o == wt_full.shape[0]
    heads = lambda w: jnp.pad(w.reshape(FOX_H, FOX_D, D), ((0, 0), (0, HP - FOX_D), (0, 0))).reshape(FOX_H * HP, D)
    small = lambda w: jnp.pad(w, ((0, SMALL_W - w.shape[0]), (0, 0)))
    packed = dict(fq=heads(parts["fq"]), fk=heads(parts["fk"]), fv=heads(parts["fv"]), sf=small(parts["fl"]), sd=small(parts["ba"]),
                  dn=parts["dn"], dz=parts["dz"], ga=parts["ga"], gb=parts["gb"])
    return jnp.concatenate([packed[name] for name, _, _, _ in _seg_layout(D)], axis=0)


def _unpack_w_in(groups, d_model):
    D = groups[0].shape[1]
    FW = FOX_H * FOX_D
    segs = {}
    for grp, g in zip(GROUPS, groups):
        o = 0
        for name, wd, _, sg in _seg_layout(d_model):
            if sg == grp:
                segs[name] = g[o:o + wd]
                o += wd
    heads = lambda g: g.reshape(FOX_H, HP, D)[:, :FOX_D].reshape(FW, D)
    return jnp.concatenate([heads(segs["fq"]), heads(segs["fk"]), heads(segs["fv"]), segs["sf"][:FOX_H], segs["dn"],
                            segs["sd"][:2 * DN_H], segs["dz"], segs["ga"], segs["gb"]], axis=0)


def _local_step(x, tgt, meta, w1, w_in_t, fbias, cw, alog, dtb, wn, w2, wf, late_shards):
    T, D = x.shape
    h0 = jnp.concatenate([jnp.zeros((N_PAD, D), f32), meta, x], axis=0)
    tgt_p = jnp.concatenate([jnp.zeros((PREFIX, D), f32), tgt], axis=0)
    wp = _pack_w_in(w_in_t)
    bias_p, alog_p, dt_p = _pad_lanes(fbias), _pad_lanes(jnp.pad(alog, ((0, 0), (DN_H, 0)))), _pad_lanes(jnp.pad(dtb, ((0, 0), (DN_H, 0))))

    (xn, fq, fk, sf, fv, dn, sd, dz, ga, gb), g_mix = _in_proj(h0, w1, wp, [late_shards[n] for n in LATE_MIX])
    qa, ka, va = _fox_prep(fq, fk, fv, sf, bias_p)
    op, g_ffn = _fox_fwd(qa, ka, va, [late_shards[n] for n in LATE_FFN])
    full = {n: _from_slabs(n, s) for n, s in zip(LATE_MIX + LATE_FFN, tuple(g_mix) + tuple(g_ffn))}
    wbf, wbd, wo, wgt, wut, wd = (full[n] for n in ("w_branch_fox", "w_branch_dn", "w_out", "w_ffn_gate", "w_ffn_up", "w_ffn_down"))
    wbf_p = jnp.pad(wbf.reshape(FOX_H, FOX_D, D), ((0, 0), (0, HP - FOX_D), (0, 0))).reshape(FOX_H * HP, D)
    qn, kn, vn, bg = _dn_prep(dn, sd, cw, alog_p, dt_p)
    u_dn, w_dn, qd_dn, kd_dn, at_dn, gl_dn, x_dn = _dn_intra(qn, kn, vn, bg)
    oraw, vnew, states = _dn_scan(u_dn, w_dn, qd_dn, kd_dn, at_dn, gl_dn)
    h1 = _mix_fwd(op, oraw, dz, ga, gb, h0, wn, wbf_p, wbd, wo)

    dh1, xn2, dgate, dup, act, dh2, acc_f = _ffn_fwd_bwd(h1, tgt_p, w2, wf, wgt, wut, wd)
    g_wg, g_wu, g_wd = _matmul_tn(dgate, xn2, "dw_ffn_gate"), _matmul_tn(dup, xn2, "dw_ffn_up"), _matmul_tn(act, dh2, "dw_ffn_down")

    dop, dor, d_mix, af, ad, dpf, dpd, yb, dmix, acc_m = _mix_bwd(dh1, op, oraw, dz, ga, gb, wn, wbf_p, wbd, wo)
    g_wbf = _matmul_tn(af, dpf, "dw_branch_fox").reshape(FOX_H, HP, D)[:, :FOX_D].reshape(FOX_H * FOX_D, D)
    g_wbd, g_wo = _matmul_tn(ad, dpd, "dw_branch_dn"), _matmul_tn(yb, dmix, "dw_out")

    dvnew, dstates = _dn_scan_bwd(dor, w_dn, qd_dn, kd_dn, at_dn, gl_dn)
    dqn, dkn, dvn, dbg = _dn_intra_bwd(qn, kn, vn, bg, x_dn, dor, vnew, dvnew, states, dstates)
    d_dn, acc_cw, acc_p = _dn_prep_bwd(dn, sd, cw, alog_p, dt_p, dqn, dkn, dvn, dbg)
    g_late = dict(w_branch_fox=g_wbf, w_branch_dn=g_wbd, w_out=g_wo, w_ffn_gate=g_wg, w_ffn_up=g_wu, w_ffn_down=g_wd)
    (dqa, dka, d_fv), recv = _fox_bwd(qa, ka, va, op, dop, [_to_slabs(n, g_late[n]) for n in LATE])
    d_fox, acc_b = _fox_prep_bwd(dqa, dka, sf, bias_p)

    dgroups = [d_fox, d_fv, d_dn, d_mix]
    g_wp = [_matmul_tn(dg, xn, "dw_in_" + grp) for grp, dg in zip(GROUPS, dgroups)]
    dh0, acc_1, (recv_w_in,) = _in_proj_bwd(dgroups, wp, h0, w1, dh1, [_to_slabs("w_in", _unpack_w_in(g_wp, D))])
    recv = dict(zip(LATE, recv), w_in=recv_w_in)

    small = dict(loss=acc_f[2, 0:1], mix_norm_w=acc_1[0], fox_forget_bias=acc_b[0, :FOX_H], dn_a_log=acc_p[0, DN_H:2 * DN_H],
                 dn_dt_bias=acc_p[1, DN_H:2 * DN_H], dn_out_norm_w=acc_m[0], ffn_norm_w=acc_f[0], final_norm_w=acc_f[1],
                 meta_tokens=dh0[N_PAD:PREFIX].reshape(-1), dn_conv_w=acc_cw[:CONV_K].reshape(-1))
    return dh0[PREFIX:], small, recv


def _mesh_pos():
    x, y, c = lax.axis_index("x"), lax.axis_index("y"), lax.axis_index("c")
    return x, y, c, 4 * x + 2 * y + c


def _peer(x, y, c, m):
    flip = lambda v, on: 1 - v if on else v
    px, py, pc = flip(x, m & 4), flip(y, m & 2), flip(c, m & 1)
    return (px, py, pc), 4 * px + 2 * py + pc


def _exchange_sems(n):
    return [pltpu.SemaphoreType.DMA((n, N_DEV - 1)), pltpu.SemaphoreType.DMA((n, N_DEV - 1)), pltpu.SemaphoreType.DMA((n,))]


def _exchange_copies(ins, outs, send_sems, recv_sems, loc_sems, gather, with_receives):
    x, y, c, me = _mesh_pos()
    src = lambda a, pid: ins[a] if gather else ins[a].at[pid]
    local = [pltpu.make_async_copy(src(a, me), outs[a].at[me], loc_sems.at[a]) for a in range(len(ins))]
    sends, recvs = [], []
    for m in range(1, N_DEV):
        peer, pid = _peer(x, y, c, m)
        for a in range(len(ins)):
            kw = dict(send_sem=send_sems.at[a, m - 1], recv_sem=recv_sems.at[a, m - 1], device_id=peer, device_id_type=MESH)
            sends.append(pltpu.make_async_remote_copy(src_ref=src(a, pid), dst_ref=outs[a].at[me], **kw))
            if with_receives:
                recvs.append(pltpu.make_async_remote_copy(src_ref=src(a, pid), dst_ref=outs[a].at[pid], **kw))
    return local, sends, recvs


def _exchange_start(ins, outs, send_sems, recv_sems, loc_sems, gather):
    local, sends, _ = _exchange_copies(ins, outs, send_sems, recv_sems, loc_sems, gather, with_receives=False)
    for cp in local + sends:
        cp.start()


def _exchange_wait(ins, outs, send_sems, recv_sems, loc_sems, gather):
    local, sends, recvs = _exchange_copies(ins, outs, send_sems, recv_sems, loc_sems, gather, with_receives=True)
    for cp in recvs:
        cp.wait_recv()
    for cp in sends:
        cp.wait_send()
    for cp in local:
        cp.wait()


def _gather_two_level(arrays, name):
    n = len(arrays)

    def body(*refs):
        ins, outs, (send_sems, recv_sems, loc_sems) = refs[:n], refs[n:2 * n], refs[2 * n:]
        x, y, c, me = _mesh_pos()
        sib = (x, y, 1 - c)
        chips = [(1 - x, y), (x, 1 - y), (1 - x, 1 - y)]
        dev_id = lambda px, py, pc: 4 * px + 2 * py + pc

        def copy(a, k, block, to, own=False):
            return pltpu.make_async_remote_copy(src_ref=ins[a] if own else outs[a].at[block], dst_ref=outs[a].at[block],
                                                send_sem=send_sems.at[a, k], recv_sem=recv_sems.at[a, k], device_id=to, device_id_type=MESH)

        local = [pltpu.make_async_copy(ins[a], outs[a].at[me], loc_sems.at[a]) for a in range(n)]
        first = [copy(a, 0, me, sib, own=True) for a in range(n)]
        first += [copy(a, 1 + j, me, (*chip, c), own=True) for j, chip in enumerate(chips) for a in range(n)]
        for cp in local + first:
            cp.start()
        passed = []
        for j, chip in enumerate(chips):
            for a in range(n):
                copy(a, 1 + j, dev_id(*chip, c), sib).wait_recv()
                cp = copy(a, 4 + j, dev_id(*chip, c), sib)
                cp.start()
                passed.append(cp)
        for a in range(n):
            copy(a, 0, dev_id(x, y, 1 - c), sib).wait_recv()
        for j, chip in enumerate(chips):
            for a in range(n):
                copy(a, 4 + j, dev_id(*chip, 1 - c), sib).wait_recv()
        for cp in first + passed:
            cp.wait_send()
        for cp in local:
            cp.wait()

    anyspec = pl.BlockSpec(memory_space=pl.ANY)
    return pl.pallas_call(
        body, name=name, in_specs=[anyspec] * n, out_specs=[anyspec] * n,
        out_shape=[SDS((N_DEV,) + a.shape, a.dtype) for a in arrays],
        scratch_shapes=_exchange_sems(n))(*arrays)


def _all_reduce_small(v):
    R = v.shape[0]

    def body(v_ref, o_ref, gath, send_sems, recv_sems):
        x, y, c, me = _mesh_pos()
        gath[me] = v_ref[...]
        sends = []
        for m in range(1, N_DEV):
            peer, _ = _peer(x, y, c, m)
            cp = pltpu.make_async_remote_copy(src_ref=v_ref, dst_ref=gath.at[me], send_sem=send_sems.at[m - 1],
                                              recv_sem=recv_sems.at[m - 1], device_id=peer, device_id_type=MESH)
            cp.start()
            sends.append(cp)
        for m in range(1, N_DEV):
            peer, pid = _peer(x, y, c, m)
            pltpu.make_async_remote_copy(src_ref=v_ref, dst_ref=gath.at[pid], send_sem=send_sems.at[m - 1],
                                         recv_sem=recv_sems.at[m - 1], device_id=peer, device_id_type=MESH).wait_recv()
        for cp in sends:
            cp.wait_send()
        tot = gath[0]
        for d in range(1, N_DEV):
            tot = tot + gath[d]
        o_ref[...] = tot

    vm = pl.BlockSpec(memory_space=pltpu.VMEM)
    return pl.pallas_call(
        body, name="all_reduce_small", in_specs=[vm], out_specs=vm, out_shape=SDS((R, HP), f32),
        scratch_shapes=[pltpu.VMEM((N_DEV, R, HP), f32), pltpu.SemaphoreType.DMA((N_DEV - 1,)), pltpu.SemaphoreType.DMA((N_DEV - 1,))],
        )(v)


def _adamw_math(w, g, m, v):
    m = ADAM_B1 * m + (1.0 - ADAM_B1) * g
    v = ADAM_B2 * v + (1.0 - ADAM_B2) * (g * g)
    m_hat = m / (1.0 - ADAM_B1 ** ADAM_STEP)
    v_hat = v / (1.0 - ADAM_B2 ** ADAM_STEP)
    return -ADAM_LR * (m_hat / (jnp.sqrt(v_hat) + ADAM_EPS) + ADAM_WD * w), m, v


def _adamw(g, w, m, v, name):
    R, Cc = w.shape[-2:]
    if R <= 512 or R % 128 == 0:
        TR, TC = (R if R <= 512 else _pick(R, (256, 128))), Cc
    else:
        TR, TC = R, _pick(Cc, (256, 128))
    slabs = g.ndim == 3
    lead = w.ndim - 2

    def body(g_ref, w_ref, m_ref, v_ref, go_ref, d_ref, mo_ref, vo_ref):
        if slabs:
            gs = g_ref[0].astype(f32)
            for k in range(1, N_DEV):
                gs = gs + g_ref[k].astype(f32)
        else:
            gs = g_ref[...]
        at = 0 if lead else Ellipsis
        d, mn, vn = _adamw_math(w_ref[at], gs, m_ref[at], v_ref[at])
        go_ref[at], d_ref[at], mo_ref[at], vo_ref[at] = gs, d, mn, vn

    grid = (R // TR, Cc // TC)
    blk = pl.BlockSpec((1,) * lead + (TR, TC), lambda i, j: (0,) * lead + (i, j))
    gblk = pl.BlockSpec((N_DEV, TR, TC), lambda i, j: (0, i, j)) if slabs else pl.BlockSpec((TR, TC), lambda i, j: (i, j))
    return pl.pallas_call(
        body, name=name, grid=grid, in_specs=[gblk, blk, blk, blk], out_specs=[blk] * 4,
        out_shape=[SDS(w.shape, f32)] * 4, compiler_params=_cp(2))(g, w, m, v)


WEIGHTS = ("meta_tokens", "mix_norm_w", "w_in", "fox_forget_bias", "dn_conv_w", "dn_a_log", "dn_dt_bias", "dn_out_norm_w",
           "w_branch_fox", "w_branch_dn", "w_out", "ffn_norm_w", "w_ffn_gate", "w_ffn_up", "w_ffn_down", "final_norm_w")
COL_SHARDED = ("w_in", "w_branch_fox", "w_branch_dn", "w_ffn_gate", "w_ffn_up")
ROW_SHARDED = ("w_out", "w_ffn_down")
BIG = COL_SHARDED + ROW_SHARDED
LATE = tuple(n for n in BIG if n != "w_in")
LATE_MIX = ("w_branch_fox", "w_branch_dn", "w_out")
LATE_FFN = ("w_ffn_gate", "w_ffn_up", "w_ffn_down")
SMALL = tuple(n for n in WEIGHTS if n not in BIG)
TRANSPOSED = ("w_in", "w_ffn_gate", "w_ffn_up")


def _to_slabs(name, g):
    r, c = g.shape
    if name in COL_SHARDED and name not in TRANSPOSED:
        return _b(g.reshape(r, N_DEV, c // N_DEV).transpose(1, 0, 2))
    return _b(g.reshape(N_DEV, r // N_DEV, c))


def _from_slabs(name, s):
    n, r, c = s.shape
    if name in COL_SHARDED and name not in TRANSPOSED:
        return s.transpose(1, 0, 2).reshape(r, n * c)
    return s.reshape(n * r, c)


def kernel(x, meta_tokens, mix_norm_w, w_in, fox_forget_bias, dn_conv_w, dn_a_log, dn_dt_bias, dn_out_norm_w, w_branch_fox, w_branch_dn, w_out, ffn_norm_w, w_ffn_gate, w_ffn_up, w_ffn_down, final_norm_w, loss_target, m_meta_tokens, m_mix_norm_w, m_w_in, m_fox_forget_bias, m_dn_conv_w, m_dn_a_log, m_dn_dt_bias, m_dn_out_norm_w, m_w_branch_fox, m_w_branch_dn, m_w_out, m_ffn_norm_w, m_w_ffn_gate, m_w_ffn_up, m_w_ffn_down, m_final_norm_w, v_meta_tokens, v_mix_norm_w, v_w_in, v_fox_forget_bias, v_dn_conv_w, v_dn_a_log, v_dn_dt_bias, v_dn_out_norm_w, v_w_branch_fox, v_w_branch_dn, v_w_out, v_ffn_norm_w, v_w_ffn_gate, v_w_ffn_up, v_w_ffn_down, v_final_norm_w):
    w = dict(meta_tokens=meta_tokens, mix_norm_w=mix_norm_w, w_in=w_in, fox_forget_bias=fox_forget_bias, dn_conv_w=dn_conv_w, dn_a_log=dn_a_log, dn_dt_bias=dn_dt_bias, dn_out_norm_w=dn_out_norm_w, w_branch_fox=w_branch_fox, w_branch_dn=w_branch_dn, w_out=w_out, ffn_norm_w=ffn_norm_w, w_ffn_gate=w_ffn_gate, w_ffn_up=w_ffn_up, w_ffn_down=w_ffn_down, final_norm_w=final_norm_w)
    mom = dict(meta_tokens=m_meta_tokens, mix_norm_w=m_mix_norm_w, w_in=m_w_in, fox_forget_bias=m_fox_forget_bias, dn_conv_w=m_dn_conv_w, dn_a_log=m_dn_a_log, dn_dt_bias=m_dn_dt_bias, dn_out_norm_w=m_dn_out_norm_w, w_branch_fox=m_w_branch_fox, w_branch_dn=m_w_branch_dn, w_out=m_w_out, ffn_norm_w=m_ffn_norm_w, w_ffn_gate=m_w_ffn_gate, w_ffn_up=m_w_ffn_up, w_ffn_down=m_w_ffn_down, final_norm_w=m_final_norm_w)
    var = dict(meta_tokens=v_meta_tokens, mix_norm_w=v_mix_norm_w, w_in=v_w_in, fox_forget_bias=v_fox_forget_bias, dn_conv_w=v_dn_conv_w, dn_a_log=v_dn_a_log, dn_dt_bias=v_dn_dt_bias, dn_out_norm_w=v_dn_out_norm_w, w_branch_fox=v_w_branch_fox, w_branch_dn=v_w_branch_dn, w_out=v_w_out, ffn_norm_w=v_ffn_norm_w, w_ffn_gate=v_w_ffn_gate, w_ffn_up=v_w_ffn_up, w_ffn_down=v_w_ffn_down, final_norm_w=v_final_norm_w)
    two_d = lambda a: a.reshape(a.shape[-2:]) if a.ndim >= 2 else a.reshape(1, -1)
    me = 4 * lax.axis_index("x") + 2 * lax.axis_index("y") + lax.axis_index("c")
    for d in (w, mom, var):
        for n in TRANSPOSED:
            d[n] = jnp.swapaxes(d[n], -1, -2)

    g_in, g_meta, g_cw = _gather_two_level([_b(two_d(w["w_in"])), two_d(w["meta_tokens"]), two_d(w["dn_conv_w"])], "all_gather_early")
    meta = g_meta.transpose(1, 0, 2).reshape(N_META, -1)
    cw = g_cw.transpose(1, 0, 2).reshape(CONV_K, -1)

    gx, g_small, recv = _local_step(
        x[0], loss_target[0], meta, two_d(w["mix_norm_w"]), _from_slabs("w_in", g_in), two_d(w["fox_forget_bias"]), cw, two_d(w["dn_a_log"]),
        two_d(w["dn_dt_bias"]), two_d(w["dn_out_norm_w"]), two_d(w["ffn_norm_w"]), two_d(w["final_norm_w"]),
        {n: _b(two_d(w[n])) for n in LATE})

    order = ("loss",) + SMALL
    flat = jnp.concatenate([g_small[n].reshape(-1) for n in order])
    rows = -(-flat.shape[0] // (8 * HP)) * 8
    tot = _all_reduce_small(jnp.pad(flat, (0, rows * HP - flat.shape[0])).reshape(rows, HP)).reshape(-1)
    summed, o = {}, 0
    for n in order:
        k = g_small[n].shape[0]
        summed[n] = tot[o:o + k]
        o += k
    loss = summed["loss"][0]
    d_model = x.shape[-1]
    mcols, ccols = d_model // N_DEV, dn_conv_w.shape[-1]
    summed["meta_tokens"] = lax.dynamic_slice(summed["meta_tokens"].reshape(N_META, d_model), (0, me * mcols), (N_META, mcols)).reshape(-1)
    summed["dn_conv_w"] = lax.dynamic_slice(summed["dn_conv_w"].reshape(CONV_K, ccols * N_DEV), (0, me * ccols), (CONV_K, ccols)).reshape(-1)

    res = {}
    for n in BIG:
        res[n] = _adamw(recv[n], w[n], mom[n], var[n], "adamw_" + n)
        if n in TRANSPOSED:
            res[n] = [jnp.swapaxes(r, -1, -2) for r in res[n]]
    sizes = [summed[n].shape[0] for n in SMALL]
    srows = -(-sum(sizes) // (8 * HP)) * 8
    pack = lambda d: jnp.pad(jnp.concatenate([d[n].reshape(-1) for n in SMALL]), (0, srows * HP - sum(sizes))).reshape(srows, HP)
    sres = _adamw(pack(summed), pack(w), pack(mom), pack(var), "adamw_small")
    o = 0
    for n, k in zip(SMALL, sizes):
        res[n] = [r.reshape(-1)[o:o + k].reshape(w[n].shape) for r in sres]
        o += k
    return (loss, gx[None], *[res[n][0] for n in WEIGHTS], *[res[n][1] for n in WEIGHTS], *[res[n][2] for n in WEIGHTS], *[res[n][3] for n in WEIGHTS])
```

```python
import functools

import jax
import jax.numpy as jnp
from jax import lax
from jax.experimental import pallas as pl
from jax.experimental.pallas import tpu as pltpu

f32, bf16 = jnp.float32, jnp.bfloat16
HI = lax.Precision.HIGHEST
MESH = pl.DeviceIdType.MESH
SDS = jax.ShapeDtypeStruct

N_DEV = 8
N_META = 16
PREFIX = 128
N_PAD = PREFIX - N_META
FOX_H, FOX_D = 8, 64
DN_H, DN_D = 4, 128
DN_C = 64
CONV_K = 4
HP = 128
SMALL_W = 256
EPS = 1e-6
NEG = -1e30
C_Q0, C_K0 = 64, 67
LSE_COL = 64
LOG2E, LN2 = 1.4426950408889634, 0.6931471805599453
C_LSE0, C_DELTA0 = 70, 65

ADAM_LR, ADAM_B1, ADAM_B2, ADAM_EPS, ADAM_WD, ADAM_STEP = 0.001, 0.9, 0.999, 1e-08, 0.01, 10

VMEM_LIMIT_V7X = 56 * 1024 * 1024
ROW_TILES = (384, 128)
ATTN_TILES = (384, 128)
FFN_TILES = (192, 64)
FOX_HEAD_GROUP = 4
FOX_HEAD_GROUP_FWD = 4
MAX_WGRAD_BLOCK = 1408
DN_INTRA_GROUP = (6, 3, 2, 1)
DN_SCAN_GROUP = (6, 3, 2, 1)


def _pick(n, cands):
    for c in cands:
        if n % c == 0:
            return c
    raise ValueError(f"no tile of {cands} divides {n}")


def _cp(n_axes=1):
    return pltpu.CompilerParams(dimension_semantics=("arbitrary",) * n_axes, vmem_limit_bytes=VMEM_LIMIT_V7X)


def _b(x):
    return x.astype(bf16)


def _dot(a, b):
    return jnp.dot(a, b, preferred_element_type=f32)


def _dot_nt(a, b):
    return lax.dot_general(a, b, (((1,), (1,)), ((), ())), preferred_element_type=f32)


def _dot_tn(a, b):
    return lax.dot_general(a, b, (((0,), (0,)), ((), ())), preferred_element_type=f32)


def _dot_hi(a, b):
    return jnp.dot(a, b, preferred_element_type=f32, precision=HI)


def _iota(shape, dim):
    return lax.broadcasted_iota(jnp.int32, shape, dim)


def _rms(x, w):
    return x * lax.rsqrt(jnp.mean(x * x, axis=-1, keepdims=True) + EPS) * w


def _sigmoid(x):
    return jax.nn.sigmoid(x)


def _load_once(pairs, sems):
    @pl.when(pl.program_id(0) == 0)
    def _():
        cps = [pltpu.make_async_copy(src, dst, sems.at[k]) for k, (src, dst) in enumerate(pairs)]
        for cp in cps:
            cp.start()
        for cp in cps:
            cp.wait()


def _seg_layout(d_model):
    return (("fq", FOX_H * HP, bf16, "fox"), ("fk", FOX_H * HP, bf16, "fox"), ("sf", SMALL_W, f32, "fox"),
            ("fv", FOX_H * HP, bf16, "fv"),
            ("dn", 3 * DN_H * DN_D, f32, "dn"), ("sd", SMALL_W, f32, "dn"),
            ("dz", DN_H * DN_D, f32, "mix"), ("ga", d_model, f32, "mix"), ("gb", d_model, f32, "mix"))


GROUPS = ("fox", "fv", "dn", "mix")


def _group_widths(d_model):
    return [sum(wd for _, wd, _, g in _seg_layout(d_model) if g == grp) for grp in GROUPS]


def _in_proj(h0, w1, wpt, shards):
    L, D = h0.shape
    NP = wpt.shape[0]
    TM = _pick(L, ROW_TILES)
    NT = L // TM
    segs = _seg_layout(D)
    ns, n = len(segs), len(shards)
    offs, o = [], 0
    for _, wd, _, _ in segs:
        offs.append(o)
        o += wd
    assert o == NP

    def body(h_ref, w1_ref, wp_hbm, *rest):
        ins, xn_ref, outs, gouts = rest[:n], rest[n], rest[n + 1:n + 1 + ns], rest[n + 1 + ns:2 * n + 1 + ns]
        wp_v, sems = rest[2 * n + 1 + ns:2 * n + 3 + ns]
        xsems = rest[2 * n + 3 + ns:]
        _load_once([(wp_hbm, wp_v)], sems)

        @pl.when(pl.program_id(0) == 0)
        def _():
            _exchange_start(ins, gouts, *xsems, gather=True)

        xn = _b(_rms(h_ref[...], w1_ref[...]))
        xn_ref[...] = xn
        for o_ref, off, (_, wd, _, _) in zip(outs, offs, segs):
            o_ref[...] = _dot_nt(xn, wp_v[off:off + wd, :]).astype(o_ref.dtype)

        @pl.when(pl.program_id(0) == NT - 1)
        def _():
            _exchange_wait(ins, gouts, *xsems, gather=True)

    row = lambda wd: pl.BlockSpec((TM, wd), lambda i: (i, 0))
    anyspec = pl.BlockSpec(memory_space=pl.ANY)
    res = pl.pallas_call(
        body, name="in_proj", grid=(NT,),
        in_specs=[row(D), pl.BlockSpec((1, D), lambda i: (0, 0)), anyspec] + [anyspec] * n,
        out_specs=[row(D)] + [row(wd) for _, wd, _, _ in segs] + [anyspec] * n,
        out_shape=[SDS((L, D), bf16)] + [SDS((L, wd), dt) for _, wd, dt, _ in segs] + [SDS((N_DEV,) + a.shape, a.dtype) for a in shards],
        scratch_shapes=[pltpu.VMEM((NP, D), bf16), pltpu.SemaphoreType.DMA((1,))] + _exchange_sems(n),
        compiler_params=_cp())(h0, w1, wpt, *shards)
    return res[:1 + ns], res[1 + ns:]


def _in_proj_bwd(dgroups, wpt, h0, w1, dh1, slabs):
    L, D = h0.shape
    NP = wpt.shape[0]
    TM = _pick(L, ROW_TILES)
    NT = L // TM
    widths = [g.shape[1] for g in dgroups]
    assert sum(widths) == NP
    ng, n = len(dgroups), len(slabs)

    def body(*refs):
        dg_refs, (wp_hbm, h_ref, w1_ref, dh1_ref) = refs[:ng], refs[ng:ng + 4]
        ins, (dh0_ref, acc_ref), outs = refs[ng + 4:ng + 4 + n], refs[ng + 4 + n:ng + 6 + n], refs[ng + 6 + n:ng + 6 + 2 * n]
        wp_v, sems = refs[ng + 6 + 2 * n:ng + 8 + 2 * n]
        xsems = refs[ng + 8 + 2 * n:]
        _load_once([(wp_hbm, wp_v)], sems)

        @pl.when(pl.program_id(0) == 0)
        def _():
            acc_ref[...] = jnp.zeros_like(acc_ref)
            _exchange_start(ins, outs, *xsems, gather=False)

        dxn, off = None, 0
        for g_ref, wd in zip(dg_refs, widths):
            part = _dot(g_ref[...], wp_v[off:off + wd, :])
            dxn = part if dxn is None else dxn + part
            off += wd
        _, vjp = jax.vjp(_rms, h_ref[...], w1_ref[...])
        dh0n, dw1 = vjp(dxn)
        dh0_ref[...] = dh1_ref[...] + dh0n
        acc_ref[0:1, :] += dw1

        @pl.when(pl.program_id(0) == NT - 1)
        def _():
            _exchange_wait(ins, outs, *xsems, gather=False)

    row = lambda wd: pl.BlockSpec((TM, wd), lambda i: (i, 0))
    anyspec = pl.BlockSpec(memory_space=pl.ANY)
    res = pl.pallas_call(
        body, name="in_proj_bwd", grid=(NT,),
        in_specs=[row(wd) for wd in widths] + [anyspec, row(D), pl.BlockSpec((1, D), lambda i: (0, 0)), row(D)] + [anyspec] * n,
        out_specs=[row(D), pl.BlockSpec((8, D), lambda i: (0, 0))] + [anyspec] * n,
        out_shape=[SDS((L, D), f32), SDS((8, D), f32)] + [SDS(a.shape, a.dtype) for a in slabs],
        scratch_shapes=[pltpu.VMEM((NP, D), bf16), pltpu.SemaphoreType.DMA((1,))] + _exchange_sems(n),
        compiler_params=_cp())(*dgroups, wpt, h0, w1, dh1, *slabs)
    return res[0], res[1], res[2:]


def _matmul_tn(a, b, name):
    L, R = a.shape
    C = b.shape[1]
    br = max(k for k in range(HP, MAX_WGRAD_BLOCK + 1, HP) if R % k == 0)

    def body(a_ref, b_ref, o_ref):
        o_ref[...] = _b(_dot_tn(a_ref[...], b_ref[...]))

    return pl.pallas_call(
        body, name=name, grid=(R // br,),
        in_specs=[pl.BlockSpec((L, br), lambda r: (0, r)), pl.BlockSpec((L, C), lambda r: (0, 0))],
        out_specs=pl.BlockSpec((br, C), lambda r: (r, 0)), out_shape=SDS((R, C), bf16), compiler_params=_cp())(a, b)


def _fox_prep(fq, fk, fv, sf, bias_p):
    L = fq.shape[0]
    T = HP
    NT = L // T
    W = FOX_H * HP

    def body(fq_ref, fk_ref, fv_ref, sf_ref, b_ref, qa_ref, ka_ref, va_ref, carry):
        @pl.when(pl.program_id(0) == 0)
        def _():
            carry[...] = jnp.zeros_like(carry)

        lane, row = _iota((T, HP), 1), _iota((T, HP), 0)
        logf = jnp.where(lane < FOX_H, jax.nn.log_sigmoid(sf_ref[...] + b_ref[...]), 0.0)
        c = _dot_hi((row >= lane).astype(f32), logf) + carry[...]
        carry[...] = jnp.sum(jnp.where(row == T - 1, c, 0.0), axis=0, keepdims=True)
        ones_q = jnp.where((lane >= C_K0) & (lane < C_K0 + 3), 1.0, 0.0)
        ones_k = jnp.where(((lane >= C_Q0) & (lane < C_Q0 + 3)) | ((lane >= C_LSE0) & (lane < C_LSE0 + 3)), 1.0, 0.0)
        ones_v = _b(jnp.where((lane >= LSE_COL) & (lane < C_DELTA0 + 3), 1.0, 0.0))
        for h in range(FOX_H):
            ch = jnp.broadcast_to(jnp.sum(jnp.where(lane == h, c, 0.0), axis=1, keepdims=True), (T, HP)) * LOG2E
            c1 = _b(ch).astype(f32)
            c2 = _b(ch - c1).astype(f32)
            c3 = _b(ch - c1 - c2).astype(f32)
            cq = jnp.where(lane == C_Q0, c1, 0.0) + jnp.where(lane == C_Q0 + 1, c2, 0.0) + jnp.where(lane == C_Q0 + 2, c3, 0.0)
            ck = jnp.where(lane == C_K0, c1, 0.0) + jnp.where(lane == C_K0 + 1, c2, 0.0) + jnp.where(lane == C_K0 + 2, c3, 0.0)
            q = fq_ref[:, h * HP:(h + 1) * HP].astype(f32) * (FOX_D ** -0.5 * LOG2E)
            k = fk_ref[:, h * HP:(h + 1) * HP].astype(f32)
            qa_ref[h] = _b(q + cq + ones_q)
            ka_ref[h] = _b(k + ones_k - ck)
            va_ref[:, h * HP:(h + 1) * HP] = fv_ref[:, h * HP:(h + 1) * HP] + ones_v

    wide = pl.BlockSpec((T, W), lambda i: (i, 0))
    return pl.pallas_call(
        body, name="fox_prep", grid=(NT,),
        in_specs=[wide, wide, wide, pl.BlockSpec((T, HP), lambda i: (i, 0)), pl.BlockSpec((1, HP), lambda i: (0, 0))],
        out_specs=[pl.BlockSpec((FOX_H, T, HP), lambda i: (0, i, 0))] * 2 + [wide],
        out_shape=[SDS((FOX_H, L, HP), bf16)] * 2 + [SDS((L, W), bf16)],
        scratch_shapes=[pltpu.VMEM((1, HP), f32)], compiler_params=_cp())(fq, fk, fv, sf, bias_p)


def _fox_prep_bwd(dqa, dka, sf, bias_p):
    L = sf.shape[0]
    T = HP
    NT = L // T
    rev = lambda i: (NT - 1 - i, 0)

    W = FOX_H * HP

    def body(dq_ref, dk_ref, sf_ref, b_ref, dg_ref, db_ref, carry):
        @pl.when(pl.program_id(0) == 0)
        def _():
            carry[...] = jnp.zeros_like(carry)
            db_ref[...] = jnp.zeros_like(db_ref)

        dq, dk = dq_ref[...], dk_ref[...]
        dg_ref[:, 0:W] = _b(dq * (FOX_D ** -0.5))
        dg_ref[:, W:2 * W] = _b(dk * LN2)
        lane, row = _iota((T, HP), 1), _iota((T, HP), 0)
        dc = jnp.zeros((T, HP), f32)
        for h in range(FOX_H):
            col = jnp.sum(jnp.where(lane == C_Q0, dq[:, h * HP:(h + 1) * HP], 0.0)
                          - jnp.where(lane == C_K0, dk[:, h * HP:(h + 1) * HP], 0.0), axis=1, keepdims=True)
            dc = dc + jnp.where(lane == h, col, 0.0)
        dl = _dot_hi((row <= lane).astype(f32), dc) + carry[...]
        carry[...] = jnp.sum(jnp.where(row == 0, dl, 0.0), axis=0, keepdims=True)
        dx = jnp.where(lane < FOX_H, dl * _sigmoid(-(sf_ref[...] + b_ref[...])), 0.0)
        dg_ref[:, 2 * W:2 * W + HP] = _b(dx)
        dg_ref[:, 2 * W + HP:] = jnp.zeros((T, SMALL_W - HP), bf16)
        db_ref[0:1, :] += jnp.sum(dx, axis=0, keepdims=True)

    return pl.pallas_call(
        body, name="fox_prep_bwd", grid=(NT,),
        in_specs=[pl.BlockSpec((T, W), rev), pl.BlockSpec((T, W), rev), pl.BlockSpec((T, HP), rev), pl.BlockSpec((1, HP), lambda i: (0, 0))],
        out_specs=[pl.BlockSpec((T, 2 * W + SMALL_W), rev), pl.BlockSpec((8, HP), lambda i: (0, 0))],
        out_shape=[SDS((L, 2 * W + SMALL_W), bf16), SDS((8, HP), f32)],
        scratch_shapes=[pltpu.VMEM((1, HP), f32)], compiler_params=_cp())(dqa, dka, sf, bias_p)


def _tile_start(j, T):
    return j * T if isinstance(j, int) else pl.multiple_of(j * T, T)


def _spread3(x, lane, col0):
    x1 = _b(x).astype(f32)
    x2 = _b(x - x1).astype(f32)
    x3 = _b(x - x1 - x2).astype(f32)
    return jnp.where(lane == col0, x1, 0.0) + jnp.where(lane == col0 + 1, x2, 0.0) + jnp.where(lane == col0 + 2, x3, 0.0)


def _fox_fwd(qa, ka, fv, shards):
    L = qa.shape[1]
    TQ = TK = _pick(L, ATTN_TILES)
    NQ = L // TQ
    n = len(shards)
    HG = FOX_HEAD_GROUP_FWD

    def body(q_ref, k_ref, v_ref, *rest):
        ins, o_ref, qb_ref, outs, sems = rest[:n], rest[n], rest[n + 1], rest[n + 2:2 * n + 2], rest[2 * n + 2:]
        h, i = pl.program_id(0), pl.program_id(1)

        @pl.when((h == 0) & (i == 0))
        def _():
            _exchange_start(ins, outs, *sems, gather=True)

        qs = [q_ref[a] for a in range(HG)]
        rowg = i * TQ + _iota((TQ, TK), 0)
        colb = _iota((TQ, TK), 1)

        def step(j, carry, masked):
            ms, accs = carry
            k0 = _tile_start(j, TK)
            ss = [_dot_nt(qs[a], k_ref[a, pl.ds(k0, TK), :]) for a in range(HG)]
            if masked:
                colg = colb + j * TK
                keep = (colg <= rowg) & (colg >= N_PAD)
                ss = [jnp.where(keep, s, NEG) for s in ss]
            m_new = [jnp.maximum(m, jnp.max(s, axis=1, keepdims=True)) for m, s in zip(ms, ss)]
            ps = [_b(jnp.exp2(s - m)) for s, m in zip(ss, m_new)]
            alphas = [jnp.exp2(m - mn) for m, mn in zip(ms, m_new)]
            accs = [al * acc + _dot(p, v_ref[pl.ds(k0, TK), a * HP:(a + 1) * HP]) for a, (al, acc, p) in enumerate(zip(alphas, accs, ps))]
            return m_new, accs

        init = ([jnp.full((TQ, 1), NEG, f32)] * HG, [jnp.zeros((TQ, HP), f32)] * HG)
        carry = step(0, init, True)
        carry = lax.fori_loop(1, i, functools.partial(step, masked=False), carry)
        ms, accs = lax.fori_loop(jnp.maximum(i, 1), i + 1, functools.partial(step, masked=True), carry)
        lane = _iota((TQ, HP), 1)
        for a in range(HG):
            l = jnp.sum(jnp.where(lane == LSE_COL, accs[a], 0.0), axis=1, keepdims=True)
            lse = ms[a] + jnp.log2(l)
            o_ref[:, a * HP:(a + 1) * HP] = jnp.where(lane == LSE_COL, lse, accs[a] / l)
            qb_ref[a] = _b(qs[a].astype(f32) - _spread3(jnp.broadcast_to(lse, (TQ, HP)), lane, C_LSE0))

        @pl.when((h == FOX_H // HG - 1) & (i == NQ - 1))
        def _():
            _exchange_wait(ins, outs, *sems, gather=True)

    anyspec = pl.BlockSpec(memory_space=pl.ANY)
    qtile = pl.BlockSpec((HG, TQ, HP), lambda h, i: (h, i, 0))
    res = pl.pallas_call(
        body, name="fox_fwd", grid=(FOX_H // HG, NQ),
        in_specs=[qtile, pl.BlockSpec((HG, L, HP), lambda h, i: (h, 0, 0)), pl.BlockSpec((L, HG * HP), lambda h, i: (0, h))] + [anyspec] * n,
        out_specs=[pl.BlockSpec((TQ, HG * HP), lambda h, i: (i, h)), qtile] + [anyspec] * n,
        out_shape=[SDS((L, FOX_H * HP), f32), SDS(qa.shape, bf16)] + [SDS((N_DEV,) + a.shape, a.dtype) for a in shards],
        scratch_shapes=_exchange_sems(n), compiler_params=_cp(2))(qa, ka, fv, *shards)
    return res[0], res[1], res[2:]


def _fox_bwd(qb, ka, va, dob, slabs):
    L = qb.shape[1]
    TQ = TK = _pick(L, ATTN_TILES)
    NQ = L // TQ
    n = len(slabs)
    HG = FOX_HEAD_GROUP

    def body(q_ref, k_ref, v_ref, do_ref, *rest):
        ins, (dq_ref, dk_ref, dv_ref), outs, sems = rest[:n], rest[n:n + 3], rest[n + 3:2 * n + 3], rest[2 * n + 3:]
        h, j = pl.program_id(0), pl.program_id(1)
        cols = [slice(a * HP, (a + 1) * HP) for a in range(HG)]

        @pl.when((h == 0) & (j == 0))
        def _():
            _exchange_start(ins, outs, *sems, gather=False)

        @pl.when(j == 0)
        def _():
            dq_ref[...] = jnp.zeros_like(dq_ref)

        kts = [k_ref[a] for a in range(HG)]
        vts = [v_ref[:, cols[a]] for a in range(HG)]
        colg = j * TK + _iota((TQ, TK), 1)
        rowb = _iota((TQ, TK), 0)

        def step(i, carry, masked):
            dks, dvs = carry
            r0 = _tile_start(i, TQ)
            rows = pl.ds(r0, TQ)
            qs = [q_ref[a, rows, :] for a in range(HG)]
            ps = [jnp.exp2(_dot_nt(q, kt)) for q, kt in zip(qs, kts)]
            if masked:
                keep = (colg <= rowb + i * TQ) & (colg >= N_PAD)
                ps = [jnp.where(keep, p, 0.0) for p in ps]
            dobs = [do_ref[rows, cols[a]] for a in range(HG)]
            dvs = [dv + _dot_tn(_b(p), dob) for dv, p, dob in zip(dvs, ps, dobs)]
            dss = [_b(p * _dot_nt(dob, vt)) for p, dob, vt in zip(ps, dobs, vts)]
            for a in range(HG):
                dq_ref[rows, cols[a]] += _dot(dss[a], kts[a])
            dks = [dk + _dot_tn(ds, q) for dk, ds, q in zip(dks, dss, qs)]
            return dks, dvs

        zeros = [jnp.zeros((TK, HP), f32)] * HG
        carry = step(j, (zeros, zeros), True)
        split = jnp.where(j == 0, NQ, j + 1)
        carry = lax.fori_loop(j + 1, split, functools.partial(step, masked=True), carry)
        dks, dvs = lax.fori_loop(split, NQ, functools.partial(step, masked=False), carry)
        for a in range(HG):
            dk_ref[:, cols[a]] = dks[a]
            dv_ref[:, cols[a]] = _b(dvs[a])

        @pl.when((h == FOX_H // HG - 1) & (j == NQ - 1))
        def _():
            _exchange_wait(ins, outs, *sems, gather=False)

    head = pl.BlockSpec((L, HG * HP), lambda h, j: (0, h))
    tile = pl.BlockSpec((TK, HG * HP), lambda h, j: (j, h))
    anyspec = pl.BlockSpec(memory_space=pl.ANY)
    res = pl.pallas_call(
        body, name="fox_bwd", grid=(FOX_H // HG, L // TK),
        in_specs=[pl.BlockSpec((HG, L, HP), lambda h, j: (h, 0, 0)), pl.BlockSpec((HG, TK, HP), lambda h, j: (h, j, 0)), tile, head]
        + [anyspec] * n,
        out_specs=[head, tile, tile] + [anyspec] * n,
        out_shape=[SDS((L, FOX_H * HP), f32), SDS((L, FOX_H * HP), f32), SDS((L, FOX_H * HP), bf16)] + [SDS(a.shape, a.dtype) for a in slabs],
        scratch_shapes=_exchange_sems(n), compiler_params=_cp(2))(qb, ka, va, dob, *slabs)
    return res[:3], res[3:]


def _dn_post(y, sd, alog_p, dt_p, valid):
    a = y * _sigmoid(y)
    W = DN_H * DN_D
    heads = []
    for part, scale in ((0, DN_D ** -0.5), (1, 1.0)):
        for h in range(DN_H):
            xh = a[:, part * W + h * DN_D:part * W + (h + 1) * DN_D]
            heads.append(xh * lax.rsqrt(jnp.sum(xh * xh, axis=-1, keepdims=True) + EPS) * scale)
    q = jnp.concatenate(heads[:DN_H], axis=1)
    k = jnp.concatenate(heads[DN_H:], axis=1)
    v = a[:, 2 * W:3 * W]
    lane = _iota(sd.shape, 1)
    beta = _sigmoid(sd) * valid
    g = -jnp.exp(alog_p) * jax.nn.softplus(sd + dt_p) * valid
    bg = jnp.where(lane < DN_H, beta, jnp.where(lane < 2 * DN_H, g, 0.0))
    return q, k, v, bg


def _conv_fwd(ext_ref, cw_ref, TM):
    y = cw_ref[0:1, :] * ext_ref[8 - (CONV_K - 1):8 - (CONV_K - 1) + TM, :]
    for i in range(1, CONV_K):
        o = 8 - (CONV_K - 1) + i
        y = y + cw_ref[i:i + 1, :] * ext_ref[o:o + TM, :]
    return y


def _dn_prep(dn, sd, cw, alog_p, dt_p):
    L, W3 = dn.shape
    TM = _pick(L, ROW_TILES)
    W = DN_H * DN_D

    def body(dn_ref, halo_ref, sd_ref, cw_ref, al_ref, dt_ref, q_ref, k_ref, v_ref, bg_ref, ext):
        i = pl.program_id(0)
        ext[0:8, :] = jnp.where(i == 0, 0.0, halo_ref[...])
        ext[8:, :] = dn_ref[...]
        y = _conv_fwd(ext, cw_ref, TM)
        valid = ((i * TM + _iota((TM, 1), 0)) >= N_PAD).astype(f32)
        q, k, v, bg = _dn_post(y, sd_ref[...], al_ref[...], dt_ref[...], valid)
        q_ref[...], k_ref[...], v_ref[...], bg_ref[...] = q, k, v, bg

    row = lambda wd: pl.BlockSpec((TM, wd), lambda i: (i, 0))
    vec = pl.BlockSpec((1, HP), lambda i: (0, 0))
    return pl.pallas_call(
        body, name="dn_prep", grid=(L // TM,),
        in_specs=[row(W3), pl.BlockSpec((8, W3), lambda i: (jnp.maximum(i * (TM // 8) - 1, 0), 0)), row(HP),
                  pl.BlockSpec((CONV_K, W3), lambda i: (0, 0)), vec, vec],
        out_specs=[row(W), row(W), row(W), row(HP)],
        out_shape=[SDS((L, W), f32)] * 3 + [SDS((L, HP), f32)],
        scratch_shapes=[pltpu.VMEM((TM + 8, W3), f32)], compiler_params=_cp())(dn, dn, sd, cw, alog_p, dt_p)


def _dn_prep_bwd(dn, sd, cw, alog_p, dt_p, dq, dk, dv, dbg):
    L, W3 = dn.shape
    TM = _pick(L, ROW_TILES)
    NT = L // TM
    W = DN_H * DN_D

    def body(dn_ref, halo_ref, sd_ref, cw_ref, al_ref, dt_ref, dq_ref, dk_ref, dv_ref, dbg_ref,
             dg_ref, dcw_ref, dp_ref, ext, dyp, carry):
        i = pl.program_id(0)
        t = NT - 1 - i

        @pl.when(i == 0)
        def _():
            carry[...] = jnp.zeros_like(carry)
            dcw_ref[...] = jnp.zeros_like(dcw_ref)
            dp_ref[...] = jnp.zeros_like(dp_ref)
            dyp[...] = jnp.zeros_like(dyp)

        ext[0:8, :] = jnp.where(t == 0, 0.0, halo_ref[...])
        ext[8:, :] = dn_ref[...]
        y = _conv_fwd(ext, cw_ref, TM)
        valid = ((t * TM + _iota((TM, 1), 0)) >= N_PAD).astype(f32)
        _, vjp = jax.vjp(functools.partial(_dn_post, valid=valid), y, sd_ref[...], al_ref[...], dt_ref[...])
        dy, dsd, dal, ddt = vjp((dq_ref[...], dk_ref[...], dv_ref[...], dbg_ref[...]))
        dg_ref[:, W3:W3 + HP] = _b(dsd)
        dg_ref[:, W3 + HP:] = jnp.zeros((TM, SMALL_W - HP), bf16)
        dp_ref[0:1, :] += dal
        dp_ref[1:2, :] += ddt
        dyp[8:8 + TM, :] = dy
        o0 = CONV_K - 1
        dext = cw_ref[0:1, :] * dyp[o0:o0 + TM + 8, :]
        for k in range(1, CONV_K):
            dext = dext + cw_ref[k:k + 1, :] * dyp[o0 - k:o0 - k + TM + 8, :]
        for k in range(CONV_K):
            o = 8 - (CONV_K - 1) + k
            dcw_ref[k:k + 1, :] += jnp.sum(dy * ext[o:o + TM, :], axis=0, keepdims=True)
        dg_ref[:, 0:W3] = _b(jnp.concatenate([dext[8:TM, :], dext[TM:TM + 8, :] + carry[...]], axis=0))
        carry[...] = dext[0:8, :]

    row = lambda wd: pl.BlockSpec((TM, wd), lambda i: (NT - 1 - i, 0))
    vec = pl.BlockSpec((1, HP), lambda i: (0, 0))
    return pl.pallas_call(
        body, name="dn_prep_bwd", grid=(NT,),
        in_specs=[row(W3), pl.BlockSpec((8, W3), lambda i: (jnp.maximum((NT - 1 - i) * (TM // 8) - 1, 0), 0)), row(HP),
                  pl.BlockSpec((CONV_K, W3), lambda i: (0, 0)), vec, vec, row(W), row(W), row(W), row(HP)],
        out_specs=[row(W3 + SMALL_W), pl.BlockSpec((8, W3), lambda i: (0, 0)), pl.BlockSpec((8, HP), lambda i: (0, 0))],
        out_shape=[SDS((L, W3 + SMALL_W), bf16), SDS((8, W3), f32), SDS((8, HP), f32)],
        scratch_shapes=[pltpu.VMEM((TM + 8, W3), f32), pltpu.VMEM((TM + 16, W3), f32), pltpu.VMEM((8, W3), f32)],
        compiler_params=_cp())(dn, dn, sd, cw, alog_p, dt_p, dq, dk, dv, dbg)


def _split2(x):
    hi = _b(x)
    return hi, _b(x - hi.astype(f32))


def _split3(x):
    hi = _b(x)
    r = x - hi.astype(f32)
    mid = _b(r)
    return hi, mid, _b(r - mid.astype(f32))


def _x3(a, b, dot):
    (a1, a2), (b1, b2) = _split2(a), _split2(b)
    return dot(a1, b1) + (dot(a1, b2) + dot(a2, b1))


@jax.custom_vjp
def _dot_x3(a, b):
    return _x3(a, b, _dot)


_dot_x3.defvjp(lambda a, b: (_x3(a, b, _dot), (a, b)), lambda res, g: (_x3(g, res[1], _dot_nt), _x3(res[0], g, _dot_tn)))


def _exact3(m, x, dot):
    x1, x2, x3 = _split3(x)
    return dot(m, x1) + (dot(m, x2) + dot(m, x3))


def _tri_ones(C, lower):
    row, col = _iota((C, C), 0), _iota((C, C), 1)
    return _b(((row >= col) if lower else (row <= col)).astype(f32))


@jax.custom_vjp
def _chunk_cumsum(x):
    return _exact3(_tri_ones(x.shape[0], True), x, _dot)


_chunk_cumsum.defvjp(lambda x: (_exact3(_tri_ones(x.shape[0], True), x, _dot), None),
                     lambda _, g: (_exact3(_tri_ones(g.shape[0], False), g, _dot),))


def _mxu_transpose(x):
    C = x.shape[0]
    eye = _b((_iota((C, C), 0) == _iota((C, C), 1)).astype(f32))
    return _exact3(eye, x, lambda m, part: _dot_tn(part, m))


@jax.custom_vjp
def _transpose_exact(x):
    return _mxu_transpose(x)


_transpose_exact.defvjp(lambda x: (_mxu_transpose(x), None), lambda _, g: (_mxu_transpose(g),))


def _unit_lower_inverses(lows):
    C = lows[0].shape[0]
    P = jnp.stack(lows)
    X = (_iota((C, C), 0) == _iota((C, C), 1)).astype(f32)[None] - P
    bdot = functools.partial(_x3, dot=lambda a, b: jnp.einsum("bij,bjk->bik", a, b, preferred_element_type=f32))
    for _ in range(5):
        P = bdot(P, P)
        X = X + bdot(X, P)
    return [X[i] for i in range(len(lows))]


@jax.custom_vjp
def _inverse_given(low, X):
    return X


def _inverse_given_bwd(X, g):
    return -_x3(_x3(X, g, _dot_tn), X, _dot_nt), jnp.zeros_like(X)


_inverse_given.defvjp(lambda low, X: (X, X), _inverse_given_bwd)


def _dn_intra_pre(q, k, v, bg):
    C = DN_C
    row, col = _iota((C, C), 0), _iota((C, C), 1)
    tri = row >= col
    G = _chunk_cumsum(bg)
    GT = _transpose_exact(G)
    lane = _iota((C, HP), 1)
    rowt = _iota((HP, C), 0)
    last = _iota((C, 1), 0) == C - 1
    heads = []
    for h in range(DN_H):
        beta = jnp.sum(jnp.where(lane == h, bg, 0.0), axis=1, keepdims=True)
        gcol = jnp.sum(jnp.where(lane == DN_H + h, G, 0.0), axis=1, keepdims=True)
        grow = jnp.sum(jnp.where(rowt == DN_H + h, GT, 0.0), axis=0, keepdims=True)
        glast = jnp.sum(jnp.where(last, gcol, 0.0), axis=0, keepdims=True)
        decay = jnp.exp(jnp.where(tri, gcol - grow, NEG))
        qh, kh, vh = (t[:, h * DN_D:(h + 1) * DN_D] for t in (q, k, v))
        kb = kh * beta
        low = jnp.where(row > col, _dot_nt(_b(kb), _b(kh)) * decay, 0.0)
        heads.append((beta, gcol, glast, decay, qh, kh, vh, kb, low))
    return heads


def _dn_intra_post(heads, xs):
    lane1 = _iota((1, HP), 1)
    us, ws, qds, kds, attns = [], [], [], [], []
    glrow = jnp.zeros((1, HP), f32)
    for h, ((beta, gcol, glast, decay, qh, kh, vh, kb, _), X) in enumerate(zip(heads, xs)):
        eg = jnp.exp(gcol)
        us.append(_dot_x3(X, vh * beta))
        ws.append(_dot_x3(X, kb * eg))
        attns.append(_dot_nt(_b(qh), _b(kh)) * decay)
        qds.append(qh * eg)
        kds.append(kh * jnp.exp(glast - gcol))
        glrow = glrow + jnp.where(lane1 == h, glast, 0.0)
    cat = lambda xs_: jnp.concatenate(xs_, axis=1)
    return cat(us), cat(ws), cat(qds), cat(kds), cat(attns), glrow, cat(list(xs))


def _dn_intra_group(q, k, v, bg, xs):
    G = q.shape[0] // DN_C
    rows = [slice(j * DN_C, (j + 1) * DN_C) for j in range(G)]
    pre = [_dn_intra_pre(q[r, :], k[r, :], v[r, :], bg[r, :]) for r in rows]
    inv = [[_inverse_given(hd[-1], x) for hd, x in zip(heads, xj)] for heads, xj in zip(pre, xs)]
    post = [_dn_intra_post(heads, xj) for heads, xj in zip(pre, inv)]
    return tuple(jnp.concatenate([p[i] for p in post], axis=0) for i in range(5)) + (tuple(p[5] for p in post),)


def _lane_pick(rowvec, h):
    return jnp.sum(jnp.where(_iota(rowvec.shape, 1) == h, rowvec, 0.0), axis=1, keepdims=True)


def _dn_intra(q, k, v, bg):
    L, W = q.shape
    NC = L // DN_C
    G = _pick(NC, DN_INTRA_GROUP)
    R = G * DN_C
    WA = DN_H * DN_C

    def body(q_ref, k_ref, v_ref, bg_ref, u_ref, w_ref, qd_ref, kd_ref, at_ref, gl_ref, x_ref):
        rows = [slice(j * DN_C, (j + 1) * DN_C) for j in range(G)]
        pre = [_dn_intra_pre(q_ref[r, :], k_ref[r, :], v_ref[r, :], bg_ref[r, :]) for r in rows]
        inv = _unit_lower_inverses([hd[-1] for heads in pre for hd in heads])
        for j, r in enumerate(rows):
            u, w, qd, kd, at, gl, xs = _dn_intra_post(pre[j], inv[j * DN_H:(j + 1) * DN_H])
            u_ref[r, :], x_ref[r, :] = u, xs
            w_ref[r, :], qd_ref[r, :], kd_ref[r, :], at_ref[r, :] = _b(w), _b(qd), _b(kd), _b(at)
            gl_ref[j] = gl

    row = lambda wd: pl.BlockSpec((R, wd), lambda n: (n, 0))
    return pl.pallas_call(
        body, name="dn_intra", grid=(NC // G,),
        in_specs=[row(W), row(W), row(W), row(HP)],
        out_specs=[row(W), row(W), row(W), row(W), row(WA), pl.BlockSpec((G, 1, HP), lambda n: (n, 0, 0)), row(WA)],
        out_shape=[SDS((L, W), f32), SDS((L, W), bf16), SDS((L, W), bf16), SDS((L, W), bf16), SDS((L, WA), bf16), SDS((NC, 1, HP), f32),
                   SDS((L, WA), f32)],
        compiler_params=_cp())(q, k, v, bg)


def _dn_scan(u, w, qd, kd, at, gl):
    L, W = u.shape
    NC = L // DN_C
    G = _pick(NC, DN_SCAN_GROUP)
    R = G * DN_C

    def body(u_ref, w_ref, qd_ref, kd_ref, at_ref, gl_ref, o_ref, vn_ref, s_ref, S):
        @pl.when(pl.program_id(0) == 0)
        def _():
            S[...] = jnp.zeros_like(S)

        for j in range(G):
            r = slice(j * DN_C, (j + 1) * DN_C)
            glrow = gl_ref[j]
            for h in range(DN_H):
                c = slice(h * DN_D, (h + 1) * DN_D)
                Sh = S[h]
                s_ref[j, h] = Sh
                Sb = _b(Sh)
                vb = _b(u_ref[r, c] - _dot(w_ref[r, c], Sb))
                vn_ref[r, c] = vb
                o_ref[r, c] = _dot(qd_ref[r, c], Sb) + _dot(at_ref[r, h * DN_C:(h + 1) * DN_C], vb)
                S[h] = Sh * jnp.exp(_lane_pick(glrow, h)) + _dot_tn(kd_ref[r, c], vb)

    row = lambda wd: pl.BlockSpec((R, wd), lambda n: (n, 0))
    return pl.pallas_call(
        body, name="dn_scan", grid=(NC // G,),
        in_specs=[row(W), row(W), row(W), row(W), row(DN_H * DN_C), pl.BlockSpec((G, 1, HP), lambda n: (n, 0, 0))],
        out_specs=[row(W), row(W), pl.BlockSpec((G, DN_H, DN_D, DN_D), lambda n: (n, 0, 0, 0))],
        out_shape=[SDS((L, W), f32), SDS((L, W), bf16), SDS((NC, DN_H, DN_D, DN_D), f32)],
        scratch_shapes=[pltpu.VMEM((DN_H, DN_D, DN_D), f32)], compiler_params=_cp())(u, w, qd, kd, at, gl)


def _dn_scan_bwd(do, w, qd, kd, at, gl):
    L, W = do.shape
    NC = L // DN_C
    G = _pick(NC, DN_SCAN_GROUP)
    R = G * DN_C
    NS = NC // G

    def body(do_ref, w_ref, qd_ref, kd_ref, at_ref, gl_ref, dvn_ref, ds_ref, dS):
        @pl.when(pl.program_id(0) == 0)
        def _():
            dS[...] = jnp.zeros_like(dS)

        for j in reversed(range(G)):
            r = slice(j * DN_C, (j + 1) * DN_C)
            glrow = gl_ref[j]
            for h in range(DN_H):
                c = slice(h * DN_D, (h + 1) * DN_D)
                dSo = dS[h]
                ds_ref[j, h] = dSo
                dob = _b(do_ref[r, c])
                dvn = _dot_tn(at_ref[r, h * DN_C:(h + 1) * DN_C], dob) + _dot(kd_ref[r, c], _b(dSo))
                dvn_ref[r, c] = dvn
                dS[h] = _dot_tn(qd_ref[r, c], dob) + dSo * jnp.exp(_lane_pick(glrow, h)) - _dot_tn(w_ref[r, c], _b(dvn))

    row = lambda wd: pl.BlockSpec((R, wd), lambda n: (NS - 1 - n, 0))
    return pl.pallas_call(
        body, name="dn_scan_bwd", grid=(NS,),
        in_specs=[row(W), row(W), row(W), row(W), row(DN_H * DN_C), pl.BlockSpec((G, 1, HP), lambda n: (NS - 1 - n, 0, 0))],
        out_specs=[row(W), pl.BlockSpec((G, DN_H, DN_D, DN_D), lambda n: (NS - 1 - n, 0, 0, 0))],
        out_shape=[SDS((L, W), f32), SDS((NC, DN_H, DN_D, DN_D), f32)],
        scratch_shapes=[pltpu.VMEM((DN_H, DN_D, DN_D), f32)], compiler_params=_cp())(do, w, qd, kd, at, gl)


def _dn_intra_bwd(q, k, v, bg, xinv, do, vn, dvn, states, dstates):
    L, W = q.shape
    NC = L // DN_C
    G = _pick(NC, DN_INTRA_GROUP)
    R = G * DN_C

    def body(q_ref, k_ref, v_ref, bg_ref, x_ref, do_ref, vn_ref, dvn_ref, s_ref, ds_ref, dq_ref, dk_ref, dv_ref, dbg_ref):
        lane1 = _iota((1, HP), 1)
        rows = [slice(j * DN_C, (j + 1) * DN_C) for j in range(G)]
        xs = [[x_ref[r, h * DN_C:(h + 1) * DN_C] for h in range(DN_H)] for r in rows]
        outs, vjp = jax.vjp(functools.partial(_dn_intra_group, xs=xs), q_ref[...], k_ref[...], v_ref[...], bg_ref[...])
        dws, dqds, dkds, dats, dgls = [], [], [], [], []
        for j, r in enumerate(rows):
            dw, dqd, dkd, dat = [], [], [], []
            dgl = jnp.zeros((1, HP), f32)
            for h in range(DN_H):
                c = slice(h * DN_D, (h + 1) * DN_D)
                Sh, dSo = s_ref[j, h], ds_ref[j, h]
                Sb, dob, vb = _b(Sh), _b(do_ref[r, c]), vn_ref[r, c]
                dw.append(-_dot_nt(_b(dvn_ref[r, c]), Sb))
                dqd.append(_dot_nt(dob, Sb))
                dat.append(_dot_nt(dob, vb))
                dkd.append(_dot_nt(vb, _b(dSo)))
                dcd = jnp.sum(jnp.sum(Sh * dSo, axis=1, keepdims=True), axis=0, keepdims=True)
                dgl = dgl + jnp.where(lane1 == h, dcd * jnp.exp(_lane_pick(outs[5][j], h)), 0.0)
            cat = lambda xs_: jnp.concatenate(xs_, axis=1)
            dws.append(cat(dw)), dqds.append(cat(dqd)), dkds.append(cat(dkd)), dats.append(cat(dat)), dgls.append(dgl)
        cat0 = lambda xs_: jnp.concatenate(xs_, axis=0)
        dq, dk, dv, dbg = vjp((dvn_ref[...], cat0(dws), cat0(dqds), cat0(dkds), cat0(dats), tuple(dgls)))
        dq_ref[...], dk_ref[...], dv_ref[...], dbg_ref[...] = dq, dk, dv, dbg

    row = lambda wd: pl.BlockSpec((R, wd), lambda n: (n, 0))
    st = pl.BlockSpec((G, DN_H, DN_D, DN_D), lambda n: (n, 0, 0, 0))
    return pl.pallas_call(
        body, name="dn_intra_bwd", grid=(NC // G,),
        in_specs=[row(W), row(W), row(W), row(HP), row(DN_H * DN_C), row(W), row(W), row(W), st, st],
        out_specs=[row(W), row(W), row(W), row(HP)],
        out_shape=[SDS((L, W), f32)] * 3 + [SDS((L, HP), f32)],
        compiler_params=_cp())(q, k, v, bg, xinv, do, vn, dvn, states, dstates)


def _dn_normgate(oraw, dz, wn):
    outs = []
    for h in range(DN_H):
        sl = slice(h * DN_D, (h + 1) * DN_D)
        z = dz[:, sl]
        outs.append(_rms(oraw[:, sl], wn) * (z * _sigmoid(z)))
    return jnp.concatenate(outs, axis=1)


def _mix_fwd(op, oraw, dz, ga, gb, h0, wn, wbf, wbd, wo):
    L, D = h0.shape
    TM = _pick(L, ROW_TILES)

    def body(op_ref, or_ref, dz_ref, ga_ref, gb_ref, h0_ref, wn_ref, wbf_ref, wbd_ref, wo_ref, h1_ref):
        pf = _dot(_b(op_ref[...]), wbf_ref[...])
        pd = _dot(_b(_dn_normgate(or_ref[...], dz_ref[...], wn_ref[...])), wbd_ref[...])
        y = _sigmoid(ga_ref[...]) * pf + _sigmoid(gb_ref[...]) * pd
        h1_ref[...] = h0_ref[...] + _dot(_b(y), wo_ref[...])

    row = lambda wd: pl.BlockSpec((TM, wd), lambda i: (i, 0))
    full = lambda a: pl.BlockSpec(a.shape, lambda i: (0, 0))
    return pl.pallas_call(
        body, name="mix_fwd", grid=(L // TM,),
        in_specs=[row(op.shape[1]), row(oraw.shape[1]), row(dz.shape[1]), row(D), row(D), row(D), full(wn), full(wbf), full(wbd), full(wo)],
        out_specs=row(D), out_shape=SDS((L, D), f32), compiler_params=_cp())(op, oraw, dz, ga, gb, h0, wn, wbf, wbd, wo)


def _mix_bwd(dh1, op, oraw, dz, ga, gb, wn, wbf, wbd, wo):
    L, D = dh1.shape
    TM = _pick(L, ROW_TILES)
    WF, WD = op.shape[1], oraw.shape[1]

    def body(dh1_ref, op_ref, or_ref, dz_ref, ga_ref, gb_ref, wn_ref, wbf_ref, wbd_ref, wo_ref,
             dop_ref, dor_ref, dg_ref, af_ref, ad_ref, dpf_ref, dpd_ref, y_ref, dmix_ref, acc_ref):
        @pl.when(pl.program_id(0) == 0)
        def _():
            acc_ref[...] = jnp.zeros_like(acc_ref)

        opv = op_ref[...]
        af = _b(opv)
        ad, vjp = jax.vjp(_dn_normgate, or_ref[...], dz_ref[...], wn_ref[...])
        adb = _b(ad)
        pf, pd = _dot(af, wbf_ref[...]), _dot(adb, wbd_ref[...])
        sa, sb = _sigmoid(ga_ref[...]), _sigmoid(gb_ref[...])
        dmix = _b(dh1_ref[...])
        dy = _dot_nt(dmix, wo_ref[...])
        dpf, dpd = _b(dy * sa), _b(dy * sb)
        dor, ddz, dwn = vjp(_dot_nt(dpd, wbd_ref[...]))
        dop = _dot_nt(dpf, wbf_ref[...])
        lane = _iota((TM, HP), 1)
        for h in range(WF // HP):
            c = slice(h * HP, (h + 1) * HP)
            delta = jnp.sum(jnp.where(lane < FOX_D, dop[:, c] * opv[:, c], 0.0), axis=1, keepdims=True)
            dop_ref[:, c] = _b(dop[:, c] - _spread3(jnp.broadcast_to(delta, (TM, HP)), lane, C_DELTA0))
        dor_ref[...] = dor
        dg_ref[:, 0:WD] = _b(ddz)
        dg_ref[:, WD:WD + D] = _b(dy * pf * sa * (1.0 - sa))
        dg_ref[:, WD + D:] = _b(dy * pd * sb * (1.0 - sb))
        af_ref[...], ad_ref[...], y_ref[...] = af, adb, _b(sa * pf + sb * pd)
        dpf_ref[...], dpd_ref[...], dmix_ref[...] = dpf, dpd, dmix
        acc_ref[0:1, :] += dwn

    row = lambda wd: pl.BlockSpec((TM, wd), lambda i: (i, 0))
    full = lambda a: pl.BlockSpec(a.shape, lambda i: (0, 0))
    return pl.pallas_call(
        body, name="mix_bwd", grid=(L // TM,),
        in_specs=[row(D), row(WF), row(WD), row(WD), row(D), row(D), full(wn), full(wbf), full(wbd), full(wo)],
        out_specs=[row(WF), row(WD), row(WD + 2 * D), row(WF), row(WD), row(D), row(D), row(D), row(D),
                   pl.BlockSpec((8, HP), lambda i: (0, 0))],
        out_shape=[SDS((L, WF), bf16), SDS((L, WD), f32), SDS((L, WD + 2 * D), bf16), SDS((L, WF), bf16), SDS((L, WD), bf16),
                   SDS((L, D), bf16), SDS((L, D), bf16), SDS((L, D), bf16), SDS((L, D), bf16), SDS((8, HP), f32)],
        compiler_params=_cp())(dh1, op, oraw, dz, ga, gb, wn, wbf, wbd, wo)


def _ffn_fwd_bwd(h1, tgt, w2, wf, wgt, wut, wd):
    L, D = h1.shape
    F = wd.shape[0]
    TM = _pick(L, FFN_TILES)

    def body(h_ref, t_ref, w2_ref, wf_ref, wg_hbm, wu_hbm, wd_hbm,
             dh1_ref, xn_ref, dg_ref, du_ref, act_ref, dh2_ref, acc_ref, wg_v, wu_v, wd_v, sems):
        i = pl.program_id(0)
        _load_once([(wg_hbm, wg_v), (wu_hbm, wu_v), (wd_hbm, wd_v)], sems)

        @pl.when(i == 0)
        def _():
            acc_ref[...] = jnp.zeros_like(acc_ref)

        h1v = h_ref[...]
        xn2, vjp2 = jax.vjp(_rms, h1v, w2_ref[...])
        xb = _b(xn2)
        g, u = _dot_nt(xb, wg_v[...]), _dot_nt(xb, wu_v[...])
        sg = _sigmoid(g)
        ab = _b(g * sg * u)
        h2 = h1v + _dot(ab, wd_v[...])
        out, vjpf = jax.vjp(_rms, h2, wf_ref[...])
        valid = (i * TM + _iota((TM, 1), 0)) >= PREFIX
        diff = jnp.where(valid, out - t_ref[...], 0.0)
        loss = 0.5 * jnp.sum(jnp.sum(diff * diff, axis=1, keepdims=True), axis=0, keepdims=True) / D
        dh2, dwf = vjpf(diff * (1.0 / D))
        dh2b = _b(dh2)
        dact = _dot_nt(dh2b, wd_v[...])
        dgb = _b(dact * u * (sg * (1.0 + g * (1.0 - sg))))
        dub = _b(dact * (g * sg))
        dh1n, dw2 = vjp2(_dot(dgb, wg_v[...]) + _dot(dub, wu_v[...]))
        dh1_ref[...] = dh2 + dh1n
        xn_ref[...], dg_ref[...], du_ref[...], act_ref[...], dh2_ref[...] = xb, dgb, dub, ab, dh2b
        acc_ref[0:1, :] += dw2
        acc_ref[1:2, :] += dwf
        acc_ref[2:3, :] += jnp.broadcast_to(loss, (1, D))

    row = lambda wd_: pl.BlockSpec((TM, wd_), lambda i: (i, 0))
    vec = pl.BlockSpec((1, D), lambda i: (0, 0))
    anyspec = pl.BlockSpec(memory_space=pl.ANY)
    return pl.pallas_call(
        body, name="ffn_fwd_bwd", grid=(L // TM,),
        in_specs=[row(D), row(D), vec, vec, anyspec, anyspec, anyspec],
        out_specs=[row(D), row(D), row(F), row(F), row(F), row(D), pl.BlockSpec((8, D), lambda i: (0, 0))],
        out_shape=[SDS((L, D), f32), SDS((L, D), bf16), SDS((L, F), bf16), SDS((L, F), bf16), SDS((L, F), bf16), SDS((L, D), bf16),
                   SDS((8, D), f32)],
        scratch_shapes=[pltpu.VMEM((F, D), bf16), pltpu.VMEM((F, D), bf16), pltpu.VMEM((F, D), bf16), pltpu.SemaphoreType.DMA((3,))],
        compiler_params=_cp())(h1, tgt, w2, wf, wgt, wut, wd)


def _pad_lanes(v, n=HP):
    return jnp.pad(v.astype(f32), ((0, 0), (0, n - v.shape[1])))


def _pack_w_in(wt_full):
    D = wt_full.shape[1]
    FW, DW = FOX_H * FOX_D, DN_H * DN_D
    o = 0
    parts = {}
    for name, wd in (("fq", FW), ("fk", FW), ("fv", FW), ("fl", FOX_H), ("dn", 3 * DW), ("ba", 2 * DN_H), ("dz", DW), ("ga", D), ("gb", D)):
        parts[name] = wt_full[o:o + wd]
        o += wd
    assert o == wt_full.shape[0]
    heads = lambda w: jnp.pad(w.reshape(FOX_H, FOX_D, D), ((0, 0), (0, HP - FOX_D), (0, 0))).reshape(FOX_H * HP, D)
    small = lambda w: jnp.pad(w, ((0, SMALL_W - w.shape[0]), (0, 0)))
    packed = dict(fq=heads(parts["fq"]), fk=heads(parts["fk"]), fv=heads(parts["fv"]), sf=small(parts["fl"]), sd=small(parts["ba"]),
                  dn=parts["dn"], dz=parts["dz"], ga=parts["ga"], gb=parts["gb"])
    return jnp.concatenate([packed[name] for name, _, _, _ in _seg_layout(D)], axis=0)


def _unpack_w_in(groups, d_model):
    D = groups[0].shape[1]
    FW = FOX_H * FOX_D
    segs = {}
    for grp, g in zip(GROUPS, groups):
        o = 0
        for name, wd, _, sg in _seg_layout(d_model):
            if sg == grp:
                segs[name] = g[o:o + wd]
                o += wd
    heads = lambda g: g.reshape(FOX_H, HP, D)[:, :FOX_D].reshape(FW, D)
    return jnp.concatenate([heads(segs["fq"]), heads(segs["fk"]), heads(segs["fv"]), segs["sf"][:FOX_H], segs["dn"],
                            segs["sd"][:2 * DN_H], segs["dz"], segs["ga"], segs["gb"]], axis=0)


def _local_step(x, tgt, meta, w1, w_in_t, fbias, cw, alog, dtb, wn, w2, wf, late_shards):
    T, D = x.shape
    h0 = jnp.concatenate([jnp.zeros((N_PAD, D), f32), meta, x], axis=0)
    tgt_p = jnp.concatenate([jnp.zeros((PREFIX, D), f32), tgt], axis=0)
    wp = _pack_w_in(w_in_t)
    bias_p, alog_p, dt_p = _pad_lanes(fbias), _pad_lanes(jnp.pad(alog, ((0, 0), (DN_H, 0)))), _pad_lanes(jnp.pad(dtb, ((0, 0), (DN_H, 0))))

    (xn, fq, fk, sf, fv, dn, sd, dz, ga, gb), g_mix = _in_proj(h0, w1, wp, [late_shards[n] for n in LATE_MIX])
    qa, ka, va = _fox_prep(fq, fk, fv, sf, bias_p)
    op, qb, g_ffn = _fox_fwd(qa, ka, va, [late_shards[n] for n in LATE_FFN])
    full = {n: _from_slabs(n, s) for n, s in zip(LATE_MIX + LATE_FFN, tuple(g_mix) + tuple(g_ffn))}
    wbf, wbd, wo, wgt, wut, wd = (full[n] for n in ("w_branch_fox", "w_branch_dn", "w_out", "w_ffn_gate", "w_ffn_up", "w_ffn_down"))
    wbf_p = jnp.pad(wbf.reshape(FOX_H, FOX_D, D), ((0, 0), (0, HP - FOX_D), (0, 0))).reshape(FOX_H * HP, D)
    qn, kn, vn, bg = _dn_prep(dn, sd, cw, alog_p, dt_p)
    u_dn, w_dn, qd_dn, kd_dn, at_dn, gl_dn, x_dn = _dn_intra(qn, kn, vn, bg)
    oraw, vnew, states = _dn_scan(u_dn, w_dn, qd_dn, kd_dn, at_dn, gl_dn)
    h1 = _mix_fwd(op, oraw, dz, ga, gb, h0, wn, wbf_p, wbd, wo)

    dh1, xn2, dgate, dup, act, dh2, acc_f = _ffn_fwd_bwd(h1, tgt_p, w2, wf, wgt, wut, wd)
    g_wg, g_wu, g_wd = _matmul_tn(dgate, xn2, "dw_ffn_gate"), _matmul_tn(dup, xn2, "dw_ffn_up"), _matmul_tn(act, dh2, "dw_ffn_down")

    dop, dor, d_mix, af, ad, dpf, dpd, yb, dmix, acc_m = _mix_bwd(dh1, op, oraw, dz, ga, gb, wn, wbf_p, wbd, wo)
    g_wbf = _matmul_tn(af, dpf, "dw_branch_fox").reshape(FOX_H, HP, D)[:, :FOX_D].reshape(FOX_H * FOX_D, D)
    g_wbd, g_wo = _matmul_tn(ad, dpd, "dw_branch_dn"), _matmul_tn(yb, dmix, "dw_out")

    dvnew, dstates = _dn_scan_bwd(dor, w_dn, qd_dn, kd_dn, at_dn, gl_dn)
    dqn, dkn, dvn, dbg = _dn_intra_bwd(qn, kn, vn, bg, x_dn, dor, vnew, dvnew, states, dstates)
    d_dn, acc_cw, acc_p = _dn_prep_bwd(dn, sd, cw, alog_p, dt_p, dqn, dkn, dvn, dbg)
    g_late = dict(w_branch_fox=g_wbf, w_branch_dn=g_wbd, w_out=g_wo, w_ffn_gate=g_wg, w_ffn_up=g_wu, w_ffn_down=g_wd)
    (dqa, dka, d_fv), recv = _fox_bwd(qb, ka, va, dop, [_to_slabs(n, g_late[n]) for n in LATE])
    d_fox, acc_b = _fox_prep_bwd(dqa, dka, sf, bias_p)

    dgroups = [d_fox, d_fv, d_dn, d_mix]
    g_wp = [_matmul_tn(dg, xn, "dw_in_" + grp) for grp, dg in zip(GROUPS, dgroups)]
    dh0, acc_1, (recv_w_in,) = _in_proj_bwd(dgroups, wp, h0, w1, dh1, [_to_slabs("w_in", _unpack_w_in(g_wp, D))])
    recv = dict(zip(LATE, recv), w_in=recv_w_in)

    small = dict(loss=acc_f[2, 0:1], mix_norm_w=acc_1[0], fox_forget_bias=acc_b[0, :FOX_H], dn_a_log=acc_p[0, DN_H:2 * DN_H],
                 dn_dt_bias=acc_p[1, DN_H:2 * DN_H], dn_out_norm_w=acc_m[0], ffn_norm_w=acc_f[0], final_norm_w=acc_f[1],
                 meta_tokens=dh0[N_PAD:PREFIX].reshape(-1), dn_conv_w=acc_cw[:CONV_K].reshape(-1))
    return dh0[PREFIX:], small, recv


def _mesh_pos():
    x, y, c = lax.axis_index("x"), lax.axis_index("y"), lax.axis_index("c")
    return x, y, c, 4 * x + 2 * y + c


def _peer(x, y, c, m):
    flip = lambda v, on: 1 - v if on else v
    px, py, pc = flip(x, m & 4), flip(y, m & 2), flip(c, m & 1)
    return (px, py, pc), 4 * px + 2 * py + pc


def _exchange_sems(n):
    return [pltpu.SemaphoreType.DMA((n, N_DEV - 1)), pltpu.SemaphoreType.DMA((n, N_DEV - 1)), pltpu.SemaphoreType.DMA((n,))]


def _exchange_copies(ins, outs, send_sems, recv_sems, loc_sems, gather, with_receives):
    x, y, c, me = _mesh_pos()
    src = lambda a, pid: ins[a] if gather else ins[a].at[pid]
    local = [pltpu.make_async_copy(src(a, me), outs[a].at[me], loc_sems.at[a]) for a in range(len(ins))]
    sends, recvs = [], []
    for m in range(1, N_DEV):
        peer, pid = _peer(x, y, c, m)
        for a in range(len(ins)):
            kw = dict(send_sem=send_sems.at[a, m - 1], recv_sem=recv_sems.at[a, m - 1], device_id=peer, device_id_type=MESH)
            sends.append(pltpu.make_async_remote_copy(src_ref=src(a, pid), dst_ref=outs[a].at[me], **kw))
            if with_receives:
                recvs.append(pltpu.make_async_remote_copy(src_ref=src(a, pid), dst_ref=outs[a].at[pid], **kw))
    return local, sends, recvs


def _exchange_start(ins, outs, send_sems, recv_sems, loc_sems, gather):
    local, sends, _ = _exchange_copies(ins, outs, send_sems, recv_sems, loc_sems, gather, with_receives=False)
    for cp in local + sends:
        cp.start()


def _exchange_wait(ins, outs, send_sems, recv_sems, loc_sems, gather):
    local, sends, recvs = _exchange_copies(ins, outs, send_sems, recv_sems, loc_sems, gather, with_receives=True)
    for cp in recvs:
        cp.wait_recv()
    for cp in sends:
        cp.wait_send()
    for cp in local:
        cp.wait()


def _gather_two_level(arrays, name):
    n = len(arrays)

    def body(*refs):
        ins, outs, (send_sems, recv_sems, loc_sems) = refs[:n], refs[n:2 * n], refs[2 * n:]
        x, y, c, me = _mesh_pos()
        sib = (x, y, 1 - c)
        chips = [(1 - x, y), (x, 1 - y), (1 - x, 1 - y)]
        dev_id = lambda px, py, pc: 4 * px + 2 * py + pc

        def copy(a, k, block, to, own=False):
            return pltpu.make_async_remote_copy(src_ref=ins[a] if own else outs[a].at[block], dst_ref=outs[a].at[block],
                                                send_sem=send_sems.at[a, k], recv_sem=recv_sems.at[a, k], device_id=to, device_id_type=MESH)

        local = [pltpu.make_async_copy(ins[a], outs[a].at[me], loc_sems.at[a]) for a in range(n)]
        first = [copy(a, 0, me, sib, own=True) for a in range(n)]
        first += [copy(a, 1 + j, me, (*chip, c), own=True) for j, chip in enumerate(chips) for a in range(n)]
        for cp in local + first:
            cp.start()
        passed = []
        for j, chip in enumerate(chips):
            for a in range(n):
                copy(a, 1 + j, dev_id(*chip, c), sib).wait_recv()
                cp = copy(a, 4 + j, dev_id(*chip, c), sib)
                cp.start()
                passed.append(cp)
        for a in range(n):
            copy(a, 0, dev_id(x, y, 1 - c), sib).wait_recv()
        for j, chip in enumerate(chips):
            for a in range(n):
                copy(a, 4 + j, dev_id(*chip, 1 - c), sib).wait_recv()
        for cp in first + passed:
            cp.wait_send()
        for cp in local:
            cp.wait()

    anyspec = pl.BlockSpec(memory_space=pl.ANY)
    return pl.pallas_call(
        body, name=name, in_specs=[anyspec] * n, out_specs=[anyspec] * n,
        out_shape=[SDS((N_DEV,) + a.shape, a.dtype) for a in arrays],
        scratch_shapes=_exchange_sems(n))(*arrays)


def _all_reduce_small(v):
    R = v.shape[0]

    def body(v_ref, o_ref, gath, send_sems, recv_sems):
        x, y, c, me = _mesh_pos()
        gath[me] = v_ref[...]
        sends = []
        for m in range(1, N_DEV):
            peer, _ = _peer(x, y, c, m)
            cp = pltpu.make_async_remote_copy(src_ref=v_ref, dst_ref=gath.at[me], send_sem=send_sems.at[m - 1],
                                              recv_sem=recv_sems.at[m - 1], device_id=peer, device_id_type=MESH)
            cp.start()
            sends.append(cp)
        for m in range(1, N_DEV):
            peer, pid = _peer(x, y, c, m)
            pltpu.make_async_remote_copy(src_ref=v_ref, dst_ref=gath.at[pid], send_sem=send_sems.at[m - 1],
                                         recv_sem=recv_sems.at[m - 1], device_id=peer, device_id_type=MESH).wait_recv()
        for cp in sends:
            cp.wait_send()
        tot = gath[0]
        for d in range(1, N_DEV):
            tot = tot + gath[d]
        o_ref[...] = tot

    vm = pl.BlockSpec(memory_space=pltpu.VMEM)
    return pl.pallas_call(
        body, name="all_reduce_small", in_specs=[vm], out_specs=vm, out_shape=SDS((R, HP), f32),
        scratch_shapes=[pltpu.VMEM((N_DEV, R, HP), f32), pltpu.SemaphoreType.DMA((N_DEV - 1,)), pltpu.SemaphoreType.DMA((N_DEV - 1,))],
        )(v)


def _adamw_math(w, g, m, v):
    m = ADAM_B1 * m + (1.0 - ADAM_B1) * g
    v = ADAM_B2 * v + (1.0 - ADAM_B2) * (g * g)
    m_hat = m / (1.0 - ADAM_B1 ** ADAM_STEP)
    v_hat = v / (1.0 - ADAM_B2 ** ADAM_STEP)
    return -ADAM_LR * (m_hat / (jnp.sqrt(v_hat) + ADAM_EPS) + ADAM_WD * w), m, v


def _adamw(g, w, m, v, name):
    R, Cc = w.shape[-2:]
    if R <= 512 or R % 128 == 0:
        TR, TC = (R if R <= 512 else _pick(R, (256, 128))), Cc
    else:
        TR, TC = R, _pick(Cc, (256, 128))
    slabs = g.ndim == 3
    lead = w.ndim - 2

    def body(g_ref, w_ref, m_ref, v_ref, go_ref, d_ref, mo_ref, vo_ref):
        if slabs:
            gs = g_ref[0].astype(f32)
            for k in range(1, N_DEV):
                gs = gs + g_ref[k].astype(f32)
        else:
            gs = g_ref[...]
        at = 0 if lead else Ellipsis
        d, mn, vn = _adamw_math(w_ref[at], gs, m_ref[at], v_ref[at])
        go_ref[at], d_ref[at], mo_ref[at], vo_ref[at] = gs, d, mn, vn

    grid = (R // TR, Cc // TC)
    blk = pl.BlockSpec((1,) * lead + (TR, TC), lambda i, j: (0,) * lead + (i, j))
    gblk = pl.BlockSpec((N_DEV, TR, TC), lambda i, j: (0, i, j)) if slabs else pl.BlockSpec((TR, TC), lambda i, j: (i, j))
    return pl.pallas_call(
        body, name=name, grid=grid, in_specs=[gblk, blk, blk, blk], out_specs=[blk] * 4,
        out_shape=[SDS(w.shape, f32)] * 4, compiler_params=_cp(2))(g, w, m, v)


WEIGHTS = ("meta_tokens", "mix_norm_w", "w_in", "fox_forget_bias", "dn_conv_w", "dn_a_log", "dn_dt_bias", "dn_out_norm_w",
           "w_branch_fox", "w_branch_dn", "w_out", "ffn_norm_w", "w_ffn_gate", "w_ffn_up", "w_ffn_down", "final_norm_w")
COL_SHARDED = ("w_in", "w_branch_fox", "w_branch_dn", "w_ffn_gate", "w_ffn_up")
ROW_SHARDED = ("w_out", "w_ffn_down")
BIG = COL_SHARDED + ROW_SHARDED
LATE = tuple(n for n in BIG if n != "w_in")
LATE_MIX = ("w_branch_fox", "w_branch_dn", "w_out")
LATE_FFN = ("w_ffn_gate", "w_ffn_up", "w_ffn_down")
SMALL = tuple(n for n in WEIGHTS if n not in BIG)
TRANSPOSED = ("w_in", "w_ffn_gate", "w_ffn_up")


def _to_slabs(name, g):
    r, c = g.shape
    if name in COL_SHARDED and name not in TRANSPOSED:
        return _b(g.reshape(r, N_DEV, c // N_DEV).transpose(1, 0, 2))
    return _b(g.reshape(N_DEV, r // N_DEV, c))


def _from_slabs(name, s):
    n, r, c = s.shape
    if name in COL_SHARDED and name not in TRANSPOSED:
        return s.transpose(1, 0, 2).reshape(r, n * c)
    return s.reshape(n * r, c)


def kernel(x, meta_tokens, mix_norm_w, w_in, fox_forget_bias, dn_conv_w, dn_a_log, dn_dt_bias, dn_out_norm_w, w_branch_fox, w_branch_dn, w_out, ffn_norm_w, w_ffn_gate, w_ffn_up, w_ffn_down, final_norm_w, loss_target, m_meta_tokens, m_mix_norm_w, m_w_in, m_fox_forget_bias, m_dn_conv_w, m_dn_a_log, m_dn_dt_bias, m_dn_out_norm_w, m_w_branch_fox, m_w_branch_dn, m_w_out, m_ffn_norm_w, m_w_ffn_gate, m_w_ffn_up, m_w_ffn_down, m_final_norm_w, v_meta_tokens, v_mix_norm_w, v_w_in, v_fox_forget_bias, v_dn_conv_w, v_dn_a_log, v_dn_dt_bias, v_dn_out_norm_w, v_w_branch_fox, v_w_branch_dn, v_w_out, v_ffn_norm_w, v_w_ffn_gate, v_w_ffn_up, v_w_ffn_down, v_final_norm_w):
    w = dict(meta_tokens=meta_tokens, mix_norm_w=mix_norm_w, w_in=w_in, fox_forget_bias=fox_forget_bias, dn_conv_w=dn_conv_w, dn_a_log=dn_a_log, dn_dt_bias=dn_dt_bias, dn_out_norm_w=dn_out_norm_w, w_branch_fox=w_branch_fox, w_branch_dn=w_branch_dn, w_out=w_out, ffn_norm_w=ffn_norm_w, w_ffn_gate=w_ffn_gate, w_ffn_up=w_ffn_up, w_ffn_down=w_ffn_down, final_norm_w=final_norm_w)
    mom = dict(meta_tokens=m_meta_tokens, mix_norm_w=m_mix_norm_w, w_in=m_w_in, fox_forget_bias=m_fox_forget_bias, dn_conv_w=m_dn_conv_w, dn_a_log=m_dn_a_log, dn_dt_bias=m_dn_dt_bias, dn_out_norm_w=m_dn_out_norm_w, w_branch_fox=m_w_branch_fox, w_branch_dn=m_w_branch_dn, w_out=m_w_out, ffn_norm_w=m_ffn_norm_w, w_ffn_gate=m_w_ffn_gate, w_ffn_up=m_w_ffn_up, w_ffn_down=m_w_ffn_down, final_norm_w=m_final_norm_w)
    var = dict(meta_tokens=v_meta_tokens, mix_norm_w=v_mix_norm_w, w_in=v_w_in, fox_forget_bias=v_fox_forget_bias, dn_conv_w=v_dn_conv_w, dn_a_log=v_dn_a_log, dn_dt_bias=v_dn_dt_bias, dn_out_norm_w=v_dn_out_norm_w, w_branch_fox=v_w_branch_fox, w_branch_dn=v_w_branch_dn, w_out=v_w_out, ffn_norm_w=v_ffn_norm_w, w_ffn_gate=v_w_ffn_gate, w_ffn_up=v_w_ffn_up, w_ffn_down=v_w_ffn_down, final_norm_w=v_final_norm_w)
    two_d = lambda a: a.reshape(a.shape[-2:]) if a.ndim >= 2 else a.reshape(1, -1)
    me = 4 * lax.axis_index("x") + 2 * lax.axis_index("y") + lax.axis_index("c")
    for d in (w, mom, var):
        for n in TRANSPOSED:
            d[n] = jnp.swapaxes(d[n], -1, -2)

    g_in, g_meta, g_cw = _gather_two_level([_b(two_d(w["w_in"])), two_d(w["meta_tokens"]), two_d(w["dn_conv_w"])], "all_gather_early")
    meta = g_meta.transpose(1, 0, 2).reshape(N_META, -1)
    cw = g_cw.transpose(1, 0, 2).reshape(CONV_K, -1)

    gx, g_small, recv = _local_step(
        x[0], loss_target[0], meta, two_d(w["mix_norm_w"]), _from_slabs("w_in", g_in), two_d(w["fox_forget_bias"]), cw, two_d(w["dn_a_log"]),
        two_d(w["dn_dt_bias"]), two_d(w["dn_out_norm_w"]), two_d(w["ffn_norm_w"]), two_d(w["final_norm_w"]),
        {n: _b(two_d(w[n])) for n in LATE})

    order = ("loss",) + SMALL
    flat = jnp.concatenate([g_small[n].reshape(-1) for n in order])
    rows = -(-flat.shape[0] // (8 * HP)) * 8
    tot = _all_reduce_small(jnp.pad(flat, (0, rows * HP - flat.shape[0])).reshape(rows, HP)).reshape(-1)
    summed, o = {}, 0
    for n in order:
        k = g_small[n].shape[0]
        summed[n] = tot[o:o + k]
        o += k
    loss = summed["loss"][0]
    d_model = x.shape[-1]
    mcols, ccols = d_model // N_DEV, dn_conv_w.shape[-1]
    summed["meta_tokens"] = lax.dynamic_slice(summed["meta_tokens"].reshape(N_META, d_model), (0, me * mcols), (N_META, mcols)).reshape(-1)
    summed["dn_conv_w"] = lax.dynamic_slice(summed["dn_conv_w"].reshape(CONV_K, ccols * N_DEV), (0, me * ccols), (CONV_K, ccols)).reshape(-1)

    res = {}
    for n in BIG:
        res[n] = _adamw(recv[n], w[n], mom[n], var[n], "adamw_" + n)
        if n in TRANSPOSED:
            res[n] = [jnp.swapaxes(r, -1, -2) for r in res[n]]
    sizes = [summed[n].shape[0] for n in SMALL]
    srows = -(-sum(sizes) // (8 * HP)) * 8
    pack = lambda d: jnp.pad(jnp.concatenate([d[n].reshape(-1) for n in SMALL]), (0, srows * HP - sum(sizes))).reshape(srows, HP)
    sres = _adamw(pack(summed), pack(w), pack(mom), pack(var), "adamw_small")
    o = 0
    for n, k in zip(SMALL, sizes):
        res[n] = [r.reshape(-1)[o:o + k].reshape(w[n].shape) for r in sres]
        o += k
    return (loss, gx[None], *[res[n][0] for n in WEIGHTS], *[res[n][1] for n in WEIGHTS], *[res[n][2] for n in WEIGHTS], *[res[n][3] for n in WEIGHTS])
```

```python
import functools

import jax
import jax.numpy as jnp
from jax import lax
from jax.experimental import pallas as pl
from jax.experimental.pallas import tpu as pltpu

f32, bf16 = jnp.float32, jnp.bfloat16
HI = lax.Precision.HIGHEST
MESH = pl.DeviceIdType.MESH
SDS = jax.ShapeDtypeStruct

N_DEV = 8
N_META = 16
PREFIX = 128
N_PAD = PREFIX - N_META
FOX_H, FOX_D = 8, 64
DN_H, DN_D = 4, 128
DN_C = 64
CONV_K = 4
HP = 128
SMALL_W = 256
EPS = 1e-6
NEG = -1e30
C_Q0, C_K0 = 64, 67
LSE_COL = 64
LOG2E, LN2 = 1.4426950408889634, 0.6931471805599453
C_LSE0, C_DELTA0 = 70, 65

ADAM_LR, ADAM_B1, ADAM_B2, ADAM_EPS, ADAM_WD, ADAM_STEP = 0.001, 0.9, 0.999, 1e-08, 0.01, 10

VMEM_LIMIT_V7X = 56 * 1024 * 1024
ROW_TILES = (384, 128)
ATTN_TILES = (384, 128)
FFN_TILES = (192, 64)
FOX_HEAD_GROUP = 4
FOX_HEAD_GROUP_FWD = 4
MAX_WGRAD_BLOCK = 1408
DN_INTRA_GROUP = (6, 3, 2, 1)
DN_SCAN_GROUP = (6, 3, 2, 1)


def _pick(n, cands):
    for c in cands:
        if n % c == 0:
            return c
    raise ValueError(f"no tile of {cands} divides {n}")


def _cp(n_axes=1):
    return pltpu.CompilerParams(dimension_semantics=("arbitrary",) * n_axes, vmem_limit_bytes=VMEM_LIMIT_V7X)


def _b(x):
    return x.astype(bf16)


def _dot(a, b):
    return jnp.dot(a, b, preferred_element_type=f32)


def _dot_nt(a, b):
    return lax.dot_general(a, b, (((1,), (1,)), ((), ())), preferred_element_type=f32)


def _dot_tn(a, b):
    return lax.dot_general(a, b, (((0,), (0,)), ((), ())), preferred_element_type=f32)


def _dot_hi(a, b):
    return jnp.dot(a, b, preferred_element_type=f32, precision=HI)


def _iota(shape, dim):
    return lax.broadcasted_iota(jnp.int32, shape, dim)


def _rms(x, w):
    return x * lax.rsqrt(jnp.mean(x * x, axis=-1, keepdims=True) + EPS) * w


def _sigmoid(x):
    return jax.nn.sigmoid(x)


def _load_once(pairs, sems):
    @pl.when(pl.program_id(0) == 0)
    def _():
        cps = [pltpu.make_async_copy(src, dst, sems.at[k]) for k, (src, dst) in enumerate(pairs)]
        for cp in cps:
            cp.start()
        for cp in cps:
            cp.wait()


def _seg_layout(d_model):
    return (("fq", FOX_H * HP, bf16, "fox"), ("fk", FOX_H * HP, bf16, "fox"), ("sf", SMALL_W, f32, "fox"),
            ("fv", FOX_H * HP, bf16, "fv"),
            ("dn", 3 * DN_H * DN_D, f32, "dn"), ("sd", SMALL_W, f32, "dn"),
            ("dz", DN_H * DN_D, f32, "mix"), ("ga", d_model, f32, "mix"), ("gb", d_model, f32, "mix"))


GROUPS = ("fox", "fv", "dn", "mix")


def _group_widths(d_model):
    return [sum(wd for _, wd, _, g in _seg_layout(d_model) if g == grp) for grp in GROUPS]


def _in_proj(h0, w1, wpt, shards):
    L, D = h0.shape
    NP = wpt.shape[0]
    TM = _pick(L, ROW_TILES)
    NT = L // TM
    segs = _seg_layout(D)
    ns, n = len(segs), len(shards)
    offs, o = [], 0
    for _, wd, _, _ in segs:
        offs.append(o)
        o += wd
    assert o == NP

    def body(h_ref, w1_ref, wp_hbm, *rest):
        ins, xn_ref, outs, gouts = rest[:n], rest[n], rest[n + 1:n + 1 + ns], rest[n + 1 + ns:2 * n + 1 + ns]
        wp_v, sems = rest[2 * n + 1 + ns:2 * n + 3 + ns]
        xsems = rest[2 * n + 3 + ns:]
        _load_once([(wp_hbm, wp_v)], sems)

        _exchange_start(ins, gouts, *xsems, gather=True, when=pl.program_id(0) == 0)

        xn = _b(_rms(h_ref[...], w1_ref[...]))
        xn_ref[...] = xn
        for o_ref, off, (_, wd, _, _) in zip(outs, offs, segs):
            o_ref[...] = _dot_nt(xn, wp_v[off:off + wd, :]).astype(o_ref.dtype)

        _exchange_wait(ins, gouts, *xsems, gather=True, when=pl.program_id(0) == NT - 1)

    row = lambda wd: pl.BlockSpec((TM, wd), lambda i: (i, 0))
    anyspec = pl.BlockSpec(memory_space=pl.ANY)
    res = pl.pallas_call(
        body, name="in_proj", grid=(NT,),
        in_specs=[row(D), pl.BlockSpec((1, D), lambda i: (0, 0)), anyspec] + [anyspec] * n,
        out_specs=[row(D)] + [row(wd) for _, wd, _, _ in segs] + [anyspec] * n,
        out_shape=[SDS((L, D), bf16)] + [SDS((L, wd), dt) for _, wd, dt, _ in segs] + [SDS((N_DEV,) + a.shape, a.dtype) for a in shards],
        scratch_shapes=[pltpu.VMEM((NP, D), bf16), pltpu.SemaphoreType.DMA((1,))] + _exchange_sems(n),
        compiler_params=_cp())(h0, w1, wpt, *shards)
    return res[:1 + ns], res[1 + ns:]


def _in_proj_bwd(dgroups, wpt, h0, w1, dh1, slabs, targets):
    L, D = h0.shape
    NP = wpt.shape[0]
    TM = _pick(L, ROW_TILES)
    NT = L // TM
    widths = [g.shape[1] for g in dgroups]
    assert sum(widths) == NP
    ng, n = len(dgroups), len(slabs)

    def body(*refs):
        dg_refs, (wp_hbm, h_ref, w1_ref, dh1_ref) = refs[:ng], refs[ng:ng + 4]
        ins, (dh0_ref, acc_ref), outs = refs[ng + 4:ng + 4 + n], refs[ng + 4 + n:ng + 6 + n], refs[ng + 6 + n:ng + 6 + 2 * n]
        wp_v, sems = refs[ng + 6 + 2 * n:ng + 8 + 2 * n]
        xsems = refs[ng + 8 + 2 * n:]
        _load_once([(wp_hbm, wp_v)], sems)

        @pl.when(pl.program_id(0) == 0)
        def _():
            acc_ref[...] = jnp.zeros_like(acc_ref)

        _exchange_start(ins, outs, *xsems, gather=False, when=pl.program_id(0) == 0, targets=targets)

        dxn, off = None, 0
        for g_ref, wd in zip(dg_refs, widths):
            part = _dot(g_ref[...], wp_v[off:off + wd, :])
            dxn = part if dxn is None else dxn + part
            off += wd
        _, vjp = jax.vjp(_rms, h_ref[...], w1_ref[...])
        dh0n, dw1 = vjp(dxn)
        dh0_ref[...] = dh1_ref[...] + dh0n
        acc_ref[0:1, :] += dw1

        _exchange_wait(ins, outs, *xsems, gather=False, when=pl.program_id(0) == NT - 1, targets=targets)

    row = lambda wd: pl.BlockSpec((TM, wd), lambda i: (i, 0))
    anyspec = pl.BlockSpec(memory_space=pl.ANY)
    res = pl.pallas_call(
        body, name="in_proj_bwd", grid=(NT,),
        in_specs=[row(wd) for wd in widths] + [anyspec, row(D), pl.BlockSpec((1, D), lambda i: (0, 0)), row(D)] + [anyspec] * n,
        out_specs=[row(D), pl.BlockSpec((8, D), lambda i: (0, 0))] + [anyspec] * n,
        out_shape=[SDS((L, D), f32), SDS((8, D), f32)] + [SDS(a.shape, a.dtype) for a in slabs],
        scratch_shapes=[pltpu.VMEM((NP, D), bf16), pltpu.SemaphoreType.DMA((1,))] + _exchange_sems(n),
        compiler_params=_cp())(*dgroups, wpt, h0, w1, dh1, *slabs)
    return res[0], res[1], res[2:]


def _matmul_tn(a, b, name):
    L, R = a.shape
    C = b.shape[1]
    br = max(k for k in range(HP, MAX_WGRAD_BLOCK + 1, HP) if R % k == 0)

    def body(a_ref, b_ref, o_ref):
        o_ref[...] = _b(_dot_tn(a_ref[...], b_ref[...]))

    return pl.pallas_call(
        body, name=name, grid=(R // br,),
        in_specs=[pl.BlockSpec((L, br), lambda r: (0, r)), pl.BlockSpec((L, C), lambda r: (0, 0))],
        out_specs=pl.BlockSpec((br, C), lambda r: (r, 0)), out_shape=SDS((R, C), bf16), compiler_params=_cp())(a, b)


def _fox_prep(fq, fk, fv, sf, bias_p):
    L = fq.shape[0]
    T = HP
    NT = L // T
    W = FOX_H * HP

    def body(fq_ref, fk_ref, fv_ref, sf_ref, b_ref, qa_ref, ka_ref, va_ref, carry):
        @pl.when(pl.program_id(0) == 0)
        def _():
            carry[...] = jnp.zeros_like(carry)

        lane, row = _iota((T, HP), 1), _iota((T, HP), 0)
        logf = jnp.where(lane < FOX_H, jax.nn.log_sigmoid(sf_ref[...] + b_ref[...]), 0.0)
        c = _dot_hi((row >= lane).astype(f32), logf) + carry[...]
        carry[...] = jnp.sum(jnp.where(row == T - 1, c, 0.0), axis=0, keepdims=True)
        ones_q = jnp.where((lane >= C_K0) & (lane < C_K0 + 3), 1.0, 0.0)
        ones_k = jnp.where(((lane >= C_Q0) & (lane < C_Q0 + 3)) | ((lane >= C_LSE0) & (lane < C_LSE0 + 3)), 1.0, 0.0)
        ones_v = _b(jnp.where((lane >= LSE_COL) & (lane < C_DELTA0 + 3), 1.0, 0.0))
        for h in range(FOX_H):
            ch = jnp.broadcast_to(jnp.sum(jnp.where(lane == h, c, 0.0), axis=1, keepdims=True), (T, HP)) * LOG2E
            c1 = _b(ch).astype(f32)
            c2 = _b(ch - c1).astype(f32)
            c3 = _b(ch - c1 - c2).astype(f32)
            cq = jnp.where(lane == C_Q0, c1, 0.0) + jnp.where(lane == C_Q0 + 1, c2, 0.0) + jnp.where(lane == C_Q0 + 2, c3, 0.0)
            ck = jnp.where(lane == C_K0, c1, 0.0) + jnp.where(lane == C_K0 + 1, c2, 0.0) + jnp.where(lane == C_K0 + 2, c3, 0.0)
            q = fq_ref[:, h * HP:(h + 1) * HP].astype(f32) * (FOX_D ** -0.5 * LOG2E)
            k = fk_ref[:, h * HP:(h + 1) * HP].astype(f32)
            qa_ref[h] = _b(q + cq + ones_q)
            ka_ref[h] = _b(k + ones_k - ck)
            va_ref[:, h * HP:(h + 1) * HP] = fv_ref[:, h * HP:(h + 1) * HP] + ones_v

    wide = pl.BlockSpec((T, W), lambda i: (i, 0))
    return pl.pallas_call(
        body, name="fox_prep", grid=(NT,),
        in_specs=[wide, wide, wide, pl.BlockSpec((T, HP), lambda i: (i, 0)), pl.BlockSpec((1, HP), lambda i: (0, 0))],
        out_specs=[pl.BlockSpec((FOX_H, T, HP), lambda i: (0, i, 0))] * 2 + [wide],
        out_shape=[SDS((FOX_H, L, HP), bf16)] * 2 + [SDS((L, W), bf16)],
        scratch_shapes=[pltpu.VMEM((1, HP), f32)], compiler_params=_cp())(fq, fk, fv, sf, bias_p)


def _fox_prep_bwd(dqa, dka, sf, bias_p):
    L = sf.shape[0]
    T = HP
    NT = L // T
    rev = lambda i: (NT - 1 - i, 0)

    W = FOX_H * HP

    def body(dq_ref, dk_ref, sf_ref, b_ref, dg_ref, db_ref, carry):
        @pl.when(pl.program_id(0) == 0)
        def _():
            carry[...] = jnp.zeros_like(carry)
            db_ref[...] = jnp.zeros_like(db_ref)

        dq, dk = dq_ref[...], dk_ref[...]
        dg_ref[:, 0:W] = _b(dq * (FOX_D ** -0.5))
        dg_ref[:, W:2 * W] = _b(dk * LN2)
        lane, row = _iota((T, HP), 1), _iota((T, HP), 0)
        dc = jnp.zeros((T, HP), f32)
        for h in range(FOX_H):
            col = jnp.sum(jnp.where(lane == C_Q0, dq[:, h * HP:(h + 1) * HP], 0.0)
                          - jnp.where(lane == C_K0, dk[:, h * HP:(h + 1) * HP], 0.0), axis=1, keepdims=True)
            dc = dc + jnp.where(lane == h, col, 0.0)
        dl = _dot_hi((row <= lane).astype(f32), dc) + carry[...]
        carry[...] = jnp.sum(jnp.where(row == 0, dl, 0.0), axis=0, keepdims=True)
        dx = jnp.where(lane < FOX_H, dl * _sigmoid(-(sf_ref[...] + b_ref[...])), 0.0)
        dg_ref[:, 2 * W:2 * W + HP] = _b(dx)
        dg_ref[:, 2 * W + HP:] = jnp.zeros((T, SMALL_W - HP), bf16)
        db_ref[0:1, :] += jnp.sum(dx, axis=0, keepdims=True)

    return pl.pallas_call(
        body, name="fox_prep_bwd", grid=(NT,),
        in_specs=[pl.BlockSpec((T, W), rev), pl.BlockSpec((T, W), rev), pl.BlockSpec((T, HP), rev), pl.BlockSpec((1, HP), lambda i: (0, 0))],
        out_specs=[pl.BlockSpec((T, 2 * W + SMALL_W), rev), pl.BlockSpec((8, HP), lambda i: (0, 0))],
        out_shape=[SDS((L, 2 * W + SMALL_W), bf16), SDS((8, HP), f32)],
        scratch_shapes=[pltpu.VMEM((1, HP), f32)], compiler_params=_cp())(dqa, dka, sf, bias_p)


def _tile_start(j, T):
    return j * T if isinstance(j, int) else pl.multiple_of(j * T, T)


def _spread3(x, lane, col0):
    x1 = _b(x).astype(f32)
    x2 = _b(x - x1).astype(f32)
    x3 = _b(x - x1 - x2).astype(f32)
    return jnp.where(lane == col0, x1, 0.0) + jnp.where(lane == col0 + 1, x2, 0.0) + jnp.where(lane == col0 + 2, x3, 0.0)


def _fox_fwd(qa, ka, fv, shards):
    L = qa.shape[1]
    TQ = TK = _pick(L, ATTN_TILES)
    NQ = L // TQ
    n = len(shards)
    HG = FOX_HEAD_GROUP_FWD

    def body(q_ref, k_ref, v_ref, *rest):
        ins, o_ref, qb_ref, outs, sems = rest[:n], rest[n], rest[n + 1], rest[n + 2:2 * n + 2], rest[2 * n + 2:]
        h, i = pl.program_id(0), pl.program_id(1)

        _exchange_start(ins, outs, *sems, gather=True, when=(h == 0) & (i == 0))

        qs = [q_ref[a] for a in range(HG)]
        rowg = i * TQ + _iota((TQ, TK), 0)
        colb = _iota((TQ, TK), 1)

        def step(j, carry, masked):
            ms, accs = carry
            k0 = _tile_start(j, TK)
            ss = [_dot_nt(qs[a], k_ref[a, pl.ds(k0, TK), :]) for a in range(HG)]
            if masked:
                colg = colb + j * TK
                keep = (colg <= rowg) & (colg >= N_PAD)
                ss = [jnp.where(keep, s, NEG) for s in ss]
            m_new = [jnp.maximum(m, jnp.max(s, axis=1, keepdims=True)) for m, s in zip(ms, ss)]
            ps = [_b(jnp.exp2(s - m)) for s, m in zip(ss, m_new)]
            alphas = [jnp.exp2(m - mn) for m, mn in zip(ms, m_new)]
            accs = [al * acc + _dot(p, v_ref[pl.ds(k0, TK), a * HP:(a + 1) * HP]) for a, (al, acc, p) in enumerate(zip(alphas, accs, ps))]
            return m_new, accs

        init = ([jnp.full((TQ, 1), NEG, f32)] * HG, [jnp.zeros((TQ, HP), f32)] * HG)
        carry = step(0, init, True)
        carry = lax.fori_loop(1, i, functools.partial(step, masked=False), carry)
        ms, accs = lax.fori_loop(jnp.maximum(i, 1), i + 1, functools.partial(step, masked=True), carry)
        lane = _iota((TQ, HP), 1)
        for a in range(HG):
            l = jnp.sum(jnp.where(lane == LSE_COL, accs[a], 0.0), axis=1, keepdims=True)
            lse = ms[a] + jnp.log2(l)
            o_ref[:, a * HP:(a + 1) * HP] = jnp.where(lane == LSE_COL, lse, accs[a] / l)
            qb_ref[a] = _b(qs[a].astype(f32) - _spread3(jnp.broadcast_to(lse, (TQ, HP)), lane, C_LSE0))

        _exchange_wait(ins, outs, *sems, gather=True, when=(h == FOX_H // HG - 1) & (i == NQ - 1))

    anyspec = pl.BlockSpec(memory_space=pl.ANY)
    qtile = pl.BlockSpec((HG, TQ, HP), lambda h, i: (h, i, 0))
    res = pl.pallas_call(
        body, name="fox_fwd", grid=(FOX_H // HG, NQ),
        in_specs=[qtile, pl.BlockSpec((HG, L, HP), lambda h, i: (h, 0, 0)), pl.BlockSpec((L, HG * HP), lambda h, i: (0, h))] + [anyspec] * n,
        out_specs=[pl.BlockSpec((TQ, HG * HP), lambda h, i: (i, h)), qtile] + [anyspec] * n,
        out_shape=[SDS((L, FOX_H * HP), f32), SDS(qa.shape, bf16)] + [SDS((N_DEV,) + a.shape, a.dtype) for a in shards],
        scratch_shapes=_exchange_sems(n), compiler_params=_cp(2))(qa, ka, fv, *shards)
    return res[0], res[1], res[2:]


def _fox_bwd(qb, ka, va, dob, slabs):
    L = qb.shape[1]
    TQ = TK = _pick(L, ATTN_TILES)
    NQ = L // TQ
    n = len(slabs)
    HG = FOX_HEAD_GROUP

    def body(q_ref, k_ref, v_ref, do_ref, *rest):
        ins, (dq_ref, dk_ref, dv_ref), outs, sems = rest[:n], rest[n:n + 3], rest[n + 3:2 * n + 3], rest[2 * n + 3:]
        h, j = pl.program_id(0), pl.program_id(1)
        cols = [slice(a * HP, (a + 1) * HP) for a in range(HG)]

        _exchange_start(ins, outs, *sems, gather=False, when=(h == 0) & (j == 0))

        @pl.when(j == 0)
        def _():
            dq_ref[...] = jnp.zeros_like(dq_ref)

        kts = [k_ref[a] for a in range(HG)]
        vts = [v_ref[:, cols[a]] for a in range(HG)]
        colg = j * TK + _iota((TQ, TK), 1)
        rowb = _iota((TQ, TK), 0)

        def step(i, carry, masked):
            dks, dvs = carry
            r0 = _tile_start(i, TQ)
            rows = pl.ds(r0, TQ)
            qs = [q_ref[a, rows, :] for a in range(HG)]
            ps = [jnp.exp2(_dot_nt(q, kt)) for q, kt in zip(qs, kts)]
            if masked:
                keep = (colg <= rowb + i * TQ) & (colg >= N_PAD)
                ps = [jnp.where(keep, p, 0.0) for p in ps]
            dobs = [do_ref[rows, cols[a]] for a in range(HG)]
            dvs = [dv + _dot_tn(_b(p), dob) for dv, p, dob in zip(dvs, ps, dobs)]
            dss = [_b(p * _dot_nt(dob, vt)) for p, dob, vt in zip(ps, dobs, vts)]
            for a in range(HG):
                dq_ref[rows, cols[a]] += _dot(dss[a], kts[a])
            dks = [dk + _dot_tn(ds, q) for dk, ds, q in zip(dks, dss, qs)]
            return dks, dvs

        zeros = [jnp.zeros((TK, HP), f32)] * HG
        carry = step(j, (zeros, zeros), True)
        split = jnp.where(j == 0, NQ, j + 1)
        carry = lax.fori_loop(j + 1, split, functools.partial(step, masked=True), carry)
        dks, dvs = lax.fori_loop(split, NQ, functools.partial(step, masked=False), carry)
        for a in range(HG):
            dk_ref[:, cols[a]] = dks[a]
            dv_ref[:, cols[a]] = _b(dvs[a])

        _exchange_wait(ins, outs, *sems, gather=False, when=(h == FOX_H // HG - 1) & (j == NQ - 1))

    head = pl.BlockSpec((L, HG * HP), lambda h, j: (0, h))
    tile = pl.BlockSpec((TK, HG * HP), lambda h, j: (j, h))
    anyspec = pl.BlockSpec(memory_space=pl.ANY)
    res = pl.pallas_call(
        body, name="fox_bwd", grid=(FOX_H // HG, L // TK),
        in_specs=[pl.BlockSpec((HG, L, HP), lambda h, j: (h, 0, 0)), pl.BlockSpec((HG, TK, HP), lambda h, j: (h, j, 0)), tile, head]
        + [anyspec] * n,
        out_specs=[head, tile, tile] + [anyspec] * n,
        out_shape=[SDS((L, FOX_H * HP), f32), SDS((L, FOX_H * HP), f32), SDS((L, FOX_H * HP), bf16)] + [SDS(a.shape, a.dtype) for a in slabs],
        scratch_shapes=_exchange_sems(n), compiler_params=_cp(2))(qb, ka, va, dob, *slabs)
    return res[:3], res[3:]


def _dn_post(y, sd, alog_p, dt_p, valid):
    a = y * _sigmoid(y)
    W = DN_H * DN_D
    heads = []
    for part, scale in ((0, DN_D ** -0.5), (1, 1.0)):
        for h in range(DN_H):
            xh = a[:, part * W + h * DN_D:part * W + (h + 1) * DN_D]
            heads.append(xh * lax.rsqrt(jnp.sum(xh * xh, axis=-1, keepdims=True) + EPS) * scale)
    q = jnp.concatenate(heads[:DN_H], axis=1)
    k = jnp.concatenate(heads[DN_H:], axis=1)
    v = a[:, 2 * W:3 * W]
    lane = _iota(sd.shape, 1)
    beta = _sigmoid(sd) * valid
    g = -jnp.exp(alog_p) * jax.nn.softplus(sd + dt_p) * valid
    bg = jnp.where(lane < DN_H, beta, jnp.where(lane < 2 * DN_H, g, 0.0))
    return q, k, v, bg


def _conv_fwd(ext_ref, cw_ref, TM):
    y = cw_ref[0:1, :] * ext_ref[8 - (CONV_K - 1):8 - (CONV_K - 1) + TM, :]
    for i in range(1, CONV_K):
        o = 8 - (CONV_K - 1) + i
        y = y + cw_ref[i:i + 1, :] * ext_ref[o:o + TM, :]
    return y


def _dn_prep(dn, sd, cw, alog_p, dt_p):
    L, W3 = dn.shape
    TM = _pick(L, ROW_TILES)
    W = DN_H * DN_D

    def body(dn_ref, halo_ref, sd_ref, cw_ref, al_ref, dt_ref, q_ref, k_ref, v_ref, bg_ref, ext):
        i = pl.program_id(0)
        ext[0:8, :] = jnp.where(i == 0, 0.0, halo_ref[...])
        ext[8:, :] = dn_ref[...]
        y = _conv_fwd(ext, cw_ref, TM)
        valid = ((i * TM + _iota((TM, 1), 0)) >= N_PAD).astype(f32)
        q, k, v, bg = _dn_post(y, sd_ref[...], al_ref[...], dt_ref[...], valid)
        q_ref[...], k_ref[...], v_ref[...], bg_ref[...] = q, k, v, bg

    row = lambda wd: pl.BlockSpec((TM, wd), lambda i: (i, 0))
    vec = pl.BlockSpec((1, HP), lambda i: (0, 0))
    return pl.pallas_call(
        body, name="dn_prep", grid=(L // TM,),
        in_specs=[row(W3), pl.BlockSpec((8, W3), lambda i: (jnp.maximum(i * (TM // 8) - 1, 0), 0)), row(HP),
                  pl.BlockSpec((CONV_K, W3), lambda i: (0, 0)), vec, vec],
        out_specs=[row(W), row(W), row(W), row(HP)],
        out_shape=[SDS((L, W), f32)] * 3 + [SDS((L, HP), f32)],
        scratch_shapes=[pltpu.VMEM((TM + 8, W3), f32)], compiler_params=_cp())(dn, dn, sd, cw, alog_p, dt_p)


def _dn_prep_bwd(dn, sd, cw, alog_p, dt_p, dq, dk, dv, dbg):
    L, W3 = dn.shape
    TM = _pick(L, ROW_TILES)
    NT = L // TM
    W = DN_H * DN_D

    def body(dn_ref, halo_ref, sd_ref, cw_ref, al_ref, dt_ref, dq_ref, dk_ref, dv_ref, dbg_ref,
             dg_ref, dcw_ref, dp_ref, ext, dyp, carry):
        i = pl.program_id(0)
        t = NT - 1 - i

        @pl.when(i == 0)
        def _():
            carry[...] = jnp.zeros_like(carry)
            dcw_ref[...] = jnp.zeros_like(dcw_ref)
            dp_ref[...] = jnp.zeros_like(dp_ref)
            dyp[...] = jnp.zeros_like(dyp)

        ext[0:8, :] = jnp.where(t == 0, 0.0, halo_ref[...])
        ext[8:, :] = dn_ref[...]
        y = _conv_fwd(ext, cw_ref, TM)
        valid = ((t * TM + _iota((TM, 1), 0)) >= N_PAD).astype(f32)
        _, vjp = jax.vjp(functools.partial(_dn_post, valid=valid), y, sd_ref[...], al_ref[...], dt_ref[...])
        dy, dsd, dal, ddt = vjp((dq_ref[...], dk_ref[...], dv_ref[...], dbg_ref[...]))
        dg_ref[:, W3:W3 + HP] = _b(dsd)
        dg_ref[:, W3 + HP:] = jnp.zeros((TM, SMALL_W - HP), bf16)
        dp_ref[0:1, :] += dal
        dp_ref[1:2, :] += ddt
        dyp[8:8 + TM, :] = dy
        o0 = CONV_K - 1
        dext = cw_ref[0:1, :] * dyp[o0:o0 + TM + 8, :]
        for k in range(1, CONV_K):
            dext = dext + cw_ref[k:k + 1, :] * dyp[o0 - k:o0 - k + TM + 8, :]
        for k in range(CONV_K):
            o = 8 - (CONV_K - 1) + k
            dcw_ref[k:k + 1, :] += jnp.sum(dy * ext[o:o + TM, :], axis=0, keepdims=True)
        dg_ref[:, 0:W3] = _b(jnp.concatenate([dext[8:TM, :], dext[TM:TM + 8, :] + carry[...]], axis=0))
        carry[...] = dext[0:8, :]

    row = lambda wd: pl.BlockSpec((TM, wd), lambda i: (NT - 1 - i, 0))
    vec = pl.BlockSpec((1, HP), lambda i: (0, 0))
    return pl.pallas_call(
        body, name="dn_prep_bwd", grid=(NT,),
        in_specs=[row(W3), pl.BlockSpec((8, W3), lambda i: (jnp.maximum((NT - 1 - i) * (TM // 8) - 1, 0), 0)), row(HP),
                  pl.BlockSpec((CONV_K, W3), lambda i: (0, 0)), vec, vec, row(W), row(W), row(W), row(HP)],
        out_specs=[row(W3 + SMALL_W), pl.BlockSpec((8, W3), lambda i: (0, 0)), pl.BlockSpec((8, HP), lambda i: (0, 0))],
        out_shape=[SDS((L, W3 + SMALL_W), bf16), SDS((8, W3), f32), SDS((8, HP), f32)],
        scratch_shapes=[pltpu.VMEM((TM + 8, W3), f32), pltpu.VMEM((TM + 16, W3), f32), pltpu.VMEM((8, W3), f32)],
        compiler_params=_cp())(dn, dn, sd, cw, alog_p, dt_p, dq, dk, dv, dbg)


def _split2(x):
    hi = _b(x)
    return hi, _b(x - hi.astype(f32))


def _split3(x):
    hi = _b(x)
    r = x - hi.astype(f32)
    mid = _b(r)
    return hi, mid, _b(r - mid.astype(f32))


def _x3(a, b, dot):
    (a1, a2), (b1, b2) = _split2(a), _split2(b)
    return dot(a1, b1) + (dot(a1, b2) + dot(a2, b1))


@jax.custom_vjp
def _dot_x3(a, b):
    return _x3(a, b, _dot)


_dot_x3.defvjp(lambda a, b: (_x3(a, b, _dot), (a, b)), lambda res, g: (_x3(g, res[1], _dot_nt), _x3(res[0], g, _dot_tn)))


def _exact3(m, x, dot):
    x1, x2, x3 = _split3(x)
    return dot(m, x1) + (dot(m, x2) + dot(m, x3))


def _tri_ones(C, lower):
    row, col = _iota((C, C), 0), _iota((C, C), 1)
    return _b(((row >= col) if lower else (row <= col)).astype(f32))


@jax.custom_vjp
def _chunk_cumsum(x):
    return _exact3(_tri_ones(x.shape[0], True), x, _dot)


_chunk_cumsum.defvjp(lambda x: (_exact3(_tri_ones(x.shape[0], True), x, _dot), None),
                     lambda _, g: (_exact3(_tri_ones(g.shape[0], False), g, _dot),))


def _mxu_transpose(x):
    C = x.shape[0]
    eye = _b((_iota((C, C), 0) == _iota((C, C), 1)).astype(f32))
    return _exact3(eye, x, lambda m, part: _dot_tn(part, m))


@jax.custom_vjp
def _transpose_exact(x):
    return _mxu_transpose(x)


_transpose_exact.defvjp(lambda x: (_mxu_transpose(x), None), lambda _, g: (_mxu_transpose(g),))


def _unit_lower_inverses(lows):
    C = lows[0].shape[0]
    P = jnp.stack(lows)
    X = (_iota((C, C), 0) == _iota((C, C), 1)).astype(f32)[None] - P
    bdot = functools.partial(_x3, dot=lambda a, b: jnp.einsum("bij,bjk->bik", a, b, preferred_element_type=f32))
    for _ in range(5):
        P = bdot(P, P)
        X = X + bdot(X, P)
    return [X[i] for i in range(len(lows))]


@jax.custom_vjp
def _inverse_given(low, X):
    return X


def _inverse_given_bwd(X, g):
    return -_x3(_x3(X, g, _dot_tn), X, _dot_nt), jnp.zeros_like(X)


_inverse_given.defvjp(lambda low, X: (X, X), _inverse_given_bwd)


def _dn_intra_pre(q, k, v, bg):
    C = DN_C
    row, col = _iota((C, C), 0), _iota((C, C), 1)
    tri = row >= col
    G = _chunk_cumsum(bg)
    GT = _transpose_exact(G)
    lane = _iota((C, HP), 1)
    rowt = _iota((HP, C), 0)
    last = _iota((C, 1), 0) == C - 1
    heads = []
    for h in range(DN_H):
        beta = jnp.sum(jnp.where(lane == h, bg, 0.0), axis=1, keepdims=True)
        gcol = jnp.sum(jnp.where(lane == DN_H + h, G, 0.0), axis=1, keepdims=True)
        grow = jnp.sum(jnp.where(rowt == DN_H + h, GT, 0.0), axis=0, keepdims=True)
        glast = jnp.sum(jnp.where(last, gcol, 0.0), axis=0, keepdims=True)
        decay = jnp.exp(jnp.where(tri, gcol - grow, NEG))
        qh, kh, vh = (t[:, h * DN_D:(h + 1) * DN_D] for t in (q, k, v))
        kb = kh * beta
        low = jnp.where(row > col, _dot_nt(_b(kb), _b(kh)) * decay, 0.0)
        heads.append((beta, gcol, glast, decay, qh, kh, vh, kb, low))
    return heads


def _dn_intra_post(heads, xs):
    lane1 = _iota((1, HP), 1)
    us, ws, qds, kds, attns = [], [], [], [], []
    glrow = jnp.zeros((1, HP), f32)
    for h, ((beta, gcol, glast, decay, qh, kh, vh, kb, _), X) in enumerate(zip(heads, xs)):
        eg = jnp.exp(gcol)
        us.append(_dot_x3(X, vh * beta))
        ws.append(_dot_x3(X, kb * eg))
        attns.append(_dot_nt(_b(qh), _b(kh)) * decay)
        qds.append(qh * eg)
        kds.append(kh * jnp.exp(glast - gcol))
        glrow = glrow + jnp.where(lane1 == h, glast, 0.0)
    cat = lambda xs_: jnp.concatenate(xs_, axis=1)
    return cat(us), cat(ws), cat(qds), cat(kds), cat(attns), glrow, cat(list(xs))


def _dn_intra_group(q, k, v, bg, xs):
    G = q.shape[0] // DN_C
    rows = [slice(j * DN_C, (j + 1) * DN_C) for j in range(G)]
    pre = [_dn_intra_pre(q[r, :], k[r, :], v[r, :], bg[r, :]) for r in rows]
    inv = [[_inverse_given(hd[-1], x) for hd, x in zip(heads, xj)] for heads, xj in zip(pre, xs)]
    post = [_dn_intra_post(heads, xj) for heads, xj in zip(pre, inv)]
    return tuple(jnp.concatenate([p[i] for p in post], axis=0) for i in range(5)) + (tuple(p[5] for p in post),)


def _lane_pick(rowvec, h):
    return jnp.sum(jnp.where(_iota(rowvec.shape, 1) == h, rowvec, 0.0), axis=1, keepdims=True)


def _dn_intra(q, k, v, bg):
    L, W = q.shape
    NC = L // DN_C
    G = _pick(NC, DN_INTRA_GROUP)
    R = G * DN_C
    WA = DN_H * DN_C

    def body(q_ref, k_ref, v_ref, bg_ref, u_ref, w_ref, qd_ref, kd_ref, at_ref, gl_ref, x_ref):
        rows = [slice(j * DN_C, (j + 1) * DN_C) for j in range(G)]
        pre = [_dn_intra_pre(q_ref[r, :], k_ref[r, :], v_ref[r, :], bg_ref[r, :]) for r in rows]
        inv = _unit_lower_inverses([hd[-1] for heads in pre for hd in heads])
        for j, r in enumerate(rows):
            u, w, qd, kd, at, gl, xs = _dn_intra_post(pre[j], inv[j * DN_H:(j + 1) * DN_H])
            u_ref[r, :], x_ref[r, :] = u, xs
            w_ref[r, :], qd_ref[r, :], kd_ref[r, :], at_ref[r, :] = _b(w), _b(qd), _b(kd), _b(at)
            gl_ref[j] = gl

    row = lambda wd: pl.BlockSpec((R, wd), lambda n: (n, 0))
    return pl.pallas_call(
        body, name="dn_intra", grid=(NC // G,),
        in_specs=[row(W), row(W), row(W), row(HP)],
        out_specs=[row(W), row(W), row(W), row(W), row(WA), pl.BlockSpec((G, 1, HP), lambda n: (n, 0, 0)), row(WA)],
        out_shape=[SDS((L, W), f32), SDS((L, W), bf16), SDS((L, W), bf16), SDS((L, W), bf16), SDS((L, WA), bf16), SDS((NC, 1, HP), f32),
                   SDS((L, WA), f32)],
        compiler_params=_cp())(q, k, v, bg)


def _dn_scan(u, w, qd, kd, at, gl):
    L, W = u.shape
    NC = L // DN_C
    G = _pick(NC, DN_SCAN_GROUP)
    R = G * DN_C

    def body(u_ref, w_ref, qd_ref, kd_ref, at_ref, gl_ref, o_ref, vn_ref, s_ref, S):
        @pl.when(pl.program_id(0) == 0)
        def _():
            S[...] = jnp.zeros_like(S)

        for j in range(G):
            r = slice(j * DN_C, (j + 1) * DN_C)
            glrow = gl_ref[j]
            for h in range(DN_H):
                c = slice(h * DN_D, (h + 1) * DN_D)
                Sh = S[h]
                s_ref[j, h] = Sh
                Sb = _b(Sh)
                vb = _b(u_ref[r, c] - _dot(w_ref[r, c], Sb))
                vn_ref[r, c] = vb
                o_ref[r, c] = _dot(qd_ref[r, c], Sb) + _dot(at_ref[r, h * DN_C:(h + 1) * DN_C], vb)
                S[h] = Sh * jnp.exp(_lane_pick(glrow, h)) + _dot_tn(kd_ref[r, c], vb)

    row = lambda wd: pl.BlockSpec((R, wd), lambda n: (n, 0))
    return pl.pallas_call(
        body, name="dn_scan", grid=(NC // G,),
        in_specs=[row(W), row(W), row(W), row(W), row(DN_H * DN_C), pl.BlockSpec((G, 1, HP), lambda n: (n, 0, 0))],
        out_specs=[row(W), row(W), pl.BlockSpec((G, DN_H, DN_D, DN_D), lambda n: (n, 0, 0, 0))],
        out_shape=[SDS((L, W), f32), SDS((L, W), bf16), SDS((NC, DN_H, DN_D, DN_D), f32)],
        scratch_shapes=[pltpu.VMEM((DN_H, DN_D, DN_D), f32)], compiler_params=_cp())(u, w, qd, kd, at, gl)


def _dn_scan_bwd(do, w, qd, kd, at, gl):
    L, W = do.shape
    NC = L // DN_C
    G = _pick(NC, DN_SCAN_GROUP)
    R = G * DN_C
    NS = NC // G

    def body(do_ref, w_ref, qd_ref, kd_ref, at_ref, gl_ref, dvn_ref, ds_ref, dS):
        @pl.when(pl.program_id(0) == 0)
        def _():
            dS[...] = jnp.zeros_like(dS)

        for j in reversed(range(G)):
            r = slice(j * DN_C, (j + 1) * DN_C)
            glrow = gl_ref[j]
            for h in range(DN_H):
                c = slice(h * DN_D, (h + 1) * DN_D)
                dSo = dS[h]
                ds_ref[j, h] = dSo
                dob = _b(do_ref[r, c])
                dvn = _dot_tn(at_ref[r, h * DN_C:(h + 1) * DN_C], dob) + _dot(kd_ref[r, c], _b(dSo))
                dvn_ref[r, c] = dvn
                dS[h] = _dot_tn(qd_ref[r, c], dob) + dSo * jnp.exp(_lane_pick(glrow, h)) - _dot_tn(w_ref[r, c], _b(dvn))

    row = lambda wd: pl.BlockSpec((R, wd), lambda n: (NS - 1 - n, 0))
    return pl.pallas_call(
        body, name="dn_scan_bwd", grid=(NS,),
        in_specs=[row(W), row(W), row(W), row(W), row(DN_H * DN_C), pl.BlockSpec((G, 1, HP), lambda n: (NS - 1 - n, 0, 0))],
        out_specs=[row(W), pl.BlockSpec((G, DN_H, DN_D, DN_D), lambda n: (NS - 1 - n, 0, 0, 0))],
        out_shape=[SDS((L, W), f32), SDS((NC, DN_H, DN_D, DN_D), f32)],
        scratch_shapes=[pltpu.VMEM((DN_H, DN_D, DN_D), f32)], compiler_params=_cp())(do, w, qd, kd, at, gl)


def _dn_intra_bwd(q, k, v, bg, xinv, do, vn, dvn, states, dstates, slabs, targets):
    L, W = q.shape
    NC = L // DN_C
    G = _pick(NC, DN_INTRA_GROUP)
    R = G * DN_C
    NS = NC // G
    n = len(slabs)

    def body(q_ref, k_ref, v_ref, bg_ref, x_ref, do_ref, vn_ref, dvn_ref, s_ref, ds_ref, *rest):
        ins, (dq_ref, dk_ref, dv_ref, dbg_ref), outs, sems = rest[:n], rest[n:n + 4], rest[n + 4:2 * n + 4], rest[2 * n + 4:]

        _exchange_start(ins, outs, *sems, gather=False, when=pl.program_id(0) == 0, targets=targets)

        lane1 = _iota((1, HP), 1)
        rows = [slice(j * DN_C, (j + 1) * DN_C) for j in range(G)]
        xs = [[x_ref[r, h * DN_C:(h + 1) * DN_C] for h in range(DN_H)] for r in rows]
        fwd, vjp = jax.vjp(functools.partial(_dn_intra_group, xs=xs), q_ref[...], k_ref[...], v_ref[...], bg_ref[...])
        dws, dqds, dkds, dats, dgls = [], [], [], [], []
        for j, r in enumerate(rows):
            dw, dqd, dkd, dat = [], [], [], []
            dgl = jnp.zeros((1, HP), f32)
            for h in range(DN_H):
                c = slice(h * DN_D, (h + 1) * DN_D)
                Sh, dSo = s_ref[j, h], ds_ref[j, h]
                Sb, dob, vb = _b(Sh), _b(do_ref[r, c]), vn_ref[r, c]
                dw.append(-_dot_nt(_b(dvn_ref[r, c]), Sb))
                dqd.append(_dot_nt(dob, Sb))
                dat.append(_dot_nt(dob, vb))
                dkd.append(_dot_nt(vb, _b(dSo)))
                dcd = jnp.sum(jnp.sum(Sh * dSo, axis=1, keepdims=True), axis=0, keepdims=True)
                dgl = dgl + jnp.where(lane1 == h, dcd * jnp.exp(_lane_pick(fwd[5][j], h)), 0.0)
            cat = lambda xs_: jnp.concatenate(xs_, axis=1)
            dws.append(cat(dw)), dqds.append(cat(dqd)), dkds.append(cat(dkd)), dats.append(cat(dat)), dgls.append(dgl)
        cat0 = lambda xs_: jnp.concatenate(xs_, axis=0)
        dq, dk, dv, dbg = vjp((dvn_ref[...], cat0(dws), cat0(dqds), cat0(dkds), cat0(dats), tuple(dgls)))
        dq_ref[...], dk_ref[...], dv_ref[...], dbg_ref[...] = dq, dk, dv, dbg

        _exchange_wait(ins, outs, *sems, gather=False, when=pl.program_id(0) == NS - 1, targets=targets)

    row = lambda wd: pl.BlockSpec((R, wd), lambda s: (s, 0))
    st = pl.BlockSpec((G, DN_H, DN_D, DN_D), lambda s: (s, 0, 0, 0))
    anyspec = pl.BlockSpec(memory_space=pl.ANY)
    res = pl.pallas_call(
        body, name="dn_intra_bwd", grid=(NS,),
        in_specs=[row(W), row(W), row(W), row(HP), row(DN_H * DN_C), row(W), row(W), row(W), st, st] + [anyspec] * n,
        out_specs=[row(W), row(W), row(W), row(HP)] + [anyspec] * n,
        out_shape=[SDS((L, W), f32)] * 3 + [SDS((L, HP), f32)] + [SDS(a.shape, a.dtype) for a in slabs],
        scratch_shapes=_exchange_sems(n), compiler_params=_cp())(q, k, v, bg, xinv, do, vn, dvn, states, dstates, *slabs)
    return res[:4], res[4:]


def _dn_normgate(oraw, dz, wn):
    outs = []
    for h in range(DN_H):
        sl = slice(h * DN_D, (h + 1) * DN_D)
        z = dz[:, sl]
        outs.append(_rms(oraw[:, sl], wn) * (z * _sigmoid(z)))
    return jnp.concatenate(outs, axis=1)


def _mix_fwd(op, oraw, dz, ga, gb, h0, wn, wbf, wbd, wo):
    L, D = h0.shape
    TM = _pick(L, ROW_TILES)

    def body(op_ref, or_ref, dz_ref, ga_ref, gb_ref, h0_ref, wn_ref, wbf_ref, wbd_ref, wo_ref, h1_ref):
        pf = _dot(_b(op_ref[...]), wbf_ref[...])
        pd = _dot(_b(_dn_normgate(or_ref[...], dz_ref[...], wn_ref[...])), wbd_ref[...])
        y = _sigmoid(ga_ref[...]) * pf + _sigmoid(gb_ref[...]) * pd
        h1_ref[...] = h0_ref[...] + _dot(_b(y), wo_ref[...])

    row = lambda wd: pl.BlockSpec((TM, wd), lambda i: (i, 0))
    full = lambda a: pl.BlockSpec(a.shape, lambda i: (0, 0))
    return pl.pallas_call(
        body, name="mix_fwd", grid=(L // TM,),
        in_specs=[row(op.shape[1]), row(oraw.shape[1]), row(dz.shape[1]), row(D), row(D), row(D), full(wn), full(wbf), full(wbd), full(wo)],
        out_specs=row(D), out_shape=SDS((L, D), f32), compiler_params=_cp())(op, oraw, dz, ga, gb, h0, wn, wbf, wbd, wo)


def _mix_bwd(dh1, op, oraw, dz, ga, gb, wn, wbf, wbd, wo):
    L, D = dh1.shape
    TM = _pick(L, ROW_TILES)
    WF, WD = op.shape[1], oraw.shape[1]

    def body(dh1_ref, op_ref, or_ref, dz_ref, ga_ref, gb_ref, wn_ref, wbf_ref, wbd_ref, wo_ref,
             dop_ref, dor_ref, dg_ref, af_ref, ad_ref, dpf_ref, dpd_ref, y_ref, dmix_ref, acc_ref):
        @pl.when(pl.program_id(0) == 0)
        def _():
            acc_ref[...] = jnp.zeros_like(acc_ref)

        opv = op_ref[...]
        af = _b(opv)
        ad, vjp = jax.vjp(_dn_normgate, or_ref[...], dz_ref[...], wn_ref[...])
        adb = _b(ad)
        pf, pd = _dot(af, wbf_ref[...]), _dot(adb, wbd_ref[...])
        sa, sb = _sigmoid(ga_ref[...]), _sigmoid(gb_ref[...])
        dmix = _b(dh1_ref[...])
        dy = _dot_nt(dmix, wo_ref[...])
        dpf, dpd = _b(dy * sa), _b(dy * sb)
        dor, ddz, dwn = vjp(_dot_nt(dpd, wbd_ref[...]))
        dop = _dot_nt(dpf, wbf_ref[...])
        lane = _iota((TM, HP), 1)
        for h in range(WF // HP):
            c = slice(h * HP, (h + 1) * HP)
            delta = jnp.sum(jnp.where(lane < FOX_D, dop[:, c] * opv[:, c], 0.0), axis=1, keepdims=True)
            dop_ref[:, c] = _b(dop[:, c] - _spread3(jnp.broadcast_to(delta, (TM, HP)), lane, C_DELTA0))
        dor_ref[...] = dor
        dg_ref[:, 0:WD] = _b(ddz)
        dg_ref[:, WD:WD + D] = _b(dy * pf * sa * (1.0 - sa))
        dg_ref[:, WD + D:] = _b(dy * pd * sb * (1.0 - sb))
        af_ref[...], ad_ref[...], y_ref[...] = af, adb, _b(sa * pf + sb * pd)
        dpf_ref[...], dpd_ref[...], dmix_ref[...] = dpf, dpd, dmix
        acc_ref[0:1, :] += dwn

    row = lambda wd: pl.BlockSpec((TM, wd), lambda i: (i, 0))
    full = lambda a: pl.BlockSpec(a.shape, lambda i: (0, 0))
    return pl.pallas_call(
        body, name="mix_bwd", grid=(L // TM,),
        in_specs=[row(D), row(WF), row(WD), row(WD), row(D), row(D), full(wn), full(wbf), full(wbd), full(wo)],
        out_specs=[row(WF), row(WD), row(WD + 2 * D), row(WF), row(WD), row(D), row(D), row(D), row(D),
                   pl.BlockSpec((8, HP), lambda i: (0, 0))],
        out_shape=[SDS((L, WF), bf16), SDS((L, WD), f32), SDS((L, WD + 2 * D), bf16), SDS((L, WF), bf16), SDS((L, WD), bf16),
                   SDS((L, D), bf16), SDS((L, D), bf16), SDS((L, D), bf16), SDS((L, D), bf16), SDS((8, HP), f32)],
        compiler_params=_cp())(dh1, op, oraw, dz, ga, gb, wn, wbf, wbd, wo)


def _ffn_fwd_bwd(h1, tgt, w2, wf, wgt, wut, wd):
    L, D = h1.shape
    F = wd.shape[0]
    TM = _pick(L, FFN_TILES)

    def body(h_ref, t_ref, w2_ref, wf_ref, wg_hbm, wu_hbm, wd_hbm,
             dh1_ref, xn_ref, dg_ref, du_ref, act_ref, dh2_ref, acc_ref, wg_v, wu_v, wd_v, sems):
        i = pl.program_id(0)
        _load_once([(wg_hbm, wg_v), (wu_hbm, wu_v), (wd_hbm, wd_v)], sems)

        @pl.when(i == 0)
        def _():
            acc_ref[...] = jnp.zeros_like(acc_ref)

        h1v = h_ref[...]
        xn2, vjp2 = jax.vjp(_rms, h1v, w2_ref[...])
        xb = _b(xn2)
        g, u = _dot_nt(xb, wg_v[...]), _dot_nt(xb, wu_v[...])
        sg = _sigmoid(g)
        ab = _b(g * sg * u)
        h2 = h1v + _dot(ab, wd_v[...])
        out, vjpf = jax.vjp(_rms, h2, wf_ref[...])
        valid = (i * TM + _iota((TM, 1), 0)) >= PREFIX
        diff = jnp.where(valid, out - t_ref[...], 0.0)
        loss = 0.5 * jnp.sum(jnp.sum(diff * diff, axis=1, keepdims=True), axis=0, keepdims=True) / D
        dh2, dwf = vjpf(diff * (1.0 / D))
        dh2b = _b(dh2)
        dact = _dot_nt(dh2b, wd_v[...])
        dgb = _b(dact * u * (sg * (1.0 + g * (1.0 - sg))))
        dub = _b(dact * (g * sg))
        dh1n, dw2 = vjp2(_dot(dgb, wg_v[...]) + _dot(dub, wu_v[...]))
        dh1_ref[...] = dh2 + dh1n
        xn_ref[...], dg_ref[...], du_ref[...], act_ref[...], dh2_ref[...] = xb, dgb, dub, ab, dh2b
        acc_ref[0:1, :] += dw2
        acc_ref[1:2, :] += dwf
        acc_ref[2:3, :] += jnp.broadcast_to(loss, (1, D))

    row = lambda wd_: pl.BlockSpec((TM, wd_), lambda i: (i, 0))
    vec = pl.BlockSpec((1, D), lambda i: (0, 0))
    anyspec = pl.BlockSpec(memory_space=pl.ANY)
    return pl.pallas_call(
        body, name="ffn_fwd_bwd", grid=(L // TM,),
        in_specs=[row(D), row(D), vec, vec, anyspec, anyspec, anyspec],
        out_specs=[row(D), row(D), row(F), row(F), row(F), row(D), pl.BlockSpec((8, D), lambda i: (0, 0))],
        out_shape=[SDS((L, D), f32), SDS((L, D), bf16), SDS((L, F), bf16), SDS((L, F), bf16), SDS((L, F), bf16), SDS((L, D), bf16),
                   SDS((8, D), f32)],
        scratch_shapes=[pltpu.VMEM((F, D), bf16), pltpu.VMEM((F, D), bf16), pltpu.VMEM((F, D), bf16), pltpu.SemaphoreType.DMA((3,))],
        compiler_params=_cp())(h1, tgt, w2, wf, wgt, wut, wd)


def _pad_lanes(v, n=HP):
    return jnp.pad(v.astype(f32), ((0, 0), (0, n - v.shape[1])))


def _pack_w_in(wt_full):
    D = wt_full.shape[1]
    FW, DW = FOX_H * FOX_D, DN_H * DN_D
    o = 0
    parts = {}
    for name, wd in (("fq", FW), ("fk", FW), ("fv", FW), ("fl", FOX_H), ("dn", 3 * DW), ("ba", 2 * DN_H), ("dz", DW), ("ga", D), ("gb", D)):
        parts[name] = wt_full[o:o + wd]
        o += wd
    assert o == wt_full.shape[0]
    heads = lambda w: jnp.pad(w.reshape(FOX_H, FOX_D, D), ((0, 0), (0, HP - FOX_D), (0, 0))).reshape(FOX_H * HP, D)
    small = lambda w: jnp.pad(w, ((0, SMALL_W - w.shape[0]), (0, 0)))
    packed = dict(fq=heads(parts["fq"]), fk=heads(parts["fk"]), fv=heads(parts["fv"]), sf=small(parts["fl"]), sd=small(parts["ba"]),
                  dn=parts["dn"], dz=parts["dz"], ga=parts["ga"], gb=parts["gb"])
    return jnp.concatenate([packed[name] for name, _, _, _ in _seg_layout(D)], axis=0)


def _unpack_w_in(groups, d_model):
    D = groups[0].shape[1]
    FW = FOX_H * FOX_D
    segs = {}
    for grp, g in zip(GROUPS, groups):
        o = 0
        for name, wd, _, sg in _seg_layout(d_model):
            if sg == grp:
                segs[name] = g[o:o + wd]
                o += wd
    heads = lambda g: g.reshape(FOX_H, HP, D)[:, :FOX_D].reshape(FW, D)
    return jnp.concatenate([heads(segs["fq"]), heads(segs["fk"]), heads(segs["fv"]), segs["sf"][:FOX_H], segs["dn"],
                            segs["sd"][:2 * DN_H], segs["dz"], segs["ga"], segs["gb"]], axis=0)


def _local_step(x, tgt, meta, w1, w_in_t, fbias, cw, alog, dtb, wn, w2, wf, late_shards):
    T, D = x.shape
    h0 = jnp.concatenate([jnp.zeros((N_PAD, D), f32), meta, x], axis=0)
    tgt_p = jnp.concatenate([jnp.zeros((PREFIX, D), f32), tgt], axis=0)
    wp = _pack_w_in(w_in_t)
    bias_p, alog_p, dt_p = _pad_lanes(fbias), _pad_lanes(jnp.pad(alog, ((0, 0), (DN_H, 0)))), _pad_lanes(jnp.pad(dtb, ((0, 0), (DN_H, 0))))

    (xn, fq, fk, sf, fv, dn, sd, dz, ga, gb), g_mix = _in_proj(h0, w1, wp, [late_shards[n] for n in LATE_MIX])
    qa, ka, va = _fox_prep(fq, fk, fv, sf, bias_p)
    op, qb, g_ffn = _fox_fwd(qa, ka, va, [late_shards[n] for n in LATE_FFN])
    full = {n: _from_slabs(n, s) for n, s in zip(LATE_MIX + LATE_FFN, tuple(g_mix) + tuple(g_ffn))}
    wbf, wbd, wo, wgt, wut, wd = (full[n] for n in ("w_branch_fox", "w_branch_dn", "w_out", "w_ffn_gate", "w_ffn_up", "w_ffn_down"))
    wbf_p = jnp.pad(wbf.reshape(FOX_H, FOX_D, D), ((0, 0), (0, HP - FOX_D), (0, 0))).reshape(FOX_H * HP, D)
    qn, kn, vn, bg = _dn_prep(dn, sd, cw, alog_p, dt_p)
    u_dn, w_dn, qd_dn, kd_dn, at_dn, gl_dn, x_dn = _dn_intra(qn, kn, vn, bg)
    oraw, vnew, states = _dn_scan(u_dn, w_dn, qd_dn, kd_dn, at_dn, gl_dn)
    h1 = _mix_fwd(op, oraw, dz, ga, gb, h0, wn, wbf_p, wbd, wo)

    dh1, xn2, dgate, dup, act, dh2, acc_f = _ffn_fwd_bwd(h1, tgt_p, w2, wf, wgt, wut, wd)
    g_wg, g_wu, g_wd = _matmul_tn(dgate, xn2, "dw_ffn_gate"), _matmul_tn(dup, xn2, "dw_ffn_up"), _matmul_tn(act, dh2, "dw_ffn_down")

    dop, dor, d_mix, af, ad, dpf, dpd, yb, dmix, acc_m = _mix_bwd(dh1, op, oraw, dz, ga, gb, wn, wbf_p, wbd, wo)
    g_wbf = _matmul_tn(af, dpf, "dw_branch_fox").reshape(FOX_H, HP, D)[:, :FOX_D].reshape(FOX_H * FOX_D, D)
    g_wbd, g_wo = _matmul_tn(ad, dpd, "dw_branch_dn"), _matmul_tn(yb, dmix, "dw_out")

    g_late = dict(w_branch_fox=g_wbf, w_branch_dn=g_wbd, w_out=g_wo, w_ffn_gate=g_wg, w_ffn_up=g_wu, w_ffn_down=g_wd)
    (dqa, dka, d_fv), recv = _fox_bwd(qb, ka, va, dop, [_to_slabs(n, g_late[n]) for n in LATE])
    d_fox, acc_b = _fox_prep_bwd(dqa, dka, sf, bias_p)

    g_wp = {grp: _matmul_tn(dg, xn, "dw_in_" + grp) for grp, dg in (("fox", d_fox), ("fv", d_fv), ("mix", d_mix))}
    slab_rows = w_in_t.shape[0] // N_DEV
    dn_lo = 3 * FOX_H * FOX_D + FOX_H
    dn_hi = dn_lo + 3 * DN_H * DN_D + 2 * DN_H
    early = tuple(p for p in range(N_DEV) if (p + 1) * slab_rows <= dn_lo or p * slab_rows >= dn_hi)
    late = tuple(p for p in range(N_DEV) if p not in early)
    assert early and late
    g_wp["dn"] = jnp.zeros((_group_widths(D)[GROUPS.index("dn")], D), bf16)
    early_slabs = _to_slabs("w_in", _unpack_w_in([g_wp[grp] for grp in GROUPS], D))

    dvnew, dstates = _dn_scan_bwd(dor, w_dn, qd_dn, kd_dn, at_dn, gl_dn)
    (dqn, dkn, dvn, dbg), (recv_early,) = _dn_intra_bwd(qn, kn, vn, bg, x_dn, dor, vnew, dvnew, states, dstates, [early_slabs], early)
    d_dn, acc_cw, acc_p = _dn_prep_bwd(dn, sd, cw, alog_p, dt_p, dqn, dkn, dvn, dbg)
    g_wp["dn"] = _matmul_tn(d_dn, xn, "dw_in_dn")
    late_slabs = _to_slabs("w_in", _unpack_w_in([g_wp[grp] for grp in GROUPS], D))
    dh0, acc_1, (recv_late,) = _in_proj_bwd([d_fox, d_fv, d_dn, d_mix], wp, h0, w1, dh1, [late_slabs], late)
    me = _mesh_pos()[3]
    is_early = functools.reduce(jnp.logical_or, [me == p for p in early])
    recv = dict(zip(LATE, recv), w_in=jnp.where(is_early, recv_early, recv_late))

    small = dict(loss=acc_f[2, 0:1], mix_norm_w=acc_1[0], fox_forget_bias=acc_b[0, :FOX_H], dn_a_log=acc_p[0, DN_H:2 * DN_H],
                 dn_dt_bias=acc_p[1, DN_H:2 * DN_H], dn_out_norm_w=acc_m[0], ffn_norm_w=acc_f[0], final_norm_w=acc_f[1],
                 meta_tokens=dh0[N_PAD:PREFIX].reshape(-1), dn_conv_w=acc_cw[:CONV_K].reshape(-1))
    return dh0[PREFIX:], small, recv


def _mesh_pos():
    x, y, c = lax.axis_index("x"), lax.axis_index("y"), lax.axis_index("c")
    return x, y, c, 4 * x + 2 * y + c


def _peer(x, y, c, m):
    flip = lambda v, on: 1 - v if on else v
    px, py, pc = flip(x, m & 4), flip(y, m & 2), flip(c, m & 1)
    return (px, py, pc), 4 * px + 2 * py + pc


def _exchange_sems(n):
    return [pltpu.SemaphoreType.DMA((n, N_DEV - 1)), pltpu.SemaphoreType.DMA((n, N_DEV - 1)), pltpu.SemaphoreType.DMA((n,))]


def _exchange_part(ins, outs, send_sems, recv_sems, loc_sems, gather, m, receive):
    x, y, c, me = _mesh_pos()
    src = lambda a, pid: ins[a] if gather else ins[a].at[pid]
    if m == 0:
        return [pltpu.make_async_copy(src(a, me), outs[a].at[me], loc_sems.at[a]) for a in range(len(ins))]
    peer, pid = _peer(x, y, c, m)
    return [pltpu.make_async_remote_copy(src_ref=src(a, pid), dst_ref=outs[a].at[pid if receive else me], send_sem=send_sems.at[a, m - 1],
                                         recv_sem=recv_sems.at[a, m - 1], device_id=peer, device_id_type=MESH) for a in range(len(ins))]


def _when_target(when, m, targets, fn):
    if targets is not None:
        x, y, c, _ = _mesh_pos()
        dev = _peer(x, y, c, m)[1]
        hit = dev == targets[0]
        for t in targets[1:]:
            hit = hit | (dev == t)
        when = when & hit
    pl.when(when)(fn)


def _exchange_start(*refs, gather, when, targets=None):
    for m in range(N_DEV):
        def start(m=m):
            for cp in _exchange_part(*refs, gather, m, receive=False):
                cp.start()
        _when_target(when, m, targets, start)


def _exchange_wait(*refs, gather, when, targets=None):
    def arrivals():
        for m in range(1, N_DEV):
            for cp in _exchange_part(*refs, gather, m, receive=True):
                cp.wait_recv()
    _when_target(when, 0, targets, arrivals)
    for m in list(range(1, N_DEV)) + [0]:
        def drained(m=m):
            for cp in _exchange_part(*refs, gather, m, receive=False):
                cp.wait() if m == 0 else cp.wait_send()
        _when_target(when, m, targets, drained)


def _gather_two_level(arrays, name):
    n = len(arrays)

    def body(*refs):
        ins, outs, (send_sems, recv_sems, loc_sems) = refs[:n], refs[n:2 * n], refs[2 * n:]
        x, y, c, me = _mesh_pos()
        sib = (x, y, 1 - c)
        chips = [(1 - x, y), (x, 1 - y), (1 - x, 1 - y)]
        dev_id = lambda px, py, pc: 4 * px + 2 * py + pc

        def copy(a, k, block, to, own=False):
            return pltpu.make_async_remote_copy(src_ref=ins[a] if own else outs[a].at[block], dst_ref=outs[a].at[block],
                                                send_sem=send_sems.at[a, k], recv_sem=recv_sems.at[a, k], device_id=to, device_id_type=MESH)

        local = [pltpu.make_async_copy(ins[a], outs[a].at[me], loc_sems.at[a]) for a in range(n)]
        first = [copy(a, 0, me, sib, own=True) for a in range(n)]
        first += [copy(a, 1 + j, me, (*chip, c), own=True) for j, chip in enumerate(chips) for a in range(n)]
        for cp in local + first:
            cp.start()
        passed = []
        for j, chip in enumerate(chips):
            for a in range(n):
                copy(a, 1 + j, dev_id(*chip, c), sib).wait_recv()
                cp = copy(a, 4 + j, dev_id(*chip, c), sib)
                cp.start()
                passed.append(cp)
        for a in range(n):
            copy(a, 0, dev_id(x, y, 1 - c), sib).wait_recv()
        for j, chip in enumerate(chips):
            for a in range(n):
                copy(a, 4 + j, dev_id(*chip, 1 - c), sib).wait_recv()
        for cp in first + passed:
            cp.wait_send()
        for cp in local:
            cp.wait()

    anyspec = pl.BlockSpec(memory_space=pl.ANY)
    return pl.pallas_call(
        body, name=name, in_specs=[anyspec] * n, out_specs=[anyspec] * n,
        out_shape=[SDS((N_DEV,) + a.shape, a.dtype) for a in arrays],
        scratch_shapes=_exchange_sems(n))(*arrays)


def _all_reduce_small(v):
    R = v.shape[0]

    def body(v_ref, o_ref, gath, send_sems, recv_sems):
        x, y, c, me = _mesh_pos()
        gath[me] = v_ref[...]
        sends = []
        for m in range(1, N_DEV):
            peer, _ = _peer(x, y, c, m)
            cp = pltpu.make_async_remote_copy(src_ref=v_ref, dst_ref=gath.at[me], send_sem=send_sems.at[m - 1],
                                              recv_sem=recv_sems.at[m - 1], device_id=peer, device_id_type=MESH)
            cp.start()
            sends.append(cp)
        for m in range(1, N_DEV):
            peer, pid = _peer(x, y, c, m)
            pltpu.make_async_remote_copy(src_ref=v_ref, dst_ref=gath.at[pid], send_sem=send_sems.at[m - 1],
                                         recv_sem=recv_sems.at[m - 1], device_id=peer, device_id_type=MESH).wait_recv()
        for cp in sends:
            cp.wait_send()
        tot = gath[0]
        for d in range(1, N_DEV):
            tot = tot + gath[d]
        o_ref[...] = tot

    vm = pl.BlockSpec(memory_space=pltpu.VMEM)
    return pl.pallas_call(
        body, name="all_reduce_small", in_specs=[vm], out_specs=vm, out_shape=SDS((R, HP), f32),
        scratch_shapes=[pltpu.VMEM((N_DEV, R, HP), f32), pltpu.SemaphoreType.DMA((N_DEV - 1,)), pltpu.SemaphoreType.DMA((N_DEV - 1,))],
        )(v)


def _adamw_math(w, g, m, v):
    m = ADAM_B1 * m + (1.0 - ADAM_B1) * g
    v = ADAM_B2 * v + (1.0 - ADAM_B2) * (g * g)
    m_hat = m / (1.0 - ADAM_B1 ** ADAM_STEP)
    v_hat = v / (1.0 - ADAM_B2 ** ADAM_STEP)
    return -ADAM_LR * (m_hat / (jnp.sqrt(v_hat) + ADAM_EPS) + ADAM_WD * w), m, v


def _adamw(g, w, m, v, name):
    R, Cc = w.shape[-2:]
    if R <= 512 or R % 128 == 0:
        TR, TC = (R if R <= 512 else _pick(R, (256, 128))), Cc
    else:
        TR, TC = R, _pick(Cc, (256, 128))
    slabs = g.ndim == 3
    lead = w.ndim - 2

    def body(g_ref, w_ref, m_ref, v_ref, go_ref, d_ref, mo_ref, vo_ref):
        if slabs:
            gs = g_ref[0].astype(f32)
            for k in range(1, N_DEV):
                gs = gs + g_ref[k].astype(f32)
        else:
            gs = g_ref[...]
        at = 0 if lead else Ellipsis
        d, mn, vn = _adamw_math(w_ref[at], gs, m_ref[at], v_ref[at])
        go_ref[at], d_ref[at], mo_ref[at], vo_ref[at] = gs, d, mn, vn

    grid = (R // TR, Cc // TC)
    blk = pl.BlockSpec((1,) * lead + (TR, TC), lambda i, j: (0,) * lead + (i, j))
    gblk = pl.BlockSpec((N_DEV, TR, TC), lambda i, j: (0, i, j)) if slabs else pl.BlockSpec((TR, TC), lambda i, j: (i, j))
    return pl.pallas_call(
        body, name=name, grid=grid, in_specs=[gblk, blk, blk, blk], out_specs=[blk] * 4,
        out_shape=[SDS(w.shape, f32)] * 4, compiler_params=_cp(2))(g, w, m, v)


WEIGHTS = ("meta_tokens", "mix_norm_w", "w_in", "fox_forget_bias", "dn_conv_w", "dn_a_log", "dn_dt_bias", "dn_out_norm_w",
           "w_branch_fox", "w_branch_dn", "w_out", "ffn_norm_w", "w_ffn_gate", "w_ffn_up", "w_ffn_down", "final_norm_w")
COL_SHARDED = ("w_in", "w_branch_fox", "w_branch_dn", "w_ffn_gate", "w_ffn_up")
ROW_SHARDED = ("w_out", "w_ffn_down")
BIG = COL_SHARDED + ROW_SHARDED
LATE = tuple(n for n in BIG if n != "w_in")
LATE_MIX = ("w_branch_fox", "w_branch_dn", "w_out")
LATE_FFN = ("w_ffn_gate", "w_ffn_up", "w_ffn_down")
SMALL = tuple(n for n in WEIGHTS if n not in BIG)
TRANSPOSED = ("w_in", "w_ffn_gate", "w_ffn_up")


def _to_slabs(name, g):
    r, c = g.shape
    if name in COL_SHARDED and name not in TRANSPOSED:
        return _b(g.reshape(r, N_DEV, c // N_DEV).transpose(1, 0, 2))
    return _b(g.reshape(N_DEV, r // N_DEV, c))


def _from_slabs(name, s):
    n, r, c = s.shape
    if name in COL_SHARDED and name not in TRANSPOSED:
        return s.transpose(1, 0, 2).reshape(r, n * c)
    return s.reshape(n * r, c)


def kernel(x, meta_tokens, mix_norm_w, w_in, fox_forget_bias, dn_conv_w, dn_a_log, dn_dt_bias, dn_out_norm_w, w_branch_fox, w_branch_dn, w_out, ffn_norm_w, w_ffn_gate, w_ffn_up, w_ffn_down, final_norm_w, loss_target, m_meta_tokens, m_mix_norm_w, m_w_in, m_fox_forget_bias, m_dn_conv_w, m_dn_a_log, m_dn_dt_bias, m_dn_out_norm_w, m_w_branch_fox, m_w_branch_dn, m_w_out, m_ffn_norm_w, m_w_ffn_gate, m_w_ffn_up, m_w_ffn_down, m_final_norm_w, v_meta_tokens, v_mix_norm_w, v_w_in, v_fox_forget_bias, v_dn_conv_w, v_dn_a_log, v_dn_dt_bias, v_dn_out_norm_w, v_w_branch_fox, v_w_branch_dn, v_w_out, v_ffn_norm_w, v_w_ffn_gate, v_w_ffn_up, v_w_ffn_down, v_final_norm_w):
    w = dict(meta_tokens=meta_tokens, mix_norm_w=mix_norm_w, w_in=w_in, fox_forget_bias=fox_forget_bias, dn_conv_w=dn_conv_w, dn_a_log=dn_a_log, dn_dt_bias=dn_dt_bias, dn_out_norm_w=dn_out_norm_w, w_branch_fox=w_branch_fox, w_branch_dn=w_branch_dn, w_out=w_out, ffn_norm_w=ffn_norm_w, w_ffn_gate=w_ffn_gate, w_ffn_up=w_ffn_up, w_ffn_down=w_ffn_down, final_norm_w=final_norm_w)
    mom = dict(meta_tokens=m_meta_tokens, mix_norm_w=m_mix_norm_w, w_in=m_w_in, fox_forget_bias=m_fox_forget_bias, dn_conv_w=m_dn_conv_w, dn_a_log=m_dn_a_log, dn_dt_bias=m_dn_dt_bias, dn_out_norm_w=m_dn_out_norm_w, w_branch_fox=m_w_branch_fox, w_branch_dn=m_w_branch_dn, w_out=m_w_out, ffn_norm_w=m_ffn_norm_w, w_ffn_gate=m_w_ffn_gate, w_ffn_up=m_w_ffn_up, w_ffn_down=m_w_ffn_down, final_norm_w=m_final_norm_w)
    var = dict(meta_tokens=v_meta_tokens, mix_norm_w=v_mix_norm_w, w_in=v_w_in, fox_forget_bias=v_fox_forget_bias, dn_conv_w=v_dn_conv_w, dn_a_log=v_dn_a_log, dn_dt_bias=v_dn_dt_bias, dn_out_norm_w=v_dn_out_norm_w, w_branch_fox=v_w_branch_fox, w_branch_dn=v_w_branch_dn, w_out=v_w_out, ffn_norm_w=v_ffn_norm_w, w_ffn_gate=v_w_ffn_gate, w_ffn_up=v_w_ffn_up, w_ffn_down=v_w_ffn_down, final_norm_w=v_final_norm_w)
    two_d = lambda a: a.reshape(a.shape[-2:]) if a.ndim >= 2 else a.reshape(1, -1)
    me = 4 * lax.axis_index("x") + 2 * lax.axis_index("y") + lax.axis_index("c")
    for d in (w, mom, var):
        for n in TRANSPOSED:
            d[n] = jnp.swapaxes(d[n], -1, -2)

    g_in, g_meta, g_cw = _gather_two_level([_b(two_d(w["w_in"])), two_d(w["meta_tokens"]), two_d(w["dn_conv_w"])], "all_gather_early")
    meta = g_meta.transpose(1, 0, 2).reshape(N_META, -1)
    cw = g_cw.transpose(1, 0, 2).reshape(CONV_K, -1)

    gx, g_small, recv = _local_step(
        x[0], loss_target[0], meta, two_d(w["mix_norm_w"]), _from_slabs("w_in", g_in), two_d(w["fox_forget_bias"]), cw, two_d(w["dn_a_log"]),
        two_d(w["dn_dt_bias"]), two_d(w["dn_out_norm_w"]), two_d(w["ffn_norm_w"]), two_d(w["final_norm_w"]),
        {n: _b(two_d(w[n])) for n in LATE})

    order = ("loss",) + SMALL
    flat = jnp.concatenate([g_small[n].reshape(-1) for n in order])
    rows = -(-flat.shape[0] // (8 * HP)) * 8
    tot = _all_reduce_small(jnp.pad(flat, (0, rows * HP - flat.shape[0])).reshape(rows, HP)).reshape(-1)
    summed, o = {}, 0
    for n in order:
        k = g_small[n].shape[0]
        summed[n] = tot[o:o + k]
        o += k
    loss = summed["loss"][0]
    d_model = x.shape[-1]
    mcols, ccols = d_model // N_DEV, dn_conv_w.shape[-1]
    summed["meta_tokens"] = lax.dynamic_slice(summed["meta_tokens"].reshape(N_META, d_model), (0, me * mcols), (N_META, mcols)).reshape(-1)
    summed["dn_conv_w"] = lax.dynamic_slice(summed["dn_conv_w"].reshape(CONV_K, ccols * N_DEV), (0, me * ccols), (CONV_K, ccols)).reshape(-1)

    res = {}
    for n in BIG:
        res[n] = _adamw(recv[n], w[n], mom[n], var[n], "adamw_" + n)
        if n in TRANSPOSED:
            res[n] = [jnp.swapaxes(r, -1, -2) for r in res[n]]
    sizes = [summed[n].shape[0] for n in SMALL]
    srows = -(-sum(sizes) // (8 * HP)) * 8
    pack = lambda d: jnp.pad(jnp.concatenate([d[n].reshape(-1) for n in SMALL]), (0, srows * HP - sum(sizes))).reshape(srows, HP)
    sres = _adamw(pack(summed), pack(w), pack(mom), pack(var), "adamw_small")
    o = 0
    for n, k in zip(SMALL, sizes):
        res[n] = [r.reshape(-1)[o:o + k].reshape(w[n].shape) for r in sres]
        o += k
    return (loss, gx[None], *[res[n][0] for n in WEIGHTS], *[res[n][1] for n in WEIGHTS], *[res[n][2] for n in WEIGHTS], *[res[n][3] for n in WEIGHTS])
```

```python
import functools

import jax
import jax.numpy as jnp
from jax import lax
from jax.experimental import pallas as pl
from jax.experimental.pallas import tpu as pltpu

f32, bf16 = jnp.float32, jnp.bfloat16
HI = lax.Precision.HIGHEST
MESH = pl.DeviceIdType.MESH
SDS = jax.ShapeDtypeStruct

N_DEV = 8
N_META = 16
PREFIX = 128
N_PAD = PREFIX - N_META
FOX_H, FOX_D = 8, 64
DN_H, DN_D = 4, 128
DN_C = 64
CONV_K = 4
HP = 128
SMALL_W = 256
EPS = 1e-6
NEG = -1e30
C_Q0, C_K0 = 64, 67
LSE_COL = 64
LOG2E, LN2 = 1.4426950408889634, 0.6931471805599453
C_LSE0, C_DELTA0 = 70, 65

ADAM_LR, ADAM_B1, ADAM_B2, ADAM_EPS, ADAM_WD, ADAM_STEP = 0.001, 0.9, 0.999, 1e-08, 0.01, 10

VMEM_LIMIT_V7X = 56 * 1024 * 1024
ROW_TILES = (384, 128)
ATTN_TILES = (384, 128)
FFN_TILES = (192, 64)
FOX_HEAD_GROUP = 4
FOX_HEAD_GROUP_FWD = 8
MAX_WGRAD_BLOCK = 1408
DN_INTRA_GROUP = (6, 3, 2, 1)
DN_SCAN_GROUP = (6, 3, 2, 1)


def _pick(n, cands):
    for c in cands:
        if n % c == 0:
            return c
    raise ValueError(f"no tile of {cands} divides {n}")


def _cp(n_axes=1):
    return pltpu.CompilerParams(dimension_semantics=("arbitrary",) * n_axes, vmem_limit_bytes=VMEM_LIMIT_V7X)


def _b(x):
    return x.astype(bf16)


def _dot(a, b):
    return jnp.dot(a, b, preferred_element_type=f32)


def _dot_nt(a, b):
    return lax.dot_general(a, b, (((1,), (1,)), ((), ())), preferred_element_type=f32)


def _dot_tn(a, b):
    return lax.dot_general(a, b, (((0,), (0,)), ((), ())), preferred_element_type=f32)


def _dot_hi(a, b):
    return jnp.dot(a, b, preferred_element_type=f32, precision=HI)


def _iota(shape, dim):
    return lax.broadcasted_iota(jnp.int32, shape, dim)


def _rms(x, w):
    return x * lax.rsqrt(jnp.mean(x * x, axis=-1, keepdims=True) + EPS) * w


def _sigmoid(x):
    return jax.nn.sigmoid(x)


def _load_once(pairs, sems):
    @pl.when(pl.program_id(0) == 0)
    def _():
        cps = [pltpu.make_async_copy(src, dst, sems.at[k]) for k, (src, dst) in enumerate(pairs)]
        for cp in cps:
            cp.start()
        for cp in cps:
            cp.wait()


def _seg_layout(d_model):
    return (("fq", FOX_H * HP, bf16, "fox"), ("fk", FOX_H * HP, bf16, "fox"), ("sf", SMALL_W, f32, "fox"),
            ("fv", FOX_H * HP, bf16, "fv"),
            ("dn", 3 * DN_H * DN_D, f32, "dn"), ("sd", SMALL_W, f32, "dn"),
            ("dz", DN_H * DN_D, f32, "mix"), ("ga", d_model, f32, "mix"), ("gb", d_model, f32, "mix"))


GROUPS = ("fox", "fv", "dn", "mix")


def _group_widths(d_model):
    return [sum(wd for _, wd, _, g in _seg_layout(d_model) if g == grp) for grp in GROUPS]


def _in_proj(h0, w1, wpt, shards):
    L, D = h0.shape
    NP = wpt.shape[0]
    TM = _pick(L, ROW_TILES)
    NT = L // TM
    segs = _seg_layout(D)
    ns, n = len(segs), len(shards)
    offs, o = [], 0
    for _, wd, _, _ in segs:
        offs.append(o)
        o += wd
    assert o == NP

    def body(h_ref, w1_ref, wp_hbm, *rest):
        ins, xn_ref, outs, gouts = rest[:n], rest[n], rest[n + 1:n + 1 + ns], rest[n + 1 + ns:2 * n + 1 + ns]
        wp_v, sems = rest[2 * n + 1 + ns:2 * n + 3 + ns]
        xsems = rest[2 * n + 3 + ns:]
        _load_once([(wp_hbm, wp_v)], sems)

        _exchange_start(ins, gouts, *xsems, gather=True, when=pl.program_id(0) == 0)

        xn = _b(_rms(h_ref[...], w1_ref[...]))
        xn_ref[...] = xn
        for o_ref, off, (_, wd, _, _) in zip(outs, offs, segs):
            o_ref[...] = _dot_nt(xn, wp_v[off:off + wd, :]).astype(o_ref.dtype)

        _exchange_wait(ins, gouts, *xsems, gather=True, when=pl.program_id(0) == NT - 1)

    row = lambda wd: pl.BlockSpec((TM, wd), lambda i: (i, 0))
    anyspec = pl.BlockSpec(memory_space=pl.ANY)
    res = pl.pallas_call(
        body, name="in_proj", grid=(NT,),
        in_specs=[row(D), pl.BlockSpec((1, D), lambda i: (0, 0)), anyspec] + [anyspec] * n,
        out_specs=[row(D)] + [row(wd) for _, wd, _, _ in segs] + [anyspec] * n,
        out_shape=[SDS((L, D), bf16)] + [SDS((L, wd), dt) for _, wd, dt, _ in segs] + [SDS((N_DEV,) + a.shape, a.dtype) for a in shards],
        scratch_shapes=[pltpu.VMEM((NP, D), bf16), pltpu.SemaphoreType.DMA((1,))] + _exchange_sems(n),
        compiler_params=_cp())(h0, w1, wpt, *shards)
    return res[:1 + ns], res[1 + ns:]


def _in_proj_bwd(dgroups, wpt, h0, w1, dh1, slabs):
    L, D = h0.shape
    NP = wpt.shape[0]
    TM = _pick(L, ROW_TILES)
    NT = L // TM
    widths = [g.shape[1] for g in dgroups]
    assert sum(widths) == NP
    ng, n = len(dgroups), len(slabs)

    def body(*refs):
        dg_refs, (wp_hbm, h_ref, w1_ref, dh1_ref) = refs[:ng], refs[ng:ng + 4]
        ins, (dh0_ref, acc_ref), outs = refs[ng + 4:ng + 4 + n], refs[ng + 4 + n:ng + 6 + n], refs[ng + 6 + n:ng + 6 + 2 * n]
        wp_v, sems = refs[ng + 6 + 2 * n:ng + 8 + 2 * n]
        xsems = refs[ng + 8 + 2 * n:]
        _load_once([(wp_hbm, wp_v)], sems)

        @pl.when(pl.program_id(0) == 0)
        def _():
            acc_ref[...] = jnp.zeros_like(acc_ref)

        _exchange_start(ins, outs, *xsems, gather=False, when=pl.program_id(0) == 0)

        dxn, off = None, 0
        for g_ref, wd in zip(dg_refs, widths):
            part = _dot(g_ref[...], wp_v[off:off + wd, :])
            dxn = part if dxn is None else dxn + part
            off += wd
        _, vjp = jax.vjp(_rms, h_ref[...], w1_ref[...])
        dh0n, dw1 = vjp(dxn)
        dh0_ref[...] = dh1_ref[...] + dh0n
        acc_ref[0:1, :] += dw1

        _exchange_wait(ins, outs, *xsems, gather=False, when=pl.program_id(0) == NT - 1)

    row = lambda wd: pl.BlockSpec((TM, wd), lambda i: (i, 0))
    anyspec = pl.BlockSpec(memory_space=pl.ANY)
    res = pl.pallas_call(
        body, name="in_proj_bwd", grid=(NT,),
        in_specs=[row(wd) for wd in widths] + [anyspec, row(D), pl.BlockSpec((1, D), lambda i: (0, 0)), row(D)] + [anyspec] * n,
        out_specs=[row(D), pl.BlockSpec((8, D), lambda i: (0, 0))] + [anyspec] * n,
        out_shape=[SDS((L, D), f32), SDS((8, D), f32)] + [SDS(a.shape, a.dtype) for a in slabs],
        scratch_shapes=[pltpu.VMEM((NP, D), bf16), pltpu.SemaphoreType.DMA((1,))] + _exchange_sems(n),
        compiler_params=_cp())(*dgroups, wpt, h0, w1, dh1, *slabs)
    return res[0], res[1], res[2:]


def _matmul_tn(a, b, name):
    L, R = a.shape
    C = b.shape[1]
    br = max(k for k in range(HP, MAX_WGRAD_BLOCK + 1, HP) if R % k == 0)

    def body(a_ref, b_ref, o_ref):
        o_ref[...] = _b(_dot_tn(a_ref[...], b_ref[...]))

    return pl.pallas_call(
        body, name=name, grid=(R // br,),
        in_specs=[pl.BlockSpec((L, br), lambda r: (0, r)), pl.BlockSpec((L, C), lambda r: (0, 0))],
        out_specs=pl.BlockSpec((br, C), lambda r: (r, 0)), out_shape=SDS((R, C), bf16), compiler_params=_cp())(a, b)


def _fox_prep(fq, fk, fv, sf, bias_p):
    L = fq.shape[0]
    T = HP
    NT = L // T
    W = FOX_H * HP

    def body(fq_ref, fk_ref, fv_ref, sf_ref, b_ref, qa_ref, ka_ref, va_ref, carry):
        @pl.when(pl.program_id(0) == 0)
        def _():
            carry[...] = jnp.zeros_like(carry)

        lane, row = _iota((T, HP), 1), _iota((T, HP), 0)
        logf = jnp.where(lane < FOX_H, jax.nn.log_sigmoid(sf_ref[...] + b_ref[...]), 0.0)
        c = _dot_hi((row >= lane).astype(f32), logf) + carry[...]
        carry[...] = jnp.sum(jnp.where(row == T - 1, c, 0.0), axis=0, keepdims=True)
        ones_q = jnp.where((lane >= C_K0) & (lane < C_K0 + 3), 1.0, 0.0)
        ones_k = jnp.where(((lane >= C_Q0) & (lane < C_Q0 + 3)) | ((lane >= C_LSE0) & (lane < C_LSE0 + 3)), 1.0, 0.0)
        ones_v = _b(jnp.where((lane >= LSE_COL) & (lane < C_DELTA0 + 3), 1.0, 0.0))
        for h in range(FOX_H):
            ch = jnp.broadcast_to(jnp.sum(jnp.where(lane == h, c, 0.0), axis=1, keepdims=True), (T, HP)) * LOG2E
            c1 = _b(ch).astype(f32)
            c2 = _b(ch - c1).astype(f32)
            c3 = _b(ch - c1 - c2).astype(f32)
            cq = jnp.where(lane == C_Q0, c1, 0.0) + jnp.where(lane == C_Q0 + 1, c2, 0.0) + jnp.where(lane == C_Q0 + 2, c3, 0.0)
            ck = jnp.where(lane == C_K0, c1, 0.0) + jnp.where(lane == C_K0 + 1, c2, 0.0) + jnp.where(lane == C_K0 + 2, c3, 0.0)
            q = fq_ref[:, h * HP:(h + 1) * HP].astype(f32) * (FOX_D ** -0.5 * LOG2E)
            k = fk_ref[:, h * HP:(h + 1) * HP].astype(f32)
            qa_ref[h] = _b(q + cq + ones_q)
            ka_ref[h] = _b(k + ones_k - ck)
            va_ref[:, h * HP:(h + 1) * HP] = fv_ref[:, h * HP:(h + 1) * HP] + ones_v

    wide = pl.BlockSpec((T, W), lambda i: (i, 0))
    return pl.pallas_call(
        body, name="fox_prep", grid=(NT,),
        in_specs=[wide, wide, wide, pl.BlockSpec((T, HP), lambda i: (i, 0)), pl.BlockSpec((1, HP), lambda i: (0, 0))],
        out_specs=[pl.BlockSpec((FOX_H, T, HP), lambda i: (0, i, 0))] * 2 + [wide],
        out_shape=[SDS((FOX_H, L, HP), bf16)] * 2 + [SDS((L, W), bf16)],
        scratch_shapes=[pltpu.VMEM((1, HP), f32)], compiler_params=_cp())(fq, fk, fv, sf, bias_p)


def _fox_prep_bwd(dqa, dka, sf, bias_p):
    L = sf.shape[0]
    T = HP
    NT = L // T
    rev = lambda i: (NT - 1 - i, 0)

    W = FOX_H * HP

    def body(dq_ref, dk_ref, sf_ref, b_ref, dg_ref, db_ref, carry):
        @pl.when(pl.program_id(0) == 0)
        def _():
            carry[...] = jnp.zeros_like(carry)
            db_ref[...] = jnp.zeros_like(db_ref)

        dq, dk = dq_ref[...], dk_ref[...]
        dg_ref[:, 0:W] = _b(dq * (FOX_D ** -0.5))
        dg_ref[:, W:2 * W] = _b(dk * LN2)
        lane, row = _iota((T, HP), 1), _iota((T, HP), 0)
        dc = jnp.zeros((T, HP), f32)
        for h in range(FOX_H):
            col = jnp.sum(jnp.where(lane == C_Q0, dq[:, h * HP:(h + 1) * HP], 0.0)
                          - jnp.where(lane == C_K0, dk[:, h * HP:(h + 1) * HP], 0.0), axis=1, keepdims=True)
            dc = dc + jnp.where(lane == h, col, 0.0)
        dl = _dot_hi((row <= lane).astype(f32), dc) + carry[...]
        carry[...] = jnp.sum(jnp.where(row == 0, dl, 0.0), axis=0, keepdims=True)
        dx = jnp.where(lane < FOX_H, dl * _sigmoid(-(sf_ref[...] + b_ref[...])), 0.0)
        dg_ref[:, 2 * W:2 * W + HP] = _b(dx)
        dg_ref[:, 2 * W + HP:] = jnp.zeros((T, SMALL_W - HP), bf16)
        db_ref[0:1, :] += jnp.sum(dx, axis=0, keepdims=True)

    return pl.pallas_call(
        body, name="fox_prep_bwd", grid=(NT,),
        in_specs=[pl.BlockSpec((T, W), rev), pl.BlockSpec((T, W), rev), pl.BlockSpec((T, HP), rev), pl.BlockSpec((1, HP), lambda i: (0, 0))],
        out_specs=[pl.BlockSpec((T, 2 * W + SMALL_W), rev), pl.BlockSpec((8, HP), lambda i: (0, 0))],
        out_shape=[SDS((L, 2 * W + SMALL_W), bf16), SDS((8, HP), f32)],
        scratch_shapes=[pltpu.VMEM((1, HP), f32)], compiler_params=_cp())(dqa, dka, sf, bias_p)


def _tile_start(j, T):
    return j * T if isinstance(j, int) else pl.multiple_of(j * T, T)


def _spread3(x, lane, col0):
    x1 = _b(x).astype(f32)
    x2 = _b(x - x1).astype(f32)
    x3 = _b(x - x1 - x2).astype(f32)
    return jnp.where(lane == col0, x1, 0.0) + jnp.where(lane == col0 + 1, x2, 0.0) + jnp.where(lane == col0 + 2, x3, 0.0)


def _fox_fwd(qa, ka, fv, shards):
    L = qa.shape[1]
    TQ = TK = _pick(L, ATTN_TILES)
    NQ = L // TQ
    n = len(shards)
    HG = FOX_HEAD_GROUP_FWD

    def body(q_ref, k_ref, v_ref, *rest):
        ins, o_ref, qb_ref, outs, sems = rest[:n], rest[n], rest[n + 1], rest[n + 2:2 * n + 2], rest[2 * n + 2:]
        h, i = pl.program_id(0), pl.program_id(1)

        _exchange_start(ins, outs, *sems, gather=True, when=(h == 0) & (i == 0))

        qs = [q_ref[a] for a in range(HG)]
        rowg = i * TQ + _iota((TQ, TK), 0)
        colb = _iota((TQ, TK), 1)

        def step(j, carry, masked):
            ms, accs = carry
            k0 = _tile_start(j, TK)
            ss = [_dot_nt(qs[a], k_ref[a, pl.ds(k0, TK), :]) for a in range(HG)]
            if masked:
                colg = colb + j * TK
                keep = (colg <= rowg) & (colg >= N_PAD)
                ss = [jnp.where(keep, s, NEG) for s in ss]
            m_new = [jnp.maximum(m, jnp.max(s, axis=1, keepdims=True)) for m, s in zip(ms, ss)]
            ps = [_b(jnp.exp2(s - m)) for s, m in zip(ss, m_new)]
            alphas = [jnp.exp2(m - mn) for m, mn in zip(ms, m_new)]
            accs = [al * acc + _dot(p, v_ref[pl.ds(k0, TK), a * HP:(a + 1) * HP]) for a, (al, acc, p) in enumerate(zip(alphas, accs, ps))]
            return m_new, accs

        init = ([jnp.full((TQ, 1), NEG, f32)] * HG, [jnp.zeros((TQ, HP), f32)] * HG)
        carry = step(0, init, True)
        carry = lax.fori_loop(1, i, functools.partial(step, masked=False), carry)
        ms, accs = lax.fori_loop(jnp.maximum(i, 1), i + 1, functools.partial(step, masked=True), carry)
        lane = _iota((TQ, HP), 1)
        for a in range(HG):
            l = jnp.sum(jnp.where(lane == LSE_COL, accs[a], 0.0), axis=1, keepdims=True)
            lse = ms[a] + jnp.log2(l)
            o_ref[:, a * HP:(a + 1) * HP] = jnp.where(lane == LSE_COL, lse, accs[a] / l)
            qb_ref[a] = _b(qs[a].astype(f32) - _spread3(jnp.broadcast_to(lse, (TQ, HP)), lane, C_LSE0))

        _exchange_wait(ins, outs, *sems, gather=True, when=(h == FOX_H // HG - 1) & (i == NQ - 1))

    anyspec = pl.BlockSpec(memory_space=pl.ANY)
    qtile = pl.BlockSpec((HG, TQ, HP), lambda h, i: (h, i, 0))
    res = pl.pallas_call(
        body, name="fox_fwd", grid=(FOX_H // HG, NQ),
        in_specs=[qtile, pl.BlockSpec((HG, L, HP), lambda h, i: (h, 0, 0)), pl.BlockSpec((L, HG * HP), lambda h, i: (0, h))] + [anyspec] * n,
        out_specs=[pl.BlockSpec((TQ, HG * HP), lambda h, i: (i, h)), qtile] + [anyspec] * n,
        out_shape=[SDS((L, FOX_H * HP), f32), SDS(qa.shape, bf16)] + [SDS((N_DEV,) + a.shape, a.dtype) for a in shards],
        scratch_shapes=_exchange_sems(n), compiler_params=_cp(2))(qa, ka, fv, *shards)
    return res[0], res[1], res[2:]


def _fox_bwd(qb, ka, va, dob, slabs):
    L = qb.shape[1]
    TQ = TK = _pick(L, ATTN_TILES)
    NQ = L // TQ
    n = len(slabs)
    HG = FOX_HEAD_GROUP

    def body(q_ref, k_ref, v_ref, do_ref, *rest):
        ins, (dq_ref, dk_ref, dv_ref), outs, sems = rest[:n], rest[n:n + 3], rest[n + 3:2 * n + 3], rest[2 * n + 3:]
        h, j = pl.program_id(0), pl.program_id(1)
        cols = [slice(a * HP, (a + 1) * HP) for a in range(HG)]

        _exchange_start(ins, outs, *sems, gather=False, when=(h == 0) & (j == 0))

        @pl.when(j == 0)
        def _():
            dq_ref[...] = jnp.zeros_like(dq_ref)

        kts = [k_ref[a] for a in range(HG)]
        vts = [v_ref[:, cols[a]] for a in range(HG)]
        colg = j * TK + _iota((TQ, TK), 1)
        rowb = _iota((TQ, TK), 0)

        def step(i, carry, masked):
            dks, dvs = carry
            r0 = _tile_start(i, TQ)
            rows = pl.ds(r0, TQ)
            qs = [q_ref[a, rows, :] for a in range(HG)]
            ps = [jnp.exp2(_dot_nt(q, kt)) for q, kt in zip(qs, kts)]
            if masked:
                keep = (colg <= rowb + i * TQ) & (colg >= N_PAD)
                ps = [jnp.where(keep, p, 0.0) for p in ps]
            dobs = [do_ref[rows, cols[a]] for a in range(HG)]
            dvs = [dv + _dot_tn(_b(p), dob) for dv, p, dob in zip(dvs, ps, dobs)]
            dss = [_b(p * _dot_nt(dob, vt)) for p, dob, vt in zip(ps, dobs, vts)]
            for a in range(HG):
                dq_ref[rows, cols[a]] += _dot(dss[a], kts[a])
            dks = [dk + _dot_tn(ds, q) for dk, ds, q in zip(dks, dss, qs)]
            return dks, dvs

        zeros = [jnp.zeros((TK, HP), f32)] * HG
        carry = step(j, (zeros, zeros), True)
        split = jnp.where(j == 0, NQ, j + 1)
        carry = lax.fori_loop(j + 1, split, functools.partial(step, masked=True), carry)
        dks, dvs = lax.fori_loop(split, NQ, functools.partial(step, masked=False), carry)
        for a in range(HG):
            dk_ref[:, cols[a]] = dks[a]
            dv_ref[:, cols[a]] = _b(dvs[a])

        _exchange_wait(ins, outs, *sems, gather=False, when=(h == FOX_H // HG - 1) & (j == NQ - 1))

    head = pl.BlockSpec((L, HG * HP), lambda h, j: (0, h))
    tile = pl.BlockSpec((TK, HG * HP), lambda h, j: (j, h))
    anyspec = pl.BlockSpec(memory_space=pl.ANY)
    res = pl.pallas_call(
        body, name="fox_bwd", grid=(FOX_H // HG, L // TK),
        in_specs=[pl.BlockSpec((HG, L, HP), lambda h, j: (h, 0, 0)), pl.BlockSpec((HG, TK, HP), lambda h, j: (h, j, 0)), tile, head]
        + [anyspec] * n,
        out_specs=[head, tile, tile] + [anyspec] * n,
        out_shape=[SDS((L, FOX_H * HP), f32), SDS((L, FOX_H * HP), f32), SDS((L, FOX_H * HP), bf16)] + [SDS(a.shape, a.dtype) for a in slabs],
        scratch_shapes=_exchange_sems(n), compiler_params=_cp(2))(qb, ka, va, dob, *slabs)
    return res[:3], res[3:]


def _dn_post(y, sd, alog_p, dt_p, valid):
    a = y * _sigmoid(y)
    W = DN_H * DN_D
    heads = []
    for part, scale in ((0, DN_D ** -0.5), (1, 1.0)):
        for h in range(DN_H):
            xh = a[:, part * W + h * DN_D:part * W + (h + 1) * DN_D]
            heads.append(xh * lax.rsqrt(jnp.sum(xh * xh, axis=-1, keepdims=True) + EPS) * scale)
    q = jnp.concatenate(heads[:DN_H], axis=1)
    k = jnp.concatenate(heads[DN_H:], axis=1)
    v = a[:, 2 * W:3 * W]
    lane = _iota(sd.shape, 1)
    beta = _sigmoid(sd) * valid
    g = -jnp.exp(alog_p) * jax.nn.softplus(sd + dt_p) * valid
    bg = jnp.where(lane < DN_H, beta, jnp.where(lane < 2 * DN_H, g, 0.0))
    return q, k, v, bg


def _conv_fwd(ext_ref, cw_ref, TM):
    y = cw_ref[0:1, :] * ext_ref[8 - (CONV_K - 1):8 - (CONV_K - 1) + TM, :]
    for i in range(1, CONV_K):
        o = 8 - (CONV_K - 1) + i
        y = y + cw_ref[i:i + 1, :] * ext_ref[o:o + TM, :]
    return y


def _dn_prep(dn, sd, cw, alog_p, dt_p):
    L, W3 = dn.shape
    TM = _pick(L, ROW_TILES)
    W = DN_H * DN_D

    def body(dn_ref, halo_ref, sd_ref, cw_ref, al_ref, dt_ref, q_ref, k_ref, v_ref, bg_ref, ext):
        i = pl.program_id(0)
        ext[0:8, :] = jnp.where(i == 0, 0.0, halo_ref[...])
        ext[8:, :] = dn_ref[...]
        y = _conv_fwd(ext, cw_ref, TM)
        valid = ((i * TM + _iota((TM, 1), 0)) >= N_PAD).astype(f32)
        q, k, v, bg = _dn_post(y, sd_ref[...], al_ref[...], dt_ref[...], valid)
        q_ref[...], k_ref[...], v_ref[...], bg_ref[...] = q, k, v, bg

    row = lambda wd: pl.BlockSpec((TM, wd), lambda i: (i, 0))
    vec = pl.BlockSpec((1, HP), lambda i: (0, 0))
    return pl.pallas_call(
        body, name="dn_prep", grid=(L // TM,),
        in_specs=[row(W3), pl.BlockSpec((8, W3), lambda i: (jnp.maximum(i * (TM // 8) - 1, 0), 0)), row(HP),
                  pl.BlockSpec((CONV_K, W3), lambda i: (0, 0)), vec, vec],
        out_specs=[row(W), row(W), row(W), row(HP)],
        out_shape=[SDS((L, W), f32)] * 3 + [SDS((L, HP), f32)],
        scratch_shapes=[pltpu.VMEM((TM + 8, W3), f32)], compiler_params=_cp())(dn, dn, sd, cw, alog_p, dt_p)


def _dn_prep_bwd(dn, sd, cw, alog_p, dt_p, dq, dk, dv, dbg):
    L, W3 = dn.shape
    TM = _pick(L, ROW_TILES)
    NT = L // TM
    W = DN_H * DN_D

    def body(dn_ref, halo_ref, sd_ref, cw_ref, al_ref, dt_ref, dq_ref, dk_ref, dv_ref, dbg_ref,
             dg_ref, dcw_ref, dp_ref, ext, dyp, carry):
        i = pl.program_id(0)
        t = NT - 1 - i

        @pl.when(i == 0)
        def _():
            carry[...] = jnp.zeros_like(carry)
            dcw_ref[...] = jnp.zeros_like(dcw_ref)
            dp_ref[...] = jnp.zeros_like(dp_ref)
            dyp[...] = jnp.zeros_like(dyp)

        ext[0:8, :] = jnp.where(t == 0, 0.0, halo_ref[...])
        ext[8:, :] = dn_ref[...]
        y = _conv_fwd(ext, cw_ref, TM)
        valid = ((t * TM + _iota((TM, 1), 0)) >= N_PAD).astype(f32)
        _, vjp = jax.vjp(functools.partial(_dn_post, valid=valid), y, sd_ref[...], al_ref[...], dt_ref[...])
        dy, dsd, dal, ddt = vjp((dq_ref[...], dk_ref[...], dv_ref[...], dbg_ref[...]))
        dg_ref[:, W3:W3 + HP] = _b(dsd)
        dg_ref[:, W3 + HP:] = jnp.zeros((TM, SMALL_W - HP), bf16)
        dp_ref[0:1, :] += dal
        dp_ref[1:2, :] += ddt
        dyp[8:8 + TM, :] = dy
        o0 = CONV_K - 1
        dext = cw_ref[0:1, :] * dyp[o0:o0 + TM + 8, :]
        for k in range(1, CONV_K):
            dext = dext + cw_ref[k:k + 1, :] * dyp[o0 - k:o0 - k + TM + 8, :]
        for k in range(CONV_K):
            o = 8 - (CONV_K - 1) + k
            dcw_ref[k:k + 1, :] += jnp.sum(dy * ext[o:o + TM, :], axis=0, keepdims=True)
        dg_ref[:, 0:W3] = _b(jnp.concatenate([dext[8:TM, :], dext[TM:TM + 8, :] + carry[...]], axis=0))
        carry[...] = dext[0:8, :]

    row = lambda wd: pl.BlockSpec((TM, wd), lambda i: (NT - 1 - i, 0))
    vec = pl.BlockSpec((1, HP), lambda i: (0, 0))
    return pl.pallas_call(
        body, name="dn_prep_bwd", grid=(NT,),
        in_specs=[row(W3), pl.BlockSpec((8, W3), lambda i: (jnp.maximum((NT - 1 - i) * (TM // 8) - 1, 0), 0)), row(HP),
                  pl.BlockSpec((CONV_K, W3), lambda i: (0, 0)), vec, vec, row(W), row(W), row(W), row(HP)],
        out_specs=[row(W3 + SMALL_W), pl.BlockSpec((8, W3), lambda i: (0, 0)), pl.BlockSpec((8, HP), lambda i: (0, 0))],
        out_shape=[SDS((L, W3 + SMALL_W), bf16), SDS((8, W3), f32), SDS((8, HP), f32)],
        scratch_shapes=[pltpu.VMEM((TM + 8, W3), f32), pltpu.VMEM((TM + 16, W3), f32), pltpu.VMEM((8, W3), f32)],
        compiler_params=_cp())(dn, dn, sd, cw, alog_p, dt_p, dq, dk, dv, dbg)


def _split2(x):
    hi = _b(x)
    return hi, _b(x - hi.astype(f32))


def _split3(x):
    hi = _b(x)
    r = x - hi.astype(f32)
    mid = _b(r)
    return hi, mid, _b(r - mid.astype(f32))


def _x3(a, b, dot):
    (a1, a2), (b1, b2) = _split2(a), _split2(b)
    return dot(a1, b1) + (dot(a1, b2) + dot(a2, b1))


@jax.custom_vjp
def _dot_x3(a, b):
    return _x3(a, b, _dot)


_dot_x3.defvjp(lambda a, b: (_x3(a, b, _dot), (a, b)), lambda res, g: (_x3(g, res[1], _dot_nt), _x3(res[0], g, _dot_tn)))


def _exact3(m, x, dot):
    x1, x2, x3 = _split3(x)
    return dot(m, x1) + (dot(m, x2) + dot(m, x3))


def _tri_ones(C, lower):
    row, col = _iota((C, C), 0), _iota((C, C), 1)
    return _b(((row >= col) if lower else (row <= col)).astype(f32))


@jax.custom_vjp
def _chunk_cumsum(x):
    return _exact3(_tri_ones(x.shape[0], True), x, _dot)


_chunk_cumsum.defvjp(lambda x: (_exact3(_tri_ones(x.shape[0], True), x, _dot), None),
                     lambda _, g: (_exact3(_tri_ones(g.shape[0], False), g, _dot),))


def _mxu_transpose(x):
    C = x.shape[0]
    eye = _b((_iota((C, C), 0) == _iota((C, C), 1)).astype(f32))
    return _exact3(eye, x, lambda m, part: _dot_tn(part, m))


@jax.custom_vjp
def _transpose_exact(x):
    return _mxu_transpose(x)


_transpose_exact.defvjp(lambda x: (_mxu_transpose(x), None), lambda _, g: (_mxu_transpose(g),))


def _unit_lower_inverses(lows):
    C = lows[0].shape[0]
    P = jnp.stack(lows)
    X = (_iota((C, C), 0) == _iota((C, C), 1)).astype(f32)[None] - P
    bdot = functools.partial(_x3, dot=lambda a, b: jnp.einsum("bij,bjk->bik", a, b, preferred_element_type=f32))
    for _ in range(5):
        P = bdot(P, P)
        X = X + bdot(X, P)
    return [X[i] for i in range(len(lows))]


@jax.custom_vjp
def _inverse_given(low, X):
    return X


def _inverse_given_bwd(X, g):
    return -_x3(_x3(X, g, _dot_tn), X, _dot_nt), jnp.zeros_like(X)


_inverse_given.defvjp(lambda low, X: (X, X), _inverse_given_bwd)


def _dn_intra_pre(q, k, v, bg):
    C = DN_C
    row, col = _iota((C, C), 0), _iota((C, C), 1)
    tri = row >= col
    G = _chunk_cumsum(bg)
    GT = _transpose_exact(G)
    lane = _iota((C, HP), 1)
    rowt = _iota((HP, C), 0)
    last = _iota((C, 1), 0) == C - 1
    heads = []
    for h in range(DN_H):
        beta = jnp.sum(jnp.where(lane == h, bg, 0.0), axis=1, keepdims=True)
        gcol = jnp.sum(jnp.where(lane == DN_H + h, G, 0.0), axis=1, keepdims=True)
        grow = jnp.sum(jnp.where(rowt == DN_H + h, GT, 0.0), axis=0, keepdims=True)
        glast = jnp.sum(jnp.where(last, gcol, 0.0), axis=0, keepdims=True)
        decay = jnp.exp(jnp.where(tri, gcol - grow, NEG))
        qh, kh, vh = (t[:, h * DN_D:(h + 1) * DN_D] for t in (q, k, v))
        kb = kh * beta
        low = jnp.where(row > col, _dot_nt(_b(kb), _b(kh)) * decay, 0.0)
        heads.append((beta, gcol, glast, decay, qh, kh, vh, kb, low))
    return heads


def _dn_intra_post(heads, xs):
    lane1 = _iota((1, HP), 1)
    us, ws, qds, kds, attns = [], [], [], [], []
    glrow = jnp.zeros((1, HP), f32)
    for h, ((beta, gcol, glast, decay, qh, kh, vh, kb, _), X) in enumerate(zip(heads, xs)):
        eg = jnp.exp(gcol)
        us.append(_dot_x3(X, vh * beta))
        ws.append(_dot_x3(X, kb * eg))
        attns.append(_dot_nt(_b(qh), _b(kh)) * decay)
        qds.append(qh * eg)
        kds.append(kh * jnp.exp(glast - gcol))
        glrow = glrow + jnp.where(lane1 == h, glast, 0.0)
    cat = lambda xs_: jnp.concatenate(xs_, axis=1)
    return cat(us), cat(ws), cat(qds), cat(kds), cat(attns), glrow, cat(list(xs))


def _dn_intra_group(q, k, v, bg, xs):
    G = q.shape[0] // DN_C
    rows = [slice(j * DN_C, (j + 1) * DN_C) for j in range(G)]
    pre = [_dn_intra_pre(q[r, :], k[r, :], v[r, :], bg[r, :]) for r in rows]
    inv = [[_inverse_given(hd[-1], x) for hd, x in zip(heads, xj)] for heads, xj in zip(pre, xs)]
    post = [_dn_intra_post(heads, xj) for heads, xj in zip(pre, inv)]
    return tuple(jnp.concatenate([p[i] for p in post], axis=0) for i in range(5)) + (tuple(p[5] for p in post),)


def _lane_pick(rowvec, h):
    return jnp.sum(jnp.where(_iota(rowvec.shape, 1) == h, rowvec, 0.0), axis=1, keepdims=True)


def _dn_intra(q, k, v, bg):
    L, W = q.shape
    NC = L // DN_C
    G = _pick(NC, DN_INTRA_GROUP)
    R = G * DN_C
    WA = DN_H * DN_C

    def body(q_ref, k_ref, v_ref, bg_ref, u_ref, w_ref, qd_ref, kd_ref, at_ref, gl_ref, x_ref):
        rows = [slice(j * DN_C, (j + 1) * DN_C) for j in range(G)]
        pre = [_dn_intra_pre(q_ref[r, :], k_ref[r, :], v_ref[r, :], bg_ref[r, :]) for r in rows]
        inv = _unit_lower_inverses([hd[-1] for heads in pre for hd in heads])
        for j, r in enumerate(rows):
            u, w, qd, kd, at, gl, xs = _dn_intra_post(pre[j], inv[j * DN_H:(j + 1) * DN_H])
            u_ref[r, :], x_ref[r, :] = u, xs
            w_ref[r, :], qd_ref[r, :], kd_ref[r, :], at_ref[r, :] = _b(w), _b(qd), _b(kd), _b(at)
            gl_ref[j] = gl

    row = lambda wd: pl.BlockSpec((R, wd), lambda n: (n, 0))
    return pl.pallas_call(
        body, name="dn_intra", grid=(NC // G,),
        in_specs=[row(W), row(W), row(W), row(HP)],
        out_specs=[row(W), row(W), row(W), row(W), row(WA), pl.BlockSpec((G, 1, HP), lambda n: (n, 0, 0)), row(WA)],
        out_shape=[SDS((L, W), f32), SDS((L, W), bf16), SDS((L, W), bf16), SDS((L, W), bf16), SDS((L, WA), bf16), SDS((NC, 1, HP), f32),
                   SDS((L, WA), f32)],
        compiler_params=_cp())(q, k, v, bg)


def _dn_scan(u, w, qd, kd, at, gl):
    L, W = u.shape
    NC = L // DN_C
    G = _pick(NC, DN_SCAN_GROUP)
    R = G * DN_C

    def body(u_ref, w_ref, qd_ref, kd_ref, at_ref, gl_ref, o_ref, vn_ref, s_ref, S):
        @pl.when(pl.program_id(0) == 0)
        def _():
            S[...] = jnp.zeros_like(S)

        for j in range(G):
            r = slice(j * DN_C, (j + 1) * DN_C)
            glrow = gl_ref[j]
            for h in range(DN_H):
                c = slice(h * DN_D, (h + 1) * DN_D)
                Sh = S[h]
                s_ref[j, h] = Sh
                Sb = _b(Sh)
                vb = _b(u_ref[r, c] - _dot(w_ref[r, c], Sb))
                vn_ref[r, c] = vb
                o_ref[r, c] = _dot(qd_ref[r, c], Sb) + _dot(at_ref[r, h * DN_C:(h + 1) * DN_C], vb)
                S[h] = Sh * jnp.exp(_lane_pick(glrow, h)) + _dot_tn(kd_ref[r, c], vb)

    row = lambda wd: pl.BlockSpec((R, wd), lambda n: (n, 0))
    return pl.pallas_call(
        body, name="dn_scan", grid=(NC // G,),
        in_specs=[row(W), row(W), row(W), row(W), row(DN_H * DN_C), pl.BlockSpec((G, 1, HP), lambda n: (n, 0, 0))],
        out_specs=[row(W), row(W), pl.BlockSpec((G, DN_H, DN_D, DN_D), lambda n: (n, 0, 0, 0))],
        out_shape=[SDS((L, W), f32), SDS((L, W), bf16), SDS((NC, DN_H, DN_D, DN_D), f32)],
        scratch_shapes=[pltpu.VMEM((DN_H, DN_D, DN_D), f32)], compiler_params=_cp())(u, w, qd, kd, at, gl)


def _dn_scan_bwd(do, w, qd, kd, at, gl):
    L, W = do.shape
    NC = L // DN_C
    G = _pick(NC, DN_SCAN_GROUP)
    R = G * DN_C
    NS = NC // G

    def body(do_ref, w_ref, qd_ref, kd_ref, at_ref, gl_ref, dvn_ref, ds_ref, dS):
        @pl.when(pl.program_id(0) == 0)
        def _():
            dS[...] = jnp.zeros_like(dS)

        for j in reversed(range(G)):
            r = slice(j * DN_C, (j + 1) * DN_C)
            glrow = gl_ref[j]
            for h in range(DN_H):
                c = slice(h * DN_D, (h + 1) * DN_D)
                dSo = dS[h]
                ds_ref[j, h] = dSo
                dob = _b(do_ref[r, c])
                dvn = _dot_tn(at_ref[r, h * DN_C:(h + 1) * DN_C], dob) + _dot(kd_ref[r, c], _b(dSo))
                dvn_ref[r, c] = dvn
                dS[h] = _dot_tn(qd_ref[r, c], dob) + dSo * jnp.exp(_lane_pick(glrow, h)) - _dot_tn(w_ref[r, c], _b(dvn))

    row = lambda wd: pl.BlockSpec((R, wd), lambda n: (NS - 1 - n, 0))
    return pl.pallas_call(
        body, name="dn_scan_bwd", grid=(NS,),
        in_specs=[row(W), row(W), row(W), row(W), row(DN_H * DN_C), pl.BlockSpec((G, 1, HP), lambda n: (NS - 1 - n, 0, 0))],
        out_specs=[row(W), pl.BlockSpec((G, DN_H, DN_D, DN_D), lambda n: (NS - 1 - n, 0, 0, 0))],
        out_shape=[SDS((L, W), f32), SDS((NC, DN_H, DN_D, DN_D), f32)],
        scratch_shapes=[pltpu.VMEM((DN_H, DN_D, DN_D), f32)], compiler_params=_cp())(do, w, qd, kd, at, gl)


def _dn_intra_bwd(q, k, v, bg, xinv, do, vn, dvn, states, dstates):
    L, W = q.shape
    NC = L // DN_C
    G = _pick(NC, DN_INTRA_GROUP)
    R = G * DN_C

    def body(q_ref, k_ref, v_ref, bg_ref, x_ref, do_ref, vn_ref, dvn_ref, s_ref, ds_ref, dq_ref, dk_ref, dv_ref, dbg_ref):
        lane1 = _iota((1, HP), 1)
        rows = [slice(j * DN_C, (j + 1) * DN_C) for j in range(G)]
        xs = [[x_ref[r, h * DN_C:(h + 1) * DN_C] for h in range(DN_H)] for r in rows]
        fwd, vjp = jax.vjp(functools.partial(_dn_intra_group, xs=xs), q_ref[...], k_ref[...], v_ref[...], bg_ref[...])
        dws, dqds, dkds, dats, dgls = [], [], [], [], []
        for j, r in enumerate(rows):
            dw, dqd, dkd, dat = [], [], [], []
            dgl = jnp.zeros((1, HP), f32)
            for h in range(DN_H):
                c = slice(h * DN_D, (h + 1) * DN_D)
                Sh, dSo = s_ref[j, h], ds_ref[j, h]
                Sb, dob, vb = _b(Sh), _b(do_ref[r, c]), vn_ref[r, c]
                dw.append(-_dot_nt(_b(dvn_ref[r, c]), Sb))
                dqd.append(_dot_nt(dob, Sb))
                dat.append(_dot_nt(dob, vb))
                dkd.append(_dot_nt(vb, _b(dSo)))
                dcd = jnp.sum(jnp.sum(Sh * dSo, axis=1, keepdims=True), axis=0, keepdims=True)
                dgl = dgl + jnp.where(lane1 == h, dcd * jnp.exp(_lane_pick(fwd[5][j], h)), 0.0)
            cat = lambda xs_: jnp.concatenate(xs_, axis=1)
            dws.append(cat(dw)), dqds.append(cat(dqd)), dkds.append(cat(dkd)), dats.append(cat(dat)), dgls.append(dgl)
        cat0 = lambda xs_: jnp.concatenate(xs_, axis=0)
        dq, dk, dv, dbg = vjp((dvn_ref[...], cat0(dws), cat0(dqds), cat0(dkds), cat0(dats), tuple(dgls)))
        dq_ref[...], dk_ref[...], dv_ref[...], dbg_ref[...] = dq, dk, dv, dbg

    row = lambda wd: pl.BlockSpec((R, wd), lambda s: (s, 0))
    st = pl.BlockSpec((G, DN_H, DN_D, DN_D), lambda s: (s, 0, 0, 0))
    return pl.pallas_call(
        body, name="dn_intra_bwd", grid=(NC // G,),
        in_specs=[row(W), row(W), row(W), row(HP), row(DN_H * DN_C), row(W), row(W), row(W), st, st],
        out_specs=[row(W), row(W), row(W), row(HP)],
        out_shape=[SDS((L, W), f32)] * 3 + [SDS((L, HP), f32)],
        compiler_params=_cp())(q, k, v, bg, xinv, do, vn, dvn, states, dstates)


def _dn_normgate(oraw, dz, wn):
    outs = []
    for h in range(DN_H):
        sl = slice(h * DN_D, (h + 1) * DN_D)
        z = dz[:, sl]
        outs.append(_rms(oraw[:, sl], wn) * (z * _sigmoid(z)))
    return jnp.concatenate(outs, axis=1)


def _mix_fwd(op, oraw, dz, ga, gb, h0, wn, wbf, wbd, wo):
    L, D = h0.shape
    TM = _pick(L, ROW_TILES)

    def body(op_ref, or_ref, dz_ref, ga_ref, gb_ref, h0_ref, wn_ref, wbf_ref, wbd_ref, wo_ref, h1_ref):
        pf = _dot(_b(op_ref[...]), wbf_ref[...])
        pd = _dot(_b(_dn_normgate(or_ref[...], dz_ref[...], wn_ref[...])), wbd_ref[...])
        y = _sigmoid(ga_ref[...]) * pf + _sigmoid(gb_ref[...]) * pd
        h1_ref[...] = h0_ref[...] + _dot(_b(y), wo_ref[...])

    row = lambda wd: pl.BlockSpec((TM, wd), lambda i: (i, 0))
    full = lambda a: pl.BlockSpec(a.shape, lambda i: (0, 0))
    return pl.pallas_call(
        body, name="mix_fwd", grid=(L // TM,),
        in_specs=[row(op.shape[1]), row(oraw.shape[1]), row(dz.shape[1]), row(D), row(D), row(D), full(wn), full(wbf), full(wbd), full(wo)],
        out_specs=row(D), out_shape=SDS((L, D), f32), compiler_params=_cp())(op, oraw, dz, ga, gb, h0, wn, wbf, wbd, wo)


def _mix_bwd(dh1, op, oraw, dz, ga, gb, wn, wbf, wbd, wo):
    L, D = dh1.shape
    TM = _pick(L, ROW_TILES)
    WF, WD = op.shape[1], oraw.shape[1]

    def body(dh1_ref, op_ref, or_ref, dz_ref, ga_ref, gb_ref, wn_ref, wbf_ref, wbd_ref, wo_ref,
             dop_ref, dor_ref, dg_ref, af_ref, ad_ref, dpf_ref, dpd_ref, y_ref, dmix_ref, acc_ref):
        @pl.when(pl.program_id(0) == 0)
        def _():
            acc_ref[...] = jnp.zeros_like(acc_ref)

        opv = op_ref[...]
        af = _b(opv)
        ad, vjp = jax.vjp(_dn_normgate, or_ref[...], dz_ref[...], wn_ref[...])
        adb = _b(ad)
        pf, pd = _dot(af, wbf_ref[...]), _dot(adb, wbd_ref[...])
        sa, sb = _sigmoid(ga_ref[...]), _sigmoid(gb_ref[...])
        dmix = _b(dh1_ref[...])
        dy = _dot_nt(dmix, wo_ref[...])
        dpf, dpd = _b(dy * sa), _b(dy * sb)
        dor, ddz, dwn = vjp(_dot_nt(dpd, wbd_ref[...]))
        dop = _dot_nt(dpf, wbf_ref[...])
        lane = _iota((TM, HP), 1)
        for h in range(WF // HP):
            c = slice(h * HP, (h + 1) * HP)
            delta = jnp.sum(jnp.where(lane < FOX_D, dop[:, c] * opv[:, c], 0.0), axis=1, keepdims=True)
            dop_ref[:, c] = _b(dop[:, c] - _spread3(jnp.broadcast_to(delta, (TM, HP)), lane, C_DELTA0))
        dor_ref[...] = dor
        dg_ref[:, 0:WD] = _b(ddz)
        dg_ref[:, WD:WD + D] = _b(dy * pf * sa * (1.0 - sa))
        dg_ref[:, WD + D:] = _b(dy * pd * sb * (1.0 - sb))
        af_ref[...], ad_ref[...], y_ref[...] = af, adb, _b(sa * pf + sb * pd)
        dpf_ref[...], dpd_ref[...], dmix_ref[...] = dpf, dpd, dmix
        acc_ref[0:1, :] += dwn

    row = lambda wd: pl.BlockSpec((TM, wd), lambda i: (i, 0))
    full = lambda a: pl.BlockSpec(a.shape, lambda i: (0, 0))
    return pl.pallas_call(
        body, name="mix_bwd", grid=(L // TM,),
        in_specs=[row(D), row(WF), row(WD), row(WD), row(D), row(D), full(wn), full(wbf), full(wbd), full(wo)],
        out_specs=[row(WF), row(WD), row(WD + 2 * D), row(WF), row(WD), row(D), row(D), row(D), row(D),
                   pl.BlockSpec((8, HP), lambda i: (0, 0))],
        out_shape=[SDS((L, WF), bf16), SDS((L, WD), f32), SDS((L, WD + 2 * D), bf16), SDS((L, WF), bf16), SDS((L, WD), bf16),
                   SDS((L, D), bf16), SDS((L, D), bf16), SDS((L, D), bf16), SDS((L, D), bf16), SDS((8, HP), f32)],
        compiler_params=_cp())(dh1, op, oraw, dz, ga, gb, wn, wbf, wbd, wo)


def _ffn_fwd_bwd(h1, tgt, w2, wf, wgt, wut, wd):
    L, D = h1.shape
    F = wd.shape[0]
    TM = _pick(L, FFN_TILES)

    def body(h_ref, t_ref, w2_ref, wf_ref, wg_hbm, wu_hbm, wd_hbm,
             dh1_ref, xn_ref, dg_ref, du_ref, act_ref, dh2_ref, acc_ref, wg_v, wu_v, wd_v, sems):
        i = pl.program_id(0)
        _load_once([(wg_hbm, wg_v), (wu_hbm, wu_v), (wd_hbm, wd_v)], sems)

        @pl.when(i == 0)
        def _():
            acc_ref[...] = jnp.zeros_like(acc_ref)

        h1v = h_ref[...]
        xn2, vjp2 = jax.vjp(_rms, h1v, w2_ref[...])
        xb = _b(xn2)
        g, u = _dot_nt(xb, wg_v[...]), _dot_nt(xb, wu_v[...])
        sg = _sigmoid(g)
        ab = _b(g * sg * u)
        h2 = h1v + _dot(ab, wd_v[...])
        out, vjpf = jax.vjp(_rms, h2, wf_ref[...])
        valid = (i * TM + _iota((TM, 1), 0)) >= PREFIX
        diff = jnp.where(valid, out - t_ref[...], 0.0)
        loss = 0.5 * jnp.sum(jnp.sum(diff * diff, axis=1, keepdims=True), axis=0, keepdims=True) / D
        dh2, dwf = vjpf(diff * (1.0 / D))
        dh2b = _b(dh2)
        dact = _dot_nt(dh2b, wd_v[...])
        dgb = _b(dact * u * (sg * (1.0 + g * (1.0 - sg))))
        dub = _b(dact * (g * sg))
        dh1n, dw2 = vjp2(_dot(dgb, wg_v[...]) + _dot(dub, wu_v[...]))
        dh1_ref[...] = dh2 + dh1n
        xn_ref[...], dg_ref[...], du_ref[...], act_ref[...], dh2_ref[...] = xb, dgb, dub, ab, dh2b
        acc_ref[0:1, :] += dw2
        acc_ref[1:2, :] += dwf
        acc_ref[2:3, :] += jnp.broadcast_to(loss, (1, D))

    row = lambda wd_: pl.BlockSpec((TM, wd_), lambda i: (i, 0))
    vec = pl.BlockSpec((1, D), lambda i: (0, 0))
    anyspec = pl.BlockSpec(memory_space=pl.ANY)
    return pl.pallas_call(
        body, name="ffn_fwd_bwd", grid=(L // TM,),
        in_specs=[row(D), row(D), vec, vec, anyspec, anyspec, anyspec],
        out_specs=[row(D), row(D), row(F), row(F), row(F), row(D), pl.BlockSpec((8, D), lambda i: (0, 0))],
        out_shape=[SDS((L, D), f32), SDS((L, D), bf16), SDS((L, F), bf16), SDS((L, F), bf16), SDS((L, F), bf16), SDS((L, D), bf16),
                   SDS((8, D), f32)],
        scratch_shapes=[pltpu.VMEM((F, D), bf16), pltpu.VMEM((F, D), bf16), pltpu.VMEM((F, D), bf16), pltpu.SemaphoreType.DMA((3,))],
        compiler_params=_cp())(h1, tgt, w2, wf, wgt, wut, wd)


def _pad_lanes(v, n=HP):
    return jnp.pad(v.astype(f32), ((0, 0), (0, n - v.shape[1])))


def _pack_w_in(wt_full):
    D = wt_full.shape[1]
    FW, DW = FOX_H * FOX_D, DN_H * DN_D
    o = 0
    parts = {}
    for name, wd in (("fq", FW), ("fk", FW), ("fv", FW), ("fl", FOX_H), ("dn", 3 * DW), ("ba", 2 * DN_H), ("dz", DW), ("ga", D), ("gb", D)):
        parts[name] = wt_full[o:o + wd]
        o += wd
    assert o == wt_full.shape[0]
    heads = lambda w: jnp.pad(w.reshape(FOX_H, FOX_D, D), ((0, 0), (0, HP - FOX_D), (0, 0))).reshape(FOX_H * HP, D)
    small = lambda w: jnp.pad(w, ((0, SMALL_W - w.shape[0]), (0, 0)))
    packed = dict(fq=heads(parts["fq"]), fk=heads(parts["fk"]), fv=heads(parts["fv"]), sf=small(parts["fl"]), sd=small(parts["ba"]),
                  dn=parts["dn"], dz=parts["dz"], ga=parts["ga"], gb=parts["gb"])
    return jnp.concatenate([packed[name] for name, _, _, _ in _seg_layout(D)], axis=0)


def _unpack_w_in(groups, d_model):
    D = groups[0].shape[1]
    FW = FOX_H * FOX_D
    segs = {}
    for grp, g in zip(GROUPS, groups):
        o = 0
        for name, wd, _, sg in _seg_layout(d_model):
            if sg == grp:
                segs[name] = g[o:o + wd]
                o += wd
    heads = lambda g: g.reshape(FOX_H, HP, D)[:, :FOX_D].reshape(FW, D)
    return jnp.concatenate([heads(segs["fq"]), heads(segs["fk"]), heads(segs["fv"]), segs["sf"][:FOX_H], segs["dn"],
                            segs["sd"][:2 * DN_H], segs["dz"], segs["ga"], segs["gb"]], axis=0)


def _local_step(x, tgt, meta, w1, w_in_t, fbias, cw, alog, dtb, wn, w2, wf, late_shards):
    T, D = x.shape
    h0 = jnp.concatenate([jnp.zeros((N_PAD, D), f32), meta, x], axis=0)
    tgt_p = jnp.concatenate([jnp.zeros((PREFIX, D), f32), tgt], axis=0)
    wp = _pack_w_in(w_in_t)
    bias_p, alog_p, dt_p = _pad_lanes(fbias), _pad_lanes(jnp.pad(alog, ((0, 0), (DN_H, 0)))), _pad_lanes(jnp.pad(dtb, ((0, 0), (DN_H, 0))))

    (xn, fq, fk, sf, fv, dn, sd, dz, ga, gb), g_mix = _in_proj(h0, w1, wp, [late_shards[n] for n in LATE_MIX])
    qa, ka, va = _fox_prep(fq, fk, fv, sf, bias_p)
    op, qb, g_ffn = _fox_fwd(qa, ka, va, [late_shards[n] for n in LATE_FFN])
    full = {n: _from_slabs(n, s) for n, s in zip(LATE_MIX + LATE_FFN, tuple(g_mix) + tuple(g_ffn))}
    wbf, wbd, wo, wgt, wut, wd = (full[n] for n in ("w_branch_fox", "w_branch_dn", "w_out", "w_ffn_gate", "w_ffn_up", "w_ffn_down"))
    wbf_p = jnp.pad(wbf.reshape(FOX_H, FOX_D, D), ((0, 0), (0, HP - FOX_D), (0, 0))).reshape(FOX_H * HP, D)
    qn, kn, vn, bg = _dn_prep(dn, sd, cw, alog_p, dt_p)
    u_dn, w_dn, qd_dn, kd_dn, at_dn, gl_dn, x_dn = _dn_intra(qn, kn, vn, bg)
    oraw, vnew, states = _dn_scan(u_dn, w_dn, qd_dn, kd_dn, at_dn, gl_dn)
    h1 = _mix_fwd(op, oraw, dz, ga, gb, h0, wn, wbf_p, wbd, wo)

    dh1, xn2, dgate, dup, act, dh2, acc_f = _ffn_fwd_bwd(h1, tgt_p, w2, wf, wgt, wut, wd)
    g_wg, g_wu, g_wd = _matmul_tn(dgate, xn2, "dw_ffn_gate"), _matmul_tn(dup, xn2, "dw_ffn_up"), _matmul_tn(act, dh2, "dw_ffn_down")

    dop, dor, d_mix, af, ad, dpf, dpd, yb, dmix, acc_m = _mix_bwd(dh1, op, oraw, dz, ga, gb, wn, wbf_p, wbd, wo)
    g_wbf = _matmul_tn(af, dpf, "dw_branch_fox").reshape(FOX_H, HP, D)[:, :FOX_D].reshape(FOX_H * FOX_D, D)
    g_wbd, g_wo = _matmul_tn(ad, dpd, "dw_branch_dn"), _matmul_tn(yb, dmix, "dw_out")

    dvnew, dstates = _dn_scan_bwd(dor, w_dn, qd_dn, kd_dn, at_dn, gl_dn)
    dqn, dkn, dvn, dbg = _dn_intra_bwd(qn, kn, vn, bg, x_dn, dor, vnew, dvnew, states, dstates)
    d_dn, acc_cw, acc_p = _dn_prep_bwd(dn, sd, cw, alog_p, dt_p, dqn, dkn, dvn, dbg)
    g_late = dict(w_branch_fox=g_wbf, w_branch_dn=g_wbd, w_out=g_wo, w_ffn_gate=g_wg, w_ffn_up=g_wu, w_ffn_down=g_wd)
    (dqa, dka, d_fv), recv = _fox_bwd(qb, ka, va, dop, [_to_slabs(n, g_late[n]) for n in LATE])
    d_fox, acc_b = _fox_prep_bwd(dqa, dka, sf, bias_p)

    dgroups = [d_fox, d_fv, d_dn, d_mix]
    g_wp = [_matmul_tn(dg, xn, "dw_in_" + grp) for grp, dg in zip(GROUPS, dgroups)]
    dh0, acc_1, (recv_w_in,) = _in_proj_bwd(dgroups, wp, h0, w1, dh1, [_to_slabs("w_in", _unpack_w_in(g_wp, D))])
    recv = dict(zip(LATE, recv), w_in=recv_w_in)

    small = dict(loss=acc_f[2, 0:1], mix_norm_w=acc_1[0], fox_forget_bias=acc_b[0, :FOX_H], dn_a_log=acc_p[0, DN_H:2 * DN_H],
                 dn_dt_bias=acc_p[1, DN_H:2 * DN_H], dn_out_norm_w=acc_m[0], ffn_norm_w=acc_f[0], final_norm_w=acc_f[1],
                 meta_tokens=dh0[N_PAD:PREFIX].reshape(-1), dn_conv_w=acc_cw[:CONV_K].reshape(-1))
    return dh0[PREFIX:], small, recv


def _mesh_pos():
    x, y, c = lax.axis_index("x"), lax.axis_index("y"), lax.axis_index("c")
    return x, y, c, 4 * x + 2 * y + c


def _peer(x, y, c, m):
    flip = lambda v, on: 1 - v if on else v
    px, py, pc = flip(x, m & 4), flip(y, m & 2), flip(c, m & 1)
    return (px, py, pc), 4 * px + 2 * py + pc


def _exchange_sems(n):
    return [pltpu.SemaphoreType.DMA((n, N_DEV - 1)), pltpu.SemaphoreType.DMA((n, N_DEV - 1)), pltpu.SemaphoreType.DMA((n,))]


def _exchange_part(ins, outs, send_sems, recv_sems, loc_sems, gather, m, receive):
    x, y, c, me = _mesh_pos()
    src = lambda a, pid: ins[a] if gather else ins[a].at[pid]
    if m == 0:
        return [pltpu.make_async_copy(src(a, me), outs[a].at[me], loc_sems.at[a]) for a in range(len(ins))]
    peer, pid = _peer(x, y, c, m)
    return [pltpu.make_async_remote_copy(src_ref=src(a, pid), dst_ref=outs[a].at[pid if receive else me], send_sem=send_sems.at[a, m - 1],
                                         recv_sem=recv_sems.at[a, m - 1], device_id=peer, device_id_type=MESH) for a in range(len(ins))]


def _exchange_start(*refs, gather, when):
    @pl.when(when)
    def _():
        for m in range(N_DEV):
            for cp in _exchange_part(*refs, gather, m, receive=False):
                cp.start()


def _exchange_wait(*refs, gather, when):
    @pl.when(when)
    def _():
        for m in range(1, N_DEV):
            for cp in _exchange_part(*refs, gather, m, receive=True):
                cp.wait_recv()
        for m in list(range(1, N_DEV)) + [0]:
            for cp in _exchange_part(*refs, gather, m, receive=False):
                cp.wait() if m == 0 else cp.wait_send()


def _gather_two_level(arrays, name):
    n = len(arrays)

    def body(*refs):
        ins, outs, (send_sems, recv_sems, loc_sems) = refs[:n], refs[n:2 * n], refs[2 * n:]
        x, y, c, me = _mesh_pos()
        sib = (x, y, 1 - c)
        chips = [(1 - x, y), (x, 1 - y), (1 - x, 1 - y)]
        dev_id = lambda px, py, pc: 4 * px + 2 * py + pc

        def copy(a, k, block, to, own=False):
            return pltpu.make_async_remote_copy(src_ref=ins[a] if own else outs[a].at[block], dst_ref=outs[a].at[block],
                                                send_sem=send_sems.at[a, k], recv_sem=recv_sems.at[a, k], device_id=to, device_id_type=MESH)

        local = [pltpu.make_async_copy(ins[a], outs[a].at[me], loc_sems.at[a]) for a in range(n)]
        first = [copy(a, 0, me, sib, own=True) for a in range(n)]
        first += [copy(a, 1 + j, me, (*chip, c), own=True) for j, chip in enumerate(chips) for a in range(n)]
        for cp in local + first:
            cp.start()
        passed = []
        for j, chip in enumerate(chips):
            for a in range(n):
                copy(a, 1 + j, dev_id(*chip, c), sib).wait_recv()
                cp = copy(a, 4 + j, dev_id(*chip, c), sib)
                cp.start()
                passed.append(cp)
        for a in range(n):
            copy(a, 0, dev_id(x, y, 1 - c), sib).wait_recv()
        for j, chip in enumerate(chips):
            for a in range(n):
                copy(a, 4 + j, dev_id(*chip, 1 - c), sib).wait_recv()
        for cp in first + passed:
            cp.wait_send()
        for cp in local:
            cp.wait()

    anyspec = pl.BlockSpec(memory_space=pl.ANY)
    return pl.pallas_call(
        body, name=name, in_specs=[anyspec] * n, out_specs=[anyspec] * n,
        out_shape=[SDS((N_DEV,) + a.shape, a.dtype) for a in arrays],
        scratch_shapes=_exchange_sems(n))(*arrays)


def _all_reduce_small(v):
    R = v.shape[0]

    def body(v_ref, o_ref, gath, send_sems, recv_sems):
        x, y, c, me = _mesh_pos()
        gath[me] = v_ref[...]
        sends = []
        for m in range(1, N_DEV):
            peer, _ = _peer(x, y, c, m)
            cp = pltpu.make_async_remote_copy(src_ref=v_ref, dst_ref=gath.at[me], send_sem=send_sems.at[m - 1],
                                              recv_sem=recv_sems.at[m - 1], device_id=peer, device_id_type=MESH)
            cp.start()
            sends.append(cp)
        for m in range(1, N_DEV):
            peer, pid = _peer(x, y, c, m)
            pltpu.make_async_remote_copy(src_ref=v_ref, dst_ref=gath.at[pid], send_sem=send_sems.at[m - 1],
                                         recv_sem=recv_sems.at[m - 1], device_id=peer, device_id_type=MESH).wait_recv()
        for cp in sends:
            cp.wait_send()
        tot = gath[0]
        for d in range(1, N_DEV):
            tot = tot + gath[d]
        o_ref[...] = tot

    vm = pl.BlockSpec(memory_space=pltpu.VMEM)
    return pl.pallas_call(
        body, name="all_reduce_small", in_specs=[vm], out_specs=vm, out_shape=SDS((R, HP), f32),
        scratch_shapes=[pltpu.VMEM((N_DEV, R, HP), f32), pltpu.SemaphoreType.DMA((N_DEV - 1,)), pltpu.SemaphoreType.DMA((N_DEV - 1,))],
        )(v)


def _adamw_math(w, g, m, v):
    m = ADAM_B1 * m + (1.0 - ADAM_B1) * g
    v = ADAM_B2 * v + (1.0 - ADAM_B2) * (g * g)
    m_hat = m / (1.0 - ADAM_B1 ** ADAM_STEP)
    v_hat = v / (1.0 - ADAM_B2 ** ADAM_STEP)
    return -ADAM_LR * (m_hat / (jnp.sqrt(v_hat) + ADAM_EPS) + ADAM_WD * w), m, v


def _adamw(g, w, m, v, name):
    R, Cc = w.shape[-2:]
    if R <= 512 or R % 128 == 0:
        TR, TC = (R if R <= 512 else _pick(R, (256, 128))), Cc
    else:
        TR, TC = R, _pick(Cc, (256, 128))
    slabs = g.ndim == 3
    lead = w.ndim - 2

    def body(g_ref, w_ref, m_ref, v_ref, go_ref, d_ref, mo_ref, vo_ref):
        if slabs:
            gs = g_ref[0].astype(f32)
            for k in range(1, N_DEV):
                gs = gs + g_ref[k].astype(f32)
        else:
            gs = g_ref[...]
        at = 0 if lead else Ellipsis
        d, mn, vn = _adamw_math(w_ref[at], gs, m_ref[at], v_ref[at])
        go_ref[at], d_ref[at], mo_ref[at], vo_ref[at] = gs, d, mn, vn

    grid = (R // TR, Cc // TC)
    blk = pl.BlockSpec((1,) * lead + (TR, TC), lambda i, j: (0,) * lead + (i, j))
    gblk = pl.BlockSpec((N_DEV, TR, TC), lambda i, j: (0, i, j)) if slabs else pl.BlockSpec((TR, TC), lambda i, j: (i, j))
    return pl.pallas_call(
        body, name=name, grid=grid, in_specs=[gblk, blk, blk, blk], out_specs=[blk] * 4,
        out_shape=[SDS(w.shape, f32)] * 4, compiler_params=_cp(2))(g, w, m, v)


WEIGHTS = ("meta_tokens", "mix_norm_w", "w_in", "fox_forget_bias", "dn_conv_w", "dn_a_log", "dn_dt_bias", "dn_out_norm_w",
           "w_branch_fox", "w_branch_dn", "w_out", "ffn_norm_w", "w_ffn_gate", "w_ffn_up", "w_ffn_down", "final_norm_w")
COL_SHARDED = ("w_in", "w_branch_fox", "w_branch_dn", "w_ffn_gate", "w_ffn_up")
ROW_SHARDED = ("w_out", "w_ffn_down")
BIG = COL_SHARDED + ROW_SHARDED
LATE = tuple(n for n in BIG if n != "w_in")
LATE_MIX = ("w_branch_fox", "w_branch_dn", "w_out")
LATE_FFN = ("w_ffn_gate", "w_ffn_up", "w_ffn_down")
SMALL = tuple(n for n in WEIGHTS if n not in BIG)
TRANSPOSED = ("w_in", "w_ffn_gate", "w_ffn_up")


def _to_slabs(name, g):
    r, c = g.shape
    if name in COL_SHARDED and name not in TRANSPOSED:
        return _b(g.reshape(r, N_DEV, c // N_DEV).transpose(1, 0, 2))
    return _b(g.reshape(N_DEV, r // N_DEV, c))


def _from_slabs(name, s):
    n, r, c = s.shape
    if name in COL_SHARDED and name not in TRANSPOSED:
        return s.transpose(1, 0, 2).reshape(r, n * c)
    return s.reshape(n * r, c)


def kernel(x, meta_tokens, mix_norm_w, w_in, fox_forget_bias, dn_conv_w, dn_a_log, dn_dt_bias, dn_out_norm_w, w_branch_fox, w_branch_dn, w_out, ffn_norm_w, w_ffn_gate, w_ffn_up, w_ffn_down, final_norm_w, loss_target, m_meta_tokens, m_mix_norm_w, m_w_in, m_fox_forget_bias, m_dn_conv_w, m_dn_a_log, m_dn_dt_bias, m_dn_out_norm_w, m_w_branch_fox, m_w_branch_dn, m_w_out, m_ffn_norm_w, m_w_ffn_gate, m_w_ffn_up, m_w_ffn_down, m_final_norm_w, v_meta_tokens, v_mix_norm_w, v_w_in, v_fox_forget_bias, v_dn_conv_w, v_dn_a_log, v_dn_dt_bias, v_dn_out_norm_w, v_w_branch_fox, v_w_branch_dn, v_w_out, v_ffn_norm_w, v_w_ffn_gate, v_w_ffn_up, v_w_ffn_down, v_final_norm_w):
    w = dict(meta_tokens=meta_tokens, mix_norm_w=mix_norm_w, w_in=w_in, fox_forget_bias=fox_forget_bias, dn_conv_w=dn_conv_w, dn_a_log=dn_a_log, dn_dt_bias=dn_dt_bias, dn_out_norm_w=dn_out_norm_w, w_branch_fox=w_branch_fox, w_branch_dn=w_branch_dn, w_out=w_out, ffn_norm_w=ffn_norm_w, w_ffn_gate=w_ffn_gate, w_ffn_up=w_ffn_up, w_ffn_down=w_ffn_down, final_norm_w=final_norm_w)
    mom = dict(meta_tokens=m_meta_tokens, mix_norm_w=m_mix_norm_w, w_in=m_w_in, fox_forget_bias=m_fox_forget_bias, dn_conv_w=m_dn_conv_w, dn_a_log=m_dn_a_log, dn_dt_bias=m_dn_dt_bias, dn_out_norm_w=m_dn_out_norm_w, w_branch_fox=m_w_branch_fox, w_branch_dn=m_w_branch_dn, w_out=m_w_out, ffn_norm_w=m_ffn_norm_w, w_ffn_gate=m_w_ffn_gate, w_ffn_up=m_w_ffn_up, w_ffn_down=m_w_ffn_down, final_norm_w=m_final_norm_w)
    var = dict(meta_tokens=v_meta_tokens, mix_norm_w=v_mix_norm_w, w_in=v_w_in, fox_forget_bias=v_fox_forget_bias, dn_conv_w=v_dn_conv_w, dn_a_log=v_dn_a_log, dn_dt_bias=v_dn_dt_bias, dn_out_norm_w=v_dn_out_norm_w, w_branch_fox=v_w_branch_fox, w_branch_dn=v_w_branch_dn, w_out=v_w_out, ffn_norm_w=v_ffn_norm_w, w_ffn_gate=v_w_ffn_gate, w_ffn_up=v_w_ffn_up, w_ffn_down=v_w_ffn_down, final_norm_w=v_final_norm_w)
    two_d = lambda a: a.reshape(a.shape[-2:]) if a.ndim >= 2 else a.reshape(1, -1)
    me = 4 * lax.axis_index("x") + 2 * lax.axis_index("y") + lax.axis_index("c")
    for d in (w, mom, var):
        for n in TRANSPOSED:
            d[n] = jnp.swapaxes(d[n], -1, -2)

    g_in, g_meta, g_cw = _gather_two_level([_b(two_d(w["w_in"])), two_d(w["meta_tokens"]), two_d(w["dn_conv_w"])], "all_gather_early")
    meta = g_meta.transpose(1, 0, 2).reshape(N_META, -1)
    cw = g_cw.transpose(1, 0, 2).reshape(CONV_K, -1)

    gx, g_small, recv = _local_step(
        x[0], loss_target[0], meta, two_d(w["mix_norm_w"]), _from_slabs("w_in", g_in), two_d(w["fox_forget_bias"]), cw, two_d(w["dn_a_log"]),
        two_d(w["dn_dt_bias"]), two_d(w["dn_out_norm_w"]), two_d(w["ffn_norm_w"]), two_d(w["final_norm_w"]),
        {n: _b(two_d(w[n])) for n in LATE})

    order = ("loss",) + SMALL
    flat = jnp.concatenate([g_small[n].reshape(-1) for n in order])
    rows = -(-flat.shape[0] // (8 * HP)) * 8
    tot = _all_reduce_small(jnp.pad(flat, (0, rows * HP - flat.shape[0])).reshape(rows, HP)).reshape(-1)
    summed, o = {}, 0
    for n in order:
        k = g_small[n].shape[0]
        summed[n] = tot[o:o + k]
        o += k
    loss = summed["loss"][0]
    d_model = x.shape[-1]
    mcols, ccols = d_model // N_DEV, dn_conv_w.shape[-1]
    summed["meta_tokens"] = lax.dynamic_slice(summed["meta_tokens"].reshape(N_META, d_model), (0, me * mcols), (N_META, mcols)).reshape(-1)
    summed["dn_conv_w"] = lax.dynamic_slice(summed["dn_conv_w"].reshape(CONV_K, ccols * N_DEV), (0, me * ccols), (CONV_K, ccols)).reshape(-1)

    res = {}
    for n in BIG:
        res[n] = _adamw(recv[n], w[n], mom[n], var[n], "adamw_" + n)
        if n in TRANSPOSED:
            res[n] = [jnp.swapaxes(r, -1, -2) for r in res[n]]
    sizes = [summed[n].shape[0] for n in SMALL]
    srows = -(-sum(sizes) // (8 * HP)) * 8
    pack = lambda d: jnp.pad(jnp.concatenate([d[n].reshape(-1) for n in SMALL]), (0, srows * HP - sum(sizes))).reshape(srows, HP)
    sres = _adamw(pack(summed), pack(w), pack(mom), pack(var), "adamw_small")
    o = 0
    for n, k in zip(SMALL, sizes):
        res[n] = [r.reshape(-1)[o:o + k].reshape(w[n].shape) for r in sres]
        o += k
    return (loss, gx[None], *[res[n][0] for n in WEIGHTS], *[res[n][1] for n in WEIGHTS], *[res[n][2] for n in WEIGHTS], *[res[n][3] for n in WEIGHTS])
```

```python
import functools

import jax
import jax.numpy as jnp
from jax import lax
from jax.experimental import pallas as pl
from jax.experimental.pallas import tpu as pltpu

f32, bf16 = jnp.float32, jnp.bfloat16
HI = lax.Precision.HIGHEST
MESH = pl.DeviceIdType.MESH
SDS = jax.ShapeDtypeStruct

N_DEV = 8
N_META = 16
PREFIX = 128
N_PAD = PREFIX - N_META
FOX_H, FOX_D = 8, 64
DN_H, DN_D = 4, 128
DN_C = 64
CONV_K = 4
HP = 128
SMALL_W = 256
EPS = 1e-6
NEG = -1e30
C_Q0, C_K0 = 64, 67
LSE_COL = 64
LOG2E, LN2 = 1.4426950408889634, 0.6931471805599453
C_LSE0, C_DELTA0 = 70, 65

ADAM_LR, ADAM_B1, ADAM_B2, ADAM_EPS, ADAM_WD, ADAM_STEP = 0.001, 0.9, 0.999, 1e-08, 0.01, 10

VMEM_LIMIT_V7X = 56 * 1024 * 1024
ROW_TILES = (384, 128)
ATTN_TILES = (384, 128)
FFN_TILES = (192, 64)
FOX_HEAD_GROUP = 4
FOX_HEAD_GROUP_FWD = 8
MAX_WGRAD_BLOCK = 1408
DN_INTRA_GROUP = (6, 3, 2, 1)
DN_SCAN_GROUP = (6, 3, 2, 1)


def _pick(n, cands):
    for c in cands:
        if n % c == 0:
            return c
    raise ValueError(f"no tile of {cands} divides {n}")


def _cp(n_axes=1):
    return pltpu.CompilerParams(dimension_semantics=("arbitrary",) * n_axes, vmem_limit_bytes=VMEM_LIMIT_V7X)


def _b(x):
    return x.astype(bf16)


def _dot(a, b):
    return jnp.dot(a, b, preferred_element_type=f32)


def _dot_nt(a, b):
    return lax.dot_general(a, b, (((1,), (1,)), ((), ())), preferred_element_type=f32)


def _dot_tn(a, b):
    return lax.dot_general(a, b, (((0,), (0,)), ((), ())), preferred_element_type=f32)


def _dot_hi(a, b):
    return jnp.dot(a, b, preferred_element_type=f32, precision=HI)


def _iota(shape, dim):
    return lax.broadcasted_iota(jnp.int32, shape, dim)


def _rms(x, w):
    return x * lax.rsqrt(jnp.mean(x * x, axis=-1, keepdims=True) + EPS) * w


def _sigmoid(x):
    return jax.nn.sigmoid(x)


def _load_once(pairs, sems):
    @pl.when(pl.program_id(0) == 0)
    def _():
        cps = [pltpu.make_async_copy(src, dst, sems.at[k]) for k, (src, dst) in enumerate(pairs)]
        for cp in cps:
            cp.start()
        for cp in cps:
            cp.wait()


def _seg_layout(d_model):
    return (("fq", FOX_H * HP, bf16, "fox"), ("fk", FOX_H * HP, bf16, "fox"), ("sf", SMALL_W, f32, "fox"),
            ("fv", FOX_H * HP, bf16, "fv"),
            ("dn", 3 * DN_H * DN_D, f32, "dn"), ("sd", SMALL_W, f32, "dn"),
            ("dz", DN_H * DN_D, f32, "mix"), ("ga", d_model, f32, "mix"), ("gb", d_model, f32, "mix"))


GROUPS = ("fox", "fv", "dn", "mix")


def _group_widths(d_model):
    return [sum(wd for _, wd, _, g in _seg_layout(d_model) if g == grp) for grp in GROUPS]


def _in_proj(h0, w1, wpt, shards):
    L, D = h0.shape
    NP = wpt.shape[0]
    TM = _pick(L, ROW_TILES)
    NT = L // TM
    segs = _seg_layout(D)
    ns, n = len(segs), len(shards)
    offs, o = [], 0
    for _, wd, _, _ in segs:
        offs.append(o)
        o += wd
    assert o == NP

    def body(h_ref, w1_ref, wp_hbm, *rest):
        ins, xn_ref, outs, gouts = rest[:n], rest[n], rest[n + 1:n + 1 + ns], rest[n + 1 + ns:2 * n + 1 + ns]
        wp_v, sems = rest[2 * n + 1 + ns:2 * n + 3 + ns]
        xsems = rest[2 * n + 3 + ns:]
        _load_once([(wp_hbm, wp_v)], sems)

        _exchange_start(ins, gouts, *xsems, gather=True, when=pl.program_id(0) == 0)

        xn = _b(_rms(h_ref[...], w1_ref[...]))
        xn_ref[...] = xn
        for o_ref, off, (_, wd, _, _) in zip(outs, offs, segs):
            o_ref[...] = _dot_nt(xn, wp_v[off:off + wd, :]).astype(o_ref.dtype)

        _exchange_wait(ins, gouts, *xsems, gather=True, when=pl.program_id(0) == NT - 1)

    row = lambda wd: pl.BlockSpec((TM, wd), lambda i: (i, 0))
    anyspec = pl.BlockSpec(memory_space=pl.ANY)
    res = pl.pallas_call(
        body, name="in_proj", grid=(NT,),
        in_specs=[row(D), pl.BlockSpec((1, D), lambda i: (0, 0)), anyspec] + [anyspec] * n,
        out_specs=[row(D)] + [row(wd) for _, wd, _, _ in segs] + [anyspec] * n,
        out_shape=[SDS((L, D), bf16)] + [SDS((L, wd), dt) for _, wd, dt, _ in segs] + [SDS((N_DEV,) + a.shape, a.dtype) for a in shards],
        scratch_shapes=[pltpu.VMEM((NP, D), bf16), pltpu.SemaphoreType.DMA((1,))] + _exchange_sems(n),
        compiler_params=_cp())(h0, w1, wpt, *shards)
    return res[:1 + ns], res[1 + ns:]


def _in_proj_bwd(dgroups, wpt, h0, w1, dh1, slabs):
    L, D = h0.shape
    NP = wpt.shape[0]
    TM = _pick(L, ROW_TILES)
    NT = L // TM
    widths = [g.shape[1] for g in dgroups]
    assert sum(widths) == NP
    ng, n = len(dgroups), len(slabs)

    def body(*refs):
        dg_refs, (wp_hbm, h_ref, w1_ref, dh1_ref) = refs[:ng], refs[ng:ng + 4]
        ins, (dh0_ref, acc_ref), outs = refs[ng + 4:ng + 4 + n], refs[ng + 4 + n:ng + 6 + n], refs[ng + 6 + n:ng + 6 + 2 * n]
        wp_v, sems = refs[ng + 6 + 2 * n:ng + 8 + 2 * n]
        xsems = refs[ng + 8 + 2 * n:]
        _load_once([(wp_hbm, wp_v)], sems)

        @pl.when(pl.program_id(0) == 0)
        def _():
            acc_ref[...] = jnp.zeros_like(acc_ref)

        _exchange_start(ins, outs, *xsems, gather=False, when=pl.program_id(0) == 0)

        dxn, off = None, 0
        for g_ref, wd in zip(dg_refs, widths):
            part = _dot(g_ref[...], wp_v[off:off + wd, :])
            dxn = part if dxn is None else dxn + part
            off += wd
        _, vjp = jax.vjp(_rms, h_ref[...], w1_ref[...])
        dh0n, dw1 = vjp(dxn)
        dh0_ref[...] = dh1_ref[...] + dh0n
        acc_ref[0:1, :] += dw1

        _exchange_wait(ins, outs, *xsems, gather=False, when=pl.program_id(0) == NT - 1)

    row = lambda wd: pl.BlockSpec((TM, wd), lambda i: (i, 0))
    anyspec = pl.BlockSpec(memory_space=pl.ANY)
    res = pl.pallas_call(
        body, name="in_proj_bwd", grid=(NT,),
        in_specs=[row(wd) for wd in widths] + [anyspec, row(D), pl.BlockSpec((1, D), lambda i: (0, 0)), row(D)] + [anyspec] * n,
        out_specs=[row(D), pl.BlockSpec((8, D), lambda i: (0, 0))] + [anyspec] * n,
        out_shape=[SDS((L, D), f32), SDS((8, D), f32)] + [SDS(a.shape, a.dtype) for a in slabs],
        scratch_shapes=[pltpu.VMEM((NP, D), bf16), pltpu.SemaphoreType.DMA((1,))] + _exchange_sems(n),
        compiler_params=_cp())(*dgroups, wpt, h0, w1, dh1, *slabs)
    return res[0], res[1], res[2:]


def _matmul_tn(a, b, name):
    L, R = a.shape
    C = b.shape[1]
    br = max(k for k in range(HP, MAX_WGRAD_BLOCK + 1, HP) if R % k == 0)

    def body(a_ref, b_ref, o_ref):
        o_ref[...] = _b(_dot_tn(a_ref[...], b_ref[...]))

    return pl.pallas_call(
        body, name=name, grid=(R // br,),
        in_specs=[pl.BlockSpec((L, br), lambda r: (0, r)), pl.BlockSpec((L, C), lambda r: (0, 0))],
        out_specs=pl.BlockSpec((br, C), lambda r: (r, 0)), out_shape=SDS((R, C), bf16), compiler_params=_cp())(a, b)


def _fox_prep(fq, fk, fv, sf, bias_p):
    L = fq.shape[0]
    T = HP
    NT = L // T
    W = FOX_H * HP

    def body(fq_ref, fk_ref, fv_ref, sf_ref, b_ref, qa_ref, ka_ref, va_ref, carry):
        @pl.when(pl.program_id(0) == 0)
        def _():
            carry[...] = jnp.zeros_like(carry)

        lane, row = _iota((T, HP), 1), _iota((T, HP), 0)
        logf = jnp.where(lane < FOX_H, jax.nn.log_sigmoid(sf_ref[...] + b_ref[...]), 0.0)
        c = _dot_hi((row >= lane).astype(f32), logf) + carry[...]
        carry[...] = jnp.sum(jnp.where(row == T - 1, c, 0.0), axis=0, keepdims=True)
        ones_q = jnp.where((lane >= C_K0) & (lane < C_K0 + 3), 1.0, 0.0)
        ones_k = jnp.where(((lane >= C_Q0) & (lane < C_Q0 + 3)) | ((lane >= C_LSE0) & (lane < C_LSE0 + 3)), 1.0, 0.0)
        ones_v = _b(jnp.where((lane >= LSE_COL) & (lane < C_DELTA0 + 3), 1.0, 0.0))
        for h in range(FOX_H):
            ch = jnp.broadcast_to(jnp.sum(jnp.where(lane == h, c, 0.0), axis=1, keepdims=True), (T, HP)) * LOG2E
            c1 = _b(ch).astype(f32)
            c2 = _b(ch - c1).astype(f32)
            c3 = _b(ch - c1 - c2).astype(f32)
            cq = jnp.where(lane == C_Q0, c1, 0.0) + jnp.where(lane == C_Q0 + 1, c2, 0.0) + jnp.where(lane == C_Q0 + 2, c3, 0.0)
            ck = jnp.where(lane == C_K0, c1, 0.0) + jnp.where(lane == C_K0 + 1, c2, 0.0) + jnp.where(lane == C_K0 + 2, c3, 0.0)
            q = fq_ref[:, h * HP:(h + 1) * HP].astype(f32) * (FOX_D ** -0.5 * LOG2E)
            k = fk_ref[:, h * HP:(h + 1) * HP].astype(f32)
            qa_ref[h] = _b(q + cq + ones_q)
            ka_ref[h] = _b(k + ones_k - ck)
            va_ref[:, h * HP:(h + 1) * HP] = fv_ref[:, h * HP:(h + 1) * HP] + ones_v

    wide = pl.BlockSpec((T, W), lambda i: (i, 0))
    return pl.pallas_call(
        body, name="fox_prep", grid=(NT,),
        in_specs=[wide, wide, wide, pl.BlockSpec((T, HP), lambda i: (i, 0)), pl.BlockSpec((1, HP), lambda i: (0, 0))],
        out_specs=[pl.BlockSpec((FOX_H, T, HP), lambda i: (0, i, 0))] * 2 + [wide],
        out_shape=[SDS((FOX_H, L, HP), bf16)] * 2 + [SDS((L, W), bf16)],
        scratch_shapes=[pltpu.VMEM((1, HP), f32)], compiler_params=_cp())(fq, fk, fv, sf, bias_p)


def _fox_prep_bwd(dqa, dka, sf, bias_p):
    L = sf.shape[0]
    T = HP
    NT = L // T
    rev = lambda i: (NT - 1 - i, 0)

    W = FOX_H * HP

    def body(dq_ref, dk_ref, sf_ref, b_ref, dg_ref, db_ref, carry):
        @pl.when(pl.program_id(0) == 0)
        def _():
            carry[...] = jnp.zeros_like(carry)
            db_ref[...] = jnp.zeros_like(db_ref)

        dq, dk = dq_ref[...], dk_ref[...]
        dg_ref[:, 0:W] = _b(dq * (FOX_D ** -0.5))
        dg_ref[:, W:2 * W] = _b(dk * LN2)
        lane, row = _iota((T, HP), 1), _iota((T, HP), 0)
        dc = jnp.zeros((T, HP), f32)
        for h in range(FOX_H):
            col = jnp.sum(jnp.where(lane == C_Q0, dq[:, h * HP:(h + 1) * HP], 0.0)
                          - jnp.where(lane == C_K0, dk[:, h * HP:(h + 1) * HP], 0.0), axis=1, keepdims=True)
            dc = dc + jnp.where(lane == h, col, 0.0)
        dl = _dot_hi((row <= lane).astype(f32), dc) + carry[...]
        carry[...] = jnp.sum(jnp.where(row == 0, dl, 0.0), axis=0, keepdims=True)
        dx = jnp.where(lane < FOX_H, dl * _sigmoid(-(sf_ref[...] + b_ref[...])), 0.0)
        dg_ref[:, 2 * W:2 * W + HP] = _b(dx)
        dg_ref[:, 2 * W + HP:] = jnp.zeros((T, SMALL_W - HP), bf16)
        db_ref[0:1, :] += jnp.sum(dx, axis=0, keepdims=True)

    return pl.pallas_call(
        body, name="fox_prep_bwd", grid=(NT,),
        in_specs=[pl.BlockSpec((T, W), rev), pl.BlockSpec((T, W), rev), pl.BlockSpec((T, HP), rev), pl.BlockSpec((1, HP), lambda i: (0, 0))],
        out_specs=[pl.BlockSpec((T, 2 * W + SMALL_W), rev), pl.BlockSpec((8, HP), lambda i: (0, 0))],
        out_shape=[SDS((L, 2 * W + SMALL_W), bf16), SDS((8, HP), f32)],
        scratch_shapes=[pltpu.VMEM((1, HP), f32)], compiler_params=_cp())(dqa, dka, sf, bias_p)


def _tile_start(j, T):
    return j * T if isinstance(j, int) else pl.multiple_of(j * T, T)


def _spread3(x, lane, col0):
    x1 = _b(x).astype(f32)
    x2 = _b(x - x1).astype(f32)
    x3 = _b(x - x1 - x2).astype(f32)
    return jnp.where(lane == col0, x1, 0.0) + jnp.where(lane == col0 + 1, x2, 0.0) + jnp.where(lane == col0 + 2, x3, 0.0)


def _fox_fwd(qa, ka, fv, shards):
    L = qa.shape[1]
    TQ = TK = _pick(L, ATTN_TILES)
    NQ = L // TQ
    n = len(shards)
    HG = FOX_HEAD_GROUP_FWD

    def body(q_ref, k_ref, v_ref, *rest):
        ins, o_ref, qb_ref, outs, sems = rest[:n], rest[n], rest[n + 1], rest[n + 2:2 * n + 2], rest[2 * n + 2:]
        h, i = pl.program_id(0), pl.program_id(1)

        _exchange_start(ins, outs, *sems, gather=True, when=(h == 0) & (i == 0))

        qs = [q_ref[a] for a in range(HG)]
        rowg = i * TQ + _iota((TQ, TK), 0)
        colb = _iota((TQ, TK), 1)

        def step(j, carry, masked):
            ms, accs = carry
            k0 = _tile_start(j, TK)
            ss = [_dot_nt(qs[a], k_ref[a, pl.ds(k0, TK), :]) for a in range(HG)]
            if masked:
                colg = colb + j * TK
                keep = (colg <= rowg) & (colg >= N_PAD)
                ss = [jnp.where(keep, s, NEG) for s in ss]
            m_new = [jnp.maximum(m, jnp.max(s, axis=1, keepdims=True)) for m, s in zip(ms, ss)]
            ps = [_b(jnp.exp2(s - m)) for s, m in zip(ss, m_new)]
            alphas = [jnp.exp2(m - mn) for m, mn in zip(ms, m_new)]
            accs = [al * acc + _dot(p, v_ref[pl.ds(k0, TK), a * HP:(a + 1) * HP]) for a, (al, acc, p) in enumerate(zip(alphas, accs, ps))]
            return m_new, accs

        init = ([jnp.full((TQ, 1), NEG, f32)] * HG, [jnp.zeros((TQ, HP), f32)] * HG)
        carry = step(0, init, True)
        carry = lax.fori_loop(1, i, functools.partial(step, masked=False), carry)
        ms, accs = lax.fori_loop(jnp.maximum(i, 1), i + 1, functools.partial(step, masked=True), carry)
        lane = _iota((TQ, HP), 1)
        for a in range(HG):
            l = jnp.sum(jnp.where(lane == LSE_COL, accs[a], 0.0), axis=1, keepdims=True)
            lse = ms[a] + jnp.log2(l)
            o_ref[:, a * HP:(a + 1) * HP] = jnp.where(lane == LSE_COL, lse, accs[a] / l)
            qb_ref[a] = _b(qs[a].astype(f32) - _spread3(jnp.broadcast_to(lse, (TQ, HP)), lane, C_LSE0))

        _exchange_wait(ins, outs, *sems, gather=True, when=(h == FOX_H // HG - 1) & (i == NQ - 1))

    anyspec = pl.BlockSpec(memory_space=pl.ANY)
    qtile = pl.BlockSpec((HG, TQ, HP), lambda h, i: (h, i, 0))
    res = pl.pallas_call(
        body, name="fox_fwd", grid=(FOX_H // HG, NQ),
        in_specs=[qtile, pl.BlockSpec((HG, L, HP), lambda h, i: (h, 0, 0)), pl.BlockSpec((L, HG * HP), lambda h, i: (0, h))] + [anyspec] * n,
        out_specs=[pl.BlockSpec((TQ, HG * HP), lambda h, i: (i, h)), qtile] + [anyspec] * n,
        out_shape=[SDS((L, FOX_H * HP), f32), SDS(qa.shape, bf16)] + [SDS((N_DEV,) + a.shape, a.dtype) for a in shards],
        scratch_shapes=_exchange_sems(n), compiler_params=_cp(2))(qa, ka, fv, *shards)
    return res[0], res[1], res[2:]


def _fox_bwd(qb, ka, va, dob, slabs):
    L = qb.shape[1]
    TQ = TK = _pick(L, ATTN_TILES)
    NQ = L // TQ
    n = len(slabs)
    HG = FOX_HEAD_GROUP

    def body(q_ref, k_ref, v_ref, do_ref, *rest):
        ins, (dq_ref, dk_ref, dv_ref), outs, sems = rest[:n], rest[n:n + 3], rest[n + 3:2 * n + 3], rest[2 * n + 3:]
        h, j = pl.program_id(0), pl.program_id(1)
        cols = [slice(a * HP, (a + 1) * HP) for a in range(HG)]

        _exchange_start(ins, outs, *sems, gather=False, when=(h == 0) & (j == 0))

        @pl.when(j == 0)
        def _():
            dq_ref[...] = jnp.zeros_like(dq_ref)

        kts = [k_ref[a] for a in range(HG)]
        vts = [v_ref[:, cols[a]] for a in range(HG)]
        colg = j * TK + _iota((TQ, TK), 1)
        rowb = _iota((TQ, TK), 0)

        def step(i, carry, masked):
            dks, dvs = carry
            r0 = _tile_start(i, TQ)
            rows = pl.ds(r0, TQ)
            qs = [q_ref[a, rows, :] for a in range(HG)]
            ps = [jnp.exp2(_dot_nt(q, kt)) for q, kt in zip(qs, kts)]
            if masked:
                keep = (colg <= rowb + i * TQ) & (colg >= N_PAD)
                ps = [jnp.where(keep, p, 0.0) for p in ps]
            dobs = [do_ref[rows, cols[a]] for a in range(HG)]
            dvs = [dv + _dot_tn(_b(p), dob) for dv, p, dob in zip(dvs, ps, dobs)]
            dss = [_b(p * _dot_nt(dob, vt)) for p, dob, vt in zip(ps, dobs, vts)]
            for a in range(HG):
                dq_ref[rows, cols[a]] += _dot(dss[a], kts[a])
            dks = [dk + _dot_tn(ds, q) for dk, ds, q in zip(dks, dss, qs)]
            return dks, dvs

        zeros = [jnp.zeros((TK, HP), f32)] * HG
        carry = step(j, (zeros, zeros), True)
        split = jnp.where(j == 0, NQ, j + 1)
        carry = lax.fori_loop(j + 1, split, functools.partial(step, masked=True), carry)
        dks, dvs = lax.fori_loop(split, NQ, functools.partial(step, masked=False), carry)
        for a in range(HG):
            dk_ref[:, cols[a]] = dks[a]
            dv_ref[:, cols[a]] = _b(dvs[a])

        _exchange_wait(ins, outs, *sems, gather=False, when=(h == FOX_H // HG - 1) & (j == NQ - 1))

    head = pl.BlockSpec((L, HG * HP), lambda h, j: (0, h))
    tile = pl.BlockSpec((TK, HG * HP), lambda h, j: (j, h))
    anyspec = pl.BlockSpec(memory_space=pl.ANY)
    res = pl.pallas_call(
        body, name="fox_bwd", grid=(FOX_H // HG, L // TK),
        in_specs=[pl.BlockSpec((HG, L, HP), lambda h, j: (h, 0, 0)), pl.BlockSpec((HG, TK, HP), lambda h, j: (h, j, 0)), tile, head]
        + [anyspec] * n,
        out_specs=[head, tile, tile] + [anyspec] * n,
        out_shape=[SDS((L, FOX_H * HP), f32), SDS((L, FOX_H * HP), f32), SDS((L, FOX_H * HP), bf16)] + [SDS(a.shape, a.dtype) for a in slabs],
        scratch_shapes=_exchange_sems(n), compiler_params=_cp(2))(qb, ka, va, dob, *slabs)
    return res[:3], res[3:]


def _dn_post(y, sd, alog_p, dt_p, valid):
    a = y * _sigmoid(y)
    W = DN_H * DN_D
    heads = []
    for part, scale in ((0, DN_D ** -0.5), (1, 1.0)):
        for h in range(DN_H):
            xh = a[:, part * W + h * DN_D:part * W + (h + 1) * DN_D]
            heads.append(xh * lax.rsqrt(jnp.sum(xh * xh, axis=-1, keepdims=True) + EPS) * scale)
    q = jnp.concatenate(heads[:DN_H], axis=1)
    k = jnp.concatenate(heads[DN_H:], axis=1)
    v = a[:, 2 * W:3 * W]
    lane = _iota(sd.shape, 1)
    beta = _sigmoid(sd) * valid
    g = -jnp.exp(alog_p) * jax.nn.softplus(sd + dt_p) * valid
    bg = jnp.where(lane < DN_H, beta, jnp.where(lane < 2 * DN_H, g, 0.0))
    return q, k, v, bg


def _conv_fwd(ext_ref, cw_ref, TM):
    y = cw_ref[0:1, :] * ext_ref[8 - (CONV_K - 1):8 - (CONV_K - 1) + TM, :]
    for i in range(1, CONV_K):
        o = 8 - (CONV_K - 1) + i
        y = y + cw_ref[i:i + 1, :] * ext_ref[o:o + TM, :]
    return y


def _dn_prep(dn, sd, cw, alog_p, dt_p):
    L, W3 = dn.shape
    TM = _pick(L, ROW_TILES)
    W = DN_H * DN_D

    def body(dn_ref, halo_ref, sd_ref, cw_ref, al_ref, dt_ref, q_ref, k_ref, v_ref, bg_ref, ext):
        i = pl.program_id(0)
        ext[0:8, :] = jnp.where(i == 0, 0.0, halo_ref[...])
        ext[8:, :] = dn_ref[...]
        y = _conv_fwd(ext, cw_ref, TM)
        valid = ((i * TM + _iota((TM, 1), 0)) >= N_PAD).astype(f32)
        q, k, v, bg = _dn_post(y, sd_ref[...], al_ref[...], dt_ref[...], valid)
        q_ref[...], k_ref[...], v_ref[...], bg_ref[...] = q, k, v, bg

    row = lambda wd: pl.BlockSpec((TM, wd), lambda i: (i, 0))
    vec = pl.BlockSpec((1, HP), lambda i: (0, 0))
    return pl.pallas_call(
        body, name="dn_prep", grid=(L // TM,),
        in_specs=[row(W3), pl.BlockSpec((8, W3), lambda i: (jnp.maximum(i * (TM // 8) - 1, 0), 0)), row(HP),
                  pl.BlockSpec((CONV_K, W3), lambda i: (0, 0)), vec, vec],
        out_specs=[row(W), row(W), row(W), row(HP)],
        out_shape=[SDS((L, W), f32)] * 3 + [SDS((L, HP), f32)],
        scratch_shapes=[pltpu.VMEM((TM + 8, W3), f32)], compiler_params=_cp())(dn, dn, sd, cw, alog_p, dt_p)


def _dn_prep_bwd(dn, sd, cw, alog_p, dt_p, dq, dk, dv, dbg):
    L, W3 = dn.shape
    TM = _pick(L, ROW_TILES)
    NT = L // TM
    W = DN_H * DN_D

    def body(dn_ref, halo_ref, sd_ref, cw_ref, al_ref, dt_ref, dq_ref, dk_ref, dv_ref, dbg_ref,
             dg_ref, dcw_ref, dp_ref, ext, dyp, carry):
        i = pl.program_id(0)
        t = NT - 1 - i

        @pl.when(i == 0)
        def _():
            carry[...] = jnp.zeros_like(carry)
            dcw_ref[...] = jnp.zeros_like(dcw_ref)
            dp_ref[...] = jnp.zeros_like(dp_ref)
            dyp[...] = jnp.zeros_like(dyp)

        ext[0:8, :] = jnp.where(t == 0, 0.0, halo_ref[...])
        ext[8:, :] = dn_ref[...]
        y = _conv_fwd(ext, cw_ref, TM)
        valid = ((t * TM + _iota((TM, 1), 0)) >= N_PAD).astype(f32)
        _, vjp = jax.vjp(functools.partial(_dn_post, valid=valid), y, sd_ref[...], al_ref[...], dt_ref[...])
        dy, dsd, dal, ddt = vjp((dq_ref[...], dk_ref[...], dv_ref[...], dbg_ref[...]))
        dg_ref[:, W3:W3 + HP] = _b(dsd)
        dg_ref[:, W3 + HP:] = jnp.zeros((TM, SMALL_W - HP), bf16)
        dp_ref[0:1, :] += dal
        dp_ref[1:2, :] += ddt
        dyp[8:8 + TM, :] = dy
        o0 = CONV_K - 1
        dext = cw_ref[0:1, :] * dyp[o0:o0 + TM + 8, :]
        for k in range(1, CONV_K):
            dext = dext + cw_ref[k:k + 1, :] * dyp[o0 - k:o0 - k + TM + 8, :]
        for k in range(CONV_K):
            o = 8 - (CONV_K - 1) + k
            dcw_ref[k:k + 1, :] += jnp.sum(dy * ext[o:o + TM, :], axis=0, keepdims=True)
        dg_ref[:, 0:W3] = _b(jnp.concatenate([dext[8:TM, :], dext[TM:TM + 8, :] + carry[...]], axis=0))
        carry[...] = dext[0:8, :]

    row = lambda wd: pl.BlockSpec((TM, wd), lambda i: (NT - 1 - i, 0))
    vec = pl.BlockSpec((1, HP), lambda i: (0, 0))
    return pl.pallas_call(
        body, name="dn_prep_bwd", grid=(NT,),
        in_specs=[row(W3), pl.BlockSpec((8, W3), lambda i: (jnp.maximum((NT - 1 - i) * (TM // 8) - 1, 0), 0)), row(HP),
                  pl.BlockSpec((CONV_K, W3), lambda i: (0, 0)), vec, vec, row(W), row(W), row(W), row(HP)],
        out_specs=[row(W3 + SMALL_W), pl.BlockSpec((8, W3), lambda i: (0, 0)), pl.BlockSpec((8, HP), lambda i: (0, 0))],
        out_shape=[SDS((L, W3 + SMALL_W), bf16), SDS((8, W3), f32), SDS((8, HP), f32)],
        scratch_shapes=[pltpu.VMEM((TM + 8, W3), f32), pltpu.VMEM((TM + 16, W3), f32), pltpu.VMEM((8, W3), f32)],
        compiler_params=_cp())(dn, dn, sd, cw, alog_p, dt_p, dq, dk, dv, dbg)


def _split2(x):
    hi = _b(x)
    return hi, _b(x - hi.astype(f32))


def _split3(x):
    hi = _b(x)
    r = x - hi.astype(f32)
    mid = _b(r)
    return hi, mid, _b(r - mid.astype(f32))


def _x3(a, b, dot):
    (a1, a2), (b1, b2) = _split2(a), _split2(b)
    return dot(a1, b1) + (dot(a1, b2) + dot(a2, b1))


@jax.custom_vjp
def _dot_x3(a, b):
    return _x3(a, b, _dot)


_dot_x3.defvjp(lambda a, b: (_x3(a, b, _dot), (a, b)), lambda res, g: (_x3(g, res[1], _dot_nt), _x3(res[0], g, _dot_tn)))


def _exact3(m, x, dot):
    x1, x2, x3 = _split3(x)
    return dot(m, x1) + (dot(m, x2) + dot(m, x3))


def _tri_ones(C, lower):
    row, col = _iota((C, C), 0), _iota((C, C), 1)
    return _b(((row >= col) if lower else (row <= col)).astype(f32))


@jax.custom_vjp
def _chunk_cumsum(x):
    return _exact3(_tri_ones(x.shape[0], True), x, _dot)


_chunk_cumsum.defvjp(lambda x: (_exact3(_tri_ones(x.shape[0], True), x, _dot), None),
                     lambda _, g: (_exact3(_tri_ones(g.shape[0], False), g, _dot),))


def _mxu_transpose(x):
    C = x.shape[0]
    eye = _b((_iota((C, C), 0) == _iota((C, C), 1)).astype(f32))
    return _exact3(eye, x, lambda m, part: _dot_tn(part, m))


@jax.custom_vjp
def _transpose_exact(x):
    return _mxu_transpose(x)


_transpose_exact.defvjp(lambda x: (_mxu_transpose(x), None), lambda _, g: (_mxu_transpose(g),))


def _unit_lower_inverses(lows):
    C = lows[0].shape[0]
    P = jnp.stack(lows)
    X = (_iota((C, C), 0) == _iota((C, C), 1)).astype(f32)[None] - P
    bdot = functools.partial(_x3, dot=lambda a, b: jnp.einsum("bij,bjk->bik", a, b, preferred_element_type=f32))
    for _ in range(5):
        P = bdot(P, P)
        X = X + bdot(X, P)
    return [X[i] for i in range(len(lows))]


@jax.custom_vjp
def _inverse_given(low, X):
    return X


def _inverse_given_bwd(X, g):
    return -_x3(_x3(X, g, _dot_tn), X, _dot_nt), jnp.zeros_like(X)


_inverse_given.defvjp(lambda low, X: (X, X), _inverse_given_bwd)


def _dn_intra_pre(q, k, v, bg):
    C = DN_C
    row, col = _iota((C, C), 0), _iota((C, C), 1)
    tri = row >= col
    G = _chunk_cumsum(bg)
    GT = _transpose_exact(G)
    lane = _iota((C, HP), 1)
    rowt = _iota((HP, C), 0)
    last = _iota((C, 1), 0) == C - 1
    heads = []
    for h in range(DN_H):
        beta = jnp.sum(jnp.where(lane == h, bg, 0.0), axis=1, keepdims=True)
        gcol = jnp.sum(jnp.where(lane == DN_H + h, G, 0.0), axis=1, keepdims=True)
        grow = jnp.sum(jnp.where(rowt == DN_H + h, GT, 0.0), axis=0, keepdims=True)
        glast = jnp.sum(jnp.where(last, gcol, 0.0), axis=0, keepdims=True)
        decay = jnp.exp(jnp.where(tri, gcol - grow, NEG))
        qh, kh, vh = (t[:, h * DN_D:(h + 1) * DN_D] for t in (q, k, v))
        kb = kh * beta
        low = jnp.where(row > col, _dot_nt(_b(kb), _b(kh)) * decay, 0.0)
        heads.append((beta, gcol, glast, decay, qh, kh, vh, kb, low))
    return heads


def _dn_intra_post(heads, xs):
    lane1 = _iota((1, HP), 1)
    us, ws, qds, kds, attns = [], [], [], [], []
    glrow = jnp.zeros((1, HP), f32)
    for h, ((beta, gcol, glast, decay, qh, kh, vh, kb, _), X) in enumerate(zip(heads, xs)):
        eg = jnp.exp(gcol)
        us.append(_dot_x3(X, vh * beta))
        ws.append(_dot_x3(X, kb * eg))
        attns.append(_dot_nt(_b(qh), _b(kh)) * decay)
        qds.append(qh * eg)
        kds.append(kh * jnp.exp(glast - gcol))
        glrow = glrow + jnp.where(lane1 == h, glast, 0.0)
    cat = lambda xs_: jnp.concatenate(xs_, axis=1)
    return cat(us), cat(ws), cat(qds), cat(kds), cat(attns), glrow, cat(list(xs))


def _dn_intra_group(q, k, v, bg, xs):
    G = q.shape[0] // DN_C
    rows = [slice(j * DN_C, (j + 1) * DN_C) for j in range(G)]
    pre = [_dn_intra_pre(q[r, :], k[r, :], v[r, :], bg[r, :]) for r in rows]
    inv = [[_inverse_given(hd[-1], x) for hd, x in zip(heads, xj)] for heads, xj in zip(pre, xs)]
    post = [_dn_intra_post(heads, xj) for heads, xj in zip(pre, inv)]
    return tuple(jnp.concatenate([p[i] for p in post], axis=0) for i in range(5)) + (tuple(p[5] for p in post),)


def _lane_pick(rowvec, h):
    return jnp.sum(jnp.where(_iota(rowvec.shape, 1) == h, rowvec, 0.0), axis=1, keepdims=True)


def _dn_intra(q, k, v, bg, shards):
    L, W = q.shape
    NC = L // DN_C
    G = _pick(NC, DN_INTRA_GROUP)
    R = G * DN_C
    NS = NC // G
    WA = DN_H * DN_C
    n = len(shards)

    def body(q_ref, k_ref, v_ref, bg_ref, *rest):
        ins, (u_ref, w_ref, qd_ref, kd_ref, at_ref, gl_ref, x_ref), gouts, sems = rest[:n], rest[n:n + 7], rest[n + 7:2 * n + 7], rest[2 * n + 7:]
        _exchange_start(ins, gouts, *sems, gather=True, when=pl.program_id(0) == 0)
        rows = [slice(j * DN_C, (j + 1) * DN_C) for j in range(G)]
        pre = [_dn_intra_pre(q_ref[r, :], k_ref[r, :], v_ref[r, :], bg_ref[r, :]) for r in rows]
        inv = _unit_lower_inverses([hd[-1] for heads in pre for hd in heads])
        for j, r in enumerate(rows):
            u, w, qd, kd, at, gl, xs = _dn_intra_post(pre[j], inv[j * DN_H:(j + 1) * DN_H])
            u_ref[r, :], x_ref[r, :] = u, xs
            w_ref[r, :], qd_ref[r, :], kd_ref[r, :], at_ref[r, :] = _b(w), _b(qd), _b(kd), _b(at)
            gl_ref[j] = gl
        _exchange_wait(ins, gouts, *sems, gather=True, when=pl.program_id(0) == NS - 1)

    row = lambda wd: pl.BlockSpec((R, wd), lambda s: (s, 0))
    anyspec = pl.BlockSpec(memory_space=pl.ANY)
    res = pl.pallas_call(
        body, name="dn_intra", grid=(NS,),
        in_specs=[row(W), row(W), row(W), row(HP)] + [anyspec] * n,
        out_specs=[row(W), row(W), row(W), row(W), row(WA), pl.BlockSpec((G, 1, HP), lambda s: (s, 0, 0)), row(WA)] + [anyspec] * n,
        out_shape=[SDS((L, W), f32), SDS((L, W), bf16), SDS((L, W), bf16), SDS((L, W), bf16), SDS((L, WA), bf16), SDS((NC, 1, HP), f32),
                   SDS((L, WA), f32)] + [SDS((N_DEV,) + a.shape, a.dtype) for a in shards],
        scratch_shapes=_exchange_sems(n), compiler_params=_cp())(q, k, v, bg, *shards)
    return res[:7], res[7:]


def _dn_scan(u, w, qd, kd, at, gl):
    L, W = u.shape
    NC = L // DN_C
    G = _pick(NC, DN_SCAN_GROUP)
    R = G * DN_C

    def body(u_ref, w_ref, qd_ref, kd_ref, at_ref, gl_ref, o_ref, vn_ref, s_ref, S):
        @pl.when(pl.program_id(0) == 0)
        def _():
            S[...] = jnp.zeros_like(S)

        for j in range(G):
            r = slice(j * DN_C, (j + 1) * DN_C)
            glrow = gl_ref[j]
            for h in range(DN_H):
                c = slice(h * DN_D, (h + 1) * DN_D)
                Sh = S[h]
                s_ref[j, h] = Sh
                Sb = _b(Sh)
                vb = _b(u_ref[r, c] - _dot(w_ref[r, c], Sb))
                vn_ref[r, c] = vb
                o_ref[r, c] = _dot(qd_ref[r, c], Sb) + _dot(at_ref[r, h * DN_C:(h + 1) * DN_C], vb)
                S[h] = Sh * jnp.exp(_lane_pick(glrow, h)) + _dot_tn(kd_ref[r, c], vb)

    row = lambda wd: pl.BlockSpec((R, wd), lambda n: (n, 0))
    return pl.pallas_call(
        body, name="dn_scan", grid=(NC // G,),
        in_specs=[row(W), row(W), row(W), row(W), row(DN_H * DN_C), pl.BlockSpec((G, 1, HP), lambda n: (n, 0, 0))],
        out_specs=[row(W), row(W), pl.BlockSpec((G, DN_H, DN_D, DN_D), lambda n: (n, 0, 0, 0))],
        out_shape=[SDS((L, W), f32), SDS((L, W), bf16), SDS((NC, DN_H, DN_D, DN_D), f32)],
        scratch_shapes=[pltpu.VMEM((DN_H, DN_D, DN_D), f32)], compiler_params=_cp())(u, w, qd, kd, at, gl)


def _dn_scan_bwd(do, w, qd, kd, at, gl):
    L, W = do.shape
    NC = L // DN_C
    G = _pick(NC, DN_SCAN_GROUP)
    R = G * DN_C
    NS = NC // G

    def body(do_ref, w_ref, qd_ref, kd_ref, at_ref, gl_ref, dvn_ref, ds_ref, dS):
        @pl.when(pl.program_id(0) == 0)
        def _():
            dS[...] = jnp.zeros_like(dS)

        for j in reversed(range(G)):
            r = slice(j * DN_C, (j + 1) * DN_C)
            glrow = gl_ref[j]
            for h in range(DN_H):
                c = slice(h * DN_D, (h + 1) * DN_D)
                dSo = dS[h]
                ds_ref[j, h] = dSo
                dob = _b(do_ref[r, c])
                dvn = _dot_tn(at_ref[r, h * DN_C:(h + 1) * DN_C], dob) + _dot(kd_ref[r, c], _b(dSo))
                dvn_ref[r, c] = dvn
                dS[h] = _dot_tn(qd_ref[r, c], dob) + dSo * jnp.exp(_lane_pick(glrow, h)) - _dot_tn(w_ref[r, c], _b(dvn))

    row = lambda wd: pl.BlockSpec((R, wd), lambda n: (NS - 1 - n, 0))
    return pl.pallas_call(
        body, name="dn_scan_bwd", grid=(NS,),
        in_specs=[row(W), row(W), row(W), row(W), row(DN_H * DN_C), pl.BlockSpec((G, 1, HP), lambda n: (NS - 1 - n, 0, 0))],
        out_specs=[row(W), pl.BlockSpec((G, DN_H, DN_D, DN_D), lambda n: (NS - 1 - n, 0, 0, 0))],
        out_shape=[SDS((L, W), f32), SDS((NC, DN_H, DN_D, DN_D), f32)],
        scratch_shapes=[pltpu.VMEM((DN_H, DN_D, DN_D), f32)], compiler_params=_cp())(do, w, qd, kd, at, gl)


def _dn_intra_bwd(q, k, v, bg, xinv, do, vn, dvn, states, dstates):
    L, W = q.shape
    NC = L // DN_C
    G = _pick(NC, DN_INTRA_GROUP)
    R = G * DN_C

    def body(q_ref, k_ref, v_ref, bg_ref, x_ref, do_ref, vn_ref, dvn_ref, s_ref, ds_ref, dq_ref, dk_ref, dv_ref, dbg_ref):
        lane1 = _iota((1, HP), 1)
        rows = [slice(j * DN_C, (j + 1) * DN_C) for j in range(G)]
        xs = [[x_ref[r, h * DN_C:(h + 1) * DN_C] for h in range(DN_H)] for r in rows]
        fwd, vjp = jax.vjp(functools.partial(_dn_intra_group, xs=xs), q_ref[...], k_ref[...], v_ref[...], bg_ref[...])
        dws, dqds, dkds, dats, dgls = [], [], [], [], []
        for j, r in enumerate(rows):
            dw, dqd, dkd, dat = [], [], [], []
            dgl = jnp.zeros((1, HP), f32)
            for h in range(DN_H):
                c = slice(h * DN_D, (h + 1) * DN_D)
                Sh, dSo = s_ref[j, h], ds_ref[j, h]
                Sb, dob, vb = _b(Sh), _b(do_ref[r, c]), vn_ref[r, c]
                dw.append(-_dot_nt(_b(dvn_ref[r, c]), Sb))
                dqd.append(_dot_nt(dob, Sb))
                dat.append(_dot_nt(dob, vb))
                dkd.append(_dot_nt(vb, _b(dSo)))
                dcd = jnp.sum(jnp.sum(Sh * dSo, axis=1, keepdims=True), axis=0, keepdims=True)
                dgl = dgl + jnp.where(lane1 == h, dcd * jnp.exp(_lane_pick(fwd[5][j], h)), 0.0)
            cat = lambda xs_: jnp.concatenate(xs_, axis=1)
            dws.append(cat(dw)), dqds.append(cat(dqd)), dkds.append(cat(dkd)), dats.append(cat(dat)), dgls.append(dgl)
        cat0 = lambda xs_: jnp.concatenate(xs_, axis=0)
        dq, dk, dv, dbg = vjp((dvn_ref[...], cat0(dws), cat0(dqds), cat0(dkds), cat0(dats), tuple(dgls)))
        dq_ref[...], dk_ref[...], dv_ref[...], dbg_ref[...] = dq, dk, dv, dbg

    row = lambda wd: pl.BlockSpec((R, wd), lambda s: (s, 0))
    st = pl.BlockSpec((G, DN_H, DN_D, DN_D), lambda s: (s, 0, 0, 0))
    return pl.pallas_call(
        body, name="dn_intra_bwd", grid=(NC // G,),
        in_specs=[row(W), row(W), row(W), row(HP), row(DN_H * DN_C), row(W), row(W), row(W), st, st],
        out_specs=[row(W), row(W), row(W), row(HP)],
        out_shape=[SDS((L, W), f32)] * 3 + [SDS((L, HP), f32)],
        compiler_params=_cp())(q, k, v, bg, xinv, do, vn, dvn, states, dstates)


def _dn_normgate(oraw, dz, wn):
    outs = []
    for h in range(DN_H):
        sl = slice(h * DN_D, (h + 1) * DN_D)
        z = dz[:, sl]
        outs.append(_rms(oraw[:, sl], wn) * (z * _sigmoid(z)))
    return jnp.concatenate(outs, axis=1)


def _mix_fwd(op, oraw, dz, ga, gb, h0, wn, wbf, wbd, wo):
    L, D = h0.shape
    TM = _pick(L, ROW_TILES)

    def body(op_ref, or_ref, dz_ref, ga_ref, gb_ref, h0_ref, wn_ref, wbf_ref, wbd_ref, wo_ref, h1_ref):
        pf = _dot(_b(op_ref[...]), wbf_ref[...])
        pd = _dot(_b(_dn_normgate(or_ref[...], dz_ref[...], wn_ref[...])), wbd_ref[...])
        y = _sigmoid(ga_ref[...]) * pf + _sigmoid(gb_ref[...]) * pd
        h1_ref[...] = h0_ref[...] + _dot(_b(y), wo_ref[...])

    row = lambda wd: pl.BlockSpec((TM, wd), lambda i: (i, 0))
    full = lambda a: pl.BlockSpec(a.shape, lambda i: (0, 0))
    return pl.pallas_call(
        body, name="mix_fwd", grid=(L // TM,),
        in_specs=[row(op.shape[1]), row(oraw.shape[1]), row(dz.shape[1]), row(D), row(D), row(D), full(wn), full(wbf), full(wbd), full(wo)],
        out_specs=row(D), out_shape=SDS((L, D), f32), compiler_params=_cp())(op, oraw, dz, ga, gb, h0, wn, wbf, wbd, wo)


def _mix_bwd(dh1, op, oraw, dz, ga, gb, wn, wbf, wbd, wo):
    L, D = dh1.shape
    TM = _pick(L, ROW_TILES)
    WF, WD = op.shape[1], oraw.shape[1]

    def body(dh1_ref, op_ref, or_ref, dz_ref, ga_ref, gb_ref, wn_ref, wbf_ref, wbd_ref, wo_ref,
             dop_ref, dor_ref, dg_ref, af_ref, ad_ref, dpf_ref, dpd_ref, y_ref, dmix_ref, acc_ref):
        @pl.when(pl.program_id(0) == 0)
        def _():
            acc_ref[...] = jnp.zeros_like(acc_ref)

        opv = op_ref[...]
        af = _b(opv)
        ad, vjp = jax.vjp(_dn_normgate, or_ref[...], dz_ref[...], wn_ref[...])
        adb = _b(ad)
        pf, pd = _dot(af, wbf_ref[...]), _dot(adb, wbd_ref[...])
        sa, sb = _sigmoid(ga_ref[...]), _sigmoid(gb_ref[...])
        dmix = _b(dh1_ref[...])
        dy = _dot_nt(dmix, wo_ref[...])
        dpf, dpd = _b(dy * sa), _b(dy * sb)
        dor, ddz, dwn = vjp(_dot_nt(dpd, wbd_ref[...]))
        dop = _dot_nt(dpf, wbf_ref[...])
        lane = _iota((TM, HP), 1)
        for h in range(WF // HP):
            c = slice(h * HP, (h + 1) * HP)
            delta = jnp.sum(jnp.where(lane < FOX_D, dop[:, c] * opv[:, c], 0.0), axis=1, keepdims=True)
            dop_ref[:, c] = _b(dop[:, c] - _spread3(jnp.broadcast_to(delta, (TM, HP)), lane, C_DELTA0))
        dor_ref[...] = dor
        dg_ref[:, 0:WD] = _b(ddz)
        dg_ref[:, WD:WD + D] = _b(dy * pf * sa * (1.0 - sa))
        dg_ref[:, WD + D:] = _b(dy * pd * sb * (1.0 - sb))
        af_ref[...], ad_ref[...], y_ref[...] = af, adb, _b(sa * pf + sb * pd)
        dpf_ref[...], dpd_ref[...], dmix_ref[...] = dpf, dpd, dmix
        acc_ref[0:1, :] += dwn

    row = lambda wd: pl.BlockSpec((TM, wd), lambda i: (i, 0))
    full = lambda a: pl.BlockSpec(a.shape, lambda i: (0, 0))
    return pl.pallas_call(
        body, name="mix_bwd", grid=(L // TM,),
        in_specs=[row(D), row(WF), row(WD), row(WD), row(D), row(D), full(wn), full(wbf), full(wbd), full(wo)],
        out_specs=[row(WF), row(WD), row(WD + 2 * D), row(WF), row(WD), row(D), row(D), row(D), row(D),
                   pl.BlockSpec((8, HP), lambda i: (0, 0))],
        out_shape=[SDS((L, WF), bf16), SDS((L, WD), f32), SDS((L, WD + 2 * D), bf16), SDS((L, WF), bf16), SDS((L, WD), bf16),
                   SDS((L, D), bf16), SDS((L, D), bf16), SDS((L, D), bf16), SDS((L, D), bf16), SDS((8, HP), f32)],
        compiler_params=_cp())(dh1, op, oraw, dz, ga, gb, wn, wbf, wbd, wo)


def _ffn_fwd_bwd(h1, tgt, w2, wf, wgt, wut, wd):
    L, D = h1.shape
    F = wd.shape[0]
    TM = _pick(L, FFN_TILES)

    def body(h_ref, t_ref, w2_ref, wf_ref, wg_hbm, wu_hbm, wd_hbm,
             dh1_ref, xn_ref, dg_ref, du_ref, act_ref, dh2_ref, acc_ref, wg_v, wu_v, wd_v, sems):
        i = pl.program_id(0)
        _load_once([(wg_hbm, wg_v), (wu_hbm, wu_v), (wd_hbm, wd_v)], sems)

        @pl.when(i == 0)
        def _():
            acc_ref[...] = jnp.zeros_like(acc_ref)

        h1v = h_ref[...]
        xn2, vjp2 = jax.vjp(_rms, h1v, w2_ref[...])
        xb = _b(xn2)
        g, u = _dot_nt(xb, wg_v[...]), _dot_nt(xb, wu_v[...])
        sg = _sigmoid(g)
        ab = _b(g * sg * u)
        h2 = h1v + _dot(ab, wd_v[...])
        out, vjpf = jax.vjp(_rms, h2, wf_ref[...])
        valid = (i * TM + _iota((TM, 1), 0)) >= PREFIX
        diff = jnp.where(valid, out - t_ref[...], 0.0)
        loss = 0.5 * jnp.sum(jnp.sum(diff * diff, axis=1, keepdims=True), axis=0, keepdims=True) / D
        dh2, dwf = vjpf(diff * (1.0 / D))
        dh2b = _b(dh2)
        dact = _dot_nt(dh2b, wd_v[...])
        dgb = _b(dact * u * (sg * (1.0 + g * (1.0 - sg))))
        dub = _b(dact * (g * sg))
        dh1n, dw2 = vjp2(_dot(dgb, wg_v[...]) + _dot(dub, wu_v[...]))
        dh1_ref[...] = dh2 + dh1n
        xn_ref[...], dg_ref[...], du_ref[...], act_ref[...], dh2_ref[...] = xb, dgb, dub, ab, dh2b
        acc_ref[0:1, :] += dw2
        acc_ref[1:2, :] += dwf
        acc_ref[2:3, :] += jnp.broadcast_to(loss, (1, D))

    row = lambda wd_: pl.BlockSpec((TM, wd_), lambda i: (i, 0))
    vec = pl.BlockSpec((1, D), lambda i: (0, 0))
    anyspec = pl.BlockSpec(memory_space=pl.ANY)
    return pl.pallas_call(
        body, name="ffn_fwd_bwd", grid=(L // TM,),
        in_specs=[row(D), row(D), vec, vec, anyspec, anyspec, anyspec],
        out_specs=[row(D), row(D), row(F), row(F), row(F), row(D), pl.BlockSpec((8, D), lambda i: (0, 0))],
        out_shape=[SDS((L, D), f32), SDS((L, D), bf16), SDS((L, F), bf16), SDS((L, F), bf16), SDS((L, F), bf16), SDS((L, D), bf16),
                   SDS((8, D), f32)],
        scratch_shapes=[pltpu.VMEM((F, D), bf16), pltpu.VMEM((F, D), bf16), pltpu.VMEM((F, D), bf16), pltpu.SemaphoreType.DMA((3,))],
        compiler_params=_cp())(h1, tgt, w2, wf, wgt, wut, wd)


def _pad_lanes(v, n=HP):
    return jnp.pad(v.astype(f32), ((0, 0), (0, n - v.shape[1])))


def _pack_w_in(wt_full):
    D = wt_full.shape[1]
    FW, DW = FOX_H * FOX_D, DN_H * DN_D
    o = 0
    parts = {}
    for name, wd in (("fq", FW), ("fk", FW), ("fv", FW), ("fl", FOX_H), ("dn", 3 * DW), ("ba", 2 * DN_H), ("dz", DW), ("ga", D), ("gb", D)):
        parts[name] = wt_full[o:o + wd]
        o += wd
    assert o == wt_full.shape[0]
    heads = lambda w: jnp.pad(w.reshape(FOX_H, FOX_D, D), ((0, 0), (0, HP - FOX_D), (0, 0))).reshape(FOX_H * HP, D)
    small = lambda w: jnp.pad(w, ((0, SMALL_W - w.shape[0]), (0, 0)))
    packed = dict(fq=heads(parts["fq"]), fk=heads(parts["fk"]), fv=heads(parts["fv"]), sf=small(parts["fl"]), sd=small(parts["ba"]),
                  dn=parts["dn"], dz=parts["dz"], ga=parts["ga"], gb=parts["gb"])
    return jnp.concatenate([packed[name] for name, _, _, _ in _seg_layout(D)], axis=0)


def _unpack_w_in(groups, d_model):
    D = groups[0].shape[1]
    FW = FOX_H * FOX_D
    segs = {}
    for grp, g in zip(GROUPS, groups):
        o = 0
        for name, wd, _, sg in _seg_layout(d_model):
            if sg == grp:
                segs[name] = g[o:o + wd]
                o += wd
    heads = lambda g: g.reshape(FOX_H, HP, D)[:, :FOX_D].reshape(FW, D)
    return jnp.concatenate([heads(segs["fq"]), heads(segs["fk"]), heads(segs["fv"]), segs["sf"][:FOX_H], segs["dn"],
                            segs["sd"][:2 * DN_H], segs["dz"], segs["ga"], segs["gb"]], axis=0)


def _local_step(x, tgt, meta, w1, w_in_t, fbias, cw, alog, dtb, wn, w2, wf, late_shards):
    T, D = x.shape
    h0 = jnp.concatenate([jnp.zeros((N_PAD, D), f32), meta, x], axis=0)
    tgt_p = jnp.concatenate([jnp.zeros((PREFIX, D), f32), tgt], axis=0)
    wp = _pack_w_in(w_in_t)
    bias_p, alog_p, dt_p = _pad_lanes(fbias), _pad_lanes(jnp.pad(alog, ((0, 0), (DN_H, 0)))), _pad_lanes(jnp.pad(dtb, ((0, 0), (DN_H, 0))))

    (xn, fq, fk, sf, fv, dn, sd, dz, ga, gb), g_mix = _in_proj(h0, w1, wp, [late_shards[n] for n in LATE_MIX])
    qa, ka, va = _fox_prep(fq, fk, fv, sf, bias_p)
    op, qb, g_ffn = _fox_fwd(qa, ka, va, [late_shards[n] for n in LATE_FFN])
    qn, kn, vn, bg = _dn_prep(dn, sd, cw, alog_p, dt_p)
    (u_dn, w_dn, qd_dn, kd_dn, at_dn, gl_dn, x_dn), g_down = _dn_intra(qn, kn, vn, bg, [late_shards[n] for n in LATE_DOWN])
    full = {n: _from_slabs(n, s) for n, s in zip(LATE_MIX + LATE_FFN + LATE_DOWN, tuple(g_mix) + tuple(g_ffn) + tuple(g_down))}
    wbf, wbd, wo, wgt, wut, wd = (full[n] for n in ("w_branch_fox", "w_branch_dn", "w_out", "w_ffn_gate", "w_ffn_up", "w_ffn_down"))
    wbf_p = jnp.pad(wbf.reshape(FOX_H, FOX_D, D), ((0, 0), (0, HP - FOX_D), (0, 0))).reshape(FOX_H * HP, D)
    oraw, vnew, states = _dn_scan(u_dn, w_dn, qd_dn, kd_dn, at_dn, gl_dn)
    h1 = _mix_fwd(op, oraw, dz, ga, gb, h0, wn, wbf_p, wbd, wo)

    dh1, xn2, dgate, dup, act, dh2, acc_f = _ffn_fwd_bwd(h1, tgt_p, w2, wf, wgt, wut, wd)
    g_wg, g_wu, g_wd = _matmul_tn(dgate, xn2, "dw_ffn_gate"), _matmul_tn(dup, xn2, "dw_ffn_up"), _matmul_tn(act, dh2, "dw_ffn_down")

    dop, dor, d_mix, af, ad, dpf, dpd, yb, dmix, acc_m = _mix_bwd(dh1, op, oraw, dz, ga, gb, wn, wbf_p, wbd, wo)
    g_wbf = _matmul_tn(af, dpf, "dw_branch_fox").reshape(FOX_H, HP, D)[:, :FOX_D].reshape(FOX_H * FOX_D, D)
    g_wbd, g_wo = _matmul_tn(ad, dpd, "dw_branch_dn"), _matmul_tn(yb, dmix, "dw_out")

    dvnew, dstates = _dn_scan_bwd(dor, w_dn, qd_dn, kd_dn, at_dn, gl_dn)
    dqn, dkn, dvn, dbg = _dn_intra_bwd(qn, kn, vn, bg, x_dn, dor, vnew, dvnew, states, dstates)
    d_dn, acc_cw, acc_p = _dn_prep_bwd(dn, sd, cw, alog_p, dt_p, dqn, dkn, dvn, dbg)
    g_late = dict(w_branch_fox=g_wbf, w_branch_dn=g_wbd, w_out=g_wo, w_ffn_gate=g_wg, w_ffn_up=g_wu, w_ffn_down=g_wd)
    (dqa, dka, d_fv), recv = _fox_bwd(qb, ka, va, dop, [_to_slabs(n, g_late[n]) for n in LATE])
    d_fox, acc_b = _fox_prep_bwd(dqa, dka, sf, bias_p)

    dgroups = [d_fox, d_fv, d_dn, d_mix]
    g_wp = [_matmul_tn(dg, xn, "dw_in_" + grp) for grp, dg in zip(GROUPS, dgroups)]
    dh0, acc_1, (recv_w_in,) = _in_proj_bwd(dgroups, wp, h0, w1, dh1, [_to_slabs("w_in", _unpack_w_in(g_wp, D))])
    recv = dict(zip(LATE, recv), w_in=recv_w_in)

    small = dict(loss=acc_f[2, 0:1], mix_norm_w=acc_1[0], fox_forget_bias=acc_b[0, :FOX_H], dn_a_log=acc_p[0, DN_H:2 * DN_H],
                 dn_dt_bias=acc_p[1, DN_H:2 * DN_H], dn_out_norm_w=acc_m[0], ffn_norm_w=acc_f[0], final_norm_w=acc_f[1],
                 meta_tokens=dh0[N_PAD:PREFIX].reshape(-1), dn_conv_w=acc_cw[:CONV_K].reshape(-1))
    return dh0[PREFIX:], small, recv


def _mesh_pos():
    x, y, c = lax.axis_index("x"), lax.axis_index("y"), lax.axis_index("c")
    return x, y, c, 4 * x + 2 * y + c


def _peer(x, y, c, m):
    flip = lambda v, on: 1 - v if on else v
    px, py, pc = flip(x, m & 4), flip(y, m & 2), flip(c, m & 1)
    return (px, py, pc), 4 * px + 2 * py + pc


def _exchange_sems(n):
    return [pltpu.SemaphoreType.DMA((n, N_DEV - 1)), pltpu.SemaphoreType.DMA((n, N_DEV - 1)), pltpu.SemaphoreType.DMA((n,))]


def _exchange_part(ins, outs, send_sems, recv_sems, loc_sems, gather, m, receive):
    x, y, c, me = _mesh_pos()
    src = lambda a, pid: ins[a] if gather else ins[a].at[pid]
    if m == 0:
        return [pltpu.make_async_copy(src(a, me), outs[a].at[me], loc_sems.at[a]) for a in range(len(ins))]
    peer, pid = _peer(x, y, c, m)
    return [pltpu.make_async_remote_copy(src_ref=src(a, pid), dst_ref=outs[a].at[pid if receive else me], send_sem=send_sems.at[a, m - 1],
                                         recv_sem=recv_sems.at[a, m - 1], device_id=peer, device_id_type=MESH) for a in range(len(ins))]


def _exchange_start(*refs, gather, when):
    @pl.when(when)
    def _():
        for m in range(N_DEV):
            for cp in _exchange_part(*refs, gather, m, receive=False):
                cp.start()


def _exchange_wait(*refs, gather, when):
    @pl.when(when)
    def _():
        for m in range(1, N_DEV):
            for cp in _exchange_part(*refs, gather, m, receive=True):
                cp.wait_recv()
        for m in list(range(1, N_DEV)) + [0]:
            for cp in _exchange_part(*refs, gather, m, receive=False):
                cp.wait() if m == 0 else cp.wait_send()


def _gather_two_level(arrays, name):
    n = len(arrays)

    def body(*refs):
        ins, outs, (send_sems, recv_sems, loc_sems) = refs[:n], refs[n:2 * n], refs[2 * n:]
        x, y, c, me = _mesh_pos()
        sib = (x, y, 1 - c)
        chips = [(1 - x, y), (x, 1 - y), (1 - x, 1 - y)]
        dev_id = lambda px, py, pc: 4 * px + 2 * py + pc

        def copy(a, k, block, to, own=False):
            return pltpu.make_async_remote_copy(src_ref=ins[a] if own else outs[a].at[block], dst_ref=outs[a].at[block],
                                                send_sem=send_sems.at[a, k], recv_sem=recv_sems.at[a, k], device_id=to, device_id_type=MESH)

        local = [pltpu.make_async_copy(ins[a], outs[a].at[me], loc_sems.at[a]) for a in range(n)]
        first = [copy(a, 0, me, sib, own=True) for a in range(n)]
        first += [copy(a, 1 + j, me, (*chip, c), own=True) for j, chip in enumerate(chips) for a in range(n)]
        for cp in local + first:
            cp.start()
        passed = []
        for j, chip in enumerate(chips):
            for a in range(n):
                copy(a, 1 + j, dev_id(*chip, c), sib).wait_recv()
                cp = copy(a, 4 + j, dev_id(*chip, c), sib)
                cp.start()
                passed.append(cp)
        for a in range(n):
            copy(a, 0, dev_id(x, y, 1 - c), sib).wait_recv()
        for j, chip in enumerate(chips):
            for a in range(n):
                copy(a, 4 + j, dev_id(*chip, 1 - c), sib).wait_recv()
        for cp in first + passed:
            cp.wait_send()
        for cp in local:
            cp.wait()

    anyspec = pl.BlockSpec(memory_space=pl.ANY)
    return pl.pallas_call(
        body, name=name, in_specs=[anyspec] * n, out_specs=[anyspec] * n,
        out_shape=[SDS((N_DEV,) + a.shape, a.dtype) for a in arrays],
        scratch_shapes=_exchange_sems(n))(*arrays)


def _all_reduce_small(v):
    R = v.shape[0]

    def body(v_ref, o_ref, gath, send_sems, recv_sems):
        x, y, c, me = _mesh_pos()
        gath[me] = v_ref[...]
        sends = []
        for m in range(1, N_DEV):
            peer, _ = _peer(x, y, c, m)
            cp = pltpu.make_async_remote_copy(src_ref=v_ref, dst_ref=gath.at[me], send_sem=send_sems.at[m - 1],
                                              recv_sem=recv_sems.at[m - 1], device_id=peer, device_id_type=MESH)
            cp.start()
            sends.append(cp)
        for m in range(1, N_DEV):
            peer, pid = _peer(x, y, c, m)
            pltpu.make_async_remote_copy(src_ref=v_ref, dst_ref=gath.at[pid], send_sem=send_sems.at[m - 1],
                                         recv_sem=recv_sems.at[m - 1], device_id=peer, device_id_type=MESH).wait_recv()
        for cp in sends:
            cp.wait_send()
        tot = gath[0]
        for d in range(1, N_DEV):
            tot = tot + gath[d]
        o_ref[...] = tot

    vm = pl.BlockSpec(memory_space=pltpu.VMEM)
    return pl.pallas_call(
        body, name="all_reduce_small", in_specs=[vm], out_specs=vm, out_shape=SDS((R, HP), f32),
        scratch_shapes=[pltpu.VMEM((N_DEV, R, HP), f32), pltpu.SemaphoreType.DMA((N_DEV - 1,)), pltpu.SemaphoreType.DMA((N_DEV - 1,))],
        )(v)


def _adamw_math(w, g, m, v):
    m = ADAM_B1 * m + (1.0 - ADAM_B1) * g
    v = ADAM_B2 * v + (1.0 - ADAM_B2) * (g * g)
    m_hat = m / (1.0 - ADAM_B1 ** ADAM_STEP)
    v_hat = v / (1.0 - ADAM_B2 ** ADAM_STEP)
    return -ADAM_LR * (m_hat / (jnp.sqrt(v_hat) + ADAM_EPS) + ADAM_WD * w), m, v


def _adamw(g, w, m, v, name):
    R, Cc = w.shape[-2:]
    if R <= 512 or R % 128 == 0:
        TR, TC = (R if R <= 512 else _pick(R, (256, 128))), Cc
    else:
        TR, TC = R, _pick(Cc, (256, 128))
    slabs = g.ndim == 3
    lead = w.ndim - 2

    def body(g_ref, w_ref, m_ref, v_ref, go_ref, d_ref, mo_ref, vo_ref):
        if slabs:
            gs = g_ref[0].astype(f32)
            for k in range(1, N_DEV):
                gs = gs + g_ref[k].astype(f32)
        else:
            gs = g_ref[...]
        at = 0 if lead else Ellipsis
        d, mn, vn = _adamw_math(w_ref[at], gs, m_ref[at], v_ref[at])
        go_ref[at], d_ref[at], mo_ref[at], vo_ref[at] = gs, d, mn, vn

    grid = (R // TR, Cc // TC)
    blk = pl.BlockSpec((1,) * lead + (TR, TC), lambda i, j: (0,) * lead + (i, j))
    gblk = pl.BlockSpec((N_DEV, TR, TC), lambda i, j: (0, i, j)) if slabs else pl.BlockSpec((TR, TC), lambda i, j: (i, j))
    return pl.pallas_call(
        body, name=name, grid=grid, in_specs=[gblk, blk, blk, blk], out_specs=[blk] * 4,
        out_shape=[SDS(w.shape, f32)] * 4, compiler_params=_cp(2))(g, w, m, v)


WEIGHTS = ("meta_tokens", "mix_norm_w", "w_in", "fox_forget_bias", "dn_conv_w", "dn_a_log", "dn_dt_bias", "dn_out_norm_w",
           "w_branch_fox", "w_branch_dn", "w_out", "ffn_norm_w", "w_ffn_gate", "w_ffn_up", "w_ffn_down", "final_norm_w")
COL_SHARDED = ("w_in", "w_branch_fox", "w_branch_dn", "w_ffn_gate", "w_ffn_up")
ROW_SHARDED = ("w_out", "w_ffn_down")
BIG = COL_SHARDED + ROW_SHARDED
LATE = tuple(n for n in BIG if n != "w_in")
LATE_MIX = ("w_branch_fox", "w_branch_dn", "w_out")
LATE_FFN = ("w_ffn_gate", "w_ffn_up")
LATE_DOWN = ("w_ffn_down",)
SMALL = tuple(n for n in WEIGHTS if n not in BIG)
TRANSPOSED = ("w_in", "w_ffn_gate", "w_ffn_up")


def _to_slabs(name, g):
    r, c = g.shape
    if name in COL_SHARDED and name not in TRANSPOSED:
        return _b(g.reshape(r, N_DEV, c // N_DEV).transpose(1, 0, 2))
    return _b(g.reshape(N_DEV, r // N_DEV, c))


def _from_slabs(name, s):
    n, r, c = s.shape
    if name in COL_SHARDED and name not in TRANSPOSED:
        return s.transpose(1, 0, 2).reshape(r, n * c)
    return s.reshape(n * r, c)


def kernel(x, meta_tokens, mix_norm_w, w_in, fox_forget_bias, dn_conv_w, dn_a_log, dn_dt_bias, dn_out_norm_w, w_branch_fox, w_branch_dn, w_out, ffn_norm_w, w_ffn_gate, w_ffn_up, w_ffn_down, final_norm_w, loss_target, m_meta_tokens, m_mix_norm_w, m_w_in, m_fox_forget_bias, m_dn_conv_w, m_dn_a_log, m_dn_dt_bias, m_dn_out_norm_w, m_w_branch_fox, m_w_branch_dn, m_w_out, m_ffn_norm_w, m_w_ffn_gate, m_w_ffn_up, m_w_ffn_down, m_final_norm_w, v_meta_tokens, v_mix_norm_w, v_w_in, v_fox_forget_bias, v_dn_conv_w, v_dn_a_log, v_dn_dt_bias, v_dn_out_norm_w, v_w_branch_fox, v_w_branch_dn, v_w_out, v_ffn_norm_w, v_w_ffn_gate, v_w_ffn_up, v_w_ffn_down, v_final_norm_w):
    w = dict(meta_tokens=meta_tokens, mix_norm_w=mix_norm_w, w_in=w_in, fox_forget_bias=fox_forget_bias, dn_conv_w=dn_conv_w, dn_a_log=dn_a_log, dn_dt_bias=dn_dt_bias, dn_out_norm_w=dn_out_norm_w, w_branch_fox=w_branch_fox, w_branch_dn=w_branch_dn, w_out=w_out, ffn_norm_w=ffn_norm_w, w_ffn_gate=w_ffn_gate, w_ffn_up=w_ffn_up, w_ffn_down=w_ffn_down, final_norm_w=final_norm_w)
    mom = dict(meta_tokens=m_meta_tokens, mix_norm_w=m_mix_norm_w, w_in=m_w_in, fox_forget_bias=m_fox_forget_bias, dn_conv_w=m_dn_conv_w, dn_a_log=m_dn_a_log, dn_dt_bias=m_dn_dt_bias, dn_out_norm_w=m_dn_out_norm_w, w_branch_fox=m_w_branch_fox, w_branch_dn=m_w_branch_dn, w_out=m_w_out, ffn_norm_w=m_ffn_norm_w, w_ffn_gate=m_w_ffn_gate, w_ffn_up=m_w_ffn_up, w_ffn_down=m_w_ffn_down, final_norm_w=m_final_norm_w)
    var = dict(meta_tokens=v_meta_tokens, mix_norm_w=v_mix_norm_w, w_in=v_w_in, fox_forget_bias=v_fox_forget_bias, dn_conv_w=v_dn_conv_w, dn_a_log=v_dn_a_log, dn_dt_bias=v_dn_dt_bias, dn_out_norm_w=v_dn_out_norm_w, w_branch_fox=v_w_branch_fox, w_branch_dn=v_w_branch_dn, w_out=v_w_out, ffn_norm_w=v_ffn_norm_w, w_ffn_gate=v_w_ffn_gate, w_ffn_up=v_w_ffn_up, w_ffn_down=v_w_ffn_down, final_norm_w=v_final_norm_w)
    two_d = lambda a: a.reshape(a.shape[-2:]) if a.ndim >= 2 else a.reshape(1, -1)
    me = 4 * lax.axis_index("x") + 2 * lax.axis_index("y") + lax.axis_index("c")
    for d in (w, mom, var):
        for n in TRANSPOSED:
            d[n] = jnp.swapaxes(d[n], -1, -2)

    g_in, g_meta, g_cw = _gather_two_level([_b(two_d(w["w_in"])), two_d(w["meta_tokens"]), two_d(w["dn_conv_w"])], "all_gather_early")
    meta = g_meta.transpose(1, 0, 2).reshape(N_META, -1)
    cw = g_cw.transpose(1, 0, 2).reshape(CONV_K, -1)

    gx, g_small, recv = _local_step(
        x[0], loss_target[0], meta, two_d(w["mix_norm_w"]), _from_slabs("w_in", g_in), two_d(w["fox_forget_bias"]), cw, two_d(w["dn_a_log"]),
        two_d(w["dn_dt_bias"]), two_d(w["dn_out_norm_w"]), two_d(w["ffn_norm_w"]), two_d(w["final_norm_w"]),
        {n: _b(two_d(w[n])) for n in LATE})

    order = ("loss",) + SMALL
    flat = jnp.concatenate([g_small[n].reshape(-1) for n in order])
    rows = -(-flat.shape[0] // (8 * HP)) * 8
    tot = _all_reduce_small(jnp.pad(flat, (0, rows * HP - flat.shape[0])).reshape(rows, HP)).reshape(-1)
    summed, o = {}, 0
    for n in order:
        k = g_small[n].shape[0]
        summed[n] = tot[o:o + k]
        o += k
    loss = summed["loss"][0]
    d_model = x.shape[-1]
    mcols, ccols = d_model // N_DEV, dn_conv_w.shape[-1]
    summed["meta_tokens"] = lax.dynamic_slice(summed["meta_tokens"].reshape(N_META, d_model), (0, me * mcols), (N_META, mcols)).reshape(-1)
    summed["dn_conv_w"] = lax.dynamic_slice(summed["dn_conv_w"].reshape(CONV_K, ccols * N_DEV), (0, me * ccols), (CONV_K, ccols)).reshape(-1)

    res = {}
    for n in BIG:
        res[n] = _adamw(recv[n], w[n], mom[n], var[n], "adamw_" + n)
        if n in TRANSPOSED:
            res[n] = [jnp.swapaxes(r, -1, -2) for r in res[n]]
    sizes = [summed[n].shape[0] for n in SMALL]
    srows = -(-sum(sizes) // (8 * HP)) * 8
    pack = lambda d: jnp.pad(jnp.concatenate([d[n].reshape(-1) for n in SMALL]), (0, srows * HP - sum(sizes))).reshape(srows, HP)
    sres = _adamw(pack(summed), pack(w), pack(mom), pack(var), "adamw_small")
    o = 0
    for n, k in zip(SMALL, sizes):
        res[n] = [r.reshape(-1)[o:o + k].reshape(w[n].shape) for r in sres]
        o += k
    return (loss, gx[None], *[res[n][0] for n in WEIGHTS], *[res[n][1] for n in WEIGHTS], *[res[n][2] for n in WEIGHTS], *[res[n][3] for n in WEIGHTS])
```

```python
import functools

import jax
import jax.numpy as jnp
from jax import lax
from jax.experimental import pallas as pl
from jax.experimental.pallas import tpu as pltpu

f32, bf16 = jnp.float32, jnp.bfloat16
HI = lax.Precision.HIGHEST
MESH = pl.DeviceIdType.MESH
SDS = jax.ShapeDtypeStruct

N_DEV = 8
N_META = 16
PREFIX = 128
N_PAD = PREFIX - N_META
FOX_H, FOX_D = 8, 64
DN_H, DN_D = 4, 128
DN_C = 64
CONV_K = 4
HP = 128
SMALL_W = 256
EPS = 1e-6
NEG = -1e30
C_Q0, C_K0 = 64, 67
LSE_COL = 64
LOG2E, LN2 = 1.4426950408889634, 0.6931471805599453
C_LSE0, C_DELTA0 = 70, 65

ADAM_LR, ADAM_B1, ADAM_B2, ADAM_EPS, ADAM_WD, ADAM_STEP = 0.001, 0.9, 0.999, 1e-08, 0.01, 10

VMEM_LIMIT_V7X = 56 * 1024 * 1024
ROW_TILES = (384, 128)
ATTN_TILES = (384, 128)
FFN_TILES = (192, 64)
FOX_HEAD_GROUP = 4
FOX_HEAD_GROUP_FWD = 8
MAX_WGRAD_BLOCK = 1408
DN_INTRA_GROUP = (6, 3, 2, 1)
DN_SCAN_GROUP = (6, 3, 2, 1)


def _pick(n, cands):
    for c in cands:
        if n % c == 0:
            return c
    raise ValueError(f"no tile of {cands} divides {n}")


def _cp(n_axes=1):
    return pltpu.CompilerParams(dimension_semantics=("arbitrary",) * n_axes, vmem_limit_bytes=VMEM_LIMIT_V7X)


def _b(x):
    return x.astype(bf16)


def _dot(a, b):
    return jnp.dot(a, b, preferred_element_type=f32)


def _dot_nt(a, b):
    return lax.dot_general(a, b, (((1,), (1,)), ((), ())), preferred_element_type=f32)


def _dot_tn(a, b):
    return lax.dot_general(a, b, (((0,), (0,)), ((), ())), preferred_element_type=f32)


def _dot_hi(a, b):
    return jnp.dot(a, b, preferred_element_type=f32, precision=HI)


def _iota(shape, dim):
    return lax.broadcasted_iota(jnp.int32, shape, dim)


def _rms(x, w):
    return x * lax.rsqrt(jnp.mean(x * x, axis=-1, keepdims=True) + EPS) * w


def _sigmoid(x):
    return jax.nn.sigmoid(x)


def _load_once(pairs, sems):
    @pl.when(pl.program_id(0) == 0)
    def _():
        cps = [pltpu.make_async_copy(src, dst, sems.at[k]) for k, (src, dst) in enumerate(pairs)]
        for cp in cps:
            cp.start()
        for cp in cps:
            cp.wait()


def _seg_layout(d_model):
    return (("fq", FOX_H * HP, bf16, "fox"), ("fk", FOX_H * HP, bf16, "fox"), ("sf", SMALL_W, f32, "fox"),
            ("fv", FOX_H * HP, bf16, "fv"),
            ("dn", 3 * DN_H * DN_D, f32, "dn"), ("sd", SMALL_W, f32, "dn"),
            ("dz", DN_H * DN_D, f32, "mix"), ("ga", d_model, f32, "mix"), ("gb", d_model, f32, "mix"))


GROUPS = ("fox", "fv", "dn", "mix")


def _in_proj(h0, w1, wpt, shards):
    L, D = h0.shape
    NP = wpt.shape[0]
    TM = _pick(L, ROW_TILES)
    NT = L // TM
    segs = _seg_layout(D)
    ns, n = len(segs), len(shards)
    offs, o = [], 0
    for _, wd, _, _ in segs:
        offs.append(o)
        o += wd
    assert o == NP

    def body(h_ref, w1_ref, wp_hbm, *rest):
        ins, xn_ref, outs, gouts = rest[:n], rest[n], rest[n + 1:n + 1 + ns], rest[n + 1 + ns:2 * n + 1 + ns]
        wp_v, sems = rest[2 * n + 1 + ns:2 * n + 3 + ns]
        xsems = rest[2 * n + 3 + ns:]
        _load_once([(wp_hbm, wp_v)], sems)

        _exchange_start(ins, gouts, *xsems, gather=True, when=pl.program_id(0) == 0)

        xn = _b(_rms(h_ref[...], w1_ref[...]))
        xn_ref[...] = xn
        for o_ref, off, (_, wd, _, _) in zip(outs, offs, segs):
            o_ref[...] = _dot_nt(xn, wp_v[off:off + wd, :]).astype(o_ref.dtype)

        _exchange_wait(ins, gouts, *xsems, gather=True, when=pl.program_id(0) == NT - 1)

    row = lambda wd: pl.BlockSpec((TM, wd), lambda i: (i, 0))
    anyspec = pl.BlockSpec(memory_space=pl.ANY)
    res = pl.pallas_call(
        body, name="in_proj", grid=(NT,),
        in_specs=[row(D), pl.BlockSpec((1, D), lambda i: (0, 0)), anyspec] + [anyspec] * n,
        out_specs=[row(D)] + [row(wd) for _, wd, _, _ in segs] + [anyspec] * n,
        out_shape=[SDS((L, D), bf16)] + [SDS((L, wd), dt) for _, wd, dt, _ in segs] + [SDS((N_DEV,) + a.shape, a.dtype) for a in shards],
        scratch_shapes=[pltpu.VMEM((NP, D), bf16), pltpu.SemaphoreType.DMA((1,))] + _exchange_sems(n),
        compiler_params=_cp())(h0, w1, wpt, *shards)
    return res[:1 + ns], res[1 + ns:]


def _in_proj_bwd(dgroups, wpt, h0, w1, dh1, slabs):
    L, D = h0.shape
    NP = wpt.shape[0]
    TM = _pick(L, ROW_TILES)
    NT = L // TM
    widths = [g.shape[1] for g in dgroups]
    assert sum(widths) == NP
    ng, n = len(dgroups), len(slabs)

    def body(*refs):
        dg_refs, (wp_hbm, h_ref, w1_ref, dh1_ref) = refs[:ng], refs[ng:ng + 4]
        ins, (dh0_ref, acc_ref), outs = refs[ng + 4:ng + 4 + n], refs[ng + 4 + n:ng + 6 + n], refs[ng + 6 + n:ng + 6 + 2 * n]
        wp_v, sems = refs[ng + 6 + 2 * n:ng + 8 + 2 * n]
        xsems = refs[ng + 8 + 2 * n:]
        _load_once([(wp_hbm, wp_v)], sems)

        @pl.when(pl.program_id(0) == 0)
        def _():
            acc_ref[...] = jnp.zeros_like(acc_ref)

        _exchange_start(ins, outs, *xsems, gather=False, when=pl.program_id(0) == 0)

        dxn, off = None, 0
        for g_ref, wd in zip(dg_refs, widths):
            part = _dot(g_ref[...], wp_v[off:off + wd, :])
            dxn = part if dxn is None else dxn + part
            off += wd
        _, vjp = jax.vjp(_rms, h_ref[...], w1_ref[...])
        dh0n, dw1 = vjp(dxn)
        dh0_ref[...] = dh1_ref[...] + dh0n
        acc_ref[0:1, :] += dw1

        _exchange_wait(ins, outs, *xsems, gather=False, when=pl.program_id(0) == NT - 1)

    row = lambda wd: pl.BlockSpec((TM, wd), lambda i: (i, 0))
    anyspec = pl.BlockSpec(memory_space=pl.ANY)
    res = pl.pallas_call(
        body, name="in_proj_bwd", grid=(NT,),
        in_specs=[row(wd) for wd in widths] + [anyspec, row(D), pl.BlockSpec((1, D), lambda i: (0, 0)), row(D)] + [anyspec] * n,
        out_specs=[row(D), pl.BlockSpec((8, D), lambda i: (0, 0))] + [anyspec] * n,
        out_shape=[SDS((L, D), f32), SDS((8, D), f32)] + [SDS(a.shape, a.dtype) for a in slabs],
        scratch_shapes=[pltpu.VMEM((NP, D), bf16), pltpu.SemaphoreType.DMA((1,))] + _exchange_sems(n),
        compiler_params=_cp())(*dgroups, wpt, h0, w1, dh1, *slabs)
    return res[0], res[1], res[2:]


def _matmul_tn(a, b, name):
    L, R = a.shape
    C = b.shape[1]
    br = max(k for k in range(HP, MAX_WGRAD_BLOCK + 1, HP) if R % k == 0)

    def body(a_ref, b_ref, o_ref):
        o_ref[...] = _b(_dot_tn(a_ref[...], b_ref[...]))

    return pl.pallas_call(
        body, name=name, grid=(R // br,),
        in_specs=[pl.BlockSpec((L, br), lambda r: (0, r)), pl.BlockSpec((L, C), lambda r: (0, 0))],
        out_specs=pl.BlockSpec((br, C), lambda r: (r, 0)), out_shape=SDS((R, C), bf16), compiler_params=_cp())(a, b)


def _fox_prep(fq, fk, fv, sf, bias_p):
    L = fq.shape[0]
    T = HP
    NT = L // T
    W = FOX_H * HP

    def body(fq_ref, fk_ref, fv_ref, sf_ref, b_ref, qa_ref, ka_ref, va_ref, carry):
        @pl.when(pl.program_id(0) == 0)
        def _():
            carry[...] = jnp.zeros_like(carry)

        lane, row = _iota((T, HP), 1), _iota((T, HP), 0)
        logf = jnp.where(lane < FOX_H, jax.nn.log_sigmoid(sf_ref[...] + b_ref[...]), 0.0)
        c = _dot_hi((row >= lane).astype(f32), logf) + carry[...]
        carry[...] = jnp.sum(jnp.where(row == T - 1, c, 0.0), axis=0, keepdims=True)
        ones_q = jnp.where((lane >= C_K0) & (lane < C_K0 + 3), 1.0, 0.0)
        ones_k = jnp.where(((lane >= C_Q0) & (lane < C_Q0 + 3)) | ((lane >= C_LSE0) & (lane < C_LSE0 + 3)), 1.0, 0.0)
        ones_v = _b(jnp.where((lane >= LSE_COL) & (lane < C_DELTA0 + 3), 1.0, 0.0))
        for h in range(FOX_H):
            ch = jnp.broadcast_to(jnp.sum(jnp.where(lane == h, c, 0.0), axis=1, keepdims=True), (T, HP)) * LOG2E
            c1 = _b(ch).astype(f32)
            c2 = _b(ch - c1).astype(f32)
            c3 = _b(ch - c1 - c2).astype(f32)
            cq = jnp.where(lane == C_Q0, c1, 0.0) + jnp.where(lane == C_Q0 + 1, c2, 0.0) + jnp.where(lane == C_Q0 + 2, c3, 0.0)
            ck = jnp.where(lane == C_K0, c1, 0.0) + jnp.where(lane == C_K0 + 1, c2, 0.0) + jnp.where(lane == C_K0 + 2, c3, 0.0)
            q = fq_ref[:, h * HP:(h + 1) * HP].astype(f32) * (FOX_D ** -0.5 * LOG2E)
            k = fk_ref[:, h * HP:(h + 1) * HP].astype(f32)
            qa_ref[h] = _b(q + cq + ones_q)
            ka_ref[h] = _b(k + ones_k - ck)
            va_ref[:, h * HP:(h + 1) * HP] = fv_ref[:, h * HP:(h + 1) * HP] + ones_v

    wide = pl.BlockSpec((T, W), lambda i: (i, 0))
    return pl.pallas_call(
        body, name="fox_prep", grid=(NT,),
        in_specs=[wide, wide, wide, pl.BlockSpec((T, HP), lambda i: (i, 0)), pl.BlockSpec((1, HP), lambda i: (0, 0))],
        out_specs=[pl.BlockSpec((FOX_H, T, HP), lambda i: (0, i, 0))] * 2 + [wide],
        out_shape=[SDS((FOX_H, L, HP), bf16)] * 2 + [SDS((L, W), bf16)],
        scratch_shapes=[pltpu.VMEM((1, HP), f32)], compiler_params=_cp())(fq, fk, fv, sf, bias_p)


def _fox_prep_bwd(dqa, dka, sf, bias_p):
    L = sf.shape[0]
    T = HP
    NT = L // T
    rev = lambda i: (NT - 1 - i, 0)

    W = FOX_H * HP

    def body(dq_ref, dk_ref, sf_ref, b_ref, dg_ref, db_ref, carry):
        @pl.when(pl.program_id(0) == 0)
        def _():
            carry[...] = jnp.zeros_like(carry)
            db_ref[...] = jnp.zeros_like(db_ref)

        dq, dk = dq_ref[...], dk_ref[...]
        dg_ref[:, 0:W] = _b(dq * (FOX_D ** -0.5))
        dg_ref[:, W:2 * W] = _b(dk * LN2)
        lane, row = _iota((T, HP), 1), _iota((T, HP), 0)
        dc = jnp.zeros((T, HP), f32)
        for h in range(FOX_H):
            col = jnp.sum(jnp.where(lane == C_Q0, dq[:, h * HP:(h + 1) * HP], 0.0)
                          - jnp.where(lane == C_K0, dk[:, h * HP:(h + 1) * HP], 0.0), axis=1, keepdims=True)
            dc = dc + jnp.where(lane == h, col, 0.0)
        dl = _dot_hi((row <= lane).astype(f32), dc) + carry[...]
        carry[...] = jnp.sum(jnp.where(row == 0, dl, 0.0), axis=0, keepdims=True)
        dx = jnp.where(lane < FOX_H, dl * _sigmoid(-(sf_ref[...] + b_ref[...])), 0.0)
        dg_ref[:, 2 * W:2 * W + HP] = _b(dx)
        dg_ref[:, 2 * W + HP:] = jnp.zeros((T, SMALL_W - HP), bf16)
        db_ref[0:1, :] += jnp.sum(dx, axis=0, keepdims=True)

    return pl.pallas_call(
        body, name="fox_prep_bwd", grid=(NT,),
        in_specs=[pl.BlockSpec((T, W), rev), pl.BlockSpec((T, W), rev), pl.BlockSpec((T, HP), rev), pl.BlockSpec((1, HP), lambda i: (0, 0))],
        out_specs=[pl.BlockSpec((T, 2 * W + SMALL_W), rev), pl.BlockSpec((8, HP), lambda i: (0, 0))],
        out_shape=[SDS((L, 2 * W + SMALL_W), bf16), SDS((8, HP), f32)],
        scratch_shapes=[pltpu.VMEM((1, HP), f32)], compiler_params=_cp())(dqa, dka, sf, bias_p)


def _tile_start(j, T):
    return j * T if isinstance(j, int) else pl.multiple_of(j * T, T)


def _spread3(x, lane, col0):
    x1 = _b(x).astype(f32)
    x2 = _b(x - x1).astype(f32)
    x3 = _b(x - x1 - x2).astype(f32)
    return jnp.where(lane == col0, x1, 0.0) + jnp.where(lane == col0 + 1, x2, 0.0) + jnp.where(lane == col0 + 2, x3, 0.0)


def _fox_fwd(qa, ka, fv, shards):
    L = qa.shape[1]
    TQ = TK = _pick(L, ATTN_TILES)
    NQ = L // TQ
    n = len(shards)
    HG = FOX_HEAD_GROUP_FWD

    def body(q_ref, k_ref, v_ref, *rest):
        ins, o_ref, qb_ref, outs, sems = rest[:n], rest[n], rest[n + 1], rest[n + 2:2 * n + 2], rest[2 * n + 2:]
        h, i = pl.program_id(0), pl.program_id(1)

        _exchange_start(ins, outs, *sems, gather=True, when=(h == 0) & (i == 0))

        qs = [q_ref[a] for a in range(HG)]
        rowg = i * TQ + _iota((TQ, TK), 0)
        colb = _iota((TQ, TK), 1)

        def step(j, carry, masked):
            ms, accs = carry
            k0 = _tile_start(j, TK)
            ss = [_dot_nt(qs[a], k_ref[a, pl.ds(k0, TK), :]) for a in range(HG)]
            if masked:
                colg = colb + j * TK
                keep = (colg <= rowg) & (colg >= N_PAD)
                ss = [jnp.where(keep, s, NEG) for s in ss]
            m_new = [jnp.maximum(m, jnp.max(s, axis=1, keepdims=True)) for m, s in zip(ms, ss)]
            ps = [_b(jnp.exp2(s - m)) for s, m in zip(ss, m_new)]
            alphas = [jnp.exp2(m - mn) for m, mn in zip(ms, m_new)]
            accs = [al * acc + _dot(p, v_ref[pl.ds(k0, TK), a * HP:(a + 1) * HP]) for a, (al, acc, p) in enumerate(zip(alphas, accs, ps))]
            return m_new, accs

        init = ([jnp.full((TQ, 1), NEG, f32)] * HG, [jnp.zeros((TQ, HP), f32)] * HG)
        carry = step(0, init, True)
        carry = lax.fori_loop(1, i, functools.partial(step, masked=False), carry)
        ms, accs = lax.fori_loop(jnp.maximum(i, 1), i + 1, functools.partial(step, masked=True), carry)
        lane = _iota((TQ, HP), 1)
        for a in range(HG):
            l = jnp.sum(jnp.where(lane == LSE_COL, accs[a], 0.0), axis=1, keepdims=True)
            lse = ms[a] + jnp.log2(l)
            o_ref[:, a * HP:(a + 1) * HP] = jnp.where(lane == LSE_COL, lse, accs[a] / l)
            qb_ref[a] = _b(qs[a].astype(f32) - _spread3(jnp.broadcast_to(lse, (TQ, HP)), lane, C_LSE0))

        _exchange_wait(ins, outs, *sems, gather=True, when=(h == FOX_H // HG - 1) & (i == NQ - 1))

    anyspec = pl.BlockSpec(memory_space=pl.ANY)
    qtile = pl.BlockSpec((HG, TQ, HP), lambda h, i: (h, i, 0))
    res = pl.pallas_call(
        body, name="fox_fwd", grid=(FOX_H // HG, NQ),
        in_specs=[qtile, pl.BlockSpec((HG, L, HP), lambda h, i: (h, 0, 0)), pl.BlockSpec((L, HG * HP), lambda h, i: (0, h))] + [anyspec] * n,
        out_specs=[pl.BlockSpec((TQ, HG * HP), lambda h, i: (i, h)), qtile] + [anyspec] * n,
        out_shape=[SDS((L, FOX_H * HP), f32), SDS(qa.shape, bf16)] + [SDS((N_DEV,) + a.shape, a.dtype) for a in shards],
        scratch_shapes=_exchange_sems(n), compiler_params=_cp(2))(qa, ka, fv, *shards)
    return res[0], res[1], res[2:]


def _fox_bwd(qb, ka, va, dob, slabs):
    L = qb.shape[1]
    TQ = TK = _pick(L, ATTN_TILES)
    NQ = L // TQ
    n = len(slabs)
    HG = FOX_HEAD_GROUP

    def body(q_ref, k_ref, v_ref, do_ref, *rest):
        ins, (dq_ref, dk_ref, dv_ref), outs, sems = rest[:n], rest[n:n + 3], rest[n + 3:2 * n + 3], rest[2 * n + 3:]
        h, j = pl.program_id(0), pl.program_id(1)
        cols = [slice(a * HP, (a + 1) * HP) for a in range(HG)]

        _exchange_start(ins, outs, *sems, gather=False, when=(h == 0) & (j == 0))

        @pl.when(j == 0)
        def _():
            dq_ref[...] = jnp.zeros_like(dq_ref)

        kts = [k_ref[a] for a in range(HG)]
        vts = [v_ref[:, cols[a]] for a in range(HG)]
        colg = j * TK + _iota((TQ, TK), 1)
        rowb = _iota((TQ, TK), 0)

        def step(i, carry, masked):
            dks, dvs = carry
            r0 = _tile_start(i, TQ)
            rows = pl.ds(r0, TQ)
            qs = [q_ref[a, rows, :] for a in range(HG)]
            ps = [jnp.exp2(_dot_nt(q, kt)) for q, kt in zip(qs, kts)]
            if masked:
                keep = (colg <= rowb + i * TQ) & (colg >= N_PAD)
                ps = [jnp.where(keep, p, 0.0) for p in ps]
            dobs = [do_ref[rows, cols[a]] for a in range(HG)]
            dvs = [dv + _dot_tn(dob, _b(p)) for dv, p, dob in zip(dvs, ps, dobs)]
            dss = [_b(p * _dot_nt(dob, vt)) for p, dob, vt in zip(ps, dobs, vts)]
            for a in range(HG):
                dq_ref[rows, cols[a]] += _dot(dss[a], kts[a])
            dks = [dk + _dot_tn(q, ds) for dk, ds, q in zip(dks, dss, qs)]
            return dks, dvs

        zeros = [jnp.zeros((HP, TK), f32)] * HG
        carry = step(j, (zeros, zeros), True)
        split = jnp.where(j == 0, NQ, j + 1)
        carry = lax.fori_loop(j + 1, split, functools.partial(step, masked=True), carry)
        dks, dvs = lax.fori_loop(split, NQ, functools.partial(step, masked=False), carry)
        for a in range(HG):
            dk_ref[:, cols[a]] = dks[a].T
            dv_ref[:, cols[a]] = _b(dvs[a].T)

        _exchange_wait(ins, outs, *sems, gather=False, when=(h == FOX_H // HG - 1) & (j == NQ - 1))

    head = pl.BlockSpec((L, HG * HP), lambda h, j: (0, h))
    tile = pl.BlockSpec((TK, HG * HP), lambda h, j: (j, h))
    anyspec = pl.BlockSpec(memory_space=pl.ANY)
    res = pl.pallas_call(
        body, name="fox_bwd", grid=(FOX_H // HG, L // TK),
        in_specs=[pl.BlockSpec((HG, L, HP), lambda h, j: (h, 0, 0)), pl.BlockSpec((HG, TK, HP), lambda h, j: (h, j, 0)), tile, head]
        + [anyspec] * n,
        out_specs=[head, tile, tile] + [anyspec] * n,
        out_shape=[SDS((L, FOX_H * HP), f32), SDS((L, FOX_H * HP), f32), SDS((L, FOX_H * HP), bf16)] + [SDS(a.shape, a.dtype) for a in slabs],
        scratch_shapes=_exchange_sems(n), compiler_params=_cp(2))(qb, ka, va, dob, *slabs)
    return res[:3], res[3:]


def _dn_post(y, sd, alog_p, dt_p, valid):
    a = y * _sigmoid(y)
    W = DN_H * DN_D
    heads = []
    for part, scale in ((0, DN_D ** -0.5), (1, 1.0)):
        for h in range(DN_H):
            xh = a[:, part * W + h * DN_D:part * W + (h + 1) * DN_D]
            heads.append(xh * lax.rsqrt(jnp.sum(xh * xh, axis=-1, keepdims=True) + EPS) * scale)
    q = jnp.concatenate(heads[:DN_H], axis=1)
    k = jnp.concatenate(heads[DN_H:], axis=1)
    v = a[:, 2 * W:3 * W]
    lane = _iota(sd.shape, 1)
    beta = _sigmoid(sd) * valid
    g = -jnp.exp(alog_p) * jax.nn.softplus(sd + dt_p) * valid
    bg = jnp.where(lane < DN_H, beta, jnp.where(lane < 2 * DN_H, g, 0.0))
    return q, k, v, bg


def _conv_fwd(ext_ref, cw_ref, TM):
    y = cw_ref[0:1, :] * ext_ref[8 - (CONV_K - 1):8 - (CONV_K - 1) + TM, :]
    for i in range(1, CONV_K):
        o = 8 - (CONV_K - 1) + i
        y = y + cw_ref[i:i + 1, :] * ext_ref[o:o + TM, :]
    return y


def _dn_prep(dn, sd, cw, alog_p, dt_p):
    L, W3 = dn.shape
    TM = _pick(L, ROW_TILES)
    W = DN_H * DN_D

    def body(dn_ref, halo_ref, sd_ref, cw_ref, al_ref, dt_ref, q_ref, k_ref, v_ref, bg_ref, ext):
        i = pl.program_id(0)
        ext[0:8, :] = jnp.where(i == 0, 0.0, halo_ref[...])
        ext[8:, :] = dn_ref[...]
        y = _conv_fwd(ext, cw_ref, TM)
        valid = ((i * TM + _iota((TM, 1), 0)) >= N_PAD).astype(f32)
        q, k, v, bg = _dn_post(y, sd_ref[...], al_ref[...], dt_ref[...], valid)
        q_ref[...], k_ref[...], v_ref[...], bg_ref[...] = q, k, v, bg

    row = lambda wd: pl.BlockSpec((TM, wd), lambda i: (i, 0))
    vec = pl.BlockSpec((1, HP), lambda i: (0, 0))
    return pl.pallas_call(
        body, name="dn_prep", grid=(L // TM,),
        in_specs=[row(W3), pl.BlockSpec((8, W3), lambda i: (jnp.maximum(i * (TM // 8) - 1, 0), 0)), row(HP),
                  pl.BlockSpec((CONV_K, W3), lambda i: (0, 0)), vec, vec],
        out_specs=[row(W), row(W), row(W), row(HP)],
        out_shape=[SDS((L, W), f32)] * 3 + [SDS((L, HP), f32)],
        scratch_shapes=[pltpu.VMEM((TM + 8, W3), f32)], compiler_params=_cp())(dn, dn, sd, cw, alog_p, dt_p)


def _dn_prep_bwd(dn, sd, cw, alog_p, dt_p, dq, dk, dv, dbg):
    L, W3 = dn.shape
    TM = _pick(L, ROW_TILES)
    NT = L // TM
    W = DN_H * DN_D

    def body(dn_ref, halo_ref, sd_ref, cw_ref, al_ref, dt_ref, dq_ref, dk_ref, dv_ref, dbg_ref,
             dg_ref, dcw_ref, dp_ref, ext, dyp, carry):
        i = pl.program_id(0)
        t = NT - 1 - i

        @pl.when(i == 0)
        def _():
            carry[...] = jnp.zeros_like(carry)
            dcw_ref[...] = jnp.zeros_like(dcw_ref)
            dp_ref[...] = jnp.zeros_like(dp_ref)
            dyp[...] = jnp.zeros_like(dyp)

        ext[0:8, :] = jnp.where(t == 0, 0.0, halo_ref[...])
        ext[8:, :] = dn_ref[...]
        y = _conv_fwd(ext, cw_ref, TM)
        valid = ((t * TM + _iota((TM, 1), 0)) >= N_PAD).astype(f32)
        _, vjp = jax.vjp(functools.partial(_dn_post, valid=valid), y, sd_ref[...], al_ref[...], dt_ref[...])
        dy, dsd, dal, ddt = vjp((dq_ref[...], dk_ref[...], dv_ref[...], dbg_ref[...]))
        dg_ref[:, W3:W3 + HP] = _b(dsd)
        dg_ref[:, W3 + HP:] = jnp.zeros((TM, SMALL_W - HP), bf16)
        dp_ref[0:1, :] += dal
        dp_ref[1:2, :] += ddt
        dyp[8:8 + TM, :] = dy
        o0 = CONV_K - 1
        dext = cw_ref[0:1, :] * dyp[o0:o0 + TM + 8, :]
        for k in range(1, CONV_K):
            dext = dext + cw_ref[k:k + 1, :] * dyp[o0 - k:o0 - k + TM + 8, :]
        for k in range(CONV_K):
            o = 8 - (CONV_K - 1) + k
            dcw_ref[k:k + 1, :] += jnp.sum(dy * ext[o:o + TM, :], axis=0, keepdims=True)
        dg_ref[:, 0:W3] = _b(jnp.concatenate([dext[8:TM, :], dext[TM:TM + 8, :] + carry[...]], axis=0))
        carry[...] = dext[0:8, :]

    row = lambda wd: pl.BlockSpec((TM, wd), lambda i: (NT - 1 - i, 0))
    vec = pl.BlockSpec((1, HP), lambda i: (0, 0))
    return pl.pallas_call(
        body, name="dn_prep_bwd", grid=(NT,),
        in_specs=[row(W3), pl.BlockSpec((8, W3), lambda i: (jnp.maximum((NT - 1 - i) * (TM // 8) - 1, 0), 0)), row(HP),
                  pl.BlockSpec((CONV_K, W3), lambda i: (0, 0)), vec, vec, row(W), row(W), row(W), row(HP)],
        out_specs=[row(W3 + SMALL_W), pl.BlockSpec((8, W3), lambda i: (0, 0)), pl.BlockSpec((8, HP), lambda i: (0, 0))],
        out_shape=[SDS((L, W3 + SMALL_W), bf16), SDS((8, W3), f32), SDS((8, HP), f32)],
        scratch_shapes=[pltpu.VMEM((TM + 8, W3), f32), pltpu.VMEM((TM + 16, W3), f32), pltpu.VMEM((8, W3), f32)],
        compiler_params=_cp())(dn, dn, sd, cw, alog_p, dt_p, dq, dk, dv, dbg)


def _split2(x):
    hi = _b(x)
    return hi, _b(x - hi.astype(f32))


def _split3(x):
    hi = _b(x)
    r = x - hi.astype(f32)
    mid = _b(r)
    return hi, mid, _b(r - mid.astype(f32))


def _x3(a, b, dot):
    (a1, a2), (b1, b2) = _split2(a), _split2(b)
    return dot(a1, b1) + (dot(a1, b2) + dot(a2, b1))


@jax.custom_vjp
def _dot_x3(a, b):
    return _x3(a, b, _dot)


_dot_x3.defvjp(lambda a, b: (_x3(a, b, _dot), (a, b)), lambda res, g: (_x3(g, res[1], _dot_nt), _x3(res[0], g, _dot_tn)))


def _exact3(m, x, dot):
    x1, x2, x3 = _split3(x)
    return dot(m, x1) + (dot(m, x2) + dot(m, x3))


def _tri_ones(C, lower):
    row, col = _iota((C, C), 0), _iota((C, C), 1)
    return _b(((row >= col) if lower else (row <= col)).astype(f32))


@jax.custom_vjp
def _chunk_cumsum(x):
    return _exact3(_tri_ones(x.shape[0], True), x, _dot)


_chunk_cumsum.defvjp(lambda x: (_exact3(_tri_ones(x.shape[0], True), x, _dot), None),
                     lambda _, g: (_exact3(_tri_ones(g.shape[0], False), g, _dot),))


def _mxu_transpose(x):
    C = x.shape[0]
    eye = _b((_iota((C, C), 0) == _iota((C, C), 1)).astype(f32))
    return _exact3(eye, x, lambda m, part: _dot_tn(part, m))


@jax.custom_vjp
def _transpose_exact(x):
    return _mxu_transpose(x)


_transpose_exact.defvjp(lambda x: (_mxu_transpose(x), None), lambda _, g: (_mxu_transpose(g),))


def _unit_lower_inverses(lows):
    C = lows[0].shape[0]
    P = jnp.stack(lows)
    X = (_iota((C, C), 0) == _iota((C, C), 1)).astype(f32)[None] - P
    bdot = functools.partial(_x3, dot=lambda a, b: jnp.einsum("bij,bjk->bik", a, b, preferred_element_type=f32))
    for _ in range(5):
        P = bdot(P, P)
        X = X + bdot(X, P)
    return [X[i] for i in range(len(lows))]


@jax.custom_vjp
def _inverse_given(low, X):
    return X


def _inverse_given_bwd(X, g):
    return -_x3(_x3(X, g, _dot_tn), X, _dot_nt), jnp.zeros_like(X)


_inverse_given.defvjp(lambda low, X: (X, X), _inverse_given_bwd)


def _dn_intra_pre(q, k, v, bg):
    C = DN_C
    row, col = _iota((C, C), 0), _iota((C, C), 1)
    tri = row >= col
    G = _chunk_cumsum(bg)
    GT = _transpose_exact(G)
    lane = _iota((C, HP), 1)
    rowt = _iota((HP, C), 0)
    last = _iota((C, 1), 0) == C - 1
    heads = []
    for h in range(DN_H):
        beta = jnp.sum(jnp.where(lane == h, bg, 0.0), axis=1, keepdims=True)
        gcol = jnp.sum(jnp.where(lane == DN_H + h, G, 0.0), axis=1, keepdims=True)
        grow = jnp.sum(jnp.where(rowt == DN_H + h, GT, 0.0), axis=0, keepdims=True)
        glast = jnp.sum(jnp.where(last, gcol, 0.0), axis=0, keepdims=True)
        decay = jnp.exp(jnp.where(tri, gcol - grow, NEG))
        qh, kh, vh = (t[:, h * DN_D:(h + 1) * DN_D] for t in (q, k, v))
        kb = kh * beta
        low = jnp.where(row > col, _dot_nt(_b(kb), _b(kh)) * decay, 0.0)
        heads.append((beta, gcol, glast, decay, qh, kh, vh, kb, low))
    return heads


def _dn_intra_post(heads, xs):
    lane1 = _iota((1, HP), 1)
    us, ws, qds, kds, attns = [], [], [], [], []
    glrow = jnp.zeros((1, HP), f32)
    for h, ((beta, gcol, glast, decay, qh, kh, vh, kb, _), X) in enumerate(zip(heads, xs)):
        eg = jnp.exp(gcol)
        us.append(_dot_x3(X, vh * beta))
        ws.append(_dot_x3(X, kb * eg))
        attns.append(_dot_nt(_b(qh), _b(kh)) * decay)
        qds.append(qh * eg)
        kds.append(kh * jnp.exp(glast - gcol))
        glrow = glrow + jnp.where(lane1 == h, glast, 0.0)
    cat = lambda xs_: jnp.concatenate(xs_, axis=1)
    return cat(us), cat(ws), cat(qds), cat(kds), cat(attns), glrow, cat(list(xs))


def _dn_intra_group(q, k, v, bg, xs):
    G = q.shape[0] // DN_C
    rows = [slice(j * DN_C, (j + 1) * DN_C) for j in range(G)]
    pre = [_dn_intra_pre(q[r, :], k[r, :], v[r, :], bg[r, :]) for r in rows]
    inv = [[_inverse_given(hd[-1], x) for hd, x in zip(heads, xj)] for heads, xj in zip(pre, xs)]
    post = [_dn_intra_post(heads, xj) for heads, xj in zip(pre, inv)]
    return tuple(jnp.concatenate([p[i] for p in post], axis=0) for i in range(5)) + (tuple(p[5] for p in post),)


def _lane_pick(rowvec, h):
    return jnp.sum(jnp.where(_iota(rowvec.shape, 1) == h, rowvec, 0.0), axis=1, keepdims=True)


def _dn_intra(q, k, v, bg, shards):
    L, W = q.shape
    NC = L // DN_C
    G = _pick(NC, DN_INTRA_GROUP)
    R = G * DN_C
    NS = NC // G
    WA = DN_H * DN_C
    n = len(shards)

    def body(q_ref, k_ref, v_ref, bg_ref, *rest):
        ins, (u_ref, w_ref, qd_ref, kd_ref, at_ref, gl_ref, x_ref), gouts, sems = rest[:n], rest[n:n + 7], rest[n + 7:2 * n + 7], rest[2 * n + 7:]
        _exchange_start(ins, gouts, *sems, gather=True, when=pl.program_id(0) == 0)
        rows = [slice(j * DN_C, (j + 1) * DN_C) for j in range(G)]
        pre = [_dn_intra_pre(q_ref[r, :], k_ref[r, :], v_ref[r, :], bg_ref[r, :]) for r in rows]
        inv = _unit_lower_inverses([hd[-1] for heads in pre for hd in heads])
        for j, r in enumerate(rows):
            u, w, qd, kd, at, gl, xs = _dn_intra_post(pre[j], inv[j * DN_H:(j + 1) * DN_H])
            u_ref[r, :], x_ref[r, :] = u, xs
            w_ref[r, :], qd_ref[r, :], kd_ref[r, :], at_ref[r, :] = _b(w), _b(qd), _b(kd), _b(at)
            gl_ref[j] = gl
        _exchange_wait(ins, gouts, *sems, gather=True, when=pl.program_id(0) == NS - 1)

    row = lambda wd: pl.BlockSpec((R, wd), lambda s: (s, 0))
    anyspec = pl.BlockSpec(memory_space=pl.ANY)
    res = pl.pallas_call(
        body, name="dn_intra", grid=(NS,),
        in_specs=[row(W), row(W), row(W), row(HP)] + [anyspec] * n,
        out_specs=[row(W), row(W), row(W), row(W), row(WA), pl.BlockSpec((G, 1, HP), lambda s: (s, 0, 0)), row(WA)] + [anyspec] * n,
        out_shape=[SDS((L, W), f32), SDS((L, W), bf16), SDS((L, W), bf16), SDS((L, W), bf16), SDS((L, WA), bf16), SDS((NC, 1, HP), f32),
                   SDS((L, WA), f32)] + [SDS((N_DEV,) + a.shape, a.dtype) for a in shards],
        scratch_shapes=_exchange_sems(n), compiler_params=_cp())(q, k, v, bg, *shards)
    return res[:7], res[7:]


def _dn_scan(u, w, qd, kd, at, gl):
    L, W = u.shape
    NC = L // DN_C
    G = _pick(NC, DN_SCAN_GROUP)
    R = G * DN_C

    def body(u_ref, w_ref, qd_ref, kd_ref, at_ref, gl_ref, o_ref, vn_ref, s_ref, S):
        @pl.when(pl.program_id(0) == 0)
        def _():
            S[...] = jnp.zeros_like(S)

        for j in range(G):
            r = slice(j * DN_C, (j + 1) * DN_C)
            glrow = gl_ref[j]
            for h in range(DN_H):
                c = slice(h * DN_D, (h + 1) * DN_D)
                Sh = S[h]
                s_ref[j, h] = Sh
                Sb = _b(Sh)
                vb = _b(u_ref[r, c] - _dot(w_ref[r, c], Sb))
                vn_ref[r, c] = vb
                o_ref[r, c] = _dot(qd_ref[r, c], Sb) + _dot(at_ref[r, h * DN_C:(h + 1) * DN_C], vb)
                S[h] = Sh * jnp.exp(_lane_pick(glrow, h)) + _dot_tn(kd_ref[r, c], vb)

    row = lambda wd: pl.BlockSpec((R, wd), lambda n: (n, 0))
    return pl.pallas_call(
        body, name="dn_scan", grid=(NC // G,),
        in_specs=[row(W), row(W), row(W), row(W), row(DN_H * DN_C), pl.BlockSpec((G, 1, HP), lambda n: (n, 0, 0))],
        out_specs=[row(W), row(W), pl.BlockSpec((G, DN_H, DN_D, DN_D), lambda n: (n, 0, 0, 0))],
        out_shape=[SDS((L, W), f32), SDS((L, W), bf16), SDS((NC, DN_H, DN_D, DN_D), f32)],
        scratch_shapes=[pltpu.VMEM((DN_H, DN_D, DN_D), f32)], compiler_params=_cp())(u, w, qd, kd, at, gl)


def _dn_scan_bwd(do, w, qd, kd, at, gl):
    L, W = do.shape
    NC = L // DN_C
    G = _pick(NC, DN_SCAN_GROUP)
    R = G * DN_C
    NS = NC // G

    def body(do_ref, w_ref, qd_ref, kd_ref, at_ref, gl_ref, dvn_ref, ds_ref, dS):
        @pl.when(pl.program_id(0) == 0)
        def _():
            dS[...] = jnp.zeros_like(dS)

        for j in reversed(range(G)):
            r = slice(j * DN_C, (j + 1) * DN_C)
            glrow = gl_ref[j]
            for h in range(DN_H):
                c = slice(h * DN_D, (h + 1) * DN_D)
                dSo = dS[h]
                ds_ref[j, h] = dSo
                dob = _b(do_ref[r, c])
                dvn = _dot_tn(at_ref[r, h * DN_C:(h + 1) * DN_C], dob) + _dot(kd_ref[r, c], _b(dSo))
                dvn_ref[r, c] = dvn
                dS[h] = _dot_tn(qd_ref[r, c], dob) + dSo * jnp.exp(_lane_pick(glrow, h)) - _dot_tn(w_ref[r, c], _b(dvn))

    row = lambda wd: pl.BlockSpec((R, wd), lambda n: (NS - 1 - n, 0))
    return pl.pallas_call(
        body, name="dn_scan_bwd", grid=(NS,),
        in_specs=[row(W), row(W), row(W), row(W), row(DN_H * DN_C), pl.BlockSpec((G, 1, HP), lambda n: (NS - 1 - n, 0, 0))],
        out_specs=[row(W), pl.BlockSpec((G, DN_H, DN_D, DN_D), lambda n: (NS - 1 - n, 0, 0, 0))],
        out_shape=[SDS((L, W), f32), SDS((NC, DN_H, DN_D, DN_D), f32)],
        scratch_shapes=[pltpu.VMEM((DN_H, DN_D, DN_D), f32)], compiler_params=_cp())(do, w, qd, kd, at, gl)


def _dn_intra_bwd(q, k, v, bg, xinv, do, vn, dvn, states, dstates):
    L, W = q.shape
    NC = L // DN_C
    G = _pick(NC, DN_INTRA_GROUP)
    R = G * DN_C

    def body(q_ref, k_ref, v_ref, bg_ref, x_ref, do_ref, vn_ref, dvn_ref, s_ref, ds_ref, dq_ref, dk_ref, dv_ref, dbg_ref):
        lane1 = _iota((1, HP), 1)
        rows = [slice(j * DN_C, (j + 1) * DN_C) for j in range(G)]
        xs = [[x_ref[r, h * DN_C:(h + 1) * DN_C] for h in range(DN_H)] for r in rows]
        fwd, vjp = jax.vjp(functools.partial(_dn_intra_group, xs=xs), q_ref[...], k_ref[...], v_ref[...], bg_ref[...])
        dws, dqds, dkds, dats, dgls = [], [], [], [], []
        for j, r in enumerate(rows):
            dw, dqd, dkd, dat = [], [], [], []
            dgl = jnp.zeros((1, HP), f32)
            for h in range(DN_H):
                c = slice(h * DN_D, (h + 1) * DN_D)
                Sh, dSo = s_ref[j, h], ds_ref[j, h]
                Sb, dob, vb = _b(Sh), _b(do_ref[r, c]), vn_ref[r, c]
                dw.append(-_dot_nt(_b(dvn_ref[r, c]), Sb))
                dqd.append(_dot_nt(dob, Sb))
                dat.append(_dot_nt(dob, vb))
                dkd.append(_dot_nt(vb, _b(dSo)))
                dcd = jnp.sum(jnp.sum(Sh * dSo, axis=1, keepdims=True), axis=0, keepdims=True)
                dgl = dgl + jnp.where(lane1 == h, dcd * jnp.exp(_lane_pick(fwd[5][j], h)), 0.0)
            cat = lambda xs_: jnp.concatenate(xs_, axis=1)
            dws.append(cat(dw)), dqds.append(cat(dqd)), dkds.append(cat(dkd)), dats.append(cat(dat)), dgls.append(dgl)
        cat0 = lambda xs_: jnp.concatenate(xs_, axis=0)
        dq, dk, dv, dbg = vjp((dvn_ref[...], cat0(dws), cat0(dqds), cat0(dkds), cat0(dats), tuple(dgls)))
        dq_ref[...], dk_ref[...], dv_ref[...], dbg_ref[...] = dq, dk, dv, dbg

    row = lambda wd: pl.BlockSpec((R, wd), lambda s: (s, 0))
    st = pl.BlockSpec((G, DN_H, DN_D, DN_D), lambda s: (s, 0, 0, 0))
    return pl.pallas_call(
        body, name="dn_intra_bwd", grid=(NC // G,),
        in_specs=[row(W), row(W), row(W), row(HP), row(DN_H * DN_C), row(W), row(W), row(W), st, st],
        out_specs=[row(W), row(W), row(W), row(HP)],
        out_shape=[SDS((L, W), f32)] * 3 + [SDS((L, HP), f32)],
        compiler_params=_cp())(q, k, v, bg, xinv, do, vn, dvn, states, dstates)


def _dn_normgate(oraw, dz, wn):
    outs = []
    for h in range(DN_H):
        sl = slice(h * DN_D, (h + 1) * DN_D)
        z = dz[:, sl]
        outs.append(_rms(oraw[:, sl], wn) * (z * _sigmoid(z)))
    return jnp.concatenate(outs, axis=1)


def _mix_fwd(op, oraw, dz, ga, gb, h0, wn, wbf, wbd, wo):
    L, D = h0.shape
    TM = _pick(L, ROW_TILES)

    def body(op_ref, or_ref, dz_ref, ga_ref, gb_ref, h0_ref, wn_ref, wbf_ref, wbd_ref, wo_ref, h1_ref):
        pf = _dot(_b(op_ref[...]), wbf_ref[...])
        pd = _dot(_b(_dn_normgate(or_ref[...], dz_ref[...], wn_ref[...])), wbd_ref[...])
        y = _sigmoid(ga_ref[...]) * pf + _sigmoid(gb_ref[...]) * pd
        h1_ref[...] = h0_ref[...] + _dot(_b(y), wo_ref[...])

    row = lambda wd: pl.BlockSpec((TM, wd), lambda i: (i, 0))
    full = lambda a: pl.BlockSpec(a.shape, lambda i: (0, 0))
    return pl.pallas_call(
        body, name="mix_fwd", grid=(L // TM,),
        in_specs=[row(op.shape[1]), row(oraw.shape[1]), row(dz.shape[1]), row(D), row(D), row(D), full(wn), full(wbf), full(wbd), full(wo)],
        out_specs=row(D), out_shape=SDS((L, D), f32), compiler_params=_cp())(op, oraw, dz, ga, gb, h0, wn, wbf, wbd, wo)


def _mix_bwd(dh1, op, oraw, dz, ga, gb, wn, wbf, wbd, wo):
    L, D = dh1.shape
    TM = _pick(L, ROW_TILES)
    WF, WD = op.shape[1], oraw.shape[1]

    def body(dh1_ref, op_ref, or_ref, dz_ref, ga_ref, gb_ref, wn_ref, wbf_ref, wbd_ref, wo_ref,
             dop_ref, dor_ref, dg_ref, af_ref, ad_ref, dpf_ref, dpd_ref, y_ref, dmix_ref, acc_ref):
        @pl.when(pl.program_id(0) == 0)
        def _():
            acc_ref[...] = jnp.zeros_like(acc_ref)

        opv = op_ref[...]
        af = _b(opv)
        ad, vjp = jax.vjp(_dn_normgate, or_ref[...], dz_ref[...], wn_ref[...])
        adb = _b(ad)
        pf, pd = _dot(af, wbf_ref[...]), _dot(adb, wbd_ref[...])
        sa, sb = _sigmoid(ga_ref[...]), _sigmoid(gb_ref[...])
        dmix = _b(dh1_ref[...])
        dy = _dot_nt(dmix, wo_ref[...])
        dpf, dpd = _b(dy * sa), _b(dy * sb)
        dor, ddz, dwn = vjp(_dot_nt(dpd, wbd_ref[...]))
        dop = _dot_nt(dpf, wbf_ref[...])
        lane = _iota((TM, HP), 1)
        for h in range(WF // HP):
            c = slice(h * HP, (h + 1) * HP)
            delta = jnp.sum(jnp.where(lane < FOX_D, dop[:, c] * opv[:, c], 0.0), axis=1, keepdims=True)
            dop_ref[:, c] = _b(dop[:, c] - _spread3(jnp.broadcast_to(delta, (TM, HP)), lane, C_DELTA0))
        dor_ref[...] = dor
        dg_ref[:, 0:WD] = _b(ddz)
        dg_ref[:, WD:WD + D] = _b(dy * pf * sa * (1.0 - sa))
        dg_ref[:, WD + D:] = _b(dy * pd * sb * (1.0 - sb))
        af_ref[...], ad_ref[...], y_ref[...] = af, adb, _b(sa * pf + sb * pd)
        dpf_ref[...], dpd_ref[...], dmix_ref[...] = dpf, dpd, dmix
        acc_ref[0:1, :] += dwn

    row = lambda wd: pl.BlockSpec((TM, wd), lambda i: (i, 0))
    full = lambda a: pl.BlockSpec(a.shape, lambda i: (0, 0))
    return pl.pallas_call(
        body, name="mix_bwd", grid=(L // TM,),
        in_specs=[row(D), row(WF), row(WD), row(WD), row(D), row(D), full(wn), full(wbf), full(wbd), full(wo)],
        out_specs=[row(WF), row(WD), row(WD + 2 * D), row(WF), row(WD), row(D), row(D), row(D), row(D),
                   pl.BlockSpec((8, HP), lambda i: (0, 0))],
        out_shape=[SDS((L, WF), bf16), SDS((L, WD), f32), SDS((L, WD + 2 * D), bf16), SDS((L, WF), bf16), SDS((L, WD), bf16),
                   SDS((L, D), bf16), SDS((L, D), bf16), SDS((L, D), bf16), SDS((L, D), bf16), SDS((8, HP), f32)],
        compiler_params=_cp())(dh1, op, oraw, dz, ga, gb, wn, wbf, wbd, wo)


def _ffn_fwd_bwd(h1, tgt, w2, wf, wgt, wut, wd):
    L, D = h1.shape
    F = wd.shape[0]
    TM = _pick(L, FFN_TILES)

    def body(h_ref, t_ref, w2_ref, wf_ref, wg_hbm, wu_hbm, wd_hbm,
             dh1_ref, xn_ref, dg_ref, du_ref, act_ref, dh2_ref, acc_ref, wg_v, wu_v, wd_v, sems):
        i = pl.program_id(0)
        _load_once([(wg_hbm, wg_v), (wu_hbm, wu_v), (wd_hbm, wd_v)], sems)

        @pl.when(i == 0)
        def _():
            acc_ref[...] = jnp.zeros_like(acc_ref)

        h1v = h_ref[...]
        xn2, vjp2 = jax.vjp(_rms, h1v, w2_ref[...])
        xb = _b(xn2)
        g, u = _dot_nt(xb, wg_v[...]), _dot_nt(xb, wu_v[...])
        sg = _sigmoid(g)
        ab = _b(g * sg * u)
        h2 = h1v + _dot(ab, wd_v[...])
        out, vjpf = jax.vjp(_rms, h2, wf_ref[...])
        valid = (i * TM + _iota((TM, 1), 0)) >= PREFIX
        diff = jnp.where(valid, out - t_ref[...], 0.0)
        loss = 0.5 * jnp.sum(jnp.sum(diff * diff, axis=1, keepdims=True), axis=0, keepdims=True) / D
        dh2, dwf = vjpf(diff * (1.0 / D))
        dh2b = _b(dh2)
        dact = _dot_nt(dh2b, wd_v[...])
        dgb = _b(dact * u * (sg * (1.0 + g * (1.0 - sg))))
        dub = _b(dact * (g * sg))
        dh1n, dw2 = vjp2(_dot(dgb, wg_v[...]) + _dot(dub, wu_v[...]))
        dh1_ref[...] = dh2 + dh1n
        xn_ref[...], dg_ref[...], du_ref[...], act_ref[...], dh2_ref[...] = xb, dgb, dub, ab, dh2b
        acc_ref[0:1, :] += dw2
        acc_ref[1:2, :] += dwf
        acc_ref[2:3, :] += jnp.broadcast_to(loss, (1, D))

    row = lambda wd_: pl.BlockSpec((TM, wd_), lambda i: (i, 0))
    vec = pl.BlockSpec((1, D), lambda i: (0, 0))
    anyspec = pl.BlockSpec(memory_space=pl.ANY)
    return pl.pallas_call(
        body, name="ffn_fwd_bwd", grid=(L // TM,),
        in_specs=[row(D), row(D), vec, vec, anyspec, anyspec, anyspec],
        out_specs=[row(D), row(D), row(F), row(F), row(F), row(D), pl.BlockSpec((8, D), lambda i: (0, 0))],
        out_shape=[SDS((L, D), f32), SDS((L, D), bf16), SDS((L, F), bf16), SDS((L, F), bf16), SDS((L, F), bf16), SDS((L, D), bf16),
                   SDS((8, D), f32)],
        scratch_shapes=[pltpu.VMEM((F, D), bf16), pltpu.VMEM((F, D), bf16), pltpu.VMEM((F, D), bf16), pltpu.SemaphoreType.DMA((3,))],
        compiler_params=_cp())(h1, tgt, w2, wf, wgt, wut, wd)


def _pad_lanes(v, n=HP):
    return jnp.pad(v.astype(f32), ((0, 0), (0, n - v.shape[1])))


def _pack_w_in(wt_full):
    D = wt_full.shape[1]
    FW, DW = FOX_H * FOX_D, DN_H * DN_D
    o = 0
    parts = {}
    for name, wd in (("fq", FW), ("fk", FW), ("fv", FW), ("fl", FOX_H), ("dn", 3 * DW), ("ba", 2 * DN_H), ("dz", DW), ("ga", D), ("gb", D)):
        parts[name] = wt_full[o:o + wd]
        o += wd
    assert o == wt_full.shape[0]
    heads = lambda w: jnp.pad(w.reshape(FOX_H, FOX_D, D), ((0, 0), (0, HP - FOX_D), (0, 0))).reshape(FOX_H * HP, D)
    small = lambda w: jnp.pad(w, ((0, SMALL_W - w.shape[0]), (0, 0)))
    packed = dict(fq=heads(parts["fq"]), fk=heads(parts["fk"]), fv=heads(parts["fv"]), sf=small(parts["fl"]), sd=small(parts["ba"]),
                  dn=parts["dn"], dz=parts["dz"], ga=parts["ga"], gb=parts["gb"])
    return jnp.concatenate([packed[name] for name, _, _, _ in _seg_layout(D)], axis=0)


def _unpack_w_in(groups, d_model):
    D = groups[0].shape[1]
    FW = FOX_H * FOX_D
    segs = {}
    for grp, g in zip(GROUPS, groups):
        o = 0
        for name, wd, _, sg in _seg_layout(d_model):
            if sg == grp:
                segs[name] = g[o:o + wd]
                o += wd
    heads = lambda g: g.reshape(FOX_H, HP, D)[:, :FOX_D].reshape(FW, D)
    return jnp.concatenate([heads(segs["fq"]), heads(segs["fk"]), heads(segs["fv"]), segs["sf"][:FOX_H], segs["dn"],
                            segs["sd"][:2 * DN_H], segs["dz"], segs["ga"], segs["gb"]], axis=0)


def _local_step(x, tgt, meta, w1, w_in_t, fbias, cw, alog, dtb, wn, w2, wf, late_shards):
    T, D = x.shape
    h0 = jnp.concatenate([jnp.zeros((N_PAD, D), f32), meta, x], axis=0)
    tgt_p = jnp.concatenate([jnp.zeros((PREFIX, D), f32), tgt], axis=0)
    wp = _pack_w_in(w_in_t)
    bias_p, alog_p, dt_p = _pad_lanes(fbias), _pad_lanes(jnp.pad(alog, ((0, 0), (DN_H, 0)))), _pad_lanes(jnp.pad(dtb, ((0, 0), (DN_H, 0))))

    (xn, fq, fk, sf, fv, dn, sd, dz, ga, gb), g_mix = _in_proj(h0, w1, wp, [late_shards[n] for n in LATE_MIX])
    qa, ka, va = _fox_prep(fq, fk, fv, sf, bias_p)
    op, qb, g_ffn = _fox_fwd(qa, ka, va, [late_shards[n] for n in LATE_FFN])
    qn, kn, vn, bg = _dn_prep(dn, sd, cw, alog_p, dt_p)
    (u_dn, w_dn, qd_dn, kd_dn, at_dn, gl_dn, x_dn), g_down = _dn_intra(qn, kn, vn, bg, [late_shards[n] for n in LATE_DOWN])
    full = {n: _from_slabs(n, s) for n, s in zip(LATE_MIX + LATE_FFN + LATE_DOWN, tuple(g_mix) + tuple(g_ffn) + tuple(g_down))}
    wbf, wbd, wo, wgt, wut, wd = (full[n] for n in ("w_branch_fox", "w_branch_dn", "w_out", "w_ffn_gate", "w_ffn_up", "w_ffn_down"))
    wbf_p = jnp.pad(wbf.reshape(FOX_H, FOX_D, D), ((0, 0), (0, HP - FOX_D), (0, 0))).reshape(FOX_H * HP, D)
    oraw, vnew, states = _dn_scan(u_dn, w_dn, qd_dn, kd_dn, at_dn, gl_dn)
    h1 = _mix_fwd(op, oraw, dz, ga, gb, h0, wn, wbf_p, wbd, wo)

    dh1, xn2, dgate, dup, act, dh2, acc_f = _ffn_fwd_bwd(h1, tgt_p, w2, wf, wgt, wut, wd)
    g_wg, g_wu, g_wd = _matmul_tn(dgate, xn2, "dw_ffn_gate"), _matmul_tn(dup, xn2, "dw_ffn_up"), _matmul_tn(act, dh2, "dw_ffn_down")

    dop, dor, d_mix, af, ad, dpf, dpd, yb, dmix, acc_m = _mix_bwd(dh1, op, oraw, dz, ga, gb, wn, wbf_p, wbd, wo)
    g_wbf = _matmul_tn(af, dpf, "dw_branch_fox").reshape(FOX_H, HP, D)[:, :FOX_D].reshape(FOX_H * FOX_D, D)
    g_wbd, g_wo = _matmul_tn(ad, dpd, "dw_branch_dn"), _matmul_tn(yb, dmix, "dw_out")

    dvnew, dstates = _dn_scan_bwd(dor, w_dn, qd_dn, kd_dn, at_dn, gl_dn)
    dqn, dkn, dvn, dbg = _dn_intra_bwd(qn, kn, vn, bg, x_dn, dor, vnew, dvnew, states, dstates)
    d_dn, acc_cw, acc_p = _dn_prep_bwd(dn, sd, cw, alog_p, dt_p, dqn, dkn, dvn, dbg)
    g_late = dict(w_branch_fox=g_wbf, w_branch_dn=g_wbd, w_out=g_wo, w_ffn_gate=g_wg, w_ffn_up=g_wu, w_ffn_down=g_wd)
    (dqa, dka, d_fv), recv = _fox_bwd(qb, ka, va, dop, [_to_slabs(n, g_late[n]) for n in LATE])
    d_fox, acc_b = _fox_prep_bwd(dqa, dka, sf, bias_p)

    dgroups = [d_fox, d_fv, d_dn, d_mix]
    g_wp = [_matmul_tn(dg, xn, "dw_in_" + grp) for grp, dg in zip(GROUPS, dgroups)]
    dh0, acc_1, (recv_w_in,) = _in_proj_bwd(dgroups, wp, h0, w1, dh1, [_to_slabs("w_in", _unpack_w_in(g_wp, D))])
    recv = dict(zip(LATE, recv), w_in=recv_w_in)

    small = dict(loss=acc_f[2, 0:1], mix_norm_w=acc_1[0], fox_forget_bias=acc_b[0, :FOX_H], dn_a_log=acc_p[0, DN_H:2 * DN_H],
                 dn_dt_bias=acc_p[1, DN_H:2 * DN_H], dn_out_norm_w=acc_m[0], ffn_norm_w=acc_f[0], final_norm_w=acc_f[1],
                 meta_tokens=dh0[N_PAD:PREFIX].reshape(-1), dn_conv_w=acc_cw[:CONV_K].reshape(-1))
    return dh0[PREFIX:], small, recv


def _mesh_pos():
    x, y, c = lax.axis_index("x"), lax.axis_index("y"), lax.axis_index("c")
    return x, y, c, 4 * x + 2 * y + c


def _peer(x, y, c, m):
    flip = lambda v, on: 1 - v if on else v
    px, py, pc = flip(x, m & 4), flip(y, m & 2), flip(c, m & 1)
    return (px, py, pc), 4 * px + 2 * py + pc


def _exchange_sems(n):
    return [pltpu.SemaphoreType.DMA((n, N_DEV - 1)), pltpu.SemaphoreType.DMA((n, N_DEV - 1)), pltpu.SemaphoreType.DMA((n,))]


def _exchange_part(ins, outs, send_sems, recv_sems, loc_sems, gather, m, receive):
    x, y, c, me = _mesh_pos()
    src = lambda a, pid: ins[a] if gather else ins[a].at[pid]
    if m == 0:
        return [pltpu.make_async_copy(src(a, me), outs[a].at[me], loc_sems.at[a]) for a in range(len(ins))]
    peer, pid = _peer(x, y, c, m)
    return [pltpu.make_async_remote_copy(src_ref=src(a, pid), dst_ref=outs[a].at[pid if receive else me], send_sem=send_sems.at[a, m - 1],
                                         recv_sem=recv_sems.at[a, m - 1], device_id=peer, device_id_type=MESH) for a in range(len(ins))]


def _exchange_start(*refs, gather, when):
    @pl.when(when)
    def _():
        for m in range(N_DEV):
            for cp in _exchange_part(*refs, gather, m, receive=False):
                cp.start()


def _exchange_wait(*refs, gather, when):
    @pl.when(when)
    def _():
        for m in range(1, N_DEV):
            for cp in _exchange_part(*refs, gather, m, receive=True):
                cp.wait_recv()
        for m in list(range(1, N_DEV)) + [0]:
            for cp in _exchange_part(*refs, gather, m, receive=False):
                cp.wait() if m == 0 else cp.wait_send()


def _gather_two_level(arrays, name):
    n = len(arrays)

    def body(*refs):
        ins, outs, (send_sems, recv_sems, loc_sems) = refs[:n], refs[n:2 * n], refs[2 * n:]
        x, y, c, me = _mesh_pos()
        sib = (x, y, 1 - c)
        chips = [(1 - x, y), (x, 1 - y), (1 - x, 1 - y)]
        dev_id = lambda px, py, pc: 4 * px + 2 * py + pc

        def copy(a, k, block, to, own=False):
            return pltpu.make_async_remote_copy(src_ref=ins[a] if own else outs[a].at[block], dst_ref=outs[a].at[block],
                                                send_sem=send_sems.at[a, k], recv_sem=recv_sems.at[a, k], device_id=to, device_id_type=MESH)

        local = [pltpu.make_async_copy(ins[a], outs[a].at[me], loc_sems.at[a]) for a in range(n)]
        first = [copy(a, 0, me, sib, own=True) for a in range(n)]
        first += [copy(a, 1 + j, me, (*chip, c), own=True) for j, chip in enumerate(chips) for a in range(n)]
        for cp in local + first:
            cp.start()
        passed = []
        for j, chip in enumerate(chips):
            for a in range(n):
                copy(a, 1 + j, dev_id(*chip, c), sib).wait_recv()
                cp = copy(a, 4 + j, dev_id(*chip, c), sib)
                cp.start()
                passed.append(cp)
        for a in range(n):
            copy(a, 0, dev_id(x, y, 1 - c), sib).wait_recv()
        for j, chip in enumerate(chips):
            for a in range(n):
                copy(a, 4 + j, dev_id(*chip, 1 - c), sib).wait_recv()
        for cp in first + passed:
            cp.wait_send()
        for cp in local:
            cp.wait()

    anyspec = pl.BlockSpec(memory_space=pl.ANY)
    return pl.pallas_call(
        body, name=name, in_specs=[anyspec] * n, out_specs=[anyspec] * n,
        out_shape=[SDS((N_DEV,) + a.shape, a.dtype) for a in arrays],
        scratch_shapes=_exchange_sems(n))(*arrays)


def _all_reduce_small(v):
    R = v.shape[0]

    def body(v_ref, o_ref, gath, send_sems, recv_sems):
        x, y, c, me = _mesh_pos()
        gath[me] = v_ref[...]
        sends = []
        for m in range(1, N_DEV):
            peer, _ = _peer(x, y, c, m)
            cp = pltpu.make_async_remote_copy(src_ref=v_ref, dst_ref=gath.at[me], send_sem=send_sems.at[m - 1],
                                              recv_sem=recv_sems.at[m - 1], device_id=peer, device_id_type=MESH)
            cp.start()
            sends.append(cp)
        for m in range(1, N_DEV):
            peer, pid = _peer(x, y, c, m)
            pltpu.make_async_remote_copy(src_ref=v_ref, dst_ref=gath.at[pid], send_sem=send_sems.at[m - 1],
                                         recv_sem=recv_sems.at[m - 1], device_id=peer, device_id_type=MESH).wait_recv()
        for cp in sends:
            cp.wait_send()
        tot = gath[0]
        for d in range(1, N_DEV):
            tot = tot + gath[d]
        o_ref[...] = tot

    vm = pl.BlockSpec(memory_space=pltpu.VMEM)
    return pl.pallas_call(
        body, name="all_reduce_small", in_specs=[vm], out_specs=vm, out_shape=SDS((R, HP), f32),
        scratch_shapes=[pltpu.VMEM((N_DEV, R, HP), f32), pltpu.SemaphoreType.DMA((N_DEV - 1,)), pltpu.SemaphoreType.DMA((N_DEV - 1,))],
        )(v)


def _adamw_math(w, g, m, v):
    m = ADAM_B1 * m + (1.0 - ADAM_B1) * g
    v = ADAM_B2 * v + (1.0 - ADAM_B2) * (g * g)
    m_hat = m / (1.0 - ADAM_B1 ** ADAM_STEP)
    v_hat = v / (1.0 - ADAM_B2 ** ADAM_STEP)
    return -ADAM_LR * (m_hat / (jnp.sqrt(v_hat) + ADAM_EPS) + ADAM_WD * w), m, v


def _adamw(g, w, m, v, name):
    R, Cc = w.shape[-2:]
    if R <= 512 or R % 128 == 0:
        TR, TC = (R if R <= 512 else _pick(R, (256, 128))), Cc
    else:
        TR, TC = R, _pick(Cc, (256, 128))
    slabs = g.ndim == 3
    lead = w.ndim - 2

    def body(g_ref, w_ref, m_ref, v_ref, go_ref, d_ref, mo_ref, vo_ref):
        if slabs:
            gs = g_ref[0].astype(f32)
            for k in range(1, N_DEV):
                gs = gs + g_ref[k].astype(f32)
        else:
            gs = g_ref[...]
        at = 0 if lead else Ellipsis
        d, mn, vn = _adamw_math(w_ref[at], gs, m_ref[at], v_ref[at])
        go_ref[at], d_ref[at], mo_ref[at], vo_ref[at] = gs, d, mn, vn

    grid = (R // TR, Cc // TC)
    blk = pl.BlockSpec((1,) * lead + (TR, TC), lambda i, j: (0,) * lead + (i, j))
    gblk = pl.BlockSpec((N_DEV, TR, TC), lambda i, j: (0, i, j)) if slabs else pl.BlockSpec((TR, TC), lambda i, j: (i, j))
    return pl.pallas_call(
        body, name=name, grid=grid, in_specs=[gblk, blk, blk, blk], out_specs=[blk] * 4,
        out_shape=[SDS(w.shape, f32)] * 4, compiler_params=_cp(2))(g, w, m, v)


WEIGHTS = ("meta_tokens", "mix_norm_w", "w_in", "fox_forget_bias", "dn_conv_w", "dn_a_log", "dn_dt_bias", "dn_out_norm_w",
           "w_branch_fox", "w_branch_dn", "w_out", "ffn_norm_w", "w_ffn_gate", "w_ffn_up", "w_ffn_down", "final_norm_w")
COL_SHARDED = ("w_in", "w_branch_fox", "w_branch_dn", "w_ffn_gate", "w_ffn_up")
ROW_SHARDED = ("w_out", "w_ffn_down")
BIG = COL_SHARDED + ROW_SHARDED
LATE = tuple(n for n in BIG if n != "w_in")
LATE_MIX = ("w_branch_fox", "w_branch_dn", "w_out")
LATE_FFN = ("w_ffn_gate", "w_ffn_up")
LATE_DOWN = ("w_ffn_down",)
SMALL = tuple(n for n in WEIGHTS if n not in BIG)
TRANSPOSED = ("w_in", "w_ffn_gate", "w_ffn_up")


def _to_slabs(name, g):
    r, c = g.shape
    if name in COL_SHARDED and name not in TRANSPOSED:
        return _b(g.reshape(r, N_DEV, c // N_DEV).transpose(1, 0, 2))
    return _b(g.reshape(N_DEV, r // N_DEV, c))


def _from_slabs(name, s):
    n, r, c = s.shape
    if name in COL_SHARDED and name not in TRANSPOSED:
        return s.transpose(1, 0, 2).reshape(r, n * c)
    return s.reshape(n * r, c)


def kernel(x, meta_tokens, mix_norm_w, w_in, fox_forget_bias, dn_conv_w, dn_a_log, dn_dt_bias, dn_out_norm_w, w_branch_fox, w_branch_dn, w_out, ffn_norm_w, w_ffn_gate, w_ffn_up, w_ffn_down, final_norm_w, loss_target, m_meta_tokens, m_mix_norm_w, m_w_in, m_fox_forget_bias, m_dn_conv_w, m_dn_a_log, m_dn_dt_bias, m_dn_out_norm_w, m_w_branch_fox, m_w_branch_dn, m_w_out, m_ffn_norm_w, m_w_ffn_gate, m_w_ffn_up, m_w_ffn_down, m_final_norm_w, v_meta_tokens, v_mix_norm_w, v_w_in, v_fox_forget_bias, v_dn_conv_w, v_dn_a_log, v_dn_dt_bias, v_dn_out_norm_w, v_w_branch_fox, v_w_branch_dn, v_w_out, v_ffn_norm_w, v_w_ffn_gate, v_w_ffn_up, v_w_ffn_down, v_final_norm_w):
    w = dict(meta_tokens=meta_tokens, mix_norm_w=mix_norm_w, w_in=w_in, fox_forget_bias=fox_forget_bias, dn_conv_w=dn_conv_w, dn_a_log=dn_a_log, dn_dt_bias=dn_dt_bias, dn_out_norm_w=dn_out_norm_w, w_branch_fox=w_branch_fox, w_branch_dn=w_branch_dn, w_out=w_out, ffn_norm_w=ffn_norm_w, w_ffn_gate=w_ffn_gate, w_ffn_up=w_ffn_up, w_ffn_down=w_ffn_down, final_norm_w=final_norm_w)
    mom = dict(meta_tokens=m_meta_tokens, mix_norm_w=m_mix_norm_w, w_in=m_w_in, fox_forget_bias=m_fox_forget_bias, dn_conv_w=m_dn_conv_w, dn_a_log=m_dn_a_log, dn_dt_bias=m_dn_dt_bias, dn_out_norm_w=m_dn_out_norm_w, w_branch_fox=m_w_branch_fox, w_branch_dn=m_w_branch_dn, w_out=m_w_out, ffn_norm_w=m_ffn_norm_w, w_ffn_gate=m_w_ffn_gate, w_ffn_up=m_w_ffn_up, w_ffn_down=m_w_ffn_down, final_norm_w=m_final_norm_w)
    var = dict(meta_tokens=v_meta_tokens, mix_norm_w=v_mix_norm_w, w_in=v_w_in, fox_forget_bias=v_fox_forget_bias, dn_conv_w=v_dn_conv_w, dn_a_log=v_dn_a_log, dn_dt_bias=v_dn_dt_bias, dn_out_norm_w=v_dn_out_norm_w, w_branch_fox=v_w_branch_fox, w_branch_dn=v_w_branch_dn, w_out=v_w_out, ffn_norm_w=v_ffn_norm_w, w_ffn_gate=v_w_ffn_gate, w_ffn_up=v_w_ffn_up, w_ffn_down=v_w_ffn_down, final_norm_w=v_final_norm_w)
    two_d = lambda a: a.reshape(a.shape[-2:]) if a.ndim >= 2 else a.reshape(1, -1)
    me = 4 * lax.axis_index("x") + 2 * lax.axis_index("y") + lax.axis_index("c")
    for d in (w, mom, var):
        for n in TRANSPOSED:
            d[n] = jnp.swapaxes(d[n], -1, -2)

    g_in, g_meta, g_cw = _gather_two_level([_b(two_d(w["w_in"])), two_d(w["meta_tokens"]), two_d(w["dn_conv_w"])], "all_gather_early")
    meta = g_meta.transpose(1, 0, 2).reshape(N_META, -1)
    cw = g_cw.transpose(1, 0, 2).reshape(CONV_K, -1)

    gx, g_small, recv = _local_step(
        x[0], loss_target[0], meta, two_d(w["mix_norm_w"]), _from_slabs("w_in", g_in), two_d(w["fox_forget_bias"]), cw, two_d(w["dn_a_log"]),
        two_d(w["dn_dt_bias"]), two_d(w["dn_out_norm_w"]), two_d(w["ffn_norm_w"]), two_d(w["final_norm_w"]),
        {n: _b(two_d(w[n])) for n in LATE})

    order = ("loss",) + SMALL
    flat = jnp.concatenate([g_small[n].reshape(-1) for n in order])
    rows = -(-flat.shape[0] // (8 * HP)) * 8
    tot = _all_reduce_small(jnp.pad(flat, (0, rows * HP - flat.shape[0])).reshape(rows, HP)).reshape(-1)
    summed, o = {}, 0
    for n in order:
        k = g_small[n].shape[0]
        summed[n] = tot[o:o + k]
        o += k
    loss = summed["loss"][0]
    d_model = x.shape[-1]
    mcols, ccols = d_model // N_DEV, dn_conv_w.shape[-1]
    summed["meta_tokens"] = lax.dynamic_slice(summed["meta_tokens"].reshape(N_META, d_model), (0, me * mcols), (N_META, mcols)).reshape(-1)
    summed["dn_conv_w"] = lax.dynamic_slice(summed["dn_conv_w"].reshape(CONV_K, ccols * N_DEV), (0, me * ccols), (CONV_K, ccols)).reshape(-1)

    res = {}
    for n in BIG:
        res[n] = _adamw(recv[n], w[n], mom[n], var[n], "adamw_" + n)
        if n in TRANSPOSED:
            res[n] = [jnp.swapaxes(r, -1, -2) for r in res[n]]
    sizes = [summed[n].shape[0] for n in SMALL]
    srows = -(-sum(sizes) // (8 * HP)) * 8
    pack = lambda d: jnp.pad(jnp.concatenate([d[n].reshape(-1) for n in SMALL]), (0, srows * HP - sum(sizes))).reshape(srows, HP)
    sres = _adamw(pack(summed), pack(w), pack(mom), pack(var), "adamw_small")
    o = 0
    for n, k in zip(SMALL, sizes):
        res[n] = [r.reshape(-1)[o:o + k].reshape(w[n].shape) for r in sres]
        o += k
    return (loss, gx[None], *[res[n][0] for n in WEIGHTS], *[res[n][1] for n in WEIGHTS], *[res[n][2] for n in WEIGHTS], *[res[n][3] for n in WEIGHTS])
```

```python
import functools
import math

import jax
import jax.numpy as jnp
from jax import lax
from jax.experimental import pallas as pl
from jax.experimental.pallas import tpu as pltpu

f32, bf16 = jnp.float32, jnp.bfloat16
HI = lax.Precision.HIGHEST
MESH = pl.DeviceIdType.MESH
SDS = jax.ShapeDtypeStruct

N_DEV = 8
N_META = 16
PREFIX = 128
N_PAD = PREFIX - N_META
FOX_H, FOX_D = 8, 64
DN_H, DN_D = 4, 128
DN_C = 64
CONV_K = 4
HP = 128
SMALL_W = 256
EPS = 1e-6
NEG = -1e30
C_Q0, C_K0 = 64, 67
LSE_COL = 64
LOG2E, LN2 = 1.4426950408889634, 0.6931471805599453
C_LSE0, C_DELTA0 = 70, 65

ADAM_LR, ADAM_B1, ADAM_B2, ADAM_EPS, ADAM_WD, ADAM_STEP = 0.001, 0.9, 0.999, 1e-08, 0.01, 10

VMEM_LIMIT_V7X = 56 * 1024 * 1024
ROW_TILES = (384, 128)
ATTN_TILES = (384, 128)
FFN_TILES = (192, 64)
FOX_HEAD_GROUP = 4
FOX_HEAD_GROUP_FWD = 8
ADAMW_TILES = (256, 128)
ADAMW_WHOLE_ROWS = 512
MAX_WGRAD_BLOCK = 1408
DN_INTRA_GROUP = (6, 3, 2, 1)
DN_SCAN_GROUP = (6, 3, 2, 1)


def _pick(n, cands):
    for c in cands:
        if n % c == 0:
            return c
    raise ValueError(f"no tile of {cands} divides {n}")


def _cp(n_axes=1):
    return pltpu.CompilerParams(dimension_semantics=("arbitrary",) * n_axes, vmem_limit_bytes=VMEM_LIMIT_V7X)


def _b(x):
    return x.astype(bf16)


def _dot(a, b):
    return jnp.dot(a, b, preferred_element_type=f32)


def _dot_nt(a, b):
    return lax.dot_general(a, b, (((1,), (1,)), ((), ())), preferred_element_type=f32)


def _dot_tn(a, b):
    return lax.dot_general(a, b, (((0,), (0,)), ((), ())), preferred_element_type=f32)


def _dot_hi(a, b):
    return jnp.dot(a, b, preferred_element_type=f32, precision=HI)


def _iota(shape, dim):
    return lax.broadcasted_iota(jnp.int32, shape, dim)


def _rms(x, w):
    return x * lax.rsqrt(jnp.mean(x * x, axis=-1, keepdims=True) + EPS) * w


def _sigmoid(x):
    return jax.nn.sigmoid(x)


def _load_once(pairs, sems):
    @pl.when(pl.program_id(0) == 0)
    def _():
        cps = [pltpu.make_async_copy(src, dst, sems.at[k]) for k, (src, dst) in enumerate(pairs)]
        for cp in cps:
            cp.start()
        for cp in cps:
            cp.wait()


def _seg_layout(d_model):
    return (("fq", FOX_H * HP, bf16, "fox"), ("fk", FOX_H * HP, bf16, "fox"), ("sf", SMALL_W, f32, "fox"),
            ("fv", FOX_H * HP, bf16, "fv"),
            ("dn", 3 * DN_H * DN_D, f32, "dn"), ("sd", SMALL_W, f32, "dn"),
            ("dz", DN_H * DN_D, f32, "mix"), ("ga", d_model, f32, "mix"), ("gb", d_model, f32, "mix"))


GROUPS = ("fox", "fv", "dn", "mix")


def _shifted_blocks(TM):
    g = math.gcd(TM, PREFIX)
    nb, npre = TM // g, PREFIX // g
    assert nb >= npre
    return g, nb, npre, [lambda i, j=j: (jnp.maximum(i * nb + j - npre, 0), 0) for j in range(nb)]


def _in_proj(x, pre, w1, wpt, shards):
    T, D = x.shape
    L = T + PREFIX
    NP = wpt.shape[0]
    TM = _pick(L, ROW_TILES)
    NT = L // TM
    g, nb, npre, xmaps = _shifted_blocks(TM)
    segs = _seg_layout(D)
    ns, n = len(segs), len(shards)
    offs, o = [], 0
    for _, wd, _, _ in segs:
        offs.append(o)
        o += wd
    assert o == NP

    def body(*refs):
        x_refs, (pre_ref, w1_ref, wp_hbm), rest = refs[:nb], refs[nb:nb + 3], refs[nb + 3:]
        ins, h_ref, xn_ref, outs, gouts = rest[:n], rest[n], rest[n + 1], rest[n + 2:n + 2 + ns], rest[n + 2 + ns:2 * n + 2 + ns]
        wp_v, sems = rest[2 * n + 2 + ns:2 * n + 4 + ns]
        xsems = rest[2 * n + 4 + ns:]
        _load_once([(wp_hbm, wp_v)], sems)

        _exchange_start(ins, gouts, *xsems, gather=True, when=pl.program_id(0) == 0)

        first = pl.program_id(0) == 0
        h = jnp.concatenate([jnp.where(first, pre_ref[j * g:(j + 1) * g, :], r[...]) if j < npre else r[...]
                             for j, r in enumerate(x_refs)], axis=0)
        h_ref[...] = h
        xn = _b(_rms(h, w1_ref[...]))
        xn_ref[...] = xn
        for o_ref, off, (_, wd, _, _) in zip(outs, offs, segs):
            o_ref[...] = _dot_nt(xn, wp_v[off:off + wd, :]).astype(o_ref.dtype)

        _exchange_wait(ins, gouts, *xsems, gather=True, when=pl.program_id(0) == NT - 1)

    row = lambda wd: pl.BlockSpec((TM, wd), lambda i: (i, 0))
    anyspec = pl.BlockSpec(memory_space=pl.ANY)
    res = pl.pallas_call(
        body, name="in_proj", grid=(NT,),
        in_specs=[pl.BlockSpec((g, D), m) for m in xmaps] + [pl.BlockSpec((PREFIX, D), lambda i: (0, 0)), pl.BlockSpec((1, D), lambda i: (0, 0)), anyspec]
        + [anyspec] * n,
        out_specs=[row(D), row(D)] + [row(wd) for _, wd, _, _ in segs] + [anyspec] * n,
        out_shape=[SDS((L, D), f32), SDS((L, D), bf16)] + [SDS((L, wd), dt) for _, wd, dt, _ in segs]
        + [SDS((N_DEV,) + a.shape, a.dtype) for a in shards],
        scratch_shapes=[pltpu.VMEM((NP, D), bf16), pltpu.SemaphoreType.DMA((1,))] + _exchange_sems(n),
        compiler_params=_cp())(*([x] * nb), pre, w1, wpt, *shards)
    return res[:2 + ns], res[2 + ns:]


def _in_proj_bwd(dgroups, wpt, h0, w1, dh1, slabs):
    L, D = h0.shape
    NP = wpt.shape[0]
    TM = _pick(L, ROW_TILES)
    NT = L // TM
    widths = [g.shape[1] for g in dgroups]
    assert sum(widths) == NP
    ng, n = len(dgroups), len(slabs)

    def body(*refs):
        dg_refs, (wp_hbm, h_ref, w1_ref, dh1_ref) = refs[:ng], refs[ng:ng + 4]
        ins, (dh0_ref, acc_ref), outs = refs[ng + 4:ng + 4 + n], refs[ng + 4 + n:ng + 6 + n], refs[ng + 6 + n:ng + 6 + 2 * n]
        wp_v, sems = refs[ng + 6 + 2 * n:ng + 8 + 2 * n]
        xsems = refs[ng + 8 + 2 * n:]
        _load_once([(wp_hbm, wp_v)], sems)

        @pl.when(pl.program_id(0) == 0)
        def _():
            acc_ref[...] = jnp.zeros_like(acc_ref)

        _exchange_start(ins, outs, *xsems, gather=False, when=pl.program_id(0) == 0)

        dxn, off = None, 0
        for g_ref, wd in zip(dg_refs, widths):
            part = _dot(g_ref[...], wp_v[off:off + wd, :])
            dxn = part if dxn is None else dxn + part
            off += wd
        _, vjp = jax.vjp(_rms, h_ref[...], w1_ref[...])
        dh0n, dw1 = vjp(dxn)
        dh0_ref[...] = dh1_ref[...] + dh0n
        acc_ref[0:1, :] += dw1

        _exchange_wait(ins, outs, *xsems, gather=False, when=pl.program_id(0) == NT - 1)

    row = lambda wd: pl.BlockSpec((TM, wd), lambda i: (i, 0))
    anyspec = pl.BlockSpec(memory_space=pl.ANY)
    res = pl.pallas_call(
        body, name="in_proj_bwd", grid=(NT,),
        in_specs=[row(wd) for wd in widths] + [anyspec, row(D), pl.BlockSpec((1, D), lambda i: (0, 0)), row(D)] + [anyspec] * n,
        out_specs=[row(D), pl.BlockSpec((8, D), lambda i: (0, 0))] + [anyspec] * n,
        out_shape=[SDS((L, D), f32), SDS((8, D), f32)] + [SDS(a.shape, a.dtype) for a in slabs],
        scratch_shapes=[pltpu.VMEM((NP, D), bf16), pltpu.SemaphoreType.DMA((1,))] + _exchange_sems(n),
        compiler_params=_cp())(*dgroups, wpt, h0, w1, dh1, *slabs)
    return res[0], res[1], res[2:]


def _matmul_tn(a, b, name):
    L, R = a.shape
    C = b.shape[1]
    br = max(k for k in range(HP, MAX_WGRAD_BLOCK + 1, HP) if R % k == 0)

    def body(a_ref, b_ref, o_ref):
        o_ref[...] = _b(_dot_tn(a_ref[...], b_ref[...]))

    return pl.pallas_call(
        body, name=name, grid=(R // br,),
        in_specs=[pl.BlockSpec((L, br), lambda r: (0, r)), pl.BlockSpec((L, C), lambda r: (0, 0))],
        out_specs=pl.BlockSpec((br, C), lambda r: (r, 0)), out_shape=SDS((R, C), bf16), compiler_params=_cp())(a, b)


def _fox_prep(fq, fk, fv, sf, bias_p):
    L = fq.shape[0]
    T = HP
    NT = L // T
    W = FOX_H * HP

    def body(fq_ref, fk_ref, fv_ref, sf_ref, b_ref, qa_ref, ka_ref, va_ref, carry):
        @pl.when(pl.program_id(0) == 0)
        def _():
            carry[...] = jnp.zeros_like(carry)

        lane, row = _iota((T, HP), 1), _iota((T, HP), 0)
        logf = jnp.where(lane < FOX_H, jax.nn.log_sigmoid(sf_ref[...] + b_ref[...]), 0.0)
        c = _dot_hi((row >= lane).astype(f32), logf) + carry[...]
        carry[...] = jnp.sum(jnp.where(row == T - 1, c, 0.0), axis=0, keepdims=True)
        ones_q = jnp.where((lane >= C_K0) & (lane < C_K0 + 3), 1.0, 0.0)
        ones_k = jnp.where(((lane >= C_Q0) & (lane < C_Q0 + 3)) | ((lane >= C_LSE0) & (lane < C_LSE0 + 3)), 1.0, 0.0)
        ones_v = _b(jnp.where((lane >= LSE_COL) & (lane < C_DELTA0 + 3), 1.0, 0.0))
        for h in range(FOX_H):
            ch = jnp.broadcast_to(jnp.sum(jnp.where(lane == h, c, 0.0), axis=1, keepdims=True), (T, HP)) * LOG2E
            c1 = _b(ch).astype(f32)
            c2 = _b(ch - c1).astype(f32)
            c3 = _b(ch - c1 - c2).astype(f32)
            cq = jnp.where(lane == C_Q0, c1, 0.0) + jnp.where(lane == C_Q0 + 1, c2, 0.0) + jnp.where(lane == C_Q0 + 2, c3, 0.0)
            ck = jnp.where(lane == C_K0, c1, 0.0) + jnp.where(lane == C_K0 + 1, c2, 0.0) + jnp.where(lane == C_K0 + 2, c3, 0.0)
            q = fq_ref[:, h * HP:(h + 1) * HP].astype(f32) * (FOX_D ** -0.5 * LOG2E)
            k = fk_ref[:, h * HP:(h + 1) * HP].astype(f32)
            qa_ref[h] = _b(q + cq + ones_q)
            ka_ref[h] = _b(k + ones_k - ck)
            va_ref[:, h * HP:(h + 1) * HP] = fv_ref[:, h * HP:(h + 1) * HP] + ones_v

    wide = pl.BlockSpec((T, W), lambda i: (i, 0))
    return pl.pallas_call(
        body, name="fox_prep", grid=(NT,),
        in_specs=[wide, wide, wide, pl.BlockSpec((T, HP), lambda i: (i, 0)), pl.BlockSpec((1, HP), lambda i: (0, 0))],
        out_specs=[pl.BlockSpec((FOX_H, T, HP), lambda i: (0, i, 0))] * 2 + [wide],
        out_shape=[SDS((FOX_H, L, HP), bf16)] * 2 + [SDS((L, W), bf16)],
        scratch_shapes=[pltpu.VMEM((1, HP), f32)], compiler_params=_cp())(fq, fk, fv, sf, bias_p)


def _fox_prep_bwd(dqa, dka, sf, bias_p):
    L = sf.shape[0]
    T = HP
    NT = L // T
    rev = lambda i: (NT - 1 - i, 0)

    W = FOX_H * HP

    def body(dq_ref, dk_ref, sf_ref, b_ref, dg_ref, db_ref, carry):
        @pl.when(pl.program_id(0) == 0)
        def _():
            carry[...] = jnp.zeros_like(carry)
            db_ref[...] = jnp.zeros_like(db_ref)

        dq, dk = dq_ref[...], dk_ref[...]
        dg_ref[:, 0:W] = _b(dq * (FOX_D ** -0.5))
        dg_ref[:, W:2 * W] = _b(dk * LN2)
        lane, row = _iota((T, HP), 1), _iota((T, HP), 0)
        dc = jnp.zeros((T, HP), f32)
        for h in range(FOX_H):
            col = jnp.sum(jnp.where(lane == C_Q0, dq[:, h * HP:(h + 1) * HP], 0.0)
                          - jnp.where(lane == C_K0, dk[:, h * HP:(h + 1) * HP], 0.0), axis=1, keepdims=True)
            dc = dc + jnp.where(lane == h, col, 0.0)
        dl = _dot_hi((row <= lane).astype(f32), dc) + carry[...]
        carry[...] = jnp.sum(jnp.where(row == 0, dl, 0.0), axis=0, keepdims=True)
        dx = jnp.where(lane < FOX_H, dl * _sigmoid(-(sf_ref[...] + b_ref[...])), 0.0)
        dg_ref[:, 2 * W:2 * W + HP] = _b(dx)
        dg_ref[:, 2 * W + HP:] = jnp.zeros((T, SMALL_W - HP), bf16)
        db_ref[0:1, :] += jnp.sum(dx, axis=0, keepdims=True)

    return pl.pallas_call(
        body, name="fox_prep_bwd", grid=(NT,),
        in_specs=[pl.BlockSpec((T, W), rev), pl.BlockSpec((T, W), rev), pl.BlockSpec((T, HP), rev), pl.BlockSpec((1, HP), lambda i: (0, 0))],
        out_specs=[pl.BlockSpec((T, 2 * W + SMALL_W), rev), pl.BlockSpec((8, HP), lambda i: (0, 0))],
        out_shape=[SDS((L, 2 * W + SMALL_W), bf16), SDS((8, HP), f32)],
        scratch_shapes=[pltpu.VMEM((1, HP), f32)], compiler_params=_cp())(dqa, dka, sf, bias_p)


def _tile_start(j, T):
    return j * T if isinstance(j, int) else pl.multiple_of(j * T, T)


def _spread3(x, lane, col0):
    x1 = _b(x).astype(f32)
    x2 = _b(x - x1).astype(f32)
    x3 = _b(x - x1 - x2).astype(f32)
    return jnp.where(lane == col0, x1, 0.0) + jnp.where(lane == col0 + 1, x2, 0.0) + jnp.where(lane == col0 + 2, x3, 0.0)


def _fox_fwd(qa, ka, fv, shards):
    L = qa.shape[1]
    TQ = TK = _pick(L, ATTN_TILES)
    NQ = L // TQ
    n = len(shards)
    HG = FOX_HEAD_GROUP_FWD

    def body(q_ref, k_ref, v_ref, *rest):
        ins, o_ref, qb_ref, outs, sems = rest[:n], rest[n], rest[n + 1], rest[n + 2:2 * n + 2], rest[2 * n + 2:]
        h, i = pl.program_id(0), pl.program_id(1)

        _exchange_start(ins, outs, *sems, gather=True, when=(h == 0) & (i == 0))

        qs = [q_ref[a] for a in range(HG)]
        rowg = i * TQ + _iota((TQ, TK), 0)
        colb = _iota((TQ, TK), 1)

        def step(j, carry, masked):
            ms, accs = carry
            k0 = _tile_start(j, TK)
            ss = [_dot_nt(qs[a], k_ref[a, pl.ds(k0, TK), :]) for a in range(HG)]
            if masked:
                colg = colb + j * TK
                keep = (colg <= rowg) & (colg >= N_PAD)
                ss = [jnp.where(keep, s, NEG) for s in ss]
            m_new = [jnp.maximum(m, jnp.max(s, axis=1, keepdims=True)) for m, s in zip(ms, ss)]
            ps = [_b(jnp.exp2(s - m)) for s, m in zip(ss, m_new)]
            alphas = [jnp.exp2(m - mn) for m, mn in zip(ms, m_new)]
            accs = [al * acc + _dot(p, v_ref[pl.ds(k0, TK), a * HP:(a + 1) * HP]) for a, (al, acc, p) in enumerate(zip(alphas, accs, ps))]
            return m_new, accs

        init = ([jnp.full((TQ, 1), NEG, f32)] * HG, [jnp.zeros((TQ, HP), f32)] * HG)
        carry = step(0, init, True)
        carry = lax.fori_loop(1, i, functools.partial(step, masked=False), carry)
        ms, accs = lax.fori_loop(jnp.maximum(i, 1), i + 1, functools.partial(step, masked=True), carry)
        lane = _iota((TQ, HP), 1)
        for a in range(HG):
            l = jnp.sum(jnp.where(lane == LSE_COL, accs[a], 0.0), axis=1, keepdims=True)
            lse = ms[a] + jnp.log2(l)
            o_ref[:, a * HP:(a + 1) * HP] = jnp.where(lane == LSE_COL, lse, accs[a] / l)
            qb_ref[a] = _b(qs[a].astype(f32) - _spread3(jnp.broadcast_to(lse, (TQ, HP)), lane, C_LSE0))

        _exchange_wait(ins, outs, *sems, gather=True, when=(h == FOX_H // HG - 1) & (i == NQ - 1))

    anyspec = pl.BlockSpec(memory_space=pl.ANY)
    qtile = pl.BlockSpec((HG, TQ, HP), lambda h, i: (h, i, 0))
    res = pl.pallas_call(
        body, name="fox_fwd", grid=(FOX_H // HG, NQ),
        in_specs=[qtile, pl.BlockSpec((HG, L, HP), lambda h, i: (h, 0, 0)), pl.BlockSpec((L, HG * HP), lambda h, i: (0, h))] + [anyspec] * n,
        out_specs=[pl.BlockSpec((TQ, HG * HP), lambda h, i: (i, h)), qtile] + [anyspec] * n,
        out_shape=[SDS((L, FOX_H * HP), f32), SDS(qa.shape, bf16)] + [SDS((N_DEV,) + a.shape, a.dtype) for a in shards],
        scratch_shapes=_exchange_sems(n), compiler_params=_cp(2))(qa, ka, fv, *shards)
    return res[0], res[1], res[2:]


def _fox_bwd(qb, ka, va, dob, slabs):
    L = qb.shape[1]
    TQ = TK = _pick(L, ATTN_TILES)
    NQ = L // TQ
    n = len(slabs)
    HG = FOX_HEAD_GROUP

    def body(q_ref, k_ref, v_ref, do_ref, *rest):
        ins, (dq_ref, dk_ref, dv_ref), outs, sems = rest[:n], rest[n:n + 3], rest[n + 3:2 * n + 3], rest[2 * n + 3:]
        h, j = pl.program_id(0), pl.program_id(1)
        cols = [slice(a * HP, (a + 1) * HP) for a in range(HG)]

        _exchange_start(ins, outs, *sems, gather=False, when=(h == 0) & (j == 0))

        @pl.when(j == 0)
        def _():
            dq_ref[...] = jnp.zeros_like(dq_ref)

        kts = [k_ref[a] for a in range(HG)]
        vts = [v_ref[:, cols[a]] for a in range(HG)]
        colg = j * TK + _iota((TQ, TK), 1)
        rowb = _iota((TQ, TK), 0)

        def step(i, carry, masked):
            dks, dvs = carry
            r0 = _tile_start(i, TQ)
            rows = pl.ds(r0, TQ)
            qs = [q_ref[a, rows, :] for a in range(HG)]
            ps = [jnp.exp2(_dot_nt(q, kt)) for q, kt in zip(qs, kts)]
            if masked:
                keep = (colg <= rowb + i * TQ) & (colg >= N_PAD)
                ps = [jnp.where(keep, p, 0.0) for p in ps]
            dobs = [do_ref[rows, cols[a]] for a in range(HG)]
            dvs = [dv + _dot_tn(dob, _b(p)) for dv, p, dob in zip(dvs, ps, dobs)]
            dss = [_b(p * _dot_nt(dob, vt)) for p, dob, vt in zip(ps, dobs, vts)]
            for a in range(HG):
                dq_ref[rows, cols[a]] += _dot(dss[a], kts[a])
            dks = [dk + _dot_tn(q, ds) for dk, ds, q in zip(dks, dss, qs)]
            return dks, dvs

        zeros = [jnp.zeros((HP, TK), f32)] * HG
        carry = step(j, (zeros, zeros), True)
        split = jnp.where(j == 0, NQ, j + 1)
        carry = lax.fori_loop(j + 1, split, functools.partial(step, masked=True), carry)
        dks, dvs = lax.fori_loop(split, NQ, functools.partial(step, masked=False), carry)
        for a in range(HG):
            dk_ref[:, cols[a]] = dks[a].T
            dv_ref[:, cols[a]] = _b(dvs[a].T)

        _exchange_wait(ins, outs, *sems, gather=False, when=(h == FOX_H // HG - 1) & (j == NQ - 1))

    head = pl.BlockSpec((L, HG * HP), lambda h, j: (0, h))
    tile = pl.BlockSpec((TK, HG * HP), lambda h, j: (j, h))
    anyspec = pl.BlockSpec(memory_space=pl.ANY)
    res = pl.pallas_call(
        body, name="fox_bwd", grid=(FOX_H // HG, L // TK),
        in_specs=[pl.BlockSpec((HG, L, HP), lambda h, j: (h, 0, 0)), pl.BlockSpec((HG, TK, HP), lambda h, j: (h, j, 0)), tile, head]
        + [anyspec] * n,
        out_specs=[head, tile, tile] + [anyspec] * n,
        out_shape=[SDS((L, FOX_H * HP), f32), SDS((L, FOX_H * HP), f32), SDS((L, FOX_H * HP), bf16)] + [SDS(a.shape, a.dtype) for a in slabs],
        scratch_shapes=_exchange_sems(n), compiler_params=_cp(2))(qb, ka, va, dob, *slabs)
    return res[:3], res[3:]


def _dn_post(y, sd, alog_p, dt_p, valid):
    a = y * _sigmoid(y)
    W = DN_H * DN_D
    heads = []
    for part, scale in ((0, DN_D ** -0.5), (1, 1.0)):
        for h in range(DN_H):
            xh = a[:, part * W + h * DN_D:part * W + (h + 1) * DN_D]
            heads.append(xh * lax.rsqrt(jnp.sum(xh * xh, axis=-1, keepdims=True) + EPS) * scale)
    q = jnp.concatenate(heads[:DN_H], axis=1)
    k = jnp.concatenate(heads[DN_H:], axis=1)
    v = a[:, 2 * W:3 * W]
    lane = _iota(sd.shape, 1)
    beta = _sigmoid(sd) * valid
    g = -jnp.exp(alog_p) * jax.nn.softplus(sd + dt_p) * valid
    bg = jnp.where(lane < DN_H, beta, jnp.where(lane < 2 * DN_H, g, 0.0))
    return q, k, v, bg


def _conv_fwd(ext_ref, cw_ref, TM):
    y = cw_ref[0:1, :] * ext_ref[8 - (CONV_K - 1):8 - (CONV_K - 1) + TM, :]
    for i in range(1, CONV_K):
        o = 8 - (CONV_K - 1) + i
        y = y + cw_ref[i:i + 1, :] * ext_ref[o:o + TM, :]
    return y


def _dn_prep(dn, sd, cw, alog_p, dt_p):
    L, W3 = dn.shape
    TM = _pick(L, ROW_TILES)
    W = DN_H * DN_D

    def body(dn_ref, halo_ref, sd_ref, cw_ref, al_ref, dt_ref, q_ref, k_ref, v_ref, bg_ref, ext):
        i = pl.program_id(0)
        ext[0:8, :] = jnp.where(i == 0, 0.0, halo_ref[...])
        ext[8:, :] = dn_ref[...]
        y = _conv_fwd(ext, cw_ref, TM)
        valid = ((i * TM + _iota((TM, 1), 0)) >= N_PAD).astype(f32)
        q, k, v, bg = _dn_post(y, sd_ref[...], al_ref[...], dt_ref[...], valid)
        q_ref[...], k_ref[...], v_ref[...], bg_ref[...] = q, k, v, bg

    row = lambda wd: pl.BlockSpec((TM, wd), lambda i: (i, 0))
    vec = pl.BlockSpec((1, HP), lambda i: (0, 0))
    return pl.pallas_call(
        body, name="dn_prep", grid=(L // TM,),
        in_specs=[row(W3), pl.BlockSpec((8, W3), lambda i: (jnp.maximum(i * (TM // 8) - 1, 0), 0)), row(HP),
                  pl.BlockSpec((CONV_K, W3), lambda i: (0, 0)), vec, vec],
        out_specs=[row(W), row(W), row(W), row(HP)],
        out_shape=[SDS((L, W), f32)] * 3 + [SDS((L, HP), f32)],
        scratch_shapes=[pltpu.VMEM((TM + 8, W3), f32)], compiler_params=_cp())(dn, dn, sd, cw, alog_p, dt_p)


def _dn_prep_bwd(dn, sd, cw, alog_p, dt_p, dq, dk, dv, dbg):
    L, W3 = dn.shape
    TM = _pick(L, ROW_TILES)
    NT = L // TM
    W = DN_H * DN_D

    def body(dn_ref, halo_ref, sd_ref, cw_ref, al_ref, dt_ref, dq_ref, dk_ref, dv_ref, dbg_ref,
             dg_ref, dcw_ref, dp_ref, ext, dyp, carry):
        i = pl.program_id(0)
        t = NT - 1 - i

        @pl.when(i == 0)
        def _():
            carry[...] = jnp.zeros_like(carry)
            dcw_ref[...] = jnp.zeros_like(dcw_ref)
            dp_ref[...] = jnp.zeros_like(dp_ref)
            dyp[...] = jnp.zeros_like(dyp)

        ext[0:8, :] = jnp.where(t == 0, 0.0, halo_ref[...])
        ext[8:, :] = dn_ref[...]
        y = _conv_fwd(ext, cw_ref, TM)
        valid = ((t * TM + _iota((TM, 1), 0)) >= N_PAD).astype(f32)
        _, vjp = jax.vjp(functools.partial(_dn_post, valid=valid), y, sd_ref[...], al_ref[...], dt_ref[...])
        dy, dsd, dal, ddt = vjp((dq_ref[...], dk_ref[...], dv_ref[...], dbg_ref[...]))
        dg_ref[:, W3:W3 + HP] = _b(dsd)
        dg_ref[:, W3 + HP:] = jnp.zeros((TM, SMALL_W - HP), bf16)
        dp_ref[0:1, :] += dal
        dp_ref[1:2, :] += ddt
        dyp[8:8 + TM, :] = dy
        o0 = CONV_K - 1
        dext = cw_ref[0:1, :] * dyp[o0:o0 + TM + 8, :]
        for k in range(1, CONV_K):
            dext = dext + cw_ref[k:k + 1, :] * dyp[o0 - k:o0 - k + TM + 8, :]
        for k in range(CONV_K):
            o = 8 - (CONV_K - 1) + k
            dcw_ref[k:k + 1, :] += jnp.sum(dy * ext[o:o + TM, :], axis=0, keepdims=True)
        dg_ref[:, 0:W3] = _b(jnp.concatenate([dext[8:TM, :], dext[TM:TM + 8, :] + carry[...]], axis=0))
        carry[...] = dext[0:8, :]

    row = lambda wd: pl.BlockSpec((TM, wd), lambda i: (NT - 1 - i, 0))
    vec = pl.BlockSpec((1, HP), lambda i: (0, 0))
    return pl.pallas_call(
        body, name="dn_prep_bwd", grid=(NT,),
        in_specs=[row(W3), pl.BlockSpec((8, W3), lambda i: (jnp.maximum((NT - 1 - i) * (TM // 8) - 1, 0), 0)), row(HP),
                  pl.BlockSpec((CONV_K, W3), lambda i: (0, 0)), vec, vec, row(W), row(W), row(W), row(HP)],
        out_specs=[row(W3 + SMALL_W), pl.BlockSpec((8, W3), lambda i: (0, 0)), pl.BlockSpec((8, HP), lambda i: (0, 0))],
        out_shape=[SDS((L, W3 + SMALL_W), bf16), SDS((8, W3), f32), SDS((8, HP), f32)],
        scratch_shapes=[pltpu.VMEM((TM + 8, W3), f32), pltpu.VMEM((TM + 16, W3), f32), pltpu.VMEM((8, W3), f32)],
        compiler_params=_cp())(dn, dn, sd, cw, alog_p, dt_p, dq, dk, dv, dbg)


def _split2(x):
    hi = _b(x)
    return hi, _b(x - hi.astype(f32))


def _split3(x):
    hi = _b(x)
    r = x - hi.astype(f32)
    mid = _b(r)
    return hi, mid, _b(r - mid.astype(f32))


def _x3(a, b, dot):
    (a1, a2), (b1, b2) = _split2(a), _split2(b)
    return dot(a1, b1) + (dot(a1, b2) + dot(a2, b1))


@jax.custom_vjp
def _dot_x3(a, b):
    return _x3(a, b, _dot)


_dot_x3.defvjp(lambda a, b: (_x3(a, b, _dot), (a, b)), lambda res, g: (_x3(g, res[1], _dot_nt), _x3(res[0], g, _dot_tn)))


def _exact3(m, x, dot):
    x1, x2, x3 = _split3(x)
    return dot(m, x1) + (dot(m, x2) + dot(m, x3))


def _tri_ones(C, lower):
    row, col = _iota((C, C), 0), _iota((C, C), 1)
    return _b(((row >= col) if lower else (row <= col)).astype(f32))


@jax.custom_vjp
def _chunk_cumsum(x):
    return _exact3(_tri_ones(x.shape[0], True), x, _dot)


_chunk_cumsum.defvjp(lambda x: (_exact3(_tri_ones(x.shape[0], True), x, _dot), None),
                     lambda _, g: (_exact3(_tri_ones(g.shape[0], False), g, _dot),))


def _mxu_transpose(x):
    C = x.shape[0]
    eye = _b((_iota((C, C), 0) == _iota((C, C), 1)).astype(f32))
    return _exact3(eye, x, lambda m, part: _dot_tn(part, m))


@jax.custom_vjp
def _transpose_exact(x):
    return _mxu_transpose(x)


_transpose_exact.defvjp(lambda x: (_mxu_transpose(x), None), lambda _, g: (_mxu_transpose(g),))


def _unit_lower_inverses(lows):
    C = lows[0].shape[0]
    P = jnp.stack(lows)
    X = (_iota((C, C), 0) == _iota((C, C), 1)).astype(f32)[None] - P
    bdot = functools.partial(_x3, dot=lambda a, b: jnp.einsum("bij,bjk->bik", a, b, preferred_element_type=f32))
    for _ in range(5):
        P = bdot(P, P)
        X = X + bdot(X, P)
    return [X[i] for i in range(len(lows))]


@jax.custom_vjp
def _inverse_given(low, X):
    return X


def _inverse_given_bwd(X, g):
    return -_x3(_x3(X, g, _dot_tn), X, _dot_nt), jnp.zeros_like(X)


_inverse_given.defvjp(lambda low, X: (X, X), _inverse_given_bwd)


def _dn_intra_pre(q, k, v, bg):
    C = DN_C
    row, col = _iota((C, C), 0), _iota((C, C), 1)
    tri = row >= col
    G = _chunk_cumsum(bg)
    GT = _transpose_exact(G)
    lane = _iota((C, HP), 1)
    rowt = _iota((HP, C), 0)
    last = _iota((C, 1), 0) == C - 1
    heads = []
    for h in range(DN_H):
        beta = jnp.sum(jnp.where(lane == h, bg, 0.0), axis=1, keepdims=True)
        gcol = jnp.sum(jnp.where(lane == DN_H + h, G, 0.0), axis=1, keepdims=True)
        grow = jnp.sum(jnp.where(rowt == DN_H + h, GT, 0.0), axis=0, keepdims=True)
        glast = jnp.sum(jnp.where(last, gcol, 0.0), axis=0, keepdims=True)
        decay = jnp.exp(jnp.where(tri, gcol - grow, NEG))
        qh, kh, vh = (t[:, h * DN_D:(h + 1) * DN_D] for t in (q, k, v))
        kb = kh * beta
        low = jnp.where(row > col, _dot_nt(_b(kb), _b(kh)) * decay, 0.0)
        heads.append((beta, gcol, glast, decay, qh, kh, vh, kb, low))
    return heads


def _dn_intra_post(heads, xs):
    lane1 = _iota((1, HP), 1)
    us, ws, qds, kds, attns = [], [], [], [], []
    glrow = jnp.zeros((1, HP), f32)
    for h, ((beta, gcol, glast, decay, qh, kh, vh, kb, _), X) in enumerate(zip(heads, xs)):
        eg = jnp.exp(gcol)
        us.append(_dot_x3(X, vh * beta))
        ws.append(_dot_x3(X, kb * eg))
        attns.append(_dot_nt(_b(qh), _b(kh)) * decay)
        qds.append(qh * eg)
        kds.append(kh * jnp.exp(glast - gcol))
        glrow = glrow + jnp.where(lane1 == h, glast, 0.0)
    cat = lambda xs_: jnp.concatenate(xs_, axis=1)
    return cat(us), cat(ws), cat(qds), cat(kds), cat(attns), glrow, cat(list(xs))


def _dn_intra_group(q, k, v, bg, xs):
    G = q.shape[0] // DN_C
    rows = [slice(j * DN_C, (j + 1) * DN_C) for j in range(G)]
    pre = [_dn_intra_pre(q[r, :], k[r, :], v[r, :], bg[r, :]) for r in rows]
    inv = [[_inverse_given(hd[-1], x) for hd, x in zip(heads, xj)] for heads, xj in zip(pre, xs)]
    post = [_dn_intra_post(heads, xj) for heads, xj in zip(pre, inv)]
    return tuple(jnp.concatenate([p[i] for p in post], axis=0) for i in range(5)) + (tuple(p[5] for p in post),)


def _lane_pick(rowvec, h):
    return jnp.sum(jnp.where(_iota(rowvec.shape, 1) == h, rowvec, 0.0), axis=1, keepdims=True)


def _dn_intra(q, k, v, bg, shards):
    L, W = q.shape
    NC = L // DN_C
    G = _pick(NC, DN_INTRA_GROUP)
    R = G * DN_C
    NS = NC // G
    WA = DN_H * DN_C
    n = len(shards)

    def body(q_ref, k_ref, v_ref, bg_ref, *rest):
        ins, (u_ref, w_ref, qd_ref, kd_ref, at_ref, gl_ref, x_ref), gouts, sems = rest[:n], rest[n:n + 7], rest[n + 7:2 * n + 7], rest[2 * n + 7:]
        _exchange_start(ins, gouts, *sems, gather=True, when=pl.program_id(0) == 0)
        rows = [slice(j * DN_C, (j + 1) * DN_C) for j in range(G)]
        pre = [_dn_intra_pre(q_ref[r, :], k_ref[r, :], v_ref[r, :], bg_ref[r, :]) for r in rows]
        inv = _unit_lower_inverses([hd[-1] for heads in pre for hd in heads])
        for j, r in enumerate(rows):
            u, w, qd, kd, at, gl, xs = _dn_intra_post(pre[j], inv[j * DN_H:(j + 1) * DN_H])
            u_ref[r, :], x_ref[r, :] = u, xs
            w_ref[r, :], qd_ref[r, :], kd_ref[r, :], at_ref[r, :] = _b(w), _b(qd), _b(kd), _b(at)
            gl_ref[j] = gl
        _exchange_wait(ins, gouts, *sems, gather=True, when=pl.program_id(0) == NS - 1)

    row = lambda wd: pl.BlockSpec((R, wd), lambda s: (s, 0))
    anyspec = pl.BlockSpec(memory_space=pl.ANY)
    res = pl.pallas_call(
        body, name="dn_intra", grid=(NS,),
        in_specs=[row(W), row(W), row(W), row(HP)] + [anyspec] * n,
        out_specs=[row(W), row(W), row(W), row(W), row(WA), pl.BlockSpec((G, 1, HP), lambda s: (s, 0, 0)), row(WA)] + [anyspec] * n,
        out_shape=[SDS((L, W), f32), SDS((L, W), bf16), SDS((L, W), bf16), SDS((L, W), bf16), SDS((L, WA), bf16), SDS((NC, 1, HP), f32),
                   SDS((L, WA), f32)] + [SDS((N_DEV,) + a.shape, a.dtype) for a in shards],
        scratch_shapes=_exchange_sems(n), compiler_params=_cp())(q, k, v, bg, *shards)
    return res[:7], res[7:]


def _dn_scan(u, w, qd, kd, at, gl):
    L, W = u.shape
    NC = L // DN_C
    G = _pick(NC, DN_SCAN_GROUP)
    R = G * DN_C

    def body(u_ref, w_ref, qd_ref, kd_ref, at_ref, gl_ref, o_ref, vn_ref, s_ref, S):
        @pl.when(pl.program_id(0) == 0)
        def _():
            S[...] = jnp.zeros_like(S)

        for j in range(G):
            r = slice(j * DN_C, (j + 1) * DN_C)
            glrow = gl_ref[j]
            for h in range(DN_H):
                c = slice(h * DN_D, (h + 1) * DN_D)
                Sh = S[h]
                s_ref[j, h] = Sh
                Sb = _b(Sh)
                vb = _b(u_ref[r, c] - _dot(w_ref[r, c], Sb))
                vn_ref[r, c] = vb
                o_ref[r, c] = _dot(qd_ref[r, c], Sb) + _dot(at_ref[r, h * DN_C:(h + 1) * DN_C], vb)
                S[h] = Sh * jnp.exp(_lane_pick(glrow, h)) + _dot_tn(kd_ref[r, c], vb)

    row = lambda wd: pl.BlockSpec((R, wd), lambda n: (n, 0))
    return pl.pallas_call(
        body, name="dn_scan", grid=(NC // G,),
        in_specs=[row(W), row(W), row(W), row(W), row(DN_H * DN_C), pl.BlockSpec((G, 1, HP), lambda n: (n, 0, 0))],
        out_specs=[row(W), row(W), pl.BlockSpec((G, DN_H, DN_D, DN_D), lambda n: (n, 0, 0, 0))],
        out_shape=[SDS((L, W), f32), SDS((L, W), bf16), SDS((NC, DN_H, DN_D, DN_D), f32)],
        scratch_shapes=[pltpu.VMEM((DN_H, DN_D, DN_D), f32)], compiler_params=_cp())(u, w, qd, kd, at, gl)


def _dn_scan_bwd(do, w, qd, kd, at, gl):
    L, W = do.shape
    NC = L // DN_C
    G = _pick(NC, DN_SCAN_GROUP)
    R = G * DN_C
    NS = NC // G

    def body(do_ref, w_ref, qd_ref, kd_ref, at_ref, gl_ref, dvn_ref, ds_ref, dS):
        @pl.when(pl.program_id(0) == 0)
        def _():
            dS[...] = jnp.zeros_like(dS)

        for j in reversed(range(G)):
            r = slice(j * DN_C, (j + 1) * DN_C)
            glrow = gl_ref[j]
            for h in range(DN_H):
                c = slice(h * DN_D, (h + 1) * DN_D)
                dSo = dS[h]
                ds_ref[j, h] = dSo
                dob = _b(do_ref[r, c])
                dvn = _dot_tn(at_ref[r, h * DN_C:(h + 1) * DN_C], dob) + _dot(kd_ref[r, c], _b(dSo))
                dvn_ref[r, c] = dvn
                dS[h] = _dot_tn(qd_ref[r, c], dob) + dSo * jnp.exp(_lane_pick(glrow, h)) - _dot_tn(w_ref[r, c], _b(dvn))

    row = lambda wd: pl.BlockSpec((R, wd), lambda n: (NS - 1 - n, 0))
    return pl.pallas_call(
        body, name="dn_scan_bwd", grid=(NS,),
        in_specs=[row(W), row(W), row(W), row(W), row(DN_H * DN_C), pl.BlockSpec((G, 1, HP), lambda n: (NS - 1 - n, 0, 0))],
        out_specs=[row(W), pl.BlockSpec((G, DN_H, DN_D, DN_D), lambda n: (NS - 1 - n, 0, 0, 0))],
        out_shape=[SDS((L, W), f32), SDS((NC, DN_H, DN_D, DN_D), f32)],
        scratch_shapes=[pltpu.VMEM((DN_H, DN_D, DN_D), f32)], compiler_params=_cp())(do, w, qd, kd, at, gl)


def _dn_intra_bwd(q, k, v, bg, xinv, do, vn, dvn, states, dstates):
    L, W = q.shape
    NC = L // DN_C
    G = _pick(NC, DN_INTRA_GROUP)
    R = G * DN_C

    def body(q_ref, k_ref, v_ref, bg_ref, x_ref, do_ref, vn_ref, dvn_ref, s_ref, ds_ref, dq_ref, dk_ref, dv_ref, dbg_ref):
        lane1 = _iota((1, HP), 1)
        rows = [slice(j * DN_C, (j + 1) * DN_C) for j in range(G)]
        xs = [[x_ref[r, h * DN_C:(h + 1) * DN_C] for h in range(DN_H)] for r in rows]
        fwd, vjp = jax.vjp(functools.partial(_dn_intra_group, xs=xs), q_ref[...], k_ref[...], v_ref[...], bg_ref[...])
        dws, dqds, dkds, dats, dgls = [], [], [], [], []
        for j, r in enumerate(rows):
            dw, dqd, dkd, dat = [], [], [], []
            dgl = jnp.zeros((1, HP), f32)
            for h in range(DN_H):
                c = slice(h * DN_D, (h + 1) * DN_D)
                Sh, dSo = s_ref[j, h], ds_ref[j, h]
                Sb, dob, vb = _b(Sh), _b(do_ref[r, c]), vn_ref[r, c]
                dw.append(-_dot_nt(_b(dvn_ref[r, c]), Sb))
                dqd.append(_dot_nt(dob, Sb))
                dat.append(_dot_nt(dob, vb))
                dkd.append(_dot_nt(vb, _b(dSo)))
                dcd = jnp.sum(jnp.sum(Sh * dSo, axis=1, keepdims=True), axis=0, keepdims=True)
                dgl = dgl + jnp.where(lane1 == h, dcd * jnp.exp(_lane_pick(fwd[5][j], h)), 0.0)
            cat = lambda xs_: jnp.concatenate(xs_, axis=1)
            dws.append(cat(dw)), dqds.append(cat(dqd)), dkds.append(cat(dkd)), dats.append(cat(dat)), dgls.append(dgl)
        cat0 = lambda xs_: jnp.concatenate(xs_, axis=0)
        dq, dk, dv, dbg = vjp((dvn_ref[...], cat0(dws), cat0(dqds), cat0(dkds), cat0(dats), tuple(dgls)))
        dq_ref[...], dk_ref[...], dv_ref[...], dbg_ref[...] = dq, dk, dv, dbg

    row = lambda wd: pl.BlockSpec((R, wd), lambda s: (s, 0))
    st = pl.BlockSpec((G, DN_H, DN_D, DN_D), lambda s: (s, 0, 0, 0))
    return pl.pallas_call(
        body, name="dn_intra_bwd", grid=(NC // G,),
        in_specs=[row(W), row(W), row(W), row(HP), row(DN_H * DN_C), row(W), row(W), row(W), st, st],
        out_specs=[row(W), row(W), row(W), row(HP)],
        out_shape=[SDS((L, W), f32)] * 3 + [SDS((L, HP), f32)],
        compiler_params=_cp())(q, k, v, bg, xinv, do, vn, dvn, states, dstates)


def _dn_normgate(oraw, dz, wn):
    outs = []
    for h in range(DN_H):
        sl = slice(h * DN_D, (h + 1) * DN_D)
        z = dz[:, sl]
        outs.append(_rms(oraw[:, sl], wn) * (z * _sigmoid(z)))
    return jnp.concatenate(outs, axis=1)


def _mix_fwd(op, oraw, dz, ga, gb, h0, wn, wbf, wbd, wo):
    L, D = h0.shape
    TM = _pick(L, ROW_TILES)

    def body(op_ref, or_ref, dz_ref, ga_ref, gb_ref, h0_ref, wn_ref, wbf_ref, wbd_ref, wo_ref, h1_ref):
        pf = _dot(_b(op_ref[...]), wbf_ref[...])
        pd = _dot(_b(_dn_normgate(or_ref[...], dz_ref[...], wn_ref[...])), wbd_ref[...])
        y = _sigmoid(ga_ref[...]) * pf + _sigmoid(gb_ref[...]) * pd
        h1_ref[...] = h0_ref[...] + _dot(_b(y), wo_ref[...])

    row = lambda wd: pl.BlockSpec((TM, wd), lambda i: (i, 0))
    full = lambda a: pl.BlockSpec(a.shape, lambda i: (0, 0))
    return pl.pallas_call(
        body, name="mix_fwd", grid=(L // TM,),
        in_specs=[row(op.shape[1]), row(oraw.shape[1]), row(dz.shape[1]), row(D), row(D), row(D), full(wn), full(wbf), full(wbd), full(wo)],
        out_specs=row(D), out_shape=SDS((L, D), f32), compiler_params=_cp())(op, oraw, dz, ga, gb, h0, wn, wbf, wbd, wo)


def _mix_bwd(dh1, op, oraw, dz, ga, gb, wn, wbf, wbd, wo):
    L, D = dh1.shape
    TM = _pick(L, ROW_TILES)
    WF, WD = op.shape[1], oraw.shape[1]

    def body(dh1_ref, op_ref, or_ref, dz_ref, ga_ref, gb_ref, wn_ref, wbf_ref, wbd_ref, wo_ref,
             dop_ref, dor_ref, dg_ref, af_ref, ad_ref, dpf_ref, dpd_ref, y_ref, dmix_ref, acc_ref):
        @pl.when(pl.program_id(0) == 0)
        def _():
            acc_ref[...] = jnp.zeros_like(acc_ref)

        opv = op_ref[...]
        af = _b(opv)
        ad, vjp = jax.vjp(_dn_normgate, or_ref[...], dz_ref[...], wn_ref[...])
        adb = _b(ad)
        pf, pd = _dot(af, wbf_ref[...]), _dot(adb, wbd_ref[...])
        sa, sb = _sigmoid(ga_ref[...]), _sigmoid(gb_ref[...])
        dmix = _b(dh1_ref[...])
        dy = _dot_nt(dmix, wo_ref[...])
        dpf, dpd = _b(dy * sa), _b(dy * sb)
        dor, ddz, dwn = vjp(_dot_nt(dpd, wbd_ref[...]))
        dop = _dot_nt(dpf, wbf_ref[...])
        lane = _iota((TM, HP), 1)
        for h in range(WF // HP):
            c = slice(h * HP, (h + 1) * HP)
            delta = jnp.sum(jnp.where(lane < FOX_D, dop[:, c] * opv[:, c], 0.0), axis=1, keepdims=True)
            dop_ref[:, c] = _b(dop[:, c] - _spread3(jnp.broadcast_to(delta, (TM, HP)), lane, C_DELTA0))
        dor_ref[...] = dor
        dg_ref[:, 0:WD] = _b(ddz)
        dg_ref[:, WD:WD + D] = _b(dy * pf * sa * (1.0 - sa))
        dg_ref[:, WD + D:] = _b(dy * pd * sb * (1.0 - sb))
        af_ref[...], ad_ref[...], y_ref[...] = af, adb, _b(sa * pf + sb * pd)
        dpf_ref[...], dpd_ref[...], dmix_ref[...] = dpf, dpd, dmix
        acc_ref[0:1, :] += dwn

    row = lambda wd: pl.BlockSpec((TM, wd), lambda i: (i, 0))
    full = lambda a: pl.BlockSpec(a.shape, lambda i: (0, 0))
    return pl.pallas_call(
        body, name="mix_bwd", grid=(L // TM,),
        in_specs=[row(D), row(WF), row(WD), row(WD), row(D), row(D), full(wn), full(wbf), full(wbd), full(wo)],
        out_specs=[row(WF), row(WD), row(WD + 2 * D), row(WF), row(WD), row(D), row(D), row(D), row(D),
                   pl.BlockSpec((8, HP), lambda i: (0, 0))],
        out_shape=[SDS((L, WF), bf16), SDS((L, WD), f32), SDS((L, WD + 2 * D), bf16), SDS((L, WF), bf16), SDS((L, WD), bf16),
                   SDS((L, D), bf16), SDS((L, D), bf16), SDS((L, D), bf16), SDS((L, D), bf16), SDS((8, HP), f32)],
        compiler_params=_cp())(dh1, op, oraw, dz, ga, gb, wn, wbf, wbd, wo)


def _ffn_fwd_bwd(h1, tgt, w2, wf, wgt, wut, wd):
    L, D = h1.shape
    F = wd.shape[0]
    TM = _pick(L, FFN_TILES)
    gt, nb, _, tmaps = _shifted_blocks(TM)

    def body(h_ref, *refs):
        t_refs, (w2_ref, wf_ref, wg_hbm, wu_hbm, wd_hbm,
                 dh1_ref, xn_ref, dg_ref, du_ref, act_ref, dh2_ref, acc_ref, wg_v, wu_v, wd_v, sems) = refs[:nb], refs[nb:]
        i = pl.program_id(0)
        _load_once([(wg_hbm, wg_v), (wu_hbm, wu_v), (wd_hbm, wd_v)], sems)

        @pl.when(i == 0)
        def _():
            acc_ref[...] = jnp.zeros_like(acc_ref)

        h1v = h_ref[...]
        xn2, vjp2 = jax.vjp(_rms, h1v, w2_ref[...])
        xb = _b(xn2)
        g, u = _dot_nt(xb, wg_v[...]), _dot_nt(xb, wu_v[...])
        sg = _sigmoid(g)
        ab = _b(g * sg * u)
        h2 = h1v + _dot(ab, wd_v[...])
        out, vjpf = jax.vjp(_rms, h2, wf_ref[...])
        valid = (i * TM + _iota((TM, 1), 0)) >= PREFIX
        diff = jnp.where(valid, out - jnp.concatenate([r[...] for r in t_refs], axis=0), 0.0)
        loss = 0.5 * jnp.sum(jnp.sum(diff * diff, axis=1, keepdims=True), axis=0, keepdims=True) / D
        dh2, dwf = vjpf(diff * (1.0 / D))
        dh2b = _b(dh2)
        dact = _dot_nt(dh2b, wd_v[...])
        dgb = _b(dact * u * (sg * (1.0 + g * (1.0 - sg))))
        dub = _b(dact * (g * sg))
        dh1n, dw2 = vjp2(_dot(dgb, wg_v[...]) + _dot(dub, wu_v[...]))
        dh1_ref[...] = dh2 + dh1n
        xn_ref[...], dg_ref[...], du_ref[...], act_ref[...], dh2_ref[...] = xb, dgb, dub, ab, dh2b
        acc_ref[0:1, :] += dw2
        acc_ref[1:2, :] += dwf
        acc_ref[2:3, :] += jnp.broadcast_to(loss, (1, D))

    row = lambda wd_: pl.BlockSpec((TM, wd_), lambda i: (i, 0))
    vec = pl.BlockSpec((1, D), lambda i: (0, 0))
    anyspec = pl.BlockSpec(memory_space=pl.ANY)
    return pl.pallas_call(
        body, name="ffn_fwd_bwd", grid=(L // TM,),
        in_specs=[row(D)] + [pl.BlockSpec((gt, D), m) for m in tmaps] + [vec, vec, anyspec, anyspec, anyspec],
        out_specs=[row(D), row(D), row(F), row(F), row(F), row(D), pl.BlockSpec((8, D), lambda i: (0, 0))],
        out_shape=[SDS((L, D), f32), SDS((L, D), bf16), SDS((L, F), bf16), SDS((L, F), bf16), SDS((L, F), bf16), SDS((L, D), bf16),
                   SDS((8, D), f32)],
        scratch_shapes=[pltpu.VMEM((F, D), bf16), pltpu.VMEM((F, D), bf16), pltpu.VMEM((F, D), bf16), pltpu.SemaphoreType.DMA((3,))],
        compiler_params=_cp())(h1, *([tgt] * nb), w2, wf, wgt, wut, wd)


def _pad_lanes(v, n=HP):
    return jnp.pad(v.astype(f32), ((0, 0), (0, n - v.shape[1])))


def _rows_of(pieces, a, b):
    out, o = [], 0
    for p in pieces:
        lo, hi = max(a, o) - o, min(b, o + p.shape[0]) - o
        if lo < hi:
            out.append(p[lo:hi])
        o += p.shape[0]
    return out[0] if len(out) == 1 else jnp.concatenate(out, axis=0)


def _pack_w_in(shards):
    D = shards.shape[2]
    FW, DW = FOX_H * FOX_D, DN_H * DN_D
    pieces = [shards[p] for p in range(shards.shape[0])]
    o = 0
    parts = {}
    for name, wd in (("fq", FW), ("fk", FW), ("fv", FW), ("fl", FOX_H), ("dn", 3 * DW), ("ba", 2 * DN_H), ("dz", DW), ("ga", D), ("gb", D)):
        parts[name] = _rows_of(pieces, o, o + wd)
        o += wd
    assert o == shards.shape[0] * shards.shape[1]
    heads = lambda w: jnp.pad(w.reshape(FOX_H, FOX_D, D), ((0, 0), (0, HP - FOX_D), (0, 0))).reshape(FOX_H * HP, D)
    small = lambda w: jnp.pad(w, ((0, SMALL_W - w.shape[0]), (0, 0)))
    packed = dict(fq=heads(parts["fq"]), fk=heads(parts["fk"]), fv=heads(parts["fv"]), sf=small(parts["fl"]), sd=small(parts["ba"]),
                  dn=parts["dn"], dz=parts["dz"], ga=parts["ga"], gb=parts["gb"])
    return jnp.concatenate([packed[name] for name, _, _, _ in _seg_layout(D)], axis=0)


def _unpack_w_in(groups, d_model):
    D = groups[0].shape[1]
    FW = FOX_H * FOX_D
    segs = {}
    for grp, g in zip(GROUPS, groups):
        o = 0
        for name, wd, _, sg in _seg_layout(d_model):
            if sg == grp:
                segs[name] = g[o:o + wd]
                o += wd
    heads = lambda g: g.reshape(FOX_H, HP, D)[:, :FOX_D].reshape(FW, D)
    pieces = [heads(segs["fq"]), heads(segs["fk"]), heads(segs["fv"]), segs["sf"][:FOX_H], segs["dn"],
              segs["sd"][:2 * DN_H], segs["dz"], segs["ga"], segs["gb"]]
    r = sum(p.shape[0] for p in pieces) // N_DEV
    return jnp.stack([_rows_of(pieces, p * r, (p + 1) * r) for p in range(N_DEV)])


def _local_step(x, tgt, meta, w1, w_in_t, fbias, cw, alog, dtb, wn, w2, wf, late_shards):
    T, D = x.shape
    pre = jnp.concatenate([jnp.zeros((N_PAD, D), f32), meta], axis=0)
    wp = _pack_w_in(w_in_t)
    bias_p, alog_p, dt_p = _pad_lanes(fbias), _pad_lanes(jnp.pad(alog, ((0, 0), (DN_H, 0)))), _pad_lanes(jnp.pad(dtb, ((0, 0), (DN_H, 0))))

    (h0, xn, fq, fk, sf, fv, dn, sd, dz, ga, gb), g_mix = _in_proj(x, pre, w1, wp, [late_shards[n] for n in LATE_MIX])
    qa, ka, va = _fox_prep(fq, fk, fv, sf, bias_p)
    op, qb, g_ffn = _fox_fwd(qa, ka, va, [late_shards[n] for n in LATE_FFN])
    qn, kn, vn, bg = _dn_prep(dn, sd, cw, alog_p, dt_p)
    (u_dn, w_dn, qd_dn, kd_dn, at_dn, gl_dn, x_dn), g_down = _dn_intra(qn, kn, vn, bg, [late_shards[n] for n in LATE_DOWN])
    full = {n: _from_slabs(n, s) for n, s in zip(LATE_MIX + LATE_FFN + LATE_DOWN, tuple(g_mix) + tuple(g_ffn) + tuple(g_down))}
    wbf, wbd, wo, wgt, wut, wd = (full[n] for n in ("w_branch_fox", "w_branch_dn", "w_out", "w_ffn_gate", "w_ffn_up", "w_ffn_down"))
    wbf_p = jnp.pad(wbf.reshape(FOX_H, FOX_D, D), ((0, 0), (0, HP - FOX_D), (0, 0))).reshape(FOX_H * HP, D)
    oraw, vnew, states = _dn_scan(u_dn, w_dn, qd_dn, kd_dn, at_dn, gl_dn)
    h1 = _mix_fwd(op, oraw, dz, ga, gb, h0, wn, wbf_p, wbd, wo)

    dh1, xn2, dgate, dup, act, dh2, acc_f = _ffn_fwd_bwd(h1, tgt, w2, wf, wgt, wut, wd)
    g_wg, g_wu, g_wd = _matmul_tn(dgate, xn2, "dw_ffn_gate"), _matmul_tn(dup, xn2, "dw_ffn_up"), _matmul_tn(act, dh2, "dw_ffn_down")

    dop, dor, d_mix, af, ad, dpf, dpd, yb, dmix, acc_m = _mix_bwd(dh1, op, oraw, dz, ga, gb, wn, wbf_p, wbd, wo)
    g_wbf = _matmul_tn(af, dpf, "dw_branch_fox").reshape(FOX_H, HP, D)[:, :FOX_D].reshape(FOX_H * FOX_D, D)
    g_wbd, g_wo = _matmul_tn(ad, dpd, "dw_branch_dn"), _matmul_tn(yb, dmix, "dw_out")

    dvnew, dstates = _dn_scan_bwd(dor, w_dn, qd_dn, kd_dn, at_dn, gl_dn)
    dqn, dkn, dvn, dbg = _dn_intra_bwd(qn, kn, vn, bg, x_dn, dor, vnew, dvnew, states, dstates)
    d_dn, acc_cw, acc_p = _dn_prep_bwd(dn, sd, cw, alog_p, dt_p, dqn, dkn, dvn, dbg)
    g_late = dict(w_branch_fox=g_wbf, w_branch_dn=g_wbd, w_out=g_wo, w_ffn_gate=g_wg, w_ffn_up=g_wu, w_ffn_down=g_wd)
    (dqa, dka, d_fv), recv = _fox_bwd(qb, ka, va, dop, [_to_slabs(n, g_late[n]) for n in LATE])
    d_fox, acc_b = _fox_prep_bwd(dqa, dka, sf, bias_p)

    dgroups = [d_fox, d_fv, d_dn, d_mix]
    g_wp = [_matmul_tn(dg, xn, "dw_in_" + grp) for grp, dg in zip(GROUPS, dgroups)]
    dh0, acc_1, (recv_w_in,) = _in_proj_bwd(dgroups, wp, h0, w1, dh1, [_unpack_w_in(g_wp, D)])
    recv = dict(zip(LATE, recv), w_in=recv_w_in)

    small = dict(loss=acc_f[2, 0:1], mix_norm_w=acc_1[0], fox_forget_bias=acc_b[0, :FOX_H], dn_a_log=acc_p[0, DN_H:2 * DN_H],
                 dn_dt_bias=acc_p[1, DN_H:2 * DN_H], dn_out_norm_w=acc_m[0], ffn_norm_w=acc_f[0], final_norm_w=acc_f[1],
                 meta_tokens=dh0[N_PAD:PREFIX].reshape(-1), dn_conv_w=acc_cw[:CONV_K].reshape(-1))
    return dh0[PREFIX:], small, recv


def _mesh_pos():
    x, y, c = lax.axis_index("x"), lax.axis_index("y"), lax.axis_index("c")
    return x, y, c, 4 * x + 2 * y + c


def _peer(x, y, c, m):
    flip = lambda v, on: 1 - v if on else v
    px, py, pc = flip(x, m & 4), flip(y, m & 2), flip(c, m & 1)
    return (px, py, pc), 4 * px + 2 * py + pc


def _exchange_sems(n):
    return [pltpu.SemaphoreType.DMA((n, N_DEV - 1)), pltpu.SemaphoreType.DMA((n, N_DEV - 1)), pltpu.SemaphoreType.DMA((n,))]


def _exchange_part(ins, outs, send_sems, recv_sems, loc_sems, gather, m, receive):
    x, y, c, me = _mesh_pos()
    src = lambda a, pid: ins[a] if gather else ins[a].at[pid]
    if m == 0:
        return [pltpu.make_async_copy(src(a, me), outs[a].at[me], loc_sems.at[a]) for a in range(len(ins))]
    peer, pid = _peer(x, y, c, m)
    return [pltpu.make_async_remote_copy(src_ref=src(a, pid), dst_ref=outs[a].at[pid if receive else me], send_sem=send_sems.at[a, m - 1],
                                         recv_sem=recv_sems.at[a, m - 1], device_id=peer, device_id_type=MESH) for a in range(len(ins))]


def _exchange_start(*refs, gather, when):
    @pl.when(when)
    def _():
        for m in range(N_DEV):
            for cp in _exchange_part(*refs, gather, m, receive=False):
                cp.start()


def _exchange_wait(*refs, gather, when):
    @pl.when(when)
    def _():
        for m in range(1, N_DEV):
            for cp in _exchange_part(*refs, gather, m, receive=True):
                cp.wait_recv()
        for m in list(range(1, N_DEV)) + [0]:
            for cp in _exchange_part(*refs, gather, m, receive=False):
                cp.wait() if m == 0 else cp.wait_send()


def _gather_two_level(arrays, name):
    n = len(arrays)

    def body(*refs):
        ins, outs, (send_sems, recv_sems, loc_sems) = refs[:n], refs[n:2 * n], refs[2 * n:]
        x, y, c, me = _mesh_pos()
        sib = (x, y, 1 - c)
        chips = [(1 - x, y), (x, 1 - y), (1 - x, 1 - y)]
        dev_id = lambda px, py, pc: 4 * px + 2 * py + pc

        def copy(a, k, block, to, own=False):
            return pltpu.make_async_remote_copy(src_ref=ins[a] if own else outs[a].at[block], dst_ref=outs[a].at[block],
                                                send_sem=send_sems.at[a, k], recv_sem=recv_sems.at[a, k], device_id=to, device_id_type=MESH)

        local = [pltpu.make_async_copy(ins[a], outs[a].at[me], loc_sems.at[a]) for a in range(n)]
        first = [copy(a, 0, me, sib, own=True) for a in range(n)]
        first += [copy(a, 1 + j, me, (*chip, c), own=True) for j, chip in enumerate(chips) for a in range(n)]
        for cp in local + first:
            cp.start()
        passed = []
        for j, chip in enumerate(chips):
            for a in range(n):
                copy(a, 1 + j, dev_id(*chip, c), sib).wait_recv()
                cp = copy(a, 4 + j, dev_id(*chip, c), sib)
                cp.start()
                passed.append(cp)
        for a in range(n):
            copy(a, 0, dev_id(x, y, 1 - c), sib).wait_recv()
        for j, chip in enumerate(chips):
            for a in range(n):
                copy(a, 4 + j, dev_id(*chip, 1 - c), sib).wait_recv()
        for cp in first + passed:
            cp.wait_send()
        for cp in local:
            cp.wait()

    anyspec = pl.BlockSpec(memory_space=pl.ANY)
    return pl.pallas_call(
        body, name=name, in_specs=[anyspec] * n, out_specs=[anyspec] * n,
        out_shape=[SDS((N_DEV,) + a.shape, a.dtype) for a in arrays],
        scratch_shapes=_exchange_sems(n))(*arrays)


def _all_reduce_small(v):
    R = v.shape[0]

    def body(v_ref, o_ref, gath, send_sems, recv_sems):
        x, y, c, me = _mesh_pos()
        gath[me] = v_ref[...]
        sends = []
        for m in range(1, N_DEV):
            peer, _ = _peer(x, y, c, m)
            cp = pltpu.make_async_remote_copy(src_ref=v_ref, dst_ref=gath.at[me], send_sem=send_sems.at[m - 1],
                                              recv_sem=recv_sems.at[m - 1], device_id=peer, device_id_type=MESH)
            cp.start()
            sends.append(cp)
        for m in range(1, N_DEV):
            peer, pid = _peer(x, y, c, m)
            pltpu.make_async_remote_copy(src_ref=v_ref, dst_ref=gath.at[pid], send_sem=send_sems.at[m - 1],
                                         recv_sem=recv_sems.at[m - 1], device_id=peer, device_id_type=MESH).wait_recv()
        for cp in sends:
            cp.wait_send()
        tot = gath[0]
        for d in range(1, N_DEV):
            tot = tot + gath[d]
        o_ref[...] = tot

    vm = pl.BlockSpec(memory_space=pltpu.VMEM)
    return pl.pallas_call(
        body, name="all_reduce_small", in_specs=[vm], out_specs=vm, out_shape=SDS((R, HP), f32),
        scratch_shapes=[pltpu.VMEM((N_DEV, R, HP), f32), pltpu.SemaphoreType.DMA((N_DEV - 1,)), pltpu.SemaphoreType.DMA((N_DEV - 1,))],
        )(v)


def _adamw_math(w, g, m, v):
    m = ADAM_B1 * m + (1.0 - ADAM_B1) * g
    v = ADAM_B2 * v + (1.0 - ADAM_B2) * (g * g)
    m_hat = m / (1.0 - ADAM_B1 ** ADAM_STEP)
    v_hat = v / (1.0 - ADAM_B2 ** ADAM_STEP)
    return -ADAM_LR * (m_hat / (jnp.sqrt(v_hat) + ADAM_EPS) + ADAM_WD * w), m, v


def _adamw(g, w, m, v, name):
    R, Cc = w.shape[-2:]
    if R <= ADAMW_WHOLE_ROWS or R % HP == 0:
        TR, TC = (R if R <= ADAMW_WHOLE_ROWS else _pick(R, ADAMW_TILES)), Cc
    else:
        TR, TC = R, _pick(Cc, ADAMW_TILES)
    slabs = g.ndim == 3
    lead = w.ndim - 2

    def body(g_ref, w_ref, m_ref, v_ref, go_ref, d_ref, mo_ref, vo_ref):
        if slabs:
            gs = g_ref[0].astype(f32)
            for k in range(1, N_DEV):
                gs = gs + g_ref[k].astype(f32)
        else:
            gs = g_ref[...]
        at = 0 if lead else Ellipsis
        d, mn, vn = _adamw_math(w_ref[at], gs, m_ref[at], v_ref[at])
        go_ref[at], d_ref[at], mo_ref[at], vo_ref[at] = gs, d, mn, vn

    grid = (R // TR, Cc // TC)
    blk = pl.BlockSpec((1,) * lead + (TR, TC), lambda i, j: (0,) * lead + (i, j))
    gblk = pl.BlockSpec((N_DEV, TR, TC), lambda i, j: (0, i, j)) if slabs else pl.BlockSpec((TR, TC), lambda i, j: (i, j))
    return pl.pallas_call(
        body, name=name, grid=grid, in_specs=[gblk, blk, blk, blk], out_specs=[blk] * 4,
        out_shape=[SDS(w.shape, f32)] * 4, compiler_params=_cp(2))(g, w, m, v)


WEIGHTS = ("meta_tokens", "mix_norm_w", "w_in", "fox_forget_bias", "dn_conv_w", "dn_a_log", "dn_dt_bias", "dn_out_norm_w",
           "w_branch_fox", "w_branch_dn", "w_out", "ffn_norm_w", "w_ffn_gate", "w_ffn_up", "w_ffn_down", "final_norm_w")
COL_SHARDED = ("w_in", "w_branch_fox", "w_branch_dn", "w_ffn_gate", "w_ffn_up")
ROW_SHARDED = ("w_out", "w_ffn_down")
BIG = COL_SHARDED + ROW_SHARDED
LATE = tuple(n for n in BIG if n != "w_in")
LATE_MIX = ("w_branch_fox", "w_branch_dn", "w_out")
LATE_FFN = ("w_ffn_gate", "w_ffn_up")
LATE_DOWN = ("w_ffn_down",)
SMALL = tuple(n for n in WEIGHTS if n not in BIG)
TRANSPOSED = ("w_in", "w_ffn_gate", "w_ffn_up")


def _to_slabs(name, g):
    r, c = g.shape
    if name in COL_SHARDED and name not in TRANSPOSED:
        return _b(g.reshape(r, N_DEV, c // N_DEV).transpose(1, 0, 2))
    return _b(g.reshape(N_DEV, r // N_DEV, c))


def _from_slabs(name, s):
    n, r, c = s.shape
    if name in COL_SHARDED and name not in TRANSPOSED:
        return s.transpose(1, 0, 2).reshape(r, n * c)
    return s.reshape(n * r, c)


def kernel(x, meta_tokens, mix_norm_w, w_in, fox_forget_bias, dn_conv_w, dn_a_log, dn_dt_bias, dn_out_norm_w, w_branch_fox, w_branch_dn, w_out, ffn_norm_w, w_ffn_gate, w_ffn_up, w_ffn_down, final_norm_w, loss_target, m_meta_tokens, m_mix_norm_w, m_w_in, m_fox_forget_bias, m_dn_conv_w, m_dn_a_log, m_dn_dt_bias, m_dn_out_norm_w, m_w_branch_fox, m_w_branch_dn, m_w_out, m_ffn_norm_w, m_w_ffn_gate, m_w_ffn_up, m_w_ffn_down, m_final_norm_w, v_meta_tokens, v_mix_norm_w, v_w_in, v_fox_forget_bias, v_dn_conv_w, v_dn_a_log, v_dn_dt_bias, v_dn_out_norm_w, v_w_branch_fox, v_w_branch_dn, v_w_out, v_ffn_norm_w, v_w_ffn_gate, v_w_ffn_up, v_w_ffn_down, v_final_norm_w):
    w = dict(meta_tokens=meta_tokens, mix_norm_w=mix_norm_w, w_in=w_in, fox_forget_bias=fox_forget_bias, dn_conv_w=dn_conv_w, dn_a_log=dn_a_log, dn_dt_bias=dn_dt_bias, dn_out_norm_w=dn_out_norm_w, w_branch_fox=w_branch_fox, w_branch_dn=w_branch_dn, w_out=w_out, ffn_norm_w=ffn_norm_w, w_ffn_gate=w_ffn_gate, w_ffn_up=w_ffn_up, w_ffn_down=w_ffn_down, final_norm_w=final_norm_w)
    mom = dict(meta_tokens=m_meta_tokens, mix_norm_w=m_mix_norm_w, w_in=m_w_in, fox_forget_bias=m_fox_forget_bias, dn_conv_w=m_dn_conv_w, dn_a_log=m_dn_a_log, dn_dt_bias=m_dn_dt_bias, dn_out_norm_w=m_dn_out_norm_w, w_branch_fox=m_w_branch_fox, w_branch_dn=m_w_branch_dn, w_out=m_w_out, ffn_norm_w=m_ffn_norm_w, w_ffn_gate=m_w_ffn_gate, w_ffn_up=m_w_ffn_up, w_ffn_down=m_w_ffn_down, final_norm_w=m_final_norm_w)
    var = dict(meta_tokens=v_meta_tokens, mix_norm_w=v_mix_norm_w, w_in=v_w_in, fox_forget_bias=v_fox_forget_bias, dn_conv_w=v_dn_conv_w, dn_a_log=v_dn_a_log, dn_dt_bias=v_dn_dt_bias, dn_out_norm_w=v_dn_out_norm_w, w_branch_fox=v_w_branch_fox, w_branch_dn=v_w_branch_dn, w_out=v_w_out, ffn_norm_w=v_ffn_norm_w, w_ffn_gate=v_w_ffn_gate, w_ffn_up=v_w_ffn_up, w_ffn_down=v_w_ffn_down, final_norm_w=v_final_norm_w)
    two_d = lambda a: a.reshape(a.shape[-2:]) if a.ndim >= 2 else a.reshape(1, -1)
    me = 4 * lax.axis_index("x") + 2 * lax.axis_index("y") + lax.axis_index("c")
    for d in (w, mom, var):
        for n in TRANSPOSED:
            d[n] = jnp.swapaxes(d[n], -1, -2)

    g_in, g_meta, g_cw = _gather_two_level([_b(two_d(w["w_in"])), two_d(w["meta_tokens"]), two_d(w["dn_conv_w"])], "all_gather_early")
    meta = g_meta.transpose(1, 0, 2).reshape(N_META, -1)
    cw = g_cw.transpose(1, 0, 2).reshape(CONV_K, -1)

    gx, g_small, recv = _local_step(
        x[0], loss_target[0], meta, two_d(w["mix_norm_w"]), g_in, two_d(w["fox_forget_bias"]), cw, two_d(w["dn_a_log"]),
        two_d(w["dn_dt_bias"]), two_d(w["dn_out_norm_w"]), two_d(w["ffn_norm_w"]), two_d(w["final_norm_w"]),
        {n: _b(two_d(w[n])) for n in LATE})

    order = ("loss",) + SMALL
    flat = jnp.concatenate([g_small[n].reshape(-1) for n in order])
    rows = -(-flat.shape[0] // (8 * HP)) * 8
    tot = _all_reduce_small(jnp.pad(flat, (0, rows * HP - flat.shape[0])).reshape(rows, HP)).reshape(-1)
    summed, o = {}, 0
    for n in order:
        k = g_small[n].shape[0]
        summed[n] = tot[o:o + k]
        o += k
    loss = summed["loss"][0]
    d_model = x.shape[-1]
    mcols, ccols = d_model // N_DEV, dn_conv_w.shape[-1]
    summed["meta_tokens"] = lax.dynamic_slice(summed["meta_tokens"].reshape(N_META, d_model), (0, me * mcols), (N_META, mcols)).reshape(-1)
    summed["dn_conv_w"] = lax.dynamic_slice(summed["dn_conv_w"].reshape(CONV_K, ccols * N_DEV), (0, me * ccols), (CONV_K, ccols)).reshape(-1)

    res = {}
    for n in BIG:
        res[n] = _adamw(recv[n], w[n], mom[n], var[n], "adamw_" + n)
        if n in TRANSPOSED:
            res[n] = [jnp.swapaxes(r, -1, -2) for r in res[n]]
    sizes = [summed[n].shape[0] for n in SMALL]
    srows = -(-sum(sizes) // (8 * HP)) * 8
    pack = lambda d: jnp.pad(jnp.concatenate([d[n].reshape(-1) for n in SMALL]), (0, srows * HP - sum(sizes))).reshape(srows, HP)
    sres = _adamw(pack(summed), pack(w), pack(mom), pack(var), "adamw_small")
    o = 0
    for n, k in zip(SMALL, sizes):
        res[n] = [r.reshape(-1)[o:o + k].reshape(w[n].shape) for r in sres]
        o += k
    return (loss, gx[None], *[res[n][0] for n in WEIGHTS], *[res[n][1] for n in WEIGHTS], *[res[n][2] for n in WEIGHTS], *[res[n][3] for n in WEIGHTS])
```

```python
import functools
import math

import jax
import jax.numpy as jnp
from jax import lax
from jax.experimental import pallas as pl
from jax.experimental.pallas import tpu as pltpu

f32, bf16 = jnp.float32, jnp.bfloat16
HI = lax.Precision.HIGHEST
MESH = pl.DeviceIdType.MESH
SDS = jax.ShapeDtypeStruct

N_DEV = 8
N_META = 16
PREFIX = 128
N_PAD = PREFIX - N_META
FOX_H, FOX_D = 8, 64
DN_H, DN_D = 4, 128
DN_C = 64
CONV_K = 4
HP = 128
SMALL_W = 256
EPS = 1e-6
NEG = -1e30
C_Q0, C_K0 = 64, 67
LSE_COL = 64
LOG2E, LN2 = 1.4426950408889634, 0.6931471805599453
C_LSE0, C_DELTA0 = 70, 65

ADAM_LR, ADAM_B1, ADAM_B2, ADAM_EPS, ADAM_WD, ADAM_STEP = 0.001, 0.9, 0.999, 1e-08, 0.01, 10

VMEM_LIMIT_V7X = 56 * 1024 * 1024
ROW_TILES = (384, 128)
ATTN_TILES = (384, 128)
FFN_TILES = (192, 64)
FOX_HEAD_GROUP = 4
FOX_HEAD_GROUP_FWD = 8
ADAMW_TILES = (256, 128)
ADAMW_WHOLE_ROWS = 512
MAX_WGRAD_BLOCK = 1408
DN_INTRA_GROUP = (6, 3, 2, 1)
DN_SCAN_GROUP = (6, 3, 2, 1)


def _pick(n, cands):
    for c in cands:
        if n % c == 0:
            return c
    raise ValueError(f"no tile of {cands} divides {n}")


def _cp(n_axes=1):
    return pltpu.CompilerParams(dimension_semantics=("arbitrary",) * n_axes, vmem_limit_bytes=VMEM_LIMIT_V7X)


def _b(x):
    return x.astype(bf16)


def _dot(a, b):
    return jnp.dot(a, b, preferred_element_type=f32)


def _dot_nt(a, b):
    return lax.dot_general(a, b, (((1,), (1,)), ((), ())), preferred_element_type=f32)


def _dot_tn(a, b):
    return lax.dot_general(a, b, (((0,), (0,)), ((), ())), preferred_element_type=f32)


def _dot_hi(a, b):
    return jnp.dot(a, b, preferred_element_type=f32, precision=HI)


def _iota(shape, dim):
    return lax.broadcasted_iota(jnp.int32, shape, dim)


def _rms(x, w):
    return x * lax.rsqrt(jnp.mean(x * x, axis=-1, keepdims=True) + EPS) * w


def _sigmoid(x):
    return jax.nn.sigmoid(x)


def _load_once(pairs, sems):
    @pl.when(pl.program_id(0) == 0)
    def _():
        cps = [pltpu.make_async_copy(src, dst, sems.at[k]) for k, (src, dst) in enumerate(pairs)]
        for cp in cps:
            cp.start()
        for cp in cps:
            cp.wait()


def _seg_layout(d_model):
    return (("fq", FOX_H * HP, bf16, "fox"), ("fk", FOX_H * HP, bf16, "fox"), ("sf", SMALL_W, f32, "fox"),
            ("fv", FOX_H * HP, bf16, "fv"),
            ("dn", 3 * DN_H * DN_D, f32, "dn"), ("sd", SMALL_W, f32, "dn"),
            ("dz", DN_H * DN_D, f32, "mix"), ("ga", d_model, f32, "mix"), ("gb", d_model, f32, "mix"))


GROUPS = ("fox", "fv", "dn", "mix")


def _shifted_blocks(TM):
    g = math.gcd(TM, PREFIX)
    nb, npre = TM // g, PREFIX // g
    assert nb >= npre
    return g, nb, npre, [lambda i, j=j: (jnp.maximum(i * nb + j - npre, 0), 0) for j in range(nb)]


def _in_proj(x, pre, w1, wpt, shards):
    T, D = x.shape
    L = T + PREFIX
    NP = wpt.shape[0]
    TM = _pick(L, ROW_TILES)
    NT = L // TM
    g, nb, npre, xmaps = _shifted_blocks(TM)
    segs = _seg_layout(D)
    ns, n = len(segs), len(shards)
    offs, o = [], 0
    for _, wd, _, _ in segs:
        offs.append(o)
        o += wd
    assert o == NP

    def body(*refs):
        x_refs, (pre_ref, w1_ref, wp_hbm), rest = refs[:nb], refs[nb:nb + 3], refs[nb + 3:]
        ins, h_ref, xn_ref, outs, gouts = rest[:n], rest[n], rest[n + 1], rest[n + 2:n + 2 + ns], rest[n + 2 + ns:2 * n + 2 + ns]
        wp_v, sems = rest[2 * n + 2 + ns:2 * n + 4 + ns]
        xsems = rest[2 * n + 4 + ns:]
        _load_once([(wp_hbm, wp_v)], sems)

        _exchange_start(ins, gouts, *xsems, gather=True, when=pl.program_id(0) == 0)

        first = pl.program_id(0) == 0
        h = jnp.concatenate([jnp.where(first, pre_ref[j * g:(j + 1) * g, :], r[...]) if j < npre else r[...]
                             for j, r in enumerate(x_refs)], axis=0)
        h_ref[...] = h
        xn = _b(_rms(h, w1_ref[...]))
        xn_ref[...] = xn
        for o_ref, off, (_, wd, _, _) in zip(outs, offs, segs):
            o_ref[...] = _dot_nt(xn, wp_v[off:off + wd, :]).astype(o_ref.dtype)

        _exchange_wait(ins, gouts, *xsems, gather=True, when=pl.program_id(0) == NT - 1)

    row = lambda wd: pl.BlockSpec((TM, wd), lambda i: (i, 0))
    anyspec = pl.BlockSpec(memory_space=pl.ANY)
    res = pl.pallas_call(
        body, name="in_proj", grid=(NT,),
        in_specs=[pl.BlockSpec((g, D), m) for m in xmaps] + [pl.BlockSpec((PREFIX, D), lambda i: (0, 0)), pl.BlockSpec((1, D), lambda i: (0, 0)), anyspec]
        + [anyspec] * n,
        out_specs=[row(D), row(D)] + [row(wd) for _, wd, _, _ in segs] + [anyspec] * n,
        out_shape=[SDS((L, D), f32), SDS((L, D), bf16)] + [SDS((L, wd), dt) for _, wd, dt, _ in segs]
        + [SDS((N_DEV,) + a.shape, a.dtype) for a in shards],
        scratch_shapes=[pltpu.VMEM((NP, D), bf16), pltpu.SemaphoreType.DMA((1,))] + _exchange_sems(n),
        compiler_params=_cp())(*([x] * nb), pre, w1, wpt, *shards)
    return res[:2 + ns], res[2 + ns:]


def _in_proj_bwd(dgroups, wpt, h0, w1, dh1, slabs):
    L, D = h0.shape
    NP = wpt.shape[0]
    TM = _pick(L, ROW_TILES)
    NT = L // TM
    widths = [g.shape[1] for g in dgroups]
    assert sum(widths) == NP
    ng, n = len(dgroups), len(slabs)

    def body(*refs):
        dg_refs, (wp_hbm, h_ref, w1_ref, dh1_ref) = refs[:ng], refs[ng:ng + 4]
        ins, (dh0_ref, acc_ref), outs = refs[ng + 4:ng + 4 + n], refs[ng + 4 + n:ng + 6 + n], refs[ng + 6 + n:ng + 6 + 2 * n]
        wp_v, sems = refs[ng + 6 + 2 * n:ng + 8 + 2 * n]
        xsems = refs[ng + 8 + 2 * n:]
        _load_once([(wp_hbm, wp_v)], sems)

        @pl.when(pl.program_id(0) == 0)
        def _():
            acc_ref[...] = jnp.zeros_like(acc_ref)

        _exchange_start(ins, outs, *xsems, gather=False, when=pl.program_id(0) == 0)

        dxn, off = None, 0
        for g_ref, wd in zip(dg_refs, widths):
            part = _dot(g_ref[...], wp_v[off:off + wd, :])
            dxn = part if dxn is None else dxn + part
            off += wd
        _, vjp = jax.vjp(_rms, h_ref[...], w1_ref[...])
        dh0n, dw1 = vjp(dxn)
        dh0_ref[...] = dh1_ref[...] + dh0n
        acc_ref[0:1, :] += dw1

        _exchange_wait(ins, outs, *xsems, gather=False, when=pl.program_id(0) == NT - 1)

    row = lambda wd: pl.BlockSpec((TM, wd), lambda i: (i, 0))
    anyspec = pl.BlockSpec(memory_space=pl.ANY)
    res = pl.pallas_call(
        body, name="in_proj_bwd", grid=(NT,),
        in_specs=[row(wd) for wd in widths] + [anyspec, row(D), pl.BlockSpec((1, D), lambda i: (0, 0)), row(D)] + [anyspec] * n,
        out_specs=[row(D), pl.BlockSpec((8, D), lambda i: (0, 0))] + [anyspec] * n,
        out_shape=[SDS((L, D), f32), SDS((8, D), f32)] + [SDS(a.shape, a.dtype) for a in slabs],
        scratch_shapes=[pltpu.VMEM((NP, D), bf16), pltpu.SemaphoreType.DMA((1,))] + _exchange_sems(n),
        compiler_params=_cp())(*dgroups, wpt, h0, w1, dh1, *slabs)
    return res[0], res[1], res[2:]


def _matmul_tn(a, b, name):
    L, R = a.shape
    C = b.shape[1]
    br = max(k for k in range(HP, MAX_WGRAD_BLOCK + 1, HP) if R % k == 0)

    def body(a_ref, b_ref, o_ref):
        o_ref[...] = _b(_dot_tn(a_ref[...], b_ref[...]))

    return pl.pallas_call(
        body, name=name, grid=(R // br,),
        in_specs=[pl.BlockSpec((L, br), lambda r: (0, r)), pl.BlockSpec((L, C), lambda r: (0, 0))],
        out_specs=pl.BlockSpec((br, C), lambda r: (r, 0)), out_shape=SDS((R, C), bf16), compiler_params=_cp())(a, b)


def _fox_prep(fq, fk, fv, sf, bias_p):
    L = fq.shape[0]
    T = HP
    NT = L // T
    W = FOX_H * HP

    def body(fq_ref, fk_ref, fv_ref, sf_ref, b_ref, qa_ref, ka_ref, va_ref, carry):
        @pl.when(pl.program_id(0) == 0)
        def _():
            carry[...] = jnp.zeros_like(carry)

        lane, row = _iota((T, HP), 1), _iota((T, HP), 0)
        logf = jnp.where(lane < FOX_H, jax.nn.log_sigmoid(sf_ref[...] + b_ref[...]), 0.0)
        c = _dot_hi((row >= lane).astype(f32), logf) + carry[...]
        carry[...] = jnp.sum(jnp.where(row == T - 1, c, 0.0), axis=0, keepdims=True)
        ones_q = jnp.where((lane >= C_K0) & (lane < C_K0 + 3), 1.0, 0.0)
        ones_k = jnp.where(((lane >= C_Q0) & (lane < C_Q0 + 3)) | ((lane >= C_LSE0) & (lane < C_LSE0 + 3)), 1.0, 0.0)
        ones_v = _b(jnp.where((lane >= LSE_COL) & (lane < C_DELTA0 + 3), 1.0, 0.0))
        for h in range(FOX_H):
            ch = jnp.broadcast_to(jnp.sum(jnp.where(lane == h, c, 0.0), axis=1, keepdims=True), (T, HP)) * LOG2E
            c1 = _b(ch).astype(f32)
            c2 = _b(ch - c1).astype(f32)
            c3 = _b(ch - c1 - c2).astype(f32)
            cq = jnp.where(lane == C_Q0, c1, 0.0) + jnp.where(lane == C_Q0 + 1, c2, 0.0) + jnp.where(lane == C_Q0 + 2, c3, 0.0)
            ck = jnp.where(lane == C_K0, c1, 0.0) + jnp.where(lane == C_K0 + 1, c2, 0.0) + jnp.where(lane == C_K0 + 2, c3, 0.0)
            q = fq_ref[:, h * HP:(h + 1) * HP].astype(f32) * (FOX_D ** -0.5 * LOG2E)
            k = fk_ref[:, h * HP:(h + 1) * HP].astype(f32)
            qa_ref[h] = _b(q + cq + ones_q)
            ka_ref[h] = _b(k + ones_k - ck)
            va_ref[:, h * HP:(h + 1) * HP] = fv_ref[:, h * HP:(h + 1) * HP] + ones_v

    wide = pl.BlockSpec((T, W), lambda i: (i, 0))
    return pl.pallas_call(
        body, name="fox_prep", grid=(NT,),
        in_specs=[wide, wide, wide, pl.BlockSpec((T, HP), lambda i: (i, 0)), pl.BlockSpec((1, HP), lambda i: (0, 0))],
        out_specs=[pl.BlockSpec((FOX_H, T, HP), lambda i: (0, i, 0))] * 2 + [wide],
        out_shape=[SDS((FOX_H, L, HP), bf16)] * 2 + [SDS((L, W), bf16)],
        scratch_shapes=[pltpu.VMEM((1, HP), f32)], compiler_params=_cp())(fq, fk, fv, sf, bias_p)


def _fox_prep_bwd(dqa, dka, sf, bias_p):
    L = sf.shape[0]
    T = HP
    NT = L // T
    rev = lambda i: (NT - 1 - i, 0)

    W = FOX_H * HP

    def body(dq_ref, dk_ref, sf_ref, b_ref, dg_ref, db_ref, carry):
        @pl.when(pl.program_id(0) == 0)
        def _():
            carry[...] = jnp.zeros_like(carry)
            db_ref[...] = jnp.zeros_like(db_ref)

        dq, dk = dq_ref[...], dk_ref[...]
        dg_ref[:, 0:W] = _b(dq * (FOX_D ** -0.5))
        dg_ref[:, W:2 * W] = _b(dk * LN2)
        lane, row = _iota((T, HP), 1), _iota((T, HP), 0)
        dc = jnp.zeros((T, HP), f32)
        for h in range(FOX_H):
            col = jnp.sum(jnp.where(lane == C_Q0, dq[:, h * HP:(h + 1) * HP], 0.0)
                          - jnp.where(lane == C_K0, dk[:, h * HP:(h + 1) * HP], 0.0), axis=1, keepdims=True)
            dc = dc + jnp.where(lane == h, col, 0.0)
        dl = _dot_hi((row <= lane).astype(f32), dc) + carry[...]
        carry[...] = jnp.sum(jnp.where(row == 0, dl, 0.0), axis=0, keepdims=True)
        dx = jnp.where(lane < FOX_H, dl * _sigmoid(-(sf_ref[...] + b_ref[...])), 0.0)
        dg_ref[:, 2 * W:2 * W + HP] = _b(dx)
        dg_ref[:, 2 * W + HP:] = jnp.zeros((T, SMALL_W - HP), bf16)
        db_ref[0:1, :] += jnp.sum(dx, axis=0, keepdims=True)

    return pl.pallas_call(
        body, name="fox_prep_bwd", grid=(NT,),
        in_specs=[pl.BlockSpec((T, W), rev), pl.BlockSpec((T, W), rev), pl.BlockSpec((T, HP), rev), pl.BlockSpec((1, HP), lambda i: (0, 0))],
        out_specs=[pl.BlockSpec((T, 2 * W + SMALL_W), rev), pl.BlockSpec((8, HP), lambda i: (0, 0))],
        out_shape=[SDS((L, 2 * W + SMALL_W), bf16), SDS((8, HP), f32)],
        scratch_shapes=[pltpu.VMEM((1, HP), f32)], compiler_params=_cp())(dqa, dka, sf, bias_p)


def _tile_start(j, T):
    return j * T if isinstance(j, int) else pl.multiple_of(j * T, T)


def _spread3(x, lane, col0):
    x1 = _b(x).astype(f32)
    x2 = _b(x - x1).astype(f32)
    x3 = _b(x - x1 - x2).astype(f32)
    return jnp.where(lane == col0, x1, 0.0) + jnp.where(lane == col0 + 1, x2, 0.0) + jnp.where(lane == col0 + 2, x3, 0.0)


def _fox_fwd(qa, ka, fv, shards):
    L = qa.shape[1]
    TQ = TK = _pick(L, ATTN_TILES)
    NQ = L // TQ
    n = len(shards)
    HG = FOX_HEAD_GROUP_FWD

    def body(q_ref, k_ref, v_ref, *rest):
        ins, o_ref, qb_ref, outs, sems = rest[:n], rest[n], rest[n + 1], rest[n + 2:2 * n + 2], rest[2 * n + 2:]
        h, i = pl.program_id(0), pl.program_id(1)

        _exchange_start(ins, outs, *sems, gather=True, when=(h == 0) & (i == 0))

        qs = [q_ref[a] for a in range(HG)]
        rowg = i * TQ + _iota((TQ, TK), 0)
        colb = _iota((TQ, TK), 1)

        def step(j, carry, masked):
            ms, accs = carry
            k0 = _tile_start(j, TK)
            ss = [_dot_nt(qs[a], k_ref[a, pl.ds(k0, TK), :]) for a in range(HG)]
            if masked:
                colg = colb + j * TK
                keep = (colg <= rowg) & (colg >= N_PAD)
                ss = [jnp.where(keep, s, NEG) for s in ss]
            m_new = [jnp.maximum(m, jnp.max(s, axis=1, keepdims=True)) for m, s in zip(ms, ss)]
            ps = [_b(jnp.exp2(s - m)) for s, m in zip(ss, m_new)]
            alphas = [jnp.exp2(m - mn) for m, mn in zip(ms, m_new)]
            accs = [al * acc + _dot(p, v_ref[pl.ds(k0, TK), a * HP:(a + 1) * HP]) for a, (al, acc, p) in enumerate(zip(alphas, accs, ps))]
            return m_new, accs

        init = ([jnp.full((TQ, 1), NEG, f32)] * HG, [jnp.zeros((TQ, HP), f32)] * HG)
        carry = step(0, init, True)
        carry = lax.fori_loop(1, i, functools.partial(step, masked=False), carry)
        ms, accs = lax.fori_loop(jnp.maximum(i, 1), i + 1, functools.partial(step, masked=True), carry)
        lane = _iota((TQ, HP), 1)
        for a in range(HG):
            l = jnp.sum(jnp.where(lane == LSE_COL, accs[a], 0.0), axis=1, keepdims=True)
            lse = ms[a] + jnp.log2(l)
            o_ref[:, a * HP:(a + 1) * HP] = jnp.where(lane == LSE_COL, lse, accs[a] / l)
            qb_ref[a] = _b(qs[a].astype(f32) - _spread3(jnp.broadcast_to(lse, (TQ, HP)), lane, C_LSE0))

        _exchange_wait(ins, outs, *sems, gather=True, when=(h == FOX_H // HG - 1) & (i == NQ - 1))

    anyspec = pl.BlockSpec(memory_space=pl.ANY)
    qtile = pl.BlockSpec((HG, TQ, HP), lambda h, i: (h, i, 0))
    res = pl.pallas_call(
        body, name="fox_fwd", grid=(FOX_H // HG, NQ),
        in_specs=[qtile, pl.BlockSpec((HG, L, HP), lambda h, i: (h, 0, 0)), pl.BlockSpec((L, HG * HP), lambda h, i: (0, h))] + [anyspec] * n,
        out_specs=[pl.BlockSpec((TQ, HG * HP), lambda h, i: (i, h)), qtile] + [anyspec] * n,
        out_shape=[SDS((L, FOX_H * HP), f32), SDS(qa.shape, bf16)] + [SDS((N_DEV,) + a.shape, a.dtype) for a in shards],
        scratch_shapes=_exchange_sems(n), compiler_params=_cp(2))(qa, ka, fv, *shards)
    return res[0], res[1], res[2:]


def _fox_bwd(qb, ka, va, dob, slabs):
    L = qb.shape[1]
    TQ = TK = _pick(L, ATTN_TILES)
    NQ = L // TQ
    n = len(slabs)
    HG = FOX_HEAD_GROUP

    def body(q_ref, k_ref, v_ref, do_ref, *rest):
        ins, (dq_ref, dk_ref, dv_ref), outs, sems = rest[:n], rest[n:n + 3], rest[n + 3:2 * n + 3], rest[2 * n + 3:]
        h, j = pl.program_id(0), pl.program_id(1)
        cols = [slice(a * HP, (a + 1) * HP) for a in range(HG)]

        _exchange_start(ins, outs, *sems, gather=False, when=(h == 0) & (j == 0))

        @pl.when(j == 0)
        def _():
            dq_ref[...] = jnp.zeros_like(dq_ref)

        kts = [k_ref[a] for a in range(HG)]
        vts = [v_ref[:, cols[a]] for a in range(HG)]
        colg = j * TK + _iota((TQ, TK), 1)
        rowb = _iota((TQ, TK), 0)

        def step(i, carry, masked):
            dks, dvs = carry
            r0 = _tile_start(i, TQ)
            rows = pl.ds(r0, TQ)
            qs = [q_ref[a, rows, :] for a in range(HG)]
            ps = [jnp.exp2(_dot_nt(q, kt)) for q, kt in zip(qs, kts)]
            if masked:
                keep = (colg <= rowb + i * TQ) & (colg >= N_PAD)
                ps = [jnp.where(keep, p, 0.0) for p in ps]
            dobs = [do_ref[rows, cols[a]] for a in range(HG)]
            dvs = [dv + _dot_tn(dob, _b(p)) for dv, p, dob in zip(dvs, ps, dobs)]
            dss = [_b(p * _dot_nt(dob, vt)) for p, dob, vt in zip(ps, dobs, vts)]
            for a in range(HG):
                dq_ref[rows, cols[a]] += _dot(dss[a], kts[a])
            dks = [dk + _dot_tn(q, ds) for dk, ds, q in zip(dks, dss, qs)]
            return dks, dvs

        zeros = [jnp.zeros((HP, TK), f32)] * HG
        carry = step(j, (zeros, zeros), True)
        split = jnp.where(j == 0, NQ, j + 1)
        carry = lax.fori_loop(j + 1, split, functools.partial(step, masked=True), carry)
        dks, dvs = lax.fori_loop(split, NQ, functools.partial(step, masked=False), carry)
        for a in range(HG):
            dk_ref[:, cols[a]] = dks[a].T
            dv_ref[:, cols[a]] = _b(dvs[a].T)

        _exchange_wait(ins, outs, *sems, gather=False, when=(h == FOX_H // HG - 1) & (j == NQ - 1))

    head = pl.BlockSpec((L, HG * HP), lambda h, j: (0, h))
    tile = pl.BlockSpec((TK, HG * HP), lambda h, j: (j, h))
    anyspec = pl.BlockSpec(memory_space=pl.ANY)
    res = pl.pallas_call(
        body, name="fox_bwd", grid=(FOX_H // HG, L // TK),
        in_specs=[pl.BlockSpec((HG, L, HP), lambda h, j: (h, 0, 0)), pl.BlockSpec((HG, TK, HP), lambda h, j: (h, j, 0)), tile, head]
        + [anyspec] * n,
        out_specs=[head, tile, tile] + [anyspec] * n,
        out_shape=[SDS((L, FOX_H * HP), f32), SDS((L, FOX_H * HP), f32), SDS((L, FOX_H * HP), bf16)] + [SDS(a.shape, a.dtype) for a in slabs],
        scratch_shapes=_exchange_sems(n), compiler_params=_cp(2))(qb, ka, va, dob, *slabs)
    return res[:3], res[3:]


def _dn_post(y, sd, alog_p, dt_p, valid):
    a = y * _sigmoid(y)
    W = DN_H * DN_D
    heads = []
    for part, scale in ((0, DN_D ** -0.5), (1, 1.0)):
        for h in range(DN_H):
            xh = a[:, part * W + h * DN_D:part * W + (h + 1) * DN_D]
            heads.append(xh * lax.rsqrt(jnp.sum(xh * xh, axis=-1, keepdims=True) + EPS) * scale)
    q = jnp.concatenate(heads[:DN_H], axis=1)
    k = jnp.concatenate(heads[DN_H:], axis=1)
    v = a[:, 2 * W:3 * W]
    lane = _iota(sd.shape, 1)
    beta = _sigmoid(sd) * valid
    g = -jnp.exp(alog_p) * jax.nn.softplus(sd + dt_p) * valid
    bg = jnp.where(lane < DN_H, beta, jnp.where(lane < 2 * DN_H, g, 0.0))
    return q, k, v, bg


def _conv_fwd(ext_ref, cw_ref, TM):
    y = cw_ref[0:1, :] * ext_ref[8 - (CONV_K - 1):8 - (CONV_K - 1) + TM, :]
    for i in range(1, CONV_K):
        o = 8 - (CONV_K - 1) + i
        y = y + cw_ref[i:i + 1, :] * ext_ref[o:o + TM, :]
    return y


def _dn_prep(dn, sd, cw, alog_p, dt_p):
    L, W3 = dn.shape
    TM = _pick(L, ROW_TILES)
    W = DN_H * DN_D

    def body(dn_ref, halo_ref, sd_ref, cw_ref, al_ref, dt_ref, q_ref, k_ref, v_ref, bg_ref, ext):
        i = pl.program_id(0)
        ext[0:8, :] = jnp.where(i == 0, 0.0, halo_ref[...])
        ext[8:, :] = dn_ref[...]
        y = _conv_fwd(ext, cw_ref, TM)
        valid = ((i * TM + _iota((TM, 1), 0)) >= N_PAD).astype(f32)
        q, k, v, bg = _dn_post(y, sd_ref[...], al_ref[...], dt_ref[...], valid)
        q_ref[...], k_ref[...], v_ref[...], bg_ref[...] = q, k, v, bg

    row = lambda wd: pl.BlockSpec((TM, wd), lambda i: (i, 0))
    vec = pl.BlockSpec((1, HP), lambda i: (0, 0))
    return pl.pallas_call(
        body, name="dn_prep", grid=(L // TM,),
        in_specs=[row(W3), pl.BlockSpec((8, W3), lambda i: (jnp.maximum(i * (TM // 8) - 1, 0), 0)), row(HP),
                  pl.BlockSpec((CONV_K, W3), lambda i: (0, 0)), vec, vec],
        out_specs=[row(W), row(W), row(W), row(HP)],
        out_shape=[SDS((L, W), f32)] * 3 + [SDS((L, HP), f32)],
        scratch_shapes=[pltpu.VMEM((TM + 8, W3), f32)], compiler_params=_cp())(dn, dn, sd, cw, alog_p, dt_p)


def _dn_prep_bwd(dn, sd, cw, alog_p, dt_p, dq, dk, dv, dbg):
    L, W3 = dn.shape
    TM = _pick(L, ROW_TILES)
    NT = L // TM
    W = DN_H * DN_D

    def body(dn_ref, halo_ref, sd_ref, cw_ref, al_ref, dt_ref, dq_ref, dk_ref, dv_ref, dbg_ref,
             dg_ref, dcw_ref, dp_ref, ext, dyp, carry):
        i = pl.program_id(0)
        t = NT - 1 - i

        @pl.when(i == 0)
        def _():
            carry[...] = jnp.zeros_like(carry)
            dcw_ref[...] = jnp.zeros_like(dcw_ref)
            dp_ref[...] = jnp.zeros_like(dp_ref)
            dyp[...] = jnp.zeros_like(dyp)

        ext[0:8, :] = jnp.where(t == 0, 0.0, halo_ref[...])
        ext[8:, :] = dn_ref[...]
        y = _conv_fwd(ext, cw_ref, TM)
        valid = ((t * TM + _iota((TM, 1), 0)) >= N_PAD).astype(f32)
        _, vjp = jax.vjp(functools.partial(_dn_post, valid=valid), y, sd_ref[...], al_ref[...], dt_ref[...])
        dy, dsd, dal, ddt = vjp((dq_ref[...], dk_ref[...], dv_ref[...], dbg_ref[...]))
        dg_ref[:, W3:W3 + HP] = _b(dsd)
        dg_ref[:, W3 + HP:] = jnp.zeros((TM, SMALL_W - HP), bf16)
        dp_ref[0:1, :] += dal
        dp_ref[1:2, :] += ddt
        dyp[8:8 + TM, :] = dy
        o0 = CONV_K - 1
        dext = cw_ref[0:1, :] * dyp[o0:o0 + TM + 8, :]
        for k in range(1, CONV_K):
            dext = dext + cw_ref[k:k + 1, :] * dyp[o0 - k:o0 - k + TM + 8, :]
        for k in range(CONV_K):
            o = 8 - (CONV_K - 1) + k
            dcw_ref[k:k + 1, :] += jnp.sum(dy * ext[o:o + TM, :], axis=0, keepdims=True)
        dg_ref[:, 0:W3] = _b(jnp.concatenate([dext[8:TM, :], dext[TM:TM + 8, :] + carry[...]], axis=0))
        carry[...] = dext[0:8, :]

    row = lambda wd: pl.BlockSpec((TM, wd), lambda i: (NT - 1 - i, 0))
    vec = pl.BlockSpec((1, HP), lambda i: (0, 0))
    return pl.pallas_call(
        body, name="dn_prep_bwd", grid=(NT,),
        in_specs=[row(W3), pl.BlockSpec((8, W3), lambda i: (jnp.maximum((NT - 1 - i) * (TM // 8) - 1, 0), 0)), row(HP),
                  pl.BlockSpec((CONV_K, W3), lambda i: (0, 0)), vec, vec, row(W), row(W), row(W), row(HP)],
        out_specs=[row(W3 + SMALL_W), pl.BlockSpec((8, W3), lambda i: (0, 0)), pl.BlockSpec((8, HP), lambda i: (0, 0))],
        out_shape=[SDS((L, W3 + SMALL_W), bf16), SDS((8, W3), f32), SDS((8, HP), f32)],
        scratch_shapes=[pltpu.VMEM((TM + 8, W3), f32), pltpu.VMEM((TM + 16, W3), f32), pltpu.VMEM((8, W3), f32)],
        compiler_params=_cp())(dn, dn, sd, cw, alog_p, dt_p, dq, dk, dv, dbg)


def _split2(x):
    hi = _b(x)
    return hi, _b(x - hi.astype(f32))


def _split3(x):
    hi = _b(x)
    r = x - hi.astype(f32)
    mid = _b(r)
    return hi, mid, _b(r - mid.astype(f32))


def _x3(a, b, dot):
    (a1, a2), (b1, b2) = _split2(a), _split2(b)
    return dot(a1, b1) + (dot(a1, b2) + dot(a2, b1))


@jax.custom_vjp
def _dot_x3(a, b):
    return _x3(a, b, _dot)


_dot_x3.defvjp(lambda a, b: (_x3(a, b, _dot), (a, b)), lambda res, g: (_x3(g, res[1], _dot_nt), _x3(res[0], g, _dot_tn)))


def _exact3(m, x, dot):
    x1, x2, x3 = _split3(x)
    return dot(m, x1) + (dot(m, x2) + dot(m, x3))


def _tri_ones(C, lower):
    row, col = _iota((C, C), 0), _iota((C, C), 1)
    return _b(((row >= col) if lower else (row <= col)).astype(f32))


@jax.custom_vjp
def _chunk_cumsum(x):
    return _exact3(_tri_ones(x.shape[0], True), x, _dot)


_chunk_cumsum.defvjp(lambda x: (_exact3(_tri_ones(x.shape[0], True), x, _dot), None),
                     lambda _, g: (_exact3(_tri_ones(g.shape[0], False), g, _dot),))


def _mxu_transpose(x):
    C = x.shape[0]
    eye = _b((_iota((C, C), 0) == _iota((C, C), 1)).astype(f32))
    return _exact3(eye, x, lambda m, part: _dot_tn(part, m))


@jax.custom_vjp
def _transpose_exact(x):
    return _mxu_transpose(x)


_transpose_exact.defvjp(lambda x: (_mxu_transpose(x), None), lambda _, g: (_mxu_transpose(g),))


def _unit_lower_inverses(lows):
    C = lows[0].shape[0]
    P = jnp.stack(lows)
    X = (_iota((C, C), 0) == _iota((C, C), 1)).astype(f32)[None] - P
    bdot = functools.partial(_x3, dot=lambda a, b: jnp.einsum("bij,bjk->bik", a, b, preferred_element_type=f32))
    for _ in range(5):
        P = bdot(P, P)
        X = X + bdot(X, P)
    return [X[i] for i in range(len(lows))]


@jax.custom_vjp
def _inverse_given(low, X):
    return X


def _inverse_given_bwd(X, g):
    return -_x3(_x3(X, g, _dot_tn), X, _dot_nt), jnp.zeros_like(X)


_inverse_given.defvjp(lambda low, X: (X, X), _inverse_given_bwd)


def _dn_intra_pre(q, k, v, bg):
    C = DN_C
    row, col = _iota((C, C), 0), _iota((C, C), 1)
    tri = row >= col
    G = _chunk_cumsum(bg)
    GT = _transpose_exact(G)
    lane = _iota((C, HP), 1)
    rowt = _iota((HP, C), 0)
    last = _iota((C, 1), 0) == C - 1
    heads = []
    for h in range(DN_H):
        beta = jnp.sum(jnp.where(lane == h, bg, 0.0), axis=1, keepdims=True)
        gcol = jnp.sum(jnp.where(lane == DN_H + h, G, 0.0), axis=1, keepdims=True)
        grow = jnp.sum(jnp.where(rowt == DN_H + h, GT, 0.0), axis=0, keepdims=True)
        glast = jnp.sum(jnp.where(last, gcol, 0.0), axis=0, keepdims=True)
        decay = jnp.exp(jnp.where(tri, gcol - grow, NEG))
        qh, kh, vh = (t[:, h * DN_D:(h + 1) * DN_D] for t in (q, k, v))
        kb = kh * beta
        low = jnp.where(row > col, _dot_nt(_b(kb), _b(kh)) * decay, 0.0)
        heads.append((beta, gcol, glast, decay, qh, kh, vh, kb, low))
    return heads


def _dn_intra_post(heads, xs):
    lane1 = _iota((1, HP), 1)
    us, ws, qds, kds, attns = [], [], [], [], []
    glrow = jnp.zeros((1, HP), f32)
    for h, ((beta, gcol, glast, decay, qh, kh, vh, kb, _), X) in enumerate(zip(heads, xs)):
        eg = jnp.exp(gcol)
        us.append(_dot_x3(X, vh * beta))
        ws.append(_dot_x3(X, kb * eg))
        attns.append(_dot_nt(_b(qh), _b(kh)) * decay)
        qds.append(qh * eg)
        kds.append(kh * jnp.exp(glast - gcol))
        glrow = glrow + jnp.where(lane1 == h, glast, 0.0)
    cat = lambda xs_: jnp.concatenate(xs_, axis=1)
    return cat(us), cat(ws), cat(qds), cat(kds), cat(attns), glrow, cat(list(xs))


def _dn_intra_group(q, k, v, bg, xs):
    G = q.shape[0] // DN_C
    rows = [slice(j * DN_C, (j + 1) * DN_C) for j in range(G)]
    pre = [_dn_intra_pre(q[r, :], k[r, :], v[r, :], bg[r, :]) for r in rows]
    inv = [[_inverse_given(hd[-1], x) for hd, x in zip(heads, xj)] for heads, xj in zip(pre, xs)]
    post = [_dn_intra_post(heads, xj) for heads, xj in zip(pre, inv)]
    return tuple(jnp.concatenate([p[i] for p in post], axis=0) for i in range(5)) + (tuple(p[5] for p in post),)


def _lane_pick(rowvec, h):
    return jnp.sum(jnp.where(_iota(rowvec.shape, 1) == h, rowvec, 0.0), axis=1, keepdims=True)


def _dn_intra(q, k, v, bg, shards):
    L, W = q.shape
    NC = L // DN_C
    G = _pick(NC, DN_INTRA_GROUP)
    R = G * DN_C
    NS = NC // G
    WA = DN_H * DN_C
    n = len(shards)

    def body(q_ref, k_ref, v_ref, bg_ref, *rest):
        ins, (u_ref, w_ref, qd_ref, kd_ref, at_ref, gl_ref, x_ref), gouts, sems = rest[:n], rest[n:n + 7], rest[n + 7:2 * n + 7], rest[2 * n + 7:]
        _exchange_start(ins, gouts, *sems, gather=True, when=pl.program_id(0) == 0)
        rows = [slice(j * DN_C, (j + 1) * DN_C) for j in range(G)]
        pre = [_dn_intra_pre(q_ref[r, :], k_ref[r, :], v_ref[r, :], bg_ref[r, :]) for r in rows]
        inv = _unit_lower_inverses([hd[-1] for heads in pre for hd in heads])
        for j, r in enumerate(rows):
            u, w, qd, kd, at, gl, xs = _dn_intra_post(pre[j], inv[j * DN_H:(j + 1) * DN_H])
            u_ref[r, :], x_ref[r, :] = u, xs
            w_ref[r, :], qd_ref[r, :], kd_ref[r, :], at_ref[r, :] = _b(w), _b(qd), _b(kd), _b(at)
            gl_ref[j] = gl
        _exchange_wait(ins, gouts, *sems, gather=True, when=pl.program_id(0) == NS - 1)

    row = lambda wd: pl.BlockSpec((R, wd), lambda s: (s, 0))
    anyspec = pl.BlockSpec(memory_space=pl.ANY)
    res = pl.pallas_call(
        body, name="dn_intra", grid=(NS,),
        in_specs=[row(W), row(W), row(W), row(HP)] + [anyspec] * n,
        out_specs=[row(W), row(W), row(W), row(W), row(WA), pl.BlockSpec((G, 1, HP), lambda s: (s, 0, 0)), row(WA)] + [anyspec] * n,
        out_shape=[SDS((L, W), f32), SDS((L, W), bf16), SDS((L, W), bf16), SDS((L, W), bf16), SDS((L, WA), bf16), SDS((NC, 1, HP), f32),
                   SDS((L, WA), f32)] + [SDS((N_DEV,) + a.shape, a.dtype) for a in shards],
        scratch_shapes=_exchange_sems(n), compiler_params=_cp())(q, k, v, bg, *shards)
    return res[:7], res[7:]


def _dn_scan(u, w, qd, kd, at, gl):
    L, W = u.shape
    NC = L // DN_C
    G = _pick(NC, DN_SCAN_GROUP)
    R = G * DN_C

    def body(u_ref, w_ref, qd_ref, kd_ref, at_ref, gl_ref, o_ref, vn_ref, s_ref, S):
        @pl.when(pl.program_id(0) == 0)
        def _():
            S[...] = jnp.zeros_like(S)

        for j in range(G):
            r = slice(j * DN_C, (j + 1) * DN_C)
            glrow = gl_ref[j]
            for h in range(DN_H):
                c = slice(h * DN_D, (h + 1) * DN_D)
                Sh = S[h]
                s_ref[j, h] = Sh
                Sb = _b(Sh)
                vb = _b(u_ref[r, c] - _dot(w_ref[r, c], Sb))
                vn_ref[r, c] = vb
                o_ref[r, c] = _dot(qd_ref[r, c], Sb) + _dot(at_ref[r, h * DN_C:(h + 1) * DN_C], vb)
                S[h] = Sh * jnp.exp(_lane_pick(glrow, h)) + _dot_tn(kd_ref[r, c], vb)

    row = lambda wd: pl.BlockSpec((R, wd), lambda n: (n, 0))
    return pl.pallas_call(
        body, name="dn_scan", grid=(NC // G,),
        in_specs=[row(W), row(W), row(W), row(W), row(DN_H * DN_C), pl.BlockSpec((G, 1, HP), lambda n: (n, 0, 0))],
        out_specs=[row(W), row(W), pl.BlockSpec((G, DN_H, DN_D, DN_D), lambda n: (n, 0, 0, 0))],
        out_shape=[SDS((L, W), f32), SDS((L, W), bf16), SDS((NC, DN_H, DN_D, DN_D), f32)],
        scratch_shapes=[pltpu.VMEM((DN_H, DN_D, DN_D), f32)], compiler_params=_cp())(u, w, qd, kd, at, gl)


def _dn_scan_bwd(do, w, qd, kd, at, gl):
    L, W = do.shape
    NC = L // DN_C
    G = _pick(NC, DN_SCAN_GROUP)
    R = G * DN_C
    NS = NC // G

    def body(do_ref, w_ref, qd_ref, kd_ref, at_ref, gl_ref, dvn_ref, ds_ref, dS):
        @pl.when(pl.program_id(0) == 0)
        def _():
            dS[...] = jnp.zeros_like(dS)

        for j in reversed(range(G)):
            r = slice(j * DN_C, (j + 1) * DN_C)
            glrow = gl_ref[j]
            for h in range(DN_H):
                c = slice(h * DN_D, (h + 1) * DN_D)
                dSo = dS[h]
                ds_ref[j, h] = dSo
                dob = _b(do_ref[r, c])
                dvn = _dot_tn(at_ref[r, h * DN_C:(h + 1) * DN_C], dob) + _dot(kd_ref[r, c], _b(dSo))
                dvn_ref[r, c] = dvn
                dS[h] = _dot_tn(qd_ref[r, c], dob) + dSo * jnp.exp(_lane_pick(glrow, h)) - _dot_tn(w_ref[r, c], _b(dvn))

    row = lambda wd: pl.BlockSpec((R, wd), lambda n: (NS - 1 - n, 0))
    return pl.pallas_call(
        body, name="dn_scan_bwd", grid=(NS,),
        in_specs=[row(W), row(W), row(W), row(W), row(DN_H * DN_C), pl.BlockSpec((G, 1, HP), lambda n: (NS - 1 - n, 0, 0))],
        out_specs=[row(W), pl.BlockSpec((G, DN_H, DN_D, DN_D), lambda n: (NS - 1 - n, 0, 0, 0))],
        out_shape=[SDS((L, W), f32), SDS((NC, DN_H, DN_D, DN_D), f32)],
        scratch_shapes=[pltpu.VMEM((DN_H, DN_D, DN_D), f32)], compiler_params=_cp())(do, w, qd, kd, at, gl)


def _dn_intra_bwd(q, k, v, bg, xinv, do, vn, dvn, states, dstates):
    L, W = q.shape
    NC = L // DN_C
    G = _pick(NC, DN_INTRA_GROUP)
    R = G * DN_C

    def body(q_ref, k_ref, v_ref, bg_ref, x_ref, do_ref, vn_ref, dvn_ref, s_ref, ds_ref, dq_ref, dk_ref, dv_ref, dbg_ref):
        lane1 = _iota((1, HP), 1)
        rows = [slice(j * DN_C, (j + 1) * DN_C) for j in range(G)]
        xs = [[x_ref[r, h * DN_C:(h + 1) * DN_C] for h in range(DN_H)] for r in rows]
        fwd, vjp = jax.vjp(functools.partial(_dn_intra_group, xs=xs), q_ref[...], k_ref[...], v_ref[...], bg_ref[...])
        dws, dqds, dkds, dats, dgls = [], [], [], [], []
        for j, r in enumerate(rows):
            dw, dqd, dkd, dat = [], [], [], []
            dgl = jnp.zeros((1, HP), f32)
            for h in range(DN_H):
                c = slice(h * DN_D, (h + 1) * DN_D)
                Sh, dSo = s_ref[j, h], ds_ref[j, h]
                Sb, dob, vb = _b(Sh), _b(do_ref[r, c]), vn_ref[r, c]
                dw.append(-_dot_nt(_b(dvn_ref[r, c]), Sb))
                dqd.append(_dot_nt(dob, Sb))
                dat.append(_dot_nt(dob, vb))
                dkd.append(_dot_nt(vb, _b(dSo)))
                dcd = jnp.sum(jnp.sum(Sh * dSo, axis=1, keepdims=True), axis=0, keepdims=True)
                dgl = dgl + jnp.where(lane1 == h, dcd * jnp.exp(_lane_pick(fwd[5][j], h)), 0.0)
            cat = lambda xs_: jnp.concatenate(xs_, axis=1)
            dws.append(cat(dw)), dqds.append(cat(dqd)), dkds.append(cat(dkd)), dats.append(cat(dat)), dgls.append(dgl)
        cat0 = lambda xs_: jnp.concatenate(xs_, axis=0)
        dq, dk, dv, dbg = vjp((dvn_ref[...], cat0(dws), cat0(dqds), cat0(dkds), cat0(dats), tuple(dgls)))
        dq_ref[...], dk_ref[...], dv_ref[...], dbg_ref[...] = dq, dk, dv, dbg

    row = lambda wd: pl.BlockSpec((R, wd), lambda s: (s, 0))
    st = pl.BlockSpec((G, DN_H, DN_D, DN_D), lambda s: (s, 0, 0, 0))
    return pl.pallas_call(
        body, name="dn_intra_bwd", grid=(NC // G,),
        in_specs=[row(W), row(W), row(W), row(HP), row(DN_H * DN_C), row(W), row(W), row(W), st, st],
        out_specs=[row(W), row(W), row(W), row(HP)],
        out_shape=[SDS((L, W), f32)] * 3 + [SDS((L, HP), f32)],
        compiler_params=_cp())(q, k, v, bg, xinv, do, vn, dvn, states, dstates)


def _dn_normgate(oraw, dz, wn):
    outs = []
    for h in range(DN_H):
        sl = slice(h * DN_D, (h + 1) * DN_D)
        z = dz[:, sl]
        outs.append(_rms(oraw[:, sl], wn) * (z * _sigmoid(z)))
    return jnp.concatenate(outs, axis=1)


def _mix_fwd(op, oraw, dz, ga, gb, h0, wn, wbf, wbd, wo):
    L, D = h0.shape
    TM = _pick(L, ROW_TILES)

    def body(op_ref, or_ref, dz_ref, ga_ref, gb_ref, h0_ref, wn_ref, wbf_ref, wbd_ref, wo_ref, h1_ref):
        pf = _dot(_b(op_ref[...]), wbf_ref[...])
        pd = _dot(_b(_dn_normgate(or_ref[...], dz_ref[...], wn_ref[...])), wbd_ref[...])
        y = _sigmoid(ga_ref[...]) * pf + _sigmoid(gb_ref[...]) * pd
        h1_ref[...] = h0_ref[...] + _dot(_b(y), wo_ref[...])

    row = lambda wd: pl.BlockSpec((TM, wd), lambda i: (i, 0))
    full = lambda a: pl.BlockSpec(a.shape, lambda i: (0, 0))
    return pl.pallas_call(
        body, name="mix_fwd", grid=(L // TM,),
        in_specs=[row(op.shape[1]), row(oraw.shape[1]), row(dz.shape[1]), row(D), row(D), row(D), full(wn), full(wbf), full(wbd), full(wo)],
        out_specs=row(D), out_shape=SDS((L, D), f32), compiler_params=_cp())(op, oraw, dz, ga, gb, h0, wn, wbf, wbd, wo)


def _mix_bwd(dh1, op, oraw, dz, ga, gb, wn, wbf, wbd, wo):
    L, D = dh1.shape
    TM = _pick(L, ROW_TILES)
    WF, WD = op.shape[1], oraw.shape[1]

    def body(dh1_ref, op_ref, or_ref, dz_ref, ga_ref, gb_ref, wn_ref, wbf_ref, wbd_ref, wo_ref,
             dop_ref, dor_ref, dg_ref, af_ref, ad_ref, dpf_ref, dpd_ref, y_ref, dmix_ref, acc_ref):
        @pl.when(pl.program_id(0) == 0)
        def _():
            acc_ref[...] = jnp.zeros_like(acc_ref)

        opv = op_ref[...]
        af = _b(opv)
        ad, vjp = jax.vjp(_dn_normgate, or_ref[...], dz_ref[...], wn_ref[...])
        adb = _b(ad)
        pf, pd = _dot(af, wbf_ref[...]), _dot(adb, wbd_ref[...])
        sa, sb = _sigmoid(ga_ref[...]), _sigmoid(gb_ref[...])
        dmix = _b(dh1_ref[...])
        dy = _dot_nt(dmix, wo_ref[...])
        dpf, dpd = _b(dy * sa), _b(dy * sb)
        dor, ddz, dwn = vjp(_dot_nt(dpd, wbd_ref[...]))
        dop = _dot_nt(dpf, wbf_ref[...])
        lane = _iota((TM, HP), 1)
        for h in range(WF // HP):
            c = slice(h * HP, (h + 1) * HP)
            delta = jnp.sum(jnp.where(lane < FOX_D, dop[:, c] * opv[:, c], 0.0), axis=1, keepdims=True)
            dop_ref[:, c] = _b(dop[:, c] - _spread3(jnp.broadcast_to(delta, (TM, HP)), lane, C_DELTA0))
        dor_ref[...] = dor
        dg_ref[:, 0:WD] = _b(ddz)
        dg_ref[:, WD:WD + D] = _b(dy * pf * sa * (1.0 - sa))
        dg_ref[:, WD + D:] = _b(dy * pd * sb * (1.0 - sb))
        af_ref[...], ad_ref[...], y_ref[...] = af, adb, _b(sa * pf + sb * pd)
        dpf_ref[...], dpd_ref[...], dmix_ref[...] = dpf, dpd, dmix
        acc_ref[0:1, :] += dwn

    row = lambda wd: pl.BlockSpec((TM, wd), lambda i: (i, 0))
    full = lambda a: pl.BlockSpec(a.shape, lambda i: (0, 0))
    return pl.pallas_call(
        body, name="mix_bwd", grid=(L // TM,),
        in_specs=[row(D), row(WF), row(WD), row(WD), row(D), row(D), full(wn), full(wbf), full(wbd), full(wo)],
        out_specs=[row(WF), row(WD), row(WD + 2 * D), row(WF), row(WD), row(D), row(D), row(D), row(D),
                   pl.BlockSpec((8, HP), lambda i: (0, 0))],
        out_shape=[SDS((L, WF), bf16), SDS((L, WD), f32), SDS((L, WD + 2 * D), bf16), SDS((L, WF), bf16), SDS((L, WD), bf16),
                   SDS((L, D), bf16), SDS((L, D), bf16), SDS((L, D), bf16), SDS((L, D), bf16), SDS((8, HP), f32)],
        compiler_params=_cp())(dh1, op, oraw, dz, ga, gb, wn, wbf, wbd, wo)


def _ffn_fwd_bwd(h1, tgt, w2, wf, wgt, wut, wd):
    L, D = h1.shape
    F = wd.shape[0]
    TM = _pick(L, FFN_TILES)
    gt, nb, _, tmaps = _shifted_blocks(TM)

    def body(h_ref, *refs):
        t_refs, (w2_ref, wf_ref, wg_hbm, wu_hbm, wd_hbm,
                 dh1_ref, xn_ref, dg_ref, du_ref, act_ref, dh2_ref, acc_ref, wg_v, wu_v, wd_v, sems) = refs[:nb], refs[nb:]
        i = pl.program_id(0)
        _load_once([(wg_hbm, wg_v), (wu_hbm, wu_v), (wd_hbm, wd_v)], sems)

        @pl.when(i == 0)
        def _():
            acc_ref[...] = jnp.zeros_like(acc_ref)

        h1v = h_ref[...]
        xn2, vjp2 = jax.vjp(_rms, h1v, w2_ref[...])
        xb = _b(xn2)
        g, u = _dot_nt(xb, wg_v[...]), _dot_nt(xb, wu_v[...])
        sg = _sigmoid(g)
        ab = _b(g * sg * u)
        h2 = h1v + _dot(ab, wd_v[...])
        out, vjpf = jax.vjp(_rms, h2, wf_ref[...])
        valid = (i * TM + _iota((TM, 1), 0)) >= PREFIX
        diff = jnp.where(valid, out - jnp.concatenate([r[...] for r in t_refs], axis=0), 0.0)
        loss = 0.5 * jnp.sum(jnp.sum(diff * diff, axis=1, keepdims=True), axis=0, keepdims=True) / D
        dh2, dwf = vjpf(diff * (1.0 / D))
        dh2b = _b(dh2)
        dact = _dot_nt(dh2b, wd_v[...])
        dgb = _b(dact * u * (sg * (1.0 + g * (1.0 - sg))))
        dub = _b(dact * (g * sg))
        dh1n, dw2 = vjp2(_dot(dgb, wg_v[...]) + _dot(dub, wu_v[...]))
        dh1_ref[...] = dh2 + dh1n
        xn_ref[...], dg_ref[...], du_ref[...], act_ref[...], dh2_ref[...] = xb, dgb, dub, ab, dh2b
        acc_ref[0:1, :] += dw2
        acc_ref[1:2, :] += dwf
        acc_ref[2:3, :] += jnp.broadcast_to(loss, (1, D))

    row = lambda wd_: pl.BlockSpec((TM, wd_), lambda i: (i, 0))
    vec = pl.BlockSpec((1, D), lambda i: (0, 0))
    anyspec = pl.BlockSpec(memory_space=pl.ANY)
    return pl.pallas_call(
        body, name="ffn_fwd_bwd", grid=(L // TM,),
        in_specs=[row(D)] + [pl.BlockSpec((gt, D), m) for m in tmaps] + [vec, vec, anyspec, anyspec, anyspec],
        out_specs=[row(D), row(D), row(F), row(F), row(F), row(D), pl.BlockSpec((8, D), lambda i: (0, 0))],
        out_shape=[SDS((L, D), f32), SDS((L, D), bf16), SDS((L, F), bf16), SDS((L, F), bf16), SDS((L, F), bf16), SDS((L, D), bf16),
                   SDS((8, D), f32)],
        scratch_shapes=[pltpu.VMEM((F, D), bf16), pltpu.VMEM((F, D), bf16), pltpu.VMEM((F, D), bf16), pltpu.SemaphoreType.DMA((3,))],
        compiler_params=_cp())(h1, *([tgt] * nb), w2, wf, wgt, wut, wd)


def _pad_lanes(v, n=HP):
    return jnp.pad(v.astype(f32), ((0, 0), (0, n - v.shape[1])))


def _pack_w_in(wt_full):
    D = wt_full.shape[1]
    FW, DW = FOX_H * FOX_D, DN_H * DN_D
    o = 0
    parts = {}
    for name, wd in (("fq", FW), ("fk", FW), ("fv", FW), ("fl", FOX_H), ("dn", 3 * DW), ("ba", 2 * DN_H), ("dz", DW), ("ga", D), ("gb", D)):
        parts[name] = wt_full[o:o + wd]
        o += wd
    assert o == wt_full.shape[0]
    heads = lambda w: jnp.pad(w.reshape(FOX_H, FOX_D, D), ((0, 0), (0, HP - FOX_D), (0, 0))).reshape(FOX_H * HP, D)
    small = lambda w: jnp.pad(w, ((0, SMALL_W - w.shape[0]), (0, 0)))
    packed = dict(fq=heads(parts["fq"]), fk=heads(parts["fk"]), fv=heads(parts["fv"]), sf=small(parts["fl"]), sd=small(parts["ba"]),
                  dn=parts["dn"], dz=parts["dz"], ga=parts["ga"], gb=parts["gb"])
    return jnp.concatenate([packed[name] for name, _, _, _ in _seg_layout(D)], axis=0)


def _unpack_w_in(groups, d_model):
    D = groups[0].shape[1]
    FW = FOX_H * FOX_D
    segs = {}
    for grp, g in zip(GROUPS, groups):
        o = 0
        for name, wd, _, sg in _seg_layout(d_model):
            if sg == grp:
                segs[name] = g[o:o + wd]
                o += wd
    heads = lambda g: g.reshape(FOX_H, HP, D)[:, :FOX_D].reshape(FW, D)
    return jnp.concatenate([heads(segs["fq"]), heads(segs["fk"]), heads(segs["fv"]), segs["sf"][:FOX_H], segs["dn"],
                            segs["sd"][:2 * DN_H], segs["dz"], segs["ga"], segs["gb"]], axis=0)


def _local_step(x, tgt, meta, w1, w_in_t, fbias, cw, alog, dtb, wn, w2, wf, late_shards):
    T, D = x.shape
    pre = jnp.concatenate([jnp.zeros((N_PAD, D), f32), meta], axis=0)
    wp = _pack_w_in(w_in_t)
    bias_p, alog_p, dt_p = _pad_lanes(fbias), _pad_lanes(jnp.pad(alog, ((0, 0), (DN_H, 0)))), _pad_lanes(jnp.pad(dtb, ((0, 0), (DN_H, 0))))

    (h0, xn, fq, fk, sf, fv, dn, sd, dz, ga, gb), g_mix = _in_proj(x, pre, w1, wp, [late_shards[n] for n in LATE_MIX])
    qa, ka, va = _fox_prep(fq, fk, fv, sf, bias_p)
    op, qb, g_ffn = _fox_fwd(qa, ka, va, [late_shards[n] for n in LATE_FFN])
    qn, kn, vn, bg = _dn_prep(dn, sd, cw, alog_p, dt_p)
    (u_dn, w_dn, qd_dn, kd_dn, at_dn, gl_dn, x_dn), g_down = _dn_intra(qn, kn, vn, bg, [late_shards[n] for n in LATE_DOWN])
    full = {n: _from_slabs(n, s) for n, s in zip(LATE_MIX + LATE_FFN + LATE_DOWN, tuple(g_mix) + tuple(g_ffn) + tuple(g_down))}
    wbf, wbd, wo, wgt, wut, wd = (full[n] for n in ("w_branch_fox", "w_branch_dn", "w_out", "w_ffn_gate", "w_ffn_up", "w_ffn_down"))
    wbf_p = jnp.pad(wbf.reshape(FOX_H, FOX_D, D), ((0, 0), (0, HP - FOX_D), (0, 0))).reshape(FOX_H * HP, D)
    oraw, vnew, states = _dn_scan(u_dn, w_dn, qd_dn, kd_dn, at_dn, gl_dn)
    h1 = _mix_fwd(op, oraw, dz, ga, gb, h0, wn, wbf_p, wbd, wo)

    dh1, xn2, dgate, dup, act, dh2, acc_f = _ffn_fwd_bwd(h1, tgt, w2, wf, wgt, wut, wd)
    g_wg, g_wu, g_wd = _matmul_tn(dgate, xn2, "dw_ffn_gate"), _matmul_tn(dup, xn2, "dw_ffn_up"), _matmul_tn(act, dh2, "dw_ffn_down")

    dop, dor, d_mix, af, ad, dpf, dpd, yb, dmix, acc_m = _mix_bwd(dh1, op, oraw, dz, ga, gb, wn, wbf_p, wbd, wo)
    g_wbf = _matmul_tn(af, dpf, "dw_branch_fox").reshape(FOX_H, HP, D)[:, :FOX_D].reshape(FOX_H * FOX_D, D)
    g_wbd, g_wo = _matmul_tn(ad, dpd, "dw_branch_dn"), _matmul_tn(yb, dmix, "dw_out")

    dvnew, dstates = _dn_scan_bwd(dor, w_dn, qd_dn, kd_dn, at_dn, gl_dn)
    dqn, dkn, dvn, dbg = _dn_intra_bwd(qn, kn, vn, bg, x_dn, dor, vnew, dvnew, states, dstates)
    d_dn, acc_cw, acc_p = _dn_prep_bwd(dn, sd, cw, alog_p, dt_p, dqn, dkn, dvn, dbg)
    g_late = dict(w_branch_fox=g_wbf, w_branch_dn=g_wbd, w_out=g_wo, w_ffn_gate=g_wg, w_ffn_up=g_wu, w_ffn_down=g_wd)
    (dqa, dka, d_fv), recv = _fox_bwd(qb, ka, va, dop, [_to_slabs(n, g_late[n]) for n in LATE])
    d_fox, acc_b = _fox_prep_bwd(dqa, dka, sf, bias_p)

    dgroups = [d_fox, d_fv, d_dn, d_mix]
    g_wp = [_matmul_tn(dg, xn, "dw_in_" + grp) for grp, dg in zip(GROUPS, dgroups)]
    dh0, acc_1, (recv_w_in,) = _in_proj_bwd(dgroups, wp, h0, w1, dh1, [_to_slabs("w_in", _unpack_w_in(g_wp, D))])
    recv = dict(zip(LATE, recv), w_in=recv_w_in)

    small = dict(loss=acc_f[2, 0:1], mix_norm_w=acc_1[0], fox_forget_bias=acc_b[0, :FOX_H], dn_a_log=acc_p[0, DN_H:2 * DN_H],
                 dn_dt_bias=acc_p[1, DN_H:2 * DN_H], dn_out_norm_w=acc_m[0], ffn_norm_w=acc_f[0], final_norm_w=acc_f[1],
                 meta_tokens=dh0[N_PAD:PREFIX].reshape(-1), dn_conv_w=acc_cw[:CONV_K].reshape(-1))
    return dh0[PREFIX:], small, recv


def _mesh_pos():
    x, y, c = lax.axis_index("x"), lax.axis_index("y"), lax.axis_index("c")
    return x, y, c, 4 * x + 2 * y + c


def _peer(x, y, c, m):
    flip = lambda v, on: 1 - v if on else v
    px, py, pc = flip(x, m & 4), flip(y, m & 2), flip(c, m & 1)
    return (px, py, pc), 4 * px + 2 * py + pc


def _exchange_sems(n):
    return [pltpu.SemaphoreType.DMA((n, N_DEV - 1)), pltpu.SemaphoreType.DMA((n, N_DEV - 1)), pltpu.SemaphoreType.DMA((n,))]


def _exchange_part(ins, outs, send_sems, recv_sems, loc_sems, gather, m, receive):
    x, y, c, me = _mesh_pos()
    src = lambda a, pid: ins[a] if gather else ins[a].at[pid]
    if m == 0:
        return [pltpu.make_async_copy(src(a, me), outs[a].at[me], loc_sems.at[a]) for a in range(len(ins))]
    peer, pid = _peer(x, y, c, m)
    return [pltpu.make_async_remote_copy(src_ref=src(a, pid), dst_ref=outs[a].at[pid if receive else me], send_sem=send_sems.at[a, m - 1],
                                         recv_sem=recv_sems.at[a, m - 1], device_id=peer, device_id_type=MESH) for a in range(len(ins))]


def _exchange_start(*refs, gather, when):
    @pl.when(when)
    def _():
        for m in range(N_DEV):
            for cp in _exchange_part(*refs, gather, m, receive=False):
                cp.start()


def _exchange_wait(*refs, gather, when):
    @pl.when(when)
    def _():
        for m in range(1, N_DEV):
            for cp in _exchange_part(*refs, gather, m, receive=True):
                cp.wait_recv()
        for m in list(range(1, N_DEV)) + [0]:
            for cp in _exchange_part(*refs, gather, m, receive=False):
                cp.wait() if m == 0 else cp.wait_send()


def _gather_two_level(arrays, name):
    n = len(arrays)

    def body(*refs):
        ins, outs, (send_sems, recv_sems, loc_sems) = refs[:n], refs[n:2 * n], refs[2 * n:]
        x, y, c, me = _mesh_pos()
        sib = (x, y, 1 - c)
        chips = [(1 - x, y), (x, 1 - y), (1 - x, 1 - y)]
        dev_id = lambda px, py, pc: 4 * px + 2 * py + pc

        def copy(a, k, block, to, own=False):
            return pltpu.make_async_remote_copy(src_ref=ins[a] if own else outs[a].at[block], dst_ref=outs[a].at[block],
                                                send_sem=send_sems.at[a, k], recv_sem=recv_sems.at[a, k], device_id=to, device_id_type=MESH)

        local = [pltpu.make_async_copy(ins[a], outs[a].at[me], loc_sems.at[a]) for a in range(n)]
        first = [copy(a, 0, me, sib, own=True) for a in range(n)]
        first += [copy(a, 1 + j, me, (*chip, c), own=True) for j, chip in enumerate(chips) for a in range(n)]
        for cp in local + first:
            cp.start()
        passed = []
        for j, chip in enumerate(chips):
            for a in range(n):
                copy(a, 1 + j, dev_id(*chip, c), sib).wait_recv()
                cp = copy(a, 4 + j, dev_id(*chip, c), sib)
                cp.start()
                passed.append(cp)
        for a in range(n):
            copy(a, 0, dev_id(x, y, 1 - c), sib).wait_recv()
        for j, chip in enumerate(chips):
            for a in range(n):
                copy(a, 4 + j, dev_id(*chip, 1 - c), sib).wait_recv()
        for cp in first + passed:
            cp.wait_send()
        for cp in local:
            cp.wait()

    anyspec = pl.BlockSpec(memory_space=pl.ANY)
    return pl.pallas_call(
        body, name=name, in_specs=[anyspec] * n, out_specs=[anyspec] * n,
        out_shape=[SDS((N_DEV,) + a.shape, a.dtype) for a in arrays],
        scratch_shapes=_exchange_sems(n))(*arrays)


def _all_reduce_small(v):
    R = v.shape[0]

    def body(v_ref, o_ref, gath, send_sems, recv_sems):
        x, y, c, me = _mesh_pos()
        gath[me] = v_ref[...]
        sends = []
        for m in range(1, N_DEV):
            peer, _ = _peer(x, y, c, m)
            cp = pltpu.make_async_remote_copy(src_ref=v_ref, dst_ref=gath.at[me], send_sem=send_sems.at[m - 1],
                                              recv_sem=recv_sems.at[m - 1], device_id=peer, device_id_type=MESH)
            cp.start()
            sends.append(cp)
        for m in range(1, N_DEV):
            peer, pid = _peer(x, y, c, m)
            pltpu.make_async_remote_copy(src_ref=v_ref, dst_ref=gath.at[pid], send_sem=send_sems.at[m - 1],
                                         recv_sem=recv_sems.at[m - 1], device_id=peer, device_id_type=MESH).wait_recv()
        for cp in sends:
            cp.wait_send()
        tot = gath[0]
        for d in range(1, N_DEV):
            tot = tot + gath[d]
        o_ref[...] = tot

    vm = pl.BlockSpec(memory_space=pltpu.VMEM)
    return pl.pallas_call(
        body, name="all_reduce_small", in_specs=[vm], out_specs=vm, out_shape=SDS((R, HP), f32),
        scratch_shapes=[pltpu.VMEM((N_DEV, R, HP), f32), pltpu.SemaphoreType.DMA((N_DEV - 1,)), pltpu.SemaphoreType.DMA((N_DEV - 1,))],
        )(v)


def _adamw_math(w, g, m, v):
    m = ADAM_B1 * m + (1.0 - ADAM_B1) * g
    v = ADAM_B2 * v + (1.0 - ADAM_B2) * (g * g)
    m_hat = m / (1.0 - ADAM_B1 ** ADAM_STEP)
    v_hat = v / (1.0 - ADAM_B2 ** ADAM_STEP)
    return -ADAM_LR * (m_hat / (jnp.sqrt(v_hat) + ADAM_EPS) + ADAM_WD * w), m, v


def _adamw(g, w, m, v, name):
    R, Cc = w.shape[-2:]
    if R <= ADAMW_WHOLE_ROWS or R % HP == 0:
        TR, TC = (R if R <= ADAMW_WHOLE_ROWS else _pick(R, ADAMW_TILES)), Cc
    else:
        TR, TC = R, _pick(Cc, ADAMW_TILES)
    slabs = g.ndim == 3
    lead = w.ndim - 2

    def body(g_ref, w_ref, m_ref, v_ref, go_ref, d_ref, mo_ref, vo_ref):
        if slabs:
            gs = g_ref[0].astype(f32)
            for k in range(1, N_DEV):
                gs = gs + g_ref[k].astype(f32)
        else:
            gs = g_ref[...]
        at = 0 if lead else Ellipsis
        d, mn, vn = _adamw_math(w_ref[at], gs, m_ref[at], v_ref[at])
        go_ref[at], d_ref[at], mo_ref[at], vo_ref[at] = gs, d, mn, vn

    grid = (R // TR, Cc // TC)
    blk = pl.BlockSpec((1,) * lead + (TR, TC), lambda i, j: (0,) * lead + (i, j))
    gblk = pl.BlockSpec((N_DEV, TR, TC), lambda i, j: (0, i, j)) if slabs else pl.BlockSpec((TR, TC), lambda i, j: (i, j))
    return pl.pallas_call(
        body, name=name, grid=grid, in_specs=[gblk, blk, blk, blk], out_specs=[blk] * 4,
        out_shape=[SDS(w.shape, f32)] * 4, compiler_params=_cp(2))(g, w, m, v)


WEIGHTS = ("meta_tokens", "mix_norm_w", "w_in", "fox_forget_bias", "dn_conv_w", "dn_a_log", "dn_dt_bias", "dn_out_norm_w",
           "w_branch_fox", "w_branch_dn", "w_out", "ffn_norm_w", "w_ffn_gate", "w_ffn_up", "w_ffn_down", "final_norm_w")
COL_SHARDED = ("w_in", "w_branch_fox", "w_branch_dn", "w_ffn_gate", "w_ffn_up")
ROW_SHARDED = ("w_out", "w_ffn_down")
BIG = COL_SHARDED + ROW_SHARDED
LATE = tuple(n for n in BIG if n != "w_in")
LATE_MIX = ("w_branch_fox", "w_branch_dn", "w_out")
LATE_FFN = ("w_ffn_gate", "w_ffn_up")
LATE_DOWN = ("w_ffn_down",)
SMALL = tuple(n for n in WEIGHTS if n not in BIG)
TRANSPOSED = ("w_in", "w_ffn_gate", "w_ffn_up")


def _to_slabs(name, g):
    r, c = g.shape
    if name in COL_SHARDED and name not in TRANSPOSED:
        return _b(g.reshape(r, N_DEV, c // N_DEV).transpose(1, 0, 2))
    return _b(g.reshape(N_DEV, r // N_DEV, c))


def _from_slabs(name, s):
    n, r, c = s.shape
    if name in COL_SHARDED and name not in TRANSPOSED:
        return s.transpose(1, 0, 2).reshape(r, n * c)
    return s.reshape(n * r, c)


def kernel(x, meta_tokens, mix_norm_w, w_in, fox_forget_bias, dn_conv_w, dn_a_log, dn_dt_bias, dn_out_norm_w, w_branch_fox, w_branch_dn, w_out, ffn_norm_w, w_ffn_gate, w_ffn_up, w_ffn_down, final_norm_w, loss_target, m_meta_tokens, m_mix_norm_w, m_w_in, m_fox_forget_bias, m_dn_conv_w, m_dn_a_log, m_dn_dt_bias, m_dn_out_norm_w, m_w_branch_fox, m_w_branch_dn, m_w_out, m_ffn_norm_w, m_w_ffn_gate, m_w_ffn_up, m_w_ffn_down, m_final_norm_w, v_meta_tokens, v_mix_norm_w, v_w_in, v_fox_forget_bias, v_dn_conv_w, v_dn_a_log, v_dn_dt_bias, v_dn_out_norm_w, v_w_branch_fox, v_w_branch_dn, v_w_out, v_ffn_norm_w, v_w_ffn_gate, v_w_ffn_up, v_w_ffn_down, v_final_norm_w):
    w = dict(meta_tokens=meta_tokens, mix_norm_w=mix_norm_w, w_in=w_in, fox_forget_bias=fox_forget_bias, dn_conv_w=dn_conv_w, dn_a_log=dn_a_log, dn_dt_bias=dn_dt_bias, dn_out_norm_w=dn_out_norm_w, w_branch_fox=w_branch_fox, w_branch_dn=w_branch_dn, w_out=w_out, ffn_norm_w=ffn_norm_w, w_ffn_gate=w_ffn_gate, w_ffn_up=w_ffn_up, w_ffn_down=w_ffn_down, final_norm_w=final_norm_w)
    mom = dict(meta_tokens=m_meta_tokens, mix_norm_w=m_mix_norm_w, w_in=m_w_in, fox_forget_bias=m_fox_forget_bias, dn_conv_w=m_dn_conv_w, dn_a_log=m_dn_a_log, dn_dt_bias=m_dn_dt_bias, dn_out_norm_w=m_dn_out_norm_w, w_branch_fox=m_w_branch_fox, w_branch_dn=m_w_branch_dn, w_out=m_w_out, ffn_norm_w=m_ffn_norm_w, w_ffn_gate=m_w_ffn_gate, w_ffn_up=m_w_ffn_up, w_ffn_down=m_w_ffn_down, final_norm_w=m_final_norm_w)
    var = dict(meta_tokens=v_meta_tokens, mix_norm_w=v_mix_norm_w, w_in=v_w_in, fox_forget_bias=v_fox_forget_bias, dn_conv_w=v_dn_conv_w, dn_a_log=v_dn_a_log, dn_dt_bias=v_dn_dt_bias, dn_out_norm_w=v_dn_out_norm_w, w_branch_fox=v_w_branch_fox, w_branch_dn=v_w_branch_dn, w_out=v_w_out, ffn_norm_w=v_ffn_norm_w, w_ffn_gate=v_w_ffn_gate, w_ffn_up=v_w_ffn_up, w_ffn_down=v_w_ffn_down, final_norm_w=v_final_norm_w)
    two_d = lambda a: a.reshape(a.shape[-2:]) if a.ndim >= 2 else a.reshape(1, -1)
    me = 4 * lax.axis_index("x") + 2 * lax.axis_index("y") + lax.axis_index("c")
    for d in (w, mom, var):
        for n in TRANSPOSED:
            d[n] = jnp.swapaxes(d[n], -1, -2)

    g_in, g_meta, g_cw = _gather_two_level([_b(two_d(w["w_in"])), two_d(w["meta_tokens"]), two_d(w["dn_conv_w"])], "all_gather_early")
    meta = g_meta.transpose(1, 0, 2).reshape(N_META, -1)
    cw = g_cw.transpose(1, 0, 2).reshape(CONV_K, -1)

    gx, g_small, recv = _local_step(
        x[0], loss_target[0], meta, two_d(w["mix_norm_w"]), _from_slabs("w_in", g_in), two_d(w["fox_forget_bias"]), cw, two_d(w["dn_a_log"]),
        two_d(w["dn_dt_bias"]), two_d(w["dn_out_norm_w"]), two_d(w["ffn_norm_w"]), two_d(w["final_norm_w"]),
        {n: _b(two_d(w[n])) for n in LATE})

    order = ("loss",) + SMALL
    flat = jnp.concatenate([g_small[n].reshape(-1) for n in order])
    rows = -(-flat.shape[0] // (8 * HP)) * 8
    tot = _all_reduce_small(jnp.pad(flat, (0, rows * HP - flat.shape[0])).reshape(rows, HP)).reshape(-1)
    summed, o = {}, 0
    for n in order:
        k = g_small[n].shape[0]
        summed[n] = tot[o:o + k]
        o += k
    loss = summed["loss"][0]
    d_model = x.shape[-1]
    mcols, ccols = d_model // N_DEV, dn_conv_w.shape[-1]
    summed["meta_tokens"] = lax.dynamic_slice(summed["meta_tokens"].reshape(N_META, d_model), (0, me * mcols), (N_META, mcols)).reshape(-1)
    summed["dn_conv_w"] = lax.dynamic_slice(summed["dn_conv_w"].reshape(CONV_K, ccols * N_DEV), (0, me * ccols), (CONV_K, ccols)).reshape(-1)

    res = {}
    for n in BIG:
        res[n] = _adamw(recv[n], w[n], mom[n], var[n], "adamw_" + n)
        if n in TRANSPOSED:
            res[n] = [jnp.swapaxes(r, -1, -2) for r in res[n]]
    sizes = [summed[n].shape[0] for n in SMALL]
    srows = -(-sum(sizes) // (8 * HP)) * 8
    pack = lambda d: jnp.pad(jnp.concatenate([d[n].reshape(-1) for n in SMALL]), (0, srows * HP - sum(sizes))).reshape(srows, HP)
    sres = _adamw(pack(summed), pack(w), pack(mom), pack(var), "adamw_small")
    o = 0
    for n, k in zip(SMALL, sizes):
        res[n] = [r.reshape(-1)[o:o + k].reshape(w[n].shape) for r in sres]
        o += k
    return (loss, gx[None], *[res[n][0] for n in WEIGHTS], *[res[n][1] for n in WEIGHTS], *[res[n][2] for n in WEIGHTS], *[res[n][3] for n in WEIGHTS])
```

```python
import functools
import math

import jax
import jax.numpy as jnp
from jax import lax
from jax.experimental import pallas as pl
from jax.experimental.pallas import tpu as pltpu

f32, bf16 = jnp.float32, jnp.bfloat16
HI = lax.Precision.HIGHEST
MESH = pl.DeviceIdType.MESH
SDS = jax.ShapeDtypeStruct

N_DEV = 8
N_META = 16
PREFIX = 128
N_PAD = PREFIX - N_META
FOX_H, FOX_D = 8, 64
DN_H, DN_D = 4, 128
DN_C = 64
CONV_K = 4
HP = 128
SMALL_W = 256
EPS = 1e-6
NEG = -1e30
C_Q0, C_K0 = 64, 67
LSE_COL = 64
LOG2E, LN2 = 1.4426950408889634, 0.6931471805599453
C_LSE0, C_DELTA0 = 70, 65

ADAM_LR, ADAM_B1, ADAM_B2, ADAM_EPS, ADAM_WD, ADAM_STEP = 0.001, 0.9, 0.999, 1e-08, 0.01, 10

VMEM_LIMIT_V7X = 56 * 1024 * 1024
ROW_TILES = (384, 128)
ATTN_TILES = (384, 128)
FFN_TILES = (192, 64)
FOX_HEAD_GROUP = 4
FOX_HEAD_GROUP_FWD = 8
ADAMW_TILES = (256, 128)
ADAMW_WHOLE_ROWS = 512
MAX_WGRAD_BLOCK = 1408
DN_INTRA_GROUP = (6, 3, 2, 1)
DN_SCAN_GROUP = (6, 3, 2, 1)


def _pick(n, cands):
    for c in cands:
        if n % c == 0:
            return c
    raise ValueError(f"no tile of {cands} divides {n}")


def _cp(n_axes=1):
    return pltpu.CompilerParams(dimension_semantics=("arbitrary",) * n_axes, vmem_limit_bytes=VMEM_LIMIT_V7X)


def _b(x):
    return x.astype(bf16)


def _dot(a, b):
    return jnp.dot(a, b, preferred_element_type=f32)


def _dot_nt(a, b):
    return lax.dot_general(a, b, (((1,), (1,)), ((), ())), preferred_element_type=f32)


def _dot_tn(a, b):
    return lax.dot_general(a, b, (((0,), (0,)), ((), ())), preferred_element_type=f32)


def _dot_hi(a, b):
    return jnp.dot(a, b, preferred_element_type=f32, precision=HI)


def _iota(shape, dim):
    return lax.broadcasted_iota(jnp.int32, shape, dim)


def _rms(x, w):
    return x * lax.rsqrt(jnp.mean(x * x, axis=-1, keepdims=True) + EPS) * w


def _sigmoid(x):
    return jax.nn.sigmoid(x)


def _load_once(pairs, sems):
    @pl.when(pl.program_id(0) == 0)
    def _():
        cps = [pltpu.make_async_copy(src, dst, sems.at[k]) for k, (src, dst) in enumerate(pairs)]
        for cp in cps:
            cp.start()
        for cp in cps:
            cp.wait()


def _seg_layout(d_model):
    return (("fq", FOX_H * HP, bf16, "fox"), ("fk", FOX_H * HP, bf16, "fox"), ("sf", SMALL_W, f32, "fox"),
            ("fv", FOX_H * HP, bf16, "fv"),
            ("dn", 3 * DN_H * DN_D, f32, "dn"), ("sd", SMALL_W, f32, "dn"),
            ("dz", DN_H * DN_D, f32, "mix"), ("ga", d_model, f32, "mix"), ("gb", d_model, f32, "mix"))


GROUPS = ("fox", "fv", "dn", "mix")


def _shifted_blocks(TM):
    g = math.gcd(TM, PREFIX)
    nb, npre = TM // g, PREFIX // g
    assert nb >= npre
    return g, nb, npre, [lambda i, j=j: (jnp.maximum(i * nb + j - npre, 0), 0) for j in range(nb)]


def _in_proj(x, pre, w1, wpt, shards):
    T, D = x.shape
    L = T + PREFIX
    NP = wpt.shape[0]
    TM = _pick(L, ROW_TILES)
    NT = L // TM
    g, nb, npre, xmaps = _shifted_blocks(TM)
    segs = _seg_layout(D)
    ns, n = len(segs), len(shards)
    offs, o = [], 0
    for _, wd, _, _ in segs:
        offs.append(o)
        o += wd
    assert o == NP

    def body(*refs):
        x_refs, (pre_ref, w1_ref, wp_hbm), rest = refs[:nb], refs[nb:nb + 3], refs[nb + 3:]
        ins, h_ref, xn_ref, outs, gouts = rest[:n], rest[n], rest[n + 1], rest[n + 2:n + 2 + ns], rest[n + 2 + ns:2 * n + 2 + ns]
        wp_v, sems = rest[2 * n + 2 + ns:2 * n + 4 + ns]
        xsems = rest[2 * n + 4 + ns:]
        _load_once([(wp_hbm, wp_v)], sems)

        _exchange_start(ins, gouts, *xsems, gather=True, when=pl.program_id(0) == 0)

        first = pl.program_id(0) == 0
        h = jnp.concatenate([jnp.where(first, pre_ref[j * g:(j + 1) * g, :], r[...]) if j < npre else r[...]
                             for j, r in enumerate(x_refs)], axis=0)
        h_ref[...] = h
        xn = _b(_rms(h, w1_ref[...]))
        xn_ref[...] = xn
        for o_ref, off, (_, wd, _, _) in zip(outs, offs, segs):
            o_ref[...] = _dot_nt(xn, wp_v[off:off + wd, :]).astype(o_ref.dtype)

        _exchange_wait(ins, gouts, *xsems, gather=True, when=pl.program_id(0) == NT - 1)

    row = lambda wd: pl.BlockSpec((TM, wd), lambda i: (i, 0))
    anyspec = pl.BlockSpec(memory_space=pl.ANY)
    res = pl.pallas_call(
        body, name="in_proj", grid=(NT,),
        in_specs=[pl.BlockSpec((g, D), m) for m in xmaps] + [pl.BlockSpec((PREFIX, D), lambda i: (0, 0)), pl.BlockSpec((1, D), lambda i: (0, 0)), anyspec]
        + [anyspec] * n,
        out_specs=[row(D), row(D)] + [row(wd) for _, wd, _, _ in segs] + [anyspec] * n,
        out_shape=[SDS((L, D), f32), SDS((L, D), bf16)] + [SDS((L, wd), dt) for _, wd, dt, _ in segs]
        + [SDS((N_DEV,) + a.shape, a.dtype) for a in shards],
        scratch_shapes=[pltpu.VMEM((NP, D), bf16), pltpu.SemaphoreType.DMA((1,))] + _exchange_sems(n),
        compiler_params=_cp())(*([x] * nb), pre, w1, wpt, *shards)
    return res[:2 + ns], res[2 + ns:]


def _in_proj_bwd(dgroups, wpt, h0, w1, dh1, slabs):
    L, D = h0.shape
    NP = wpt.shape[0]
    TM = _pick(L, ROW_TILES)
    NT = L // TM
    widths = [g.shape[1] for g in dgroups]
    assert sum(widths) == NP
    ng, n = len(dgroups), len(slabs)

    def body(*refs):
        dg_refs, (wp_hbm, h_ref, w1_ref, dh1_ref) = refs[:ng], refs[ng:ng + 4]
        ins, (dh0_ref, acc_ref), outs = refs[ng + 4:ng + 4 + n], refs[ng + 4 + n:ng + 6 + n], refs[ng + 6 + n:ng + 6 + 2 * n]
        wp_v, sems = refs[ng + 6 + 2 * n:ng + 8 + 2 * n]
        xsems = refs[ng + 8 + 2 * n:]
        _load_once([(wp_hbm, wp_v)], sems)

        @pl.when(pl.program_id(0) == 0)
        def _():
            acc_ref[...] = jnp.zeros_like(acc_ref)

        _exchange_start(ins, outs, *xsems, gather=False, when=pl.program_id(0) == 0)

        dxn, off = None, 0
        for g_ref, wd in zip(dg_refs, widths):
            part = _dot(g_ref[...], wp_v[off:off + wd, :])
            dxn = part if dxn is None else dxn + part
            off += wd
        _, vjp = jax.vjp(_rms, h_ref[...], w1_ref[...])
        dh0n, dw1 = vjp(dxn)
        dh0_ref[...] = dh1_ref[...] + dh0n
        acc_ref[0:1, :] += dw1

        _exchange_wait(ins, outs, *xsems, gather=False, when=pl.program_id(0) == NT - 1)

    row = lambda wd: pl.BlockSpec((TM, wd), lambda i: (i, 0))
    anyspec = pl.BlockSpec(memory_space=pl.ANY)
    res = pl.pallas_call(
        body, name="in_proj_bwd", grid=(NT,),
        in_specs=[row(wd) for wd in widths] + [anyspec, row(D), pl.BlockSpec((1, D), lambda i: (0, 0)), row(D)] + [anyspec] * n,
        out_specs=[row(D), pl.BlockSpec((8, D), lambda i: (0, 0))] + [anyspec] * n,
        out_shape=[SDS((L, D), f32), SDS((8, D), f32)] + [SDS(a.shape, a.dtype) for a in slabs],
        scratch_shapes=[pltpu.VMEM((NP, D), bf16), pltpu.SemaphoreType.DMA((1,))] + _exchange_sems(n),
        compiler_params=_cp())(*dgroups, wpt, h0, w1, dh1, *slabs)
    return res[0], res[1], res[2:]


def _matmul_tn(a, b, name):
    L, R = a.shape
    C = b.shape[1]
    br = max(k for k in range(HP, MAX_WGRAD_BLOCK + 1, HP) if R % k == 0)

    def body(a_ref, b_ref, o_ref):
        o_ref[...] = _b(_dot_tn(a_ref[...], b_ref[...]))

    return pl.pallas_call(
        body, name=name, grid=(R // br,),
        in_specs=[pl.BlockSpec((L, br), lambda r: (0, r)), pl.BlockSpec((L, C), lambda r: (0, 0))],
        out_specs=pl.BlockSpec((br, C), lambda r: (r, 0)), out_shape=SDS((R, C), bf16), compiler_params=_cp())(a, b)


def _fox_prep(fq, fk, fv, sf, bias_p):
    L = fq.shape[0]
    T = HP
    NT = L // T
    W = FOX_H * HP

    def body(fq_ref, fk_ref, fv_ref, sf_ref, b_ref, qa_ref, ka_ref, va_ref, carry):
        @pl.when(pl.program_id(0) == 0)
        def _():
            carry[...] = jnp.zeros_like(carry)

        lane, row = _iota((T, HP), 1), _iota((T, HP), 0)
        logf = jnp.where(lane < FOX_H, jax.nn.log_sigmoid(sf_ref[...] + b_ref[...]), 0.0)
        c = _dot_hi((row >= lane).astype(f32), logf) + carry[...]
        carry[...] = jnp.sum(jnp.where(row == T - 1, c, 0.0), axis=0, keepdims=True)
        ones_q = jnp.where((lane >= C_K0) & (lane < C_K0 + 3), 1.0, 0.0)
        ones_k = jnp.where(((lane >= C_Q0) & (lane < C_Q0 + 3)) | ((lane >= C_LSE0) & (lane < C_LSE0 + 3)), 1.0, 0.0)
        ones_v = _b(jnp.where((lane >= LSE_COL) & (lane < C_DELTA0 + 3), 1.0, 0.0))
        for h in range(FOX_H):
            ch = jnp.broadcast_to(jnp.sum(jnp.where(lane == h, c, 0.0), axis=1, keepdims=True), (T, HP)) * LOG2E
            c1 = _b(ch).astype(f32)
            c2 = _b(ch - c1).astype(f32)
            c3 = _b(ch - c1 - c2).astype(f32)
            cq = jnp.where(lane == C_Q0, c1, 0.0) + jnp.where(lane == C_Q0 + 1, c2, 0.0) + jnp.where(lane == C_Q0 + 2, c3, 0.0)
            ck = jnp.where(lane == C_K0, c1, 0.0) + jnp.where(lane == C_K0 + 1, c2, 0.0) + jnp.where(lane == C_K0 + 2, c3, 0.0)
            q = fq_ref[:, h * HP:(h + 1) * HP].astype(f32) * (FOX_D ** -0.5 * LOG2E)
            k = fk_ref[:, h * HP:(h + 1) * HP].astype(f32)
            qa_ref[h] = _b(q + cq + ones_q)
            ka_ref[h] = _b(k + ones_k - ck)
            va_ref[:, h * HP:(h + 1) * HP] = fv_ref[:, h * HP:(h + 1) * HP] + ones_v

    wide = pl.BlockSpec((T, W), lambda i: (i, 0))
    return pl.pallas_call(
        body, name="fox_prep", grid=(NT,),
        in_specs=[wide, wide, wide, pl.BlockSpec((T, HP), lambda i: (i, 0)), pl.BlockSpec((1, HP), lambda i: (0, 0))],
        out_specs=[pl.BlockSpec((FOX_H, T, HP), lambda i: (0, i, 0))] * 2 + [wide],
        out_shape=[SDS((FOX_H, L, HP), bf16)] * 2 + [SDS((L, W), bf16)],
        scratch_shapes=[pltpu.VMEM((1, HP), f32)], compiler_params=_cp())(fq, fk, fv, sf, bias_p)


def _fox_prep_bwd(dqa, dka, sf, bias_p):
    L = sf.shape[0]
    T = HP
    NT = L // T
    rev = lambda i: (NT - 1 - i, 0)

    W = FOX_H * HP

    def body(dq_ref, dk_ref, sf_ref, b_ref, dg_ref, db_ref, carry):
        @pl.when(pl.program_id(0) == 0)
        def _():
            carry[...] = jnp.zeros_like(carry)
            db_ref[...] = jnp.zeros_like(db_ref)

        dq, dk = dq_ref[...], dk_ref[...]
        dg_ref[:, 0:W] = _b(dq * (FOX_D ** -0.5))
        dg_ref[:, W:2 * W] = _b(dk * LN2)
        lane, row = _iota((T, HP), 1), _iota((T, HP), 0)
        dc = jnp.zeros((T, HP), f32)
        for h in range(FOX_H):
            col = jnp.sum(jnp.where(lane == C_Q0, dq[:, h * HP:(h + 1) * HP], 0.0)
                          - jnp.where(lane == C_K0, dk[:, h * HP:(h + 1) * HP], 0.0), axis=1, keepdims=True)
            dc = dc + jnp.where(lane == h, col, 0.0)
        dl = _dot_hi((row <= lane).astype(f32), dc) + carry[...]
        carry[...] = jnp.sum(jnp.where(row == 0, dl, 0.0), axis=0, keepdims=True)
        dx = jnp.where(lane < FOX_H, dl * _sigmoid(-(sf_ref[...] + b_ref[...])), 0.0)
        dg_ref[:, 2 * W:2 * W + HP] = _b(dx)
        dg_ref[:, 2 * W + HP:] = jnp.zeros((T, SMALL_W - HP), bf16)
        db_ref[0:1, :] += jnp.sum(dx, axis=0, keepdims=True)

    return pl.pallas_call(
        body, name="fox_prep_bwd", grid=(NT,),
        in_specs=[pl.BlockSpec((T, W), rev), pl.BlockSpec((T, W), rev), pl.BlockSpec((T, HP), rev), pl.BlockSpec((1, HP), lambda i: (0, 0))],
        out_specs=[pl.BlockSpec((T, 2 * W + SMALL_W), rev), pl.BlockSpec((8, HP), lambda i: (0, 0))],
        out_shape=[SDS((L, 2 * W + SMALL_W), bf16), SDS((8, HP), f32)],
        scratch_shapes=[pltpu.VMEM((1, HP), f32)], compiler_params=_cp())(dqa, dka, sf, bias_p)


def _tile_start(j, T):
    return j * T if isinstance(j, int) else pl.multiple_of(j * T, T)


def _spread3(x, lane, col0):
    x1 = _b(x).astype(f32)
    x2 = _b(x - x1).astype(f32)
    x3 = _b(x - x1 - x2).astype(f32)
    return jnp.where(lane == col0, x1, 0.0) + jnp.where(lane == col0 + 1, x2, 0.0) + jnp.where(lane == col0 + 2, x3, 0.0)


def _fox_fwd(qa, ka, fv, shards):
    L = qa.shape[1]
    TQ = TK = _pick(L, ATTN_TILES)
    NQ = L // TQ
    n = len(shards)
    HG = FOX_HEAD_GROUP_FWD

    def body(q_ref, k_ref, v_ref, *rest):
        ins, o_ref, qb_ref, outs, sems = rest[:n], rest[n], rest[n + 1], rest[n + 2:2 * n + 2], rest[2 * n + 2:]
        h, i = pl.program_id(0), pl.program_id(1)

        _exchange_start(ins, outs, *sems, gather=True, when=(h == 0) & (i == 0))

        qs = [q_ref[a] for a in range(HG)]
        rowg = i * TQ + _iota((TQ, TK), 0)
        colb = _iota((TQ, TK), 1)

        def step(j, carry, masked):
            ms, accs = carry
            k0 = _tile_start(j, TK)
            ss = [_dot_nt(qs[a], k_ref[a, pl.ds(k0, TK), :]) for a in range(HG)]
            if masked:
                colg = colb + j * TK
                keep = (colg <= rowg) & (colg >= N_PAD)
                ss = [jnp.where(keep, s, NEG) for s in ss]
            m_new = [jnp.maximum(m, jnp.max(s, axis=1, keepdims=True)) for m, s in zip(ms, ss)]
            ps = [_b(jnp.exp2(s - m)) for s, m in zip(ss, m_new)]
            alphas = [jnp.exp2(m - mn) for m, mn in zip(ms, m_new)]
            accs = [al * acc + _dot(p, v_ref[pl.ds(k0, TK), a * HP:(a + 1) * HP]) for a, (al, acc, p) in enumerate(zip(alphas, accs, ps))]
            return m_new, accs

        init = ([jnp.full((TQ, 1), NEG, f32)] * HG, [jnp.zeros((TQ, HP), f32)] * HG)
        carry = step(0, init, True)
        carry = lax.fori_loop(1, i, functools.partial(step, masked=False), carry)
        ms, accs = lax.fori_loop(jnp.maximum(i, 1), i + 1, functools.partial(step, masked=True), carry)
        lane = _iota((TQ, HP), 1)
        for a in range(HG):
            l = jnp.sum(jnp.where(lane == LSE_COL, accs[a], 0.0), axis=1, keepdims=True)
            lse = ms[a] + jnp.log2(l)
            o_ref[:, a * HP:(a + 1) * HP] = jnp.where(lane == LSE_COL, lse, accs[a] / l)
            qb_ref[a] = _b(qs[a].astype(f32) - _spread3(jnp.broadcast_to(lse, (TQ, HP)), lane, C_LSE0))

        _exchange_wait(ins, outs, *sems, gather=True, when=(h == FOX_H // HG - 1) & (i == NQ - 1))

    anyspec = pl.BlockSpec(memory_space=pl.ANY)
    qtile = pl.BlockSpec((HG, TQ, HP), lambda h, i: (h, i, 0))
    res = pl.pallas_call(
        body, name="fox_fwd", grid=(FOX_H // HG, NQ),
        in_specs=[qtile, pl.BlockSpec((HG, L, HP), lambda h, i: (h, 0, 0)), pl.BlockSpec((L, HG * HP), lambda h, i: (0, h))] + [anyspec] * n,
        out_specs=[pl.BlockSpec((TQ, HG * HP), lambda h, i: (i, h)), qtile] + [anyspec] * n,
        out_shape=[SDS((L, FOX_H * HP), f32), SDS(qa.shape, bf16)] + [SDS((N_DEV,) + a.shape, a.dtype) for a in shards],
        scratch_shapes=_exchange_sems(n), compiler_params=_cp(2))(qa, ka, fv, *shards)
    return res[0], res[1], res[2:]


def _fox_bwd(qb, ka, va, dob, slabs):
    L = qb.shape[1]
    TQ = TK = _pick(L, ATTN_TILES)
    NQ = L // TQ
    n = len(slabs)
    HG = FOX_HEAD_GROUP

    def body(q_ref, k_ref, v_ref, do_ref, *rest):
        ins, (dq_ref, dk_ref, dv_ref), outs, sems = rest[:n], rest[n:n + 3], rest[n + 3:2 * n + 3], rest[2 * n + 3:]
        h, j = pl.program_id(0), pl.program_id(1)
        cols = [slice(a * HP, (a + 1) * HP) for a in range(HG)]

        _exchange_start(ins, outs, *sems, gather=False, when=(h == 0) & (j == 0))

        @pl.when(j == 0)
        def _():
            dq_ref[...] = jnp.zeros_like(dq_ref)

        kts = [k_ref[a] for a in range(HG)]
        vts = [v_ref[:, cols[a]] for a in range(HG)]
        colg = j * TK + _iota((TQ, TK), 1)
        rowb = _iota((TQ, TK), 0)

        def step(i, carry, masked):
            dks, dvs = carry
            r0 = _tile_start(i, TQ)
            rows = pl.ds(r0, TQ)
            qs = [q_ref[a, rows, :] for a in range(HG)]
            ps = [jnp.exp2(_dot_nt(q, kt)) for q, kt in zip(qs, kts)]
            if masked:
                keep = (colg <= rowb + i * TQ) & (colg >= N_PAD)
                ps = [jnp.where(keep, p, 0.0) for p in ps]
            dobs = [do_ref[rows, cols[a]] for a in range(HG)]
            dvs = [dv + _dot_tn(dob, _b(p)) for dv, p, dob in zip(dvs, ps, dobs)]
            dss = [_b(p * _dot_nt(dob, vt)) for p, dob, vt in zip(ps, dobs, vts)]
            for a in range(HG):
                dq_ref[rows, cols[a]] += _dot(dss[a], kts[a])
            dks = [dk + _dot_tn(q, ds) for dk, ds, q in zip(dks, dss, qs)]
            return dks, dvs

        zeros = [jnp.zeros((HP, TK), f32)] * HG
        carry = step(j, (zeros, zeros), True)
        split = jnp.where(j == 0, NQ, j + 1)
        carry = lax.fori_loop(j + 1, split, functools.partial(step, masked=True), carry)
        dks, dvs = lax.fori_loop(split, NQ, functools.partial(step, masked=False), carry)
        for a in range(HG):
            dk_ref[:, cols[a]] = dks[a].T
            dv_ref[:, cols[a]] = _b(dvs[a].T)

        _exchange_wait(ins, outs, *sems, gather=False, when=(h == FOX_H // HG - 1) & (j == NQ - 1))

    head = pl.BlockSpec((L, HG * HP), lambda h, j: (0, h))
    tile = pl.BlockSpec((TK, HG * HP), lambda h, j: (j, h))
    anyspec = pl.BlockSpec(memory_space=pl.ANY)
    res = pl.pallas_call(
        body, name="fox_bwd", grid=(FOX_H // HG, L // TK),
        in_specs=[pl.BlockSpec((HG, L, HP), lambda h, j: (h, 0, 0)), pl.BlockSpec((HG, TK, HP), lambda h, j: (h, j, 0)), tile, head]
        + [anyspec] * n,
        out_specs=[head, tile, tile] + [anyspec] * n,
        out_shape=[SDS((L, FOX_H * HP), f32), SDS((L, FOX_H * HP), f32), SDS((L, FOX_H * HP), bf16)] + [SDS(a.shape, a.dtype) for a in slabs],
        scratch_shapes=_exchange_sems(n), compiler_params=_cp(2))(qb, ka, va, dob, *slabs)
    return res[:3], res[3:]


def _dn_post(y, sd, alog_p, dt_p, valid):
    a = y * _sigmoid(y)
    W = DN_H * DN_D
    heads = []
    for part, scale in ((0, DN_D ** -0.5), (1, 1.0)):
        for h in range(DN_H):
            xh = a[:, part * W + h * DN_D:part * W + (h + 1) * DN_D]
            heads.append(xh * lax.rsqrt(jnp.sum(xh * xh, axis=-1, keepdims=True) + EPS) * scale)
    q = jnp.concatenate(heads[:DN_H], axis=1)
    k = jnp.concatenate(heads[DN_H:], axis=1)
    v = a[:, 2 * W:3 * W]
    lane = _iota(sd.shape, 1)
    beta = _sigmoid(sd) * valid
    g = -jnp.exp(alog_p) * jax.nn.softplus(sd + dt_p) * valid
    bg = jnp.where(lane < DN_H, beta, jnp.where(lane < 2 * DN_H, g, 0.0))
    return q, k, v, bg


def _conv_fwd(ext_ref, cw_ref, TM):
    y = cw_ref[0:1, :] * ext_ref[8 - (CONV_K - 1):8 - (CONV_K - 1) + TM, :]
    for i in range(1, CONV_K):
        o = 8 - (CONV_K - 1) + i
        y = y + cw_ref[i:i + 1, :] * ext_ref[o:o + TM, :]
    return y


def _dn_prep(dn, sd, cw, alog_p, dt_p):
    L, W3 = dn.shape
    TM = _pick(L, ROW_TILES)
    W = DN_H * DN_D

    def body(dn_ref, halo_ref, sd_ref, cw_ref, al_ref, dt_ref, q_ref, k_ref, v_ref, bg_ref, ext):
        i = pl.program_id(0)
        ext[0:8, :] = jnp.where(i == 0, 0.0, halo_ref[...])
        ext[8:, :] = dn_ref[...]
        y = _conv_fwd(ext, cw_ref, TM)
        valid = ((i * TM + _iota((TM, 1), 0)) >= N_PAD).astype(f32)
        q, k, v, bg = _dn_post(y, sd_ref[...], al_ref[...], dt_ref[...], valid)
        q_ref[...], k_ref[...], v_ref[...], bg_ref[...] = q, k, v, bg

    row = lambda wd: pl.BlockSpec((TM, wd), lambda i: (i, 0))
    vec = pl.BlockSpec((1, HP), lambda i: (0, 0))
    return pl.pallas_call(
        body, name="dn_prep", grid=(L // TM,),
        in_specs=[row(W3), pl.BlockSpec((8, W3), lambda i: (jnp.maximum(i * (TM // 8) - 1, 0), 0)), row(HP),
                  pl.BlockSpec((CONV_K, W3), lambda i: (0, 0)), vec, vec],
        out_specs=[row(W), row(W), row(W), row(HP)],
        out_shape=[SDS((L, W), f32)] * 3 + [SDS((L, HP), f32)],
        scratch_shapes=[pltpu.VMEM((TM + 8, W3), f32)], compiler_params=_cp())(dn, dn, sd, cw, alog_p, dt_p)


def _dn_prep_bwd(dn, sd, cw, alog_p, dt_p, dq, dk, dv, dbg):
    L, W3 = dn.shape
    TM = _pick(L, ROW_TILES)
    NT = L // TM
    W = DN_H * DN_D

    def body(dn_ref, halo_ref, sd_ref, cw_ref, al_ref, dt_ref, dq_ref, dk_ref, dv_ref, dbg_ref,
             dg_ref, dcw_ref, dp_ref, ext, dyp, carry):
        i = pl.program_id(0)
        t = NT - 1 - i

        @pl.when(i == 0)
        def _():
            carry[...] = jnp.zeros_like(carry)
            dcw_ref[...] = jnp.zeros_like(dcw_ref)
            dp_ref[...] = jnp.zeros_like(dp_ref)
            dyp[...] = jnp.zeros_like(dyp)

        ext[0:8, :] = jnp.where(t == 0, 0.0, halo_ref[...])
        ext[8:, :] = dn_ref[...]
        y = _conv_fwd(ext, cw_ref, TM)
        valid = ((t * TM + _iota((TM, 1), 0)) >= N_PAD).astype(f32)
        _, vjp = jax.vjp(functools.partial(_dn_post, valid=valid), y, sd_ref[...], al_ref[...], dt_ref[...])
        dy, dsd, dal, ddt = vjp((dq_ref[...], dk_ref[...], dv_ref[...], dbg_ref[...]))
        dg_ref[:, W3:W3 + HP] = _b(dsd)
        dg_ref[:, W3 + HP:] = jnp.zeros((TM, SMALL_W - HP), bf16)
        dp_ref[0:1, :] += dal
        dp_ref[1:2, :] += ddt
        dyp[8:8 + TM, :] = dy
        o0 = CONV_K - 1
        dext = cw_ref[0:1, :] * dyp[o0:o0 + TM + 8, :]
        for k in range(1, CONV_K):
            dext = dext + cw_ref[k:k + 1, :] * dyp[o0 - k:o0 - k + TM + 8, :]
        for k in range(CONV_K):
            o = 8 - (CONV_K - 1) + k
            dcw_ref[k:k + 1, :] += jnp.sum(dy * ext[o:o + TM, :], axis=0, keepdims=True)
        dg_ref[:, 0:W3] = _b(jnp.concatenate([dext[8:TM, :], dext[TM:TM + 8, :] + carry[...]], axis=0))
        carry[...] = dext[0:8, :]

    row = lambda wd: pl.BlockSpec((TM, wd), lambda i: (NT - 1 - i, 0))
    vec = pl.BlockSpec((1, HP), lambda i: (0, 0))
    return pl.pallas_call(
        body, name="dn_prep_bwd", grid=(NT,),
        in_specs=[row(W3), pl.BlockSpec((8, W3), lambda i: (jnp.maximum((NT - 1 - i) * (TM // 8) - 1, 0), 0)), row(HP),
                  pl.BlockSpec((CONV_K, W3), lambda i: (0, 0)), vec, vec, row(W), row(W), row(W), row(HP)],
        out_specs=[row(W3 + SMALL_W), pl.BlockSpec((8, W3), lambda i: (0, 0)), pl.BlockSpec((8, HP), lambda i: (0, 0))],
        out_shape=[SDS((L, W3 + SMALL_W), bf16), SDS((8, W3), f32), SDS((8, HP), f32)],
        scratch_shapes=[pltpu.VMEM((TM + 8, W3), f32), pltpu.VMEM((TM + 16, W3), f32), pltpu.VMEM((8, W3), f32)],
        compiler_params=_cp())(dn, dn, sd, cw, alog_p, dt_p, dq, dk, dv, dbg)


def _split2(x):
    hi = _b(x)
    return hi, _b(x - hi.astype(f32))


def _split3(x):
    hi = _b(x)
    r = x - hi.astype(f32)
    mid = _b(r)
    return hi, mid, _b(r - mid.astype(f32))


def _x3(a, b, dot):
    (a1, a2), (b1, b2) = _split2(a), _split2(b)
    return dot(a1, b1) + (dot(a1, b2) + dot(a2, b1))


@jax.custom_vjp
def _dot_x3(a, b):
    return _x3(a, b, _dot)


_dot_x3.defvjp(lambda a, b: (_x3(a, b, _dot), (a, b)), lambda res, g: (_x3(g, res[1], _dot_nt), _x3(res[0], g, _dot_tn)))


def _exact3(m, x, dot):
    x1, x2, x3 = _split3(x)
    return dot(m, x1) + (dot(m, x2) + dot(m, x3))


def _tri_ones(C, lower):
    row, col = _iota((C, C), 0), _iota((C, C), 1)
    return _b(((row >= col) if lower else (row <= col)).astype(f32))


@jax.custom_vjp
def _chunk_cumsum(x):
    return _exact3(_tri_ones(x.shape[0], True), x, _dot)


_chunk_cumsum.defvjp(lambda x: (_exact3(_tri_ones(x.shape[0], True), x, _dot), None),
                     lambda _, g: (_exact3(_tri_ones(g.shape[0], False), g, _dot),))


def _mxu_transpose(x):
    C = x.shape[0]
    eye = _b((_iota((C, C), 0) == _iota((C, C), 1)).astype(f32))
    return _exact3(eye, x, lambda m, part: _dot_tn(part, m))


@jax.custom_vjp
def _transpose_exact(x):
    return _mxu_transpose(x)


_transpose_exact.defvjp(lambda x: (_mxu_transpose(x), None), lambda _, g: (_mxu_transpose(g),))


def _unit_lower_inverses(lows):
    C = lows[0].shape[0]
    P = jnp.stack(lows)
    X = (_iota((C, C), 0) == _iota((C, C), 1)).astype(f32)[None] - P
    bdot = functools.partial(_x3, dot=lambda a, b: jnp.einsum("bij,bjk->bik", a, b, preferred_element_type=f32))
    for _ in range(5):
        P = bdot(P, P)
        X = X + bdot(X, P)
    return [X[i] for i in range(len(lows))]


@jax.custom_vjp
def _inverse_given(low, X):
    return X


def _inverse_given_bwd(X, g):
    return -_x3(_x3(X, g, _dot_tn), X, _dot_nt), jnp.zeros_like(X)


_inverse_given.defvjp(lambda low, X: (X, X), _inverse_given_bwd)


def _dn_intra_pre(q, k, v, bg):
    C = DN_C
    row, col = _iota((C, C), 0), _iota((C, C), 1)
    tri = row >= col
    G = _chunk_cumsum(bg)
    GT = _transpose_exact(G)
    lane = _iota((C, HP), 1)
    rowt = _iota((HP, C), 0)
    last = _iota((C, 1), 0) == C - 1
    heads = []
    for h in range(DN_H):
        beta = jnp.sum(jnp.where(lane == h, bg, 0.0), axis=1, keepdims=True)
        gcol = jnp.sum(jnp.where(lane == DN_H + h, G, 0.0), axis=1, keepdims=True)
        grow = jnp.sum(jnp.where(rowt == DN_H + h, GT, 0.0), axis=0, keepdims=True)
        glast = jnp.sum(jnp.where(last, gcol, 0.0), axis=0, keepdims=True)
        decay = jnp.exp(jnp.where(tri, gcol - grow, NEG))
        qh, kh, vh = (t[:, h * DN_D:(h + 1) * DN_D] for t in (q, k, v))
        kb = kh * beta
        low = jnp.where(row > col, _dot_nt(_b(kb), _b(kh)) * decay, 0.0)
        heads.append((beta, gcol, glast, decay, qh, kh, vh, kb, low))
    return heads


def _dn_intra_post(heads, xs):
    lane1 = _iota((1, HP), 1)
    us, ws, qds, kds, attns = [], [], [], [], []
    glrow = jnp.zeros((1, HP), f32)
    for h, ((beta, gcol, glast, decay, qh, kh, vh, kb, _), X) in enumerate(zip(heads, xs)):
        eg = jnp.exp(gcol)
        us.append(_dot_x3(X, vh * beta))
        ws.append(_dot_x3(X, kb * eg))
        attns.append(_dot_nt(_b(qh), _b(kh)) * decay)
        qds.append(qh * eg)
        kds.append(kh * jnp.exp(glast - gcol))
        glrow = glrow + jnp.where(lane1 == h, glast, 0.0)
    cat = lambda xs_: jnp.concatenate(xs_, axis=1)
    return cat(us), cat(ws), cat(qds), cat(kds), cat(attns), glrow, cat(list(xs))


def _dn_intra_group(q, k, v, bg, xs):
    G = q.shape[0] // DN_C
    rows = [slice(j * DN_C, (j + 1) * DN_C) for j in range(G)]
    pre = [_dn_intra_pre(q[r, :], k[r, :], v[r, :], bg[r, :]) for r in rows]
    inv = [[_inverse_given(hd[-1], x) for hd, x in zip(heads, xj)] for heads, xj in zip(pre, xs)]
    post = [_dn_intra_post(heads, xj) for heads, xj in zip(pre, inv)]
    return tuple(jnp.concatenate([p[i] for p in post], axis=0) for i in range(5)) + (tuple(p[5] for p in post),)


def _lane_pick(rowvec, h):
    return jnp.sum(jnp.where(_iota(rowvec.shape, 1) == h, rowvec, 0.0), axis=1, keepdims=True)


def _dn_intra(q, k, v, bg, shards):
    L, W = q.shape
    NC = L // DN_C
    G = _pick(NC, DN_INTRA_GROUP)
    R = G * DN_C
    NS = NC // G
    WA = DN_H * DN_C
    n = len(shards)

    def body(q_ref, k_ref, v_ref, bg_ref, *rest):
        ins, (u_ref, w_ref, qd_ref, kd_ref, at_ref, gl_ref, x_ref), gouts, sems = rest[:n], rest[n:n + 7], rest[n + 7:2 * n + 7], rest[2 * n + 7:]
        _exchange_start(ins, gouts, *sems, gather=True, when=pl.program_id(0) == 0)
        rows = [slice(j * DN_C, (j + 1) * DN_C) for j in range(G)]
        pre = [_dn_intra_pre(q_ref[r, :], k_ref[r, :], v_ref[r, :], bg_ref[r, :]) for r in rows]
        inv = _unit_lower_inverses([hd[-1] for heads in pre for hd in heads])
        for j, r in enumerate(rows):
            u, w, qd, kd, at, gl, xs = _dn_intra_post(pre[j], inv[j * DN_H:(j + 1) * DN_H])
            u_ref[r, :], x_ref[r, :] = u, xs
            w_ref[r, :], qd_ref[r, :], kd_ref[r, :], at_ref[r, :] = _b(w), _b(qd), _b(kd), _b(at)
            gl_ref[j] = gl
        _exchange_wait(ins, gouts, *sems, gather=True, when=pl.program_id(0) == NS - 1)

    row = lambda wd: pl.BlockSpec((R, wd), lambda s: (s, 0))
    anyspec = pl.BlockSpec(memory_space=pl.ANY)
    res = pl.pallas_call(
        body, name="dn_intra", grid=(NS,),
        in_specs=[row(W), row(W), row(W), row(HP)] + [anyspec] * n,
        out_specs=[row(W), row(W), row(W), row(W), row(WA), pl.BlockSpec((G, 1, HP), lambda s: (s, 0, 0)), row(WA)] + [anyspec] * n,
        out_shape=[SDS((L, W), f32), SDS((L, W), bf16), SDS((L, W), bf16), SDS((L, W), bf16), SDS((L, WA), bf16), SDS((NC, 1, HP), f32),
                   SDS((L, WA), f32)] + [SDS((N_DEV,) + a.shape, a.dtype) for a in shards],
        scratch_shapes=_exchange_sems(n), compiler_params=_cp())(q, k, v, bg, *shards)
    return res[:7], res[7:]


def _dn_scan(u, w, qd, kd, at, gl):
    L, W = u.shape
    NC = L // DN_C
    G = _pick(NC, DN_SCAN_GROUP)
    R = G * DN_C

    def body(u_ref, w_ref, qd_ref, kd_ref, at_ref, gl_ref, o_ref, vn_ref, s_ref, S):
        @pl.when(pl.program_id(0) == 0)
        def _():
            S[...] = jnp.zeros_like(S)

        for j in range(G):
            r = slice(j * DN_C, (j + 1) * DN_C)
            glrow = gl_ref[j]
            for h in range(DN_H):
                c = slice(h * DN_D, (h + 1) * DN_D)
                Sh = S[h]
                s_ref[j, h] = Sh
                Sb = _b(Sh)
                vb = _b(u_ref[r, c] - _dot(w_ref[r, c], Sb))
                vn_ref[r, c] = vb
                o_ref[r, c] = _dot(qd_ref[r, c], Sb) + _dot(at_ref[r, h * DN_C:(h + 1) * DN_C], vb)
                S[h] = Sh * jnp.exp(_lane_pick(glrow, h)) + _dot_tn(kd_ref[r, c], vb)

    row = lambda wd: pl.BlockSpec((R, wd), lambda n: (n, 0))
    return pl.pallas_call(
        body, name="dn_scan", grid=(NC // G,),
        in_specs=[row(W), row(W), row(W), row(W), row(DN_H * DN_C), pl.BlockSpec((G, 1, HP), lambda n: (n, 0, 0))],
        out_specs=[row(W), row(W), pl.BlockSpec((G, DN_H, DN_D, DN_D), lambda n: (n, 0, 0, 0))],
        out_shape=[SDS((L, W), f32), SDS((L, W), bf16), SDS((NC, DN_H, DN_D, DN_D), f32)],
        scratch_shapes=[pltpu.VMEM((DN_H, DN_D, DN_D), f32)], compiler_params=_cp())(u, w, qd, kd, at, gl)


def _dn_scan_bwd(do, w, qd, kd, at, gl):
    L, W = do.shape
    NC = L // DN_C
    G = _pick(NC, DN_SCAN_GROUP)
    R = G * DN_C
    NS = NC // G

    def body(do_ref, w_ref, qd_ref, kd_ref, at_ref, gl_ref, dvn_ref, ds_ref, dS):
        @pl.when(pl.program_id(0) == 0)
        def _():
            dS[...] = jnp.zeros_like(dS)

        for j in reversed(range(G)):
            r = slice(j * DN_C, (j + 1) * DN_C)
            glrow = gl_ref[j]
            for h in range(DN_H):
                c = slice(h * DN_D, (h + 1) * DN_D)
                dSo = dS[h]
                ds_ref[j, h] = dSo
                dob = _b(do_ref[r, c])
                dvn = _dot_tn(at_ref[r, h * DN_C:(h + 1) * DN_C], dob) + _dot(kd_ref[r, c], _b(dSo))
                dvn_ref[r, c] = dvn
                dS[h] = _dot_tn(qd_ref[r, c], dob) + dSo * jnp.exp(_lane_pick(glrow, h)) - _dot_tn(w_ref[r, c], _b(dvn))

    row = lambda wd: pl.BlockSpec((R, wd), lambda n: (NS - 1 - n, 0))
    return pl.pallas_call(
        body, name="dn_scan_bwd", grid=(NS,),
        in_specs=[row(W), row(W), row(W), row(W), row(DN_H * DN_C), pl.BlockSpec((G, 1, HP), lambda n: (NS - 1 - n, 0, 0))],
        out_specs=[row(W), pl.BlockSpec((G, DN_H, DN_D, DN_D), lambda n: (NS - 1 - n, 0, 0, 0))],
        out_shape=[SDS((L, W), f32), SDS((NC, DN_H, DN_D, DN_D), f32)],
        scratch_shapes=[pltpu.VMEM((DN_H, DN_D, DN_D), f32)], compiler_params=_cp())(do, w, qd, kd, at, gl)


def _dn_intra_bwd(q, k, v, bg, xinv, do, vn, dvn, states, dstates):
    L, W = q.shape
    NC = L // DN_C
    G = _pick(NC, DN_INTRA_GROUP)
    R = G * DN_C

    def body(q_ref, k_ref, v_ref, bg_ref, x_ref, do_ref, vn_ref, dvn_ref, s_ref, ds_ref, dq_ref, dk_ref, dv_ref, dbg_ref):
        lane1 = _iota((1, HP), 1)
        rows = [slice(j * DN_C, (j + 1) * DN_C) for j in range(G)]
        xs = [[x_ref[r, h * DN_C:(h + 1) * DN_C] for h in range(DN_H)] for r in rows]
        fwd, vjp = jax.vjp(functools.partial(_dn_intra_group, xs=xs), q_ref[...], k_ref[...], v_ref[...], bg_ref[...])
        dws, dqds, dkds, dats, dgls = [], [], [], [], []
        for j, r in enumerate(rows):
            dw, dqd, dkd, dat = [], [], [], []
            dgl = jnp.zeros((1, HP), f32)
            for h in range(DN_H):
                c = slice(h * DN_D, (h + 1) * DN_D)
                Sh, dSo = s_ref[j, h], ds_ref[j, h]
                Sb, dob, vb = _b(Sh), _b(do_ref[r, c]), vn_ref[r, c]
                dw.append(-_dot_nt(_b(dvn_ref[r, c]), Sb))
                dqd.append(_dot_nt(dob, Sb))
                dat.append(_dot_nt(dob, vb))
                dkd.append(_dot_nt(vb, _b(dSo)))
                dcd = jnp.sum(jnp.sum(Sh * dSo, axis=1, keepdims=True), axis=0, keepdims=True)
                dgl = dgl + jnp.where(lane1 == h, dcd * jnp.exp(_lane_pick(fwd[5][j], h)), 0.0)
            cat = lambda xs_: jnp.concatenate(xs_, axis=1)
            dws.append(cat(dw)), dqds.append(cat(dqd)), dkds.append(cat(dkd)), dats.append(cat(dat)), dgls.append(dgl)
        cat0 = lambda xs_: jnp.concatenate(xs_, axis=0)
        dq, dk, dv, dbg = vjp((dvn_ref[...], cat0(dws), cat0(dqds), cat0(dkds), cat0(dats), tuple(dgls)))
        dq_ref[...], dk_ref[...], dv_ref[...], dbg_ref[...] = dq, dk, dv, dbg

    row = lambda wd: pl.BlockSpec((R, wd), lambda s: (s, 0))
    st = pl.BlockSpec((G, DN_H, DN_D, DN_D), lambda s: (s, 0, 0, 0))
    return pl.pallas_call(
        body, name="dn_intra_bwd", grid=(NC // G,),
        in_specs=[row(W), row(W), row(W), row(HP), row(DN_H * DN_C), row(W), row(W), row(W), st, st],
        out_specs=[row(W), row(W), row(W), row(HP)],
        out_shape=[SDS((L, W), f32)] * 3 + [SDS((L, HP), f32)],
        compiler_params=_cp())(q, k, v, bg, xinv, do, vn, dvn, states, dstates)


def _dn_normgate(oraw, dz, wn):
    outs = []
    for h in range(DN_H):
        sl = slice(h * DN_D, (h + 1) * DN_D)
        z = dz[:, sl]
        outs.append(_rms(oraw[:, sl], wn) * (z * _sigmoid(z)))
    return jnp.concatenate(outs, axis=1)


def _mix_fwd(op, oraw, dz, ga, gb, h0, wn, wbf, wbd, wo):
    L, D = h0.shape
    TM = _pick(L, ROW_TILES)

    def body(op_ref, or_ref, dz_ref, ga_ref, gb_ref, h0_ref, wn_ref, wbf_ref, wbd_ref, wo_ref, h1_ref):
        pf = _dot(_b(op_ref[...]), wbf_ref[...])
        pd = _dot(_b(_dn_normgate(or_ref[...], dz_ref[...], wn_ref[...])), wbd_ref[...])
        y = _sigmoid(ga_ref[...]) * pf + _sigmoid(gb_ref[...]) * pd
        h1_ref[...] = h0_ref[...] + _dot(_b(y), wo_ref[...])

    row = lambda wd: pl.BlockSpec((TM, wd), lambda i: (i, 0))
    full = lambda a: pl.BlockSpec(a.shape, lambda i: (0, 0))
    return pl.pallas_call(
        body, name="mix_fwd", grid=(L // TM,),
        in_specs=[row(op.shape[1]), row(oraw.shape[1]), row(dz.shape[1]), row(D), row(D), row(D), full(wn), full(wbf), full(wbd), full(wo)],
        out_specs=row(D), out_shape=SDS((L, D), f32), compiler_params=_cp())(op, oraw, dz, ga, gb, h0, wn, wbf, wbd, wo)


def _mix_bwd(dh1, op, oraw, dz, ga, gb, wn, wbf, wbd, wo):
    L, D = dh1.shape
    TM = _pick(L, ROW_TILES)
    WF, WD = op.shape[1], oraw.shape[1]

    def body(dh1_ref, op_ref, or_ref, dz_ref, ga_ref, gb_ref, wn_ref, wbf_ref, wbd_ref, wo_ref,
             dop_ref, dor_ref, dg_ref, af_ref, ad_ref, dpf_ref, dpd_ref, y_ref, dmix_ref, acc_ref):
        @pl.when(pl.program_id(0) == 0)
        def _():
            acc_ref[...] = jnp.zeros_like(acc_ref)

        opv = op_ref[...]
        af = _b(opv)
        ad, vjp = jax.vjp(_dn_normgate, or_ref[...], dz_ref[...], wn_ref[...])
        adb = _b(ad)
        pf, pd = _dot(af, wbf_ref[...]), _dot(adb, wbd_ref[...])
        sa, sb = _sigmoid(ga_ref[...]), _sigmoid(gb_ref[...])
        dmix = _b(dh1_ref[...])
        dy = _dot_nt(dmix, wo_ref[...])
        dpf, dpd = _b(dy * sa), _b(dy * sb)
        dor, ddz, dwn = vjp(_dot_nt(dpd, wbd_ref[...]))
        dop = _dot_nt(dpf, wbf_ref[...])
        lane = _iota((TM, HP), 1)
        for h in range(WF // HP):
            c = slice(h * HP, (h + 1) * HP)
            delta = jnp.sum(jnp.where(lane < FOX_D, dop[:, c] * opv[:, c], 0.0), axis=1, keepdims=True)
            dop_ref[:, c] = _b(dop[:, c] - _spread3(jnp.broadcast_to(delta, (TM, HP)), lane, C_DELTA0))
        dor_ref[...] = dor
        dg_ref[:, 0:WD] = _b(ddz)
        dg_ref[:, WD:WD + D] = _b(dy * pf * sa * (1.0 - sa))
        dg_ref[:, WD + D:] = _b(dy * pd * sb * (1.0 - sb))
        af_ref[...], ad_ref[...], y_ref[...] = af, adb, _b(sa * pf + sb * pd)
        dpf_ref[...], dpd_ref[...], dmix_ref[...] = dpf, dpd, dmix
        acc_ref[0:1, :] += dwn

    row = lambda wd: pl.BlockSpec((TM, wd), lambda i: (i, 0))
    full = lambda a: pl.BlockSpec(a.shape, lambda i: (0, 0))
    return pl.pallas_call(
        body, name="mix_bwd", grid=(L // TM,),
        in_specs=[row(D), row(WF), row(WD), row(WD), row(D), row(D), full(wn), full(wbf), full(wbd), full(wo)],
        out_specs=[row(WF), row(WD), row(WD + 2 * D), row(WF), row(WD), row(D), row(D), row(D), row(D),
                   pl.BlockSpec((8, HP), lambda i: (0, 0))],
        out_shape=[SDS((L, WF), bf16), SDS((L, WD), f32), SDS((L, WD + 2 * D), bf16), SDS((L, WF), bf16), SDS((L, WD), bf16),
                   SDS((L, D), bf16), SDS((L, D), bf16), SDS((L, D), bf16), SDS((L, D), bf16), SDS((8, HP), f32)],
        compiler_params=_cp())(dh1, op, oraw, dz, ga, gb, wn, wbf, wbd, wo)


def _ffn_fwd_bwd(h1, tgt, w2, wf, wgt, wut, wd):
    L, D = h1.shape
    F = wd.shape[0]
    TM = _pick(L, FFN_TILES)
    gt, nb, _, tmaps = _shifted_blocks(TM)

    def body(h_ref, *refs):
        t_refs, (w2_ref, wf_ref, wg_hbm, wu_hbm, wd_hbm,
                 dh1_ref, xn_ref, dg_ref, du_ref, act_ref, dh2_ref, acc_ref, wg_v, wu_v, wd_v, sems) = refs[:nb], refs[nb:]
        i = pl.program_id(0)
        _load_once([(wg_hbm, wg_v), (wu_hbm, wu_v), (wd_hbm, wd_v)], sems)

        @pl.when(i == 0)
        def _():
            acc_ref[...] = jnp.zeros_like(acc_ref)

        h1v = h_ref[...]
        xn2, vjp2 = jax.vjp(_rms, h1v, w2_ref[...])
        xb = _b(xn2)
        g, u = _dot_nt(xb, wg_v[...]), _dot_nt(xb, wu_v[...])
        sg = _sigmoid(g)
        ab = _b(g * sg * u)
        h2 = h1v + _dot(ab, wd_v[...])
        out, vjpf = jax.vjp(_rms, h2, wf_ref[...])
        valid = (i * TM + _iota((TM, 1), 0)) >= PREFIX
        diff = jnp.where(valid, out - jnp.concatenate([r[...] for r in t_refs], axis=0), 0.0)
        loss = 0.5 * jnp.sum(jnp.sum(diff * diff, axis=1, keepdims=True), axis=0, keepdims=True) / D
        dh2, dwf = vjpf(diff * (1.0 / D))
        dh2b = _b(dh2)
        dact = _dot_nt(dh2b, wd_v[...])
        dgb = _b(dact * u * (sg * (1.0 + g * (1.0 - sg))))
        dub = _b(dact * (g * sg))
        dh1n, dw2 = vjp2(_dot(dgb, wg_v[...]) + _dot(dub, wu_v[...]))
        dh1_ref[...] = dh2 + dh1n
        xn_ref[...], dg_ref[...], du_ref[...], act_ref[...], dh2_ref[...] = xb, dgb, dub, ab, dh2b
        acc_ref[0:1, :] += dw2
        acc_ref[1:2, :] += dwf
        acc_ref[2:3, :] += jnp.broadcast_to(loss, (1, D))

    row = lambda wd_: pl.BlockSpec((TM, wd_), lambda i: (i, 0))
    vec = pl.BlockSpec((1, D), lambda i: (0, 0))
    anyspec = pl.BlockSpec(memory_space=pl.ANY)
    return pl.pallas_call(
        body, name="ffn_fwd_bwd", grid=(L // TM,),
        in_specs=[row(D)] + [pl.BlockSpec((gt, D), m) for m in tmaps] + [vec, vec, anyspec, anyspec, anyspec],
        out_specs=[row(D), row(D), row(F), row(F), row(F), row(D), pl.BlockSpec((8, D), lambda i: (0, 0))],
        out_shape=[SDS((L, D), f32), SDS((L, D), bf16), SDS((L, F), bf16), SDS((L, F), bf16), SDS((L, F), bf16), SDS((L, D), bf16),
                   SDS((8, D), f32)],
        scratch_shapes=[pltpu.VMEM((F, D), bf16), pltpu.VMEM((F, D), bf16), pltpu.VMEM((F, D), bf16), pltpu.SemaphoreType.DMA((3,))],
        compiler_params=_cp())(h1, *([tgt] * nb), w2, wf, wgt, wut, wd)


def _pad_lanes(v, n=HP):
    return jnp.pad(v.astype(f32), ((0, 0), (0, n - v.shape[1])))


def _pack_w_in(wt_full):
    D = wt_full.shape[1]
    FW, DW = FOX_H * FOX_D, DN_H * DN_D
    o = 0
    parts = {}
    for name, wd in (("fq", FW), ("fk", FW), ("fv", FW), ("fl", FOX_H), ("dn", 3 * DW), ("ba", 2 * DN_H), ("dz", DW), ("ga", D), ("gb", D)):
        parts[name] = wt_full[o:o + wd]
        o += wd
    assert o == wt_full.shape[0]
    heads = lambda w: jnp.pad(w.reshape(FOX_H, FOX_D, D), ((0, 0), (0, HP - FOX_D), (0, 0))).reshape(FOX_H * HP, D)
    small = lambda w: jnp.pad(w, ((0, SMALL_W - w.shape[0]), (0, 0)))
    packed = dict(fq=heads(parts["fq"]), fk=heads(parts["fk"]), fv=heads(parts["fv"]), sf=small(parts["fl"]), sd=small(parts["ba"]),
                  dn=parts["dn"], dz=parts["dz"], ga=parts["ga"], gb=parts["gb"])
    return jnp.concatenate([packed[name] for name, _, _, _ in _seg_layout(D)], axis=0)


def _unpack_w_in(groups, d_model):
    D = groups[0].shape[1]
    FW = FOX_H * FOX_D
    segs = {}
    for grp, g in zip(GROUPS, groups):
        o = 0
        for name, wd, _, sg in _seg_layout(d_model):
            if sg == grp:
                segs[name] = g[o:o + wd]
                o += wd
    heads = lambda g: g.reshape(FOX_H, HP, D)[:, :FOX_D].reshape(FW, D)
    return jnp.concatenate([heads(segs["fq"]), heads(segs["fk"]), heads(segs["fv"]), segs["sf"][:FOX_H], segs["dn"],
                            segs["sd"][:2 * DN_H], segs["dz"], segs["ga"], segs["gb"]], axis=0)


def _local_step(x, tgt, meta, w1, w_in_t, fbias, cw, alog, dtb, wn, w2, wf, late_shards):
    T, D = x.shape
    pre = jnp.concatenate([jnp.zeros((N_PAD, D), f32), meta], axis=0)
    wp = _pack_w_in(w_in_t)
    bias_p, alog_p, dt_p = _pad_lanes(fbias), _pad_lanes(jnp.pad(alog, ((0, 0), (DN_H, 0)))), _pad_lanes(jnp.pad(dtb, ((0, 0), (DN_H, 0))))

    (h0, xn, fq, fk, sf, fv, dn, sd, dz, ga, gb), g_mix = _in_proj(x, pre, w1, wp, [late_shards[n] for n in LATE_MIX])
    qa, ka, va = _fox_prep(fq, fk, fv, sf, bias_p)
    op, qb, g_ffn = _fox_fwd(qa, ka, va, [late_shards[n] for n in LATE_FFN])
    qn, kn, vn, bg = _dn_prep(dn, sd, cw, alog_p, dt_p)
    (u_dn, w_dn, qd_dn, kd_dn, at_dn, gl_dn, x_dn), g_down = _dn_intra(qn, kn, vn, bg, [late_shards[n] for n in LATE_DOWN])
    full = {n: _from_slabs(n, s) for n, s in zip(LATE_MIX + LATE_FFN + LATE_DOWN, tuple(g_mix) + tuple(g_ffn) + tuple(g_down))}
    wbf, wbd, wo, wgt, wut, wd = (full[n] for n in ("w_branch_fox", "w_branch_dn", "w_out", "w_ffn_gate", "w_ffn_up", "w_ffn_down"))
    wbf_p = jnp.pad(wbf.reshape(FOX_H, FOX_D, D), ((0, 0), (0, HP - FOX_D), (0, 0))).reshape(FOX_H * HP, D)
    oraw, vnew, states = _dn_scan(u_dn, w_dn, qd_dn, kd_dn, at_dn, gl_dn)
    h1 = _mix_fwd(op, oraw, dz, ga, gb, h0, wn, wbf_p, wbd, wo)

    dh1, xn2, dgate, dup, act, dh2, acc_f = _ffn_fwd_bwd(h1, tgt, w2, wf, wgt, wut, wd)
    g_wg, g_wu, g_wd = _matmul_tn(dgate, xn2, "dw_ffn_gate"), _matmul_tn(dup, xn2, "dw_ffn_up"), _matmul_tn(act, dh2, "dw_ffn_down")

    dop, dor, d_mix, af, ad, dpf, dpd, yb, dmix, acc_m = _mix_bwd(dh1, op, oraw, dz, ga, gb, wn, wbf_p, wbd, wo)
    g_wbf = _matmul_tn(af, dpf, "dw_branch_fox").reshape(FOX_H, HP, D)[:, :FOX_D].reshape(FOX_H * FOX_D, D)
    g_wbd, g_wo = _matmul_tn(ad, dpd, "dw_branch_dn"), _matmul_tn(yb, dmix, "dw_out")

    dvnew, dstates = _dn_scan_bwd(dor, w_dn, qd_dn, kd_dn, at_dn, gl_dn)
    dqn, dkn, dvn, dbg = _dn_intra_bwd(qn, kn, vn, bg, x_dn, dor, vnew, dvnew, states, dstates)
    d_dn, acc_cw, acc_p = _dn_prep_bwd(dn, sd, cw, alog_p, dt_p, dqn, dkn, dvn, dbg)
    g_late = dict(w_branch_fox=g_wbf, w_branch_dn=g_wbd, w_out=g_wo, w_ffn_gate=g_wg, w_ffn_up=g_wu, w_ffn_down=g_wd)
    (dqa, dka, d_fv), recv = _fox_bwd(qb, ka, va, dop, [_to_slabs(n, g_late[n]) for n in LATE])
    d_fox, acc_b = _fox_prep_bwd(dqa, dka, sf, bias_p)

    dgroups = [d_fox, d_fv, d_dn, d_mix]
    g_wp = [_matmul_tn(dg, xn, "dw_in_" + grp) for grp, dg in zip(GROUPS, dgroups)]
    dh0, acc_1, (recv_w_in,) = _in_proj_bwd(dgroups, wp, h0, w1, dh1, [_to_slabs("w_in", _unpack_w_in(g_wp, D))])
    recv = dict(zip(LATE, recv), w_in=recv_w_in)

    small = dict(loss=acc_f[2, 0:1], mix_norm_w=acc_1[0], fox_forget_bias=acc_b[0, :FOX_H], dn_a_log=acc_p[0, DN_H:2 * DN_H],
                 dn_dt_bias=acc_p[1, DN_H:2 * DN_H], dn_out_norm_w=acc_m[0], ffn_norm_w=acc_f[0], final_norm_w=acc_f[1],
                 meta_tokens=dh0[N_PAD:PREFIX].reshape(-1), dn_conv_w=acc_cw[:CONV_K].reshape(-1))
    return dh0[PREFIX:], small, recv


def _mesh_pos():
    x, y, c = lax.axis_index("x"), lax.axis_index("y"), lax.axis_index("c")
    return x, y, c, 4 * x + 2 * y + c


def _peer(x, y, c, m):
    flip = lambda v, on: 1 - v if on else v
    px, py, pc = flip(x, m & 4), flip(y, m & 2), flip(c, m & 1)
    return (px, py, pc), 4 * px + 2 * py + pc


def _exchange_sems(n):
    return [pltpu.SemaphoreType.DMA((n, N_DEV - 1)), pltpu.SemaphoreType.DMA((n, N_DEV - 1)), pltpu.SemaphoreType.DMA((n,))]


def _exchange_part(ins, outs, send_sems, recv_sems, loc_sems, gather, m, receive):
    x, y, c, me = _mesh_pos()
    src = lambda a, pid: ins[a] if gather else ins[a].at[pid]
    if m == 0:
        return [pltpu.make_async_copy(src(a, me), outs[a].at[me], loc_sems.at[a]) for a in range(len(ins))]
    peer, pid = _peer(x, y, c, m)
    return [pltpu.make_async_remote_copy(src_ref=src(a, pid), dst_ref=outs[a].at[pid if receive else me], send_sem=send_sems.at[a, m - 1],
                                         recv_sem=recv_sems.at[a, m - 1], device_id=peer, device_id_type=MESH) for a in range(len(ins))]


def _exchange_start(*refs, gather, when):
    @pl.when(when)
    def _():
        for m in range(N_DEV):
            for cp in _exchange_part(*refs, gather, m, receive=False):
                cp.start()


def _exchange_wait(*refs, gather, when):
    @pl.when(when)
    def _():
        for m in range(1, N_DEV):
            for cp in _exchange_part(*refs, gather, m, receive=True):
                cp.wait_recv()
        for m in list(range(1, N_DEV)) + [0]:
            for cp in _exchange_part(*refs, gather, m, receive=False):
                cp.wait() if m == 0 else cp.wait_send()


def _gather_two_level(arrays, name):
    n = len(arrays)

    def body(*refs):
        ins, outs, (send_sems, recv_sems, loc_sems) = refs[:n], refs[n:2 * n], refs[2 * n:]
        x, y, c, me = _mesh_pos()
        sib = (x, y, 1 - c)
        chips = [(1 - x, y), (x, 1 - y), (1 - x, 1 - y)]
        dev_id = lambda px, py, pc: 4 * px + 2 * py + pc

        def copy(a, k, block, to, own=False):
            return pltpu.make_async_remote_copy(src_ref=ins[a] if own else outs[a].at[block], dst_ref=outs[a].at[block],
                                                send_sem=send_sems.at[a, k], recv_sem=recv_sems.at[a, k], device_id=to, device_id_type=MESH)

        local = [pltpu.make_async_copy(ins[a], outs[a].at[me], loc_sems.at[a]) for a in range(n)]
        first = [copy(a, 0, me, sib, own=True) for a in range(n)]
        first += [copy(a, 1 + j, me, (*chip, c), own=True) for j, chip in enumerate(chips) for a in range(n)]
        for cp in local + first:
            cp.start()
        passed = []
        for j, chip in enumerate(chips):
            for a in range(n):
                copy(a, 1 + j, dev_id(*chip, c), sib).wait_recv()
                cp = copy(a, 4 + j, dev_id(*chip, c), sib)
                cp.start()
                passed.append(cp)
        for a in range(n):
            copy(a, 0, dev_id(x, y, 1 - c), sib).wait_recv()
        for j, chip in enumerate(chips):
            for a in range(n):
                copy(a, 4 + j, dev_id(*chip, 1 - c), sib).wait_recv()
        for cp in first + passed:
            cp.wait_send()
        for cp in local:
            cp.wait()

    anyspec = pl.BlockSpec(memory_space=pl.ANY)
    return pl.pallas_call(
        body, name=name, in_specs=[anyspec] * n, out_specs=[anyspec] * n,
        out_shape=[SDS((N_DEV,) + a.shape, a.dtype) for a in arrays],
        scratch_shapes=_exchange_sems(n))(*arrays)


def _all_reduce_small(v):
    R = v.shape[0]

    def body(v_ref, o_ref, gath, send_sems, recv_sems):
        x, y, c, me = _mesh_pos()
        gath[me] = v_ref[...]
        sends = []
        for m in range(1, N_DEV):
            peer, _ = _peer(x, y, c, m)
            cp = pltpu.make_async_remote_copy(src_ref=v_ref, dst_ref=gath.at[me], send_sem=send_sems.at[m - 1],
                                              recv_sem=recv_sems.at[m - 1], device_id=peer, device_id_type=MESH)
            cp.start()
            sends.append(cp)
        for m in range(1, N_DEV):
            peer, pid = _peer(x, y, c, m)
            pltpu.make_async_remote_copy(src_ref=v_ref, dst_ref=gath.at[pid], send_sem=send_sems.at[m - 1],
                                         recv_sem=recv_sems.at[m - 1], device_id=peer, device_id_type=MESH).wait_recv()
        for cp in sends:
            cp.wait_send()
        tot = gath[0]
        for d in range(1, N_DEV):
            tot = tot + gath[d]
        o_ref[...] = tot

    vm = pl.BlockSpec(memory_space=pltpu.VMEM)
    return pl.pallas_call(
        body, name="all_reduce_small", in_specs=[vm], out_specs=vm, out_shape=SDS((R, HP), f32),
        scratch_shapes=[pltpu.VMEM((N_DEV, R, HP), f32), pltpu.SemaphoreType.DMA((N_DEV - 1,)), pltpu.SemaphoreType.DMA((N_DEV - 1,))],
        )(v)


def _adamw_math(w, g, m, v):
    m = ADAM_B1 * m + (1.0 - ADAM_B1) * g
    v = ADAM_B2 * v + (1.0 - ADAM_B2) * (g * g)
    m_hat = m / (1.0 - ADAM_B1 ** ADAM_STEP)
    v_hat = v / (1.0 - ADAM_B2 ** ADAM_STEP)
    return -ADAM_LR * (m_hat / (jnp.sqrt(v_hat) + ADAM_EPS) + ADAM_WD * w), m, v


def _adamw(g, w, m, v, name):
    R, Cc = w.shape[-2:]
    if R <= ADAMW_WHOLE_ROWS or R % HP == 0:
        TR, TC = (R if R <= ADAMW_WHOLE_ROWS else _pick(R, ADAMW_TILES)), Cc
    else:
        TR, TC = R, _pick(Cc, ADAMW_TILES)
    slabs = g.ndim == 3
    lead = w.ndim - 2

    def body(g_ref, w_ref, m_ref, v_ref, go_ref, d_ref, mo_ref, vo_ref):
        if slabs:
            gs = g_ref[0].astype(f32)
            for k in range(1, N_DEV):
                gs = gs + g_ref[k].astype(f32)
        else:
            gs = g_ref[...]
        at = 0 if lead else Ellipsis
        d, mn, vn = _adamw_math(w_ref[at], gs, m_ref[at], v_ref[at])
        go_ref[at], d_ref[at], mo_ref[at], vo_ref[at] = gs, d, mn, vn

    grid = (R // TR, Cc // TC)
    blk = pl.BlockSpec((1,) * lead + (TR, TC), lambda i, j: (0,) * lead + (i, j))
    gblk = pl.BlockSpec((N_DEV, TR, TC), lambda i, j: (0, i, j)) if slabs else pl.BlockSpec((TR, TC), lambda i, j: (i, j))
    return pl.pallas_call(
        body, name=name, grid=grid, in_specs=[gblk, blk, blk, blk], out_specs=[blk] * 4,
        out_shape=[SDS(w.shape, f32)] * 4, compiler_params=_cp(2))(g, w, m, v)


def _adamw_small(gs, ws, ms, vs):
    k = len(ws)

    def body(*refs):
        ins, outs = refs[:4 * k], refs[4 * k:]
        for t in range(k):
            g, w, m, v = (ins[j * k + t][...] for j in range(4))
            outs[t][...], outs[k + t][...], outs[2 * k + t][...] = _adamw_math(w, g, m, v)

    whole = lambda a: pl.BlockSpec(a.shape, lambda i, nd=a.ndim: (0,) * nd)
    res = pl.pallas_call(
        body, name="adamw_small", grid=(1,), in_specs=[whole(a) for a in (*gs, *ws, *ms, *vs)],
        out_specs=[whole(a) for a in ws] * 3, out_shape=[SDS(a.shape, f32) for a in ws] * 3,
        compiler_params=_cp())(*gs, *ws, *ms, *vs)
    return [[res[j * k + t] for j in range(3)] for t in range(k)]


WEIGHTS = ("meta_tokens", "mix_norm_w", "w_in", "fox_forget_bias", "dn_conv_w", "dn_a_log", "dn_dt_bias", "dn_out_norm_w",
           "w_branch_fox", "w_branch_dn", "w_out", "ffn_norm_w", "w_ffn_gate", "w_ffn_up", "w_ffn_down", "final_norm_w")
COL_SHARDED = ("w_in", "w_branch_fox", "w_branch_dn", "w_ffn_gate", "w_ffn_up")
ROW_SHARDED = ("w_out", "w_ffn_down")
BIG = COL_SHARDED + ROW_SHARDED
LATE = tuple(n for n in BIG if n != "w_in")
LATE_MIX = ("w_branch_fox", "w_branch_dn", "w_out")
LATE_FFN = ("w_ffn_gate", "w_ffn_up")
LATE_DOWN = ("w_ffn_down",)
SMALL = tuple(n for n in WEIGHTS if n not in BIG)
TRANSPOSED = ("w_in", "w_ffn_gate", "w_ffn_up")


def _to_slabs(name, g):
    r, c = g.shape
    if name in COL_SHARDED and name not in TRANSPOSED:
        return _b(g.reshape(r, N_DEV, c // N_DEV).transpose(1, 0, 2))
    return _b(g.reshape(N_DEV, r // N_DEV, c))


def _from_slabs(name, s):
    n, r, c = s.shape
    if name in COL_SHARDED and name not in TRANSPOSED:
        return s.transpose(1, 0, 2).reshape(r, n * c)
    return s.reshape(n * r, c)


def kernel(x, meta_tokens, mix_norm_w, w_in, fox_forget_bias, dn_conv_w, dn_a_log, dn_dt_bias, dn_out_norm_w, w_branch_fox, w_branch_dn, w_out, ffn_norm_w, w_ffn_gate, w_ffn_up, w_ffn_down, final_norm_w, loss_target, m_meta_tokens, m_mix_norm_w, m_w_in, m_fox_forget_bias, m_dn_conv_w, m_dn_a_log, m_dn_dt_bias, m_dn_out_norm_w, m_w_branch_fox, m_w_branch_dn, m_w_out, m_ffn_norm_w, m_w_ffn_gate, m_w_ffn_up, m_w_ffn_down, m_final_norm_w, v_meta_tokens, v_mix_norm_w, v_w_in, v_fox_forget_bias, v_dn_conv_w, v_dn_a_log, v_dn_dt_bias, v_dn_out_norm_w, v_w_branch_fox, v_w_branch_dn, v_w_out, v_ffn_norm_w, v_w_ffn_gate, v_w_ffn_up, v_w_ffn_down, v_final_norm_w):
    w = dict(meta_tokens=meta_tokens, mix_norm_w=mix_norm_w, w_in=w_in, fox_forget_bias=fox_forget_bias, dn_conv_w=dn_conv_w, dn_a_log=dn_a_log, dn_dt_bias=dn_dt_bias, dn_out_norm_w=dn_out_norm_w, w_branch_fox=w_branch_fox, w_branch_dn=w_branch_dn, w_out=w_out, ffn_norm_w=ffn_norm_w, w_ffn_gate=w_ffn_gate, w_ffn_up=w_ffn_up, w_ffn_down=w_ffn_down, final_norm_w=final_norm_w)
    mom = dict(meta_tokens=m_meta_tokens, mix_norm_w=m_mix_norm_w, w_in=m_w_in, fox_forget_bias=m_fox_forget_bias, dn_conv_w=m_dn_conv_w, dn_a_log=m_dn_a_log, dn_dt_bias=m_dn_dt_bias, dn_out_norm_w=m_dn_out_norm_w, w_branch_fox=m_w_branch_fox, w_branch_dn=m_w_branch_dn, w_out=m_w_out, ffn_norm_w=m_ffn_norm_w, w_ffn_gate=m_w_ffn_gate, w_ffn_up=m_w_ffn_up, w_ffn_down=m_w_ffn_down, final_norm_w=m_final_norm_w)
    var = dict(meta_tokens=v_meta_tokens, mix_norm_w=v_mix_norm_w, w_in=v_w_in, fox_forget_bias=v_fox_forget_bias, dn_conv_w=v_dn_conv_w, dn_a_log=v_dn_a_log, dn_dt_bias=v_dn_dt_bias, dn_out_norm_w=v_dn_out_norm_w, w_branch_fox=v_w_branch_fox, w_branch_dn=v_w_branch_dn, w_out=v_w_out, ffn_norm_w=v_ffn_norm_w, w_ffn_gate=v_w_ffn_gate, w_ffn_up=v_w_ffn_up, w_ffn_down=v_w_ffn_down, final_norm_w=v_final_norm_w)
    two_d = lambda a: a.reshape(a.shape[-2:]) if a.ndim >= 2 else a.reshape(1, -1)
    me = 4 * lax.axis_index("x") + 2 * lax.axis_index("y") + lax.axis_index("c")
    for d in (w, mom, var):
        for n in TRANSPOSED:
            d[n] = jnp.swapaxes(d[n], -1, -2)

    g_in, g_meta, g_cw = _gather_two_level([_b(two_d(w["w_in"])), two_d(w["meta_tokens"]), two_d(w["dn_conv_w"])], "all_gather_early")
    meta = g_meta.transpose(1, 0, 2).reshape(N_META, -1)
    cw = g_cw.transpose(1, 0, 2).reshape(CONV_K, -1)

    gx, g_small, recv = _local_step(
        x[0], loss_target[0], meta, two_d(w["mix_norm_w"]), _from_slabs("w_in", g_in), two_d(w["fox_forget_bias"]), cw, two_d(w["dn_a_log"]),
        two_d(w["dn_dt_bias"]), two_d(w["dn_out_norm_w"]), two_d(w["ffn_norm_w"]), two_d(w["final_norm_w"]),
        {n: _b(two_d(w[n])) for n in LATE})

    order = ("loss",) + SMALL
    flat = jnp.concatenate([g_small[n].reshape(-1) for n in order])
    rows = -(-flat.shape[0] // (8 * HP)) * 8
    tot = _all_reduce_small(jnp.pad(flat, (0, rows * HP - flat.shape[0])).reshape(rows, HP)).reshape(-1)
    summed, o = {}, 0
    for n in order:
        k = g_small[n].shape[0]
        summed[n] = tot[o:o + k]
        o += k
    loss = summed["loss"][0]
    d_model = x.shape[-1]
    mcols, ccols = d_model // N_DEV, dn_conv_w.shape[-1]
    summed["meta_tokens"] = lax.dynamic_slice(summed["meta_tokens"].reshape(N_META, d_model), (0, me * mcols), (N_META, mcols)).reshape(-1)
    summed["dn_conv_w"] = lax.dynamic_slice(summed["dn_conv_w"].reshape(CONV_K, ccols * N_DEV), (0, me * ccols), (CONV_K, ccols)).reshape(-1)

    res = {}
    for n in BIG:
        res[n] = _adamw(recv[n], w[n], mom[n], var[n], "adamw_" + n)
        if n in TRANSPOSED:
            res[n] = [jnp.swapaxes(r, -1, -2) for r in res[n]]
    gs = [summed[n].reshape(two_d(w[n]).shape) for n in SMALL]
    sres = _adamw_small(gs, *[[two_d(d[n]) for n in SMALL] for d in (w, mom, var)])
    for n, g, r in zip(SMALL, gs, sres):
        res[n] = [a.reshape(w[n].shape) for a in (g, *r)]
    return (loss, gx[None], *[res[n][0] for n in WEIGHTS], *[res[n][1] for n in WEIGHTS], *[res[n][2] for n in WEIGHTS], *[res[n][3] for n in WEIGHTS])
```

```python
import functools
import math

import jax
import jax.numpy as jnp
from jax import lax
from jax.experimental import pallas as pl
from jax.experimental.pallas import tpu as pltpu

f32, bf16 = jnp.float32, jnp.bfloat16
HI = lax.Precision.HIGHEST
MESH = pl.DeviceIdType.MESH
SDS = jax.ShapeDtypeStruct

N_DEV = 8
N_META = 16
PREFIX = 128
N_PAD = PREFIX - N_META
FOX_H, FOX_D = 8, 64
DN_H, DN_D = 4, 128
DN_C = 64
CONV_K = 4
HP = 128
SMALL_W = 256
EPS = 1e-6
NEG = -1e30
C_Q0, C_K0 = 64, 67
LSE_COL = 64
LOG2E, LN2 = 1.4426950408889634, 0.6931471805599453
C_LSE0, C_DELTA0 = 70, 65

ADAM_LR, ADAM_B1, ADAM_B2, ADAM_EPS, ADAM_WD, ADAM_STEP = 0.001, 0.9, 0.999, 1e-08, 0.01, 10

VMEM_LIMIT_V7X = 56 * 1024 * 1024
ROW_TILES = (384, 128)
ATTN_TILES = (384, 128)
FFN_TILES = (192, 64)
FOX_HEAD_GROUP = 4
FOX_HEAD_GROUP_FWD = 8
ADAMW_TILES = (256, 128)
ADAMW_WHOLE_ROWS = 512
MAX_WGRAD_BLOCK = 1408
DN_INTRA_GROUP = (6, 3, 2, 1)
DN_SCAN_GROUP = (6, 3, 2, 1)


def _pick(n, cands):
    for c in cands:
        if n % c == 0:
            return c
    raise ValueError(f"no tile of {cands} divides {n}")


def _cp(n_axes=1):
    return pltpu.CompilerParams(dimension_semantics=("arbitrary",) * n_axes, vmem_limit_bytes=VMEM_LIMIT_V7X)


def _b(x):
    return x.astype(bf16)


def _dot(a, b):
    return jnp.dot(a, b, preferred_element_type=f32)


def _dot_nt(a, b):
    return lax.dot_general(a, b, (((1,), (1,)), ((), ())), preferred_element_type=f32)


def _dot_tn(a, b):
    return lax.dot_general(a, b, (((0,), (0,)), ((), ())), preferred_element_type=f32)


def _dot_hi(a, b):
    return jnp.dot(a, b, preferred_element_type=f32, precision=HI)


def _iota(shape, dim):
    return lax.broadcasted_iota(jnp.int32, shape, dim)


def _rms(x, w):
    return x * lax.rsqrt(jnp.mean(x * x, axis=-1, keepdims=True) + EPS) * w


def _sigmoid(x):
    return jax.nn.sigmoid(x)


def _load_once(pairs, sems):
    @pl.when(pl.program_id(0) == 0)
    def _():
        cps = [pltpu.make_async_copy(src, dst, sems.at[k]) for k, (src, dst) in enumerate(pairs)]
        for cp in cps:
            cp.start()
        for cp in cps:
            cp.wait()


def _seg_layout(d_model):
    return (("fq", FOX_H * HP, bf16, "fox"), ("fk", FOX_H * HP, bf16, "fox"), ("sf", SMALL_W, f32, "fox"),
            ("fv", FOX_H * HP, bf16, "fv"),
            ("dn", 3 * DN_H * DN_D, f32, "dn"), ("sd", SMALL_W, f32, "dn"),
            ("dz", DN_H * DN_D, f32, "mix"), ("ga", d_model, f32, "mix"), ("gb", d_model, f32, "mix"))


GROUPS = ("fox", "fv", "dn", "mix")


def _shifted_blocks(TM):
    g = math.gcd(TM, PREFIX)
    nb, npre = TM // g, PREFIX // g
    assert nb >= npre
    return g, nb, npre, [lambda i, j=j: (jnp.maximum(i * nb + j - npre, 0), 0) for j in range(nb)]


def _in_proj(x, pre, w1, wpt, shards):
    T, D = x.shape
    L = T + PREFIX
    NP = wpt.shape[0]
    TM = _pick(L, ROW_TILES)
    NT = L // TM
    g, nb, npre, xmaps = _shifted_blocks(TM)
    segs = _seg_layout(D)
    ns, n = len(segs), len(shards)
    offs, o = [], 0
    for _, wd, _, _ in segs:
        offs.append(o)
        o += wd
    assert o == NP

    def body(*refs):
        x_refs, (pre_ref, w1_ref, wp_hbm), rest = refs[:nb], refs[nb:nb + 3], refs[nb + 3:]
        ins, h_ref, xn_ref, outs, gouts = rest[:n], rest[n], rest[n + 1], rest[n + 2:n + 2 + ns], rest[n + 2 + ns:2 * n + 2 + ns]
        wp_v, sems = rest[2 * n + 2 + ns:2 * n + 4 + ns]
        xsems = rest[2 * n + 4 + ns:]
        _load_once([(wp_hbm, wp_v)], sems)

        _exchange_start(ins, gouts, *xsems, gather=True, when=pl.program_id(0) == 0)

        first = pl.program_id(0) == 0
        h = jnp.concatenate([jnp.where(first, pre_ref[j * g:(j + 1) * g, :], r[...]) if j < npre else r[...]
                             for j, r in enumerate(x_refs)], axis=0)
        h_ref[...] = h
        xn = _b(_rms(h, w1_ref[...]))
        xn_ref[...] = xn
        for o_ref, off, (_, wd, _, _) in zip(outs, offs, segs):
            o_ref[...] = _dot_nt(xn, wp_v[off:off + wd, :]).astype(o_ref.dtype)

        _exchange_wait(ins, gouts, *xsems, gather=True, when=pl.program_id(0) == NT - 1)

    row = lambda wd: pl.BlockSpec((TM, wd), lambda i: (i, 0))
    anyspec = pl.BlockSpec(memory_space=pl.ANY)
    res = pl.pallas_call(
        body, name="in_proj", grid=(NT,),
        in_specs=[pl.BlockSpec((g, D), m) for m in xmaps] + [pl.BlockSpec((PREFIX, D), lambda i: (0, 0)), pl.BlockSpec((1, D), lambda i: (0, 0)), anyspec]
        + [anyspec] * n,
        out_specs=[row(D), row(D)] + [row(wd) for _, wd, _, _ in segs] + [anyspec] * n,
        out_shape=[SDS((L, D), f32), SDS((L, D), bf16)] + [SDS((L, wd), dt) for _, wd, dt, _ in segs]
        + [SDS((N_DEV,) + a.shape, a.dtype) for a in shards],
        scratch_shapes=[pltpu.VMEM((NP, D), bf16), pltpu.SemaphoreType.DMA((1,))] + _exchange_sems(n),
        compiler_params=_cp())(*([x] * nb), pre, w1, wpt, *shards)
    return res[:2 + ns], res[2 + ns:]


def _in_proj_bwd(dgroups, wpt, h0, w1, dh1, slabs):
    L, D = h0.shape
    NP = wpt.shape[0]
    TM = _pick(L, ROW_TILES)
    NT = L // TM
    assert NT >= 3 and TM >= PREFIX
    widths = [g.shape[1] for g in dgroups]
    assert sum(widths) == NP
    ng, n = len(dgroups), len(slabs)

    def body(*refs):
        dg_refs, (wp_hbm, h_ref, w1_ref, dh1_ref) = refs[:ng], refs[ng:ng + 4]
        ins, (dpre_ref, acc_ref, gx_hbm), outs = refs[ng + 4:ng + 4 + n], refs[ng + 4 + n:ng + 7 + n], refs[ng + 7 + n:ng + 7 + 2 * n]
        wp_v, sems, buf, gsem = refs[ng + 7 + 2 * n:ng + 11 + 2 * n]
        xsems = refs[ng + 11 + 2 * n:]
        _load_once([(wp_hbm, wp_v)], sems)
        i = pl.program_id(0)
        slot = i % 2

        def first_copy():
            return pltpu.make_async_copy(buf.at[0, pl.ds(PREFIX, TM - PREFIX)], gx_hbm.at[pl.ds(0, TM - PREFIX)], gsem.at[0])

        def tile_copy(step, slot_):
            return pltpu.make_async_copy(buf.at[slot_], gx_hbm.at[pl.ds(step * TM - PREFIX, TM)], gsem.at[slot_])

        @pl.when(i == 0)
        def _():
            acc_ref[...] = jnp.zeros_like(acc_ref)

        if TM > PREFIX:
            @pl.when(i == 2)
            def _():
                first_copy().wait()

        @pl.when(i > 2)
        def _():
            tile_copy(i - 2, slot).wait()

        _exchange_start(ins, outs, *xsems, gather=False, when=i == 0)

        dxn, off = None, 0
        for g_ref, wd in zip(dg_refs, widths):
            part = _dot(g_ref[...], wp_v[off:off + wd, :])
            dxn = part if dxn is None else dxn + part
            off += wd
        _, vjp = jax.vjp(_rms, h_ref[...], w1_ref[...])
        dh0n, dw1 = vjp(dxn)
        dh0 = dh1_ref[...] + dh0n
        buf[slot] = dh0
        acc_ref[0:1, :] += dw1

        @pl.when(i == 0)
        def _():
            dpre_ref[...] = dh0[:PREFIX]
            if TM > PREFIX:
                first_copy().start()

        @pl.when(i > 0)
        def _():
            tile_copy(i, slot).start()

        @pl.when(i == NT - 1)
        def _():
            tile_copy(i - 1, 1 - slot).wait()
            tile_copy(i, slot).wait()

        _exchange_wait(ins, outs, *xsems, gather=False, when=i == NT - 1)

    row = lambda wd: pl.BlockSpec((TM, wd), lambda i: (i, 0))
    const = lambda r: pl.BlockSpec((r, D), lambda i: (0, 0))
    anyspec = pl.BlockSpec(memory_space=pl.ANY)
    res = pl.pallas_call(
        body, name="in_proj_bwd", grid=(NT,),
        in_specs=[row(wd) for wd in widths] + [anyspec, row(D), const(1), row(D)] + [anyspec] * n,
        out_specs=[const(PREFIX), const(8), anyspec] + [anyspec] * n,
        out_shape=[SDS((PREFIX, D), f32), SDS((8, D), f32), SDS((L - PREFIX, D), f32)] + [SDS(a.shape, a.dtype) for a in slabs],
        scratch_shapes=[pltpu.VMEM((NP, D), bf16), pltpu.SemaphoreType.DMA((1,)), pltpu.VMEM((2, TM, D), f32), pltpu.SemaphoreType.DMA((2,))]
        + _exchange_sems(n),
        compiler_params=_cp())(*dgroups, wpt, h0, w1, dh1, *slabs)
    return res[0], res[1], res[2], res[3:]


def _matmul_tn(a, b, name):
    L, R = a.shape
    C = b.shape[1]
    br = max(k for k in range(HP, MAX_WGRAD_BLOCK + 1, HP) if R % k == 0)

    def body(a_ref, b_ref, o_ref):
        o_ref[...] = _b(_dot_tn(a_ref[...], b_ref[...]))

    return pl.pallas_call(
        body, name=name, grid=(R // br,),
        in_specs=[pl.BlockSpec((L, br), lambda r: (0, r)), pl.BlockSpec((L, C), lambda r: (0, 0))],
        out_specs=pl.BlockSpec((br, C), lambda r: (r, 0)), out_shape=SDS((R, C), bf16), compiler_params=_cp())(a, b)


def _fox_prep(fq, fk, fv, sf, bias_p):
    L = fq.shape[0]
    T = HP
    NT = L // T
    W = FOX_H * HP

    def body(fq_ref, fk_ref, fv_ref, sf_ref, b_ref, qa_ref, ka_ref, va_ref, carry):
        @pl.when(pl.program_id(0) == 0)
        def _():
            carry[...] = jnp.zeros_like(carry)

        lane, row = _iota((T, HP), 1), _iota((T, HP), 0)
        logf = jnp.where(lane < FOX_H, jax.nn.log_sigmoid(sf_ref[...] + b_ref[...]), 0.0)
        c = _dot_hi((row >= lane).astype(f32), logf) + carry[...]
        carry[...] = jnp.sum(jnp.where(row == T - 1, c, 0.0), axis=0, keepdims=True)
        ones_q = jnp.where((lane >= C_K0) & (lane < C_K0 + 3), 1.0, 0.0)
        ones_k = jnp.where(((lane >= C_Q0) & (lane < C_Q0 + 3)) | ((lane >= C_LSE0) & (lane < C_LSE0 + 3)), 1.0, 0.0)
        ones_v = _b(jnp.where((lane >= LSE_COL) & (lane < C_DELTA0 + 3), 1.0, 0.0))
        for h in range(FOX_H):
            ch = jnp.broadcast_to(jnp.sum(jnp.where(lane == h, c, 0.0), axis=1, keepdims=True), (T, HP)) * LOG2E
            c1 = _b(ch).astype(f32)
            c2 = _b(ch - c1).astype(f32)
            c3 = _b(ch - c1 - c2).astype(f32)
            cq = jnp.where(lane == C_Q0, c1, 0.0) + jnp.where(lane == C_Q0 + 1, c2, 0.0) + jnp.where(lane == C_Q0 + 2, c3, 0.0)
            ck = jnp.where(lane == C_K0, c1, 0.0) + jnp.where(lane == C_K0 + 1, c2, 0.0) + jnp.where(lane == C_K0 + 2, c3, 0.0)
            q = fq_ref[:, h * HP:(h + 1) * HP].astype(f32) * (FOX_D ** -0.5 * LOG2E)
            k = fk_ref[:, h * HP:(h + 1) * HP].astype(f32)
            qa_ref[h] = _b(q + cq + ones_q)
            ka_ref[h] = _b(k + ones_k - ck)
            va_ref[:, h * HP:(h + 1) * HP] = fv_ref[:, h * HP:(h + 1) * HP] + ones_v

    wide = pl.BlockSpec((T, W), lambda i: (i, 0))
    return pl.pallas_call(
        body, name="fox_prep", grid=(NT,),
        in_specs=[wide, wide, wide, pl.BlockSpec((T, HP), lambda i: (i, 0)), pl.BlockSpec((1, HP), lambda i: (0, 0))],
        out_specs=[pl.BlockSpec((FOX_H, T, HP), lambda i: (0, i, 0))] * 2 + [wide],
        out_shape=[SDS((FOX_H, L, HP), bf16)] * 2 + [SDS((L, W), bf16)],
        scratch_shapes=[pltpu.VMEM((1, HP), f32)], compiler_params=_cp())(fq, fk, fv, sf, bias_p)


def _fox_prep_bwd(dqa, dka, sf, bias_p):
    L = sf.shape[0]
    T = HP
    NT = L // T
    rev = lambda i: (NT - 1 - i, 0)

    W = FOX_H * HP

    def body(dq_ref, dk_ref, sf_ref, b_ref, dg_ref, db_ref, carry):
        @pl.when(pl.program_id(0) == 0)
        def _():
            carry[...] = jnp.zeros_like(carry)
            db_ref[...] = jnp.zeros_like(db_ref)

        dq, dk = dq_ref[...], dk_ref[...]
        dg_ref[:, 0:W] = _b(dq * (FOX_D ** -0.5))
        dg_ref[:, W:2 * W] = _b(dk * LN2)
        lane, row = _iota((T, HP), 1), _iota((T, HP), 0)
        dc = jnp.zeros((T, HP), f32)
        for h in range(FOX_H):
            col = jnp.sum(jnp.where(lane == C_Q0, dq[:, h * HP:(h + 1) * HP], 0.0)
                          - jnp.where(lane == C_K0, dk[:, h * HP:(h + 1) * HP], 0.0), axis=1, keepdims=True)
            dc = dc + jnp.where(lane == h, col, 0.0)
        dl = _dot_hi((row <= lane).astype(f32), dc) + carry[...]
        carry[...] = jnp.sum(jnp.where(row == 0, dl, 0.0), axis=0, keepdims=True)
        dx = jnp.where(lane < FOX_H, dl * _sigmoid(-(sf_ref[...] + b_ref[...])), 0.0)
        dg_ref[:, 2 * W:2 * W + HP] = _b(dx)
        dg_ref[:, 2 * W + HP:] = jnp.zeros((T, SMALL_W - HP), bf16)
        db_ref[0:1, :] += jnp.sum(dx, axis=0, keepdims=True)

    return pl.pallas_call(
        body, name="fox_prep_bwd", grid=(NT,),
        in_specs=[pl.BlockSpec((T, W), rev), pl.BlockSpec((T, W), rev), pl.BlockSpec((T, HP), rev), pl.BlockSpec((1, HP), lambda i: (0, 0))],
        out_specs=[pl.BlockSpec((T, 2 * W + SMALL_W), rev), pl.BlockSpec((8, HP), lambda i: (0, 0))],
        out_shape=[SDS((L, 2 * W + SMALL_W), bf16), SDS((8, HP), f32)],
        scratch_shapes=[pltpu.VMEM((1, HP), f32)], compiler_params=_cp())(dqa, dka, sf, bias_p)


def _tile_start(j, T):
    return j * T if isinstance(j, int) else pl.multiple_of(j * T, T)


def _spread3(x, lane, col0):
    x1 = _b(x).astype(f32)
    x2 = _b(x - x1).astype(f32)
    x3 = _b(x - x1 - x2).astype(f32)
    return jnp.where(lane == col0, x1, 0.0) + jnp.where(lane == col0 + 1, x2, 0.0) + jnp.where(lane == col0 + 2, x3, 0.0)


def _fox_fwd(qa, ka, fv, shards):
    L = qa.shape[1]
    TQ = TK = _pick(L, ATTN_TILES)
    NQ = L // TQ
    n = len(shards)
    HG = FOX_HEAD_GROUP_FWD

    def body(q_ref, k_ref, v_ref, *rest):
        ins, o_ref, qb_ref, outs, sems = rest[:n], rest[n], rest[n + 1], rest[n + 2:2 * n + 2], rest[2 * n + 2:]
        h, i = pl.program_id(0), pl.program_id(1)

        _exchange_start(ins, outs, *sems, gather=True, when=(h == 0) & (i == 0))

        qs = [q_ref[a] for a in range(HG)]
        rowg = i * TQ + _iota((TQ, TK), 0)
        colb = _iota((TQ, TK), 1)

        def step(j, carry, masked):
            ms, accs = carry
            k0 = _tile_start(j, TK)
            ss = [_dot_nt(qs[a], k_ref[a, pl.ds(k0, TK), :]) for a in range(HG)]
            if masked:
                colg = colb + j * TK
                keep = (colg <= rowg) & (colg >= N_PAD)
                ss = [jnp.where(keep, s, NEG) for s in ss]
            m_new = [jnp.maximum(m, jnp.max(s, axis=1, keepdims=True)) for m, s in zip(ms, ss)]
            ps = [_b(jnp.exp2(s - m)) for s, m in zip(ss, m_new)]
            alphas = [jnp.exp2(m - mn) for m, mn in zip(ms, m_new)]
            accs = [al * acc + _dot(p, v_ref[pl.ds(k0, TK), a * HP:(a + 1) * HP]) for a, (al, acc, p) in enumerate(zip(alphas, accs, ps))]
            return m_new, accs

        init = ([jnp.full((TQ, 1), NEG, f32)] * HG, [jnp.zeros((TQ, HP), f32)] * HG)
        carry = step(0, init, True)
        carry = lax.fori_loop(1, i, functools.partial(step, masked=False), carry)
        ms, accs = lax.fori_loop(jnp.maximum(i, 1), i + 1, functools.partial(step, masked=True), carry)
        lane = _iota((TQ, HP), 1)
        for a in range(HG):
            l = jnp.sum(jnp.where(lane == LSE_COL, accs[a], 0.0), axis=1, keepdims=True)
            lse = ms[a] + jnp.log2(l)
            o_ref[:, a * HP:(a + 1) * HP] = jnp.where(lane == LSE_COL, lse, accs[a] / l)
            qb_ref[a] = _b(qs[a].astype(f32) - _spread3(jnp.broadcast_to(lse, (TQ, HP)), lane, C_LSE0))

        _exchange_wait(ins, outs, *sems, gather=True, when=(h == FOX_H // HG - 1) & (i == NQ - 1))

    anyspec = pl.BlockSpec(memory_space=pl.ANY)
    qtile = pl.BlockSpec((HG, TQ, HP), lambda h, i: (h, i, 0))
    res = pl.pallas_call(
        body, name="fox_fwd", grid=(FOX_H // HG, NQ),
        in_specs=[qtile, pl.BlockSpec((HG, L, HP), lambda h, i: (h, 0, 0)), pl.BlockSpec((L, HG * HP), lambda h, i: (0, h))] + [anyspec] * n,
        out_specs=[pl.BlockSpec((TQ, HG * HP), lambda h, i: (i, h)), qtile] + [anyspec] * n,
        out_shape=[SDS((L, FOX_H * HP), f32), SDS(qa.shape, bf16)] + [SDS((N_DEV,) + a.shape, a.dtype) for a in shards],
        scratch_shapes=_exchange_sems(n), compiler_params=_cp(2))(qa, ka, fv, *shards)
    return res[0], res[1], res[2:]


def _fox_bwd(qb, ka, va, dob, slabs):
    L = qb.shape[1]
    TQ = TK = _pick(L, ATTN_TILES)
    NQ = L // TQ
    n = len(slabs)
    HG = FOX_HEAD_GROUP

    def body(q_ref, k_ref, v_ref, do_ref, *rest):
        ins, (dq_ref, dk_ref, dv_ref), outs, sems = rest[:n], rest[n:n + 3], rest[n + 3:2 * n + 3], rest[2 * n + 3:]
        h, j = pl.program_id(0), pl.program_id(1)
        cols = [slice(a * HP, (a + 1) * HP) for a in range(HG)]

        _exchange_start(ins, outs, *sems, gather=False, when=(h == 0) & (j == 0))

        @pl.when(j == 0)
        def _():
            dq_ref[...] = jnp.zeros_like(dq_ref)

        kts = [k_ref[a] for a in range(HG)]
        vts = [v_ref[:, cols[a]] for a in range(HG)]
        colg = j * TK + _iota((TQ, TK), 1)
        rowb = _iota((TQ, TK), 0)

        def step(i, carry, masked):
            dks, dvs = carry
            r0 = _tile_start(i, TQ)
            rows = pl.ds(r0, TQ)
            qs = [q_ref[a, rows, :] for a in range(HG)]
            ps = [jnp.exp2(_dot_nt(q, kt)) for q, kt in zip(qs, kts)]
            if masked:
                keep = (colg <= rowb + i * TQ) & (colg >= N_PAD)
                ps = [jnp.where(keep, p, 0.0) for p in ps]
            dobs = [do_ref[rows, cols[a]] for a in range(HG)]
            dvs = [dv + _dot_tn(dob, _b(p)) for dv, p, dob in zip(dvs, ps, dobs)]
            dss = [_b(p * _dot_nt(dob, vt)) for p, dob, vt in zip(ps, dobs, vts)]
            for a in range(HG):
                dq_ref[rows, cols[a]] += _dot(dss[a], kts[a])
            dks = [dk + _dot_tn(q, ds) for dk, ds, q in zip(dks, dss, qs)]
            return dks, dvs

        zeros = [jnp.zeros((HP, TK), f32)] * HG
        carry = step(j, (zeros, zeros), True)
        split = jnp.where(j == 0, NQ, j + 1)
        carry = lax.fori_loop(j + 1, split, functools.partial(step, masked=True), carry)
        dks, dvs = lax.fori_loop(split, NQ, functools.partial(step, masked=False), carry)
        for a in range(HG):
            dk_ref[:, cols[a]] = dks[a].T
            dv_ref[:, cols[a]] = _b(dvs[a].T)

        _exchange_wait(ins, outs, *sems, gather=False, when=(h == FOX_H // HG - 1) & (j == NQ - 1))

    head = pl.BlockSpec((L, HG * HP), lambda h, j: (0, h))
    tile = pl.BlockSpec((TK, HG * HP), lambda h, j: (j, h))
    anyspec = pl.BlockSpec(memory_space=pl.ANY)
    res = pl.pallas_call(
        body, name="fox_bwd", grid=(FOX_H // HG, L // TK),
        in_specs=[pl.BlockSpec((HG, L, HP), lambda h, j: (h, 0, 0)), pl.BlockSpec((HG, TK, HP), lambda h, j: (h, j, 0)), tile, head]
        + [anyspec] * n,
        out_specs=[head, tile, tile] + [anyspec] * n,
        out_shape=[SDS((L, FOX_H * HP), f32), SDS((L, FOX_H * HP), f32), SDS((L, FOX_H * HP), bf16)] + [SDS(a.shape, a.dtype) for a in slabs],
        scratch_shapes=_exchange_sems(n), compiler_params=_cp(2))(qb, ka, va, dob, *slabs)
    return res[:3], res[3:]


def _dn_post(y, sd, alog_p, dt_p, valid):
    a = y * _sigmoid(y)
    W = DN_H * DN_D
    heads = []
    for part, scale in ((0, DN_D ** -0.5), (1, 1.0)):
        for h in range(DN_H):
            xh = a[:, part * W + h * DN_D:part * W + (h + 1) * DN_D]
            heads.append(xh * lax.rsqrt(jnp.sum(xh * xh, axis=-1, keepdims=True) + EPS) * scale)
    q = jnp.concatenate(heads[:DN_H], axis=1)
    k = jnp.concatenate(heads[DN_H:], axis=1)
    v = a[:, 2 * W:3 * W]
    lane = _iota(sd.shape, 1)
    beta = _sigmoid(sd) * valid
    g = -jnp.exp(alog_p) * jax.nn.softplus(sd + dt_p) * valid
    bg = jnp.where(lane < DN_H, beta, jnp.where(lane < 2 * DN_H, g, 0.0))
    return q, k, v, bg


def _conv_fwd(ext_ref, cw_ref, TM):
    y = cw_ref[0:1, :] * ext_ref[8 - (CONV_K - 1):8 - (CONV_K - 1) + TM, :]
    for i in range(1, CONV_K):
        o = 8 - (CONV_K - 1) + i
        y = y + cw_ref[i:i + 1, :] * ext_ref[o:o + TM, :]
    return y


def _dn_prep(dn, sd, cw, alog_p, dt_p):
    L, W3 = dn.shape
    TM = _pick(L, ROW_TILES)
    W = DN_H * DN_D

    def body(dn_ref, halo_ref, sd_ref, cw_ref, al_ref, dt_ref, q_ref, k_ref, v_ref, bg_ref, ext):
        i = pl.program_id(0)
        ext[0:8, :] = jnp.where(i == 0, 0.0, halo_ref[...])
        ext[8:, :] = dn_ref[...]
        y = _conv_fwd(ext, cw_ref, TM)
        valid = ((i * TM + _iota((TM, 1), 0)) >= N_PAD).astype(f32)
        q, k, v, bg = _dn_post(y, sd_ref[...], al_ref[...], dt_ref[...], valid)
        q_ref[...], k_ref[...], v_ref[...], bg_ref[...] = q, k, v, bg

    row = lambda wd: pl.BlockSpec((TM, wd), lambda i: (i, 0))
    vec = pl.BlockSpec((1, HP), lambda i: (0, 0))
    return pl.pallas_call(
        body, name="dn_prep", grid=(L // TM,),
        in_specs=[row(W3), pl.BlockSpec((8, W3), lambda i: (jnp.maximum(i * (TM // 8) - 1, 0), 0)), row(HP),
                  pl.BlockSpec((CONV_K, W3), lambda i: (0, 0)), vec, vec],
        out_specs=[row(W), row(W), row(W), row(HP)],
        out_shape=[SDS((L, W), f32)] * 3 + [SDS((L, HP), f32)],
        scratch_shapes=[pltpu.VMEM((TM + 8, W3), f32)], compiler_params=_cp())(dn, dn, sd, cw, alog_p, dt_p)


def _dn_prep_bwd(dn, sd, cw, alog_p, dt_p, dq, dk, dv, dbg):
    L, W3 = dn.shape
    TM = _pick(L, ROW_TILES)
    NT = L // TM
    W = DN_H * DN_D

    def body(dn_ref, halo_ref, sd_ref, cw_ref, al_ref, dt_ref, dq_ref, dk_ref, dv_ref, dbg_ref,
             dg_ref, dcw_ref, dp_ref, ext, dyp, carry):
        i = pl.program_id(0)
        t = NT - 1 - i

        @pl.when(i == 0)
        def _():
            carry[...] = jnp.zeros_like(carry)
            dcw_ref[...] = jnp.zeros_like(dcw_ref)
            dp_ref[...] = jnp.zeros_like(dp_ref)
            dyp[...] = jnp.zeros_like(dyp)

        ext[0:8, :] = jnp.where(t == 0, 0.0, halo_ref[...])
        ext[8:, :] = dn_ref[...]
        y = _conv_fwd(ext, cw_ref, TM)
        valid = ((t * TM + _iota((TM, 1), 0)) >= N_PAD).astype(f32)
        _, vjp = jax.vjp(functools.partial(_dn_post, valid=valid), y, sd_ref[...], al_ref[...], dt_ref[...])
        dy, dsd, dal, ddt = vjp((dq_ref[...], dk_ref[...], dv_ref[...], dbg_ref[...]))
        dg_ref[:, W3:W3 + HP] = _b(dsd)
        dg_ref[:, W3 + HP:] = jnp.zeros((TM, SMALL_W - HP), bf16)
        dp_ref[0:1, :] += dal
        dp_ref[1:2, :] += ddt
        dyp[8:8 + TM, :] = dy
        o0 = CONV_K - 1
        dext = cw_ref[0:1, :] * dyp[o0:o0 + TM + 8, :]
        for k in range(1, CONV_K):
            dext = dext + cw_ref[k:k + 1, :] * dyp[o0 - k:o0 - k + TM + 8, :]
        for k in range(CONV_K):
            o = 8 - (CONV_K - 1) + k
            dcw_ref[k:k + 1, :] += jnp.sum(dy * ext[o:o + TM, :], axis=0, keepdims=True)
        dg_ref[:, 0:W3] = _b(jnp.concatenate([dext[8:TM, :], dext[TM:TM + 8, :] + carry[...]], axis=0))
        carry[...] = dext[0:8, :]

    row = lambda wd: pl.BlockSpec((TM, wd), lambda i: (NT - 1 - i, 0))
    vec = pl.BlockSpec((1, HP), lambda i: (0, 0))
    return pl.pallas_call(
        body, name="dn_prep_bwd", grid=(NT,),
        in_specs=[row(W3), pl.BlockSpec((8, W3), lambda i: (jnp.maximum((NT - 1 - i) * (TM // 8) - 1, 0), 0)), row(HP),
                  pl.BlockSpec((CONV_K, W3), lambda i: (0, 0)), vec, vec, row(W), row(W), row(W), row(HP)],
        out_specs=[row(W3 + SMALL_W), pl.BlockSpec((8, W3), lambda i: (0, 0)), pl.BlockSpec((8, HP), lambda i: (0, 0))],
        out_shape=[SDS((L, W3 + SMALL_W), bf16), SDS((8, W3), f32), SDS((8, HP), f32)],
        scratch_shapes=[pltpu.VMEM((TM + 8, W3), f32), pltpu.VMEM((TM + 16, W3), f32), pltpu.VMEM((8, W3), f32)],
        compiler_params=_cp())(dn, dn, sd, cw, alog_p, dt_p, dq, dk, dv, dbg)


def _split2(x):
    hi = _b(x)
    return hi, _b(x - hi.astype(f32))


def _split3(x):
    hi = _b(x)
    r = x - hi.astype(f32)
    mid = _b(r)
    return hi, mid, _b(r - mid.astype(f32))


def _x3(a, b, dot):
    (a1, a2), (b1, b2) = _split2(a), _split2(b)
    return dot(a1, b1) + (dot(a1, b2) + dot(a2, b1))


@jax.custom_vjp
def _dot_x3(a, b):
    return _x3(a, b, _dot)


_dot_x3.defvjp(lambda a, b: (_x3(a, b, _dot), (a, b)), lambda res, g: (_x3(g, res[1], _dot_nt), _x3(res[0], g, _dot_tn)))


def _exact3(m, x, dot):
    x1, x2, x3 = _split3(x)
    return dot(m, x1) + (dot(m, x2) + dot(m, x3))


def _tri_ones(C, lower):
    row, col = _iota((C, C), 0), _iota((C, C), 1)
    return _b(((row >= col) if lower else (row <= col)).astype(f32))


@jax.custom_vjp
def _chunk_cumsum(x):
    return _exact3(_tri_ones(x.shape[0], True), x, _dot)


_chunk_cumsum.defvjp(lambda x: (_exact3(_tri_ones(x.shape[0], True), x, _dot), None),
                     lambda _, g: (_exact3(_tri_ones(g.shape[0], False), g, _dot),))


def _mxu_transpose(x):
    C = x.shape[0]
    eye = _b((_iota((C, C), 0) == _iota((C, C), 1)).astype(f32))
    return _exact3(eye, x, lambda m, part: _dot_tn(part, m))


@jax.custom_vjp
def _transpose_exact(x):
    return _mxu_transpose(x)


_transpose_exact.defvjp(lambda x: (_mxu_transpose(x), None), lambda _, g: (_mxu_transpose(g),))


def _unit_lower_inverses(lows):
    C = lows[0].shape[0]
    P = jnp.stack(lows)
    X = (_iota((C, C), 0) == _iota((C, C), 1)).astype(f32)[None] - P
    bdot = functools.partial(_x3, dot=lambda a, b: jnp.einsum("bij,bjk->bik", a, b, preferred_element_type=f32))
    for _ in range(5):
        P = bdot(P, P)
        X = X + bdot(X, P)
    return [X[i] for i in range(len(lows))]


@jax.custom_vjp
def _inverse_given(low, X):
    return X


def _inverse_given_bwd(X, g):
    return -_x3(_x3(X, g, _dot_tn), X, _dot_nt), jnp.zeros_like(X)


_inverse_given.defvjp(lambda low, X: (X, X), _inverse_given_bwd)


def _dn_intra_pre(q, k, v, bg):
    C = DN_C
    row, col = _iota((C, C), 0), _iota((C, C), 1)
    tri = row >= col
    G = _chunk_cumsum(bg)
    GT = _transpose_exact(G)
    lane = _iota((C, HP), 1)
    rowt = _iota((HP, C), 0)
    last = _iota((C, 1), 0) == C - 1
    heads = []
    for h in range(DN_H):
        beta = jnp.sum(jnp.where(lane == h, bg, 0.0), axis=1, keepdims=True)
        gcol = jnp.sum(jnp.where(lane == DN_H + h, G, 0.0), axis=1, keepdims=True)
        grow = jnp.sum(jnp.where(rowt == DN_H + h, GT, 0.0), axis=0, keepdims=True)
        glast = jnp.sum(jnp.where(last, gcol, 0.0), axis=0, keepdims=True)
        decay = jnp.exp(jnp.where(tri, gcol - grow, NEG))
        qh, kh, vh = (t[:, h * DN_D:(h + 1) * DN_D] for t in (q, k, v))
        kb = kh * beta
        low = jnp.where(row > col, _dot_nt(_b(kb), _b(kh)) * decay, 0.0)
        heads.append((beta, gcol, glast, decay, qh, kh, vh, kb, low))
    return heads


def _dn_intra_post(heads, xs):
    lane1 = _iota((1, HP), 1)
    us, ws, qds, kds, attns = [], [], [], [], []
    glrow = jnp.zeros((1, HP), f32)
    for h, ((beta, gcol, glast, decay, qh, kh, vh, kb, _), X) in enumerate(zip(heads, xs)):
        eg = jnp.exp(gcol)
        us.append(_dot_x3(X, vh * beta))
        ws.append(_dot_x3(X, kb * eg))
        attns.append(_dot_nt(_b(qh), _b(kh)) * decay)
        qds.append(qh * eg)
        kds.append(kh * jnp.exp(glast - gcol))
        glrow = glrow + jnp.where(lane1 == h, glast, 0.0)
    cat = lambda xs_: jnp.concatenate(xs_, axis=1)
    return cat(us), cat(ws), cat(qds), cat(kds), cat(attns), glrow, cat(list(xs))


def _dn_intra_group(q, k, v, bg, xs):
    G = q.shape[0] // DN_C
    rows = [slice(j * DN_C, (j + 1) * DN_C) for j in range(G)]
    pre = [_dn_intra_pre(q[r, :], k[r, :], v[r, :], bg[r, :]) for r in rows]
    inv = [[_inverse_given(hd[-1], x) for hd, x in zip(heads, xj)] for heads, xj in zip(pre, xs)]
    post = [_dn_intra_post(heads, xj) for heads, xj in zip(pre, inv)]
    return tuple(jnp.concatenate([p[i] for p in post], axis=0) for i in range(5)) + (tuple(p[5] for p in post),)


def _lane_pick(rowvec, h):
    return jnp.sum(jnp.where(_iota(rowvec.shape, 1) == h, rowvec, 0.0), axis=1, keepdims=True)


def _dn_intra(q, k, v, bg, shards):
    L, W = q.shape
    NC = L // DN_C
    G = _pick(NC, DN_INTRA_GROUP)
    R = G * DN_C
    NS = NC // G
    WA = DN_H * DN_C
    n = len(shards)

    def body(q_ref, k_ref, v_ref, bg_ref, *rest):
        ins, (u_ref, w_ref, qd_ref, kd_ref, at_ref, gl_ref, x_ref), gouts, sems = rest[:n], rest[n:n + 7], rest[n + 7:2 * n + 7], rest[2 * n + 7:]
        _exchange_start(ins, gouts, *sems, gather=True, when=pl.program_id(0) == 0)
        rows = [slice(j * DN_C, (j + 1) * DN_C) for j in range(G)]
        pre = [_dn_intra_pre(q_ref[r, :], k_ref[r, :], v_ref[r, :], bg_ref[r, :]) for r in rows]
        inv = _unit_lower_inverses([hd[-1] for heads in pre for hd in heads])
        for j, r in enumerate(rows):
            u, w, qd, kd, at, gl, xs = _dn_intra_post(pre[j], inv[j * DN_H:(j + 1) * DN_H])
            u_ref[r, :], x_ref[r, :] = u, xs
            w_ref[r, :], qd_ref[r, :], kd_ref[r, :], at_ref[r, :] = _b(w), _b(qd), _b(kd), _b(at)
            gl_ref[j] = gl
        _exchange_wait(ins, gouts, *sems, gather=True, when=pl.program_id(0) == NS - 1)

    row = lambda wd: pl.BlockSpec((R, wd), lambda s: (s, 0))
    anyspec = pl.BlockSpec(memory_space=pl.ANY)
    res = pl.pallas_call(
        body, name="dn_intra", grid=(NS,),
        in_specs=[row(W), row(W), row(W), row(HP)] + [anyspec] * n,
        out_specs=[row(W), row(W), row(W), row(W), row(WA), pl.BlockSpec((G, 1, HP), lambda s: (s, 0, 0)), row(WA)] + [anyspec] * n,
        out_shape=[SDS((L, W), f32), SDS((L, W), bf16), SDS((L, W), bf16), SDS((L, W), bf16), SDS((L, WA), bf16), SDS((NC, 1, HP), f32),
                   SDS((L, WA), f32)] + [SDS((N_DEV,) + a.shape, a.dtype) for a in shards],
        scratch_shapes=_exchange_sems(n), compiler_params=_cp())(q, k, v, bg, *shards)
    return res[:7], res[7:]


def _dn_scan(u, w, qd, kd, at, gl):
    L, W = u.shape
    NC = L // DN_C
    G = _pick(NC, DN_SCAN_GROUP)
    R = G * DN_C

    def body(u_ref, w_ref, qd_ref, kd_ref, at_ref, gl_ref, o_ref, vn_ref, s_ref, S):
        @pl.when(pl.program_id(0) == 0)
        def _():
            S[...] = jnp.zeros_like(S)

        for j in range(G):
            r = slice(j * DN_C, (j + 1) * DN_C)
            glrow = gl_ref[j]
            for h in range(DN_H):
                c = slice(h * DN_D, (h + 1) * DN_D)
                Sh = S[h]
                s_ref[j, h] = Sh
                Sb = _b(Sh)
                vb = _b(u_ref[r, c] - _dot(w_ref[r, c], Sb))
                vn_ref[r, c] = vb
                o_ref[r, c] = _dot(qd_ref[r, c], Sb) + _dot(at_ref[r, h * DN_C:(h + 1) * DN_C], vb)
                S[h] = Sh * jnp.exp(_lane_pick(glrow, h)) + _dot_tn(kd_ref[r, c], vb)

    row = lambda wd: pl.BlockSpec((R, wd), lambda n: (n, 0))
    return pl.pallas_call(
        body, name="dn_scan", grid=(NC // G,),
        in_specs=[row(W), row(W), row(W), row(W), row(DN_H * DN_C), pl.BlockSpec((G, 1, HP), lambda n: (n, 0, 0))],
        out_specs=[row(W), row(W), pl.BlockSpec((G, DN_H, DN_D, DN_D), lambda n: (n, 0, 0, 0))],
        out_shape=[SDS((L, W), f32), SDS((L, W), bf16), SDS((NC, DN_H, DN_D, DN_D), f32)],
        scratch_shapes=[pltpu.VMEM((DN_H, DN_D, DN_D), f32)], compiler_params=_cp())(u, w, qd, kd, at, gl)


def _dn_scan_bwd(do, w, qd, kd, at, gl):
    L, W = do.shape
    NC = L // DN_C
    G = _pick(NC, DN_SCAN_GROUP)
    R = G * DN_C
    NS = NC // G

    def body(do_ref, w_ref, qd_ref, kd_ref, at_ref, gl_ref, dvn_ref, ds_ref, dS):
        @pl.when(pl.program_id(0) == 0)
        def _():
            dS[...] = jnp.zeros_like(dS)

        for j in reversed(range(G)):
            r = slice(j * DN_C, (j + 1) * DN_C)
            glrow = gl_ref[j]
            for h in range(DN_H):
                c = slice(h * DN_D, (h + 1) * DN_D)
                dSo = dS[h]
                ds_ref[j, h] = dSo
                dob = _b(do_ref[r, c])
                dvn = _dot_tn(at_ref[r, h * DN_C:(h + 1) * DN_C], dob) + _dot(kd_ref[r, c], _b(dSo))
                dvn_ref[r, c] = dvn
                dS[h] = _dot_tn(qd_ref[r, c], dob) + dSo * jnp.exp(_lane_pick(glrow, h)) - _dot_tn(w_ref[r, c], _b(dvn))

    row = lambda wd: pl.BlockSpec((R, wd), lambda n: (NS - 1 - n, 0))
    return pl.pallas_call(
        body, name="dn_scan_bwd", grid=(NS,),
        in_specs=[row(W), row(W), row(W), row(W), row(DN_H * DN_C), pl.BlockSpec((G, 1, HP), lambda n: (NS - 1 - n, 0, 0))],
        out_specs=[row(W), pl.BlockSpec((G, DN_H, DN_D, DN_D), lambda n: (NS - 1 - n, 0, 0, 0))],
        out_shape=[SDS((L, W), f32), SDS((NC, DN_H, DN_D, DN_D), f32)],
        scratch_shapes=[pltpu.VMEM((DN_H, DN_D, DN_D), f32)], compiler_params=_cp())(do, w, qd, kd, at, gl)


def _dn_intra_bwd(q, k, v, bg, xinv, do, vn, dvn, states, dstates):
    L, W = q.shape
    NC = L // DN_C
    G = _pick(NC, DN_INTRA_GROUP)
    R = G * DN_C

    def body(q_ref, k_ref, v_ref, bg_ref, x_ref, do_ref, vn_ref, dvn_ref, s_ref, ds_ref, dq_ref, dk_ref, dv_ref, dbg_ref):
        lane1 = _iota((1, HP), 1)
        rows = [slice(j * DN_C, (j + 1) * DN_C) for j in range(G)]
        xs = [[x_ref[r, h * DN_C:(h + 1) * DN_C] for h in range(DN_H)] for r in rows]
        fwd, vjp = jax.vjp(functools.partial(_dn_intra_group, xs=xs), q_ref[...], k_ref[...], v_ref[...], bg_ref[...])
        dws, dqds, dkds, dats, dgls = [], [], [], [], []
        for j, r in enumerate(rows):
            dw, dqd, dkd, dat = [], [], [], []
            dgl = jnp.zeros((1, HP), f32)
            for h in range(DN_H):
                c = slice(h * DN_D, (h + 1) * DN_D)
                Sh, dSo = s_ref[j, h], ds_ref[j, h]
                Sb, dob, vb = _b(Sh), _b(do_ref[r, c]), vn_ref[r, c]
                dw.append(-_dot_nt(_b(dvn_ref[r, c]), Sb))
                dqd.append(_dot_nt(dob, Sb))
                dat.append(_dot_nt(dob, vb))
                dkd.append(_dot_nt(vb, _b(dSo)))
                dcd = jnp.sum(jnp.sum(Sh * dSo, axis=1, keepdims=True), axis=0, keepdims=True)
                dgl = dgl + jnp.where(lane1 == h, dcd * jnp.exp(_lane_pick(fwd[5][j], h)), 0.0)
            cat = lambda xs_: jnp.concatenate(xs_, axis=1)
            dws.append(cat(dw)), dqds.append(cat(dqd)), dkds.append(cat(dkd)), dats.append(cat(dat)), dgls.append(dgl)
        cat0 = lambda xs_: jnp.concatenate(xs_, axis=0)
        dq, dk, dv, dbg = vjp((dvn_ref[...], cat0(dws), cat0(dqds), cat0(dkds), cat0(dats), tuple(dgls)))
        dq_ref[...], dk_ref[...], dv_ref[...], dbg_ref[...] = dq, dk, dv, dbg

    row = lambda wd: pl.BlockSpec((R, wd), lambda s: (s, 0))
    st = pl.BlockSpec((G, DN_H, DN_D, DN_D), lambda s: (s, 0, 0, 0))
    return pl.pallas_call(
        body, name="dn_intra_bwd", grid=(NC // G,),
        in_specs=[row(W), row(W), row(W), row(HP), row(DN_H * DN_C), row(W), row(W), row(W), st, st],
        out_specs=[row(W), row(W), row(W), row(HP)],
        out_shape=[SDS((L, W), f32)] * 3 + [SDS((L, HP), f32)],
        compiler_params=_cp())(q, k, v, bg, xinv, do, vn, dvn, states, dstates)


def _dn_normgate(oraw, dz, wn):
    outs = []
    for h in range(DN_H):
        sl = slice(h * DN_D, (h + 1) * DN_D)
        z = dz[:, sl]
        outs.append(_rms(oraw[:, sl], wn) * (z * _sigmoid(z)))
    return jnp.concatenate(outs, axis=1)


def _mix_fwd(op, oraw, dz, ga, gb, h0, wn, wbf, wbd, wo):
    L, D = h0.shape
    TM = _pick(L, ROW_TILES)

    def body(op_ref, or_ref, dz_ref, ga_ref, gb_ref, h0_ref, wn_ref, wbf_ref, wbd_ref, wo_ref, h1_ref):
        pf = _dot(_b(op_ref[...]), wbf_ref[...])
        pd = _dot(_b(_dn_normgate(or_ref[...], dz_ref[...], wn_ref[...])), wbd_ref[...])
        y = _sigmoid(ga_ref[...]) * pf + _sigmoid(gb_ref[...]) * pd
        h1_ref[...] = h0_ref[...] + _dot(_b(y), wo_ref[...])

    row = lambda wd: pl.BlockSpec((TM, wd), lambda i: (i, 0))
    full = lambda a: pl.BlockSpec(a.shape, lambda i: (0, 0))
    return pl.pallas_call(
        body, name="mix_fwd", grid=(L // TM,),
        in_specs=[row(op.shape[1]), row(oraw.shape[1]), row(dz.shape[1]), row(D), row(D), row(D), full(wn), full(wbf), full(wbd), full(wo)],
        out_specs=row(D), out_shape=SDS((L, D), f32), compiler_params=_cp())(op, oraw, dz, ga, gb, h0, wn, wbf, wbd, wo)


def _mix_bwd(dh1, op, oraw, dz, ga, gb, wn, wbf, wbd, wo):
    L, D = dh1.shape
    TM = _pick(L, ROW_TILES)
    WF, WD = op.shape[1], oraw.shape[1]

    def body(dh1_ref, op_ref, or_ref, dz_ref, ga_ref, gb_ref, wn_ref, wbf_ref, wbd_ref, wo_ref,
             dop_ref, dor_ref, dg_ref, af_ref, ad_ref, dpf_ref, dpd_ref, y_ref, dmix_ref, acc_ref):
        @pl.when(pl.program_id(0) == 0)
        def _():
            acc_ref[...] = jnp.zeros_like(acc_ref)

        opv = op_ref[...]
        af = _b(opv)
        ad, vjp = jax.vjp(_dn_normgate, or_ref[...], dz_ref[...], wn_ref[...])
        adb = _b(ad)
        pf, pd = _dot(af, wbf_ref[...]), _dot(adb, wbd_ref[...])
        sa, sb = _sigmoid(ga_ref[...]), _sigmoid(gb_ref[...])
        dmix = _b(dh1_ref[...])
        dy = _dot_nt(dmix, wo_ref[...])
        dpf, dpd = _b(dy * sa), _b(dy * sb)
        dor, ddz, dwn = vjp(_dot_nt(dpd, wbd_ref[...]))
        dop = _dot_nt(dpf, wbf_ref[...])
        lane = _iota((TM, HP), 1)
        for h in range(WF // HP):
            c = slice(h * HP, (h + 1) * HP)
            delta = jnp.sum(jnp.where(lane < FOX_D, dop[:, c] * opv[:, c], 0.0), axis=1, keepdims=True)
            dop_ref[:, c] = _b(dop[:, c] - _spread3(jnp.broadcast_to(delta, (TM, HP)), lane, C_DELTA0))
        dor_ref[...] = dor
        dg_ref[:, 0:WD] = _b(ddz)
        dg_ref[:, WD:WD + D] = _b(dy * pf * sa * (1.0 - sa))
        dg_ref[:, WD + D:] = _b(dy * pd * sb * (1.0 - sb))
        af_ref[...], ad_ref[...], y_ref[...] = af, adb, _b(sa * pf + sb * pd)
        dpf_ref[...], dpd_ref[...], dmix_ref[...] = dpf, dpd, dmix
        acc_ref[0:1, :] += dwn

    row = lambda wd: pl.BlockSpec((TM, wd), lambda i: (i, 0))
    full = lambda a: pl.BlockSpec(a.shape, lambda i: (0, 0))
    return pl.pallas_call(
        body, name="mix_bwd", grid=(L // TM,),
        in_specs=[row(D), row(WF), row(WD), row(WD), row(D), row(D), full(wn), full(wbf), full(wbd), full(wo)],
        out_specs=[row(WF), row(WD), row(WD + 2 * D), row(WF), row(WD), row(D), row(D), row(D), row(D),
                   pl.BlockSpec((8, HP), lambda i: (0, 0))],
        out_shape=[SDS((L, WF), bf16), SDS((L, WD), f32), SDS((L, WD + 2 * D), bf16), SDS((L, WF), bf16), SDS((L, WD), bf16),
                   SDS((L, D), bf16), SDS((L, D), bf16), SDS((L, D), bf16), SDS((L, D), bf16), SDS((8, HP), f32)],
        compiler_params=_cp())(dh1, op, oraw, dz, ga, gb, wn, wbf, wbd, wo)


def _ffn_fwd_bwd(h1, tgt, w2, wf, wgt, wut, wd):
    L, D = h1.shape
    F = wd.shape[0]
    TM = _pick(L, FFN_TILES)
    gt, nb, _, tmaps = _shifted_blocks(TM)

    def body(h_ref, *refs):
        t_refs, (w2_ref, wf_ref, wg_hbm, wu_hbm, wd_hbm,
                 dh1_ref, xn_ref, dg_ref, du_ref, act_ref, dh2_ref, acc_ref, wg_v, wu_v, wd_v, sems) = refs[:nb], refs[nb:]
        i = pl.program_id(0)
        _load_once([(wg_hbm, wg_v), (wu_hbm, wu_v), (wd_hbm, wd_v)], sems)

        @pl.when(i == 0)
        def _():
            acc_ref[...] = jnp.zeros_like(acc_ref)

        h1v = h_ref[...]
        xn2, vjp2 = jax.vjp(_rms, h1v, w2_ref[...])
        xb = _b(xn2)
        g, u = _dot_nt(xb, wg_v[...]), _dot_nt(xb, wu_v[...])
        sg = _sigmoid(g)
        ab = _b(g * sg * u)
        h2 = h1v + _dot(ab, wd_v[...])
        out, vjpf = jax.vjp(_rms, h2, wf_ref[...])
        valid = (i * TM + _iota((TM, 1), 0)) >= PREFIX
        diff = jnp.where(valid, out - jnp.concatenate([r[...] for r in t_refs], axis=0), 0.0)
        loss = 0.5 * jnp.sum(jnp.sum(diff * diff, axis=1, keepdims=True), axis=0, keepdims=True) / D
        dh2, dwf = vjpf(diff * (1.0 / D))
        dh2b = _b(dh2)
        dact = _dot_nt(dh2b, wd_v[...])
        dgb = _b(dact * u * (sg * (1.0 + g * (1.0 - sg))))
        dub = _b(dact * (g * sg))
        dh1n, dw2 = vjp2(_dot(dgb, wg_v[...]) + _dot(dub, wu_v[...]))
        dh1_ref[...] = dh2 + dh1n
        xn_ref[...], dg_ref[...], du_ref[...], act_ref[...], dh2_ref[...] = xb, dgb, dub, ab, dh2b
        acc_ref[0:1, :] += dw2
        acc_ref[1:2, :] += dwf
        acc_ref[2:3, :] += jnp.broadcast_to(loss, (1, D))

    row = lambda wd_: pl.BlockSpec((TM, wd_), lambda i: (i, 0))
    vec = pl.BlockSpec((1, D), lambda i: (0, 0))
    anyspec = pl.BlockSpec(memory_space=pl.ANY)
    return pl.pallas_call(
        body, name="ffn_fwd_bwd", grid=(L // TM,),
        in_specs=[row(D)] + [pl.BlockSpec((gt, D), m) for m in tmaps] + [vec, vec, anyspec, anyspec, anyspec],
        out_specs=[row(D), row(D), row(F), row(F), row(F), row(D), pl.BlockSpec((8, D), lambda i: (0, 0))],
        out_shape=[SDS((L, D), f32), SDS((L, D), bf16), SDS((L, F), bf16), SDS((L, F), bf16), SDS((L, F), bf16), SDS((L, D), bf16),
                   SDS((8, D), f32)],
        scratch_shapes=[pltpu.VMEM((F, D), bf16), pltpu.VMEM((F, D), bf16), pltpu.VMEM((F, D), bf16), pltpu.SemaphoreType.DMA((3,))],
        compiler_params=_cp())(h1, *([tgt] * nb), w2, wf, wgt, wut, wd)


def _pad_lanes(v, n=HP):
    return jnp.pad(v.astype(f32), ((0, 0), (0, n - v.shape[1])))


def _pack_w_in(wt_full):
    D = wt_full.shape[1]
    FW, DW = FOX_H * FOX_D, DN_H * DN_D
    o = 0
    parts = {}
    for name, wd in (("fq", FW), ("fk", FW), ("fv", FW), ("fl", FOX_H), ("dn", 3 * DW), ("ba", 2 * DN_H), ("dz", DW), ("ga", D), ("gb", D)):
        parts[name] = wt_full[o:o + wd]
        o += wd
    assert o == wt_full.shape[0]
    heads = lambda w: jnp.pad(w.reshape(FOX_H, FOX_D, D), ((0, 0), (0, HP - FOX_D), (0, 0))).reshape(FOX_H * HP, D)
    small = lambda w: jnp.pad(w, ((0, SMALL_W - w.shape[0]), (0, 0)))
    packed = dict(fq=heads(parts["fq"]), fk=heads(parts["fk"]), fv=heads(parts["fv"]), sf=small(parts["fl"]), sd=small(parts["ba"]),
                  dn=parts["dn"], dz=parts["dz"], ga=parts["ga"], gb=parts["gb"])
    return jnp.concatenate([packed[name] for name, _, _, _ in _seg_layout(D)], axis=0)


def _unpack_w_in(groups, d_model):
    D = groups[0].shape[1]
    FW = FOX_H * FOX_D
    segs = {}
    for grp, g in zip(GROUPS, groups):
        o = 0
        for name, wd, _, sg in _seg_layout(d_model):
            if sg == grp:
                segs[name] = g[o:o + wd]
                o += wd
    heads = lambda g: g.reshape(FOX_H, HP, D)[:, :FOX_D].reshape(FW, D)
    return jnp.concatenate([heads(segs["fq"]), heads(segs["fk"]), heads(segs["fv"]), segs["sf"][:FOX_H], segs["dn"],
                            segs["sd"][:2 * DN_H], segs["dz"], segs["ga"], segs["gb"]], axis=0)


def _local_step(x, tgt, meta, w1, w_in_t, fbias, cw, alog, dtb, wn, w2, wf, late_shards):
    T, D = x.shape
    pre = jnp.concatenate([jnp.zeros((N_PAD, D), f32), meta], axis=0)
    wp = _pack_w_in(w_in_t)
    bias_p, alog_p, dt_p = _pad_lanes(fbias), _pad_lanes(jnp.pad(alog, ((0, 0), (DN_H, 0)))), _pad_lanes(jnp.pad(dtb, ((0, 0), (DN_H, 0))))

    (h0, xn, fq, fk, sf, fv, dn, sd, dz, ga, gb), g_mix = _in_proj(x, pre, w1, wp, [late_shards[n] for n in LATE_MIX])
    qa, ka, va = _fox_prep(fq, fk, fv, sf, bias_p)
    op, qb, g_ffn = _fox_fwd(qa, ka, va, [late_shards[n] for n in LATE_FFN])
    qn, kn, vn, bg = _dn_prep(dn, sd, cw, alog_p, dt_p)
    (u_dn, w_dn, qd_dn, kd_dn, at_dn, gl_dn, x_dn), g_down = _dn_intra(qn, kn, vn, bg, [late_shards[n] for n in LATE_DOWN])
    full = {n: _from_slabs(n, s) for n, s in zip(LATE_MIX + LATE_FFN + LATE_DOWN, tuple(g_mix) + tuple(g_ffn) + tuple(g_down))}
    wbf, wbd, wo, wgt, wut, wd = (full[n] for n in ("w_branch_fox", "w_branch_dn", "w_out", "w_ffn_gate", "w_ffn_up", "w_ffn_down"))
    wbf_p = jnp.pad(wbf.reshape(FOX_H, FOX_D, D), ((0, 0), (0, HP - FOX_D), (0, 0))).reshape(FOX_H * HP, D)
    oraw, vnew, states = _dn_scan(u_dn, w_dn, qd_dn, kd_dn, at_dn, gl_dn)
    h1 = _mix_fwd(op, oraw, dz, ga, gb, h0, wn, wbf_p, wbd, wo)

    dh1, xn2, dgate, dup, act, dh2, acc_f = _ffn_fwd_bwd(h1, tgt, w2, wf, wgt, wut, wd)
    g_wg, g_wu, g_wd = _matmul_tn(dgate, xn2, "dw_ffn_gate"), _matmul_tn(dup, xn2, "dw_ffn_up"), _matmul_tn(act, dh2, "dw_ffn_down")

    dop, dor, d_mix, af, ad, dpf, dpd, yb, dmix, acc_m = _mix_bwd(dh1, op, oraw, dz, ga, gb, wn, wbf_p, wbd, wo)
    g_wbf = _matmul_tn(af, dpf, "dw_branch_fox").reshape(FOX_H, HP, D)[:, :FOX_D].reshape(FOX_H * FOX_D, D)
    g_wbd, g_wo = _matmul_tn(ad, dpd, "dw_branch_dn"), _matmul_tn(yb, dmix, "dw_out")

    dvnew, dstates = _dn_scan_bwd(dor, w_dn, qd_dn, kd_dn, at_dn, gl_dn)
    dqn, dkn, dvn, dbg = _dn_intra_bwd(qn, kn, vn, bg, x_dn, dor, vnew, dvnew, states, dstates)
    d_dn, acc_cw, acc_p = _dn_prep_bwd(dn, sd, cw, alog_p, dt_p, dqn, dkn, dvn, dbg)
    g_late = dict(w_branch_fox=g_wbf, w_branch_dn=g_wbd, w_out=g_wo, w_ffn_gate=g_wg, w_ffn_up=g_wu, w_ffn_down=g_wd)
    (dqa, dka, d_fv), recv = _fox_bwd(qb, ka, va, dop, [_to_slabs(n, g_late[n]) for n in LATE])
    d_fox, acc_b = _fox_prep_bwd(dqa, dka, sf, bias_p)

    dgroups = [d_fox, d_fv, d_dn, d_mix]
    g_wp = [_matmul_tn(dg, xn, "dw_in_" + grp) for grp, dg in zip(GROUPS, dgroups)]
    dpre, acc_1, gx, (recv_w_in,) = _in_proj_bwd(dgroups, wp, h0, w1, dh1, [_to_slabs("w_in", _unpack_w_in(g_wp, D))])
    recv = dict(zip(LATE, recv), w_in=recv_w_in)

    small = dict(loss=acc_f[2, 0:1], mix_norm_w=acc_1[0], fox_forget_bias=acc_b[0, :FOX_H], dn_a_log=acc_p[0, DN_H:2 * DN_H],
                 dn_dt_bias=acc_p[1, DN_H:2 * DN_H], dn_out_norm_w=acc_m[0], ffn_norm_w=acc_f[0], final_norm_w=acc_f[1],
                 meta_tokens=dpre[N_PAD:PREFIX].reshape(-1), dn_conv_w=acc_cw[:CONV_K].reshape(-1))
    return gx, small, recv


def _mesh_pos():
    x, y, c = lax.axis_index("x"), lax.axis_index("y"), lax.axis_index("c")
    return x, y, c, 4 * x + 2 * y + c


def _peer(x, y, c, m):
    flip = lambda v, on: 1 - v if on else v
    px, py, pc = flip(x, m & 4), flip(y, m & 2), flip(c, m & 1)
    return (px, py, pc), 4 * px + 2 * py + pc


def _exchange_sems(n):
    return [pltpu.SemaphoreType.DMA((n, N_DEV - 1)), pltpu.SemaphoreType.DMA((n, N_DEV - 1)), pltpu.SemaphoreType.DMA((n,))]


def _exchange_part(ins, outs, send_sems, recv_sems, loc_sems, gather, m, receive):
    x, y, c, me = _mesh_pos()
    src = lambda a, pid: ins[a] if gather else ins[a].at[pid]
    if m == 0:
        return [pltpu.make_async_copy(src(a, me), outs[a].at[me], loc_sems.at[a]) for a in range(len(ins))]
    peer, pid = _peer(x, y, c, m)
    return [pltpu.make_async_remote_copy(src_ref=src(a, pid), dst_ref=outs[a].at[pid if receive else me], send_sem=send_sems.at[a, m - 1],
                                         recv_sem=recv_sems.at[a, m - 1], device_id=peer, device_id_type=MESH) for a in range(len(ins))]


def _exchange_start(*refs, gather, when):
    @pl.when(when)
    def _():
        for m in range(N_DEV):
            for cp in _exchange_part(*refs, gather, m, receive=False):
                cp.start()


def _exchange_wait(*refs, gather, when):
    @pl.when(when)
    def _():
        for m in range(1, N_DEV):
            for cp in _exchange_part(*refs, gather, m, receive=True):
                cp.wait_recv()
        for m in list(range(1, N_DEV)) + [0]:
            for cp in _exchange_part(*refs, gather, m, receive=False):
                cp.wait() if m == 0 else cp.wait_send()


def _gather_two_level(arrays, name):
    n = len(arrays)

    def body(*refs):
        ins, outs, (send_sems, recv_sems, loc_sems) = refs[:n], refs[n:2 * n], refs[2 * n:]
        x, y, c, me = _mesh_pos()
        sib = (x, y, 1 - c)
        chips = [(1 - x, y), (x, 1 - y), (1 - x, 1 - y)]
        dev_id = lambda px, py, pc: 4 * px + 2 * py + pc

        def copy(a, k, block, to, own=False):
            return pltpu.make_async_remote_copy(src_ref=ins[a] if own else outs[a].at[block], dst_ref=outs[a].at[block],
                                                send_sem=send_sems.at[a, k], recv_sem=recv_sems.at[a, k], device_id=to, device_id_type=MESH)

        local = [pltpu.make_async_copy(ins[a], outs[a].at[me], loc_sems.at[a]) for a in range(n)]
        first = [copy(a, 0, me, sib, own=True) for a in range(n)]
        first += [copy(a, 1 + j, me, (*chip, c), own=True) for j, chip in enumerate(chips) for a in range(n)]
        for cp in local + first:
            cp.start()
        passed = []
        for j, chip in enumerate(chips):
            for a in range(n):
                copy(a, 1 + j, dev_id(*chip, c), sib).wait_recv()
                cp = copy(a, 4 + j, dev_id(*chip, c), sib)
                cp.start()
                passed.append(cp)
        for a in range(n):
            copy(a, 0, dev_id(x, y, 1 - c), sib).wait_recv()
        for j, chip in enumerate(chips):
            for a in range(n):
                copy(a, 4 + j, dev_id(*chip, 1 - c), sib).wait_recv()
        for cp in first + passed:
            cp.wait_send()
        for cp in local:
            cp.wait()

    anyspec = pl.BlockSpec(memory_space=pl.ANY)
    return pl.pallas_call(
        body, name=name, in_specs=[anyspec] * n, out_specs=[anyspec] * n,
        out_shape=[SDS((N_DEV,) + a.shape, a.dtype) for a in arrays],
        scratch_shapes=_exchange_sems(n))(*arrays)


def _all_reduce_small(v):
    R = v.shape[0]

    def body(v_ref, o_ref, gath, send_sems, recv_sems):
        x, y, c, me = _mesh_pos()
        gath[me] = v_ref[...]
        sends = []
        for m in range(1, N_DEV):
            peer, _ = _peer(x, y, c, m)
            cp = pltpu.make_async_remote_copy(src_ref=v_ref, dst_ref=gath.at[me], send_sem=send_sems.at[m - 1],
                                              recv_sem=recv_sems.at[m - 1], device_id=peer, device_id_type=MESH)
            cp.start()
            sends.append(cp)
        for m in range(1, N_DEV):
            peer, pid = _peer(x, y, c, m)
            pltpu.make_async_remote_copy(src_ref=v_ref, dst_ref=gath.at[pid], send_sem=send_sems.at[m - 1],
                                         recv_sem=recv_sems.at[m - 1], device_id=peer, device_id_type=MESH).wait_recv()
        for cp in sends:
            cp.wait_send()
        tot = gath[0]
        for d in range(1, N_DEV):
            tot = tot + gath[d]
        o_ref[...] = tot

    vm = pl.BlockSpec(memory_space=pltpu.VMEM)
    return pl.pallas_call(
        body, name="all_reduce_small", in_specs=[vm], out_specs=vm, out_shape=SDS((R, HP), f32),
        scratch_shapes=[pltpu.VMEM((N_DEV, R, HP), f32), pltpu.SemaphoreType.DMA((N_DEV - 1,)), pltpu.SemaphoreType.DMA((N_DEV - 1,))],
        )(v)


def _adamw_math(w, g, m, v):
    m = ADAM_B1 * m + (1.0 - ADAM_B1) * g
    v = ADAM_B2 * v + (1.0 - ADAM_B2) * (g * g)
    m_hat = m / (1.0 - ADAM_B1 ** ADAM_STEP)
    v_hat = v / (1.0 - ADAM_B2 ** ADAM_STEP)
    return -ADAM_LR * (m_hat / (jnp.sqrt(v_hat) + ADAM_EPS) + ADAM_WD * w), m, v


def _adamw(g, w, m, v, name):
    R, Cc = w.shape[-2:]
    if R <= ADAMW_WHOLE_ROWS or R % HP == 0:
        TR, TC = (R if R <= ADAMW_WHOLE_ROWS else _pick(R, ADAMW_TILES)), Cc
    else:
        TR, TC = R, _pick(Cc, ADAMW_TILES)
    slabs = g.ndim == 3
    lead = w.ndim - 2

    def body(g_ref, w_ref, m_ref, v_ref, go_ref, d_ref, mo_ref, vo_ref):
        if slabs:
            gs = g_ref[0].astype(f32)
            for k in range(1, N_DEV):
                gs = gs + g_ref[k].astype(f32)
        else:
            gs = g_ref[...]
        at = 0 if lead else Ellipsis
        d, mn, vn = _adamw_math(w_ref[at], gs, m_ref[at], v_ref[at])
        go_ref[at], d_ref[at], mo_ref[at], vo_ref[at] = gs, d, mn, vn

    grid = (R // TR, Cc // TC)
    blk = pl.BlockSpec((1,) * lead + (TR, TC), lambda i, j: (0,) * lead + (i, j))
    gblk = pl.BlockSpec((N_DEV, TR, TC), lambda i, j: (0, i, j)) if slabs else pl.BlockSpec((TR, TC), lambda i, j: (i, j))
    return pl.pallas_call(
        body, name=name, grid=grid, in_specs=[gblk, blk, blk, blk], out_specs=[blk] * 4,
        out_shape=[SDS(w.shape, f32)] * 4, compiler_params=_cp(2))(g, w, m, v)


def _adamw_small(gs, ws, ms, vs):
    k = len(ws)

    def body(*refs):
        ins, outs = refs[:4 * k], refs[4 * k:]
        for t in range(k):
            g, w, m, v = (ins[j * k + t][...] for j in range(4))
            outs[t][...], outs[k + t][...], outs[2 * k + t][...] = _adamw_math(w, g, m, v)

    whole = lambda a: pl.BlockSpec(a.shape, lambda i, nd=a.ndim: (0,) * nd)
    res = pl.pallas_call(
        body, name="adamw_small", grid=(1,), in_specs=[whole(a) for a in (*gs, *ws, *ms, *vs)],
        out_specs=[whole(a) for a in ws] * 3, out_shape=[SDS(a.shape, f32) for a in ws] * 3,
        compiler_params=_cp())(*gs, *ws, *ms, *vs)
    return [[res[j * k + t] for j in range(3)] for t in range(k)]


WEIGHTS = ("meta_tokens", "mix_norm_w", "w_in", "fox_forget_bias", "dn_conv_w", "dn_a_log", "dn_dt_bias", "dn_out_norm_w",
           "w_branch_fox", "w_branch_dn", "w_out", "ffn_norm_w", "w_ffn_gate", "w_ffn_up", "w_ffn_down", "final_norm_w")
COL_SHARDED = ("w_in", "w_branch_fox", "w_branch_dn", "w_ffn_gate", "w_ffn_up")
ROW_SHARDED = ("w_out", "w_ffn_down")
BIG = COL_SHARDED + ROW_SHARDED
LATE = tuple(n for n in BIG if n != "w_in")
LATE_MIX = ("w_branch_fox", "w_branch_dn", "w_out")
LATE_FFN = ("w_ffn_gate", "w_ffn_up")
LATE_DOWN = ("w_ffn_down",)
SMALL = tuple(n for n in WEIGHTS if n not in BIG)
TRANSPOSED = ("w_in", "w_ffn_gate", "w_ffn_up")


def _to_slabs(name, g):
    r, c = g.shape
    if name in COL_SHARDED and name not in TRANSPOSED:
        return _b(g.reshape(r, N_DEV, c // N_DEV).transpose(1, 0, 2))
    return _b(g.reshape(N_DEV, r // N_DEV, c))


def _from_slabs(name, s):
    n, r, c = s.shape
    if name in COL_SHARDED and name not in TRANSPOSED:
        return s.transpose(1, 0, 2).reshape(r, n * c)
    return s.reshape(n * r, c)


def kernel(x, meta_tokens, mix_norm_w, w_in, fox_forget_bias, dn_conv_w, dn_a_log, dn_dt_bias, dn_out_norm_w, w_branch_fox, w_branch_dn, w_out, ffn_norm_w, w_ffn_gate, w_ffn_up, w_ffn_down, final_norm_w, loss_target, m_meta_tokens, m_mix_norm_w, m_w_in, m_fox_forget_bias, m_dn_conv_w, m_dn_a_log, m_dn_dt_bias, m_dn_out_norm_w, m_w_branch_fox, m_w_branch_dn, m_w_out, m_ffn_norm_w, m_w_ffn_gate, m_w_ffn_up, m_w_ffn_down, m_final_norm_w, v_meta_tokens, v_mix_norm_w, v_w_in, v_fox_forget_bias, v_dn_conv_w, v_dn_a_log, v_dn_dt_bias, v_dn_out_norm_w, v_w_branch_fox, v_w_branch_dn, v_w_out, v_ffn_norm_w, v_w_ffn_gate, v_w_ffn_up, v_w_ffn_down, v_final_norm_w):
    w = dict(meta_tokens=meta_tokens, mix_norm_w=mix_norm_w, w_in=w_in, fox_forget_bias=fox_forget_bias, dn_conv_w=dn_conv_w, dn_a_log=dn_a_log, dn_dt_bias=dn_dt_bias, dn_out_norm_w=dn_out_norm_w, w_branch_fox=w_branch_fox, w_branch_dn=w_branch_dn, w_out=w_out, ffn_norm_w=ffn_norm_w, w_ffn_gate=w_ffn_gate, w_ffn_up=w_ffn_up, w_ffn_down=w_ffn_down, final_norm_w=final_norm_w)
    mom = dict(meta_tokens=m_meta_tokens, mix_norm_w=m_mix_norm_w, w_in=m_w_in, fox_forget_bias=m_fox_forget_bias, dn_conv_w=m_dn_conv_w, dn_a_log=m_dn_a_log, dn_dt_bias=m_dn_dt_bias, dn_out_norm_w=m_dn_out_norm_w, w_branch_fox=m_w_branch_fox, w_branch_dn=m_w_branch_dn, w_out=m_w_out, ffn_norm_w=m_ffn_norm_w, w_ffn_gate=m_w_ffn_gate, w_ffn_up=m_w_ffn_up, w_ffn_down=m_w_ffn_down, final_norm_w=m_final_norm_w)
    var = dict(meta_tokens=v_meta_tokens, mix_norm_w=v_mix_norm_w, w_in=v_w_in, fox_forget_bias=v_fox_forget_bias, dn_conv_w=v_dn_conv_w, dn_a_log=v_dn_a_log, dn_dt_bias=v_dn_dt_bias, dn_out_norm_w=v_dn_out_norm_w, w_branch_fox=v_w_branch_fox, w_branch_dn=v_w_branch_dn, w_out=v_w_out, ffn_norm_w=v_ffn_norm_w, w_ffn_gate=v_w_ffn_gate, w_ffn_up=v_w_ffn_up, w_ffn_down=v_w_ffn_down, final_norm_w=v_final_norm_w)
    two_d = lambda a: a.reshape(a.shape[-2:]) if a.ndim >= 2 else a.reshape(1, -1)
    me = 4 * lax.axis_index("x") + 2 * lax.axis_index("y") + lax.axis_index("c")
    for d in (w, mom, var):
        for n in TRANSPOSED:
            d[n] = jnp.swapaxes(d[n], -1, -2)

    g_in, g_meta, g_cw = _gather_two_level([_b(two_d(w["w_in"])), two_d(w["meta_tokens"]), two_d(w["dn_conv_w"])], "all_gather_early")
    meta = g_meta.transpose(1, 0, 2).reshape(N_META, -1)
    cw = g_cw.transpose(1, 0, 2).reshape(CONV_K, -1)

    gx, g_small, recv = _local_step(
        x[0], loss_target[0], meta, two_d(w["mix_norm_w"]), _from_slabs("w_in", g_in), two_d(w["fox_forget_bias"]), cw, two_d(w["dn_a_log"]),
        two_d(w["dn_dt_bias"]), two_d(w["dn_out_norm_w"]), two_d(w["ffn_norm_w"]), two_d(w["final_norm_w"]),
        {n: _b(two_d(w[n])) for n in LATE})

    order = ("loss",) + SMALL
    flat = jnp.concatenate([g_small[n].reshape(-1) for n in order])
    rows = -(-flat.shape[0] // (8 * HP)) * 8
    tot = _all_reduce_small(jnp.pad(flat, (0, rows * HP - flat.shape[0])).reshape(rows, HP)).reshape(-1)
    summed, o = {}, 0
    for n in order:
        k = g_small[n].shape[0]
        summed[n] = tot[o:o + k]
        o += k
    loss = summed["loss"][0]
    d_model = x.shape[-1]
    mcols, ccols = d_model // N_DEV, dn_conv_w.shape[-1]
    summed["meta_tokens"] = lax.dynamic_slice(summed["meta_tokens"].reshape(N_META, d_model), (0, me * mcols), (N_META, mcols)).reshape(-1)
    summed["dn_conv_w"] = lax.dynamic_slice(summed["dn_conv_w"].reshape(CONV_K, ccols * N_DEV), (0, me * ccols), (CONV_K, ccols)).reshape(-1)

    res = {}
    for n in BIG:
        res[n] = _adamw(recv[n], w[n], mom[n], var[n], "adamw_" + n)
        if n in TRANSPOSED:
            res[n] = [jnp.swapaxes(r, -1, -2) for r in res[n]]
    gs = [summed[n].reshape(two_d(w[n]).shape) for n in SMALL]
    sres = _adamw_small(gs, *[[two_d(d[n]) for n in SMALL] for d in (w, mom, var)])
    for n, g, r in zip(SMALL, gs, sres):
        res[n] = [a.reshape(w[n].shape) for a in (g, *r)]
    return (loss, gx[None], *[res[n][0] for n in WEIGHTS], *[res[n][1] for n in WEIGHTS], *[res[n][2] for n in WEIGHTS], *[res[n][3] for n in WEIGHTS])
```

```python
import functools
import math

import jax
import jax.numpy as jnp
from jax import lax
from jax.experimental import pallas as pl
from jax.experimental.pallas import tpu as pltpu

f32, bf16 = jnp.float32, jnp.bfloat16
HI = lax.Precision.HIGHEST
MESH = pl.DeviceIdType.MESH
SDS = jax.ShapeDtypeStruct

N_DEV = 8
N_META = 16
PREFIX = 128
N_PAD = PREFIX - N_META
FOX_H, FOX_D = 8, 64
DN_H, DN_D = 4, 128
DN_C = 64
CONV_K = 4
HP = 128
SMALL_W = 256
EPS = 1e-6
NEG = -1e30
C_Q0, C_K0 = 64, 67
LSE_COL = 64
LOG2E, LN2 = 1.4426950408889634, 0.6931471805599453
C_LSE0, C_DELTA0 = 70, 65

ADAM_LR, ADAM_B1, ADAM_B2, ADAM_EPS, ADAM_WD, ADAM_STEP = 0.001, 0.9, 0.999, 1e-08, 0.01, 10

VMEM_LIMIT_V7X = 56 * 1024 * 1024
ROW_TILES = (384, 128)
ATTN_TILES = (384, 128)
FFN_TILES = (192, 64)
FFN_CHUNK = 256
FOX_HEAD_GROUP = 4
FOX_HEAD_GROUP_FWD = 8
ADAMW_TILES = (256, 128)
ADAMW_WHOLE_ROWS = 512
MAX_WGRAD_BLOCK = 1408
DN_INTRA_GROUP = (6, 3, 2, 1)
DN_SCAN_GROUP = (6, 3, 2, 1)


def _pick(n, cands):
    for c in cands:
        if n % c == 0:
            return c
    raise ValueError(f"no tile of {cands} divides {n}")


def _cp(n_axes=1):
    return pltpu.CompilerParams(dimension_semantics=("arbitrary",) * n_axes, vmem_limit_bytes=VMEM_LIMIT_V7X)


def _b(x):
    return x.astype(bf16)


def _dot(a, b):
    return jnp.dot(a, b, preferred_element_type=f32)


def _dot_nt(a, b):
    return lax.dot_general(a, b, (((1,), (1,)), ((), ())), preferred_element_type=f32)


def _dot_tn(a, b):
    return lax.dot_general(a, b, (((0,), (0,)), ((), ())), preferred_element_type=f32)


def _dot_hi(a, b):
    return jnp.dot(a, b, preferred_element_type=f32, precision=HI)


def _iota(shape, dim):
    return lax.broadcasted_iota(jnp.int32, shape, dim)


def _rms(x, w):
    return x * lax.rsqrt(jnp.mean(x * x, axis=-1, keepdims=True) + EPS) * w


def _sigmoid(x):
    return jax.nn.sigmoid(x)


def _load_once(pairs, sems):
    @pl.when(pl.program_id(0) == 0)
    def _():
        cps = [pltpu.make_async_copy(src, dst, sems.at[k]) for k, (src, dst) in enumerate(pairs)]
        for cp in cps:
            cp.start()
        for cp in cps:
            cp.wait()


def _seg_layout(d_model):
    return (("fq", FOX_H * HP, bf16, "fox"), ("fk", FOX_H * HP, bf16, "fox"), ("sf", SMALL_W, f32, "fox"),
            ("fv", FOX_H * HP, bf16, "fv"),
            ("dn", 3 * DN_H * DN_D, f32, "dn"), ("sd", SMALL_W, f32, "dn"),
            ("dz", DN_H * DN_D, f32, "mix"), ("ga", d_model, f32, "mix"), ("gb", d_model, f32, "mix"))


GROUPS = ("fox", "fv", "dn", "mix")


def _shifted_blocks(TM):
    g = math.gcd(TM, PREFIX)
    nb, npre = TM // g, PREFIX // g
    assert nb >= npre
    return g, nb, npre, [lambda i, j=j: (jnp.maximum(i * nb + j - npre, 0), 0) for j in range(nb)]


def _in_proj(x, pre, w1, wpt, shards):
    T, D = x.shape
    L = T + PREFIX
    NP = wpt.shape[0]
    TM = _pick(L, ROW_TILES)
    NT = L // TM
    g, nb, npre, xmaps = _shifted_blocks(TM)
    segs = _seg_layout(D)
    ns, n = len(segs), len(shards)
    offs, o = [], 0
    for _, wd, _, _ in segs:
        offs.append(o)
        o += wd
    assert o == NP

    def body(*refs):
        x_refs, (pre_ref, w1_ref, wp_hbm), rest = refs[:nb], refs[nb:nb + 3], refs[nb + 3:]
        ins, h_ref, xn_ref, outs, gouts = rest[:n], rest[n], rest[n + 1], rest[n + 2:n + 2 + ns], rest[n + 2 + ns:2 * n + 2 + ns]
        wp_v, sems = rest[2 * n + 2 + ns:2 * n + 4 + ns]
        xsems = rest[2 * n + 4 + ns:]
        _load_once([(wp_hbm, wp_v)], sems)

        _exchange_start(ins, gouts, *xsems, gather=True, when=pl.program_id(0) == 0)

        first = pl.program_id(0) == 0
        h = jnp.concatenate([jnp.where(first, pre_ref[j * g:(j + 1) * g, :], r[...]) if j < npre else r[...]
                             for j, r in enumerate(x_refs)], axis=0)
        h_ref[...] = h
        xn = _b(_rms(h, w1_ref[...]))
        xn_ref[...] = xn
        for o_ref, off, (_, wd, _, _) in zip(outs, offs, segs):
            o_ref[...] = _dot_nt(xn, wp_v[off:off + wd, :]).astype(o_ref.dtype)

        _exchange_wait(ins, gouts, *xsems, gather=True, when=pl.program_id(0) == NT - 1)

    row = lambda wd: pl.BlockSpec((TM, wd), lambda i: (i, 0))
    anyspec = pl.BlockSpec(memory_space=pl.ANY)
    res = pl.pallas_call(
        body, name="in_proj", grid=(NT,),
        in_specs=[pl.BlockSpec((g, D), m) for m in xmaps] + [pl.BlockSpec((PREFIX, D), lambda i: (0, 0)), pl.BlockSpec((1, D), lambda i: (0, 0)), anyspec]
        + [anyspec] * n,
        out_specs=[row(D), row(D)] + [row(wd) for _, wd, _, _ in segs] + [anyspec] * n,
        out_shape=[SDS((L, D), f32), SDS((L, D), bf16)] + [SDS((L, wd), dt) for _, wd, dt, _ in segs]
        + [SDS((N_DEV,) + a.shape, a.dtype) for a in shards],
        scratch_shapes=[pltpu.VMEM((NP, D), bf16), pltpu.SemaphoreType.DMA((1,))] + _exchange_sems(n),
        compiler_params=_cp())(*([x] * nb), pre, w1, wpt, *shards)
    return res[:2 + ns], res[2 + ns:]


def _in_proj_bwd(dgroups, wpt, h0, w1, dh1, slabs):
    L, D = h0.shape
    NP = wpt.shape[0]
    TM = _pick(L, ROW_TILES)
    NT = L // TM
    widths = [g.shape[1] for g in dgroups]
    assert sum(widths) == NP
    ng, n = len(dgroups), len(slabs)

    def body(*refs):
        dg_refs, (wp_hbm, h_ref, w1_ref, dh1_ref) = refs[:ng], refs[ng:ng + 4]
        ins, (dh0_ref, acc_ref), outs = refs[ng + 4:ng + 4 + n], refs[ng + 4 + n:ng + 6 + n], refs[ng + 6 + n:ng + 6 + 2 * n]
        wp_v, sems = refs[ng + 6 + 2 * n:ng + 8 + 2 * n]
        xsems = refs[ng + 8 + 2 * n:]
        _load_once([(wp_hbm, wp_v)], sems)

        @pl.when(pl.program_id(0) == 0)
        def _():
            acc_ref[...] = jnp.zeros_like(acc_ref)

        _exchange_start(ins, outs, *xsems, gather=False, when=pl.program_id(0) == 0)

        dxn, off = None, 0
        for g_ref, wd in zip(dg_refs, widths):
            part = _dot(g_ref[...], wp_v[off:off + wd, :])
            dxn = part if dxn is None else dxn + part
            off += wd
        _, vjp = jax.vjp(_rms, h_ref[...], w1_ref[...])
        dh0n, dw1 = vjp(dxn)
        dh0_ref[...] = dh1_ref[...] + dh0n
        acc_ref[0:1, :] += dw1

        _exchange_wait(ins, outs, *xsems, gather=False, when=pl.program_id(0) == NT - 1)

    row = lambda wd: pl.BlockSpec((TM, wd), lambda i: (i, 0))
    anyspec = pl.BlockSpec(memory_space=pl.ANY)
    res = pl.pallas_call(
        body, name="in_proj_bwd", grid=(NT,),
        in_specs=[row(wd) for wd in widths] + [anyspec, row(D), pl.BlockSpec((1, D), lambda i: (0, 0)), row(D)] + [anyspec] * n,
        out_specs=[row(D), pl.BlockSpec((8, D), lambda i: (0, 0))] + [anyspec] * n,
        out_shape=[SDS((L, D), f32), SDS((8, D), f32)] + [SDS(a.shape, a.dtype) for a in slabs],
        scratch_shapes=[pltpu.VMEM((NP, D), bf16), pltpu.SemaphoreType.DMA((1,))] + _exchange_sems(n),
        compiler_params=_cp())(*dgroups, wpt, h0, w1, dh1, *slabs)
    return res[0], res[1], res[2:]


def _matmul_tn(a, b, name):
    L, R = a.shape
    C = b.shape[1]
    br = max(k for k in range(HP, MAX_WGRAD_BLOCK + 1, HP) if R % k == 0)

    def body(a_ref, b_ref, o_ref):
        o_ref[...] = _b(_dot_tn(a_ref[...], b_ref[...]))

    return pl.pallas_call(
        body, name=name, grid=(R // br,),
        in_specs=[pl.BlockSpec((L, br), lambda r: (0, r)), pl.BlockSpec((L, C), lambda r: (0, 0))],
        out_specs=pl.BlockSpec((br, C), lambda r: (r, 0)), out_shape=SDS((R, C), bf16), compiler_params=_cp())(a, b)


def _fox_prep(fq, fk, fv, sf, bias_p):
    L = fq.shape[0]
    T = HP
    NT = L // T
    W = FOX_H * HP

    def body(fq_ref, fk_ref, fv_ref, sf_ref, b_ref, qa_ref, ka_ref, va_ref, carry):
        @pl.when(pl.program_id(0) == 0)
        def _():
            carry[...] = jnp.zeros_like(carry)

        lane, row = _iota((T, HP), 1), _iota((T, HP), 0)
        logf = jnp.where(lane < FOX_H, jax.nn.log_sigmoid(sf_ref[...] + b_ref[...]), 0.0)
        c = _dot_hi((row >= lane).astype(f32), logf) + carry[...]
        carry[...] = jnp.sum(jnp.where(row == T - 1, c, 0.0), axis=0, keepdims=True)
        ones_q = jnp.where((lane >= C_K0) & (lane < C_K0 + 3), 1.0, 0.0)
        ones_k = jnp.where(((lane >= C_Q0) & (lane < C_Q0 + 3)) | ((lane >= C_LSE0) & (lane < C_LSE0 + 3)), 1.0, 0.0)
        ones_v = _b(jnp.where((lane >= LSE_COL) & (lane < C_DELTA0 + 3), 1.0, 0.0))
        for h in range(FOX_H):
            ch = jnp.broadcast_to(jnp.sum(jnp.where(lane == h, c, 0.0), axis=1, keepdims=True), (T, HP)) * LOG2E
            c1 = _b(ch).astype(f32)
            c2 = _b(ch - c1).astype(f32)
            c3 = _b(ch - c1 - c2).astype(f32)
            cq = jnp.where(lane == C_Q0, c1, 0.0) + jnp.where(lane == C_Q0 + 1, c2, 0.0) + jnp.where(lane == C_Q0 + 2, c3, 0.0)
            ck = jnp.where(lane == C_K0, c1, 0.0) + jnp.where(lane == C_K0 + 1, c2, 0.0) + jnp.where(lane == C_K0 + 2, c3, 0.0)
            q = fq_ref[:, h * HP:(h + 1) * HP].astype(f32) * (FOX_D ** -0.5 * LOG2E)
            k = fk_ref[:, h * HP:(h + 1) * HP].astype(f32)
            qa_ref[h] = _b(q + cq + ones_q)
            ka_ref[h] = _b(k + ones_k - ck)
            va_ref[:, h * HP:(h + 1) * HP] = fv_ref[:, h * HP:(h + 1) * HP] + ones_v

    wide = pl.BlockSpec((T, W), lambda i: (i, 0))
    return pl.pallas_call(
        body, name="fox_prep", grid=(NT,),
        in_specs=[wide, wide, wide, pl.BlockSpec((T, HP), lambda i: (i, 0)), pl.BlockSpec((1, HP), lambda i: (0, 0))],
        out_specs=[pl.BlockSpec((FOX_H, T, HP), lambda i: (0, i, 0))] * 2 + [wide],
        out_shape=[SDS((FOX_H, L, HP), bf16)] * 2 + [SDS((L, W), bf16)],
        scratch_shapes=[pltpu.VMEM((1, HP), f32)], compiler_params=_cp())(fq, fk, fv, sf, bias_p)


def _fox_prep_bwd(dqa, dka, sf, bias_p):
    L = sf.shape[0]
    T = HP
    NT = L // T
    rev = lambda i: (NT - 1 - i, 0)

    W = FOX_H * HP

    def body(dq_ref, dk_ref, sf_ref, b_ref, dg_ref, db_ref, carry):
        @pl.when(pl.program_id(0) == 0)
        def _():
            carry[...] = jnp.zeros_like(carry)
            db_ref[...] = jnp.zeros_like(db_ref)

        dq, dk = dq_ref[...], dk_ref[...]
        dg_ref[:, 0:W] = _b(dq * (FOX_D ** -0.5))
        dg_ref[:, W:2 * W] = _b(dk * LN2)
        lane, row = _iota((T, HP), 1), _iota((T, HP), 0)
        dc = jnp.zeros((T, HP), f32)
        for h in range(FOX_H):
            col = jnp.sum(jnp.where(lane == C_Q0, dq[:, h * HP:(h + 1) * HP], 0.0)
                          - jnp.where(lane == C_K0, dk[:, h * HP:(h + 1) * HP], 0.0), axis=1, keepdims=True)
            dc = dc + jnp.where(lane == h, col, 0.0)
        dl = _dot_hi((row <= lane).astype(f32), dc) + carry[...]
        carry[...] = jnp.sum(jnp.where(row == 0, dl, 0.0), axis=0, keepdims=True)
        dx = jnp.where(lane < FOX_H, dl * _sigmoid(-(sf_ref[...] + b_ref[...])), 0.0)
        dg_ref[:, 2 * W:2 * W + HP] = _b(dx)
        dg_ref[:, 2 * W + HP:] = jnp.zeros((T, SMALL_W - HP), bf16)
        db_ref[0:1, :] += jnp.sum(dx, axis=0, keepdims=True)

    return pl.pallas_call(
        body, name="fox_prep_bwd", grid=(NT,),
        in_specs=[pl.BlockSpec((T, W), rev), pl.BlockSpec((T, W), rev), pl.BlockSpec((T, HP), rev), pl.BlockSpec((1, HP), lambda i: (0, 0))],
        out_specs=[pl.BlockSpec((T, 2 * W + SMALL_W), rev), pl.BlockSpec((8, HP), lambda i: (0, 0))],
        out_shape=[SDS((L, 2 * W + SMALL_W), bf16), SDS((8, HP), f32)],
        scratch_shapes=[pltpu.VMEM((1, HP), f32)], compiler_params=_cp())(dqa, dka, sf, bias_p)


def _tile_start(j, T):
    return j * T if isinstance(j, int) else pl.multiple_of(j * T, T)


def _spread3(x, lane, col0):
    x1 = _b(x).astype(f32)
    x2 = _b(x - x1).astype(f32)
    x3 = _b(x - x1 - x2).astype(f32)
    return jnp.where(lane == col0, x1, 0.0) + jnp.where(lane == col0 + 1, x2, 0.0) + jnp.where(lane == col0 + 2, x3, 0.0)


def _fox_fwd(qa, ka, fv, shards):
    L = qa.shape[1]
    TQ = TK = _pick(L, ATTN_TILES)
    NQ = L // TQ
    n = len(shards)
    HG = FOX_HEAD_GROUP_FWD

    def body(q_ref, k_ref, v_ref, *rest):
        ins, o_ref, qb_ref, outs, sems = rest[:n], rest[n], rest[n + 1], rest[n + 2:2 * n + 2], rest[2 * n + 2:]
        h, i = pl.program_id(0), pl.program_id(1)

        _exchange_start(ins, outs, *sems, gather=True, when=(h == 0) & (i == 0))

        qs = [q_ref[a] for a in range(HG)]
        rowg = i * TQ + _iota((TQ, TK), 0)
        colb = _iota((TQ, TK), 1)

        def step(j, carry, masked):
            ms, accs = carry
            k0 = _tile_start(j, TK)
            ss = [_dot_nt(qs[a], k_ref[a, pl.ds(k0, TK), :]) for a in range(HG)]
            if masked:
                colg = colb + j * TK
                keep = (colg <= rowg) & (colg >= N_PAD)
                ss = [jnp.where(keep, s, NEG) for s in ss]
            m_new = [jnp.maximum(m, jnp.max(s, axis=1, keepdims=True)) for m, s in zip(ms, ss)]
            ps = [_b(jnp.exp2(s - m)) for s, m in zip(ss, m_new)]
            alphas = [jnp.exp2(m - mn) for m, mn in zip(ms, m_new)]
            accs = [al * acc + _dot(p, v_ref[pl.ds(k0, TK), a * HP:(a + 1) * HP]) for a, (al, acc, p) in enumerate(zip(alphas, accs, ps))]
            return m_new, accs

        init = ([jnp.full((TQ, 1), NEG, f32)] * HG, [jnp.zeros((TQ, HP), f32)] * HG)
        carry = step(0, init, True)
        carry = lax.fori_loop(1, i, functools.partial(step, masked=False), carry)
        ms, accs = lax.fori_loop(jnp.maximum(i, 1), i + 1, functools.partial(step, masked=True), carry)
        lane = _iota((TQ, HP), 1)
        for a in range(HG):
            l = jnp.sum(jnp.where(lane == LSE_COL, accs[a], 0.0), axis=1, keepdims=True)
            lse = ms[a] + jnp.log2(l)
            o_ref[:, a * HP:(a + 1) * HP] = jnp.where(lane == LSE_COL, lse, accs[a] / l)
            qb_ref[a] = _b(qs[a].astype(f32) - _spread3(jnp.broadcast_to(lse, (TQ, HP)), lane, C_LSE0))

        _exchange_wait(ins, outs, *sems, gather=True, when=(h == FOX_H // HG - 1) & (i == NQ - 1))

    anyspec = pl.BlockSpec(memory_space=pl.ANY)
    qtile = pl.BlockSpec((HG, TQ, HP), lambda h, i: (h, i, 0))
    res = pl.pallas_call(
        body, name="fox_fwd", grid=(FOX_H // HG, NQ),
        in_specs=[qtile, pl.BlockSpec((HG, L, HP), lambda h, i: (h, 0, 0)), pl.BlockSpec((L, HG * HP), lambda h, i: (0, h))] + [anyspec] * n,
        out_specs=[pl.BlockSpec((TQ, HG * HP), lambda h, i: (i, h)), qtile] + [anyspec] * n,
        out_shape=[SDS((L, FOX_H * HP), f32), SDS(qa.shape, bf16)] + [SDS((N_DEV,) + a.shape, a.dtype) for a in shards],
        scratch_shapes=_exchange_sems(n), compiler_params=_cp(2))(qa, ka, fv, *shards)
    return res[0], res[1], res[2:]


def _fox_bwd(qb, ka, va, dob, slabs):
    L = qb.shape[1]
    TQ = TK = _pick(L, ATTN_TILES)
    NQ = L // TQ
    n = len(slabs)
    HG = FOX_HEAD_GROUP

    def body(q_ref, k_ref, v_ref, do_ref, *rest):
        ins, (dq_ref, dk_ref, dv_ref), outs, sems = rest[:n], rest[n:n + 3], rest[n + 3:2 * n + 3], rest[2 * n + 3:]
        h, j = pl.program_id(0), pl.program_id(1)
        cols = [slice(a * HP, (a + 1) * HP) for a in range(HG)]

        _exchange_start(ins, outs, *sems, gather=False, when=(h == 0) & (j == 0))

        @pl.when(j == 0)
        def _():
            dq_ref[...] = jnp.zeros_like(dq_ref)

        kts = [k_ref[a] for a in range(HG)]
        vts = [v_ref[:, cols[a]] for a in range(HG)]
        colg = j * TK + _iota((TQ, TK), 1)
        rowb = _iota((TQ, TK), 0)

        def step(i, carry, masked):
            dks, dvs = carry
            r0 = _tile_start(i, TQ)
            rows = pl.ds(r0, TQ)
            qs = [q_ref[a, rows, :] for a in range(HG)]
            ps = [jnp.exp2(_dot_nt(q, kt)) for q, kt in zip(qs, kts)]
            if masked:
                keep = (colg <= rowb + i * TQ) & (colg >= N_PAD)
                ps = [jnp.where(keep, p, 0.0) for p in ps]
            dobs = [do_ref[rows, cols[a]] for a in range(HG)]
            dvs = [dv + _dot_tn(dob, _b(p)) for dv, p, dob in zip(dvs, ps, dobs)]
            dss = [_b(p * _dot_nt(dob, vt)) for p, dob, vt in zip(ps, dobs, vts)]
            for a in range(HG):
                dq_ref[rows, cols[a]] += _dot(dss[a], kts[a])
            dks = [dk + _dot_tn(q, ds) for dk, ds, q in zip(dks, dss, qs)]
            return dks, dvs

        zeros = [jnp.zeros((HP, TK), f32)] * HG
        carry = step(j, (zeros, zeros), True)
        split = jnp.where(j == 0, NQ, j + 1)
        carry = lax.fori_loop(j + 1, split, functools.partial(step, masked=True), carry)
        dks, dvs = lax.fori_loop(split, NQ, functools.partial(step, masked=False), carry)
        for a in range(HG):
            dk_ref[:, cols[a]] = dks[a].T
            dv_ref[:, cols[a]] = _b(dvs[a].T)

        _exchange_wait(ins, outs, *sems, gather=False, when=(h == FOX_H // HG - 1) & (j == NQ - 1))

    head = pl.BlockSpec((L, HG * HP), lambda h, j: (0, h))
    tile = pl.BlockSpec((TK, HG * HP), lambda h, j: (j, h))
    anyspec = pl.BlockSpec(memory_space=pl.ANY)
    res = pl.pallas_call(
        body, name="fox_bwd", grid=(FOX_H // HG, L // TK),
        in_specs=[pl.BlockSpec((HG, L, HP), lambda h, j: (h, 0, 0)), pl.BlockSpec((HG, TK, HP), lambda h, j: (h, j, 0)), tile, head]
        + [anyspec] * n,
        out_specs=[head, tile, tile] + [anyspec] * n,
        out_shape=[SDS((L, FOX_H * HP), f32), SDS((L, FOX_H * HP), f32), SDS((L, FOX_H * HP), bf16)] + [SDS(a.shape, a.dtype) for a in slabs],
        scratch_shapes=_exchange_sems(n), compiler_params=_cp(2))(qb, ka, va, dob, *slabs)
    return res[:3], res[3:]


def _dn_post(y, sd, alog_p, dt_p, valid):
    a = y * _sigmoid(y)
    W = DN_H * DN_D
    heads = []
    for part, scale in ((0, DN_D ** -0.5), (1, 1.0)):
        for h in range(DN_H):
            xh = a[:, part * W + h * DN_D:part * W + (h + 1) * DN_D]
            heads.append(xh * lax.rsqrt(jnp.sum(xh * xh, axis=-1, keepdims=True) + EPS) * scale)
    q = jnp.concatenate(heads[:DN_H], axis=1)
    k = jnp.concatenate(heads[DN_H:], axis=1)
    v = a[:, 2 * W:3 * W]
    lane = _iota(sd.shape, 1)
    beta = _sigmoid(sd) * valid
    g = -jnp.exp(alog_p) * jax.nn.softplus(sd + dt_p) * valid
    bg = jnp.where(lane < DN_H, beta, jnp.where(lane < 2 * DN_H, g, 0.0))
    return q, k, v, bg


def _conv_fwd(ext_ref, cw_ref, TM):
    y = cw_ref[0:1, :] * ext_ref[8 - (CONV_K - 1):8 - (CONV_K - 1) + TM, :]
    for i in range(1, CONV_K):
        o = 8 - (CONV_K - 1) + i
        y = y + cw_ref[i:i + 1, :] * ext_ref[o:o + TM, :]
    return y


def _dn_prep(dn, sd, cw, alog_p, dt_p):
    L, W3 = dn.shape
    TM = _pick(L, ROW_TILES)
    W = DN_H * DN_D

    def body(dn_ref, halo_ref, sd_ref, cw_ref, al_ref, dt_ref, q_ref, k_ref, v_ref, bg_ref, ext):
        i = pl.program_id(0)
        ext[0:8, :] = jnp.where(i == 0, 0.0, halo_ref[...])
        ext[8:, :] = dn_ref[...]
        y = _conv_fwd(ext, cw_ref, TM)
        valid = ((i * TM + _iota((TM, 1), 0)) >= N_PAD).astype(f32)
        q, k, v, bg = _dn_post(y, sd_ref[...], al_ref[...], dt_ref[...], valid)
        q_ref[...], k_ref[...], v_ref[...], bg_ref[...] = q, k, v, bg

    row = lambda wd: pl.BlockSpec((TM, wd), lambda i: (i, 0))
    vec = pl.BlockSpec((1, HP), lambda i: (0, 0))
    return pl.pallas_call(
        body, name="dn_prep", grid=(L // TM,),
        in_specs=[row(W3), pl.BlockSpec((8, W3), lambda i: (jnp.maximum(i * (TM // 8) - 1, 0), 0)), row(HP),
                  pl.BlockSpec((CONV_K, W3), lambda i: (0, 0)), vec, vec],
        out_specs=[row(W), row(W), row(W), row(HP)],
        out_shape=[SDS((L, W), f32)] * 3 + [SDS((L, HP), f32)],
        scratch_shapes=[pltpu.VMEM((TM + 8, W3), f32)], compiler_params=_cp())(dn, dn, sd, cw, alog_p, dt_p)


def _dn_prep_bwd(dn, sd, cw, alog_p, dt_p, dq, dk, dv, dbg):
    L, W3 = dn.shape
    TM = _pick(L, ROW_TILES)
    NT = L // TM
    W = DN_H * DN_D

    def body(dn_ref, halo_ref, sd_ref, cw_ref, al_ref, dt_ref, dq_ref, dk_ref, dv_ref, dbg_ref,
             dg_ref, dcw_ref, dp_ref, ext, dyp, carry):
        i = pl.program_id(0)
        t = NT - 1 - i

        @pl.when(i == 0)
        def _():
            carry[...] = jnp.zeros_like(carry)
            dcw_ref[...] = jnp.zeros_like(dcw_ref)
            dp_ref[...] = jnp.zeros_like(dp_ref)
            dyp[...] = jnp.zeros_like(dyp)

        ext[0:8, :] = jnp.where(t == 0, 0.0, halo_ref[...])
        ext[8:, :] = dn_ref[...]
        y = _conv_fwd(ext, cw_ref, TM)
        valid = ((t * TM + _iota((TM, 1), 0)) >= N_PAD).astype(f32)
        _, vjp = jax.vjp(functools.partial(_dn_post, valid=valid), y, sd_ref[...], al_ref[...], dt_ref[...])
        dy, dsd, dal, ddt = vjp((dq_ref[...], dk_ref[...], dv_ref[...], dbg_ref[...]))
        dg_ref[:, W3:W3 + HP] = _b(dsd)
        dg_ref[:, W3 + HP:] = jnp.zeros((TM, SMALL_W - HP), bf16)
        dp_ref[0:1, :] += dal
        dp_ref[1:2, :] += ddt
        dyp[8:8 + TM, :] = dy
        o0 = CONV_K - 1
        dext = cw_ref[0:1, :] * dyp[o0:o0 + TM + 8, :]
        for k in range(1, CONV_K):
            dext = dext + cw_ref[k:k + 1, :] * dyp[o0 - k:o0 - k + TM + 8, :]
        for k in range(CONV_K):
            o = 8 - (CONV_K - 1) + k
            dcw_ref[k:k + 1, :] += jnp.sum(dy * ext[o:o + TM, :], axis=0, keepdims=True)
        dg_ref[:, 0:W3] = _b(jnp.concatenate([dext[8:TM, :], dext[TM:TM + 8, :] + carry[...]], axis=0))
        carry[...] = dext[0:8, :]

    row = lambda wd: pl.BlockSpec((TM, wd), lambda i: (NT - 1 - i, 0))
    vec = pl.BlockSpec((1, HP), lambda i: (0, 0))
    return pl.pallas_call(
        body, name="dn_prep_bwd", grid=(NT,),
        in_specs=[row(W3), pl.BlockSpec((8, W3), lambda i: (jnp.maximum((NT - 1 - i) * (TM // 8) - 1, 0), 0)), row(HP),
                  pl.BlockSpec((CONV_K, W3), lambda i: (0, 0)), vec, vec, row(W), row(W), row(W), row(HP)],
        out_specs=[row(W3 + SMALL_W), pl.BlockSpec((8, W3), lambda i: (0, 0)), pl.BlockSpec((8, HP), lambda i: (0, 0))],
        out_shape=[SDS((L, W3 + SMALL_W), bf16), SDS((8, W3), f32), SDS((8, HP), f32)],
        scratch_shapes=[pltpu.VMEM((TM + 8, W3), f32), pltpu.VMEM((TM + 16, W3), f32), pltpu.VMEM((8, W3), f32)],
        compiler_params=_cp())(dn, dn, sd, cw, alog_p, dt_p, dq, dk, dv, dbg)


def _split2(x):
    hi = _b(x)
    return hi, _b(x - hi.astype(f32))


def _split3(x):
    hi = _b(x)
    r = x - hi.astype(f32)
    mid = _b(r)
    return hi, mid, _b(r - mid.astype(f32))


def _x3(a, b, dot):
    (a1, a2), (b1, b2) = _split2(a), _split2(b)
    return dot(a1, b1) + (dot(a1, b2) + dot(a2, b1))


@jax.custom_vjp
def _dot_x3(a, b):
    return _x3(a, b, _dot)


_dot_x3.defvjp(lambda a, b: (_x3(a, b, _dot), (a, b)), lambda res, g: (_x3(g, res[1], _dot_nt), _x3(res[0], g, _dot_tn)))


def _exact3(m, x, dot):
    x1, x2, x3 = _split3(x)
    return dot(m, x1) + (dot(m, x2) + dot(m, x3))


def _tri_ones(C, lower):
    row, col = _iota((C, C), 0), _iota((C, C), 1)
    return _b(((row >= col) if lower else (row <= col)).astype(f32))


@jax.custom_vjp
def _chunk_cumsum(x):
    return _exact3(_tri_ones(x.shape[0], True), x, _dot)


_chunk_cumsum.defvjp(lambda x: (_exact3(_tri_ones(x.shape[0], True), x, _dot), None),
                     lambda _, g: (_exact3(_tri_ones(g.shape[0], False), g, _dot),))


def _mxu_transpose(x):
    C = x.shape[0]
    eye = _b((_iota((C, C), 0) == _iota((C, C), 1)).astype(f32))
    return _exact3(eye, x, lambda m, part: _dot_tn(part, m))


@jax.custom_vjp
def _transpose_exact(x):
    return _mxu_transpose(x)


_transpose_exact.defvjp(lambda x: (_mxu_transpose(x), None), lambda _, g: (_mxu_transpose(g),))


def _unit_lower_inverses(lows):
    C = lows[0].shape[0]
    P = jnp.stack(lows)
    X = (_iota((C, C), 0) == _iota((C, C), 1)).astype(f32)[None] - P
    bdot = functools.partial(_x3, dot=lambda a, b: jnp.einsum("bij,bjk->bik", a, b, preferred_element_type=f32))
    for _ in range(5):
        P = bdot(P, P)
        X = X + bdot(X, P)
    return [X[i] for i in range(len(lows))]


@jax.custom_vjp
def _inverse_given(low, X):
    return X


def _inverse_given_bwd(X, g):
    return -_x3(_x3(X, g, _dot_tn), X, _dot_nt), jnp.zeros_like(X)


_inverse_given.defvjp(lambda low, X: (X, X), _inverse_given_bwd)


def _dn_intra_pre(q, k, v, bg):
    C = DN_C
    row, col = _iota((C, C), 0), _iota((C, C), 1)
    tri = row >= col
    G = _chunk_cumsum(bg)
    GT = _transpose_exact(G)
    lane = _iota((C, HP), 1)
    rowt = _iota((HP, C), 0)
    last = _iota((C, 1), 0) == C - 1
    heads = []
    for h in range(DN_H):
        beta = jnp.sum(jnp.where(lane == h, bg, 0.0), axis=1, keepdims=True)
        gcol = jnp.sum(jnp.where(lane == DN_H + h, G, 0.0), axis=1, keepdims=True)
        grow = jnp.sum(jnp.where(rowt == DN_H + h, GT, 0.0), axis=0, keepdims=True)
        glast = jnp.sum(jnp.where(last, gcol, 0.0), axis=0, keepdims=True)
        decay = jnp.exp(jnp.where(tri, gcol - grow, NEG))
        qh, kh, vh = (t[:, h * DN_D:(h + 1) * DN_D] for t in (q, k, v))
        kb = kh * beta
        low = jnp.where(row > col, _dot_nt(_b(kb), _b(kh)) * decay, 0.0)
        heads.append((beta, gcol, glast, decay, qh, kh, vh, kb, low))
    return heads


def _dn_intra_post(heads, xs):
    lane1 = _iota((1, HP), 1)
    us, ws, qds, kds, attns = [], [], [], [], []
    glrow = jnp.zeros((1, HP), f32)
    for h, ((beta, gcol, glast, decay, qh, kh, vh, kb, _), X) in enumerate(zip(heads, xs)):
        eg = jnp.exp(gcol)
        us.append(_dot_x3(X, vh * beta))
        ws.append(_dot_x3(X, kb * eg))
        attns.append(_dot_nt(_b(qh), _b(kh)) * decay)
        qds.append(qh * eg)
        kds.append(kh * jnp.exp(glast - gcol))
        glrow = glrow + jnp.where(lane1 == h, glast, 0.0)
    cat = lambda xs_: jnp.concatenate(xs_, axis=1)
    return cat(us), cat(ws), cat(qds), cat(kds), cat(attns), glrow, cat(list(xs))


def _dn_intra_group(q, k, v, bg, xs):
    G = q.shape[0] // DN_C
    rows = [slice(j * DN_C, (j + 1) * DN_C) for j in range(G)]
    pre = [_dn_intra_pre(q[r, :], k[r, :], v[r, :], bg[r, :]) for r in rows]
    inv = [[_inverse_given(hd[-1], x) for hd, x in zip(heads, xj)] for heads, xj in zip(pre, xs)]
    post = [_dn_intra_post(heads, xj) for heads, xj in zip(pre, inv)]
    return tuple(jnp.concatenate([p[i] for p in post], axis=0) for i in range(5)) + (tuple(p[5] for p in post),)


def _lane_pick(rowvec, h):
    return jnp.sum(jnp.where(_iota(rowvec.shape, 1) == h, rowvec, 0.0), axis=1, keepdims=True)


def _dn_intra(q, k, v, bg, shards):
    L, W = q.shape
    NC = L // DN_C
    G = _pick(NC, DN_INTRA_GROUP)
    R = G * DN_C
    NS = NC // G
    WA = DN_H * DN_C
    n = len(shards)

    def body(q_ref, k_ref, v_ref, bg_ref, *rest):
        ins, (u_ref, w_ref, qd_ref, kd_ref, at_ref, gl_ref, x_ref), gouts, sems = rest[:n], rest[n:n + 7], rest[n + 7:2 * n + 7], rest[2 * n + 7:]
        _exchange_start(ins, gouts, *sems, gather=True, when=pl.program_id(0) == 0)
        rows = [slice(j * DN_C, (j + 1) * DN_C) for j in range(G)]
        pre = [_dn_intra_pre(q_ref[r, :], k_ref[r, :], v_ref[r, :], bg_ref[r, :]) for r in rows]
        inv = _unit_lower_inverses([hd[-1] for heads in pre for hd in heads])
        for j, r in enumerate(rows):
            u, w, qd, kd, at, gl, xs = _dn_intra_post(pre[j], inv[j * DN_H:(j + 1) * DN_H])
            u_ref[r, :], x_ref[r, :] = u, xs
            w_ref[r, :], qd_ref[r, :], kd_ref[r, :], at_ref[r, :] = _b(w), _b(qd), _b(kd), _b(at)
            gl_ref[j] = gl
        _exchange_wait(ins, gouts, *sems, gather=True, when=pl.program_id(0) == NS - 1)

    row = lambda wd: pl.BlockSpec((R, wd), lambda s: (s, 0))
    anyspec = pl.BlockSpec(memory_space=pl.ANY)
    res = pl.pallas_call(
        body, name="dn_intra", grid=(NS,),
        in_specs=[row(W), row(W), row(W), row(HP)] + [anyspec] * n,
        out_specs=[row(W), row(W), row(W), row(W), row(WA), pl.BlockSpec((G, 1, HP), lambda s: (s, 0, 0)), row(WA)] + [anyspec] * n,
        out_shape=[SDS((L, W), f32), SDS((L, W), bf16), SDS((L, W), bf16), SDS((L, W), bf16), SDS((L, WA), bf16), SDS((NC, 1, HP), f32),
                   SDS((L, WA), f32)] + [SDS((N_DEV,) + a.shape, a.dtype) for a in shards],
        scratch_shapes=_exchange_sems(n), compiler_params=_cp())(q, k, v, bg, *shards)
    return res[:7], res[7:]


def _dn_scan(u, w, qd, kd, at, gl):
    L, W = u.shape
    NC = L // DN_C
    G = _pick(NC, DN_SCAN_GROUP)
    R = G * DN_C

    def body(u_ref, w_ref, qd_ref, kd_ref, at_ref, gl_ref, o_ref, vn_ref, s_ref, S):
        @pl.when(pl.program_id(0) == 0)
        def _():
            S[...] = jnp.zeros_like(S)

        for j in range(G):
            r = slice(j * DN_C, (j + 1) * DN_C)
            glrow = gl_ref[j]
            for h in range(DN_H):
                c = slice(h * DN_D, (h + 1) * DN_D)
                Sh = S[h]
                s_ref[j, h] = Sh
                Sb = _b(Sh)
                vb = _b(u_ref[r, c] - _dot(w_ref[r, c], Sb))
                vn_ref[r, c] = vb
                o_ref[r, c] = _dot(qd_ref[r, c], Sb) + _dot(at_ref[r, h * DN_C:(h + 1) * DN_C], vb)
                S[h] = Sh * jnp.exp(_lane_pick(glrow, h)) + _dot_tn(kd_ref[r, c], vb)

    row = lambda wd: pl.BlockSpec((R, wd), lambda n: (n, 0))
    return pl.pallas_call(
        body, name="dn_scan", grid=(NC // G,),
        in_specs=[row(W), row(W), row(W), row(W), row(DN_H * DN_C), pl.BlockSpec((G, 1, HP), lambda n: (n, 0, 0))],
        out_specs=[row(W), row(W), pl.BlockSpec((G, DN_H, DN_D, DN_D), lambda n: (n, 0, 0, 0))],
        out_shape=[SDS((L, W), f32), SDS((L, W), bf16), SDS((NC, DN_H, DN_D, DN_D), f32)],
        scratch_shapes=[pltpu.VMEM((DN_H, DN_D, DN_D), f32)], compiler_params=_cp())(u, w, qd, kd, at, gl)


def _dn_scan_bwd(do, w, qd, kd, at, gl):
    L, W = do.shape
    NC = L // DN_C
    G = _pick(NC, DN_SCAN_GROUP)
    R = G * DN_C
    NS = NC // G

    def body(do_ref, w_ref, qd_ref, kd_ref, at_ref, gl_ref, dvn_ref, ds_ref, dS):
        @pl.when(pl.program_id(0) == 0)
        def _():
            dS[...] = jnp.zeros_like(dS)

        for j in reversed(range(G)):
            r = slice(j * DN_C, (j + 1) * DN_C)
            glrow = gl_ref[j]
            for h in range(DN_H):
                c = slice(h * DN_D, (h + 1) * DN_D)
                dSo = dS[h]
                ds_ref[j, h] = dSo
                dob = _b(do_ref[r, c])
                dvn = _dot_tn(at_ref[r, h * DN_C:(h + 1) * DN_C], dob) + _dot(kd_ref[r, c], _b(dSo))
                dvn_ref[r, c] = dvn
                dS[h] = _dot_tn(qd_ref[r, c], dob) + dSo * jnp.exp(_lane_pick(glrow, h)) - _dot_tn(w_ref[r, c], _b(dvn))

    row = lambda wd: pl.BlockSpec((R, wd), lambda n: (NS - 1 - n, 0))
    return pl.pallas_call(
        body, name="dn_scan_bwd", grid=(NS,),
        in_specs=[row(W), row(W), row(W), row(W), row(DN_H * DN_C), pl.BlockSpec((G, 1, HP), lambda n: (NS - 1 - n, 0, 0))],
        out_specs=[row(W), pl.BlockSpec((G, DN_H, DN_D, DN_D), lambda n: (NS - 1 - n, 0, 0, 0))],
        out_shape=[SDS((L, W), f32), SDS((NC, DN_H, DN_D, DN_D), f32)],
        scratch_shapes=[pltpu.VMEM((DN_H, DN_D, DN_D), f32)], compiler_params=_cp())(do, w, qd, kd, at, gl)


def _dn_intra_bwd(q, k, v, bg, xinv, do, vn, dvn, states, dstates):
    L, W = q.shape
    NC = L // DN_C
    G = _pick(NC, DN_INTRA_GROUP)
    R = G * DN_C

    def body(q_ref, k_ref, v_ref, bg_ref, x_ref, do_ref, vn_ref, dvn_ref, s_ref, ds_ref, dq_ref, dk_ref, dv_ref, dbg_ref):
        lane1 = _iota((1, HP), 1)
        rows = [slice(j * DN_C, (j + 1) * DN_C) for j in range(G)]
        xs = [[x_ref[r, h * DN_C:(h + 1) * DN_C] for h in range(DN_H)] for r in rows]
        fwd, vjp = jax.vjp(functools.partial(_dn_intra_group, xs=xs), q_ref[...], k_ref[...], v_ref[...], bg_ref[...])
        dws, dqds, dkds, dats, dgls = [], [], [], [], []
        for j, r in enumerate(rows):
            dw, dqd, dkd, dat = [], [], [], []
            dgl = jnp.zeros((1, HP), f32)
            for h in range(DN_H):
                c = slice(h * DN_D, (h + 1) * DN_D)
                Sh, dSo = s_ref[j, h], ds_ref[j, h]
                Sb, dob, vb = _b(Sh), _b(do_ref[r, c]), vn_ref[r, c]
                dw.append(-_dot_nt(_b(dvn_ref[r, c]), Sb))
                dqd.append(_dot_nt(dob, Sb))
                dat.append(_dot_nt(dob, vb))
                dkd.append(_dot_nt(vb, _b(dSo)))
                dcd = jnp.sum(jnp.sum(Sh * dSo, axis=1, keepdims=True), axis=0, keepdims=True)
                dgl = dgl + jnp.where(lane1 == h, dcd * jnp.exp(_lane_pick(fwd[5][j], h)), 0.0)
            cat = lambda xs_: jnp.concatenate(xs_, axis=1)
            dws.append(cat(dw)), dqds.append(cat(dqd)), dkds.append(cat(dkd)), dats.append(cat(dat)), dgls.append(dgl)
        cat0 = lambda xs_: jnp.concatenate(xs_, axis=0)
        dq, dk, dv, dbg = vjp((dvn_ref[...], cat0(dws), cat0(dqds), cat0(dkds), cat0(dats), tuple(dgls)))
        dq_ref[...], dk_ref[...], dv_ref[...], dbg_ref[...] = dq, dk, dv, dbg

    row = lambda wd: pl.BlockSpec((R, wd), lambda s: (s, 0))
    st = pl.BlockSpec((G, DN_H, DN_D, DN_D), lambda s: (s, 0, 0, 0))
    return pl.pallas_call(
        body, name="dn_intra_bwd", grid=(NC // G,),
        in_specs=[row(W), row(W), row(W), row(HP), row(DN_H * DN_C), row(W), row(W), row(W), st, st],
        out_specs=[row(W), row(W), row(W), row(HP)],
        out_shape=[SDS((L, W), f32)] * 3 + [SDS((L, HP), f32)],
        compiler_params=_cp())(q, k, v, bg, xinv, do, vn, dvn, states, dstates)


def _dn_normgate(oraw, dz, wn):
    outs = []
    for h in range(DN_H):
        sl = slice(h * DN_D, (h + 1) * DN_D)
        z = dz[:, sl]
        outs.append(_rms(oraw[:, sl], wn) * (z * _sigmoid(z)))
    return jnp.concatenate(outs, axis=1)


def _mix_fwd(op, oraw, dz, ga, gb, h0, wn, wbf, wbd, wo):
    L, D = h0.shape
    TM = _pick(L, ROW_TILES)

    def body(op_ref, or_ref, dz_ref, ga_ref, gb_ref, h0_ref, wn_ref, wbf_ref, wbd_ref, wo_ref, h1_ref):
        pf = _dot(_b(op_ref[...]), wbf_ref[...])
        pd = _dot(_b(_dn_normgate(or_ref[...], dz_ref[...], wn_ref[...])), wbd_ref[...])
        y = _sigmoid(ga_ref[...]) * pf + _sigmoid(gb_ref[...]) * pd
        h1_ref[...] = h0_ref[...] + _dot(_b(y), wo_ref[...])

    row = lambda wd: pl.BlockSpec((TM, wd), lambda i: (i, 0))
    full = lambda a: pl.BlockSpec(a.shape, lambda i: (0, 0))
    return pl.pallas_call(
        body, name="mix_fwd", grid=(L // TM,),
        in_specs=[row(op.shape[1]), row(oraw.shape[1]), row(dz.shape[1]), row(D), row(D), row(D), full(wn), full(wbf), full(wbd), full(wo)],
        out_specs=row(D), out_shape=SDS((L, D), f32), compiler_params=_cp())(op, oraw, dz, ga, gb, h0, wn, wbf, wbd, wo)


def _mix_bwd(dh1, op, oraw, dz, ga, gb, wn, wbf, wbd, wo):
    L, D = dh1.shape
    TM = _pick(L, ROW_TILES)
    WF, WD = op.shape[1], oraw.shape[1]

    def body(dh1_ref, op_ref, or_ref, dz_ref, ga_ref, gb_ref, wn_ref, wbf_ref, wbd_ref, wo_ref,
             dop_ref, dor_ref, dg_ref, af_ref, ad_ref, dpf_ref, dpd_ref, y_ref, dmix_ref, acc_ref):
        @pl.when(pl.program_id(0) == 0)
        def _():
            acc_ref[...] = jnp.zeros_like(acc_ref)

        opv = op_ref[...]
        af = _b(opv)
        ad, vjp = jax.vjp(_dn_normgate, or_ref[...], dz_ref[...], wn_ref[...])
        adb = _b(ad)
        pf, pd = _dot(af, wbf_ref[...]), _dot(adb, wbd_ref[...])
        sa, sb = _sigmoid(ga_ref[...]), _sigmoid(gb_ref[...])
        dmix = _b(dh1_ref[...])
        dy = _dot_nt(dmix, wo_ref[...])
        dpf, dpd = _b(dy * sa), _b(dy * sb)
        dor, ddz, dwn = vjp(_dot_nt(dpd, wbd_ref[...]))
        dop = _dot_nt(dpf, wbf_ref[...])
        lane = _iota((TM, HP), 1)
        for h in range(WF // HP):
            c = slice(h * HP, (h + 1) * HP)
            delta = jnp.sum(jnp.where(lane < FOX_D, dop[:, c] * opv[:, c], 0.0), axis=1, keepdims=True)
            dop_ref[:, c] = _b(dop[:, c] - _spread3(jnp.broadcast_to(delta, (TM, HP)), lane, C_DELTA0))
        dor_ref[...] = dor
        dg_ref[:, 0:WD] = _b(ddz)
        dg_ref[:, WD:WD + D] = _b(dy * pf * sa * (1.0 - sa))
        dg_ref[:, WD + D:] = _b(dy * pd * sb * (1.0 - sb))
        af_ref[...], ad_ref[...], y_ref[...] = af, adb, _b(sa * pf + sb * pd)
        dpf_ref[...], dpd_ref[...], dmix_ref[...] = dpf, dpd, dmix
        acc_ref[0:1, :] += dwn

    row = lambda wd: pl.BlockSpec((TM, wd), lambda i: (i, 0))
    full = lambda a: pl.BlockSpec(a.shape, lambda i: (0, 0))
    return pl.pallas_call(
        body, name="mix_bwd", grid=(L // TM,),
        in_specs=[row(D), row(WF), row(WD), row(WD), row(D), row(D), full(wn), full(wbf), full(wbd), full(wo)],
        out_specs=[row(WF), row(WD), row(WD + 2 * D), row(WF), row(WD), row(D), row(D), row(D), row(D),
                   pl.BlockSpec((8, HP), lambda i: (0, 0))],
        out_shape=[SDS((L, WF), bf16), SDS((L, WD), f32), SDS((L, WD + 2 * D), bf16), SDS((L, WF), bf16), SDS((L, WD), bf16),
                   SDS((L, D), bf16), SDS((L, D), bf16), SDS((L, D), bf16), SDS((L, D), bf16), SDS((8, HP), f32)],
        compiler_params=_cp())(dh1, op, oraw, dz, ga, gb, wn, wbf, wbd, wo)


def _ffn_fwd_bwd(h1, tgt, w2, wf, wgt, wut, wd):
    L, D = h1.shape
    F = wd.shape[0]
    TM = _pick(L, FFN_TILES)
    gt, nb, _, tmaps = _shifted_blocks(TM)
    NC = F // FFN_CHUNK
    chunks = [slice(c * FFN_CHUNK, (c + 1) * FFN_CHUNK) for c in range(NC)]

    def body(h_ref, *refs):
        t_refs, (w2_ref, wf_ref, wg_hbm, wu_hbm, wd_hbm,
                 dh1_ref, xn_ref, dg_ref, du_ref, act_ref, dh2_ref, acc_ref, wg_v, wu_v, wd_v, sems, g_s, u_s) = refs[:nb], refs[nb:]
        i = pl.program_id(0)
        _load_once([(wg_hbm, wg_v), (wu_hbm, wu_v), (wd_hbm, wd_v)], sems)

        @pl.when(i == 0)
        def _():
            acc_ref[...] = jnp.zeros_like(acc_ref)

        h1v = h_ref[...]
        xn2, vjp2 = jax.vjp(_rms, h1v, w2_ref[...])
        xb = _b(xn2)
        up = lambda c: (_dot_nt(xb, wg_v[chunks[c], :]), _dot_nt(xb, wu_v[chunks[c], :]))
        h2, nxt = h1v, up(0)
        for c in range(NC):
            (g, u), nxt = nxt, (up(c + 1) if c + 1 < NC else None)
            g_s[:, chunks[c]], u_s[:, chunks[c]] = g, u
            ab = _b(g * _sigmoid(g) * u)
            act_ref[:, chunks[c]] = ab
            h2 = h2 + _dot(ab, wd_v[chunks[c], :])
        out, vjpf = jax.vjp(_rms, h2, wf_ref[...])
        valid = (i * TM + _iota((TM, 1), 0)) >= PREFIX
        diff = jnp.where(valid, out - jnp.concatenate([r[...] for r in t_refs], axis=0), 0.0)
        loss = 0.5 * jnp.sum(jnp.sum(diff * diff, axis=1, keepdims=True), axis=0, keepdims=True) / D
        dh2, dwf = vjpf(diff * (1.0 / D))
        dh2b = _b(dh2)
        down = lambda c: _dot_nt(dh2b, wd_v[chunks[c], :])
        dxn, nxt = None, down(0)
        for c in range(NC):
            dact, nxt = nxt, (down(c + 1) if c + 1 < NC else None)
            g, u = g_s[:, chunks[c]], u_s[:, chunks[c]]
            sg = _sigmoid(g)
            dgb = _b(dact * u * (sg * (1.0 + g * (1.0 - sg))))
            dub = _b(dact * (g * sg))
            dg_ref[:, chunks[c]], du_ref[:, chunks[c]] = dgb, dub
            part = _dot(dgb, wg_v[chunks[c], :]) + _dot(dub, wu_v[chunks[c], :])
            dxn = part if dxn is None else dxn + part
        dh1n, dw2 = vjp2(dxn)
        dh1_ref[...] = dh2 + dh1n
        xn_ref[...], dh2_ref[...] = xb, dh2b
        acc_ref[0:1, :] += dw2
        acc_ref[1:2, :] += dwf
        acc_ref[2:3, :] += jnp.broadcast_to(loss, (1, D))

    row = lambda wd_: pl.BlockSpec((TM, wd_), lambda i: (i, 0))
    vec = pl.BlockSpec((1, D), lambda i: (0, 0))
    anyspec = pl.BlockSpec(memory_space=pl.ANY)
    return pl.pallas_call(
        body, name="ffn_fwd_bwd", grid=(L // TM,),
        in_specs=[row(D)] + [pl.BlockSpec((gt, D), m) for m in tmaps] + [vec, vec, anyspec, anyspec, anyspec],
        out_specs=[row(D), row(D), row(F), row(F), row(F), row(D), pl.BlockSpec((8, D), lambda i: (0, 0))],
        out_shape=[SDS((L, D), f32), SDS((L, D), bf16), SDS((L, F), bf16), SDS((L, F), bf16), SDS((L, F), bf16), SDS((L, D), bf16),
                   SDS((8, D), f32)],
        scratch_shapes=[pltpu.VMEM((F, D), bf16), pltpu.VMEM((F, D), bf16), pltpu.VMEM((F, D), bf16), pltpu.SemaphoreType.DMA((3,)),
                        pltpu.VMEM((TM, F), f32), pltpu.VMEM((TM, F), f32)],
        compiler_params=_cp())(h1, *([tgt] * nb), w2, wf, wgt, wut, wd)


def _pad_lanes(v, n=HP):
    return jnp.pad(v.astype(f32), ((0, 0), (0, n - v.shape[1])))


def _pack_w_in(wt_full):
    D = wt_full.shape[1]
    FW, DW = FOX_H * FOX_D, DN_H * DN_D
    o = 0
    parts = {}
    for name, wd in (("fq", FW), ("fk", FW), ("fv", FW), ("fl", FOX_H), ("dn", 3 * DW), ("ba", 2 * DN_H), ("dz", DW), ("ga", D), ("gb", D)):
        parts[name] = wt_full[o:o + wd]
        o += wd
    assert o == wt_full.shape[0]
    heads = lambda w: jnp.pad(w.reshape(FOX_H, FOX_D, D), ((0, 0), (0, HP - FOX_D), (0, 0))).reshape(FOX_H * HP, D)
    small = lambda w: jnp.pad(w, ((0, SMALL_W - w.shape[0]), (0, 0)))
    packed = dict(fq=heads(parts["fq"]), fk=heads(parts["fk"]), fv=heads(parts["fv"]), sf=small(parts["fl"]), sd=small(parts["ba"]),
                  dn=parts["dn"], dz=parts["dz"], ga=parts["ga"], gb=parts["gb"])
    return jnp.concatenate([packed[name] for name, _, _, _ in _seg_layout(D)], axis=0)


def _unpack_w_in(groups, d_model):
    D = groups[0].shape[1]
    FW = FOX_H * FOX_D
    segs = {}
    for grp, g in zip(GROUPS, groups):
        o = 0
        for name, wd, _, sg in _seg_layout(d_model):
            if sg == grp:
                segs[name] = g[o:o + wd]
                o += wd
    heads = lambda g: g.reshape(FOX_H, HP, D)[:, :FOX_D].reshape(FW, D)
    return jnp.concatenate([heads(segs["fq"]), heads(segs["fk"]), heads(segs["fv"]), segs["sf"][:FOX_H], segs["dn"],
                            segs["sd"][:2 * DN_H], segs["dz"], segs["ga"], segs["gb"]], axis=0)


def _local_step(x, tgt, meta, w1, w_in_t, fbias, cw, alog, dtb, wn, w2, wf, late_shards):
    T, D = x.shape
    pre = jnp.concatenate([jnp.zeros((N_PAD, D), f32), meta], axis=0)
    wp = _pack_w_in(w_in_t)
    bias_p, alog_p, dt_p = _pad_lanes(fbias), _pad_lanes(jnp.pad(alog, ((0, 0), (DN_H, 0)))), _pad_lanes(jnp.pad(dtb, ((0, 0), (DN_H, 0))))

    (h0, xn, fq, fk, sf, fv, dn, sd, dz, ga, gb), g_mix = _in_proj(x, pre, w1, wp, [late_shards[n] for n in LATE_MIX])
    qa, ka, va = _fox_prep(fq, fk, fv, sf, bias_p)
    op, qb, g_ffn = _fox_fwd(qa, ka, va, [late_shards[n] for n in LATE_FFN])
    qn, kn, vn, bg = _dn_prep(dn, sd, cw, alog_p, dt_p)
    (u_dn, w_dn, qd_dn, kd_dn, at_dn, gl_dn, x_dn), g_down = _dn_intra(qn, kn, vn, bg, [late_shards[n] for n in LATE_DOWN])
    full = {n: _from_slabs(n, s) for n, s in zip(LATE_MIX + LATE_FFN + LATE_DOWN, tuple(g_mix) + tuple(g_ffn) + tuple(g_down))}
    wbf, wbd, wo, wgt, wut, wd = (full[n] for n in ("w_branch_fox", "w_branch_dn", "w_out", "w_ffn_gate", "w_ffn_up", "w_ffn_down"))
    wbf_p = jnp.pad(wbf.reshape(FOX_H, FOX_D, D), ((0, 0), (0, HP - FOX_D), (0, 0))).reshape(FOX_H * HP, D)
    oraw, vnew, states = _dn_scan(u_dn, w_dn, qd_dn, kd_dn, at_dn, gl_dn)
    h1 = _mix_fwd(op, oraw, dz, ga, gb, h0, wn, wbf_p, wbd, wo)

    dh1, xn2, dgate, dup, act, dh2, acc_f = _ffn_fwd_bwd(h1, tgt, w2, wf, wgt, wut, wd)
    g_wg, g_wu, g_wd = _matmul_tn(dgate, xn2, "dw_ffn_gate"), _matmul_tn(dup, xn2, "dw_ffn_up"), _matmul_tn(act, dh2, "dw_ffn_down")

    dop, dor, d_mix, af, ad, dpf, dpd, yb, dmix, acc_m = _mix_bwd(dh1, op, oraw, dz, ga, gb, wn, wbf_p, wbd, wo)
    g_wbf = _matmul_tn(af, dpf, "dw_branch_fox").reshape(FOX_H, HP, D)[:, :FOX_D].reshape(FOX_H * FOX_D, D)
    g_wbd, g_wo = _matmul_tn(ad, dpd, "dw_branch_dn"), _matmul_tn(yb, dmix, "dw_out")

    dvnew, dstates = _dn_scan_bwd(dor, w_dn, qd_dn, kd_dn, at_dn, gl_dn)
    dqn, dkn, dvn, dbg = _dn_intra_bwd(qn, kn, vn, bg, x_dn, dor, vnew, dvnew, states, dstates)
    d_dn, acc_cw, acc_p = _dn_prep_bwd(dn, sd, cw, alog_p, dt_p, dqn, dkn, dvn, dbg)
    g_late = dict(w_branch_fox=g_wbf, w_branch_dn=g_wbd, w_out=g_wo, w_ffn_gate=g_wg, w_ffn_up=g_wu, w_ffn_down=g_wd)
    (dqa, dka, d_fv), recv = _fox_bwd(qb, ka, va, dop, [_to_slabs(n, g_late[n]) for n in LATE])
    d_fox, acc_b = _fox_prep_bwd(dqa, dka, sf, bias_p)

    dgroups = [d_fox, d_fv, d_dn, d_mix]
    g_wp = [_matmul_tn(dg, xn, "dw_in_" + grp) for grp, dg in zip(GROUPS, dgroups)]
    dh0, acc_1, (recv_w_in,) = _in_proj_bwd(dgroups, wp, h0, w1, dh1, [_to_slabs("w_in", _unpack_w_in(g_wp, D))])
    recv = dict(zip(LATE, recv), w_in=recv_w_in)

    small = dict(loss=acc_f[2, 0:1], mix_norm_w=acc_1[0], fox_forget_bias=acc_b[0, :FOX_H], dn_a_log=acc_p[0, DN_H:2 * DN_H],
                 dn_dt_bias=acc_p[1, DN_H:2 * DN_H], dn_out_norm_w=acc_m[0], ffn_norm_w=acc_f[0], final_norm_w=acc_f[1],
                 meta_tokens=dh0[N_PAD:PREFIX].reshape(-1), dn_conv_w=acc_cw[:CONV_K].reshape(-1))
    return dh0[PREFIX:], small, recv


def _mesh_pos():
    x, y, c = lax.axis_index("x"), lax.axis_index("y"), lax.axis_index("c")
    return x, y, c, 4 * x + 2 * y + c


def _peer(x, y, c, m):
    flip = lambda v, on: 1 - v if on else v
    px, py, pc = flip(x, m & 4), flip(y, m & 2), flip(c, m & 1)
    return (px, py, pc), 4 * px + 2 * py + pc


def _exchange_sems(n):
    return [pltpu.SemaphoreType.DMA((n, N_DEV - 1)), pltpu.SemaphoreType.DMA((n, N_DEV - 1)), pltpu.SemaphoreType.DMA((n,))]


def _exchange_part(ins, outs, send_sems, recv_sems, loc_sems, gather, m, receive):
    x, y, c, me = _mesh_pos()
    src = lambda a, pid: ins[a] if gather else ins[a].at[pid]
    if m == 0:
        return [pltpu.make_async_copy(src(a, me), outs[a].at[me], loc_sems.at[a]) for a in range(len(ins))]
    peer, pid = _peer(x, y, c, m)
    return [pltpu.make_async_remote_copy(src_ref=src(a, pid), dst_ref=outs[a].at[pid if receive else me], send_sem=send_sems.at[a, m - 1],
                                         recv_sem=recv_sems.at[a, m - 1], device_id=peer, device_id_type=MESH) for a in range(len(ins))]


def _exchange_start(*refs, gather, when):
    @pl.when(when)
    def _():
        for m in range(N_DEV):
            for cp in _exchange_part(*refs, gather, m, receive=False):
                cp.start()


def _exchange_wait(*refs, gather, when):
    @pl.when(when)
    def _():
        for m in range(1, N_DEV):
            for cp in _exchange_part(*refs, gather, m, receive=True):
                cp.wait_recv()
        for m in list(range(1, N_DEV)) + [0]:
            for cp in _exchange_part(*refs, gather, m, receive=False):
                cp.wait() if m == 0 else cp.wait_send()


def _gather_two_level(arrays, name):
    n = len(arrays)

    def body(*refs):
        ins, outs, (send_sems, recv_sems, loc_sems) = refs[:n], refs[n:2 * n], refs[2 * n:]
        x, y, c, me = _mesh_pos()
        sib = (x, y, 1 - c)
        chips = [(1 - x, y), (x, 1 - y), (1 - x, 1 - y)]
        dev_id = lambda px, py, pc: 4 * px + 2 * py + pc

        def copy(a, k, block, to, own=False):
            return pltpu.make_async_remote_copy(src_ref=ins[a] if own else outs[a].at[block], dst_ref=outs[a].at[block],
                                                send_sem=send_sems.at[a, k], recv_sem=recv_sems.at[a, k], device_id=to, device_id_type=MESH)

        local = [pltpu.make_async_copy(ins[a], outs[a].at[me], loc_sems.at[a]) for a in range(n)]
        first = [copy(a, 0, me, sib, own=True) for a in range(n)]
        first += [copy(a, 1 + j, me, (*chip, c), own=True) for j, chip in enumerate(chips) for a in range(n)]
        for cp in local + first:
            cp.start()
        passed = []
        for j, chip in enumerate(chips):
            for a in range(n):
                copy(a, 1 + j, dev_id(*chip, c), sib).wait_recv()
                cp = copy(a, 4 + j, dev_id(*chip, c), sib)
                cp.start()
                passed.append(cp)
        for a in range(n):
            copy(a, 0, dev_id(x, y, 1 - c), sib).wait_recv()
        for j, chip in enumerate(chips):
            for a in range(n):
                copy(a, 4 + j, dev_id(*chip, 1 - c), sib).wait_recv()
        for cp in first + passed:
            cp.wait_send()
        for cp in local:
            cp.wait()

    anyspec = pl.BlockSpec(memory_space=pl.ANY)
    return pl.pallas_call(
        body, name=name, in_specs=[anyspec] * n, out_specs=[anyspec] * n,
        out_shape=[SDS((N_DEV,) + a.shape, a.dtype) for a in arrays],
        scratch_shapes=_exchange_sems(n))(*arrays)


def _all_reduce_small(v):
    R = v.shape[0]

    def body(v_ref, o_ref, gath, send_sems, recv_sems):
        x, y, c, me = _mesh_pos()
        gath[me] = v_ref[...]
        sends = []
        for m in range(1, N_DEV):
            peer, _ = _peer(x, y, c, m)
            cp = pltpu.make_async_remote_copy(src_ref=v_ref, dst_ref=gath.at[me], send_sem=send_sems.at[m - 1],
                                              recv_sem=recv_sems.at[m - 1], device_id=peer, device_id_type=MESH)
            cp.start()
            sends.append(cp)
        for m in range(1, N_DEV):
            peer, pid = _peer(x, y, c, m)
            pltpu.make_async_remote_copy(src_ref=v_ref, dst_ref=gath.at[pid], send_sem=send_sems.at[m - 1],
                                         recv_sem=recv_sems.at[m - 1], device_id=peer, device_id_type=MESH).wait_recv()
        for cp in sends:
            cp.wait_send()
        tot = gath[0]
        for d in range(1, N_DEV):
            tot = tot + gath[d]
        o_ref[...] = tot

    vm = pl.BlockSpec(memory_space=pltpu.VMEM)
    return pl.pallas_call(
        body, name="all_reduce_small", in_specs=[vm], out_specs=vm, out_shape=SDS((R, HP), f32),
        scratch_shapes=[pltpu.VMEM((N_DEV, R, HP), f32), pltpu.SemaphoreType.DMA((N_DEV - 1,)), pltpu.SemaphoreType.DMA((N_DEV - 1,))],
        )(v)


def _adamw_math(w, g, m, v):
    m = ADAM_B1 * m + (1.0 - ADAM_B1) * g
    v = ADAM_B2 * v + (1.0 - ADAM_B2) * (g * g)
    m_hat = m / (1.0 - ADAM_B1 ** ADAM_STEP)
    v_hat = v / (1.0 - ADAM_B2 ** ADAM_STEP)
    return -ADAM_LR * (m_hat / (jnp.sqrt(v_hat) + ADAM_EPS) + ADAM_WD * w), m, v


def _adamw(g, w, m, v, name):
    R, Cc = w.shape[-2:]
    if R <= ADAMW_WHOLE_ROWS or R % HP == 0:
        TR, TC = (R if R <= ADAMW_WHOLE_ROWS else _pick(R, ADAMW_TILES)), Cc
    else:
        TR, TC = R, _pick(Cc, ADAMW_TILES)
    slabs = g.ndim == 3
    lead = w.ndim - 2

    def body(g_ref, w_ref, m_ref, v_ref, go_ref, d_ref, mo_ref, vo_ref):
        if slabs:
            gs = g_ref[0].astype(f32)
            for k in range(1, N_DEV):
                gs = gs + g_ref[k].astype(f32)
        else:
            gs = g_ref[...]
        at = 0 if lead else Ellipsis
        d, mn, vn = _adamw_math(w_ref[at], gs, m_ref[at], v_ref[at])
        go_ref[at], d_ref[at], mo_ref[at], vo_ref[at] = gs, d, mn, vn

    grid = (R // TR, Cc // TC)
    blk = pl.BlockSpec((1,) * lead + (TR, TC), lambda i, j: (0,) * lead + (i, j))
    gblk = pl.BlockSpec((N_DEV, TR, TC), lambda i, j: (0, i, j)) if slabs else pl.BlockSpec((TR, TC), lambda i, j: (i, j))
    return pl.pallas_call(
        body, name=name, grid=grid, in_specs=[gblk, blk, blk, blk], out_specs=[blk] * 4,
        out_shape=[SDS(w.shape, f32)] * 4, compiler_params=_cp(2))(g, w, m, v)


def _adamw_small(gs, ws, ms, vs):
    k = len(ws)

    def body(*refs):
        ins, outs = refs[:4 * k], refs[4 * k:]
        for t in range(k):
            g, w, m, v = (ins[j * k + t][...] for j in range(4))
            outs[t][...], outs[k + t][...], outs[2 * k + t][...] = _adamw_math(w, g, m, v)

    whole = lambda a: pl.BlockSpec(a.shape, lambda i, nd=a.ndim: (0,) * nd)
    res = pl.pallas_call(
        body, name="adamw_small", grid=(1,), in_specs=[whole(a) for a in (*gs, *ws, *ms, *vs)],
        out_specs=[whole(a) for a in ws] * 3, out_shape=[SDS(a.shape, f32) for a in ws] * 3,
        compiler_params=_cp())(*gs, *ws, *ms, *vs)
    return [[res[j * k + t] for j in range(3)] for t in range(k)]


WEIGHTS = ("meta_tokens", "mix_norm_w", "w_in", "fox_forget_bias", "dn_conv_w", "dn_a_log", "dn_dt_bias", "dn_out_norm_w",
           "w_branch_fox", "w_branch_dn", "w_out", "ffn_norm_w", "w_ffn_gate", "w_ffn_up", "w_ffn_down", "final_norm_w")
COL_SHARDED = ("w_in", "w_branch_fox", "w_branch_dn", "w_ffn_gate", "w_ffn_up")
ROW_SHARDED = ("w_out", "w_ffn_down")
BIG = COL_SHARDED + ROW_SHARDED
LATE = tuple(n for n in BIG if n != "w_in")
LATE_MIX = ("w_branch_fox", "w_branch_dn", "w_out")
LATE_FFN = ("w_ffn_gate", "w_ffn_up")
LATE_DOWN = ("w_ffn_down",)
SMALL = tuple(n for n in WEIGHTS if n not in BIG)
TRANSPOSED = ("w_in", "w_ffn_gate", "w_ffn_up")


def _to_slabs(name, g):
    r, c = g.shape
    if name in COL_SHARDED and name not in TRANSPOSED:
        return _b(g.reshape(r, N_DEV, c // N_DEV).transpose(1, 0, 2))
    return _b(g.reshape(N_DEV, r // N_DEV, c))


def _from_slabs(name, s):
    n, r, c = s.shape
    if name in COL_SHARDED and name not in TRANSPOSED:
        return s.transpose(1, 0, 2).reshape(r, n * c)
    return s.reshape(n * r, c)


def kernel(x, meta_tokens, mix_norm_w, w_in, fox_forget_bias, dn_conv_w, dn_a_log, dn_dt_bias, dn_out_norm_w, w_branch_fox, w_branch_dn, w_out, ffn_norm_w, w_ffn_gate, w_ffn_up, w_ffn_down, final_norm_w, loss_target, m_meta_tokens, m_mix_norm_w, m_w_in, m_fox_forget_bias, m_dn_conv_w, m_dn_a_log, m_dn_dt_bias, m_dn_out_norm_w, m_w_branch_fox, m_w_branch_dn, m_w_out, m_ffn_norm_w, m_w_ffn_gate, m_w_ffn_up, m_w_ffn_down, m_final_norm_w, v_meta_tokens, v_mix_norm_w, v_w_in, v_fox_forget_bias, v_dn_conv_w, v_dn_a_log, v_dn_dt_bias, v_dn_out_norm_w, v_w_branch_fox, v_w_branch_dn, v_w_out, v_ffn_norm_w, v_w_ffn_gate, v_w_ffn_up, v_w_ffn_down, v_final_norm_w):
    w = dict(meta_tokens=meta_tokens, mix_norm_w=mix_norm_w, w_in=w_in, fox_forget_bias=fox_forget_bias, dn_conv_w=dn_conv_w, dn_a_log=dn_a_log, dn_dt_bias=dn_dt_bias, dn_out_norm_w=dn_out_norm_w, w_branch_fox=w_branch_fox, w_branch_dn=w_branch_dn, w_out=w_out, ffn_norm_w=ffn_norm_w, w_ffn_gate=w_ffn_gate, w_ffn_up=w_ffn_up, w_ffn_down=w_ffn_down, final_norm_w=final_norm_w)
    mom = dict(meta_tokens=m_meta_tokens, mix_norm_w=m_mix_norm_w, w_in=m_w_in, fox_forget_bias=m_fox_forget_bias, dn_conv_w=m_dn_conv_w, dn_a_log=m_dn_a_log, dn_dt_bias=m_dn_dt_bias, dn_out_norm_w=m_dn_out_norm_w, w_branch_fox=m_w_branch_fox, w_branch_dn=m_w_branch_dn, w_out=m_w_out, ffn_norm_w=m_ffn_norm_w, w_ffn_gate=m_w_ffn_gate, w_ffn_up=m_w_ffn_up, w_ffn_down=m_w_ffn_down, final_norm_w=m_final_norm_w)
    var = dict(meta_tokens=v_meta_tokens, mix_norm_w=v_mix_norm_w, w_in=v_w_in, fox_forget_bias=v_fox_forget_bias, dn_conv_w=v_dn_conv_w, dn_a_log=v_dn_a_log, dn_dt_bias=v_dn_dt_bias, dn_out_norm_w=v_dn_out_norm_w, w_branch_fox=v_w_branch_fox, w_branch_dn=v_w_branch_dn, w_out=v_w_out, ffn_norm_w=v_ffn_norm_w, w_ffn_gate=v_w_ffn_gate, w_ffn_up=v_w_ffn_up, w_ffn_down=v_w_ffn_down, final_norm_w=v_final_norm_w)
    two_d = lambda a: a.reshape(a.shape[-2:]) if a.ndim >= 2 else a.reshape(1, -1)
    me = 4 * lax.axis_index("x") + 2 * lax.axis_index("y") + lax.axis_index("c")
    for d in (w, mom, var):
        for n in TRANSPOSED:
            d[n] = jnp.swapaxes(d[n], -1, -2)

    g_in, g_meta, g_cw = _gather_two_level([_b(two_d(w["w_in"])), two_d(w["meta_tokens"]), two_d(w["dn_conv_w"])], "all_gather_early")
    meta = g_meta.transpose(1, 0, 2).reshape(N_META, -1)
    cw = g_cw.transpose(1, 0, 2).reshape(CONV_K, -1)

    gx, g_small, recv = _local_step(
        x[0], loss_target[0], meta, two_d(w["mix_norm_w"]), _from_slabs("w_in", g_in), two_d(w["fox_forget_bias"]), cw, two_d(w["dn_a_log"]),
        two_d(w["dn_dt_bias"]), two_d(w["dn_out_norm_w"]), two_d(w["ffn_norm_w"]), two_d(w["final_norm_w"]),
        {n: _b(two_d(w[n])) for n in LATE})

    order = ("loss",) + SMALL
    flat = jnp.concatenate([g_small[n].reshape(-1) for n in order])
    rows = -(-flat.shape[0] // (8 * HP)) * 8
    tot = _all_reduce_small(jnp.pad(flat, (0, rows * HP - flat.shape[0])).reshape(rows, HP)).reshape(-1)
    summed, o = {}, 0
    for n in order:
        k = g_small[n].shape[0]
        summed[n] = tot[o:o + k]
        o += k
    loss = summed["loss"][0]
    d_model = x.shape[-1]
    mcols, ccols = d_model // N_DEV, dn_conv_w.shape[-1]
    summed["meta_tokens"] = lax.dynamic_slice(summed["meta_tokens"].reshape(N_META, d_model), (0, me * mcols), (N_META, mcols)).reshape(-1)
    summed["dn_conv_w"] = lax.dynamic_slice(summed["dn_conv_w"].reshape(CONV_K, ccols * N_DEV), (0, me * ccols), (CONV_K, ccols)).reshape(-1)

    res = {}
    for n in BIG:
        res[n] = _adamw(recv[n], w[n], mom[n], var[n], "adamw_" + n)
        if n in TRANSPOSED:
            res[n] = [jnp.swapaxes(r, -1, -2) for r in res[n]]
    gs = [summed[n].reshape(two_d(w[n]).shape) for n in SMALL]
    sres = _adamw_small(gs, *[[two_d(d[n]) for n in SMALL] for d in (w, mom, var)])
    for n, g, r in zip(SMALL, gs, sres):
        res[n] = [a.reshape(w[n].shape) for a in (g, *r)]
    return (loss, gx[None], *[res[n][0] for n in WEIGHTS], *[res[n][1] for n in WEIGHTS], *[res[n][2] for n in WEIGHTS], *[res[n][3] for n in WEIGHTS])
```

```python
import functools
import math

import jax
import jax.numpy as jnp
from jax import lax
from jax.experimental import pallas as pl
from jax.experimental.pallas import tpu as pltpu

f32, bf16 = jnp.float32, jnp.bfloat16
HI = lax.Precision.HIGHEST
MESH = pl.DeviceIdType.MESH
SDS = jax.ShapeDtypeStruct

N_DEV = 8
N_META = 16
PREFIX = 128
N_PAD = PREFIX - N_META
FOX_H, FOX_D = 8, 64
DN_H, DN_D = 4, 128
DN_C = 64
CONV_K = 4
HP = 128
SMALL_W = 256
EPS = 1e-6
NEG = -1e30
C_Q0, C_K0 = 64, 67
LSE_COL = 64
LOG2E, LN2 = 1.4426950408889634, 0.6931471805599453
C_LSE0, C_DELTA0 = 70, 65

ADAM_LR, ADAM_B1, ADAM_B2, ADAM_EPS, ADAM_WD, ADAM_STEP = 0.001, 0.9, 0.999, 1e-08, 0.01, 10

VMEM_LIMIT_V7X = 56 * 1024 * 1024
ROW_TILES = (384, 128)
ATTN_TILES = (384, 128)
FFN_TILES = (384, 128)
FFN_CHUNK = 256
FOX_HEAD_GROUP = 4
FOX_HEAD_GROUP_FWD = 8
ADAMW_TILES = (256, 128)
ADAMW_WHOLE_ROWS = 512
MAX_WGRAD_BLOCK = 1408
DN_INTRA_GROUP = (6, 3, 2, 1)
DN_SCAN_GROUP = (6, 3, 2, 1)


def _pick(n, cands):
    for c in cands:
        if n % c == 0:
            return c
    raise ValueError(f"no tile of {cands} divides {n}")


def _cp(n_axes=1):
    return pltpu.CompilerParams(dimension_semantics=("arbitrary",) * n_axes, vmem_limit_bytes=VMEM_LIMIT_V7X)


def _b(x):
    return x.astype(bf16)


def _dot(a, b):
    return jnp.dot(a, b, preferred_element_type=f32)


def _dot_nt(a, b):
    return lax.dot_general(a, b, (((1,), (1,)), ((), ())), preferred_element_type=f32)


def _dot_tn(a, b):
    return lax.dot_general(a, b, (((0,), (0,)), ((), ())), preferred_element_type=f32)


def _dot_hi(a, b):
    return jnp.dot(a, b, preferred_element_type=f32, precision=HI)


def _iota(shape, dim):
    return lax.broadcasted_iota(jnp.int32, shape, dim)


def _rms(x, w):
    return x * lax.rsqrt(jnp.mean(x * x, axis=-1, keepdims=True) + EPS) * w


def _sigmoid(x):
    return jax.nn.sigmoid(x)


def _load_once(pairs, sems):
    @pl.when(pl.program_id(0) == 0)
    def _():
        cps = [pltpu.make_async_copy(src, dst, sems.at[k]) for k, (src, dst) in enumerate(pairs)]
        for cp in cps:
            cp.start()
        for cp in cps:
            cp.wait()


def _seg_layout(d_model):
    return (("fq", FOX_H * HP, bf16, "fox"), ("fk", FOX_H * HP, bf16, "fox"), ("sf", SMALL_W, f32, "fox"),
            ("fv", FOX_H * HP, bf16, "fv"),
            ("dn", 3 * DN_H * DN_D, f32, "dn"), ("sd", SMALL_W, f32, "dn"),
            ("dz", DN_H * DN_D, f32, "mix"), ("ga", d_model, f32, "mix"), ("gb", d_model, f32, "mix"))


GROUPS = ("fox", "fv", "dn", "mix")


def _shifted_blocks(TM):
    g = math.gcd(TM, PREFIX)
    nb, npre = TM // g, PREFIX // g
    assert nb >= npre
    return g, nb, npre, [lambda i, j=j: (jnp.maximum(i * nb + j - npre, 0), 0) for j in range(nb)]


def _in_proj(x, pre, w1, wpt, shards):
    T, D = x.shape
    L = T + PREFIX
    NP = wpt.shape[0]
    TM = _pick(L, ROW_TILES)
    NT = L // TM
    g, nb, npre, xmaps = _shifted_blocks(TM)
    segs = _seg_layout(D)
    ns, n = len(segs), len(shards)
    offs, o = [], 0
    for _, wd, _, _ in segs:
        offs.append(o)
        o += wd
    assert o == NP

    def body(*refs):
        x_refs, (pre_ref, w1_ref, wp_hbm), rest = refs[:nb], refs[nb:nb + 3], refs[nb + 3:]
        ins, h_ref, xn_ref, outs, gouts = rest[:n], rest[n], rest[n + 1], rest[n + 2:n + 2 + ns], rest[n + 2 + ns:2 * n + 2 + ns]
        wp_v, sems = rest[2 * n + 2 + ns:2 * n + 4 + ns]
        xsems = rest[2 * n + 4 + ns:]
        _load_once([(wp_hbm, wp_v)], sems)

        _exchange_start(ins, gouts, *xsems, gather=True, when=pl.program_id(0) == 0)

        first = pl.program_id(0) == 0
        h = jnp.concatenate([jnp.where(first, pre_ref[j * g:(j + 1) * g, :], r[...]) if j < npre else r[...]
                             for j, r in enumerate(x_refs)], axis=0)
        h_ref[...] = h
        xn = _b(_rms(h, w1_ref[...]))
        xn_ref[...] = xn
        for o_ref, off, (_, wd, _, _) in zip(outs, offs, segs):
            o_ref[...] = _dot_nt(xn, wp_v[off:off + wd, :]).astype(o_ref.dtype)

        _exchange_wait(ins, gouts, *xsems, gather=True, when=pl.program_id(0) == NT - 1)

    row = lambda wd: pl.BlockSpec((TM, wd), lambda i: (i, 0))
    anyspec = pl.BlockSpec(memory_space=pl.ANY)
    res = pl.pallas_call(
        body, name="in_proj", grid=(NT,),
        in_specs=[pl.BlockSpec((g, D), m) for m in xmaps] + [pl.BlockSpec((PREFIX, D), lambda i: (0, 0)), pl.BlockSpec((1, D), lambda i: (0, 0)), anyspec]
        + [anyspec] * n,
        out_specs=[row(D), row(D)] + [row(wd) for _, wd, _, _ in segs] + [anyspec] * n,
        out_shape=[SDS((L, D), f32), SDS((L, D), bf16)] + [SDS((L, wd), dt) for _, wd, dt, _ in segs]
        + [SDS((N_DEV,) + a.shape, a.dtype) for a in shards],
        scratch_shapes=[pltpu.VMEM((NP, D), bf16), pltpu.SemaphoreType.DMA((1,))] + _exchange_sems(n),
        compiler_params=_cp())(*([x] * nb), pre, w1, wpt, *shards)
    return res[:2 + ns], res[2 + ns:]


def _in_proj_bwd(dgroups, wpt, h0, w1, dh1, slabs):
    L, D = h0.shape
    NP = wpt.shape[0]
    TM = _pick(L, ROW_TILES)
    NT = L // TM
    widths = [g.shape[1] for g in dgroups]
    assert sum(widths) == NP
    ng, n = len(dgroups), len(slabs)

    def body(*refs):
        dg_refs, (wp_hbm, h_ref, w1_ref, dh1_ref) = refs[:ng], refs[ng:ng + 4]
        ins, (dh0_ref, acc_ref), outs = refs[ng + 4:ng + 4 + n], refs[ng + 4 + n:ng + 6 + n], refs[ng + 6 + n:ng + 6 + 2 * n]
        wp_v, sems = refs[ng + 6 + 2 * n:ng + 8 + 2 * n]
        xsems = refs[ng + 8 + 2 * n:]
        _load_once([(wp_hbm, wp_v)], sems)

        @pl.when(pl.program_id(0) == 0)
        def _():
            acc_ref[...] = jnp.zeros_like(acc_ref)

        _exchange_start(ins, outs, *xsems, gather=False, when=pl.program_id(0) == 0)

        dxn, off = None, 0
        for g_ref, wd in zip(dg_refs, widths):
            part = _dot(g_ref[...], wp_v[off:off + wd, :])
            dxn = part if dxn is None else dxn + part
            off += wd
        _, vjp = jax.vjp(_rms, h_ref[...], w1_ref[...])
        dh0n, dw1 = vjp(dxn)
        dh0_ref[...] = dh1_ref[...] + dh0n
        acc_ref[0:1, :] += dw1

        _exchange_wait(ins, outs, *xsems, gather=False, when=pl.program_id(0) == NT - 1)

    row = lambda wd: pl.BlockSpec((TM, wd), lambda i: (i, 0))
    anyspec = pl.BlockSpec(memory_space=pl.ANY)
    res = pl.pallas_call(
        body, name="in_proj_bwd", grid=(NT,),
        in_specs=[row(wd) for wd in widths] + [anyspec, row(D), pl.BlockSpec((1, D), lambda i: (0, 0)), row(D)] + [anyspec] * n,
        out_specs=[row(D), pl.BlockSpec((8, D), lambda i: (0, 0))] + [anyspec] * n,
        out_shape=[SDS((L, D), f32), SDS((8, D), f32)] + [SDS(a.shape, a.dtype) for a in slabs],
        scratch_shapes=[pltpu.VMEM((NP, D), bf16), pltpu.SemaphoreType.DMA((1,))] + _exchange_sems(n),
        compiler_params=_cp())(*dgroups, wpt, h0, w1, dh1, *slabs)
    return res[0], res[1], res[2:]


def _matmul_tn(a, b, name):
    L, R = a.shape
    C = b.shape[1]
    br = max(k for k in range(HP, MAX_WGRAD_BLOCK + 1, HP) if R % k == 0)

    def body(a_ref, b_ref, o_ref):
        o_ref[...] = _b(_dot_tn(a_ref[...], b_ref[...]))

    return pl.pallas_call(
        body, name=name, grid=(R // br,),
        in_specs=[pl.BlockSpec((L, br), lambda r: (0, r)), pl.BlockSpec((L, C), lambda r: (0, 0))],
        out_specs=pl.BlockSpec((br, C), lambda r: (r, 0)), out_shape=SDS((R, C), bf16), compiler_params=_cp())(a, b)


def _fox_prep(fq, fk, fv, sf, bias_p):
    L = fq.shape[0]
    T = HP
    NT = L // T
    W = FOX_H * HP

    def body(fq_ref, fk_ref, fv_ref, sf_ref, b_ref, qa_ref, ka_ref, va_ref, carry):
        @pl.when(pl.program_id(0) == 0)
        def _():
            carry[...] = jnp.zeros_like(carry)

        lane, row = _iota((T, HP), 1), _iota((T, HP), 0)
        logf = jnp.where(lane < FOX_H, jax.nn.log_sigmoid(sf_ref[...] + b_ref[...]), 0.0)
        c = _dot_hi((row >= lane).astype(f32), logf) + carry[...]
        carry[...] = jnp.sum(jnp.where(row == T - 1, c, 0.0), axis=0, keepdims=True)
        ones_q = jnp.where((lane >= C_K0) & (lane < C_K0 + 3), 1.0, 0.0)
        ones_k = jnp.where(((lane >= C_Q0) & (lane < C_Q0 + 3)) | ((lane >= C_LSE0) & (lane < C_LSE0 + 3)), 1.0, 0.0)
        ones_v = _b(jnp.where((lane >= LSE_COL) & (lane < C_DELTA0 + 3), 1.0, 0.0))
        for h in range(FOX_H):
            ch = jnp.broadcast_to(jnp.sum(jnp.where(lane == h, c, 0.0), axis=1, keepdims=True), (T, HP)) * LOG2E
            c1 = _b(ch).astype(f32)
            c2 = _b(ch - c1).astype(f32)
            c3 = _b(ch - c1 - c2).astype(f32)
            cq = jnp.where(lane == C_Q0, c1, 0.0) + jnp.where(lane == C_Q0 + 1, c2, 0.0) + jnp.where(lane == C_Q0 + 2, c3, 0.0)
            ck = jnp.where(lane == C_K0, c1, 0.0) + jnp.where(lane == C_K0 + 1, c2, 0.0) + jnp.where(lane == C_K0 + 2, c3, 0.0)
            q = fq_ref[:, h * HP:(h + 1) * HP].astype(f32) * (FOX_D ** -0.5 * LOG2E)
            k = fk_ref[:, h * HP:(h + 1) * HP].astype(f32)
            qa_ref[h] = _b(q + cq + ones_q)
            ka_ref[h] = _b(k + ones_k - ck)
            va_ref[:, h * HP:(h + 1) * HP] = fv_ref[:, h * HP:(h + 1) * HP] + ones_v

    wide = pl.BlockSpec((T, W), lambda i: (i, 0))
    return pl.pallas_call(
        body, name="fox_prep", grid=(NT,),
        in_specs=[wide, wide, wide, pl.BlockSpec((T, HP), lambda i: (i, 0)), pl.BlockSpec((1, HP), lambda i: (0, 0))],
        out_specs=[pl.BlockSpec((FOX_H, T, HP), lambda i: (0, i, 0))] * 2 + [wide],
        out_shape=[SDS((FOX_H, L, HP), bf16)] * 2 + [SDS((L, W), bf16)],
        scratch_shapes=[pltpu.VMEM((1, HP), f32)], compiler_params=_cp())(fq, fk, fv, sf, bias_p)


def _fox_prep_bwd(dqa, dka, sf, bias_p):
    L = sf.shape[0]
    T = HP
    NT = L // T
    rev = lambda i: (NT - 1 - i, 0)

    W = FOX_H * HP

    def body(dq_ref, dk_ref, sf_ref, b_ref, dg_ref, db_ref, carry):
        @pl.when(pl.program_id(0) == 0)
        def _():
            carry[...] = jnp.zeros_like(carry)
            db_ref[...] = jnp.zeros_like(db_ref)

        dq, dk = dq_ref[...], dk_ref[...]
        dg_ref[:, 0:W] = _b(dq * (FOX_D ** -0.5))
        dg_ref[:, W:2 * W] = _b(dk * LN2)
        lane, row = _iota((T, HP), 1), _iota((T, HP), 0)
        dc = jnp.zeros((T, HP), f32)
        for h in range(FOX_H):
            col = jnp.sum(jnp.where(lane == C_Q0, dq[:, h * HP:(h + 1) * HP], 0.0)
                          - jnp.where(lane == C_K0, dk[:, h * HP:(h + 1) * HP], 0.0), axis=1, keepdims=True)
            dc = dc + jnp.where(lane == h, col, 0.0)
        dl = _dot_hi((row <= lane).astype(f32), dc) + carry[...]
        carry[...] = jnp.sum(jnp.where(row == 0, dl, 0.0), axis=0, keepdims=True)
        dx = jnp.where(lane < FOX_H, dl * _sigmoid(-(sf_ref[...] + b_ref[...])), 0.0)
        dg_ref[:, 2 * W:2 * W + HP] = _b(dx)
        dg_ref[:, 2 * W + HP:] = jnp.zeros((T, SMALL_W - HP), bf16)
        db_ref[0:1, :] += jnp.sum(dx, axis=0, keepdims=True)

    return pl.pallas_call(
        body, name="fox_prep_bwd", grid=(NT,),
        in_specs=[pl.BlockSpec((T, W), rev), pl.BlockSpec((T, W), rev), pl.BlockSpec((T, HP), rev), pl.BlockSpec((1, HP), lambda i: (0, 0))],
        out_specs=[pl.BlockSpec((T, 2 * W + SMALL_W), rev), pl.BlockSpec((8, HP), lambda i: (0, 0))],
        out_shape=[SDS((L, 2 * W + SMALL_W), bf16), SDS((8, HP), f32)],
        scratch_shapes=[pltpu.VMEM((1, HP), f32)], compiler_params=_cp())(dqa, dka, sf, bias_p)


def _tile_start(j, T):
    return j * T if isinstance(j, int) else pl.multiple_of(j * T, T)


def _spread3(x, lane, col0):
    x1 = _b(x).astype(f32)
    x2 = _b(x - x1).astype(f32)
    x3 = _b(x - x1 - x2).astype(f32)
    return jnp.where(lane == col0, x1, 0.0) + jnp.where(lane == col0 + 1, x2, 0.0) + jnp.where(lane == col0 + 2, x3, 0.0)


def _fox_fwd(qa, ka, fv, shards):
    L = qa.shape[1]
    TQ = TK = _pick(L, ATTN_TILES)
    NQ = L // TQ
    n = len(shards)
    HG = FOX_HEAD_GROUP_FWD

    def body(q_ref, k_ref, v_ref, *rest):
        ins, o_ref, qb_ref, outs, sems = rest[:n], rest[n], rest[n + 1], rest[n + 2:2 * n + 2], rest[2 * n + 2:]
        h, i = pl.program_id(0), pl.program_id(1)

        _exchange_start(ins, outs, *sems, gather=True, when=(h == 0) & (i == 0))

        qs = [q_ref[a] for a in range(HG)]
        rowg = i * TQ + _iota((TQ, TK), 0)
        colb = _iota((TQ, TK), 1)

        def step(j, carry, masked):
            ms, accs = carry
            k0 = _tile_start(j, TK)
            ss = [_dot_nt(qs[a], k_ref[a, pl.ds(k0, TK), :]) for a in range(HG)]
            if masked:
                colg = colb + j * TK
                keep = (colg <= rowg) & (colg >= N_PAD)
                ss = [jnp.where(keep, s, NEG) for s in ss]
            m_new = [jnp.maximum(m, jnp.max(s, axis=1, keepdims=True)) for m, s in zip(ms, ss)]
            ps = [_b(jnp.exp2(s - m)) for s, m in zip(ss, m_new)]
            alphas = [jnp.exp2(m - mn) for m, mn in zip(ms, m_new)]
            accs = [al * acc + _dot(p, v_ref[pl.ds(k0, TK), a * HP:(a + 1) * HP]) for a, (al, acc, p) in enumerate(zip(alphas, accs, ps))]
            return m_new, accs

        init = ([jnp.full((TQ, 1), NEG, f32)] * HG, [jnp.zeros((TQ, HP), f32)] * HG)
        carry = step(0, init, True)
        carry = lax.fori_loop(1, i, functools.partial(step, masked=False), carry)
        ms, accs = lax.fori_loop(jnp.maximum(i, 1), i + 1, functools.partial(step, masked=True), carry)
        lane = _iota((TQ, HP), 1)
        for a in range(HG):
            l = jnp.sum(jnp.where(lane == LSE_COL, accs[a], 0.0), axis=1, keepdims=True)
            lse = ms[a] + jnp.log2(l)
            o_ref[:, a * HP:(a + 1) * HP] = jnp.where(lane == LSE_COL, lse, accs[a] / l)
            qb_ref[a] = _b(qs[a].astype(f32) - _spread3(jnp.broadcast_to(lse, (TQ, HP)), lane, C_LSE0))

        _exchange_wait(ins, outs, *sems, gather=True, when=(h == FOX_H // HG - 1) & (i == NQ - 1))

    anyspec = pl.BlockSpec(memory_space=pl.ANY)
    qtile = pl.BlockSpec((HG, TQ, HP), lambda h, i: (h, i, 0))
    res = pl.pallas_call(
        body, name="fox_fwd", grid=(FOX_H // HG, NQ),
        in_specs=[qtile, pl.BlockSpec((HG, L, HP), lambda h, i: (h, 0, 0)), pl.BlockSpec((L, HG * HP), lambda h, i: (0, h))] + [anyspec] * n,
        out_specs=[pl.BlockSpec((TQ, HG * HP), lambda h, i: (i, h)), qtile] + [anyspec] * n,
        out_shape=[SDS((L, FOX_H * HP), f32), SDS(qa.shape, bf16)] + [SDS((N_DEV,) + a.shape, a.dtype) for a in shards],
        scratch_shapes=_exchange_sems(n), compiler_params=_cp(2))(qa, ka, fv, *shards)
    return res[0], res[1], res[2:]


def _fox_bwd(qb, ka, va, dob, slabs):
    L = qb.shape[1]
    TQ = TK = _pick(L, ATTN_TILES)
    NQ = L // TQ
    n = len(slabs)
    HG = FOX_HEAD_GROUP

    def body(q_ref, k_ref, v_ref, do_ref, *rest):
        ins, (dq_ref, dk_ref, dv_ref), outs, sems = rest[:n], rest[n:n + 3], rest[n + 3:2 * n + 3], rest[2 * n + 3:]
        h, j = pl.program_id(0), pl.program_id(1)
        cols = [slice(a * HP, (a + 1) * HP) for a in range(HG)]

        _exchange_start(ins, outs, *sems, gather=False, when=(h == 0) & (j == 0))

        @pl.when(j == 0)
        def _():
            dq_ref[...] = jnp.zeros_like(dq_ref)

        kts = [k_ref[a] for a in range(HG)]
        vts = [v_ref[:, cols[a]] for a in range(HG)]
        colg = j * TK + _iota((TQ, TK), 1)
        rowb = _iota((TQ, TK), 0)

        def step(i, carry, masked):
            dks, dvs = carry
            r0 = _tile_start(i, TQ)
            rows = pl.ds(r0, TQ)
            qs = [q_ref[a, rows, :] for a in range(HG)]
            ps = [jnp.exp2(_dot_nt(q, kt)) for q, kt in zip(qs, kts)]
            if masked:
                keep = (colg <= rowb + i * TQ) & (colg >= N_PAD)
                ps = [jnp.where(keep, p, 0.0) for p in ps]
            dobs = [do_ref[rows, cols[a]] for a in range(HG)]
            dvs = [dv + _dot_tn(dob, _b(p)) for dv, p, dob in zip(dvs, ps, dobs)]
            dss = [_b(p * _dot_nt(dob, vt)) for p, dob, vt in zip(ps, dobs, vts)]
            for a in range(HG):
                dq_ref[rows, cols[a]] += _dot(dss[a], kts[a])
            dks = [dk + _dot_tn(q, ds) for dk, ds, q in zip(dks, dss, qs)]
            return dks, dvs

        zeros = [jnp.zeros((HP, TK), f32)] * HG
        carry = step(j, (zeros, zeros), True)
        split = jnp.where(j == 0, NQ, j + 1)
        carry = lax.fori_loop(j + 1, split, functools.partial(step, masked=True), carry)
        dks, dvs = lax.fori_loop(split, NQ, functools.partial(step, masked=False), carry)
        for a in range(HG):
            dk_ref[:, cols[a]] = dks[a].T
            dv_ref[:, cols[a]] = _b(dvs[a].T)

        _exchange_wait(ins, outs, *sems, gather=False, when=(h == FOX_H // HG - 1) & (j == NQ - 1))

    head = pl.BlockSpec((L, HG * HP), lambda h, j: (0, h))
    tile = pl.BlockSpec((TK, HG * HP), lambda h, j: (j, h))
    anyspec = pl.BlockSpec(memory_space=pl.ANY)
    res = pl.pallas_call(
        body, name="fox_bwd", grid=(FOX_H // HG, L // TK),
        in_specs=[pl.BlockSpec((HG, L, HP), lambda h, j: (h, 0, 0)), pl.BlockSpec((HG, TK, HP), lambda h, j: (h, j, 0)), tile, head]
        + [anyspec] * n,
        out_specs=[head, tile, tile] + [anyspec] * n,
        out_shape=[SDS((L, FOX_H * HP), f32), SDS((L, FOX_H * HP), f32), SDS((L, FOX_H * HP), bf16)] + [SDS(a.shape, a.dtype) for a in slabs],
        scratch_shapes=_exchange_sems(n), compiler_params=_cp(2))(qb, ka, va, dob, *slabs)
    return res[:3], res[3:]


def _dn_post(y, sd, alog_p, dt_p, valid):
    a = y * _sigmoid(y)
    W = DN_H * DN_D
    heads = []
    for part, scale in ((0, DN_D ** -0.5), (1, 1.0)):
        for h in range(DN_H):
            xh = a[:, part * W + h * DN_D:part * W + (h + 1) * DN_D]
            heads.append(xh * lax.rsqrt(jnp.sum(xh * xh, axis=-1, keepdims=True) + EPS) * scale)
    q = jnp.concatenate(heads[:DN_H], axis=1)
    k = jnp.concatenate(heads[DN_H:], axis=1)
    v = a[:, 2 * W:3 * W]
    lane = _iota(sd.shape, 1)
    beta = _sigmoid(sd) * valid
    g = -jnp.exp(alog_p) * jax.nn.softplus(sd + dt_p) * valid
    bg = jnp.where(lane < DN_H, beta, jnp.where(lane < 2 * DN_H, g, 0.0))
    return q, k, v, bg


def _conv_fwd(ext_ref, cw_ref, TM):
    y = cw_ref[0:1, :] * ext_ref[8 - (CONV_K - 1):8 - (CONV_K - 1) + TM, :]
    for i in range(1, CONV_K):
        o = 8 - (CONV_K - 1) + i
        y = y + cw_ref[i:i + 1, :] * ext_ref[o:o + TM, :]
    return y


def _dn_prep(dn, sd, cw, alog_p, dt_p):
    L, W3 = dn.shape
    TM = _pick(L, ROW_TILES)
    W = DN_H * DN_D

    def body(dn_ref, halo_ref, sd_ref, cw_ref, al_ref, dt_ref, q_ref, k_ref, v_ref, bg_ref, ext):
        i = pl.program_id(0)
        ext[0:8, :] = jnp.where(i == 0, 0.0, halo_ref[...])
        ext[8:, :] = dn_ref[...]
        y = _conv_fwd(ext, cw_ref, TM)
        valid = ((i * TM + _iota((TM, 1), 0)) >= N_PAD).astype(f32)
        q, k, v, bg = _dn_post(y, sd_ref[...], al_ref[...], dt_ref[...], valid)
        q_ref[...], k_ref[...], v_ref[...], bg_ref[...] = q, k, v, bg

    row = lambda wd: pl.BlockSpec((TM, wd), lambda i: (i, 0))
    vec = pl.BlockSpec((1, HP), lambda i: (0, 0))
    return pl.pallas_call(
        body, name="dn_prep", grid=(L // TM,),
        in_specs=[row(W3), pl.BlockSpec((8, W3), lambda i: (jnp.maximum(i * (TM // 8) - 1, 0), 0)), row(HP),
                  pl.BlockSpec((CONV_K, W3), lambda i: (0, 0)), vec, vec],
        out_specs=[row(W), row(W), row(W), row(HP)],
        out_shape=[SDS((L, W), f32)] * 3 + [SDS((L, HP), f32)],
        scratch_shapes=[pltpu.VMEM((TM + 8, W3), f32)], compiler_params=_cp())(dn, dn, sd, cw, alog_p, dt_p)


def _dn_prep_bwd(dn, sd, cw, alog_p, dt_p, dq, dk, dv, dbg):
    L, W3 = dn.shape
    TM = _pick(L, ROW_TILES)
    NT = L // TM
    W = DN_H * DN_D

    def body(dn_ref, halo_ref, sd_ref, cw_ref, al_ref, dt_ref, dq_ref, dk_ref, dv_ref, dbg_ref,
             dg_ref, dcw_ref, dp_ref, ext, dyp, carry):
        i = pl.program_id(0)
        t = NT - 1 - i

        @pl.when(i == 0)
        def _():
            carry[...] = jnp.zeros_like(carry)
            dcw_ref[...] = jnp.zeros_like(dcw_ref)
            dp_ref[...] = jnp.zeros_like(dp_ref)
            dyp[...] = jnp.zeros_like(dyp)

        ext[0:8, :] = jnp.where(t == 0, 0.0, halo_ref[...])
        ext[8:, :] = dn_ref[...]
        y = _conv_fwd(ext, cw_ref, TM)
        valid = ((t * TM + _iota((TM, 1), 0)) >= N_PAD).astype(f32)
        _, vjp = jax.vjp(functools.partial(_dn_post, valid=valid), y, sd_ref[...], al_ref[...], dt_ref[...])
        dy, dsd, dal, ddt = vjp((dq_ref[...], dk_ref[...], dv_ref[...], dbg_ref[...]))
        dg_ref[:, W3:W3 + HP] = _b(dsd)
        dg_ref[:, W3 + HP:] = jnp.zeros((TM, SMALL_W - HP), bf16)
        dp_ref[0:1, :] += dal
        dp_ref[1:2, :] += ddt
        dyp[8:8 + TM, :] = dy
        o0 = CONV_K - 1
        dext = cw_ref[0:1, :] * dyp[o0:o0 + TM + 8, :]
        for k in range(1, CONV_K):
            dext = dext + cw_ref[k:k + 1, :] * dyp[o0 - k:o0 - k + TM + 8, :]
        for k in range(CONV_K):
            o = 8 - (CONV_K - 1) + k
            dcw_ref[k:k + 1, :] += jnp.sum(dy * ext[o:o + TM, :], axis=0, keepdims=True)
        dg_ref[:, 0:W3] = _b(jnp.concatenate([dext[8:TM, :], dext[TM:TM + 8, :] + carry[...]], axis=0))
        carry[...] = dext[0:8, :]

    row = lambda wd: pl.BlockSpec((TM, wd), lambda i: (NT - 1 - i, 0))
    vec = pl.BlockSpec((1, HP), lambda i: (0, 0))
    return pl.pallas_call(
        body, name="dn_prep_bwd", grid=(NT,),
        in_specs=[row(W3), pl.BlockSpec((8, W3), lambda i: (jnp.maximum((NT - 1 - i) * (TM // 8) - 1, 0), 0)), row(HP),
                  pl.BlockSpec((CONV_K, W3), lambda i: (0, 0)), vec, vec, row(W), row(W), row(W), row(HP)],
        out_specs=[row(W3 + SMALL_W), pl.BlockSpec((8, W3), lambda i: (0, 0)), pl.BlockSpec((8, HP), lambda i: (0, 0))],
        out_shape=[SDS((L, W3 + SMALL_W), bf16), SDS((8, W3), f32), SDS((8, HP), f32)],
        scratch_shapes=[pltpu.VMEM((TM + 8, W3), f32), pltpu.VMEM((TM + 16, W3), f32), pltpu.VMEM((8, W3), f32)],
        compiler_params=_cp())(dn, dn, sd, cw, alog_p, dt_p, dq, dk, dv, dbg)


def _split2(x):
    hi = _b(x)
    return hi, _b(x - hi.astype(f32))


def _split3(x):
    hi = _b(x)
    r = x - hi.astype(f32)
    mid = _b(r)
    return hi, mid, _b(r - mid.astype(f32))


def _x3(a, b, dot):
    (a1, a2), (b1, b2) = _split2(a), _split2(b)
    return dot(a1, b1) + (dot(a1, b2) + dot(a2, b1))


@jax.custom_vjp
def _dot_x3(a, b):
    return _x3(a, b, _dot)


_dot_x3.defvjp(lambda a, b: (_x3(a, b, _dot), (a, b)), lambda res, g: (_x3(g, res[1], _dot_nt), _x3(res[0], g, _dot_tn)))


def _exact3(m, x, dot):
    x1, x2, x3 = _split3(x)
    return dot(m, x1) + (dot(m, x2) + dot(m, x3))


def _tri_ones(C, lower):
    row, col = _iota((C, C), 0), _iota((C, C), 1)
    return _b(((row >= col) if lower else (row <= col)).astype(f32))


@jax.custom_vjp
def _chunk_cumsum(x):
    return _exact3(_tri_ones(x.shape[0], True), x, _dot)


_chunk_cumsum.defvjp(lambda x: (_exact3(_tri_ones(x.shape[0], True), x, _dot), None),
                     lambda _, g: (_exact3(_tri_ones(g.shape[0], False), g, _dot),))


def _mxu_transpose(x):
    C = x.shape[0]
    eye = _b((_iota((C, C), 0) == _iota((C, C), 1)).astype(f32))
    return _exact3(eye, x, lambda m, part: _dot_tn(part, m))


@jax.custom_vjp
def _transpose_exact(x):
    return _mxu_transpose(x)


_transpose_exact.defvjp(lambda x: (_mxu_transpose(x), None), lambda _, g: (_mxu_transpose(g),))


def _unit_lower_inverses(lows):
    C = lows[0].shape[0]
    P = jnp.stack(lows)
    X = (_iota((C, C), 0) == _iota((C, C), 1)).astype(f32)[None] - P
    bdot = functools.partial(_x3, dot=lambda a, b: jnp.einsum("bij,bjk->bik", a, b, preferred_element_type=f32))
    for _ in range(5):
        P = bdot(P, P)
        X = X + bdot(X, P)
    return [X[i] for i in range(len(lows))]


@jax.custom_vjp
def _inverse_given(low, X):
    return X


def _inverse_given_bwd(X, g):
    return -_x3(_x3(X, g, _dot_tn), X, _dot_nt), jnp.zeros_like(X)


_inverse_given.defvjp(lambda low, X: (X, X), _inverse_given_bwd)


def _dn_intra_pre(q, k, v, bg):
    C = DN_C
    row, col = _iota((C, C), 0), _iota((C, C), 1)
    tri = row >= col
    G = _chunk_cumsum(bg)
    GT = _transpose_exact(G)
    lane = _iota((C, HP), 1)
    rowt = _iota((HP, C), 0)
    last = _iota((C, 1), 0) == C - 1
    heads = []
    for h in range(DN_H):
        beta = jnp.sum(jnp.where(lane == h, bg, 0.0), axis=1, keepdims=True)
        gcol = jnp.sum(jnp.where(lane == DN_H + h, G, 0.0), axis=1, keepdims=True)
        grow = jnp.sum(jnp.where(rowt == DN_H + h, GT, 0.0), axis=0, keepdims=True)
        glast = jnp.sum(jnp.where(last, gcol, 0.0), axis=0, keepdims=True)
        decay = jnp.exp(jnp.where(tri, gcol - grow, NEG))
        qh, kh, vh = (t[:, h * DN_D:(h + 1) * DN_D] for t in (q, k, v))
        kb = kh * beta
        low = jnp.where(row > col, _dot_nt(_b(kb), _b(kh)) * decay, 0.0)
        heads.append((beta, gcol, glast, decay, qh, kh, vh, kb, low))
    return heads


def _dn_intra_post(heads, xs):
    lane1 = _iota((1, HP), 1)
    us, ws, qds, kds, attns = [], [], [], [], []
    glrow = jnp.zeros((1, HP), f32)
    for h, ((beta, gcol, glast, decay, qh, kh, vh, kb, _), X) in enumerate(zip(heads, xs)):
        eg = jnp.exp(gcol)
        us.append(_dot_x3(X, vh * beta))
        ws.append(_dot_x3(X, kb * eg))
        attns.append(_dot_nt(_b(qh), _b(kh)) * decay)
        qds.append(qh * eg)
        kds.append(kh * jnp.exp(glast - gcol))
        glrow = glrow + jnp.where(lane1 == h, glast, 0.0)
    cat = lambda xs_: jnp.concatenate(xs_, axis=1)
    return cat(us), cat(ws), cat(qds), cat(kds), cat(attns), glrow, cat(list(xs))


def _dn_intra_group(q, k, v, bg, xs):
    G = q.shape[0] // DN_C
    rows = [slice(j * DN_C, (j + 1) * DN_C) for j in range(G)]
    pre = [_dn_intra_pre(q[r, :], k[r, :], v[r, :], bg[r, :]) for r in rows]
    inv = [[_inverse_given(hd[-1], x) for hd, x in zip(heads, xj)] for heads, xj in zip(pre, xs)]
    post = [_dn_intra_post(heads, xj) for heads, xj in zip(pre, inv)]
    return tuple(jnp.concatenate([p[i] for p in post], axis=0) for i in range(5)) + (tuple(p[5] for p in post),)


def _lane_pick(rowvec, h):
    return jnp.sum(jnp.where(_iota(rowvec.shape, 1) == h, rowvec, 0.0), axis=1, keepdims=True)


def _dn_intra(q, k, v, bg, shards):
    L, W = q.shape
    NC = L // DN_C
    G = _pick(NC, DN_INTRA_GROUP)
    R = G * DN_C
    NS = NC // G
    WA = DN_H * DN_C
    n = len(shards)

    def body(q_ref, k_ref, v_ref, bg_ref, *rest):
        ins, (u_ref, w_ref, qd_ref, kd_ref, at_ref, gl_ref, x_ref), gouts, sems = rest[:n], rest[n:n + 7], rest[n + 7:2 * n + 7], rest[2 * n + 7:]
        _exchange_start(ins, gouts, *sems, gather=True, when=pl.program_id(0) == 0)
        rows = [slice(j * DN_C, (j + 1) * DN_C) for j in range(G)]
        pre = [_dn_intra_pre(q_ref[r, :], k_ref[r, :], v_ref[r, :], bg_ref[r, :]) for r in rows]
        inv = _unit_lower_inverses([hd[-1] for heads in pre for hd in heads])
        for j, r in enumerate(rows):
            u, w, qd, kd, at, gl, xs = _dn_intra_post(pre[j], inv[j * DN_H:(j + 1) * DN_H])
            u_ref[r, :], x_ref[r, :] = u, xs
            w_ref[r, :], qd_ref[r, :], kd_ref[r, :], at_ref[r, :] = _b(w), _b(qd), _b(kd), _b(at)
            gl_ref[j] = gl
        _exchange_wait(ins, gouts, *sems, gather=True, when=pl.program_id(0) == NS - 1)

    row = lambda wd: pl.BlockSpec((R, wd), lambda s: (s, 0))
    anyspec = pl.BlockSpec(memory_space=pl.ANY)
    res = pl.pallas_call(
        body, name="dn_intra", grid=(NS,),
        in_specs=[row(W), row(W), row(W), row(HP)] + [anyspec] * n,
        out_specs=[row(W), row(W), row(W), row(W), row(WA), pl.BlockSpec((G, 1, HP), lambda s: (s, 0, 0)), row(WA)] + [anyspec] * n,
        out_shape=[SDS((L, W), f32), SDS((L, W), bf16), SDS((L, W), bf16), SDS((L, W), bf16), SDS((L, WA), bf16), SDS((NC, 1, HP), f32),
                   SDS((L, WA), f32)] + [SDS((N_DEV,) + a.shape, a.dtype) for a in shards],
        scratch_shapes=_exchange_sems(n), compiler_params=_cp())(q, k, v, bg, *shards)
    return res[:7], res[7:]


def _dn_scan(u, w, qd, kd, at, gl):
    L, W = u.shape
    NC = L // DN_C
    G = _pick(NC, DN_SCAN_GROUP)
    R = G * DN_C

    def body(u_ref, w_ref, qd_ref, kd_ref, at_ref, gl_ref, o_ref, vn_ref, s_ref, S):
        @pl.when(pl.program_id(0) == 0)
        def _():
            S[...] = jnp.zeros_like(S)

        for j in range(G):
            r = slice(j * DN_C, (j + 1) * DN_C)
            glrow = gl_ref[j]
            for h in range(DN_H):
                c = slice(h * DN_D, (h + 1) * DN_D)
                Sh = S[h]
                s_ref[j, h] = Sh
                Sb = _b(Sh)
                vb = _b(u_ref[r, c] - _dot(w_ref[r, c], Sb))
                vn_ref[r, c] = vb
                o_ref[r, c] = _dot(qd_ref[r, c], Sb) + _dot(at_ref[r, h * DN_C:(h + 1) * DN_C], vb)
                S[h] = Sh * jnp.exp(_lane_pick(glrow, h)) + _dot_tn(kd_ref[r, c], vb)

    row = lambda wd: pl.BlockSpec((R, wd), lambda n: (n, 0))
    return pl.pallas_call(
        body, name="dn_scan", grid=(NC // G,),
        in_specs=[row(W), row(W), row(W), row(W), row(DN_H * DN_C), pl.BlockSpec((G, 1, HP), lambda n: (n, 0, 0))],
        out_specs=[row(W), row(W), pl.BlockSpec((G, DN_H, DN_D, DN_D), lambda n: (n, 0, 0, 0))],
        out_shape=[SDS((L, W), f32), SDS((L, W), bf16), SDS((NC, DN_H, DN_D, DN_D), f32)],
        scratch_shapes=[pltpu.VMEM((DN_H, DN_D, DN_D), f32)], compiler_params=_cp())(u, w, qd, kd, at, gl)


def _dn_scan_bwd(do, w, qd, kd, at, gl):
    L, W = do.shape
    NC = L // DN_C
    G = _pick(NC, DN_SCAN_GROUP)
    R = G * DN_C
    NS = NC // G

    def body(do_ref, w_ref, qd_ref, kd_ref, at_ref, gl_ref, dvn_ref, ds_ref, dS):
        @pl.when(pl.program_id(0) == 0)
        def _():
            dS[...] = jnp.zeros_like(dS)

        for j in reversed(range(G)):
            r = slice(j * DN_C, (j + 1) * DN_C)
            glrow = gl_ref[j]
            for h in range(DN_H):
                c = slice(h * DN_D, (h + 1) * DN_D)
                dSo = dS[h]
                ds_ref[j, h] = dSo
                dob = _b(do_ref[r, c])
                dvn = _dot_tn(at_ref[r, h * DN_C:(h + 1) * DN_C], dob) + _dot(kd_ref[r, c], _b(dSo))
                dvn_ref[r, c] = dvn
                dS[h] = _dot_tn(qd_ref[r, c], dob) + dSo * jnp.exp(_lane_pick(glrow, h)) - _dot_tn(w_ref[r, c], _b(dvn))

    row = lambda wd: pl.BlockSpec((R, wd), lambda n: (NS - 1 - n, 0))
    return pl.pallas_call(
        body, name="dn_scan_bwd", grid=(NS,),
        in_specs=[row(W), row(W), row(W), row(W), row(DN_H * DN_C), pl.BlockSpec((G, 1, HP), lambda n: (NS - 1 - n, 0, 0))],
        out_specs=[row(W), pl.BlockSpec((G, DN_H, DN_D, DN_D), lambda n: (NS - 1 - n, 0, 0, 0))],
        out_shape=[SDS((L, W), f32), SDS((NC, DN_H, DN_D, DN_D), f32)],
        scratch_shapes=[pltpu.VMEM((DN_H, DN_D, DN_D), f32)], compiler_params=_cp())(do, w, qd, kd, at, gl)


def _dn_intra_bwd(q, k, v, bg, xinv, do, vn, dvn, states, dstates):
    L, W = q.shape
    NC = L // DN_C
    G = _pick(NC, DN_INTRA_GROUP)
    R = G * DN_C

    def body(q_ref, k_ref, v_ref, bg_ref, x_ref, do_ref, vn_ref, dvn_ref, s_ref, ds_ref, dq_ref, dk_ref, dv_ref, dbg_ref):
        lane1 = _iota((1, HP), 1)
        rows = [slice(j * DN_C, (j + 1) * DN_C) for j in range(G)]
        xs = [[x_ref[r, h * DN_C:(h + 1) * DN_C] for h in range(DN_H)] for r in rows]
        fwd, vjp = jax.vjp(functools.partial(_dn_intra_group, xs=xs), q_ref[...], k_ref[...], v_ref[...], bg_ref[...])
        dws, dqds, dkds, dats, dgls = [], [], [], [], []
        for j, r in enumerate(rows):
            dw, dqd, dkd, dat = [], [], [], []
            dgl = jnp.zeros((1, HP), f32)
            for h in range(DN_H):
                c = slice(h * DN_D, (h + 1) * DN_D)
                Sh, dSo = s_ref[j, h], ds_ref[j, h]
                Sb, dob, vb = _b(Sh), _b(do_ref[r, c]), vn_ref[r, c]
                dw.append(-_dot_nt(_b(dvn_ref[r, c]), Sb))
                dqd.append(_dot_nt(dob, Sb))
                dat.append(_dot_nt(dob, vb))
                dkd.append(_dot_nt(vb, _b(dSo)))
                dcd = jnp.sum(jnp.sum(Sh * dSo, axis=1, keepdims=True), axis=0, keepdims=True)
                dgl = dgl + jnp.where(lane1 == h, dcd * jnp.exp(_lane_pick(fwd[5][j], h)), 0.0)
            cat = lambda xs_: jnp.concatenate(xs_, axis=1)
            dws.append(cat(dw)), dqds.append(cat(dqd)), dkds.append(cat(dkd)), dats.append(cat(dat)), dgls.append(dgl)
        cat0 = lambda xs_: jnp.concatenate(xs_, axis=0)
        dq, dk, dv, dbg = vjp((dvn_ref[...], cat0(dws), cat0(dqds), cat0(dkds), cat0(dats), tuple(dgls)))
        dq_ref[...], dk_ref[...], dv_ref[...], dbg_ref[...] = dq, dk, dv, dbg

    row = lambda wd: pl.BlockSpec((R, wd), lambda s: (s, 0))
    st = pl.BlockSpec((G, DN_H, DN_D, DN_D), lambda s: (s, 0, 0, 0))
    return pl.pallas_call(
        body, name="dn_intra_bwd", grid=(NC // G,),
        in_specs=[row(W), row(W), row(W), row(HP), row(DN_H * DN_C), row(W), row(W), row(W), st, st],
        out_specs=[row(W), row(W), row(W), row(HP)],
        out_shape=[SDS((L, W), f32)] * 3 + [SDS((L, HP), f32)],
        compiler_params=_cp())(q, k, v, bg, xinv, do, vn, dvn, states, dstates)


def _dn_normgate(oraw, dz, wn):
    outs = []
    for h in range(DN_H):
        sl = slice(h * DN_D, (h + 1) * DN_D)
        z = dz[:, sl]
        outs.append(_rms(oraw[:, sl], wn) * (z * _sigmoid(z)))
    return jnp.concatenate(outs, axis=1)


def _mix_fwd(op, oraw, dz, ga, gb, h0, wn, wbf, wbd, wo):
    L, D = h0.shape
    TM = _pick(L, ROW_TILES)

    def body(op_ref, or_ref, dz_ref, ga_ref, gb_ref, h0_ref, wn_ref, wbf_ref, wbd_ref, wo_ref, h1_ref):
        pf = _dot(_b(op_ref[...]), wbf_ref[...])
        pd = _dot(_b(_dn_normgate(or_ref[...], dz_ref[...], wn_ref[...])), wbd_ref[...])
        y = _sigmoid(ga_ref[...]) * pf + _sigmoid(gb_ref[...]) * pd
        h1_ref[...] = h0_ref[...] + _dot(_b(y), wo_ref[...])

    row = lambda wd: pl.BlockSpec((TM, wd), lambda i: (i, 0))
    full = lambda a: pl.BlockSpec(a.shape, lambda i: (0, 0))
    return pl.pallas_call(
        body, name="mix_fwd", grid=(L // TM,),
        in_specs=[row(op.shape[1]), row(oraw.shape[1]), row(dz.shape[1]), row(D), row(D), row(D), full(wn), full(wbf), full(wbd), full(wo)],
        out_specs=row(D), out_shape=SDS((L, D), f32), compiler_params=_cp())(op, oraw, dz, ga, gb, h0, wn, wbf, wbd, wo)


def _mix_bwd(dh1, op, oraw, dz, ga, gb, wn, wbf, wbd, wo):
    L, D = dh1.shape
    TM = _pick(L, ROW_TILES)
    WF, WD = op.shape[1], oraw.shape[1]

    def body(dh1_ref, op_ref, or_ref, dz_ref, ga_ref, gb_ref, wn_ref, wbf_ref, wbd_ref, wo_ref,
             dop_ref, dor_ref, dg_ref, af_ref, ad_ref, dpf_ref, dpd_ref, y_ref, dmix_ref, acc_ref):
        @pl.when(pl.program_id(0) == 0)
        def _():
            acc_ref[...] = jnp.zeros_like(acc_ref)

        opv = op_ref[...]
        af = _b(opv)
        ad, vjp = jax.vjp(_dn_normgate, or_ref[...], dz_ref[...], wn_ref[...])
        adb = _b(ad)
        pf, pd = _dot(af, wbf_ref[...]), _dot(adb, wbd_ref[...])
        sa, sb = _sigmoid(ga_ref[...]), _sigmoid(gb_ref[...])
        dmix = _b(dh1_ref[...])
        dy = _dot_nt(dmix, wo_ref[...])
        dpf, dpd = _b(dy * sa), _b(dy * sb)
        dor, ddz, dwn = vjp(_dot_nt(dpd, wbd_ref[...]))
        dop = _dot_nt(dpf, wbf_ref[...])
        lane = _iota((TM, HP), 1)
        for h in range(WF // HP):
            c = slice(h * HP, (h + 1) * HP)
            delta = jnp.sum(jnp.where(lane < FOX_D, dop[:, c] * opv[:, c], 0.0), axis=1, keepdims=True)
            dop_ref[:, c] = _b(dop[:, c] - _spread3(jnp.broadcast_to(delta, (TM, HP)), lane, C_DELTA0))
        dor_ref[...] = dor
        dg_ref[:, 0:WD] = _b(ddz)
        dg_ref[:, WD:WD + D] = _b(dy * pf * sa * (1.0 - sa))
        dg_ref[:, WD + D:] = _b(dy * pd * sb * (1.0 - sb))
        af_ref[...], ad_ref[...], y_ref[...] = af, adb, _b(sa * pf + sb * pd)
        dpf_ref[...], dpd_ref[...], dmix_ref[...] = dpf, dpd, dmix
        acc_ref[0:1, :] += dwn

    row = lambda wd: pl.BlockSpec((TM, wd), lambda i: (i, 0))
    full = lambda a: pl.BlockSpec(a.shape, lambda i: (0, 0))
    return pl.pallas_call(
        body, name="mix_bwd", grid=(L // TM,),
        in_specs=[row(D), row(WF), row(WD), row(WD), row(D), row(D), full(wn), full(wbf), full(wbd), full(wo)],
        out_specs=[row(WF), row(WD), row(WD + 2 * D), row(WF), row(WD), row(D), row(D), row(D), row(D),
                   pl.BlockSpec((8, HP), lambda i: (0, 0))],
        out_shape=[SDS((L, WF), bf16), SDS((L, WD), f32), SDS((L, WD + 2 * D), bf16), SDS((L, WF), bf16), SDS((L, WD), bf16),
                   SDS((L, D), bf16), SDS((L, D), bf16), SDS((L, D), bf16), SDS((L, D), bf16), SDS((8, HP), f32)],
        compiler_params=_cp())(dh1, op, oraw, dz, ga, gb, wn, wbf, wbd, wo)


def _ffn_fwd_bwd(h1, tgt, w2, wf, wgt, wut, wd):
    L, D = h1.shape
    F = wd.shape[0]
    TM = _pick(L, FFN_TILES)
    gt, nb, _, tmaps = _shifted_blocks(TM)
    NC = F // FFN_CHUNK
    chunks = [slice(c * FFN_CHUNK, (c + 1) * FFN_CHUNK) for c in range(NC)]

    def body(h_ref, *refs):
        t_refs, (w2_ref, wf_ref, wg_hbm, wu_hbm, wd_hbm,
                 dh1_ref, xn_ref, dg_ref, du_ref, act_ref, dh2_ref, acc_ref, wg_v, wu_v, wd_v, sems, g_s, u_s) = refs[:nb], refs[nb:]
        i = pl.program_id(0)
        _load_once([(wg_hbm, wg_v), (wu_hbm, wu_v), (wd_hbm, wd_v)], sems)

        @pl.when(i == 0)
        def _():
            acc_ref[...] = jnp.zeros_like(acc_ref)

        h1v = h_ref[...]
        xn2, vjp2 = jax.vjp(_rms, h1v, w2_ref[...])
        xb = _b(xn2)
        up = lambda c: (_dot_nt(xb, wg_v[chunks[c], :]), _dot_nt(xb, wu_v[chunks[c], :]))
        h2, nxt = h1v, up(0)
        for c in range(NC):
            (g, u), nxt = nxt, (up(c + 1) if c + 1 < NC else None)
            g_s[:, chunks[c]], u_s[:, chunks[c]] = g, u
            ab = _b(g * _sigmoid(g) * u)
            act_ref[:, chunks[c]] = ab
            h2 = h2 + _dot(ab, wd_v[chunks[c], :])
        out, vjpf = jax.vjp(_rms, h2, wf_ref[...])
        valid = (i * TM + _iota((TM, 1), 0)) >= PREFIX
        diff = jnp.where(valid, out - jnp.concatenate([r[...] for r in t_refs], axis=0), 0.0)
        loss = 0.5 * jnp.sum(jnp.sum(diff * diff, axis=1, keepdims=True), axis=0, keepdims=True) / D
        dh2, dwf = vjpf(diff * (1.0 / D))
        dh2b = _b(dh2)
        down = lambda c: _dot_nt(dh2b, wd_v[chunks[c], :])
        dxn, nxt = None, down(0)
        for c in range(NC):
            dact, nxt = nxt, (down(c + 1) if c + 1 < NC else None)
            g, u = g_s[:, chunks[c]], u_s[:, chunks[c]]
            sg = _sigmoid(g)
            dgb = _b(dact * u * (sg * (1.0 + g * (1.0 - sg))))
            dub = _b(dact * (g * sg))
            dg_ref[:, chunks[c]], du_ref[:, chunks[c]] = dgb, dub
            part = _dot(dgb, wg_v[chunks[c], :]) + _dot(dub, wu_v[chunks[c], :])
            dxn = part if dxn is None else dxn + part
        dh1n, dw2 = vjp2(dxn)
        dh1_ref[...] = dh2 + dh1n
        xn_ref[...], dh2_ref[...] = xb, dh2b
        acc_ref[0:1, :] += dw2
        acc_ref[1:2, :] += dwf
        acc_ref[2:3, :] += jnp.broadcast_to(loss, (1, D))

    row = lambda wd_: pl.BlockSpec((TM, wd_), lambda i: (i, 0))
    once = lambda wd_: pl.BlockSpec((TM, wd_), lambda i: (i, 0), pipeline_mode=pl.Buffered(1))
    vec = pl.BlockSpec((1, D), lambda i: (0, 0))
    anyspec = pl.BlockSpec(memory_space=pl.ANY)
    return pl.pallas_call(
        body, name="ffn_fwd_bwd", grid=(L // TM,),
        in_specs=[row(D)] + [pl.BlockSpec((gt, D), m) for m in tmaps] + [vec, vec, anyspec, anyspec, anyspec],
        out_specs=[once(D), once(D), once(F), once(F), once(F), once(D), pl.BlockSpec((8, D), lambda i: (0, 0))],
        out_shape=[SDS((L, D), f32), SDS((L, D), bf16), SDS((L, F), bf16), SDS((L, F), bf16), SDS((L, F), bf16), SDS((L, D), bf16),
                   SDS((8, D), f32)],
        scratch_shapes=[pltpu.VMEM((F, D), bf16), pltpu.VMEM((F, D), bf16), pltpu.VMEM((F, D), bf16), pltpu.SemaphoreType.DMA((3,)),
                        pltpu.VMEM((TM, F), f32), pltpu.VMEM((TM, F), f32)],
        compiler_params=_cp())(h1, *([tgt] * nb), w2, wf, wgt, wut, wd)


def _pad_lanes(v, n=HP):
    return jnp.pad(v.astype(f32), ((0, 0), (0, n - v.shape[1])))


def _pack_w_in(wt_full):
    D = wt_full.shape[1]
    FW, DW = FOX_H * FOX_D, DN_H * DN_D
    o = 0
    parts = {}
    for name, wd in (("fq", FW), ("fk", FW), ("fv", FW), ("fl", FOX_H), ("dn", 3 * DW), ("ba", 2 * DN_H), ("dz", DW), ("ga", D), ("gb", D)):
        parts[name] = wt_full[o:o + wd]
        o += wd
    assert o == wt_full.shape[0]
    heads = lambda w: jnp.pad(w.reshape(FOX_H, FOX_D, D), ((0, 0), (0, HP - FOX_D), (0, 0))).reshape(FOX_H * HP, D)
    small = lambda w: jnp.pad(w, ((0, SMALL_W - w.shape[0]), (0, 0)))
    packed = dict(fq=heads(parts["fq"]), fk=heads(parts["fk"]), fv=heads(parts["fv"]), sf=small(parts["fl"]), sd=small(parts["ba"]),
                  dn=parts["dn"], dz=parts["dz"], ga=parts["ga"], gb=parts["gb"])
    return jnp.concatenate([packed[name] for name, _, _, _ in _seg_layout(D)], axis=0)


def _unpack_w_in(groups, d_model):
    D = groups[0].shape[1]
    FW = FOX_H * FOX_D
    segs = {}
    for grp, g in zip(GROUPS, groups):
        o = 0
        for name, wd, _, sg in _seg_layout(d_model):
            if sg == grp:
                segs[name] = g[o:o + wd]
                o += wd
    heads = lambda g: g.reshape(FOX_H, HP, D)[:, :FOX_D].reshape(FW, D)
    return jnp.concatenate([heads(segs["fq"]), heads(segs["fk"]), heads(segs["fv"]), segs["sf"][:FOX_H], segs["dn"],
                            segs["sd"][:2 * DN_H], segs["dz"], segs["ga"], segs["gb"]], axis=0)


def _local_step(x, tgt, meta, w1, w_in_t, fbias, cw, alog, dtb, wn, w2, wf, late_shards):
    T, D = x.shape
    pre = jnp.concatenate([jnp.zeros((N_PAD, D), f32), meta], axis=0)
    wp = _pack_w_in(w_in_t)
    bias_p, alog_p, dt_p = _pad_lanes(fbias), _pad_lanes(jnp.pad(alog, ((0, 0), (DN_H, 0)))), _pad_lanes(jnp.pad(dtb, ((0, 0), (DN_H, 0))))

    (h0, xn, fq, fk, sf, fv, dn, sd, dz, ga, gb), g_mix = _in_proj(x, pre, w1, wp, [late_shards[n] for n in LATE_MIX])
    qa, ka, va = _fox_prep(fq, fk, fv, sf, bias_p)
    op, qb, g_ffn = _fox_fwd(qa, ka, va, [late_shards[n] for n in LATE_FFN])
    qn, kn, vn, bg = _dn_prep(dn, sd, cw, alog_p, dt_p)
    (u_dn, w_dn, qd_dn, kd_dn, at_dn, gl_dn, x_dn), g_down = _dn_intra(qn, kn, vn, bg, [late_shards[n] for n in LATE_DOWN])
    full = {n: _from_slabs(n, s) for n, s in zip(LATE_MIX + LATE_FFN + LATE_DOWN, tuple(g_mix) + tuple(g_ffn) + tuple(g_down))}
    wbf, wbd, wo, wgt, wut, wd = (full[n] for n in ("w_branch_fox", "w_branch_dn", "w_out", "w_ffn_gate", "w_ffn_up", "w_ffn_down"))
    wbf_p = jnp.pad(wbf.reshape(FOX_H, FOX_D, D), ((0, 0), (0, HP - FOX_D), (0, 0))).reshape(FOX_H * HP, D)
    oraw, vnew, states = _dn_scan(u_dn, w_dn, qd_dn, kd_dn, at_dn, gl_dn)
    h1 = _mix_fwd(op, oraw, dz, ga, gb, h0, wn, wbf_p, wbd, wo)

    dh1, xn2, dgate, dup, act, dh2, acc_f = _ffn_fwd_bwd(h1, tgt, w2, wf, wgt, wut, wd)
    g_wg, g_wu, g_wd = _matmul_tn(dgate, xn2, "dw_ffn_gate"), _matmul_tn(dup, xn2, "dw_ffn_up"), _matmul_tn(act, dh2, "dw_ffn_down")

    dop, dor, d_mix, af, ad, dpf, dpd, yb, dmix, acc_m = _mix_bwd(dh1, op, oraw, dz, ga, gb, wn, wbf_p, wbd, wo)
    g_wbf = _matmul_tn(af, dpf, "dw_branch_fox").reshape(FOX_H, HP, D)[:, :FOX_D].reshape(FOX_H * FOX_D, D)
    g_wbd, g_wo = _matmul_tn(ad, dpd, "dw_branch_dn"), _matmul_tn(yb, dmix, "dw_out")

    dvnew, dstates = _dn_scan_bwd(dor, w_dn, qd_dn, kd_dn, at_dn, gl_dn)
    dqn, dkn, dvn, dbg = _dn_intra_bwd(qn, kn, vn, bg, x_dn, dor, vnew, dvnew, states, dstates)
    d_dn, acc_cw, acc_p = _dn_prep_bwd(dn, sd, cw, alog_p, dt_p, dqn, dkn, dvn, dbg)
    g_late = dict(w_branch_fox=g_wbf, w_branch_dn=g_wbd, w_out=g_wo, w_ffn_gate=g_wg, w_ffn_up=g_wu, w_ffn_down=g_wd)
    (dqa, dka, d_fv), recv = _fox_bwd(qb, ka, va, dop, [_to_slabs(n, g_late[n]) for n in LATE])
    d_fox, acc_b = _fox_prep_bwd(dqa, dka, sf, bias_p)

    dgroups = [d_fox, d_fv, d_dn, d_mix]
    g_wp = [_matmul_tn(dg, xn, "dw_in_" + grp) for grp, dg in zip(GROUPS, dgroups)]
    dh0, acc_1, (recv_w_in,) = _in_proj_bwd(dgroups, wp, h0, w1, dh1, [_to_slabs("w_in", _unpack_w_in(g_wp, D))])
    recv = dict(zip(LATE, recv), w_in=recv_w_in)

    small = dict(loss=acc_f[2, 0:1], mix_norm_w=acc_1[0], fox_forget_bias=acc_b[0, :FOX_H], dn_a_log=acc_p[0, DN_H:2 * DN_H],
                 dn_dt_bias=acc_p[1, DN_H:2 * DN_H], dn_out_norm_w=acc_m[0], ffn_norm_w=acc_f[0], final_norm_w=acc_f[1],
                 meta_tokens=dh0[N_PAD:PREFIX].reshape(-1), dn_conv_w=acc_cw[:CONV_K].reshape(-1))
    return dh0[PREFIX:], small, recv


def _mesh_pos():
    x, y, c = lax.axis_index("x"), lax.axis_index("y"), lax.axis_index("c")
    return x, y, c, 4 * x + 2 * y + c


def _peer(x, y, c, m):
    flip = lambda v, on: 1 - v if on else v
    px, py, pc = flip(x, m & 4), flip(y, m & 2), flip(c, m & 1)
    return (px, py, pc), 4 * px + 2 * py + pc


def _exchange_sems(n):
    return [pltpu.SemaphoreType.DMA((n, N_DEV - 1)), pltpu.SemaphoreType.DMA((n, N_DEV - 1)), pltpu.SemaphoreType.DMA((n,))]


def _exchange_part(ins, outs, send_sems, recv_sems, loc_sems, gather, m, receive):
    x, y, c, me = _mesh_pos()
    src = lambda a, pid: ins[a] if gather else ins[a].at[pid]
    if m == 0:
        return [pltpu.make_async_copy(src(a, me), outs[a].at[me], loc_sems.at[a]) for a in range(len(ins))]
    peer, pid = _peer(x, y, c, m)
    return [pltpu.make_async_remote_copy(src_ref=src(a, pid), dst_ref=outs[a].at[pid if receive else me], send_sem=send_sems.at[a, m - 1],
                                         recv_sem=recv_sems.at[a, m - 1], device_id=peer, device_id_type=MESH) for a in range(len(ins))]


def _exchange_start(*refs, gather, when):
    @pl.when(when)
    def _():
        for m in range(N_DEV):
            for cp in _exchange_part(*refs, gather, m, receive=False):
                cp.start()


def _exchange_wait(*refs, gather, when):
    @pl.when(when)
    def _():
        for m in range(1, N_DEV):
            for cp in _exchange_part(*refs, gather, m, receive=True):
                cp.wait_recv()
        for m in list(range(1, N_DEV)) + [0]:
            for cp in _exchange_part(*refs, gather, m, receive=False):
                cp.wait() if m == 0 else cp.wait_send()


def _gather_two_level(arrays, name):
    n = len(arrays)

    def body(*refs):
        ins, outs, (send_sems, recv_sems, loc_sems) = refs[:n], refs[n:2 * n], refs[2 * n:]
        x, y, c, me = _mesh_pos()
        sib = (x, y, 1 - c)
        chips = [(1 - x, y), (x, 1 - y), (1 - x, 1 - y)]
        dev_id = lambda px, py, pc: 4 * px + 2 * py + pc

        def copy(a, k, block, to, own=False):
            return pltpu.make_async_remote_copy(src_ref=ins[a] if own else outs[a].at[block], dst_ref=outs[a].at[block],
                                                send_sem=send_sems.at[a, k], recv_sem=recv_sems.at[a, k], device_id=to, device_id_type=MESH)

        local = [pltpu.make_async_copy(ins[a], outs[a].at[me], loc_sems.at[a]) for a in range(n)]
        first = [copy(a, 0, me, sib, own=True) for a in range(n)]
        first += [copy(a, 1 + j, me, (*chip, c), own=True) for j, chip in enumerate(chips) for a in range(n)]
        for cp in local + first:
            cp.start()
        passed = []
        for j, chip in enumerate(chips):
            for a in range(n):
                copy(a, 1 + j, dev_id(*chip, c), sib).wait_recv()
                cp = copy(a, 4 + j, dev_id(*chip, c), sib)
                cp.start()
                passed.append(cp)
        for a in range(n):
            copy(a, 0, dev_id(x, y, 1 - c), sib).wait_recv()
        for j, chip in enumerate(chips):
            for a in range(n):
                copy(a, 4 + j, dev_id(*chip, 1 - c), sib).wait_recv()
        for cp in first + passed:
            cp.wait_send()
        for cp in local:
            cp.wait()

    anyspec = pl.BlockSpec(memory_space=pl.ANY)
    return pl.pallas_call(
        body, name=name, in_specs=[anyspec] * n, out_specs=[anyspec] * n,
        out_shape=[SDS((N_DEV,) + a.shape, a.dtype) for a in arrays],
        scratch_shapes=_exchange_sems(n))(*arrays)


def _all_reduce_small(v):
    R = v.shape[0]

    def body(v_ref, o_ref, gath, send_sems, recv_sems):
        x, y, c, me = _mesh_pos()
        gath[me] = v_ref[...]
        sends = []
        for m in range(1, N_DEV):
            peer, _ = _peer(x, y, c, m)
            cp = pltpu.make_async_remote_copy(src_ref=v_ref, dst_ref=gath.at[me], send_sem=send_sems.at[m - 1],
                                              recv_sem=recv_sems.at[m - 1], device_id=peer, device_id_type=MESH)
            cp.start()
            sends.append(cp)
        for m in range(1, N_DEV):
            peer, pid = _peer(x, y, c, m)
            pltpu.make_async_remote_copy(src_ref=v_ref, dst_ref=gath.at[pid], send_sem=send_sems.at[m - 1],
                                         recv_sem=recv_sems.at[m - 1], device_id=peer, device_id_type=MESH).wait_recv()
        for cp in sends:
            cp.wait_send()
        tot = gath[0]
        for d in range(1, N_DEV):
            tot = tot + gath[d]
        o_ref[...] = tot

    vm = pl.BlockSpec(memory_space=pltpu.VMEM)
    return pl.pallas_call(
        body, name="all_reduce_small", in_specs=[vm], out_specs=vm, out_shape=SDS((R, HP), f32),
        scratch_shapes=[pltpu.VMEM((N_DEV, R, HP), f32), pltpu.SemaphoreType.DMA((N_DEV - 1,)), pltpu.SemaphoreType.DMA((N_DEV - 1,))],
        )(v)


def _adamw_math(w, g, m, v):
    m = ADAM_B1 * m + (1.0 - ADAM_B1) * g
    v = ADAM_B2 * v + (1.0 - ADAM_B2) * (g * g)
    m_hat = m / (1.0 - ADAM_B1 ** ADAM_STEP)
    v_hat = v / (1.0 - ADAM_B2 ** ADAM_STEP)
    return -ADAM_LR * (m_hat / (jnp.sqrt(v_hat) + ADAM_EPS) + ADAM_WD * w), m, v


def _adamw(g, w, m, v, name):
    R, Cc = w.shape[-2:]
    if R <= ADAMW_WHOLE_ROWS or R % HP == 0:
        TR, TC = (R if R <= ADAMW_WHOLE_ROWS else _pick(R, ADAMW_TILES)), Cc
    else:
        TR, TC = R, _pick(Cc, ADAMW_TILES)
    slabs = g.ndim == 3
    lead = w.ndim - 2

    def body(g_ref, w_ref, m_ref, v_ref, go_ref, d_ref, mo_ref, vo_ref):
        if slabs:
            gs = g_ref[0].astype(f32)
            for k in range(1, N_DEV):
                gs = gs + g_ref[k].astype(f32)
        else:
            gs = g_ref[...]
        at = 0 if lead else Ellipsis
        d, mn, vn = _adamw_math(w_ref[at], gs, m_ref[at], v_ref[at])
        go_ref[at], d_ref[at], mo_ref[at], vo_ref[at] = gs, d, mn, vn

    grid = (R // TR, Cc // TC)
    blk = pl.BlockSpec((1,) * lead + (TR, TC), lambda i, j: (0,) * lead + (i, j))
    gblk = pl.BlockSpec((N_DEV, TR, TC), lambda i, j: (0, i, j)) if slabs else pl.BlockSpec((TR, TC), lambda i, j: (i, j))
    return pl.pallas_call(
        body, name=name, grid=grid, in_specs=[gblk, blk, blk, blk], out_specs=[blk] * 4,
        out_shape=[SDS(w.shape, f32)] * 4, compiler_params=_cp(2))(g, w, m, v)


def _adamw_small(gs, ws, ms, vs):
    k = len(ws)

    def body(*refs):
        ins, outs = refs[:4 * k], refs[4 * k:]
        for t in range(k):
            g, w, m, v = (ins[j * k + t][...] for j in range(4))
            outs[t][...], outs[k + t][...], outs[2 * k + t][...] = _adamw_math(w, g, m, v)

    whole = lambda a: pl.BlockSpec(a.shape, lambda i, nd=a.ndim: (0,) * nd)
    res = pl.pallas_call(
        body, name="adamw_small", grid=(1,), in_specs=[whole(a) for a in (*gs, *ws, *ms, *vs)],
        out_specs=[whole(a) for a in ws] * 3, out_shape=[SDS(a.shape, f32) for a in ws] * 3,
        compiler_params=_cp())(*gs, *ws, *ms, *vs)
    return [[res[j * k + t] for j in range(3)] for t in range(k)]


WEIGHTS = ("meta_tokens", "mix_norm_w", "w_in", "fox_forget_bias", "dn_conv_w", "dn_a_log", "dn_dt_bias", "dn_out_norm_w",
           "w_branch_fox", "w_branch_dn", "w_out", "ffn_norm_w", "w_ffn_gate", "w_ffn_up", "w_ffn_down", "final_norm_w")
COL_SHARDED = ("w_in", "w_branch_fox", "w_branch_dn", "w_ffn_gate", "w_ffn_up")
ROW_SHARDED = ("w_out", "w_ffn_down")
BIG = COL_SHARDED + ROW_SHARDED
LATE = tuple(n for n in BIG if n != "w_in")
LATE_MIX = ("w_branch_fox", "w_branch_dn", "w_out")
LATE_FFN = ("w_ffn_gate", "w_ffn_up")
LATE_DOWN = ("w_ffn_down",)
SMALL = tuple(n for n in WEIGHTS if n not in BIG)
TRANSPOSED = ("w_in", "w_ffn_gate", "w_ffn_up")


def _to_slabs(name, g):
    r, c = g.shape
    if name in COL_SHARDED and name not in TRANSPOSED:
        return _b(g.reshape(r, N_DEV, c // N_DEV).transpose(1, 0, 2))
    return _b(g.reshape(N_DEV, r // N_DEV, c))


def _from_slabs(name, s):
    n, r, c = s.shape
    if name in COL_SHARDED and name not in TRANSPOSED:
        return s.transpose(1, 0, 2).reshape(r, n * c)
    return s.reshape(n * r, c)


def kernel(x, meta_tokens, mix_norm_w, w_in, fox_forget_bias, dn_conv_w, dn_a_log, dn_dt_bias, dn_out_norm_w, w_branch_fox, w_branch_dn, w_out, ffn_norm_w, w_ffn_gate, w_ffn_up, w_ffn_down, final_norm_w, loss_target, m_meta_tokens, m_mix_norm_w, m_w_in, m_fox_forget_bias, m_dn_conv_w, m_dn_a_log, m_dn_dt_bias, m_dn_out_norm_w, m_w_branch_fox, m_w_branch_dn, m_w_out, m_ffn_norm_w, m_w_ffn_gate, m_w_ffn_up, m_w_ffn_down, m_final_norm_w, v_meta_tokens, v_mix_norm_w, v_w_in, v_fox_forget_bias, v_dn_conv_w, v_dn_a_log, v_dn_dt_bias, v_dn_out_norm_w, v_w_branch_fox, v_w_branch_dn, v_w_out, v_ffn_norm_w, v_w_ffn_gate, v_w_ffn_up, v_w_ffn_down, v_final_norm_w):
    w = dict(meta_tokens=meta_tokens, mix_norm_w=mix_norm_w, w_in=w_in, fox_forget_bias=fox_forget_bias, dn_conv_w=dn_conv_w, dn_a_log=dn_a_log, dn_dt_bias=dn_dt_bias, dn_out_norm_w=dn_out_norm_w, w_branch_fox=w_branch_fox, w_branch_dn=w_branch_dn, w_out=w_out, ffn_norm_w=ffn_norm_w, w_ffn_gate=w_ffn_gate, w_ffn_up=w_ffn_up, w_ffn_down=w_ffn_down, final_norm_w=final_norm_w)
    mom = dict(meta_tokens=m_meta_tokens, mix_norm_w=m_mix_norm_w, w_in=m_w_in, fox_forget_bias=m_fox_forget_bias, dn_conv_w=m_dn_conv_w, dn_a_log=m_dn_a_log, dn_dt_bias=m_dn_dt_bias, dn_out_norm_w=m_dn_out_norm_w, w_branch_fox=m_w_branch_fox, w_branch_dn=m_w_branch_dn, w_out=m_w_out, ffn_norm_w=m_ffn_norm_w, w_ffn_gate=m_w_ffn_gate, w_ffn_up=m_w_ffn_up, w_ffn_down=m_w_ffn_down, final_norm_w=m_final_norm_w)
    var = dict(meta_tokens=v_meta_tokens, mix_norm_w=v_mix_norm_w, w_in=v_w_in, fox_forget_bias=v_fox_forget_bias, dn_conv_w=v_dn_conv_w, dn_a_log=v_dn_a_log, dn_dt_bias=v_dn_dt_bias, dn_out_norm_w=v_dn_out_norm_w, w_branch_fox=v_w_branch_fox, w_branch_dn=v_w_branch_dn, w_out=v_w_out, ffn_norm_w=v_ffn_norm_w, w_ffn_gate=v_w_ffn_gate, w_ffn_up=v_w_ffn_up, w_ffn_down=v_w_ffn_down, final_norm_w=v_final_norm_w)
    two_d = lambda a: a.reshape(a.shape[-2:]) if a.ndim >= 2 else a.reshape(1, -1)
    me = 4 * lax.axis_index("x") + 2 * lax.axis_index("y") + lax.axis_index("c")
    for d in (w, mom, var):
        for n in TRANSPOSED:
            d[n] = jnp.swapaxes(d[n], -1, -2)

    g_in, g_meta, g_cw = _gather_two_level([_b(two_d(w["w_in"])), two_d(w["meta_tokens"]), two_d(w["dn_conv_w"])], "all_gather_early")
    meta = g_meta.transpose(1, 0, 2).reshape(N_META, -1)
    cw = g_cw.transpose(1, 0, 2).reshape(CONV_K, -1)

    gx, g_small, recv = _local_step(
        x[0], loss_target[0], meta, two_d(w["mix_norm_w"]), _from_slabs("w_in", g_in), two_d(w["fox_forget_bias"]), cw, two_d(w["dn_a_log"]),
        two_d(w["dn_dt_bias"]), two_d(w["dn_out_norm_w"]), two_d(w["ffn_norm_w"]), two_d(w["final_norm_w"]),
        {n: _b(two_d(w[n])) for n in LATE})

    order = ("loss",) + SMALL
    flat = jnp.concatenate([g_small[n].reshape(-1) for n in order])
    rows = -(-flat.shape[0] // (8 * HP)) * 8
    tot = _all_reduce_small(jnp.pad(flat, (0, rows * HP - flat.shape[0])).reshape(rows, HP)).reshape(-1)
    summed, o = {}, 0
    for n in order:
        k = g_small[n].shape[0]
        summed[n] = tot[o:o + k]
        o += k
    loss = summed["loss"][0]
    d_model = x.shape[-1]
    mcols, ccols = d_model // N_DEV, dn_conv_w.shape[-1]
    summed["meta_tokens"] = lax.dynamic_slice(summed["meta_tokens"].reshape(N_META, d_model), (0, me * mcols), (N_META, mcols)).reshape(-1)
    summed["dn_conv_w"] = lax.dynamic_slice(summed["dn_conv_w"].reshape(CONV_K, ccols * N_DEV), (0, me * ccols), (CONV_K, ccols)).reshape(-1)

    res = {}
    for n in BIG:
        res[n] = _adamw(recv[n], w[n], mom[n], var[n], "adamw_" + n)
        if n in TRANSPOSED:
            res[n] = [jnp.swapaxes(r, -1, -2) for r in res[n]]
    gs = [summed[n].reshape(two_d(w[n]).shape) for n in SMALL]
    sres = _adamw_small(gs, *[[two_d(d[n]) for n in SMALL] for d in (w, mom, var)])
    for n, g, r in zip(SMALL, gs, sres):
        res[n] = [a.reshape(w[n].shape) for a in (g, *r)]
    return (loss, gx[None], *[res[n][0] for n in WEIGHTS], *[res[n][1] for n in WEIGHTS], *[res[n][2] for n in WEIGHTS], *[res[n][3] for n in WEIGHTS])
```

```python
import functools
import math

import jax
import jax.numpy as jnp
from jax import lax
from jax.experimental import pallas as pl
from jax.experimental.pallas import tpu as pltpu

f32, bf16 = jnp.float32, jnp.bfloat16
HI = lax.Precision.HIGHEST
MESH = pl.DeviceIdType.MESH
SDS = jax.ShapeDtypeStruct

N_DEV = 8
N_META = 16
PREFIX = 128
N_PAD = PREFIX - N_META
FOX_H, FOX_D = 8, 64
DN_H, DN_D = 4, 128
DN_C = 64
CONV_K = 4
HP = 128
SMALL_W = 256
EPS = 1e-6
NEG = -1e30
C_Q0, C_K0 = 64, 67
LSE_COL = 64
LOG2E, LN2 = 1.4426950408889634, 0.6931471805599453
C_LSE0, C_DELTA0 = 70, 65

ADAM_LR, ADAM_B1, ADAM_B2, ADAM_EPS, ADAM_WD, ADAM_STEP = 0.001, 0.9, 0.999, 1e-08, 0.01, 10

VMEM_LIMIT_V7X = 56 * 1024 * 1024
ROW_TILES = (384, 128)
ATTN_TILES = (384, 128)
FFN_TILES = (384, 128)
FFN_CHUNK = 256
FOX_HEAD_GROUP = 4
FOX_HEAD_GROUP_FWD = 8
ADAMW_TILES = (256, 128)
ADAMW_WHOLE_ROWS = 512
MAX_WGRAD_BLOCK = 1408
MAX_WGRAD_TOKENS = 1056
DN_INTRA_GROUP = (6, 3, 2, 1)
DN_SCAN_GROUP = (6, 3, 2, 1)


def _pick(n, cands):
    for c in cands:
        if n % c == 0:
            return c
    raise ValueError(f"no tile of {cands} divides {n}")


def _cp(n_axes=1):
    return pltpu.CompilerParams(dimension_semantics=("arbitrary",) * n_axes, vmem_limit_bytes=VMEM_LIMIT_V7X)


def _b(x):
    return x.astype(bf16)


def _dot(a, b):
    return jnp.dot(a, b, preferred_element_type=f32)


def _dot_nt(a, b):
    return lax.dot_general(a, b, (((1,), (1,)), ((), ())), preferred_element_type=f32)


def _dot_tn(a, b):
    return lax.dot_general(a, b, (((0,), (0,)), ((), ())), preferred_element_type=f32)


def _dot_hi(a, b):
    return jnp.dot(a, b, preferred_element_type=f32, precision=HI)


def _iota(shape, dim):
    return lax.broadcasted_iota(jnp.int32, shape, dim)


def _rms(x, w):
    return x * lax.rsqrt(jnp.mean(x * x, axis=-1, keepdims=True) + EPS) * w


def _sigmoid(x):
    return jax.nn.sigmoid(x)


def _load_once(pairs, sems):
    @pl.when(pl.program_id(0) == 0)
    def _():
        cps = [pltpu.make_async_copy(src, dst, sems.at[k]) for k, (src, dst) in enumerate(pairs)]
        for cp in cps:
            cp.start()
        for cp in cps:
            cp.wait()


def _seg_layout(d_model):
    return (("fq", FOX_H * HP, bf16, "fox"), ("fk", FOX_H * HP, bf16, "fox"), ("sf", SMALL_W, f32, "fox"),
            ("fv", FOX_H * HP, bf16, "fv"),
            ("dn", 3 * DN_H * DN_D, f32, "dn"), ("sd", SMALL_W, f32, "dn"),
            ("dz", DN_H * DN_D, f32, "mix"), ("ga", d_model, f32, "mix"), ("gb", d_model, f32, "mix"))


GROUPS = ("fox", "fv", "dn", "mix")


def _shifted_blocks(TM):
    g = math.gcd(TM, PREFIX)
    nb, npre = TM // g, PREFIX // g
    assert nb >= npre
    return g, nb, npre, [lambda i, j=j: (jnp.maximum(i * nb + j - npre, 0), 0) for j in range(nb)]


def _in_proj(x, pre, w1, wpt, shards):
    T, D = x.shape
    L = T + PREFIX
    NP = wpt.shape[0]
    TM = _pick(L, ROW_TILES)
    NT = L // TM
    g, nb, npre, xmaps = _shifted_blocks(TM)
    segs = _seg_layout(D)
    ns, n = len(segs), len(shards)
    offs, o = [], 0
    for _, wd, _, _ in segs:
        offs.append(o)
        o += wd
    assert o == NP

    def body(*refs):
        x_refs, (pre_ref, w1_ref, wp_hbm), rest = refs[:nb], refs[nb:nb + 3], refs[nb + 3:]
        ins, h_ref, xn_ref, outs, gouts = rest[:n], rest[n], rest[n + 1], rest[n + 2:n + 2 + ns], rest[n + 2 + ns:2 * n + 2 + ns]
        wp_v, sems = rest[2 * n + 2 + ns:2 * n + 4 + ns]
        xsems = rest[2 * n + 4 + ns:]
        _load_once([(wp_hbm, wp_v)], sems)

        _exchange_start(ins, gouts, *xsems, gather=True, when=pl.program_id(0) == 0)

        first = pl.program_id(0) == 0
        h = jnp.concatenate([jnp.where(first, pre_ref[j * g:(j + 1) * g, :], r[...]) if j < npre else r[...]
                             for j, r in enumerate(x_refs)], axis=0)
        h_ref[...] = h
        xn = _b(_rms(h, w1_ref[...]))
        xn_ref[...] = xn
        for o_ref, off, (_, wd, _, _) in zip(outs, offs, segs):
            o_ref[...] = _dot_nt(xn, wp_v[off:off + wd, :]).astype(o_ref.dtype)

        _exchange_wait(ins, gouts, *xsems, gather=True, when=pl.program_id(0) == NT - 1)

    row = lambda wd: pl.BlockSpec((TM, wd), lambda i: (i, 0))
    anyspec = pl.BlockSpec(memory_space=pl.ANY)
    res = pl.pallas_call(
        body, name="in_proj", grid=(NT,),
        in_specs=[pl.BlockSpec((g, D), m) for m in xmaps] + [pl.BlockSpec((PREFIX, D), lambda i: (0, 0)), pl.BlockSpec((1, D), lambda i: (0, 0)), anyspec]
        + [anyspec] * n,
        out_specs=[row(D), row(D)] + [row(wd) for _, wd, _, _ in segs] + [anyspec] * n,
        out_shape=[SDS((L, D), f32), SDS((L, D), bf16)] + [SDS((L, wd), dt) for _, wd, dt, _ in segs]
        + [SDS((N_DEV,) + a.shape, a.dtype) for a in shards],
        scratch_shapes=[pltpu.VMEM((NP, D), bf16), pltpu.SemaphoreType.DMA((1,))] + _exchange_sems(n),
        compiler_params=_cp())(*([x] * nb), pre, w1, wpt, *shards)
    return res[:2 + ns], res[2 + ns:]


def _in_proj_bwd(dgroups, wpt, h0, w1, dh1, slabs):
    L, D = h0.shape
    NP = wpt.shape[0]
    TM = _pick(L, ROW_TILES)
    NT = L // TM
    widths = [g.shape[1] for g in dgroups]
    assert sum(widths) == NP
    ng, n = len(dgroups), len(slabs)

    def body(*refs):
        dg_refs, (wp_hbm, h_ref, w1_ref, dh1_ref) = refs[:ng], refs[ng:ng + 4]
        ins, (dh0_ref, acc_ref), outs = refs[ng + 4:ng + 4 + n], refs[ng + 4 + n:ng + 6 + n], refs[ng + 6 + n:ng + 6 + 2 * n]
        wp_v, sems = refs[ng + 6 + 2 * n:ng + 8 + 2 * n]
        xsems = refs[ng + 8 + 2 * n:]
        _load_once([(wp_hbm, wp_v)], sems)

        @pl.when(pl.program_id(0) == 0)
        def _():
            acc_ref[...] = jnp.zeros_like(acc_ref)

        _exchange_start(ins, outs, *xsems, gather=False, when=pl.program_id(0) == 0)

        dxn, off = None, 0
        for g_ref, wd in zip(dg_refs, widths):
            part = _dot(g_ref[...], wp_v[off:off + wd, :])
            dxn = part if dxn is None else dxn + part
            off += wd
        _, vjp = jax.vjp(_rms, h_ref[...], w1_ref[...])
        dh0n, dw1 = vjp(dxn)
        dh0_ref[...] = dh1_ref[...] + dh0n
        acc_ref[0:1, :] += dw1

        _exchange_wait(ins, outs, *xsems, gather=False, when=pl.program_id(0) == NT - 1)

    row = lambda wd: pl.BlockSpec((TM, wd), lambda i: (i, 0))
    anyspec = pl.BlockSpec(memory_space=pl.ANY)
    res = pl.pallas_call(
        body, name="in_proj_bwd", grid=(NT,),
        in_specs=[row(wd) for wd in widths] + [anyspec, row(D), pl.BlockSpec((1, D), lambda i: (0, 0)), row(D)] + [anyspec] * n,
        out_specs=[row(D), pl.BlockSpec((8, D), lambda i: (0, 0))] + [anyspec] * n,
        out_shape=[SDS((L, D), f32), SDS((8, D), f32)] + [SDS(a.shape, a.dtype) for a in slabs],
        scratch_shapes=[pltpu.VMEM((NP, D), bf16), pltpu.SemaphoreType.DMA((1,))] + _exchange_sems(n),
        compiler_params=_cp())(*dgroups, wpt, h0, w1, dh1, *slabs)
    return res[0], res[1], res[2:]


def _matmul_tn(a, b, name):
    L, R = a.shape
    C = b.shape[1]
    br = max(k for k in range(HP, MAX_WGRAD_BLOCK + 1, HP) if R % k == 0)
    tk = max(k for k in range(16, MAX_WGRAD_TOKENS + 1, 16) if L % k == 0)
    KC = L // tk

    def body(a_ref, b_ref, o_ref, acc):
        k = pl.program_id(1)
        part = _dot_tn(a_ref[...], b_ref[...])

        @pl.when(k == 0)
        def _():
            acc[...] = part

        @pl.when(k > 0)
        def _():
            acc[...] += part

        @pl.when(k == KC - 1)
        def _():
            o_ref[...] = _b(acc[...])

    return pl.pallas_call(
        body, name=name, grid=(R // br, KC),
        in_specs=[pl.BlockSpec((tk, br), lambda r, k: (k, r)), pl.BlockSpec((tk, C), lambda r, k: (k, 0))],
        out_specs=pl.BlockSpec((br, C), lambda r, k: (r, 0)), out_shape=SDS((R, C), bf16),
        scratch_shapes=[pltpu.VMEM((br, C), f32)], compiler_params=_cp(2))(a, b)


def _fox_prep(fq, fk, fv, sf, bias_p):
    L = fq.shape[0]
    T = HP
    NT = L // T
    W = FOX_H * HP

    def body(fq_ref, fk_ref, fv_ref, sf_ref, b_ref, qa_ref, ka_ref, va_ref, carry):
        @pl.when(pl.program_id(0) == 0)
        def _():
            carry[...] = jnp.zeros_like(carry)

        lane, row = _iota((T, HP), 1), _iota((T, HP), 0)
        logf = jnp.where(lane < FOX_H, jax.nn.log_sigmoid(sf_ref[...] + b_ref[...]), 0.0)
        c = _dot_hi((row >= lane).astype(f32), logf) + carry[...]
        carry[...] = jnp.sum(jnp.where(row == T - 1, c, 0.0), axis=0, keepdims=True)
        ones_q = jnp.where((lane >= C_K0) & (lane < C_K0 + 3), 1.0, 0.0)
        ones_k = jnp.where(((lane >= C_Q0) & (lane < C_Q0 + 3)) | ((lane >= C_LSE0) & (lane < C_LSE0 + 3)), 1.0, 0.0)
        ones_v = _b(jnp.where((lane >= LSE_COL) & (lane < C_DELTA0 + 3), 1.0, 0.0))
        for h in range(FOX_H):
            ch = jnp.broadcast_to(jnp.sum(jnp.where(lane == h, c, 0.0), axis=1, keepdims=True), (T, HP)) * LOG2E
            c1 = _b(ch).astype(f32)
            c2 = _b(ch - c1).astype(f32)
            c3 = _b(ch - c1 - c2).astype(f32)
            cq = jnp.where(lane == C_Q0, c1, 0.0) + jnp.where(lane == C_Q0 + 1, c2, 0.0) + jnp.where(lane == C_Q0 + 2, c3, 0.0)
            ck = jnp.where(lane == C_K0, c1, 0.0) + jnp.where(lane == C_K0 + 1, c2, 0.0) + jnp.where(lane == C_K0 + 2, c3, 0.0)
            q = fq_ref[:, h * HP:(h + 1) * HP].astype(f32) * (FOX_D ** -0.5 * LOG2E)
            k = fk_ref[:, h * HP:(h + 1) * HP].astype(f32)
            qa_ref[h] = _b(q + cq + ones_q)
            ka_ref[h] = _b(k + ones_k - ck)
            va_ref[:, h * HP:(h + 1) * HP] = fv_ref[:, h * HP:(h + 1) * HP] + ones_v

    wide = pl.BlockSpec((T, W), lambda i: (i, 0))
    return pl.pallas_call(
        body, name="fox_prep", grid=(NT,),
        in_specs=[wide, wide, wide, pl.BlockSpec((T, HP), lambda i: (i, 0)), pl.BlockSpec((1, HP), lambda i: (0, 0))],
        out_specs=[pl.BlockSpec((FOX_H, T, HP), lambda i: (0, i, 0))] * 2 + [wide],
        out_shape=[SDS((FOX_H, L, HP), bf16)] * 2 + [SDS((L, W), bf16)],
        scratch_shapes=[pltpu.VMEM((1, HP), f32)], compiler_params=_cp())(fq, fk, fv, sf, bias_p)


def _fox_prep_bwd(dqa, dka, sf, bias_p):
    L = sf.shape[0]
    T = HP
    NT = L // T
    rev = lambda i: (NT - 1 - i, 0)

    W = FOX_H * HP

    def body(dq_ref, dk_ref, sf_ref, b_ref, dg_ref, db_ref, carry):
        @pl.when(pl.program_id(0) == 0)
        def _():
            carry[...] = jnp.zeros_like(carry)
            db_ref[...] = jnp.zeros_like(db_ref)

        dq, dk = dq_ref[...], dk_ref[...]
        dg_ref[:, 0:W] = _b(dq * (FOX_D ** -0.5))
        dg_ref[:, W:2 * W] = _b(dk * LN2)
        lane, row = _iota((T, HP), 1), _iota((T, HP), 0)
        dc = jnp.zeros((T, HP), f32)
        for h in range(FOX_H):
            col = jnp.sum(jnp.where(lane == C_Q0, dq[:, h * HP:(h + 1) * HP], 0.0)
                          - jnp.where(lane == C_K0, dk[:, h * HP:(h + 1) * HP], 0.0), axis=1, keepdims=True)
            dc = dc + jnp.where(lane == h, col, 0.0)
        dl = _dot_hi((row <= lane).astype(f32), dc) + carry[...]
        carry[...] = jnp.sum(jnp.where(row == 0, dl, 0.0), axis=0, keepdims=True)
        dx = jnp.where(lane < FOX_H, dl * _sigmoid(-(sf_ref[...] + b_ref[...])), 0.0)
        dg_ref[:, 2 * W:2 * W + HP] = _b(dx)
        dg_ref[:, 2 * W + HP:] = jnp.zeros((T, SMALL_W - HP), bf16)
        db_ref[0:1, :] += jnp.sum(dx, axis=0, keepdims=True)

    return pl.pallas_call(
        body, name="fox_prep_bwd", grid=(NT,),
        in_specs=[pl.BlockSpec((T, W), rev), pl.BlockSpec((T, W), rev), pl.BlockSpec((T, HP), rev), pl.BlockSpec((1, HP), lambda i: (0, 0))],
        out_specs=[pl.BlockSpec((T, 2 * W + SMALL_W), rev), pl.BlockSpec((8, HP), lambda i: (0, 0))],
        out_shape=[SDS((L, 2 * W + SMALL_W), bf16), SDS((8, HP), f32)],
        scratch_shapes=[pltpu.VMEM((1, HP), f32)], compiler_params=_cp())(dqa, dka, sf, bias_p)


def _tile_start(j, T):
    return j * T if isinstance(j, int) else pl.multiple_of(j * T, T)


def _spread3(x, lane, col0):
    x1 = _b(x).astype(f32)
    x2 = _b(x - x1).astype(f32)
    x3 = _b(x - x1 - x2).astype(f32)
    return jnp.where(lane == col0, x1, 0.0) + jnp.where(lane == col0 + 1, x2, 0.0) + jnp.where(lane == col0 + 2, x3, 0.0)


def _fox_fwd(qa, ka, fv, shards):
    L = qa.shape[1]
    TQ = TK = _pick(L, ATTN_TILES)
    NQ = L // TQ
    n = len(shards)
    HG = FOX_HEAD_GROUP_FWD

    def body(q_ref, k_ref, v_ref, *rest):
        ins, o_ref, qb_ref, outs, sems = rest[:n], rest[n], rest[n + 1], rest[n + 2:2 * n + 2], rest[2 * n + 2:]
        h, i = pl.program_id(0), pl.program_id(1)

        _exchange_start(ins, outs, *sems, gather=True, when=(h == 0) & (i == 0))

        qs = [q_ref[a] for a in range(HG)]
        rowg = i * TQ + _iota((TQ, TK), 0)
        colb = _iota((TQ, TK), 1)

        def step(j, carry, masked):
            ms, accs = carry
            k0 = _tile_start(j, TK)
            ss = [_dot_nt(qs[a], k_ref[a, pl.ds(k0, TK), :]) for a in range(HG)]
            if masked:
                colg = colb + j * TK
                keep = (colg <= rowg) & (colg >= N_PAD)
                ss = [jnp.where(keep, s, NEG) for s in ss]
            m_new = [jnp.maximum(m, jnp.max(s, axis=1, keepdims=True)) for m, s in zip(ms, ss)]
            ps = [_b(jnp.exp2(s - m)) for s, m in zip(ss, m_new)]
            alphas = [jnp.exp2(m - mn) for m, mn in zip(ms, m_new)]
            accs = [al * acc + _dot(p, v_ref[pl.ds(k0, TK), a * HP:(a + 1) * HP]) for a, (al, acc, p) in enumerate(zip(alphas, accs, ps))]
            return m_new, accs

        init = ([jnp.full((TQ, 1), NEG, f32)] * HG, [jnp.zeros((TQ, HP), f32)] * HG)
        carry = step(0, init, True)
        carry = lax.fori_loop(1, i, functools.partial(step, masked=False), carry)
        ms, accs = lax.fori_loop(jnp.maximum(i, 1), i + 1, functools.partial(step, masked=True), carry)
        lane = _iota((TQ, HP), 1)
        for a in range(HG):
            l = jnp.sum(jnp.where(lane == LSE_COL, accs[a], 0.0), axis=1, keepdims=True)
            lse = ms[a] + jnp.log2(l)
            o_ref[:, a * HP:(a + 1) * HP] = jnp.where(lane == LSE_COL, lse, accs[a] / l)
            qb_ref[a] = _b(qs[a].astype(f32) - _spread3(jnp.broadcast_to(lse, (TQ, HP)), lane, C_LSE0))

        _exchange_wait(ins, outs, *sems, gather=True, when=(h == FOX_H // HG - 1) & (i == NQ - 1))

    anyspec = pl.BlockSpec(memory_space=pl.ANY)
    qtile = pl.BlockSpec((HG, TQ, HP), lambda h, i: (h, i, 0))
    res = pl.pallas_call(
        body, name="fox_fwd", grid=(FOX_H // HG, NQ),
        in_specs=[qtile, pl.BlockSpec((HG, L, HP), lambda h, i: (h, 0, 0)), pl.BlockSpec((L, HG * HP), lambda h, i: (0, h))] + [anyspec] * n,
        out_specs=[pl.BlockSpec((TQ, HG * HP), lambda h, i: (i, h)), qtile] + [anyspec] * n,
        out_shape=[SDS((L, FOX_H * HP), f32), SDS(qa.shape, bf16)] + [SDS((N_DEV,) + a.shape, a.dtype) for a in shards],
        scratch_shapes=_exchange_sems(n), compiler_params=_cp(2))(qa, ka, fv, *shards)
    return res[0], res[1], res[2:]


def _fox_bwd(qb, ka, va, dob, slabs):
    L = qb.shape[1]
    TQ = TK = _pick(L, ATTN_TILES)
    NQ = L // TQ
    n = len(slabs)
    HG = FOX_HEAD_GROUP

    def body(q_ref, k_ref, v_ref, do_ref, *rest):
        ins, (dq_ref, dk_ref, dv_ref), outs, sems = rest[:n], rest[n:n + 3], rest[n + 3:2 * n + 3], rest[2 * n + 3:]
        h, j = pl.program_id(0), pl.program_id(1)
        cols = [slice(a * HP, (a + 1) * HP) for a in range(HG)]

        _exchange_start(ins, outs, *sems, gather=False, when=(h == 0) & (j == 0))

        @pl.when(j == 0)
        def _():
            dq_ref[...] = jnp.zeros_like(dq_ref)

        kts = [k_ref[a] for a in range(HG)]
        vts = [v_ref[:, cols[a]] for a in range(HG)]
        colg = j * TK + _iota((TQ, TK), 1)
        rowb = _iota((TQ, TK), 0)

        def step(i, carry, masked):
            dks, dvs = carry
            r0 = _tile_start(i, TQ)
            rows = pl.ds(r0, TQ)
            qs = [q_ref[a, rows, :] for a in range(HG)]
            ps = [jnp.exp2(_dot_nt(q, kt)) for q, kt in zip(qs, kts)]
            if masked:
                keep = (colg <= rowb + i * TQ) & (colg >= N_PAD)
                ps = [jnp.where(keep, p, 0.0) for p in ps]
            dobs = [do_ref[rows, cols[a]] for a in range(HG)]
            dvs = [dv + _dot_tn(dob, _b(p)) for dv, p, dob in zip(dvs, ps, dobs)]
            dss = [_b(p * _dot_nt(dob, vt)) for p, dob, vt in zip(ps, dobs, vts)]
            for a in range(HG):
                dq_ref[rows, cols[a]] += _dot(dss[a], kts[a])
            dks = [dk + _dot_tn(q, ds) for dk, ds, q in zip(dks, dss, qs)]
            return dks, dvs

        zeros = [jnp.zeros((HP, TK), f32)] * HG
        carry = step(j, (zeros, zeros), True)
        split = jnp.where(j == 0, NQ, j + 1)
        carry = lax.fori_loop(j + 1, split, functools.partial(step, masked=True), carry)
        dks, dvs = lax.fori_loop(split, NQ, functools.partial(step, masked=False), carry)
        for a in range(HG):
            dk_ref[:, cols[a]] = dks[a].T
            dv_ref[:, cols[a]] = _b(dvs[a].T)

        _exchange_wait(ins, outs, *sems, gather=False, when=(h == FOX_H // HG - 1) & (j == NQ - 1))

    head = pl.BlockSpec((L, HG * HP), lambda h, j: (0, h))
    tile = pl.BlockSpec((TK, HG * HP), lambda h, j: (j, h))
    anyspec = pl.BlockSpec(memory_space=pl.ANY)
    res = pl.pallas_call(
        body, name="fox_bwd", grid=(FOX_H // HG, L // TK),
        in_specs=[pl.BlockSpec((HG, L, HP), lambda h, j: (h, 0, 0)), pl.BlockSpec((HG, TK, HP), lambda h, j: (h, j, 0)), tile, head]
        + [anyspec] * n,
        out_specs=[head, tile, tile] + [anyspec] * n,
        out_shape=[SDS((L, FOX_H * HP), f32), SDS((L, FOX_H * HP), f32), SDS((L, FOX_H * HP), bf16)] + [SDS(a.shape, a.dtype) for a in slabs],
        scratch_shapes=_exchange_sems(n), compiler_params=_cp(2))(qb, ka, va, dob, *slabs)
    return res[:3], res[3:]


def _dn_post(y, sd, alog_p, dt_p, valid):
    a = y * _sigmoid(y)
    W = DN_H * DN_D
    heads = []
    for part, scale in ((0, DN_D ** -0.5), (1, 1.0)):
        for h in range(DN_H):
            xh = a[:, part * W + h * DN_D:part * W + (h + 1) * DN_D]
            heads.append(xh * lax.rsqrt(jnp.sum(xh * xh, axis=-1, keepdims=True) + EPS) * scale)
    q = jnp.concatenate(heads[:DN_H], axis=1)
    k = jnp.concatenate(heads[DN_H:], axis=1)
    v = a[:, 2 * W:3 * W]
    lane = _iota(sd.shape, 1)
    beta = _sigmoid(sd) * valid
    g = -jnp.exp(alog_p) * jax.nn.softplus(sd + dt_p) * valid
    bg = jnp.where(lane < DN_H, beta, jnp.where(lane < 2 * DN_H, g, 0.0))
    return q, k, v, bg


def _conv_fwd(ext_ref, cw_ref, TM):
    y = cw_ref[0:1, :] * ext_ref[8 - (CONV_K - 1):8 - (CONV_K - 1) + TM, :]
    for i in range(1, CONV_K):
        o = 8 - (CONV_K - 1) + i
        y = y + cw_ref[i:i + 1, :] * ext_ref[o:o + TM, :]
    return y


def _dn_prep(dn, sd, cw, alog_p, dt_p):
    L, W3 = dn.shape
    TM = _pick(L, ROW_TILES)
    W = DN_H * DN_D

    def body(dn_ref, halo_ref, sd_ref, cw_ref, al_ref, dt_ref, q_ref, k_ref, v_ref, bg_ref, ext):
        i = pl.program_id(0)
        ext[0:8, :] = jnp.where(i == 0, 0.0, halo_ref[...])
        ext[8:, :] = dn_ref[...]
        y = _conv_fwd(ext, cw_ref, TM)
        valid = ((i * TM + _iota((TM, 1), 0)) >= N_PAD).astype(f32)
        q, k, v, bg = _dn_post(y, sd_ref[...], al_ref[...], dt_ref[...], valid)
        q_ref[...], k_ref[...], v_ref[...], bg_ref[...] = q, k, v, bg

    row = lambda wd: pl.BlockSpec((TM, wd), lambda i: (i, 0))
    vec = pl.BlockSpec((1, HP), lambda i: (0, 0))
    return pl.pallas_call(
        body, name="dn_prep", grid=(L // TM,),
        in_specs=[row(W3), pl.BlockSpec((8, W3), lambda i: (jnp.maximum(i * (TM // 8) - 1, 0), 0)), row(HP),
                  pl.BlockSpec((CONV_K, W3), lambda i: (0, 0)), vec, vec],
        out_specs=[row(W), row(W), row(W), row(HP)],
        out_shape=[SDS((L, W), f32)] * 3 + [SDS((L, HP), f32)],
        scratch_shapes=[pltpu.VMEM((TM + 8, W3), f32)], compiler_params=_cp())(dn, dn, sd, cw, alog_p, dt_p)


def _dn_prep_bwd(dn, sd, cw, alog_p, dt_p, dq, dk, dv, dbg):
    L, W3 = dn.shape
    TM = _pick(L, ROW_TILES)
    NT = L // TM
    W = DN_H * DN_D

    def body(dn_ref, halo_ref, sd_ref, cw_ref, al_ref, dt_ref, dq_ref, dk_ref, dv_ref, dbg_ref,
             dg_ref, dcw_ref, dp_ref, ext, dyp, carry):
        i = pl.program_id(0)
        t = NT - 1 - i

        @pl.when(i == 0)
        def _():
            carry[...] = jnp.zeros_like(carry)
            dcw_ref[...] = jnp.zeros_like(dcw_ref)
            dp_ref[...] = jnp.zeros_like(dp_ref)
            dyp[...] = jnp.zeros_like(dyp)

        ext[0:8, :] = jnp.where(t == 0, 0.0, halo_ref[...])
        ext[8:, :] = dn_ref[...]
        y = _conv_fwd(ext, cw_ref, TM)
        valid = ((t * TM + _iota((TM, 1), 0)) >= N_PAD).astype(f32)
        _, vjp = jax.vjp(functools.partial(_dn_post, valid=valid), y, sd_ref[...], al_ref[...], dt_ref[...])
        dy, dsd, dal, ddt = vjp((dq_ref[...], dk_ref[...], dv_ref[...], dbg_ref[...]))
        dg_ref[:, W3:W3 + HP] = _b(dsd)
        dg_ref[:, W3 + HP:] = jnp.zeros((TM, SMALL_W - HP), bf16)
        dp_ref[0:1, :] += dal
        dp_ref[1:2, :] += ddt
        dyp[8:8 + TM, :] = dy
        o0 = CONV_K - 1
        dext = cw_ref[0:1, :] * dyp[o0:o0 + TM + 8, :]
        for k in range(1, CONV_K):
            dext = dext + cw_ref[k:k + 1, :] * dyp[o0 - k:o0 - k + TM + 8, :]
        for k in range(CONV_K):
            o = 8 - (CONV_K - 1) + k
            dcw_ref[k:k + 1, :] += jnp.sum(dy * ext[o:o + TM, :], axis=0, keepdims=True)
        dg_ref[:, 0:W3] = _b(jnp.concatenate([dext[8:TM, :], dext[TM:TM + 8, :] + carry[...]], axis=0))
        carry[...] = dext[0:8, :]

    row = lambda wd: pl.BlockSpec((TM, wd), lambda i: (NT - 1 - i, 0))
    vec = pl.BlockSpec((1, HP), lambda i: (0, 0))
    return pl.pallas_call(
        body, name="dn_prep_bwd", grid=(NT,),
        in_specs=[row(W3), pl.BlockSpec((8, W3), lambda i: (jnp.maximum((NT - 1 - i) * (TM // 8) - 1, 0), 0)), row(HP),
                  pl.BlockSpec((CONV_K, W3), lambda i: (0, 0)), vec, vec, row(W), row(W), row(W), row(HP)],
        out_specs=[row(W3 + SMALL_W), pl.BlockSpec((8, W3), lambda i: (0, 0)), pl.BlockSpec((8, HP), lambda i: (0, 0))],
        out_shape=[SDS((L, W3 + SMALL_W), bf16), SDS((8, W3), f32), SDS((8, HP), f32)],
        scratch_shapes=[pltpu.VMEM((TM + 8, W3), f32), pltpu.VMEM((TM + 16, W3), f32), pltpu.VMEM((8, W3), f32)],
        compiler_params=_cp())(dn, dn, sd, cw, alog_p, dt_p, dq, dk, dv, dbg)


def _split2(x):
    hi = _b(x)
    return hi, _b(x - hi.astype(f32))


def _split3(x):
    hi = _b(x)
    r = x - hi.astype(f32)
    mid = _b(r)
    return hi, mid, _b(r - mid.astype(f32))


def _x3(a, b, dot):
    (a1, a2), (b1, b2) = _split2(a), _split2(b)
    return dot(a1, b1) + (dot(a1, b2) + dot(a2, b1))


@jax.custom_vjp
def _dot_x3(a, b):
    return _x3(a, b, _dot)


_dot_x3.defvjp(lambda a, b: (_x3(a, b, _dot), (a, b)), lambda res, g: (_x3(g, res[1], _dot_nt), _x3(res[0], g, _dot_tn)))


def _exact3(m, x, dot):
    x1, x2, x3 = _split3(x)
    return dot(m, x1) + (dot(m, x2) + dot(m, x3))


def _tri_ones(C, lower):
    row, col = _iota((C, C), 0), _iota((C, C), 1)
    return _b(((row >= col) if lower else (row <= col)).astype(f32))


@jax.custom_vjp
def _chunk_cumsum(x):
    return _exact3(_tri_ones(x.shape[0], True), x, _dot)


_chunk_cumsum.defvjp(lambda x: (_exact3(_tri_ones(x.shape[0], True), x, _dot), None),
                     lambda _, g: (_exact3(_tri_ones(g.shape[0], False), g, _dot),))


def _mxu_transpose(x):
    C = x.shape[0]
    eye = _b((_iota((C, C), 0) == _iota((C, C), 1)).astype(f32))
    return _exact3(eye, x, lambda m, part: _dot_tn(part, m))


@jax.custom_vjp
def _transpose_exact(x):
    return _mxu_transpose(x)


_transpose_exact.defvjp(lambda x: (_mxu_transpose(x), None), lambda _, g: (_mxu_transpose(g),))


def _unit_lower_inverses(lows):
    C = lows[0].shape[0]
    P = jnp.stack(lows)
    X = (_iota((C, C), 0) == _iota((C, C), 1)).astype(f32)[None] - P
    bdot = functools.partial(_x3, dot=lambda a, b: jnp.einsum("bij,bjk->bik", a, b, preferred_element_type=f32))
    for _ in range(5):
        P = bdot(P, P)
        X = X + bdot(X, P)
    return [X[i] for i in range(len(lows))]


@jax.custom_vjp
def _inverse_given(low, X):
    return X


def _inverse_given_bwd(X, g):
    return -_x3(_x3(X, g, _dot_tn), X, _dot_nt), jnp.zeros_like(X)


_inverse_given.defvjp(lambda low, X: (X, X), _inverse_given_bwd)


def _dn_intra_pre(q, k, v, bg):
    C = DN_C
    row, col = _iota((C, C), 0), _iota((C, C), 1)
    tri = row >= col
    G = _chunk_cumsum(bg)
    GT = _transpose_exact(G)
    lane = _iota((C, HP), 1)
    rowt = _iota((HP, C), 0)
    last = _iota((C, 1), 0) == C - 1
    heads = []
    for h in range(DN_H):
        beta = jnp.sum(jnp.where(lane == h, bg, 0.0), axis=1, keepdims=True)
        gcol = jnp.sum(jnp.where(lane == DN_H + h, G, 0.0), axis=1, keepdims=True)
        grow = jnp.sum(jnp.where(rowt == DN_H + h, GT, 0.0), axis=0, keepdims=True)
        glast = jnp.sum(jnp.where(last, gcol, 0.0), axis=0, keepdims=True)
        decay = jnp.exp(jnp.where(tri, gcol - grow, NEG))
        qh, kh, vh = (t[:, h * DN_D:(h + 1) * DN_D] for t in (q, k, v))
        kb = kh * beta
        low = jnp.where(row > col, _dot_nt(_b(kb), _b(kh)) * decay, 0.0)
        heads.append((beta, gcol, glast, decay, qh, kh, vh, kb, low))
    return heads


def _dn_intra_post(heads, xs):
    lane1 = _iota((1, HP), 1)
    us, ws, qds, kds, attns = [], [], [], [], []
    glrow = jnp.zeros((1, HP), f32)
    for h, ((beta, gcol, glast, decay, qh, kh, vh, kb, _), X) in enumerate(zip(heads, xs)):
        eg = jnp.exp(gcol)
        us.append(_dot_x3(X, vh * beta))
        ws.append(_dot_x3(X, kb * eg))
        attns.append(_dot_nt(_b(qh), _b(kh)) * decay)
        qds.append(qh * eg)
        kds.append(kh * jnp.exp(glast - gcol))
        glrow = glrow + jnp.where(lane1 == h, glast, 0.0)
    cat = lambda xs_: jnp.concatenate(xs_, axis=1)
    return cat(us), cat(ws), cat(qds), cat(kds), cat(attns), glrow, cat(list(xs))


def _dn_intra_group(q, k, v, bg, xs):
    G = q.shape[0] // DN_C
    rows = [slice(j * DN_C, (j + 1) * DN_C) for j in range(G)]
    pre = [_dn_intra_pre(q[r, :], k[r, :], v[r, :], bg[r, :]) for r in rows]
    inv = [[_inverse_given(hd[-1], x) for hd, x in zip(heads, xj)] for heads, xj in zip(pre, xs)]
    post = [_dn_intra_post(heads, xj) for heads, xj in zip(pre, inv)]
    return tuple(jnp.concatenate([p[i] for p in post], axis=0) for i in range(5)) + (tuple(p[5] for p in post),)


def _lane_pick(rowvec, h):
    return jnp.sum(jnp.where(_iota(rowvec.shape, 1) == h, rowvec, 0.0), axis=1, keepdims=True)


def _dn_intra(q, k, v, bg, shards):
    L, W = q.shape
    NC = L // DN_C
    G = _pick(NC, DN_INTRA_GROUP)
    R = G * DN_C
    NS = NC // G
    WA = DN_H * DN_C
    n = len(shards)

    def body(q_ref, k_ref, v_ref, bg_ref, *rest):
        ins, (u_ref, w_ref, qd_ref, kd_ref, at_ref, gl_ref, x_ref), gouts, sems = rest[:n], rest[n:n + 7], rest[n + 7:2 * n + 7], rest[2 * n + 7:]
        _exchange_start(ins, gouts, *sems, gather=True, when=pl.program_id(0) == 0)
        rows = [slice(j * DN_C, (j + 1) * DN_C) for j in range(G)]
        pre = [_dn_intra_pre(q_ref[r, :], k_ref[r, :], v_ref[r, :], bg_ref[r, :]) for r in rows]
        inv = _unit_lower_inverses([hd[-1] for heads in pre for hd in heads])
        for j, r in enumerate(rows):
            u, w, qd, kd, at, gl, xs = _dn_intra_post(pre[j], inv[j * DN_H:(j + 1) * DN_H])
            u_ref[r, :], x_ref[r, :] = u, xs
            w_ref[r, :], qd_ref[r, :], kd_ref[r, :], at_ref[r, :] = _b(w), _b(qd), _b(kd), _b(at)
            gl_ref[j] = gl
        _exchange_wait(ins, gouts, *sems, gather=True, when=pl.program_id(0) == NS - 1)

    row = lambda wd: pl.BlockSpec((R, wd), lambda s: (s, 0))
    anyspec = pl.BlockSpec(memory_space=pl.ANY)
    res = pl.pallas_call(
        body, name="dn_intra", grid=(NS,),
        in_specs=[row(W), row(W), row(W), row(HP)] + [anyspec] * n,
        out_specs=[row(W), row(W), row(W), row(W), row(WA), pl.BlockSpec((G, 1, HP), lambda s: (s, 0, 0)), row(WA)] + [anyspec] * n,
        out_shape=[SDS((L, W), f32), SDS((L, W), bf16), SDS((L, W), bf16), SDS((L, W), bf16), SDS((L, WA), bf16), SDS((NC, 1, HP), f32),
                   SDS((L, WA), f32)] + [SDS((N_DEV,) + a.shape, a.dtype) for a in shards],
        scratch_shapes=_exchange_sems(n), compiler_params=_cp())(q, k, v, bg, *shards)
    return res[:7], res[7:]


def _dn_scan(u, w, qd, kd, at, gl):
    L, W = u.shape
    NC = L // DN_C
    G = _pick(NC, DN_SCAN_GROUP)
    R = G * DN_C

    def body(u_ref, w_ref, qd_ref, kd_ref, at_ref, gl_ref, o_ref, vn_ref, s_ref, S):
        @pl.when(pl.program_id(0) == 0)
        def _():
            S[...] = jnp.zeros_like(S)

        for j in range(G):
            r = slice(j * DN_C, (j + 1) * DN_C)
            glrow = gl_ref[j]
            for h in range(DN_H):
                c = slice(h * DN_D, (h + 1) * DN_D)
                Sh = S[h]
                s_ref[j, h] = Sh
                Sb = _b(Sh)
                vb = _b(u_ref[r, c] - _dot(w_ref[r, c], Sb))
                vn_ref[r, c] = vb
                o_ref[r, c] = _dot(qd_ref[r, c], Sb) + _dot(at_ref[r, h * DN_C:(h + 1) * DN_C], vb)
                S[h] = Sh * jnp.exp(_lane_pick(glrow, h)) + _dot_tn(kd_ref[r, c], vb)

    row = lambda wd: pl.BlockSpec((R, wd), lambda n: (n, 0))
    return pl.pallas_call(
        body, name="dn_scan", grid=(NC // G,),
        in_specs=[row(W), row(W), row(W), row(W), row(DN_H * DN_C), pl.BlockSpec((G, 1, HP), lambda n: (n, 0, 0))],
        out_specs=[row(W), row(W), pl.BlockSpec((G, DN_H, DN_D, DN_D), lambda n: (n, 0, 0, 0))],
        out_shape=[SDS((L, W), f32), SDS((L, W), bf16), SDS((NC, DN_H, DN_D, DN_D), f32)],
        scratch_shapes=[pltpu.VMEM((DN_H, DN_D, DN_D), f32)], compiler_params=_cp())(u, w, qd, kd, at, gl)


def _dn_scan_bwd(do, w, qd, kd, at, gl):
    L, W = do.shape
    NC = L // DN_C
    G = _pick(NC, DN_SCAN_GROUP)
    R = G * DN_C
    NS = NC // G

    def body(do_ref, w_ref, qd_ref, kd_ref, at_ref, gl_ref, dvn_ref, ds_ref, dS):
        @pl.when(pl.program_id(0) == 0)
        def _():
            dS[...] = jnp.zeros_like(dS)

        for j in reversed(range(G)):
            r = slice(j * DN_C, (j + 1) * DN_C)
            glrow = gl_ref[j]
            for h in range(DN_H):
                c = slice(h * DN_D, (h + 1) * DN_D)
                dSo = dS[h]
                ds_ref[j, h] = dSo
                dob = _b(do_ref[r, c])
                dvn = _dot_tn(at_ref[r, h * DN_C:(h + 1) * DN_C], dob) + _dot(kd_ref[r, c], _b(dSo))
                dvn_ref[r, c] = dvn
                dS[h] = _dot_tn(qd_ref[r, c], dob) + dSo * jnp.exp(_lane_pick(glrow, h)) - _dot_tn(w_ref[r, c], _b(dvn))

    row = lambda wd: pl.BlockSpec((R, wd), lambda n: (NS - 1 - n, 0))
    return pl.pallas_call(
        body, name="dn_scan_bwd", grid=(NS,),
        in_specs=[row(W), row(W), row(W), row(W), row(DN_H * DN_C), pl.BlockSpec((G, 1, HP), lambda n: (NS - 1 - n, 0, 0))],
        out_specs=[row(W), pl.BlockSpec((G, DN_H, DN_D, DN_D), lambda n: (NS - 1 - n, 0, 0, 0))],
        out_shape=[SDS((L, W), f32), SDS((NC, DN_H, DN_D, DN_D), f32)],
        scratch_shapes=[pltpu.VMEM((DN_H, DN_D, DN_D), f32)], compiler_params=_cp())(do, w, qd, kd, at, gl)


def _dn_intra_bwd(q, k, v, bg, xinv, do, vn, dvn, states, dstates):
    L, W = q.shape
    NC = L // DN_C
    G = _pick(NC, DN_INTRA_GROUP)
    R = G * DN_C

    def body(q_ref, k_ref, v_ref, bg_ref, x_ref, do_ref, vn_ref, dvn_ref, s_ref, ds_ref, dq_ref, dk_ref, dv_ref, dbg_ref):
        lane1 = _iota((1, HP), 1)
        rows = [slice(j * DN_C, (j + 1) * DN_C) for j in range(G)]
        xs = [[x_ref[r, h * DN_C:(h + 1) * DN_C] for h in range(DN_H)] for r in rows]
        fwd, vjp = jax.vjp(functools.partial(_dn_intra_group, xs=xs), q_ref[...], k_ref[...], v_ref[...], bg_ref[...])
        dws, dqds, dkds, dats, dgls = [], [], [], [], []
        for j, r in enumerate(rows):
            dw, dqd, dkd, dat = [], [], [], []
            dgl = jnp.zeros((1, HP), f32)
            for h in range(DN_H):
                c = slice(h * DN_D, (h + 1) * DN_D)
                Sh, dSo = s_ref[j, h], ds_ref[j, h]
                Sb, dob, vb = _b(Sh), _b(do_ref[r, c]), vn_ref[r, c]
                dw.append(-_dot_nt(_b(dvn_ref[r, c]), Sb))
                dqd.append(_dot_nt(dob, Sb))
                dat.append(_dot_nt(dob, vb))
                dkd.append(_dot_nt(vb, _b(dSo)))
                dcd = jnp.sum(jnp.sum(Sh * dSo, axis=1, keepdims=True), axis=0, keepdims=True)
                dgl = dgl + jnp.where(lane1 == h, dcd * jnp.exp(_lane_pick(fwd[5][j], h)), 0.0)
            cat = lambda xs_: jnp.concatenate(xs_, axis=1)
            dws.append(cat(dw)), dqds.append(cat(dqd)), dkds.append(cat(dkd)), dats.append(cat(dat)), dgls.append(dgl)
        cat0 = lambda xs_: jnp.concatenate(xs_, axis=0)
        dq, dk, dv, dbg = vjp((dvn_ref[...], cat0(dws), cat0(dqds), cat0(dkds), cat0(dats), tuple(dgls)))
        dq_ref[...], dk_ref[...], dv_ref[...], dbg_ref[...] = dq, dk, dv, dbg

    row = lambda wd: pl.BlockSpec((R, wd), lambda s: (s, 0))
    st = pl.BlockSpec((G, DN_H, DN_D, DN_D), lambda s: (s, 0, 0, 0))
    return pl.pallas_call(
        body, name="dn_intra_bwd", grid=(NC // G,),
        in_specs=[row(W), row(W), row(W), row(HP), row(DN_H * DN_C), row(W), row(W), row(W), st, st],
        out_specs=[row(W), row(W), row(W), row(HP)],
        out_shape=[SDS((L, W), f32)] * 3 + [SDS((L, HP), f32)],
        compiler_params=_cp())(q, k, v, bg, xinv, do, vn, dvn, states, dstates)


def _dn_normgate(oraw, dz, wn):
    outs = []
    for h in range(DN_H):
        sl = slice(h * DN_D, (h + 1) * DN_D)
        z = dz[:, sl]
        outs.append(_rms(oraw[:, sl], wn) * (z * _sigmoid(z)))
    return jnp.concatenate(outs, axis=1)


def _mix_fwd(op, oraw, dz, ga, gb, h0, wn, wbf, wbd, wo):
    L, D = h0.shape
    TM = _pick(L, ROW_TILES)

    def body(op_ref, or_ref, dz_ref, ga_ref, gb_ref, h0_ref, wn_ref, wbf_ref, wbd_ref, wo_ref, h1_ref):
        pf = _dot(_b(op_ref[...]), wbf_ref[...])
        pd = _dot(_b(_dn_normgate(or_ref[...], dz_ref[...], wn_ref[...])), wbd_ref[...])
        y = _sigmoid(ga_ref[...]) * pf + _sigmoid(gb_ref[...]) * pd
        h1_ref[...] = h0_ref[...] + _dot(_b(y), wo_ref[...])

    row = lambda wd: pl.BlockSpec((TM, wd), lambda i: (i, 0))
    full = lambda a: pl.BlockSpec(a.shape, lambda i: (0, 0))
    return pl.pallas_call(
        body, name="mix_fwd", grid=(L // TM,),
        in_specs=[row(op.shape[1]), row(oraw.shape[1]), row(dz.shape[1]), row(D), row(D), row(D), full(wn), full(wbf), full(wbd), full(wo)],
        out_specs=row(D), out_shape=SDS((L, D), f32), compiler_params=_cp())(op, oraw, dz, ga, gb, h0, wn, wbf, wbd, wo)


def _mix_bwd(dh1, op, oraw, dz, ga, gb, wn, wbf, wbd, wo):
    L, D = dh1.shape
    TM = _pick(L, ROW_TILES)
    WF, WD = op.shape[1], oraw.shape[1]

    def body(dh1_ref, op_ref, or_ref, dz_ref, ga_ref, gb_ref, wn_ref, wbf_ref, wbd_ref, wo_ref,
             dop_ref, dor_ref, dg_ref, af_ref, ad_ref, dpf_ref, dpd_ref, y_ref, dmix_ref, acc_ref):
        @pl.when(pl.program_id(0) == 0)
        def _():
            acc_ref[...] = jnp.zeros_like(acc_ref)

        opv = op_ref[...]
        af = _b(opv)
        ad, vjp = jax.vjp(_dn_normgate, or_ref[...], dz_ref[...], wn_ref[...])
        adb = _b(ad)
        pf, pd = _dot(af, wbf_ref[...]), _dot(adb, wbd_ref[...])
        sa, sb = _sigmoid(ga_ref[...]), _sigmoid(gb_ref[...])
        dmix = _b(dh1_ref[...])
        dy = _dot_nt(dmix, wo_ref[...])
        dpf, dpd = _b(dy * sa), _b(dy * sb)
        dor, ddz, dwn = vjp(_dot_nt(dpd, wbd_ref[...]))
        dop = _dot_nt(dpf, wbf_ref[...])
        lane = _iota((TM, HP), 1)
        for h in range(WF // HP):
            c = slice(h * HP, (h + 1) * HP)
            delta = jnp.sum(jnp.where(lane < FOX_D, dop[:, c] * opv[:, c], 0.0), axis=1, keepdims=True)
            dop_ref[:, c] = _b(dop[:, c] - _spread3(jnp.broadcast_to(delta, (TM, HP)), lane, C_DELTA0))
        dor_ref[...] = dor
        dg_ref[:, 0:WD] = _b(ddz)
        dg_ref[:, WD:WD + D] = _b(dy * pf * sa * (1.0 - sa))
        dg_ref[:, WD + D:] = _b(dy * pd * sb * (1.0 - sb))
        af_ref[...], ad_ref[...], y_ref[...] = af, adb, _b(sa * pf + sb * pd)
        dpf_ref[...], dpd_ref[...], dmix_ref[...] = dpf, dpd, dmix
        acc_ref[0:1, :] += dwn

    row = lambda wd: pl.BlockSpec((TM, wd), lambda i: (i, 0))
    full = lambda a: pl.BlockSpec(a.shape, lambda i: (0, 0))
    return pl.pallas_call(
        body, name="mix_bwd", grid=(L // TM,),
        in_specs=[row(D), row(WF), row(WD), row(WD), row(D), row(D), full(wn), full(wbf), full(wbd), full(wo)],
        out_specs=[row(WF), row(WD), row(WD + 2 * D), row(WF), row(WD), row(D), row(D), row(D), row(D),
                   pl.BlockSpec((8, HP), lambda i: (0, 0))],
        out_shape=[SDS((L, WF), bf16), SDS((L, WD), f32), SDS((L, WD + 2 * D), bf16), SDS((L, WF), bf16), SDS((L, WD), bf16),
                   SDS((L, D), bf16), SDS((L, D), bf16), SDS((L, D), bf16), SDS((L, D), bf16), SDS((8, HP), f32)],
        compiler_params=_cp())(dh1, op, oraw, dz, ga, gb, wn, wbf, wbd, wo)


def _ffn_fwd_bwd(h1, tgt, w2, wf, wgt, wut, wd):
    L, D = h1.shape
    F = wd.shape[0]
    TM = _pick(L, FFN_TILES)
    gt, nb, _, tmaps = _shifted_blocks(TM)
    NC = F // FFN_CHUNK
    chunks = [slice(c * FFN_CHUNK, (c + 1) * FFN_CHUNK) for c in range(NC)]

    def body(h_ref, *refs):
        t_refs, (w2_ref, wf_ref, wg_hbm, wu_hbm, wd_hbm,
                 dh1_ref, xn_ref, dg_ref, du_ref, act_ref, dh2_ref, acc_ref, wg_v, wu_v, wd_v, sems, g_s, u_s) = refs[:nb], refs[nb:]
        i = pl.program_id(0)
        _load_once([(wg_hbm, wg_v), (wu_hbm, wu_v), (wd_hbm, wd_v)], sems)

        @pl.when(i == 0)
        def _():
            acc_ref[...] = jnp.zeros_like(acc_ref)

        h1v = h_ref[...]
        xn2, vjp2 = jax.vjp(_rms, h1v, w2_ref[...])
        xb = _b(xn2)
        up = lambda c: (_dot_nt(xb, wg_v[chunks[c], :]), _dot_nt(xb, wu_v[chunks[c], :]))
        h2, nxt = h1v, up(0)
        for c in range(NC):
            (g, u), nxt = nxt, (up(c + 1) if c + 1 < NC else None)
            g_s[:, chunks[c]], u_s[:, chunks[c]] = g, u
            ab = _b(g * _sigmoid(g) * u)
            act_ref[:, chunks[c]] = ab
            h2 = h2 + _dot(ab, wd_v[chunks[c], :])
        out, vjpf = jax.vjp(_rms, h2, wf_ref[...])
        valid = (i * TM + _iota((TM, 1), 0)) >= PREFIX
        diff = jnp.where(valid, out - jnp.concatenate([r[...] for r in t_refs], axis=0), 0.0)
        loss = 0.5 * jnp.sum(jnp.sum(diff * diff, axis=1, keepdims=True), axis=0, keepdims=True) / D
        dh2, dwf = vjpf(diff * (1.0 / D))
        dh2b = _b(dh2)
        down = lambda c: _dot_nt(dh2b, wd_v[chunks[c], :])
        dxn, nxt = None, down(0)
        for c in range(NC):
            dact, nxt = nxt, (down(c + 1) if c + 1 < NC else None)
            g, u = g_s[:, chunks[c]], u_s[:, chunks[c]]
            sg = _sigmoid(g)
            dgb = _b(dact * u * (sg * (1.0 + g * (1.0 - sg))))
            dub = _b(dact * (g * sg))
            dg_ref[:, chunks[c]], du_ref[:, chunks[c]] = dgb, dub
            part = _dot(dgb, wg_v[chunks[c], :]) + _dot(dub, wu_v[chunks[c], :])
            dxn = part if dxn is None else dxn + part
        dh1n, dw2 = vjp2(dxn)
        dh1_ref[...] = dh2 + dh1n
        xn_ref[...], dh2_ref[...] = xb, dh2b
        acc_ref[0:1, :] += dw2
        acc_ref[1:2, :] += dwf
        acc_ref[2:3, :] += jnp.broadcast_to(loss, (1, D))

    row = lambda wd_: pl.BlockSpec((TM, wd_), lambda i: (i, 0))
    once = lambda wd_: pl.BlockSpec((TM, wd_), lambda i: (i, 0), pipeline_mode=pl.Buffered(1))
    vec = pl.BlockSpec((1, D), lambda i: (0, 0))
    anyspec = pl.BlockSpec(memory_space=pl.ANY)
    return pl.pallas_call(
        body, name="ffn_fwd_bwd", grid=(L // TM,),
        in_specs=[row(D)] + [pl.BlockSpec((gt, D), m) for m in tmaps] + [vec, vec, anyspec, anyspec, anyspec],
        out_specs=[once(D), once(D), once(F), once(F), once(F), once(D), pl.BlockSpec((8, D), lambda i: (0, 0))],
        out_shape=[SDS((L, D), f32), SDS((L, D), bf16), SDS((L, F), bf16), SDS((L, F), bf16), SDS((L, F), bf16), SDS((L, D), bf16),
                   SDS((8, D), f32)],
        scratch_shapes=[pltpu.VMEM((F, D), bf16), pltpu.VMEM((F, D), bf16), pltpu.VMEM((F, D), bf16), pltpu.SemaphoreType.DMA((3,)),
                        pltpu.VMEM((TM, F), f32), pltpu.VMEM((TM, F), f32)],
        compiler_params=_cp())(h1, *([tgt] * nb), w2, wf, wgt, wut, wd)


def _pad_lanes(v, n=HP):
    return jnp.pad(v.astype(f32), ((0, 0), (0, n - v.shape[1])))


def _pack_w_in(wt_full):
    D = wt_full.shape[1]
    FW, DW = FOX_H * FOX_D, DN_H * DN_D
    o = 0
    parts = {}
    for name, wd in (("fq", FW), ("fk", FW), ("fv", FW), ("fl", FOX_H), ("dn", 3 * DW), ("ba", 2 * DN_H), ("dz", DW), ("ga", D), ("gb", D)):
        parts[name] = wt_full[o:o + wd]
        o += wd
    assert o == wt_full.shape[0]
    heads = lambda w: jnp.pad(w.reshape(FOX_H, FOX_D, D), ((0, 0), (0, HP - FOX_D), (0, 0))).reshape(FOX_H * HP, D)
    small = lambda w: jnp.pad(w, ((0, SMALL_W - w.shape[0]), (0, 0)))
    packed = dict(fq=heads(parts["fq"]), fk=heads(parts["fk"]), fv=heads(parts["fv"]), sf=small(parts["fl"]), sd=small(parts["ba"]),
                  dn=parts["dn"], dz=parts["dz"], ga=parts["ga"], gb=parts["gb"])
    return jnp.concatenate([packed[name] for name, _, _, _ in _seg_layout(D)], axis=0)


def _unpack_w_in(groups, d_model):
    D = groups[0].shape[1]
    FW = FOX_H * FOX_D
    segs = {}
    for grp, g in zip(GROUPS, groups):
        o = 0
        for name, wd, _, sg in _seg_layout(d_model):
            if sg == grp:
                segs[name] = g[o:o + wd]
                o += wd
    heads = lambda g: g.reshape(FOX_H, HP, D)[:, :FOX_D].reshape(FW, D)
    return jnp.concatenate([heads(segs["fq"]), heads(segs["fk"]), heads(segs["fv"]), segs["sf"][:FOX_H], segs["dn"],
                            segs["sd"][:2 * DN_H], segs["dz"], segs["ga"], segs["gb"]], axis=0)


def _local_step(x, tgt, meta, w1, w_in_t, fbias, cw, alog, dtb, wn, w2, wf, late_shards):
    T, D = x.shape
    pre = jnp.concatenate([jnp.zeros((N_PAD, D), f32), meta], axis=0)
    wp = _pack_w_in(w_in_t)
    bias_p, alog_p, dt_p = _pad_lanes(fbias), _pad_lanes(jnp.pad(alog, ((0, 0), (DN_H, 0)))), _pad_lanes(jnp.pad(dtb, ((0, 0), (DN_H, 0))))

    (h0, xn, fq, fk, sf, fv, dn, sd, dz, ga, gb), g_mix = _in_proj(x, pre, w1, wp, [late_shards[n] for n in LATE_MIX])
    qa, ka, va = _fox_prep(fq, fk, fv, sf, bias_p)
    op, qb, g_ffn = _fox_fwd(qa, ka, va, [late_shards[n] for n in LATE_FFN])
    qn, kn, vn, bg = _dn_prep(dn, sd, cw, alog_p, dt_p)
    (u_dn, w_dn, qd_dn, kd_dn, at_dn, gl_dn, x_dn), g_down = _dn_intra(qn, kn, vn, bg, [late_shards[n] for n in LATE_DOWN])
    full = {n: _from_slabs(n, s) for n, s in zip(LATE_MIX + LATE_FFN + LATE_DOWN, tuple(g_mix) + tuple(g_ffn) + tuple(g_down))}
    wbf, wbd, wo, wgt, wut, wd = (full[n] for n in ("w_branch_fox", "w_branch_dn", "w_out", "w_ffn_gate", "w_ffn_up", "w_ffn_down"))
    wbf_p = jnp.pad(wbf.reshape(FOX_H, FOX_D, D), ((0, 0), (0, HP - FOX_D), (0, 0))).reshape(FOX_H * HP, D)
    oraw, vnew, states = _dn_scan(u_dn, w_dn, qd_dn, kd_dn, at_dn, gl_dn)
    h1 = _mix_fwd(op, oraw, dz, ga, gb, h0, wn, wbf_p, wbd, wo)

    dh1, xn2, dgate, dup, act, dh2, acc_f = _ffn_fwd_bwd(h1, tgt, w2, wf, wgt, wut, wd)
    g_wg, g_wu, g_wd = _matmul_tn(dgate, xn2, "dw_ffn_gate"), _matmul_tn(dup, xn2, "dw_ffn_up"), _matmul_tn(act, dh2, "dw_ffn_down")

    dop, dor, d_mix, af, ad, dpf, dpd, yb, dmix, acc_m = _mix_bwd(dh1, op, oraw, dz, ga, gb, wn, wbf_p, wbd, wo)
    g_wbf = _matmul_tn(af, dpf, "dw_branch_fox").reshape(FOX_H, HP, D)[:, :FOX_D].reshape(FOX_H * FOX_D, D)
    g_wbd, g_wo = _matmul_tn(ad, dpd, "dw_branch_dn"), _matmul_tn(yb, dmix, "dw_out")

    dvnew, dstates = _dn_scan_bwd(dor, w_dn, qd_dn, kd_dn, at_dn, gl_dn)
    dqn, dkn, dvn, dbg = _dn_intra_bwd(qn, kn, vn, bg, x_dn, dor, vnew, dvnew, states, dstates)
    d_dn, acc_cw, acc_p = _dn_prep_bwd(dn, sd, cw, alog_p, dt_p, dqn, dkn, dvn, dbg)
    g_late = dict(w_branch_fox=g_wbf, w_branch_dn=g_wbd, w_out=g_wo, w_ffn_gate=g_wg, w_ffn_up=g_wu, w_ffn_down=g_wd)
    (dqa, dka, d_fv), recv = _fox_bwd(qb, ka, va, dop, [_to_slabs(n, g_late[n]) for n in LATE])
    d_fox, acc_b = _fox_prep_bwd(dqa, dka, sf, bias_p)

    dgroups = [d_fox, d_fv, d_dn, d_mix]
    g_wp = [_matmul_tn(dg, xn, "dw_in_" + grp) for grp, dg in zip(GROUPS, dgroups)]
    dh0, acc_1, (recv_w_in,) = _in_proj_bwd(dgroups, wp, h0, w1, dh1, [_to_slabs("w_in", _unpack_w_in(g_wp, D))])
    recv = dict(zip(LATE, recv), w_in=recv_w_in)

    small = dict(loss=acc_f[2, 0:1], mix_norm_w=acc_1[0], fox_forget_bias=acc_b[0, :FOX_H], dn_a_log=acc_p[0, DN_H:2 * DN_H],
                 dn_dt_bias=acc_p[1, DN_H:2 * DN_H], dn_out_norm_w=acc_m[0], ffn_norm_w=acc_f[0], final_norm_w=acc_f[1],
                 meta_tokens=dh0[N_PAD:PREFIX].reshape(-1), dn_conv_w=acc_cw[:CONV_K].reshape(-1))
    return dh0[PREFIX:], small, recv


def _mesh_pos():
    x, y, c = lax.axis_index("x"), lax.axis_index("y"), lax.axis_index("c")
    return x, y, c, 4 * x + 2 * y + c


def _peer(x, y, c, m):
    flip = lambda v, on: 1 - v if on else v
    px, py, pc = flip(x, m & 4), flip(y, m & 2), flip(c, m & 1)
    return (px, py, pc), 4 * px + 2 * py + pc


def _exchange_sems(n):
    return [pltpu.SemaphoreType.DMA((n, N_DEV - 1)), pltpu.SemaphoreType.DMA((n, N_DEV - 1)), pltpu.SemaphoreType.DMA((n,))]


def _exchange_part(ins, outs, send_sems, recv_sems, loc_sems, gather, m, receive):
    x, y, c, me = _mesh_pos()
    src = lambda a, pid: ins[a] if gather else ins[a].at[pid]
    if m == 0:
        return [pltpu.make_async_copy(src(a, me), outs[a].at[me], loc_sems.at[a]) for a in range(len(ins))]
    peer, pid = _peer(x, y, c, m)
    return [pltpu.make_async_remote_copy(src_ref=src(a, pid), dst_ref=outs[a].at[pid if receive else me], send_sem=send_sems.at[a, m - 1],
                                         recv_sem=recv_sems.at[a, m - 1], device_id=peer, device_id_type=MESH) for a in range(len(ins))]


def _exchange_start(*refs, gather, when):
    @pl.when(when)
    def _():
        for m in range(N_DEV):
            for cp in _exchange_part(*refs, gather, m, receive=False):
                cp.start()


def _exchange_wait(*refs, gather, when):
    @pl.when(when)
    def _():
        for m in range(1, N_DEV):
            for cp in _exchange_part(*refs, gather, m, receive=True):
                cp.wait_recv()
        for m in list(range(1, N_DEV)) + [0]:
            for cp in _exchange_part(*refs, gather, m, receive=False):
                cp.wait() if m == 0 else cp.wait_send()


def _gather_two_level(arrays, name):
    n = len(arrays)

    def body(*refs):
        ins, outs, (send_sems, recv_sems, loc_sems) = refs[:n], refs[n:2 * n], refs[2 * n:]
        x, y, c, me = _mesh_pos()
        sib = (x, y, 1 - c)
        chips = [(1 - x, y), (x, 1 - y), (1 - x, 1 - y)]
        dev_id = lambda px, py, pc: 4 * px + 2 * py + pc

        def copy(a, k, block, to, own=False):
            return pltpu.make_async_remote_copy(src_ref=ins[a] if own else outs[a].at[block], dst_ref=outs[a].at[block],
                                                send_sem=send_sems.at[a, k], recv_sem=recv_sems.at[a, k], device_id=to, device_id_type=MESH)

        local = [pltpu.make_async_copy(ins[a], outs[a].at[me], loc_sems.at[a]) for a in range(n)]
        first = [copy(a, 0, me, sib, own=True) for a in range(n)]
        first += [copy(a, 1 + j, me, (*chip, c), own=True) for j, chip in enumerate(chips) for a in range(n)]
        for cp in local + first:
            cp.start()
        passed = []
        for j, chip in enumerate(chips):
            for a in range(n):
                copy(a, 1 + j, dev_id(*chip, c), sib).wait_recv()
                cp = copy(a, 4 + j, dev_id(*chip, c), sib)
                cp.start()
                passed.append(cp)
        for a in range(n):
            copy(a, 0, dev_id(x, y, 1 - c), sib).wait_recv()
        for j, chip in enumerate(chips):
            for a in range(n):
                copy(a, 4 + j, dev_id(*chip, 1 - c), sib).wait_recv()
        for cp in first + passed:
            cp.wait_send()
        for cp in local:
            cp.wait()

    anyspec = pl.BlockSpec(memory_space=pl.ANY)
    return pl.pallas_call(
        body, name=name, in_specs=[anyspec] * n, out_specs=[anyspec] * n,
        out_shape=[SDS((N_DEV,) + a.shape, a.dtype) for a in arrays],
        scratch_shapes=_exchange_sems(n))(*arrays)


def _all_reduce_small(v):
    R = v.shape[0]

    def body(v_ref, o_ref, gath, send_sems, recv_sems):
        x, y, c, me = _mesh_pos()
        gath[me] = v_ref[...]
        sends = []
        for m in range(1, N_DEV):
            peer, _ = _peer(x, y, c, m)
            cp = pltpu.make_async_remote_copy(src_ref=v_ref, dst_ref=gath.at[me], send_sem=send_sems.at[m - 1],
                                              recv_sem=recv_sems.at[m - 1], device_id=peer, device_id_type=MESH)
            cp.start()
            sends.append(cp)
        for m in range(1, N_DEV):
            peer, pid = _peer(x, y, c, m)
            pltpu.make_async_remote_copy(src_ref=v_ref, dst_ref=gath.at[pid], send_sem=send_sems.at[m - 1],
                                         recv_sem=recv_sems.at[m - 1], device_id=peer, device_id_type=MESH).wait_recv()
        for cp in sends:
            cp.wait_send()
        tot = gath[0]
        for d in range(1, N_DEV):
            tot = tot + gath[d]
        o_ref[...] = tot

    vm = pl.BlockSpec(memory_space=pltpu.VMEM)
    return pl.pallas_call(
        body, name="all_reduce_small", in_specs=[vm], out_specs=vm, out_shape=SDS((R, HP), f32),
        scratch_shapes=[pltpu.VMEM((N_DEV, R, HP), f32), pltpu.SemaphoreType.DMA((N_DEV - 1,)), pltpu.SemaphoreType.DMA((N_DEV - 1,))],
        )(v)


def _adamw_math(w, g, m, v):
    m = ADAM_B1 * m + (1.0 - ADAM_B1) * g
    v = ADAM_B2 * v + (1.0 - ADAM_B2) * (g * g)
    m_hat = m / (1.0 - ADAM_B1 ** ADAM_STEP)
    v_hat = v / (1.0 - ADAM_B2 ** ADAM_STEP)
    return -ADAM_LR * (m_hat / (jnp.sqrt(v_hat) + ADAM_EPS) + ADAM_WD * w), m, v


def _adamw(g, w, m, v, name):
    R, Cc = w.shape[-2:]
    if R <= ADAMW_WHOLE_ROWS or R % HP == 0:
        TR, TC = (R if R <= ADAMW_WHOLE_ROWS else _pick(R, ADAMW_TILES)), Cc
    else:
        TR, TC = R, _pick(Cc, ADAMW_TILES)
    slabs = g.ndim == 3
    lead = w.ndim - 2

    def body(g_ref, w_ref, m_ref, v_ref, go_ref, d_ref, mo_ref, vo_ref):
        if slabs:
            gs = g_ref[0].astype(f32)
            for k in range(1, N_DEV):
                gs = gs + g_ref[k].astype(f32)
        else:
            gs = g_ref[...]
        at = 0 if lead else Ellipsis
        d, mn, vn = _adamw_math(w_ref[at], gs, m_ref[at], v_ref[at])
        go_ref[at], d_ref[at], mo_ref[at], vo_ref[at] = gs, d, mn, vn

    grid = (R // TR, Cc // TC)
    blk = pl.BlockSpec((1,) * lead + (TR, TC), lambda i, j: (0,) * lead + (i, j))
    gblk = pl.BlockSpec((N_DEV, TR, TC), lambda i, j: (0, i, j)) if slabs else pl.BlockSpec((TR, TC), lambda i, j: (i, j))
    return pl.pallas_call(
        body, name=name, grid=grid, in_specs=[gblk, blk, blk, blk], out_specs=[blk] * 4,
        out_shape=[SDS(w.shape, f32)] * 4, compiler_params=_cp(2))(g, w, m, v)


def _adamw_small(gs, ws, ms, vs):
    k = len(ws)

    def body(*refs):
        ins, outs = refs[:4 * k], refs[4 * k:]
        for t in range(k):
            g, w, m, v = (ins[j * k + t][...] for j in range(4))
            outs[t][...], outs[k + t][...], outs[2 * k + t][...] = _adamw_math(w, g, m, v)

    whole = lambda a: pl.BlockSpec(a.shape, lambda i, nd=a.ndim: (0,) * nd)
    res = pl.pallas_call(
        body, name="adamw_small", grid=(1,), in_specs=[whole(a) for a in (*gs, *ws, *ms, *vs)],
        out_specs=[whole(a) for a in ws] * 3, out_shape=[SDS(a.shape, f32) for a in ws] * 3,
        compiler_params=_cp())(*gs, *ws, *ms, *vs)
    return [[res[j * k + t] for j in range(3)] for t in range(k)]


WEIGHTS = ("meta_tokens", "mix_norm_w", "w_in", "fox_forget_bias", "dn_conv_w", "dn_a_log", "dn_dt_bias", "dn_out_norm_w",
           "w_branch_fox", "w_branch_dn", "w_out", "ffn_norm_w", "w_ffn_gate", "w_ffn_up", "w_ffn_down", "final_norm_w")
COL_SHARDED = ("w_in", "w_branch_fox", "w_branch_dn", "w_ffn_gate", "w_ffn_up")
ROW_SHARDED = ("w_out", "w_ffn_down")
BIG = COL_SHARDED + ROW_SHARDED
LATE = tuple(n for n in BIG if n != "w_in")
LATE_MIX = ("w_branch_fox", "w_branch_dn", "w_out")
LATE_FFN = ("w_ffn_gate", "w_ffn_up")
LATE_DOWN = ("w_ffn_down",)
SMALL = tuple(n for n in WEIGHTS if n not in BIG)
TRANSPOSED = ("w_in", "w_ffn_gate", "w_ffn_up")


def _to_slabs(name, g):
    r, c = g.shape
    if name in COL_SHARDED and name not in TRANSPOSED:
        return _b(g.reshape(r, N_DEV, c // N_DEV).transpose(1, 0, 2))
    return _b(g.reshape(N_DEV, r // N_DEV, c))


def _from_slabs(name, s):
    n, r, c = s.shape
    if name in COL_SHARDED and name not in TRANSPOSED:
        return s.transpose(1, 0, 2).reshape(r, n * c)
    return s.reshape(n * r, c)


def kernel(x, meta_tokens, mix_norm_w, w_in, fox_forget_bias, dn_conv_w, dn_a_log, dn_dt_bias, dn_out_norm_w, w_branch_fox, w_branch_dn, w_out, ffn_norm_w, w_ffn_gate, w_ffn_up, w_ffn_down, final_norm_w, loss_target, m_meta_tokens, m_mix_norm_w, m_w_in, m_fox_forget_bias, m_dn_conv_w, m_dn_a_log, m_dn_dt_bias, m_dn_out_norm_w, m_w_branch_fox, m_w_branch_dn, m_w_out, m_ffn_norm_w, m_w_ffn_gate, m_w_ffn_up, m_w_ffn_down, m_final_norm_w, v_meta_tokens, v_mix_norm_w, v_w_in, v_fox_forget_bias, v_dn_conv_w, v_dn_a_log, v_dn_dt_bias, v_dn_out_norm_w, v_w_branch_fox, v_w_branch_dn, v_w_out, v_ffn_norm_w, v_w_ffn_gate, v_w_ffn_up, v_w_ffn_down, v_final_norm_w):
    w = dict(meta_tokens=meta_tokens, mix_norm_w=mix_norm_w, w_in=w_in, fox_forget_bias=fox_forget_bias, dn_conv_w=dn_conv_w, dn_a_log=dn_a_log, dn_dt_bias=dn_dt_bias, dn_out_norm_w=dn_out_norm_w, w_branch_fox=w_branch_fox, w_branch_dn=w_branch_dn, w_out=w_out, ffn_norm_w=ffn_norm_w, w_ffn_gate=w_ffn_gate, w_ffn_up=w_ffn_up, w_ffn_down=w_ffn_down, final_norm_w=final_norm_w)
    mom = dict(meta_tokens=m_meta_tokens, mix_norm_w=m_mix_norm_w, w_in=m_w_in, fox_forget_bias=m_fox_forget_bias, dn_conv_w=m_dn_conv_w, dn_a_log=m_dn_a_log, dn_dt_bias=m_dn_dt_bias, dn_out_norm_w=m_dn_out_norm_w, w_branch_fox=m_w_branch_fox, w_branch_dn=m_w_branch_dn, w_out=m_w_out, ffn_norm_w=m_ffn_norm_w, w_ffn_gate=m_w_ffn_gate, w_ffn_up=m_w_ffn_up, w_ffn_down=m_w_ffn_down, final_norm_w=m_final_norm_w)
    var = dict(meta_tokens=v_meta_tokens, mix_norm_w=v_mix_norm_w, w_in=v_w_in, fox_forget_bias=v_fox_forget_bias, dn_conv_w=v_dn_conv_w, dn_a_log=v_dn_a_log, dn_dt_bias=v_dn_dt_bias, dn_out_norm_w=v_dn_out_norm_w, w_branch_fox=v_w_branch_fox, w_branch_dn=v_w_branch_dn, w_out=v_w_out, ffn_norm_w=v_ffn_norm_w, w_ffn_gate=v_w_ffn_gate, w_ffn_up=v_w_ffn_up, w_ffn_down=v_w_ffn_down, final_norm_w=v_final_norm_w)
    two_d = lambda a: a.reshape(a.shape[-2:]) if a.ndim >= 2 else a.reshape(1, -1)
    me = 4 * lax.axis_index("x") + 2 * lax.axis_index("y") + lax.axis_index("c")
    for d in (w, mom, var):
        for n in TRANSPOSED:
            d[n] = jnp.swapaxes(d[n], -1, -2)

    g_in, g_meta, g_cw = _gather_two_level([_b(two_d(w["w_in"])), two_d(w["meta_tokens"]), two_d(w["dn_conv_w"])], "all_gather_early")
    meta = g_meta.transpose(1, 0, 2).reshape(N_META, -1)
    cw = g_cw.transpose(1, 0, 2).reshape(CONV_K, -1)

    gx, g_small, recv = _local_step(
        x[0], loss_target[0], meta, two_d(w["mix_norm_w"]), _from_slabs("w_in", g_in), two_d(w["fox_forget_bias"]), cw, two_d(w["dn_a_log"]),
        two_d(w["dn_dt_bias"]), two_d(w["dn_out_norm_w"]), two_d(w["ffn_norm_w"]), two_d(w["final_norm_w"]),
        {n: _b(two_d(w[n])) for n in LATE})

    order = ("loss",) + SMALL
    flat = jnp.concatenate([g_small[n].reshape(-1) for n in order])
    rows = -(-flat.shape[0] // (8 * HP)) * 8
    tot = _all_reduce_small(jnp.pad(flat, (0, rows * HP - flat.shape[0])).reshape(rows, HP)).reshape(-1)
    summed, o = {}, 0
    for n in order:
        k = g_small[n].shape[0]
        summed[n] = tot[o:o + k]
        o += k
    loss = summed["loss"][0]
    d_model = x.shape[-1]
    mcols, ccols = d_model // N_DEV, dn_conv_w.shape[-1]
    summed["meta_tokens"] = lax.dynamic_slice(summed["meta_tokens"].reshape(N_META, d_model), (0, me * mcols), (N_META, mcols)).reshape(-1)
    summed["dn_conv_w"] = lax.dynamic_slice(summed["dn_conv_w"].reshape(CONV_K, ccols * N_DEV), (0, me * ccols), (CONV_K, ccols)).reshape(-1)

    res = {}
    for n in BIG:
        res[n] = _adamw(recv[n], w[n], mom[n], var[n], "adamw_" + n)
        if n in TRANSPOSED:
            res[n] = [jnp.swapaxes(r, -1, -2) for r in res[n]]
    gs = [summed[n].reshape(two_d(w[n]).shape) for n in SMALL]
    sres = _adamw_small(gs, *[[two_d(d[n]) for n in SMALL] for d in (w, mom, var)])
    for n, g, r in zip(SMALL, gs, sres):
        res[n] = [a.reshape(w[n].shape) for a in (g, *r)]
    return (loss, gx[None], *[res[n][0] for n in WEIGHTS], *[res[n][1] for n in WEIGHTS], *[res[n][2] for n in WEIGHTS], *[res[n][3] for n in WEIGHTS])
```
